```python
import jax, jax.numpy as jnp
from jax import lax
import numpy as np

D_MODEL = 1024
BATCH = 8
SEQ = 4096
DEPTH = 1

HG_HEADS = 8
HG_KEY_DIM = 128
HG_VAL_DIM = D_MODEL // HG_HEADS
HG_KEY_WIDTH = HG_HEADS * HG_KEY_DIM
HG_VAL_WIDTH = HG_HEADS * HG_VAL_DIM
CHUNK = 32
CONV_WIDTH = D_MODEL
CONV_KERNEL = 31
D_FF = 4 * D_MODEL
EPS = 1e-6
SPLITS = (HG_KEY_WIDTH, HG_KEY_WIDTH, HG_VAL_WIDTH, HG_VAL_WIDTH,
          CONV_WIDTH, CONV_WIDTH, D_MODEL, D_MODEL)
IN_COLS = sum(SPLITS)

kernel_name = "hgrn2_conformer_gated_hybrid_block"


def rmsnorm(x, g):
    xf = x.astype(jnp.float32)
    y = xf * lax.rsqrt(jnp.mean(xf * xf, axis=-1, keepdims=True) + EPS)
    return (y * g.astype(jnp.float32)).astype(x.dtype)


def layernorm(x, g, b):
    xf = x.astype(jnp.float32)
    mu = jnp.mean(xf, axis=-1, keepdims=True)
    var = jnp.mean(jnp.square(xf - mu), axis=-1, keepdims=True)
    y = (xf - mu) * lax.rsqrt(var + EPS)
    return (y * g.astype(jnp.float32) + b.astype(jnp.float32)).astype(x.dtype)


def hgrn2_mixer(q_raw, f_raw, i_raw, og_raw, lb, norm_g):
    B, S, _ = q_raw.shape
    dt = q_raw.dtype
    n_chunks = S // CHUNK
    q = jax.nn.silu(q_raw.astype(jnp.float32))
    lbf = lb.astype(jnp.float32)
    f = lbf + (1.0 - lbf) * jax.nn.sigmoid(f_raw.astype(jnp.float32))
    k = 1.0 - f
    logf = jnp.log(f)
    v = i_raw.astype(jnp.float32)

    def to_chunks(t, d):
        return t.reshape(B, n_chunks, CHUNK, HG_HEADS, d).transpose(1, 0, 3, 2, 4)

    qc, kc, gc = to_chunks(q, HG_KEY_DIM), to_chunks(k, HG_KEY_DIM), to_chunks(logf, HG_KEY_DIM)
    vc = to_chunks(v, HG_VAL_DIM)
    causal = jnp.tril(jnp.ones((CHUNK, CHUNK), dtype=bool))

    def step(state, inp):
        qb, kb, vb, gb = inp
        b = jnp.cumsum(gb, axis=2)
        inter = jnp.einsum('bhik,bhkv->bhiv', qb * jnp.exp(b), state)
        diff = b[:, :, :, None, :] - b[:, :, None, :, :]
        decay = jnp.exp(jnp.where(causal[:, :, None], diff, -jnp.inf))
        scores = jnp.einsum('bhik,bhjk,bhijk->bhij', qb, kb, decay)
        intra = jnp.einsum('bhij,bhjv->bhiv', scores, vb)
        b_last = b[:, :, -1:, :]
        new_state = (jnp.exp(b_last[:, :, 0, :])[..., None] * state
                     + jnp.einsum('bhjk,bhjv->bhkv', kb * jnp.exp(b_last - b), vb))
        return new_state, inter + intra

    s0 = jnp.zeros((B, HG_HEADS, HG_KEY_DIM, HG_VAL_DIM), jnp.float32)
    _, o = lax.scan(step, s0, (qc, kc, vc, gc))
    o = o.transpose(1, 0, 3, 2, 4).reshape(B, S, HG_HEADS, HG_VAL_DIM)
    o = rmsnorm(o, norm_g.reshape(HG_HEADS, HG_VAL_DIM))
    o = o.reshape(B, S, HG_VAL_WIDTH) * jax.nn.silu(og_raw.astype(jnp.float32))
    return o.astype(dt)


def conformer_conv(val, gate, dw, db, ln_g, ln_b):
    u = val * jax.nn.sigmoid(gate)
    u = lax.conv_general_dilated(
        u, dw[:, None, :].astype(u.dtype), window_strides=(1,),
        padding=[(CONV_KERNEL - 1, 0)], dimension_numbers=('NWC', 'WIO', 'NWC'),
        feature_group_count=CONV_WIDTH) + db
    return jax.nn.silu(layernorm(u, ln_g, ln_b))


def _fwd_setup_inputs(seed: int = 0) -> dict:
    key = jax.random.key(seed)
    ks = jax.random.split(key, 24)
    nrm = lambda k, shape, s: jax.random.normal(k, shape, jnp.float32) * s
    gain = lambda k, shape: 1.0 + nrm(k, shape, 0.05)
    return {
        "x": nrm(ks[0], (BATCH, SEQ, D_MODEL), 1.0),
        "c": nrm(ks[1], (BATCH, D_MODEL), 1.0),
        "w_ada": nrm(ks[2], (DEPTH, D_MODEL, 6 * D_MODEL), 0.2 * D_MODEL ** -0.5),
        "b_ada": nrm(ks[3], (DEPTH, 6 * D_MODEL), 0.02),
        "pre_norm_tm": gain(ks[4], (DEPTH, D_MODEL)),
        "post_norm_tm": gain(ks[5], (DEPTH, D_MODEL)),
        "pre_norm_cm": gain(ks[6], (DEPTH, D_MODEL)),
        "post_norm_cm": gain(ks[7], (DEPTH, D_MODEL)),
        "w_in": nrm(ks[8], (DEPTH, D_MODEL, IN_COLS), D_MODEL ** -0.5),
        "b_in": nrm(ks[9], (DEPTH, IN_COLS), 0.02),
        "hg_lb_logits": nrm(ks[10], (DEPTH + 1, HG_KEY_WIDTH), 0.5),
        "hg_norm": gain(ks[11], (DEPTH, HG_VAL_WIDTH)),
        "conv_dw": nrm(ks[12], (DEPTH, CONV_KERNEL, CONV_WIDTH), CONV_KERNEL ** -0.5),
        "conv_db": nrm(ks[13], (DEPTH, CONV_WIDTH), 0.02),
        "conv_ln_g": gain(ks[14], (DEPTH, CONV_WIDTH)),
        "conv_ln_b": nrm(ks[15], (DEPTH, CONV_WIDTH), 0.02),
        "w_br_a": nrm(ks[16], (DEPTH, HG_VAL_WIDTH, D_MODEL), HG_VAL_WIDTH ** -0.5),
        "w_br_b": nrm(ks[17], (DEPTH, CONV_WIDTH, D_MODEL), CONV_WIDTH ** -0.5),
        "w_out": nrm(ks[18], (DEPTH, D_MODEL, D_MODEL), D_MODEL ** -0.5),
        "w_ff1": nrm(ks[19], (DEPTH, D_MODEL, D_FF), D_MODEL ** -0.5),
        "w_ff2": nrm(ks[20], (DEPTH, D_FF, D_MODEL), D_FF ** -0.5),
    }


def _fwd_reference(x, c, w_ada, b_ada, pre_norm_tm, post_norm_tm, pre_norm_cm, post_norm_cm,
              w_in, b_in, hg_lb_logits, hg_norm, conv_dw, conv_db, conv_ln_g, conv_ln_b,
              w_br_a, w_br_b, w_out, w_ff1, w_ff2):
    lb_all = jnp.cumsum(jax.nn.softmax(hg_lb_logits.astype(jnp.float32), axis=0), axis=0)
    c_act = jax.nn.silu(c)
    split_idx = list(np.cumsum(SPLITS)[:-1])
    for l in range(DEPTH):
        mod = c_act @ w_ada[l] + b_ada[l]
        sh1, sc1, g1, sh2, sc2, g2 = [m[:, None, :] for m in jnp.split(mod, 6, axis=-1)]

        h = rmsnorm(x, pre_norm_tm[l]) * (1.0 + sc1) + sh1
        p = h @ w_in[l] + b_in[l]
        q_r, f_r, i_r, og_r, cv_r, cg_r, ga_r, gb_r = jnp.split(p, split_idx, axis=-1)
        y_a = hgrn2_mixer(q_r, f_r, i_r, og_r, lb_all[l], hg_norm[l]) @ w_br_a[l]
        y_b = conformer_conv(cv_r, cg_r, conv_dw[l], conv_db[l],
                             conv_ln_g[l], conv_ln_b[l]) @ w_br_b[l]
        merged = jax.nn.sigmoid(ga_r) * y_a + jax.nn.sigmoid(gb_r) * y_b
        y = merged @ w_out[l]
        x = x + g1 * rmsnorm(y, post_norm_tm[l])

        h = rmsnorm(x, pre_norm_cm[l]) * (1.0 + sc2) + sh2
        y = jnp.square(jax.nn.relu(h @ w_ff1[l])) @ w_ff2[l]
        x = x + g2 * rmsnorm(y, post_norm_cm[l])
    return x


import jax as _jax
import jax.numpy as _jnp

TWIN_FORMAT = 'train_step'
FWD_PARAMS = ['x', 'c', 'w_ada', 'b_ada', 'pre_norm_tm', 'post_norm_tm', 'pre_norm_cm', 'post_norm_cm', 'w_in', 'b_in', 'hg_lb_logits', 'hg_norm', 'conv_dw', 'conv_db', 'conv_ln_g', 'conv_ln_b', 'w_br_a', 'w_br_b', 'w_out', 'w_ff1', 'w_ff2']
TWIN_WEIGHTS = ['w_ada', 'b_ada', 'pre_norm_tm', 'post_norm_tm', 'pre_norm_cm', 'post_norm_cm', 'w_in', 'b_in', 'hg_lb_logits', 'hg_norm', 'conv_dw', 'conv_db', 'conv_ln_g', 'conv_ln_b', 'w_br_a', 'w_br_b', 'w_out', 'w_ff1', 'w_ff2']
TWIN_DIFF_INPUT = 'x'
TWIN_INPUTS = ['x', 'c', 'w_ada', 'b_ada', 'pre_norm_tm', 'post_norm_tm', 'pre_norm_cm', 'post_norm_cm', 'w_in', 'b_in', 'hg_lb_logits', 'hg_norm', 'conv_dw', 'conv_db', 'conv_ln_g', 'conv_ln_b', 'w_br_a', 'w_br_b', 'w_out', 'w_ff1', 'w_ff2', 'loss_target', 'm_w_ada', 'm_b_ada', 'm_pre_norm_tm', 'm_post_norm_tm', 'm_pre_norm_cm', 'm_post_norm_cm', 'm_w_in', 'm_b_in', 'm_hg_lb_logits', 'm_hg_norm', 'm_conv_dw', 'm_conv_db', 'm_conv_ln_g', 'm_conv_ln_b', 'm_w_br_a', 'm_w_br_b', 'm_w_out', 'm_w_ff1', 'm_w_ff2', 'v_w_ada', 'v_b_ada', 'v_pre_norm_tm', 'v_post_norm_tm', 'v_pre_norm_cm', 'v_post_norm_cm', 'v_w_in', 'v_b_in', 'v_hg_lb_logits', 'v_hg_norm', 'v_conv_dw', 'v_conv_db', 'v_conv_ln_g', 'v_conv_ln_b', 'v_w_br_a', 'v_w_br_b', 'v_w_out', 'v_w_ff1', 'v_w_ff2']
TWIN_OUTPUTS = ['loss', 'grad_x', 'grad_w_ada', 'grad_b_ada', 'grad_pre_norm_tm', 'grad_post_norm_tm', 'grad_pre_norm_cm', 'grad_post_norm_cm', 'grad_w_in', 'grad_b_in', 'grad_hg_lb_logits', 'grad_hg_norm', 'grad_conv_dw', 'grad_conv_db', 'grad_conv_ln_g', 'grad_conv_ln_b', 'grad_w_br_a', 'grad_w_br_b', 'grad_w_out', 'grad_w_ff1', 'grad_w_ff2', 'delta_w_ada', 'delta_b_ada', 'delta_pre_norm_tm', 'delta_post_norm_tm', 'delta_pre_norm_cm', 'delta_post_norm_cm', 'delta_w_in', 'delta_b_in', 'delta_hg_lb_logits', 'delta_hg_norm', 'delta_conv_dw', 'delta_conv_db', 'delta_conv_ln_g', 'delta_conv_ln_b', 'delta_w_br_a', 'delta_w_br_b', 'delta_w_out', 'delta_w_ff1', 'delta_w_ff2', 'new_m_w_ada', 'new_m_b_ada', 'new_m_pre_norm_tm', 'new_m_post_norm_tm', 'new_m_pre_norm_cm', 'new_m_post_norm_cm', 'new_m_w_in', 'new_m_b_in', 'new_m_hg_lb_logits', 'new_m_hg_norm', 'new_m_conv_dw', 'new_m_conv_db', 'new_m_conv_ln_g', 'new_m_conv_ln_b', 'new_m_w_br_a', 'new_m_w_br_b', 'new_m_w_out', 'new_m_w_ff1', 'new_m_w_ff2', 'new_v_w_ada', 'new_v_b_ada', 'new_v_pre_norm_tm', 'new_v_post_norm_tm', 'new_v_pre_norm_cm', 'new_v_post_norm_cm', 'new_v_w_in', 'new_v_b_in', 'new_v_hg_lb_logits', 'new_v_hg_norm', 'new_v_conv_dw', 'new_v_conv_db', 'new_v_conv_ln_g', 'new_v_conv_ln_b', 'new_v_w_br_a', 'new_v_w_br_b', 'new_v_w_out', 'new_v_w_ff1', 'new_v_w_ff2']
TWIN_LEAF_KINDS = {'loss': 'loss', 'grad_x': 'grad_x', 'grad_w_ada': 'grad_w', 'grad_b_ada': 'grad_w', 'grad_pre_norm_tm': 'grad_w', 'grad_post_norm_tm': 'grad_w', 'grad_pre_norm_cm': 'grad_w', 'grad_post_norm_cm': 'grad_w', 'grad_w_in': 'grad_w', 'grad_b_in': 'grad_w', 'grad_hg_lb_logits': 'grad_w', 'grad_hg_norm': 'grad_w', 'grad_conv_dw': 'grad_w', 'grad_conv_db': 'grad_w', 'grad_conv_ln_g': 'grad_w', 'grad_conv_ln_b': 'grad_w', 'grad_w_br_a': 'grad_w', 'grad_w_br_b': 'grad_w', 'grad_w_out': 'grad_w', 'grad_w_ff1': 'grad_w', 'grad_w_ff2': 'grad_w', 'delta_w_ada': 'delta_w', 'delta_b_ada': 'delta_w', 'delta_pre_norm_tm': 'delta_w', 'delta_post_norm_tm': 'delta_w', 'delta_pre_norm_cm': 'delta_w', 'delta_post_norm_cm': 'delta_w', 'delta_w_in': 'delta_w', 'delta_b_in': 'delta_w', 'delta_hg_lb_logits': 'delta_w', 'delta_hg_norm': 'delta_w', 'delta_conv_dw': 'delta_w', 'delta_conv_db': 'delta_w', 'delta_conv_ln_g': 'delta_w', 'delta_conv_ln_b': 'delta_w', 'delta_w_br_a': 'delta_w', 'delta_w_br_b': 'delta_w', 'delta_w_out': 'delta_w', 'delta_w_ff1': 'delta_w', 'delta_w_ff2': 'delta_w', 'new_m_w_ada': 'new_m', 'new_m_b_ada': 'new_m', 'new_m_pre_norm_tm': 'new_m', 'new_m_post_norm_tm': 'new_m', 'new_m_pre_norm_cm': 'new_m', 'new_m_post_norm_cm': 'new_m', 'new_m_w_in': 'new_m', 'new_m_b_in': 'new_m', 'new_m_hg_lb_logits': 'new_m', 'new_m_hg_norm': 'new_m', 'new_m_conv_dw': 'new_m', 'new_m_conv_db': 'new_m', 'new_m_conv_ln_g': 'new_m', 'new_m_conv_ln_b': 'new_m', 'new_m_w_br_a': 'new_m', 'new_m_w_br_b': 'new_m', 'new_m_w_out': 'new_m', 'new_m_w_ff1': 'new_m', 'new_m_w_ff2': 'new_m', 'new_v_w_ada': 'new_v', 'new_v_b_ada': 'new_v', 'new_v_pre_norm_tm': 'new_v', 'new_v_post_norm_tm': 'new_v', 'new_v_pre_norm_cm': 'new_v', 'new_v_post_norm_cm': 'new_v', 'new_v_w_in': 'new_v', 'new_v_b_in': 'new_v', 'new_v_hg_lb_logits': 'new_v', 'new_v_hg_norm': 'new_v', 'new_v_conv_dw': 'new_v', 'new_v_conv_db': 'new_v', 'new_v_conv_ln_g': 'new_v', 'new_v_conv_ln_b': 'new_v', 'new_v_w_br_a': 'new_v', 'new_v_w_br_b': 'new_v', 'new_v_w_out': 'new_v', 'new_v_w_ff1': 'new_v', 'new_v_w_ff2': 'new_v'}


def _forward(args):
    return _fwd_reference(*[args[k] for k in FWD_PARAMS])


def _output_shape():
    out = _jax.eval_shape(lambda: _forward(_fwd_setup_inputs(0)))
    return out.shape, out.dtype

N_MICROBATCH = 1
ADAM_LR = 0.001
ADAM_B1 = 0.9
ADAM_B2 = 0.999
ADAM_EPS = 1e-08
ADAM_WD = 0.01
ADAM_STEP = 10
PER_EXAMPLE_BATCH_AXIS = {'x': 0, 'c': 0, 'loss_target': 0}
SHARED_INPUTS = []
_WEIGHT_DTYPES = {'w_ada': _jnp.float32, 'b_ada': _jnp.float32, 'pre_norm_tm': _jnp.float32, 'post_norm_tm': _jnp.float32, 'pre_norm_cm': _jnp.float32, 'post_norm_cm': _jnp.float32, 'w_in': _jnp.float32, 'b_in': _jnp.float32, 'hg_lb_logits': _jnp.float32, 'hg_norm': _jnp.float32, 'conv_dw': _jnp.float32, 'conv_db': _jnp.float32, 'conv_ln_g': _jnp.float32, 'conv_ln_b': _jnp.float32, 'w_br_a': _jnp.float32, 'w_br_b': _jnp.float32, 'w_out': _jnp.float32, 'w_ff1': _jnp.float32, 'w_ff2': _jnp.float32}
MOMENT_SCALE = {'w_ada': 6.545358e-01, 'b_ada': 1.323553e+00, 'pre_norm_tm': 4.328069e-02, 'post_norm_tm': 5.845616e-01, 'pre_norm_cm': 4.756551e-02, 'post_norm_cm': 6.249333e-01, 'w_in': 1.599090e-02, 'b_in': 4.245718e-02, 'hg_lb_logits': 1.999025e-03, 'hg_norm': 2.593444e-02, 'conv_dw': 2.472676e-02, 'conv_db': 9.719978e-02, 'conv_ln_g': 4.609863e-02, 'conv_ln_b': 5.843154e-02, 'w_br_a': 2.497719e-02, 'w_br_b': 2.899291e-02, 'w_out': 3.946426e-02, 'w_ff1': 2.332321e-02, 'w_ff2': 5.978815e-02}


def _to_microbatches(a, axis):
    t = _jnp.moveaxis(a, axis, 0)
    t = t.reshape((N_MICROBATCH, t.shape[0] // N_MICROBATCH) + t.shape[1:])
    return _jnp.moveaxis(t, 1, axis + 1)


def setup_inputs(seed: int = 0) -> dict:
    inp = _fwd_setup_inputs(seed)
    key = _jax.random.fold_in(_jax.random.key(seed), 7919)
    shape, _ = _output_shape()
    out = dict(inp)
    out["loss_target"] = _jax.random.normal(_jax.random.fold_in(key, 0), shape, _jnp.float32)
    for i, name in enumerate(TWIN_WEIGHTS):
        w = inp[name].astype(_jnp.float32)
        if MOMENT_SCALE is None:
            s = _jnp.sqrt(_jnp.mean(_jnp.square(w)) + 1e-30)
        else:
            s = MOMENT_SCALE[name]
        km, kv = _jax.random.split(_jax.random.fold_in(key, i + 1))
        out[name] = w
        out["m_" + name] = s * _jax.random.normal(km, w.shape, _jnp.float32)
        out["v_" + name] = (s * s) * _jax.random.uniform(kv, w.shape, _jnp.float32, 0.5, 1.5)
    if N_MICROBATCH > 1:
        for name, axis in PER_EXAMPLE_BATCH_AXIS.items():
            out[name] = _to_microbatches(out[name], axis)
    return {'x': out['x'], 'c': out['c'], 'w_ada': out['w_ada'], 'b_ada': out['b_ada'], 'pre_norm_tm': out['pre_norm_tm'], 'post_norm_tm': out['post_norm_tm'], 'pre_norm_cm': out['pre_norm_cm'], 'post_norm_cm': out['post_norm_cm'], 'w_in': out['w_in'], 'b_in': out['b_in'], 'hg_lb_logits': out['hg_lb_logits'], 'hg_norm': out['hg_norm'], 'conv_dw': out['conv_dw'], 'conv_db': out['conv_db'], 'conv_ln_g': out['conv_ln_g'], 'conv_ln_b': out['conv_ln_b'], 'w_br_a': out['w_br_a'], 'w_br_b': out['w_br_b'], 'w_out': out['w_out'], 'w_ff1': out['w_ff1'], 'w_ff2': out['w_ff2'], 'loss_target': out['loss_target'], 'm_w_ada': out['m_w_ada'], 'm_b_ada': out['m_b_ada'], 'm_pre_norm_tm': out['m_pre_norm_tm'], 'm_post_norm_tm': out['m_post_norm_tm'], 'm_pre_norm_cm': out['m_pre_norm_cm'], 'm_post_norm_cm': out['m_post_norm_cm'], 'm_w_in': out['m_w_in'], 'm_b_in': out['m_b_in'], 'm_hg_lb_logits': out['m_hg_lb_logits'], 'm_hg_norm': out['m_hg_norm'], 'm_conv_dw': out['m_conv_dw'], 'm_conv_db': out['m_conv_db'], 'm_conv_ln_g': out['m_conv_ln_g'], 'm_conv_ln_b': out['m_conv_ln_b'], 'm_w_br_a': out['m_w_br_a'], 'm_w_br_b': out['m_w_br_b'], 'm_w_out': out['m_w_out'], 'm_w_ff1': out['m_w_ff1'], 'm_w_ff2': out['m_w_ff2'], 'v_w_ada': out['v_w_ada'], 'v_b_ada': out['v_b_ada'], 'v_pre_norm_tm': out['v_pre_norm_tm'], 'v_post_norm_tm': out['v_post_norm_tm'], 'v_pre_norm_cm': out['v_pre_norm_cm'], 'v_post_norm_cm': out['v_post_norm_cm'], 'v_w_in': out['v_w_in'], 'v_b_in': out['v_b_in'], 'v_hg_lb_logits': out['v_hg_lb_logits'], 'v_hg_norm': out['v_hg_norm'], 'v_conv_dw': out['v_conv_dw'], 'v_conv_db': out['v_conv_db'], 'v_conv_ln_g': out['v_conv_ln_g'], 'v_conv_ln_b': out['v_conv_ln_b'], 'v_w_br_a': out['v_w_br_a'], 'v_w_br_b': out['v_w_br_b'], 'v_w_out': out['v_w_out'], 'v_w_ff1': out['v_w_ff1'], 'v_w_ff2': out['v_w_ff2']}


def _loss(weights, diff, rest, loss_target):
    with _jax.named_scope("forward"):
        args = {**rest, TWIN_DIFF_INPUT: diff, **{k: w.astype(_WEIGHT_DTYPES[k]) for k, w in weights.items()}}
        y = _forward(args)
    with _jax.named_scope("loss_head"):
        err = _jnp.square(y.astype(_jnp.float32) - loss_target)
        return 0.5 * _jnp.sum(_jnp.mean(err, axis=-1)) if err.ndim else 0.5 * err


def _adamw(w, g, m, v):
    m = ADAM_B1 * m + (1.0 - ADAM_B1) * g
    v = ADAM_B2 * v + (1.0 - ADAM_B2) * _jnp.square(g)
    m_hat = m / (1.0 - ADAM_B1 ** ADAM_STEP)
    v_hat = v / (1.0 - ADAM_B2 ** ADAM_STEP)
    delta = -ADAM_LR * (m_hat / (_jnp.sqrt(v_hat) + ADAM_EPS) + ADAM_WD * w)
    return delta, m, v


def reference(x, c, w_ada, b_ada, pre_norm_tm, post_norm_tm, pre_norm_cm, post_norm_cm, w_in, b_in, hg_lb_logits, hg_norm, conv_dw, conv_db, conv_ln_g, conv_ln_b, w_br_a, w_br_b, w_out, w_ff1, w_ff2, loss_target, m_w_ada, m_b_ada, m_pre_norm_tm, m_post_norm_tm, m_pre_norm_cm, m_post_norm_cm, m_w_in, m_b_in, m_hg_lb_logits, m_hg_norm, m_conv_dw, m_conv_db, m_conv_ln_g, m_conv_ln_b, m_w_br_a, m_w_br_b, m_w_out, m_w_ff1, m_w_ff2, v_w_ada, v_b_ada, v_pre_norm_tm, v_post_norm_tm, v_pre_norm_cm, v_post_norm_cm, v_w_in, v_b_in, v_hg_lb_logits, v_hg_norm, v_conv_dw, v_conv_db, v_conv_ln_g, v_conv_ln_b, v_w_br_a, v_w_br_b, v_w_out, v_w_ff1, v_w_ff2):
    given = dict(x=x, c=c, w_ada=w_ada, b_ada=b_ada, pre_norm_tm=pre_norm_tm, post_norm_tm=post_norm_tm, pre_norm_cm=pre_norm_cm, post_norm_cm=post_norm_cm, w_in=w_in, b_in=b_in, hg_lb_logits=hg_lb_logits, hg_norm=hg_norm, conv_dw=conv_dw, conv_db=conv_db, conv_ln_g=conv_ln_g, conv_ln_b=conv_ln_b, w_br_a=w_br_a, w_br_b=w_br_b, w_out=w_out, w_ff1=w_ff1, w_ff2=w_ff2, loss_target=loss_target, m_w_ada=m_w_ada, m_b_ada=m_b_ada, m_pre_norm_tm=m_pre_norm_tm, m_post_norm_tm=m_post_norm_tm, m_pre_norm_cm=m_pre_norm_cm, m_post_norm_cm=m_post_norm_cm, m_w_in=m_w_in, m_b_in=m_b_in, m_hg_lb_logits=m_hg_lb_logits, m_hg_norm=m_hg_norm, m_conv_dw=m_conv_dw, m_conv_db=m_conv_db, m_conv_ln_g=m_conv_ln_g, m_conv_ln_b=m_conv_ln_b, m_w_br_a=m_w_br_a, m_w_br_b=m_w_br_b, m_w_out=m_w_out, m_w_ff1=m_w_ff1, m_w_ff2=m_w_ff2, v_w_ada=v_w_ada, v_b_ada=v_b_ada, v_pre_norm_tm=v_pre_norm_tm, v_post_norm_tm=v_post_norm_tm, v_pre_norm_cm=v_pre_norm_cm, v_post_norm_cm=v_post_norm_cm, v_w_in=v_w_in, v_b_in=v_b_in, v_hg_lb_logits=v_hg_lb_logits, v_hg_norm=v_hg_norm, v_conv_dw=v_conv_dw, v_conv_db=v_conv_db, v_conv_ln_g=v_conv_ln_g, v_conv_ln_b=v_conv_ln_b, v_w_br_a=v_w_br_a, v_w_br_b=v_w_br_b, v_w_out=v_w_out, v_w_ff1=v_w_ff1, v_w_ff2=v_w_ff2)
    weights = {n: given[n] for n in TWIN_WEIGHTS}
    shared = {n: given[n] for n in SHARED_INPUTS}
    per_example = {n: given[n] for n in ['x', 'c']}
    grad_fn = _jax.value_and_grad(_loss, argnums=(0, 1))

    def one_microbatch(ex, loss_target):
        ex = dict(ex)
        diff = ex.pop(TWIN_DIFF_INPUT)
        return grad_fn(weights, diff, {**shared, **ex}, loss_target)

    if N_MICROBATCH == 1:
        loss, (grad_w, grad_x) = one_microbatch(per_example, given["loss_target"])
    else:
        def body(carry, xs):
            loss_sum, grad_sum = carry
            l_k, (gw_k, gx_k) = one_microbatch(xs[0], xs[1])
            with _jax.named_scope("update"):
                return (loss_sum + l_k, _jax.tree.map(_jnp.add, grad_sum, gw_k)), gx_k

        init = (_jnp.zeros((), _jnp.float32), _jax.tree.map(_jnp.zeros_like, weights))
        (loss, grad_w), grad_x = _jax.lax.scan(body, init, (per_example, given["loss_target"]))
    with _jax.named_scope("update"):
        delta_w, new_m, new_v = {}, {}, {}
        for n in TWIN_WEIGHTS:
            delta_w[n], new_m[n], new_v[n] = _adamw(weights[n], grad_w[n], given["m_" + n], given["v_" + n])
    return (loss, grad_x, *[grad_w[n] for n in TWIN_WEIGHTS], *[delta_w[n] for n in TWIN_WEIGHTS],
            *[new_m[n] for n in TWIN_WEIGHTS], *[new_v[n] for n in TWIN_WEIGHTS])
```

```python
import functools

import jax
import jax.numpy as jnp
from jax import lax
from jax.experimental import pallas as pl
from jax.experimental.pallas import tpu as pltpu

F32, BF16 = jnp.float32, jnp.bfloat16
SDS = jax.ShapeDtypeStruct
BS = pl.BlockSpec
MESH = pl.DeviceIdType.MESH
HI = lax.Precision.HIGHEST

D = 1024
D_FF = 4096
IN_COLS = 8192
HEADS, DK = 8, 128
CHUNK = 32
CONV_K = 31
HALO = 32
SUB = 32
EPS = 1e-6
N_CHIPS, N_DEV = 4, 8
TM = 256
TB = 256
VMEM_LIMIT = 56 * 1024 * 1024

R_IN, R_BR, R_FF, R_ADA = 2048, 256, 1024, 1536
PACK_W = R_IN + 3 * R_BR + 2 * R_FF + R_ADA
PACK_G = R_IN + 3 * R_BR + 2 * R_FF
SMALL_ROWS = 64

ADAM_LR, ADAM_B1, ADAM_B2, ADAM_EPS, ADAM_WD, ADAM_STEP = 0.001, 0.9, 0.999, 1e-08, 0.01, 10

NN = (((1,), (0,)), ((), ()))
NT = (((1,), (1,)), ((), ()))
TN = (((0,), (0,)), ((), ()))


def _mm(a, b, dims=NN, precision=None):
    return lax.dot_general(a, b, dims, preferred_element_type=F32, precision=precision)


def _sig(v):
    return jax.nn.sigmoid(v)


def _dsilu(v, s):
    return s * (1.0 + v * (1.0 - s))


def _params(*sem):
    return pltpu.CompilerParams(dimension_semantics=sem if sem else None, vmem_limit_bytes=VMEM_LIMIT)


def _rowsum(v):
    return jnp.sum(v, axis=0, keepdims=True)


def _mesh_pos():
    return lax.axis_index("x"), lax.axis_index("y"), lax.axis_index("c")


def _allgather_call(blk, name, in_vmem, with_sum):
    m_per, n = blk.shape

    def body(x_ref, out_ref, *rest):
        if with_sum:
            sum_ref, send_sems, recv_sems, local_sem = rest
        else:
            send_sems, recv_sems, local_sem = rest
        x, y, c = _mesh_pos()
        me, sibling = (x, y, c), (x, y, 1 - c)
        chips = [(1 - x, y), (x, 1 - y), (1 - x, 1 - y)]

        def rows(px, py, pc):
            return out_ref.at[pl.ds((4 * px + 2 * py + pc) * m_per, m_per), :]

        def copy(k, block, to, src=None):
            return pltpu.make_async_remote_copy(
                src_ref=rows(*block) if src is None else src, dst_ref=rows(*block),
                send_sem=send_sems.at[k], recv_sem=recv_sems.at[k], device_id=to, device_id_type=MESH)

        mine = pltpu.make_async_copy(x_ref, rows(*me), local_sem)
        mine.start()
        first = [copy(0, me, sibling, src=x_ref)]
        first += [copy(1 + j, me, (*chip, c), src=x_ref) for j, chip in enumerate(chips)]
        for cp in first:
            cp.start()
        passed = [copy(4 + j, (*chip, c), sibling) for j, chip in enumerate(chips)]
        for j, chip in enumerate(chips):
            copy(1 + j, (*chip, c), me).wait_recv()
            passed[j].start()
        copy(0, sibling, me).wait_recv()
        for j, chip in enumerate(chips):
            copy(4 + j, (*chip, 1 - c), me).wait_recv()
        for cp in first + passed:
            cp.wait_send()
        mine.wait()
        if with_sum:
            acc = out_ref[0:m_per, :]
            for d in range(1, N_DEV):
                acc = acc + out_ref[d * m_per:(d + 1) * m_per, :]
            sum_ref[...] = acc

    space = pltpu.VMEM if in_vmem else pl.ANY
    out_shape = [SDS((N_DEV * m_per, n), blk.dtype)]
    out_specs = [BS(memory_space=space)]
    if with_sum:
        out_shape.append(SDS((m_per, n), blk.dtype))
        out_specs.append(BS(memory_space=pltpu.VMEM))
    return pl.pallas_call(
        body, name=name, out_shape=out_shape, in_specs=[BS(memory_space=space)], out_specs=out_specs,
        scratch_shapes=[pltpu.SemaphoreType.DMA((7,)), pltpu.SemaphoreType.DMA((7,)), pltpu.SemaphoreType.DMA],
        compiler_params=pltpu.CompilerParams(vmem_limit_bytes=VMEM_LIMIT),
    )(blk)


def _sibling_halves_call(g):
    _, _, h, n = g.shape

    def body(g_ref, out_ref, send_sems, recv_sems):
        x, y, c = _mesh_pos()
        cps = [pltpu.make_async_remote_copy(
            src_ref=g_ref.at[k, 1 - c], dst_ref=out_ref.at[k], send_sem=send_sems.at[k], recv_sem=recv_sems.at[k],
            device_id=(x, y, 1 - c), device_id_type=MESH) for k in range(N_CHIPS)]
        for cp in cps:
            cp.start()
        for cp in cps:
            cp.wait()

    return pl.pallas_call(
        body, name="rs_sibling_halves", out_shape=SDS((N_CHIPS, h, n), g.dtype),
        in_specs=[BS(memory_space=pl.ANY)], out_specs=BS(memory_space=pl.ANY),
        scratch_shapes=[pltpu.SemaphoreType.DMA((N_CHIPS,)), pltpu.SemaphoreType.DMA((N_CHIPS,))],
    )(g)


def _chip_exchange_call(p):
    _, h, n = p.shape

    def body(p_ref, out_ref, send_sems, recv_sems):
        x, y, c = _mesh_pos()
        chips = [(1 - x, y), (x, 1 - y), (1 - x, 1 - y)]
        cps = [pltpu.make_async_remote_copy(
            src_ref=p_ref.at[2 * cx + cy], dst_ref=out_ref.at[j], send_sem=send_sems.at[j], recv_sem=recv_sems.at[j],
            device_id=(cx, cy, c), device_id_type=MESH) for j, (cx, cy) in enumerate(chips)]
        for cp in cps:
            cp.start()
        for cp in cps:
            cp.wait()

    return pl.pallas_call(
        body, name="rs_chip_exchange", out_shape=SDS((3, h, n), p.dtype),
        in_specs=[BS(memory_space=pl.ANY)], out_specs=BS(memory_space=pl.ANY),
        scratch_shapes=[pltpu.SemaphoreType.DMA((3,)), pltpu.SemaphoreType.DMA((3,))],
    )(p)


def _sibling_join_call(r):
    h, n = r.shape

    def body(r_ref, out_ref, send_sem, recv_sem, local_sem):
        x, y, c = _mesh_pos()
        local = pltpu.make_async_copy(r_ref, out_ref.at[c], local_sem)
        local.start()
        send = pltpu.make_async_remote_copy(
            src_ref=r_ref, dst_ref=out_ref.at[c], send_sem=send_sem, recv_sem=recv_sem,
            device_id=(x, y, 1 - c), device_id_type=MESH)
        send.start()
        recv = pltpu.make_async_remote_copy(
            src_ref=r_ref, dst_ref=out_ref.at[1 - c], send_sem=send_sem, recv_sem=recv_sem,
            device_id=(x, y, 1 - c), device_id_type=MESH)
        send.wait_send()
        recv.wait_recv()
        local.wait()

    return pl.pallas_call(
        body, name="rs_sibling_join", out_shape=SDS((2, h, n), r.dtype),
        in_specs=[BS(memory_space=pl.ANY)], out_specs=BS(memory_space=pl.ANY),
        scratch_shapes=[pltpu.SemaphoreType.DMA, pltpu.SemaphoreType.DMA, pltpu.SemaphoreType.DMA],
    )(r)


def _add_halves_call(g, recv, c_idx):
    _, _, h, n = g.shape
    tr = h // 8

    def body(c_ref, g_ref, r_ref, o_ref):
        o_ref[...] = (g_ref[...].astype(F32) + r_ref[...].astype(F32)).astype(BF16)

    return pl.pallas_call(
        body, name="rs_add_halves", out_shape=SDS((N_CHIPS, h, n), BF16),
        grid_spec=pltpu.PrefetchScalarGridSpec(
            num_scalar_prefetch=1, grid=(N_CHIPS, 8),
            in_specs=[BS((None, None, tr, n), lambda k, r, c_ref: (k, c_ref[0], r, 0)),
                      BS((None, tr, n), lambda k, r, c_ref: (k, r, 0))],
            out_specs=BS((None, tr, n), lambda k, r, c_ref: (k, r, 0))),
        compiler_params=_params("arbitrary", "arbitrary"),
    )(c_idx, g, recv)


def _add_chips_call(p, recv, chip_idx):
    _, h, n = p.shape
    tr = h // 8

    def body(k_ref, p_ref, r_ref, o_ref):
        acc = p_ref[...].astype(F32)
        for j in range(3):
            acc = acc + r_ref[j].astype(F32)
        o_ref[...] = acc

    return pl.pallas_call(
        body, name="rs_add_chips", out_shape=SDS((h, n), F32),
        grid_spec=pltpu.PrefetchScalarGridSpec(
            num_scalar_prefetch=1, grid=(8,),
            in_specs=[BS((None, tr, n), lambda r, k_ref: (k_ref[0], r, 0)),
                      BS((3, tr, n), lambda r, k_ref: (0, r, 0))],
            out_specs=BS((tr, n), lambda r, k_ref: (r, 0))),
        compiler_params=_params("arbitrary"),
    )(chip_idx, p, recv)


def _mod_call(c, w_ada_g, b_ada):
    wc = w_ada_g.shape[2]

    def body(c_ref, w_ref, b_ref, mod_ref, cact_ref):
        cv = c_ref[...]
        ca = cv * _sig(cv)
        cact_ref[...] = ca
        cb = jnp.broadcast_to(ca, (8, D)).astype(BF16)
        for k in range(N_CHIPS):
            mod_ref[:, k * wc:(k + 1) * wc] = _mm(cb, w_ref[k]) + b_ref[:, k * wc:(k + 1) * wc]

    return pl.pallas_call(
        body, name="adaln_mod", out_shape=(SDS((8, 6 * D), F32), SDS((1, D), F32)),
        compiler_params=pltpu.CompilerParams(vmem_limit_bytes=VMEM_LIMIT),
    )(c, w_ada_g, b_ada)


def _fwd_in_call(x, mod, pre_tm, w_in_g, b_in):
    S = x.shape[0]
    wc = w_in_g.shape[2]

    def body(x_ref, mod_ref, g_ref, w_hbm, b_ref, p_ref, h_ref, w_vmem, sem):
        @pl.when(pl.program_id(0) == 0)
        def _():
            cp = pltpu.make_async_copy(w_hbm, w_vmem, sem)
            cp.start()
            cp.wait()

        xv = x_ref[...]
        r = lax.rsqrt(jnp.mean(xv * xv, axis=-1, keepdims=True) + EPS)
        h = xv * r * g_ref[...] * (1.0 + mod_ref[:, D:2 * D]) + mod_ref[:, 0:D]
        hb = h.astype(BF16)
        h_ref[...] = hb
        for k in range(N_CHIPS):
            p_ref[:, k * wc:(k + 1) * wc] = _mm(hb, w_vmem[k]) + b_ref[:, k * wc:(k + 1) * wc]

    return pl.pallas_call(
        body, name="fwd_in", grid=(S // TM,),
        out_shape=(SDS((S, IN_COLS), F32), SDS((S, D), BF16)),
        in_specs=[BS((TM, D), lambda i: (i, 0)), BS((1, 6 * D), lambda i: (0, 0)), BS((1, D), lambda i: (0, 0)),
                  BS(memory_space=pl.ANY), BS((1, IN_COLS), lambda i: (0, 0))],
        out_specs=(BS((TM, IN_COLS), lambda i: (i, 0)), BS((TM, D), lambda i: (i, 0))),
        scratch_shapes=[pltpu.VMEM(w_in_g.shape, BF16), pltpu.SemaphoreType.DMA],
        compiler_params=_params("arbitrary"),
    )(x, mod, pre_tm, w_in_g, b_in)


def _lower_bound(lg_ref):
    l0, l1 = lg_ref[0:1, :], lg_ref[1:2, :]
    mx = jnp.maximum(l0, l1)
    e0, e1 = jnp.exp(l0 - mx), jnp.exp(l1 - mx)
    return e0 / (e0 + e1)


def _tri_masks():
    ri = lax.broadcasted_iota(jnp.int32, (CHUNK, CHUNK), 0)
    ci = lax.broadcasted_iota(jnp.int32, (CHUNK, CHUNK), 1)
    return (ri >= ci).astype(F32), (ci >= ri).astype(F32)


def _hg_gates(q_r, f_r, lb, tril):
    sq = _sig(q_r)
    q = q_r * sq
    sf = _sig(f_r)
    f = lb + (1.0 - lb) * sf
    k = 1.0 - f
    g = jnp.log(f)
    b = _mm(tril, g, NN, HI)
    b_last = _rowsum(g)
    row = lax.broadcasted_iota(jnp.int32, g.shape, 0)
    ref = _rowsum(jnp.where(row < CHUNK // 2, g, 0.0))
    e = jnp.exp(b)
    eq = jnp.exp(jnp.minimum(b - ref, 80.0))
    ek = jnp.exp(jnp.minimum(ref - b, 80.0))
    dd = jnp.exp(b_last - b)
    return dict(sq=sq, q=q, sf=sf, f=f, k=k, e=e, eq=eq, ek=ek, dd=dd, elast=jnp.exp(b_last),
                qe=q * e, qt=q * eq, kt=k * ek, kd=k * dd)


def _hgrn_fwd_call(p, logits, gn):
    S = p.shape[0]
    ncb = TB // CHUNK

    def body(q_ref, f_ref, v_ref, og_ref, lg_ref, gn_ref, o_ref, oa_ref, st_ref, st_scr):
        @pl.when(pl.program_id(0) == 0)
        def _():
            st_scr[...] = jnp.zeros_like(st_scr)

        lb = _lower_bound(lg_ref)
        tril, _ = _tri_masks()

        def chunk(ci, carry):
            rows = pl.ds(pl.multiple_of(ci * CHUNK, CHUNK), CHUNK)
            st_ref[ci] = st_scr[...]
            t = _hg_gates(q_ref[rows, :], f_ref[rows, :], lb, tril)
            v = v_ref[rows, :]
            for h in range(HEADS):
                sl = slice(h * DK, (h + 1) * DK)
                stp = st_scr[:, sl]
                vb = v[:, sl].astype(BF16)
                inter = _mm(t["qe"][:, sl].astype(BF16), stp.astype(BF16), NT)
                a = _mm(t["qt"][:, sl].astype(BF16), t["kt"][:, sl].astype(BF16), NT) * tril
                o = inter + _mm(a.astype(BF16), vb)
                st_scr[:, sl] = stp * t["elast"][:, sl] + _mm(vb, t["kd"][:, sl].astype(BF16), TN)
                oh = o * lax.rsqrt(jnp.mean(o * o, axis=-1, keepdims=True) + EPS)
                og = og_ref[rows, sl]
                o_ref[rows, sl] = o
                oa_ref[rows, sl] = (oh * gn_ref[:, sl] * (og * _sig(og))).astype(BF16)
            return carry

        lax.fori_loop(0, ncb, chunk, 0)

    col = lambda j: BS((TB, D), lambda i, j=j: (i, j))
    return pl.pallas_call(
        body, name="hgrn_fwd", grid=(S // TB,),
        out_shape=(SDS((S, D), F32), SDS((S, D), BF16), SDS((S // CHUNK, DK, D), F32)),
        in_specs=[col(0), col(1), col(2), col(3), BS((2, D), lambda i: (0, 0)), BS((1, D), lambda i: (0, 0))],
        out_specs=(BS((TB, D), lambda i: (i, 0)), BS((TB, D), lambda i: (i, 0)),
                   BS((ncb, DK, D), lambda i: (i, 0, 0))),
        scratch_shapes=[pltpu.VMEM((DK, D), F32)],
        compiler_params=_params("arbitrary"),
    )(p, p, p, p, logits, gn)


def _layernorm_stats(uc):
    mu = jnp.mean(uc, axis=-1, keepdims=True)
    xc = uc - mu
    rs = lax.rsqrt(jnp.mean(xc * xc, axis=-1, keepdims=True) + EPS)
    return xc * rs, rs


def _conv_fwd_call(p, dw, db, ln_g, ln_b):
    S = p.shape[0]

    def body(cv_ref, cg_ref, dw_ref, db_ref, g_ref, b_ref, u_ref, uc_ref, cb_ref, uext):
        @pl.when(pl.program_id(0) == 0)
        def _():
            uext[0:HALO, :] = jnp.zeros((HALO, D), F32)

        u = cv_ref[...] * _sig(cg_ref[...])
        uext[HALO:HALO + TM, :] = u
        u_ref[...] = u
        for rb in range(TM // SUB):
            acc = jnp.broadcast_to(db_ref[...], (SUB, D))
            for j in range(CONV_K):
                s0 = HALO - (CONV_K - 1) + j + rb * SUB
                acc = acc + dw_ref[j:j + 1, :] * uext[s0:s0 + SUB, :]
            uc_ref[rb * SUB:(rb + 1) * SUB, :] = acc
            xh, _ = _layernorm_stats(acc)
            ln = xh * g_ref[...] + b_ref[...]
            cb_ref[rb * SUB:(rb + 1) * SUB, :] = (ln * _sig(ln)).astype(BF16)
        uext[0:HALO, :] = uext[TM:TM + HALO, :]

    vec = BS((1, D), lambda i: (0, 0))
    return pl.pallas_call(
        body, name="conv_fwd", grid=(S // TM,),
        out_shape=(SDS((S, D), F32), SDS((S, D), F32), SDS((S, D), BF16)),
        in_specs=[BS((TM, D), lambda i: (i, 4)), BS((TM, D), lambda i: (i, 5)),
                  BS((CONV_K, D), lambda i: (0, 0)), vec, vec, vec],
        out_specs=(BS((TM, D), lambda i: (i, 0)),) * 3,
        scratch_shapes=[pltpu.VMEM((HALO + TM, D), F32)],
        compiler_params=_params("arbitrary"),
    )(p, p, dw, db, ln_g, ln_b)


def _merge_fwd_call(oa, cb, p, x, mod, post_tm, pre_cm, w_a, w_b, w_o):
    S = x.shape[0]

    def body(oa_ref, cb_ref, ga_ref, gb_ref, x_ref, mod_ref, post_ref, pre_ref, wa_ref, wb_ref, wo_ref,
             ya_ref, yb_ref, mg_ref, y_ref, x2_ref, h2_ref):
        ya = _mm(oa_ref[...], wa_ref[...])
        yb = _mm(cb_ref[...], wb_ref[...])
        ya_ref[...] = ya
        yb_ref[...] = yb
        mg = (_sig(ga_ref[...]) * ya + _sig(gb_ref[...]) * yb).astype(BF16)
        mg_ref[...] = mg
        y = _mm(mg, wo_ref[...])
        y_ref[...] = y
        n = y * lax.rsqrt(jnp.mean(y * y, axis=-1, keepdims=True) + EPS) * post_ref[...]
        x2 = x_ref[...] + mod_ref[:, 2 * D:3 * D] * n
        x2_ref[...] = x2
        r2 = lax.rsqrt(jnp.mean(x2 * x2, axis=-1, keepdims=True) + EPS)
        h2 = x2 * r2 * pre_ref[...] * (1.0 + mod_ref[:, 4 * D:5 * D]) + mod_ref[:, 3 * D:4 * D]
        h2_ref[...] = h2.astype(BF16)

    tile = BS((TM, D), lambda i: (i, 0))
    vec = BS((1, D), lambda i: (0, 0))
    wsp = BS((D, D), lambda i: (0, 0))
    return pl.pallas_call(
        body, name="merge_fwd", grid=(S // TM,),
        out_shape=(SDS((S, D), F32), SDS((S, D), F32), SDS((S, D), BF16), SDS((S, D), F32), SDS((S, D), F32),
                   SDS((S, D), BF16)),
        in_specs=[tile, tile, BS((TM, D), lambda i: (i, 6)), BS((TM, D), lambda i: (i, 7)), tile,
                  BS((1, 6 * D), lambda i: (0, 0)), vec, vec, wsp, wsp, wsp],
        out_specs=(tile,) * 6,
        compiler_params=_params("arbitrary"),
    )(oa, cb, p, p, x, mod, post_tm, pre_cm, w_a, w_b, w_o)


def _ffn_call(h2, x2, target, mod, post_cm, pre_cm, w1_g, w2):
    S = x2.shape[0]

    def body(h2_ref, x2_ref, t_ref, mod_ref, post_ref, pre_ref, w1_hbm, w2_hbm,
             z_ref, da_ref, dy2_ref, dx2_ref, acc_ref, w1_v, w2_v, ra_scr, sems):
        @pl.when(pl.program_id(0) == 0)
        def _():
            c1 = pltpu.make_async_copy(w1_hbm, w1_v, sems.at[0])
            c2 = pltpu.make_async_copy(w2_hbm, w2_v, sems.at[1])
            c1.start()
            c2.start()
            c1.wait()
            c2.wait()
            acc_ref[...] = jnp.zeros_like(acc_ref)

        h2 = h2_ref[...]
        for k in range(N_CHIPS):
            ra = jnp.maximum(_mm(h2, w1_v[k]), 0.0)
            ra_scr[:, k * D:(k + 1) * D] = ra
            z_ref[:, k * D:(k + 1) * D] = (ra * ra).astype(BF16)
        y2 = _mm(z_ref[...], w2_v[...])
        ry = lax.rsqrt(jnp.mean(y2 * y2, axis=-1, keepdims=True) + EPS)
        yn = y2 * ry
        n = yn * post_ref[...]
        g2 = mod_ref[:, 5 * D:6 * D]
        x2 = x2_ref[...]
        err = x2 + g2 * n - t_ref[...]
        acc_ref[5:6, :] += _rowsum(err * err) * (0.5 / D)
        dout = err * (1.0 / D)
        acc_ref[0:1, :] += _rowsum(dout * n)
        dn = dout * g2
        acc_ref[1:2, :] += _rowsum(dn * yn)
        dyn = dn * post_ref[...]
        dy2 = (ry * (dyn - yn * jnp.mean(dyn * yn, axis=-1, keepdims=True))).astype(BF16)
        dy2_ref[...] = dy2
        dz = _mm(dy2, w2_v[...], NT)
        da_ref[...] = (dz * (2.0 * ra_scr[...])).astype(BF16)
        dh2 = jnp.zeros((TM, D), F32)
        for k in range(N_CHIPS):
            dh2 = dh2 + _mm(da_ref[:, k * D:(k + 1) * D], w1_v[k], NT)
        r2 = lax.rsqrt(jnp.mean(x2 * x2, axis=-1, keepdims=True) + EPS)
        xn = x2 * r2
        yv = xn * pre_ref[...]
        acc_ref[2:3, :] += _rowsum(dh2)
        acc_ref[3:4, :] += _rowsum(dh2 * yv)
        dyv = dh2 * (1.0 + mod_ref[:, 4 * D:5 * D])
        acc_ref[4:5, :] += _rowsum(dyv * xn)
        dxn = dyv * pre_ref[...]
        dx2_ref[...] = dout + r2 * (dxn - xn * jnp.mean(dxn * xn, axis=-1, keepdims=True))

    tile = BS((TM, D), lambda i: (i, 0))
    wide = BS((TM, D_FF), lambda i: (i, 0))
    vec = BS((1, D), lambda i: (0, 0))
    return pl.pallas_call(
        body, name="ffn_fwd_bwd", grid=(S // TM,),
        out_shape=(SDS((S, D_FF), BF16), SDS((S, D_FF), BF16), SDS((S, D), BF16), SDS((S, D), F32),
                   SDS((8, D), F32)),
        in_specs=[tile, tile, tile, BS((1, 6 * D), lambda i: (0, 0)), vec, vec,
                  BS(memory_space=pl.ANY), BS(memory_space=pl.ANY)],
        out_specs=(wide, wide, tile, tile, BS((8, D), lambda i: (0, 0))),
        scratch_shapes=[pltpu.VMEM(w1_g.shape, BF16), pltpu.VMEM(w2.shape, BF16), pltpu.VMEM((TM, D_FF), F32),
                        pltpu.SemaphoreType.DMA((2,))],
        compiler_params=_params("arbitrary"),
    )(h2, x2, target, mod, post_cm, pre_cm, w1_g, w2)


def _merge_bwd_call(dx2, y, ya, yb, p, mod, post_tm, w_a, w_b, w_o):
    S = y.shape[0]

    def body(dx2_ref, y_ref, ya_ref, yb_ref, ga_ref, gb_ref, mod_ref, post_ref, wa_ref, wb_ref, wo_ref,
             dy_ref, dya_ref, dyb_ref, doa_ref, dcb_ref, dpg_ref, acc_ref, bsum_ref):
        @pl.when(pl.program_id(0) == 0)
        def _():
            acc_ref[...] = jnp.zeros_like(acc_ref)
            bsum_ref[...] = jnp.zeros_like(bsum_ref)

        y = y_ref[...]
        ry = lax.rsqrt(jnp.mean(y * y, axis=-1, keepdims=True) + EPS)
        yn = y * ry
        dx2 = dx2_ref[...]
        acc_ref[0:1, :] += _rowsum(dx2 * (yn * post_ref[...]))
        dn = dx2 * mod_ref[:, 2 * D:3 * D]
        acc_ref[1:2, :] += _rowsum(dn * yn)
        dyn = dn * post_ref[...]
        dy = (ry * (dyn - yn * jnp.mean(dyn * yn, axis=-1, keepdims=True))).astype(BF16)
        dy_ref[...] = dy
        dmg = _mm(dy, wo_ref[...], NT)
        sa, sb = _sig(ga_ref[...]), _sig(gb_ref[...])
        dya = (dmg * sa).astype(BF16)
        dyb = (dmg * sb).astype(BF16)
        dya_ref[...] = dya
        dyb_ref[...] = dyb
        dga = dmg * ya_ref[...] * (sa * (1.0 - sa))
        dgb = dmg * yb_ref[...] * (sb * (1.0 - sb))
        dpg_ref[:, 0:D] = dga.astype(BF16)
        dpg_ref[:, D:2 * D] = dgb.astype(BF16)
        bsum_ref[:, 0:D] += _rowsum(dga)
        bsum_ref[:, D:2 * D] += _rowsum(dgb)
        doa_ref[...] = _mm(dya, wa_ref[...], NT)
        dcb_ref[...] = _mm(dyb, wb_ref[...], NT)

    tile = BS((TM, D), lambda i: (i, 0))
    vec = BS((1, D), lambda i: (0, 0))
    wsp = BS((D, D), lambda i: (0, 0))
    return pl.pallas_call(
        body, name="merge_bwd", grid=(S // TM,),
        out_shape=(SDS((S, D), BF16), SDS((S, D), BF16), SDS((S, D), BF16), SDS((S, D), F32), SDS((S, D), F32),
                   SDS((S, 2 * D), BF16), SDS((8, D), F32), SDS((1, 2 * D), F32)),
        in_specs=[tile, tile, tile, tile, BS((TM, D), lambda i: (i, 6)), BS((TM, D), lambda i: (i, 7)),
                  BS((1, 6 * D), lambda i: (0, 0)), vec, wsp, wsp, wsp],
        out_specs=(tile, tile, tile, tile, tile, BS((TM, 2 * D), lambda i: (i, 0)),
                   BS((8, D), lambda i: (0, 0)), BS((1, 2 * D), lambda i: (0, 0))),
        compiler_params=_params("arbitrary"),
    )(dx2, y, ya, yb, p, p, mod, post_tm, w_a, w_b, w_o)


def _hgrn_bwd_call(p, o, doa, st, logits, gn):
    S = p.shape[0]
    nb = S // TB
    ncb = TB // CHUNK

    def body(q_ref, f_ref, v_ref, og_ref, o_ref, doa_ref, st_ref, lg_ref, gn_ref,
             dp_ref, bsum_ref, dlg_ref, dgn_ref, dst_scr, dlb_scr, dqe_s, dqt_s, dkt_s, dkd_s, dv_s, dog_s, dble_s):
        i = pl.program_id(0)

        @pl.when(i == 0)
        def _():
            dst_scr[...] = jnp.zeros_like(dst_scr)
            dlb_scr[...] = jnp.zeros_like(dlb_scr)
            bsum_ref[...] = jnp.zeros_like(bsum_ref)
            dgn_ref[...] = jnp.zeros_like(dgn_ref)

        lb = _lower_bound(lg_ref)
        tril, triu = _tri_masks()

        def chunk(tt, carry):
            ci = ncb - 1 - tt
            rows = pl.ds(pl.multiple_of(ci * CHUNK, CHUNK), CHUNK)
            q_r, f_r = q_ref[rows, :], f_ref[rows, :]
            t = _hg_gates(q_r, f_r, lb, tril)
            v = v_ref[rows, :]
            for h in range(HEADS):
                sl = slice(h * DK, (h + 1) * DK)
                stp = st_ref[ci, :, sl]
                stb = stp.astype(BF16)
                qeb = t["qe"][:, sl].astype(BF16)
                qtb = t["qt"][:, sl].astype(BF16)
                ktb = t["kt"][:, sl].astype(BF16)
                kdb = t["kd"][:, sl].astype(BF16)
                vb = v[:, sl].astype(BF16)
                a = _mm(qtb, ktb, NT) * tril
                o_h = o_ref[rows, sl]
                rinv = lax.rsqrt(jnp.mean(o_h * o_h, axis=-1, keepdims=True) + EPS)
                oh = o_h * rinv
                og = og_ref[rows, sl]
                so = _sig(og)
                d_oa = doa_ref[rows, sl]
                don = d_oa * (og * so)
                dog_s[:, sl] = d_oa * (oh * gn_ref[:, sl]) * _dsilu(og, so)
                dgn_ref[:, sl] += _rowsum(don * oh)
                doh = don * gn_ref[:, sl]
                do = (rinv * (doh - oh * jnp.mean(doh * oh, axis=-1, keepdims=True))).astype(BF16)
                dqe_s[:, sl] = _mm(do, stb, NN)
                dstp = _mm(do, qeb, TN)
                dab = (_mm(do, vb, NT) * tril).astype(BF16)
                dqt_s[:, sl] = _mm(dab, ktb, NN)
                dkt_s[:, sl] = _mm(dab, qtb, TN)
                dstn = dst_scr[:, sl]
                dsb = dstn.astype(BF16)
                dkd_s[:, sl] = _mm(vb, dsb, NN)
                dv_s[:, sl] = _mm(a.astype(BF16), do, TN) + _mm(kdb, dsb, NT)
                el = t["elast"][:, sl]
                dst_scr[:, sl] = dstn * el + dstp
                dble_s[:, sl] = el * _rowsum(stp * dstn)
            dqe, dqt, dkt, dkd = dqe_s[...], dqt_s[...], dkt_s[...], dkd_s[...]
            dq = dqe * t["e"] + dqt * t["eq"]
            dk = dkt * t["ek"] + dkd * t["dd"]
            dkk = dkd * t["kd"]
            dbv = dqe * t["qe"] + dqt * t["qt"] - dkt * t["kt"] - dkk
            dg = _mm(triu, dbv, NN, HI) + (_rowsum(dkk) + dble_s[...])
            df = dg / t["f"] - dk
            sf = t["sf"]
            dlb_scr[...] += _rowsum(df * (1.0 - sf))
            dqr = dq * _dsilu(q_r, t["sq"])
            dfr = df * (1.0 - lb) * (sf * (1.0 - sf))
            dvv, dog = dv_s[...], dog_s[...]
            dp_ref[rows, 0:D] = dqr.astype(BF16)
            dp_ref[rows, D:2 * D] = dfr.astype(BF16)
            dp_ref[rows, 2 * D:3 * D] = dvv.astype(BF16)
            dp_ref[rows, 3 * D:4 * D] = dog.astype(BF16)
            bsum_ref[:, 0:D] += _rowsum(dqr)
            bsum_ref[:, D:2 * D] += _rowsum(dfr)
            bsum_ref[:, 2 * D:3 * D] += _rowsum(dvv)
            bsum_ref[:, 3 * D:4 * D] += _rowsum(dog)
            return carry

        lax.fori_loop(0, ncb, chunk, 0)

        dl = dlb_scr[...] * lb * (1.0 - lb)
        dlg_ref[0:1, :] = dl
        dlg_ref[1:2, :] = -dl

    col = lambda j: BS((TB, D), lambda i, j=j: (nb - 1 - i, j))
    rev = BS((TB, D), lambda i: (nb - 1 - i, 0))
    cd = pltpu.VMEM((CHUNK, D), F32)
    return pl.pallas_call(
        body, name="hgrn_bwd", grid=(nb,),
        out_shape=(SDS((S, 4 * D), BF16), SDS((1, 4 * D), F32), SDS((2, D), F32), SDS((1, D), F32)),
        in_specs=[col(0), col(1), col(2), col(3), rev, rev, BS((ncb, DK, D), lambda i: (nb - 1 - i, 0, 0)),
                  BS((2, D), lambda i: (0, 0)), BS((1, D), lambda i: (0, 0))],
        out_specs=(BS((TB, 4 * D), lambda i: (nb - 1 - i, 0)), BS((1, 4 * D), lambda i: (0, 0)),
                   BS((2, D), lambda i: (0, 0)), BS((1, D), lambda i: (0, 0))),
        scratch_shapes=[pltpu.VMEM((DK, D), F32), pltpu.VMEM((1, D), F32), cd, cd, cd, cd, cd, cd,
                        pltpu.VMEM((1, D), F32)],
        compiler_params=_params("arbitrary"),
    )(p, p, p, p, o, doa, st, logits, gn)


def _conv_bwd_call(dcb, uc, u, p, dw, ln_g, ln_b):
    S = uc.shape[0]
    nb = S // TM
    hb = TM // HALO

    def body(dcb_ref, uc_ref, u_ref, uh_ref, cv_ref, cg_ref, dw_ref, g_ref, b_ref,
             dp_ref, bsum_ref, ddw_ref, acc_ref, uext, dext):
        i = pl.program_id(0)

        @pl.when(i == 0)
        def _():
            dext[TM:TM + HALO, :] = jnp.zeros((HALO, D), F32)
            bsum_ref[...] = jnp.zeros_like(bsum_ref)
            ddw_ref[...] = jnp.zeros_like(ddw_ref)
            acc_ref[...] = jnp.zeros_like(acc_ref)

        first_tile = (nb - 1 - i) == 0
        uext[0:HALO, :] = jnp.where(first_tile, 0.0, uh_ref[...])
        uext[HALO:HALO + TM, :] = u_ref[...]

        for rb in range(TM // SUB):
            rs_ = slice(rb * SUB, (rb + 1) * SUB)
            xh, rs = _layernorm_stats(uc_ref[rs_, :])
            ln = xh * g_ref[...] + b_ref[...]
            dln = dcb_ref[rs_, :] * _dsilu(ln, _sig(ln))
            acc_ref[1:2, :] += _rowsum(dln * xh)
            acc_ref[2:3, :] += _rowsum(dln)
            dxh = dln * g_ref[...]
            duc = rs * (dxh - jnp.mean(dxh, axis=-1, keepdims=True)
                        - xh * jnp.mean(dxh * xh, axis=-1, keepdims=True))
            dext[rs_, :] = duc
            acc_ref[0:1, :] += _rowsum(duc)

        for j in range(CONV_K):
            part = jnp.zeros((SUB, D), F32)
            for rb in range(TM // SUB):
                s0 = HALO - (CONV_K - 1) + j + rb * SUB
                part = part + dext[rb * SUB:(rb + 1) * SUB, :] * uext[s0:s0 + SUB, :]
            ddw_ref[j:j + 1, :] += _rowsum(part)

        for rb in range(TM // SUB):
            rs_ = slice(rb * SUB, (rb + 1) * SUB)
            du = jnp.zeros((SUB, D), F32)
            for j in range(CONV_K):
                s0 = rb * SUB + (CONV_K - 1) - j
                du = du + dw_ref[j:j + 1, :] * dext[s0:s0 + SUB, :]
            cg = cg_ref[rs_, :]
            sg = _sig(cg)
            dcv = du * sg
            dcg = du * cv_ref[rs_, :] * (sg * (1.0 - sg))
            dp_ref[rs_, 0:D] = dcv.astype(BF16)
            dp_ref[rs_, D:2 * D] = dcg.astype(BF16)
            bsum_ref[:, 0:D] += _rowsum(dcv)
            bsum_ref[:, D:2 * D] += _rowsum(dcg)

        dext[TM:TM + HALO, :] = dext[0:HALO, :]

    rev = BS((TM, D), lambda i: (nb - 1 - i, 0))
    vec = BS((1, D), lambda i: (0, 0))
    return pl.pallas_call(
        body, name="conv_bwd", grid=(nb,),
        out_shape=(SDS((S, 2 * D), BF16), SDS((1, 2 * D), F32), SDS((32, D), F32), SDS((8, D), F32)),
        in_specs=[rev, rev, rev, BS((HALO, D), lambda i: (jnp.maximum((nb - 1 - i) * hb - 1, 0), 0)),
                  BS((TM, D), lambda i: (nb - 1 - i, 4)), BS((TM, D), lambda i: (nb - 1 - i, 5)),
                  BS((CONV_K, D), lambda i: (0, 0)), vec, vec],
        out_specs=(BS((TM, 2 * D), lambda i: (nb - 1 - i, 0)), BS((1, 2 * D), lambda i: (0, 0)),
                   BS((32, D), lambda i: (0, 0)), BS((8, D), lambda i: (0, 0))),
        scratch_shapes=[pltpu.VMEM((HALO + TM, D), F32), pltpu.VMEM((TM + HALO, D), F32)],
        compiler_params=_params("arbitrary"),
    )(dcb, uc, u, u, p, p, dw, ln_g, ln_b)


def _in_bwd_call(dp_hg, dp_cv, dp_gt, x, dx2, mod, pre_tm, w_in_g):
    S = x.shape[0]
    wc = w_in_g.shape[2]

    def body(hg_ref, cv_ref, gt_ref, x_ref, dx2_ref, mod_ref, g_ref, w_hbm, gx_ref, acc_ref, w_vmem, sem):
        @pl.when(pl.program_id(0) == 0)
        def _():
            cp = pltpu.make_async_copy(w_hbm, w_vmem, sem)
            cp.start()
            cp.wait()
            acc_ref[...] = jnp.zeros_like(acc_ref)

        dh = _mm(hg_ref[:, 0:wc], w_vmem[0], NT) + _mm(hg_ref[:, wc:2 * wc], w_vmem[1], NT)
        dh = dh + _mm(cv_ref[...], w_vmem[2], NT) + _mm(gt_ref[...], w_vmem[3], NT)
        xv = x_ref[...]
        r = lax.rsqrt(jnp.mean(xv * xv, axis=-1, keepdims=True) + EPS)
        xn = xv * r
        yv = xn * g_ref[...]
        acc_ref[0:1, :] += _rowsum(dh)
        acc_ref[1:2, :] += _rowsum(dh * yv)
        dyv = dh * (1.0 + mod_ref[:, D:2 * D])
        acc_ref[2:3, :] += _rowsum(dyv * xn)
        dxn = dyv * g_ref[...]
        gx_ref[...] = dx2_ref[...] + r * (dxn - xn * jnp.mean(dxn * xn, axis=-1, keepdims=True))

    tile = BS((TM, D), lambda i: (i, 0))
    return pl.pallas_call(
        body, name="in_bwd", grid=(S // TM,),
        out_shape=(SDS((S, D), F32), SDS((8, D), F32)),
        in_specs=[BS((TM, 4 * D), lambda i: (i, 0)), BS((TM, 2 * D), lambda i: (i, 0)),
                  BS((TM, 2 * D), lambda i: (i, 0)), tile, tile, BS((1, 6 * D), lambda i: (0, 0)),
                  BS((1, D), lambda i: (0, 0)), BS(memory_space=pl.ANY)],
        out_specs=(tile, BS((8, D), lambda i: (0, 0))),
        scratch_shapes=[pltpu.VMEM(w_in_g.shape, BF16), pltpu.SemaphoreType.DMA],
        compiler_params=_params("arbitrary"),
    )(dp_hg, dp_cv, dp_gt, x, dx2, mod, pre_tm, w_in_g)


def _wgrad_call(a, b, name, n_blocks=1):
    S, M = a.shape
    N = b.shape[1]
    bm, bn, bk = min(M, 1024), min(N // n_blocks, 1024), min(S, 1024)
    per = (N // n_blocks) // bn
    nk = S // bk

    def body(a_ref, b_ref, o_ref, acc):
        k = pl.program_id(2)

        @pl.when(k == 0)
        def _():
            acc[...] = jnp.zeros_like(acc)

        acc[...] += _mm(a_ref[...], b_ref[...], TN)

        @pl.when(k == nk - 1)
        def _():
            o_ref[...] = acc[...].astype(BF16)

    return pl.pallas_call(
        body, name=name, grid=(M // bm, N // bn, nk),
        out_shape=SDS((n_blocks, M, N // n_blocks), BF16),
        in_specs=[BS((bk, bm), lambda i, j, k: (k, i)), BS((bk, bn), lambda i, j, k: (k, j))],
        out_specs=BS((None, bm, bn), lambda i, j, k: (j // per, i, j % per)),
        scratch_shapes=[pltpu.VMEM((bm, bn), F32)],
        compiler_params=_params("parallel", "parallel", "arbitrary"),
    )(a, b)


def _outer_call(cact, dmod):
    n = dmod.shape[1]

    def body(a_ref, b_ref, o_ref):
        o_ref[...] = _mm(a_ref[...], b_ref[...], TN, HI)

    return pl.pallas_call(
        body, name="wgrad_ada", out_shape=SDS((D, n), F32),
        compiler_params=pltpu.CompilerParams(vmem_limit_bytes=VMEM_LIMIT),
    )(cact, dmod)


def _adamw_call(w, g, m, v, name):
    R, C = w.shape
    tr = R
    while tr * C > 512 * 1024 and tr % 16 == 0:
        tr //= 2
    c1 = 1.0 - ADAM_B1 ** ADAM_STEP
    c2 = 1.0 - ADAM_B2 ** ADAM_STEP

    def body(w_ref, g_ref, m_ref, v_ref, d_ref, m2_ref, v2_ref):
        g = g_ref[...]
        m2 = ADAM_B1 * m_ref[...] + (1.0 - ADAM_B1) * g
        v2 = ADAM_B2 * v_ref[...] + (1.0 - ADAM_B2) * (g * g)
        m2_ref[...] = m2
        v2_ref[...] = v2
        d_ref[...] = -ADAM_LR * ((m2 / c1) / (jnp.sqrt(v2 / c2) + ADAM_EPS) + ADAM_WD * w_ref[...])

    tile = BS((tr, C), lambda i: (i, 0))
    return pl.pallas_call(
        body, name=name, grid=(R // tr,), out_shape=(SDS((R, C), F32),) * 3,
        in_specs=[tile] * 4, out_specs=(tile,) * 3, compiler_params=_params("parallel"),
    )(w, g, m, v)


def _local_step(x, c, target, wts, small):
    mod8, cact = _mod_call(c, wts["ada"], small["b_ada"])
    mod = mod8[0:1]
    p, h1 = _fwd_in_call(x, mod, small["pre_tm"], wts["in"], small["b_in"])
    o, oa, st = _hgrn_fwd_call(p, small["logits"], small["hg_norm"])
    u, uc, cb = _conv_fwd_call(p, small["conv_dw"], small["conv_db"], small["ln_g"], small["ln_b"])
    ya, yb, mg, y, x2, h2 = _merge_fwd_call(oa, cb, p, x, mod, small["post_tm"], small["pre_cm"],
                                           wts["br_a"], wts["br_b"], wts["out"])
    z, da, dy2, dx2, acc_f = _ffn_call(h2, x2, target, mod, small["post_cm"], small["pre_cm"],
                                      wts["ff1"], wts["ff2"])
    dy, dya, dyb, doa, dcb, dp_gt, acc_m, bs_gt = _merge_bwd_call(
        dx2, y, ya, yb, p, mod, small["post_tm"], wts["br_a"], wts["br_b"], wts["out"])
    dp_hg, bs_hg, dlg, dgn = _hgrn_bwd_call(p, o, doa, st, small["logits"], small["hg_norm"])
    dp_cv, bs_cv, ddw, acc_c = _conv_bwd_call(dcb, uc, u, p, small["conv_dw"], small["ln_g"], small["ln_b"])
    gx, acc_i = _in_bwd_call(dp_hg, dp_cv, dp_gt, x, dx2, mod, small["pre_tm"], wts["in"])

    grads = {
        "in": jnp.concatenate([_wgrad_call(h1, dp_hg, "wgrad_in_hg", 2), _wgrad_call(h1, dp_cv, "wgrad_in_cv"),
                               _wgrad_call(h1, dp_gt, "wgrad_in_gt")], axis=0),
        "br_a": _wgrad_call(oa, dya, "wgrad_br_a")[0],
        "br_b": _wgrad_call(cb, dyb, "wgrad_br_b")[0],
        "out": _wgrad_call(mg, dy, "wgrad_out")[0],
        "ff1": _wgrad_call(h2, da, "wgrad_ff1", 4),
        "ff2": _wgrad_call(z, dy2, "wgrad_ff2")[0],
    }
    zrow = jnp.zeros((1, D), F32)
    rows = [acc_i[0:1], acc_i[1:2], acc_m[0:1], acc_f[2:3], acc_f[3:4], acc_f[0:1],
            acc_i[2:3], acc_m[1:2], acc_f[4:5], acc_f[1:2],
            jnp.concatenate([bs_hg, bs_cv, bs_gt], axis=1).reshape(8, D),
            dlg, dgn, acc_c[0:1], acc_c[1:2], acc_c[2:3],
            ddw,
            cact, acc_f[5:6]] + [zrow] * 6
    return gx, jnp.concatenate(rows, axis=0), grads


def kernel(x, c, w_ada, b_ada, pre_norm_tm, post_norm_tm, pre_norm_cm, post_norm_cm, w_in, b_in, hg_lb_logits, hg_norm, conv_dw, conv_db, conv_ln_g, conv_ln_b, w_br_a, w_br_b, w_out, w_ff1, w_ff2, loss_target, m_w_ada, m_b_ada, m_pre_norm_tm, m_post_norm_tm, m_pre_norm_cm, m_post_norm_cm, m_w_in, m_b_in, m_hg_lb_logits, m_hg_norm, m_conv_dw, m_conv_db, m_conv_ln_g, m_conv_ln_b, m_w_br_a, m_w_br_b, m_w_out, m_w_ff1, m_w_ff2, v_w_ada, v_b_ada, v_pre_norm_tm, v_post_norm_tm, v_pre_norm_cm, v_post_norm_cm, v_w_in, v_b_in, v_hg_lb_logits, v_hg_norm, v_conv_dw, v_conv_db, v_conv_ln_g, v_conv_ln_b, v_w_br_a, v_w_br_b, v_w_out, v_w_ff1, v_w_ff2):
    xi, yi, ci = lax.axis_index("x"), lax.axis_index("y"), lax.axis_index("c")
    chip = 2 * xi + yi
    c_idx = jnp.reshape(ci, (1,)).astype(jnp.int32)
    chip_idx = jnp.reshape(chip, (1,)).astype(jnp.int32)

    def pack_big(w_in_, br_a_, br_b_, out_, ff1_, ff2_):
        return jnp.concatenate([w_in_[0].reshape(R_IN, D), br_a_[0], br_b_[0], out_[0], ff1_[0], ff2_[0]], axis=0)

    def pack_small(ada_b, pre_t, post_t, pre_c, post_c, in_b, lg, hgn, cdb, lng, lnb, cdw):
        flat = jnp.concatenate([cdw[0].reshape(-1), jnp.zeros((8 * D - CONV_K * 256,), F32)]).reshape(8, D)
        return jnp.concatenate([ada_b.reshape(6, D), pre_t, post_t, pre_c, post_c, in_b.reshape(8, D), lg, hgn,
                                cdb, lng, lnb, flat], axis=0)

    pack = jnp.concatenate([pack_big(w_in, w_br_a, w_br_b, w_out, w_ff1, w_ff2), w_ada[0].reshape(R_ADA, D)],
                           axis=0).astype(BF16)
    half = lax.dynamic_slice_in_dim(pack, ci * (PACK_W // 2), PACK_W // 2, axis=0)
    wg = _allgather_call(half, "gather_weights", in_vmem=False, with_sum=False)[0].reshape(N_CHIPS, PACK_W, D)
    o1, o2, o3, o4, o5 = R_IN, R_IN + R_BR, R_IN + 2 * R_BR, R_IN + 3 * R_BR, R_IN + 3 * R_BR + R_FF
    wts = {
        "in": wg[:, 0:o1].reshape(N_CHIPS, D, IN_COLS // N_CHIPS),
        "br_a": wg[:, o1:o2].reshape(D, D),
        "br_b": wg[:, o2:o3].reshape(D, D),
        "out": wg[:, o3:o4].reshape(D, D),
        "ff1": wg[:, o4:o5],
        "ff2": wg[:, o5:PACK_G].reshape(D_FF, D),
        "ada": wg[:, PACK_G:PACK_W].reshape(N_CHIPS, D, 6 * D // N_CHIPS),
    }
    dw_blk = jnp.concatenate([conv_dw[0].reshape(-1), jnp.zeros((8 * D - CONV_K * 256,), F32)]).reshape(8, D)
    dw_all = _allgather_call(dw_blk, "gather_conv_dw", in_vmem=True, with_sum=False)[0]
    dw_all = dw_all.reshape(N_CHIPS, 2, 8 * D)[:, 0, :CONV_K * 256].reshape(N_CHIPS, CONV_K, 256)
    dw_full = dw_all.transpose(1, 0, 2).reshape(CONV_K, D)

    small = dict(b_ada=b_ada, pre_tm=pre_norm_tm, post_tm=post_norm_tm, pre_cm=pre_norm_cm, post_cm=post_norm_cm,
                 b_in=b_in, logits=hg_lb_logits, hg_norm=hg_norm, conv_dw=dw_full, conv_db=conv_db,
                 ln_g=conv_ln_g, ln_b=conv_ln_b)

    gx, srows, grads = _local_step(x[0], c, loss_target[0], wts, small)

    sall, ssum = _allgather_call(srows, "gather_small", in_vmem=True, with_sum=True)
    sall = sall.reshape(N_DEV, SMALL_ROWS, D)
    loss = jnp.sum(ssum[57])
    dmod_all = sall[:, 0:6, :].reshape(N_DEV, 6 * D)
    wa = 6 * D // N_CHIPS
    g_ada = _outer_call(sall[:, 56, :], lax.dynamic_slice_in_dim(dmod_all, chip * wa, wa, axis=1))
    g_dw = lax.dynamic_slice_in_dim(ssum[24:24 + CONV_K], chip * 256, 256, axis=1)
    g_small = jnp.concatenate(
        [ssum[0:24], jnp.concatenate([g_dw.reshape(-1), jnp.zeros((8 * D - CONV_K * 256,), F32)]).reshape(8, D)],
        axis=0)

    gp = jnp.concatenate([grads["in"].reshape(N_CHIPS, R_IN, D), grads["br_a"].reshape(N_CHIPS, R_BR, D),
                          grads["br_b"].reshape(N_CHIPS, R_BR, D), grads["out"].reshape(N_CHIPS, R_BR, D),
                          grads["ff1"], grads["ff2"].reshape(N_CHIPS, R_FF, D)], axis=1)
    gp = gp.reshape(N_CHIPS, 2, PACK_G // 2, D)
    part = _add_halves_call(gp, _sibling_halves_call(gp), c_idx)
    red = _add_chips_call(part, _chip_exchange_call(part), chip_idx)
    g_big = _sibling_join_call(red).reshape(PACK_G, D)

    shapes = {"in": w_in.shape, "br_a": w_br_a.shape, "br_b": w_br_b.shape, "out": w_out.shape,
              "ff1": w_ff1.shape, "ff2": w_ff2.shape}
    offs = {"in": (0, o1), "br_a": (o1, o2), "br_b": (o2, o3), "out": (o3, o4), "ff1": (o4, o5), "ff2": (o5, PACK_G)}
    wmv = {"in": (w_in, m_w_in, v_w_in), "br_a": (w_br_a, m_w_br_a, v_w_br_a), "br_b": (w_br_b, m_w_br_b, v_w_br_b),
           "out": (w_out, m_w_out, v_w_out), "ff1": (w_ff1, m_w_ff1, v_w_ff1), "ff2": (w_ff2, m_w_ff2, v_w_ff2)}
    res = {}
    for n in offs:
        shp = shapes[n]
        g2d = g_big[offs[n][0]:offs[n][1]].reshape(shp[1], shp[2])
        w_, m_, v_ = (a[0] for a in wmv[n])
        d_, m2_, v2_ = _adamw_call(w_, g2d, m_, v_, "adamw_" + n)
        res[n] = tuple(a.reshape(shp) for a in (g2d, d_, m2_, v2_))
    d_, m2_, v2_ = _adamw_call(w_ada[0], g_ada, m_w_ada[0], v_w_ada[0], "adamw_ada")
    res["ada"] = tuple(a.reshape(w_ada.shape) for a in (g_ada, d_, m2_, v2_))

    ws = pack_small(b_ada, pre_norm_tm, post_norm_tm, pre_norm_cm, post_norm_cm, b_in, hg_lb_logits, hg_norm,
                    conv_db, conv_ln_g, conv_ln_b, conv_dw)
    ms = pack_small(m_b_ada, m_pre_norm_tm, m_post_norm_tm, m_pre_norm_cm, m_post_norm_cm, m_b_in, m_hg_lb_logits,
                    m_hg_norm, m_conv_db, m_conv_ln_g, m_conv_ln_b, m_conv_dw)
    vs = pack_small(v_b_ada, v_pre_norm_tm, v_post_norm_tm, v_pre_norm_cm, v_post_norm_cm, v_b_in, v_hg_lb_logits,
                    v_hg_norm, v_conv_db, v_conv_ln_g, v_conv_ln_b, v_conv_dw)
    sres = (g_small,) + tuple(_adamw_call(ws, g_small, ms, vs, "adamw_small"))

    def unpack_small(t):
        return {"b_ada": t[0:6].reshape(1, 6 * D), "pre_tm": t[6:7], "post_tm": t[7:8], "pre_cm": t[8:9],
                "post_cm": t[9:10], "b_in": t[10:18].reshape(1, IN_COLS), "logits": t[18:20], "hg_norm": t[20:21],
                "conv_db": t[21:22], "ln_g": t[22:23], "ln_b": t[23:24],
                "conv_dw": t[24:32].reshape(-1)[:CONV_K * 256].reshape(1, CONV_K, 256)}

    order = ["ada", "b_ada", "pre_tm", "post_tm", "pre_cm", "post_cm", "in", "b_in", "logits", "hg_norm", "conv_dw",
             "conv_db", "ln_g", "ln_b", "br_a", "br_b", "out", "ff1", "ff2"]
    outs = [loss, gx.reshape(x.shape)]
    for kind in range(4):
        sm = unpack_small(sres[kind])
        for n in order:
            outs.append(res[n][kind] if n in res else sm[n])
    return tuple(outs)
```

```python
import functools

import jax
import jax.numpy as jnp
from jax import lax
from jax.experimental import pallas as pl
from jax.experimental.pallas import tpu as pltpu

F32, BF16 = jnp.float32, jnp.bfloat16
SDS = jax.ShapeDtypeStruct
BS = pl.BlockSpec
MESH = pl.DeviceIdType.MESH
HI = lax.Precision.HIGHEST

D = 1024
D_FF = 4096
IN_COLS = 8192
HEADS, DK = 8, 128
CHUNK = 32
CONV_K = 31
HALO = 32
SUB = 32
EPS = 1e-6
N_CHIPS, N_DEV = 4, 8
TM = 256
TB = 256
VMEM_LIMIT = 56 * 1024 * 1024

R_IN, R_BR, R_FF, R_ADA = 2048, 256, 1024, 1536
PACK_W = R_IN + 3 * R_BR + 2 * R_FF + R_ADA
PACK_G = R_IN + 3 * R_BR + 2 * R_FF
SMALL_ROWS = 64

ADAM_LR, ADAM_B1, ADAM_B2, ADAM_EPS, ADAM_WD, ADAM_STEP = 0.001, 0.9, 0.999, 1e-08, 0.01, 10

NN = (((1,), (0,)), ((), ()))
NT = (((1,), (1,)), ((), ()))
TN = (((0,), (0,)), ((), ()))


def _mm(a, b, dims=NN, precision=None):
    return lax.dot_general(a, b, dims, preferred_element_type=F32, precision=precision)


def _sig(v):
    return jax.nn.sigmoid(v)


def _dsilu(v, s):
    return s * (1.0 + v * (1.0 - s))


def _params(*sem):
    return pltpu.CompilerParams(dimension_semantics=sem if sem else None, vmem_limit_bytes=VMEM_LIMIT)


def _rowsum(v):
    return jnp.sum(v, axis=0, keepdims=True)


def _mesh_pos():
    return lax.axis_index("x"), lax.axis_index("y"), lax.axis_index("c")


def _allgather_call(blk, name, in_vmem, with_sum):
    m_per, n = blk.shape

    def body(x_ref, out_ref, *rest):
        if with_sum:
            sum_ref, send_sems, recv_sems, local_sem = rest
        else:
            send_sems, recv_sems, local_sem = rest
        x, y, c = _mesh_pos()
        me, sibling = (x, y, c), (x, y, 1 - c)
        chips = [(1 - x, y), (x, 1 - y), (1 - x, 1 - y)]

        def rows(px, py, pc):
            return out_ref.at[pl.ds((4 * px + 2 * py + pc) * m_per, m_per), :]

        def copy(k, block, to, src=None):
            return pltpu.make_async_remote_copy(
                src_ref=rows(*block) if src is None else src, dst_ref=rows(*block),
                send_sem=send_sems.at[k], recv_sem=recv_sems.at[k], device_id=to, device_id_type=MESH)

        mine = pltpu.make_async_copy(x_ref, rows(*me), local_sem)
        mine.start()
        first = [copy(0, me, sibling, src=x_ref)]
        first += [copy(1 + j, me, (*chip, c), src=x_ref) for j, chip in enumerate(chips)]
        for cp in first:
            cp.start()
        passed = [copy(4 + j, (*chip, c), sibling) for j, chip in enumerate(chips)]
        for j, chip in enumerate(chips):
            copy(1 + j, (*chip, c), me).wait_recv()
            passed[j].start()
        copy(0, sibling, me).wait_recv()
        for j, chip in enumerate(chips):
            copy(4 + j, (*chip, 1 - c), me).wait_recv()
        for cp in first + passed:
            cp.wait_send()
        mine.wait()
        if with_sum:
            acc = out_ref[0:m_per, :]
            for d in range(1, N_DEV):
                acc = acc + out_ref[d * m_per:(d + 1) * m_per, :]
            sum_ref[...] = acc

    space = pltpu.VMEM if in_vmem else pl.ANY
    out_shape = [SDS((N_DEV * m_per, n), blk.dtype)]
    out_specs = [BS(memory_space=space)]
    if with_sum:
        out_shape.append(SDS((m_per, n), blk.dtype))
        out_specs.append(BS(memory_space=pltpu.VMEM))
    return pl.pallas_call(
        body, name=name, out_shape=out_shape, in_specs=[BS(memory_space=space)], out_specs=out_specs,
        scratch_shapes=[pltpu.SemaphoreType.DMA((7,)), pltpu.SemaphoreType.DMA((7,)), pltpu.SemaphoreType.DMA],
        compiler_params=pltpu.CompilerParams(vmem_limit_bytes=VMEM_LIMIT),
    )(blk)


def _sibling_halves_call(g):
    _, _, h, n = g.shape

    def body(g_ref, out_ref, send_sems, recv_sems):
        x, y, c = _mesh_pos()
        cps = [pltpu.make_async_remote_copy(
            src_ref=g_ref.at[k, 1 - c], dst_ref=out_ref.at[k], send_sem=send_sems.at[k], recv_sem=recv_sems.at[k],
            device_id=(x, y, 1 - c), device_id_type=MESH) for k in range(N_CHIPS)]
        for cp in cps:
            cp.start()
        for cp in cps:
            cp.wait()

    return pl.pallas_call(
        body, name="rs_sibling_halves", out_shape=SDS((N_CHIPS, h, n), g.dtype),
        in_specs=[BS(memory_space=pl.ANY)], out_specs=BS(memory_space=pl.ANY),
        scratch_shapes=[pltpu.SemaphoreType.DMA((N_CHIPS,)), pltpu.SemaphoreType.DMA((N_CHIPS,))],
    )(g)


def _chip_exchange_call(p):
    _, h, n = p.shape

    def body(p_ref, out_ref, send_sems, recv_sems):
        x, y, c = _mesh_pos()
        chips = [(1 - x, y), (x, 1 - y), (1 - x, 1 - y)]
        cps = [pltpu.make_async_remote_copy(
            src_ref=p_ref.at[2 * cx + cy], dst_ref=out_ref.at[j], send_sem=send_sems.at[j], recv_sem=recv_sems.at[j],
            device_id=(cx, cy, c), device_id_type=MESH) for j, (cx, cy) in enumerate(chips)]
        for cp in cps:
            cp.start()
        for cp in cps:
            cp.wait()

    return pl.pallas_call(
        body, name="rs_chip_exchange", out_shape=SDS((3, h, n), p.dtype),
        in_specs=[BS(memory_space=pl.ANY)], out_specs=BS(memory_space=pl.ANY),
        scratch_shapes=[pltpu.SemaphoreType.DMA((3,)), pltpu.SemaphoreType.DMA((3,))],
    )(p)


def _sibling_join_call(r):
    h, n = r.shape
    q = h // 4

    def body(r_ref, out_ref, send_sems, recv_sems):
        x, y, c = _mesh_pos()
        cps = [pltpu.make_async_remote_copy(
            src_ref=r_ref.at[pl.ds(k * q, q)], dst_ref=out_ref.at[pl.ds(k * q, q)],
            send_sem=send_sems.at[k], recv_sem=recv_sems.at[k],
            device_id=(x, y, 1 - c), device_id_type=MESH) for k in range(4)]
        for cp in cps:
            cp.start()
        for cp in cps:
            cp.wait()

    return pl.pallas_call(
        body, name="rs_sibling_join", out_shape=SDS((h, n), r.dtype),
        in_specs=[BS(memory_space=pl.ANY)], out_specs=BS(memory_space=pl.ANY),
        scratch_shapes=[pltpu.SemaphoreType.DMA((4,)), pltpu.SemaphoreType.DMA((4,))],
    )(r)


def _add_halves_call(g, recv, c_idx):
    _, _, h, n = g.shape
    tr = h // 8

    def body(c_ref, g_ref, r_ref, o_ref):
        o_ref[...] = (g_ref[...].astype(F32) + r_ref[...].astype(F32)).astype(BF16)

    return pl.pallas_call(
        body, name="rs_add_halves", out_shape=SDS((N_CHIPS, h, n), BF16),
        grid_spec=pltpu.PrefetchScalarGridSpec(
            num_scalar_prefetch=1, grid=(N_CHIPS, 8),
            in_specs=[BS((None, None, tr, n), lambda k, r, c_ref: (k, c_ref[0], r, 0)),
                      BS((None, tr, n), lambda k, r, c_ref: (k, r, 0))],
            out_specs=BS((None, tr, n), lambda k, r, c_ref: (k, r, 0))),
        compiler_params=_params("arbitrary", "arbitrary"),
    )(c_idx, g, recv)


def _add_chips_call(p, recv, chip_idx):
    _, h, n = p.shape
    tr = h // 8

    def body(k_ref, p_ref, r_ref, o_ref):
        acc = p_ref[...].astype(F32)
        for j in range(3):
            acc = acc + r_ref[j].astype(F32)
        o_ref[...] = acc

    return pl.pallas_call(
        body, name="rs_add_chips", out_shape=SDS((h, n), F32),
        grid_spec=pltpu.PrefetchScalarGridSpec(
            num_scalar_prefetch=1, grid=(8,),
            in_specs=[BS((None, tr, n), lambda r, k_ref: (k_ref[0], r, 0)),
                      BS((3, tr, n), lambda r, k_ref: (0, r, 0))],
            out_specs=BS((tr, n), lambda r, k_ref: (r, 0))),
        compiler_params=_params("arbitrary"),
    )(chip_idx, p, recv)


def _mod_call(c, w_ada_g, b_ada):
    wc = w_ada_g.shape[2]

    def body(c_ref, w_ref, b_ref, mod_ref, cact_ref):
        cv = c_ref[...]
        ca = cv * _sig(cv)
        cact_ref[...] = ca
        cb = jnp.broadcast_to(ca, (8, D)).astype(BF16)
        for k in range(N_CHIPS):
            mod_ref[:, k * wc:(k + 1) * wc] = _mm(cb, w_ref[k]) + b_ref[:, k * wc:(k + 1) * wc]

    return pl.pallas_call(
        body, name="adaln_mod", out_shape=(SDS((8, 6 * D), F32), SDS((1, D), F32)),
        compiler_params=pltpu.CompilerParams(vmem_limit_bytes=VMEM_LIMIT),
    )(c, w_ada_g, b_ada)


def _fwd_in_call(x, mod, pre_tm, w_in_g, b_in):
    S = x.shape[0]
    wc = w_in_g.shape[2]

    def body(x_ref, mod_ref, g_ref, w_hbm, b_ref, p_ref, h_ref, w_vmem, sem):
        @pl.when(pl.program_id(0) == 0)
        def _():
            cp = pltpu.make_async_copy(w_hbm, w_vmem, sem)
            cp.start()
            cp.wait()

        xv = x_ref[...]
        r = lax.rsqrt(jnp.mean(xv * xv, axis=-1, keepdims=True) + EPS)
        h = xv * r * g_ref[...] * (1.0 + mod_ref[:, D:2 * D]) + mod_ref[:, 0:D]
        hb = h.astype(BF16)
        h_ref[...] = hb
        for k in range(N_CHIPS):
            p_ref[:, k * wc:(k + 1) * wc] = _mm(hb, w_vmem[k]) + b_ref[:, k * wc:(k + 1) * wc]

    return pl.pallas_call(
        body, name="fwd_in", grid=(S // TM,),
        out_shape=(SDS((S, IN_COLS), F32), SDS((S, D), BF16)),
        in_specs=[BS((TM, D), lambda i: (i, 0)), BS((1, 6 * D), lambda i: (0, 0)), BS((1, D), lambda i: (0, 0)),
                  BS(memory_space=pl.ANY), BS((1, IN_COLS), lambda i: (0, 0))],
        out_specs=(BS((TM, IN_COLS), lambda i: (i, 0)), BS((TM, D), lambda i: (i, 0))),
        scratch_shapes=[pltpu.VMEM(w_in_g.shape, BF16), pltpu.SemaphoreType.DMA],
        compiler_params=_params("arbitrary"),
    )(x, mod, pre_tm, w_in_g, b_in)


def _lower_bound(lg_ref):
    l0, l1 = lg_ref[0:1, :], lg_ref[1:2, :]
    mx = jnp.maximum(l0, l1)
    e0, e1 = jnp.exp(l0 - mx), jnp.exp(l1 - mx)
    return e0 / (e0 + e1)


def _tri_masks():
    ri = lax.broadcasted_iota(jnp.int32, (CHUNK, CHUNK), 0)
    ci = lax.broadcasted_iota(jnp.int32, (CHUNK, CHUNK), 1)
    return (ri >= ci).astype(F32), (ci >= ri).astype(F32)


def _hg_gates(q_r, f_r, lb, tril):
    sq = _sig(q_r)
    q = q_r * sq
    sf = _sig(f_r)
    f = lb + (1.0 - lb) * sf
    k = 1.0 - f
    g = jnp.log(f)
    b = _mm(tril, g, NN, HI)
    b_last = _rowsum(g)
    row = lax.broadcasted_iota(jnp.int32, g.shape, 0)
    ref = _rowsum(jnp.where(row < CHUNK // 2, g, 0.0))
    e = jnp.exp(b)
    eq = jnp.exp(jnp.minimum(b - ref, 80.0))
    ek = jnp.exp(jnp.minimum(ref - b, 80.0))
    dd = jnp.exp(b_last - b)
    return dict(sq=sq, q=q, sf=sf, f=f, k=k, e=e, eq=eq, ek=ek, dd=dd, elast=jnp.exp(b_last),
                qe=q * e, qt=q * eq, kt=k * ek, kd=k * dd)


def _hgrn_fwd_call(p, logits, gn):
    S = p.shape[0]
    ncb = TB // CHUNK

    def body(q_ref, f_ref, v_ref, og_ref, lg_ref, gn_ref, o_ref, oa_ref, st_ref, st_scr):
        @pl.when(pl.program_id(0) == 0)
        def _():
            st_scr[...] = jnp.zeros_like(st_scr)

        lb = _lower_bound(lg_ref)
        tril, _ = _tri_masks()

        def chunk(ci, carry):
            rows = pl.ds(pl.multiple_of(ci * CHUNK, CHUNK), CHUNK)
            st_ref[ci] = st_scr[...]
            t = _hg_gates(q_ref[rows, :], f_ref[rows, :], lb, tril)
            v = v_ref[rows, :]
            for h in range(HEADS):
                sl = slice(h * DK, (h + 1) * DK)
                stp = st_scr[:, sl]
                vb = v[:, sl].astype(BF16)
                inter = _mm(t["qe"][:, sl].astype(BF16), stp.astype(BF16), NT)
                a = _mm(t["qt"][:, sl].astype(BF16), t["kt"][:, sl].astype(BF16), NT) * tril
                o = inter + _mm(a.astype(BF16), vb)
                st_scr[:, sl] = stp * t["elast"][:, sl] + _mm(vb, t["kd"][:, sl].astype(BF16), TN)
                oh = o * lax.rsqrt(jnp.mean(o * o, axis=-1, keepdims=True) + EPS)
                og = og_ref[rows, sl]
                o_ref[rows, sl] = o
                oa_ref[rows, sl] = (oh * gn_ref[:, sl] * (og * _sig(og))).astype(BF16)
            return carry

        lax.fori_loop(0, ncb, chunk, 0)

    col = lambda j: BS((TB, D), lambda i, j=j: (i, j))
    return pl.pallas_call(
        body, name="hgrn_fwd", grid=(S // TB,),
        out_shape=(SDS((S, D), F32), SDS((S, D), BF16), SDS((S // CHUNK, DK, D), F32)),
        in_specs=[col(0), col(1), col(2), col(3), BS((2, D), lambda i: (0, 0)), BS((1, D), lambda i: (0, 0))],
        out_specs=(BS((TB, D), lambda i: (i, 0)), BS((TB, D), lambda i: (i, 0)),
                   BS((ncb, DK, D), lambda i: (i, 0, 0))),
        scratch_shapes=[pltpu.VMEM((DK, D), F32)],
        compiler_params=_params("arbitrary"),
    )(p, p, p, p, logits, gn)


def _layernorm_stats(uc):
    mu = jnp.mean(uc, axis=-1, keepdims=True)
    xc = uc - mu
    rs = lax.rsqrt(jnp.mean(xc * xc, axis=-1, keepdims=True) + EPS)
    return xc * rs, rs


def _conv_fwd_call(p, dw, db, ln_g, ln_b):
    S = p.shape[0]

    def body(cv_ref, cg_ref, dw_ref, db_ref, g_ref, b_ref, u_ref, uc_ref, cb_ref, uext):
        @pl.when(pl.program_id(0) == 0)
        def _():
            uext[0:HALO, :] = jnp.zeros((HALO, D), F32)

        u = cv_ref[...] * _sig(cg_ref[...])
        uext[HALO:HALO + TM, :] = u
        u_ref[...] = u
        for rb in range(TM // SUB):
            acc = jnp.broadcast_to(db_ref[...], (SUB, D))
            for j in range(CONV_K):
                s0 = HALO - (CONV_K - 1) + j + rb * SUB
                acc = acc + dw_ref[j:j + 1, :] * uext[s0:s0 + SUB, :]
            uc_ref[rb * SUB:(rb + 1) * SUB, :] = acc
            xh, _ = _layernorm_stats(acc)
            ln = xh * g_ref[...] + b_ref[...]
            cb_ref[rb * SUB:(rb + 1) * SUB, :] = (ln * _sig(ln)).astype(BF16)
        uext[0:HALO, :] = uext[TM:TM + HALO, :]

    vec = BS((1, D), lambda i: (0, 0))
    return pl.pallas_call(
        body, name="conv_fwd", grid=(S // TM,),
        out_shape=(SDS((S, D), F32), SDS((S, D), F32), SDS((S, D), BF16)),
        in_specs=[BS((TM, D), lambda i: (i, 4)), BS((TM, D), lambda i: (i, 5)),
                  BS((CONV_K, D), lambda i: (0, 0)), vec, vec, vec],
        out_specs=(BS((TM, D), lambda i: (i, 0)),) * 3,
        scratch_shapes=[pltpu.VMEM((HALO + TM, D), F32)],
        compiler_params=_params("arbitrary"),
    )(p, p, dw, db, ln_g, ln_b)


def _merge_fwd_call(oa, cb, p, x, mod, post_tm, pre_cm, w_a, w_b, w_o):
    S = x.shape[0]

    def body(oa_ref, cb_ref, ga_ref, gb_ref, x_ref, mod_ref, post_ref, pre_ref, wa_ref, wb_ref, wo_ref,
             ya_ref, yb_ref, mg_ref, y_ref, x2_ref, h2_ref):
        ya = _mm(oa_ref[...], wa_ref[...])
        yb = _mm(cb_ref[...], wb_ref[...])
        ya_ref[...] = ya
        yb_ref[...] = yb
        mg = (_sig(ga_ref[...]) * ya + _sig(gb_ref[...]) * yb).astype(BF16)
        mg_ref[...] = mg
        y = _mm(mg, wo_ref[...])
        y_ref[...] = y
        n = y * lax.rsqrt(jnp.mean(y * y, axis=-1, keepdims=True) + EPS) * post_ref[...]
        x2 = x_ref[...] + mod_ref[:, 2 * D:3 * D] * n
        x2_ref[...] = x2
        r2 = lax.rsqrt(jnp.mean(x2 * x2, axis=-1, keepdims=True) + EPS)
        h2 = x2 * r2 * pre_ref[...] * (1.0 + mod_ref[:, 4 * D:5 * D]) + mod_ref[:, 3 * D:4 * D]
        h2_ref[...] = h2.astype(BF16)

    tile = BS((TM, D), lambda i: (i, 0))
    vec = BS((1, D), lambda i: (0, 0))
    wsp = BS((D, D), lambda i: (0, 0))
    return pl.pallas_call(
        body, name="merge_fwd", grid=(S // TM,),
        out_shape=(SDS((S, D), F32), SDS((S, D), F32), SDS((S, D), BF16), SDS((S, D), F32), SDS((S, D), F32),
                   SDS((S, D), BF16)),
        in_specs=[tile, tile, BS((TM, D), lambda i: (i, 6)), BS((TM, D), lambda i: (i, 7)), tile,
                  BS((1, 6 * D), lambda i: (0, 0)), vec, vec, wsp, wsp, wsp],
        out_specs=(tile,) * 6,
        compiler_params=_params("arbitrary"),
    )(oa, cb, p, p, x, mod, post_tm, pre_cm, w_a, w_b, w_o)


def _ffn_call(h2, x2, target, mod, post_cm, pre_cm, w1_g, w2):
    S = x2.shape[0]

    def body(h2_ref, x2_ref, t_ref, mod_ref, post_ref, pre_ref, w1_hbm, w2_hbm,
             z_ref, da_ref, dy2_ref, dx2_ref, acc_ref, w1_v, w2_v, ra_scr, sems):
        @pl.when(pl.program_id(0) == 0)
        def _():
            c1 = pltpu.make_async_copy(w1_hbm, w1_v, sems.at[0])
            c2 = pltpu.make_async_copy(w2_hbm, w2_v, sems.at[1])
            c1.start()
            c2.start()
            c1.wait()
            c2.wait()
            acc_ref[...] = jnp.zeros_like(acc_ref)

        h2 = h2_ref[...]
        for k in range(N_CHIPS):
            ra = jnp.maximum(_mm(h2, w1_v[k]), 0.0)
            ra_scr[:, k * D:(k + 1) * D] = ra
            z_ref[:, k * D:(k + 1) * D] = (ra * ra).astype(BF16)
        y2 = _mm(z_ref[...], w2_v[...])
        ry = lax.rsqrt(jnp.mean(y2 * y2, axis=-1, keepdims=True) + EPS)
        yn = y2 * ry
        n = yn * post_ref[...]
        g2 = mod_ref[:, 5 * D:6 * D]
        x2 = x2_ref[...]
        err = x2 + g2 * n - t_ref[...]
        acc_ref[5:6, :] += _rowsum(err * err) * (0.5 / D)
        dout = err * (1.0 / D)
        acc_ref[0:1, :] += _rowsum(dout * n)
        dn = dout * g2
        acc_ref[1:2, :] += _rowsum(dn * yn)
        dyn = dn * post_ref[...]
        dy2 = (ry * (dyn - yn * jnp.mean(dyn * yn, axis=-1, keepdims=True))).astype(BF16)
        dy2_ref[...] = dy2
        dz = _mm(dy2, w2_v[...], NT)
        da_ref[...] = (dz * (2.0 * ra_scr[...])).astype(BF16)
        dh2 = jnp.zeros((TM, D), F32)
        for k in range(N_CHIPS):
            dh2 = dh2 + _mm(da_ref[:, k * D:(k + 1) * D], w1_v[k], NT)
        r2 = lax.rsqrt(jnp.mean(x2 * x2, axis=-1, keepdims=True) + EPS)
        xn = x2 * r2
        yv = xn * pre_ref[...]
        acc_ref[2:3, :] += _rowsum(dh2)
        acc_ref[3:4, :] += _rowsum(dh2 * yv)
        dyv = dh2 * (1.0 + mod_ref[:, 4 * D:5 * D])
        acc_ref[4:5, :] += _rowsum(dyv * xn)
        dxn = dyv * pre_ref[...]
        dx2_ref[...] = dout + r2 * (dxn - xn * jnp.mean(dxn * xn, axis=-1, keepdims=True))

    tile = BS((TM, D), lambda i: (i, 0))
    wide = BS((TM, D_FF), lambda i: (i, 0))
    vec = BS((1, D), lambda i: (0, 0))
    return pl.pallas_call(
        body, name="ffn_fwd_bwd", grid=(S // TM,),
        out_shape=(SDS((S, D_FF), BF16), SDS((S, D_FF), BF16), SDS((S, D), BF16), SDS((S, D), F32),
                   SDS((8, D), F32)),
        in_specs=[tile, tile, tile, BS((1, 6 * D), lambda i: (0, 0)), vec, vec,
                  BS(memory_space=pl.ANY), BS(memory_space=pl.ANY)],
        out_specs=(wide, wide, tile, tile, BS((8, D), lambda i: (0, 0))),
        scratch_shapes=[pltpu.VMEM(w1_g.shape, BF16), pltpu.VMEM(w2.shape, BF16), pltpu.VMEM((TM, D_FF), F32),
                        pltpu.SemaphoreType.DMA((2,))],
        compiler_params=_params("arbitrary"),
    )(h2, x2, target, mod, post_cm, pre_cm, w1_g, w2)


def _merge_bwd_call(dx2, y, ya, yb, p, mod, post_tm, w_a, w_b, w_o):
    S = y.shape[0]

    def body(dx2_ref, y_ref, ya_ref, yb_ref, ga_ref, gb_ref, mod_ref, post_ref, wa_ref, wb_ref, wo_ref,
             dy_ref, dya_ref, dyb_ref, doa_ref, dcb_ref, dpg_ref, acc_ref, bsum_ref):
        @pl.when(pl.program_id(0) == 0)
        def _():
            acc_ref[...] = jnp.zeros_like(acc_ref)
            bsum_ref[...] = jnp.zeros_like(bsum_ref)

        y = y_ref[...]
        ry = lax.rsqrt(jnp.mean(y * y, axis=-1, keepdims=True) + EPS)
        yn = y * ry
        dx2 = dx2_ref[...]
        acc_ref[0:1, :] += _rowsum(dx2 * (yn * post_ref[...]))
        dn = dx2 * mod_ref[:, 2 * D:3 * D]
        acc_ref[1:2, :] += _rowsum(dn * yn)
        dyn = dn * post_ref[...]
        dy = (ry * (dyn - yn * jnp.mean(dyn * yn, axis=-1, keepdims=True))).astype(BF16)
        dy_ref[...] = dy
        dmg = _mm(dy, wo_ref[...], NT)
        sa, sb = _sig(ga_ref[...]), _sig(gb_ref[...])
        dya = (dmg * sa).astype(BF16)
        dyb = (dmg * sb).astype(BF16)
        dya_ref[...] = dya
        dyb_ref[...] = dyb
        dga = dmg * ya_ref[...] * (sa * (1.0 - sa))
        dgb = dmg * yb_ref[...] * (sb * (1.0 - sb))
        dpg_ref[:, 0:D] = dga.astype(BF16)
        dpg_ref[:, D:2 * D] = dgb.astype(BF16)
        bsum_ref[:, 0:D] += _rowsum(dga)
        bsum_ref[:, D:2 * D] += _rowsum(dgb)
        doa_ref[...] = _mm(dya, wa_ref[...], NT)
        dcb_ref[...] = _mm(dyb, wb_ref[...], NT)

    tile = BS((TM, D), lambda i: (i, 0))
    vec = BS((1, D), lambda i: (0, 0))
    wsp = BS((D, D), lambda i: (0, 0))
    return pl.pallas_call(
        body, name="merge_bwd", grid=(S // TM,),
        out_shape=(SDS((S, D), BF16), SDS((S, D), BF16), SDS((S, D), BF16), SDS((S, D), F32), SDS((S, D), F32),
                   SDS((S, 2 * D), BF16), SDS((8, D), F32), SDS((1, 2 * D), F32)),
        in_specs=[tile, tile, tile, tile, BS((TM, D), lambda i: (i, 6)), BS((TM, D), lambda i: (i, 7)),
                  BS((1, 6 * D), lambda i: (0, 0)), vec, wsp, wsp, wsp],
        out_specs=(tile, tile, tile, tile, tile, BS((TM, 2 * D), lambda i: (i, 0)),
                   BS((8, D), lambda i: (0, 0)), BS((1, 2 * D), lambda i: (0, 0))),
        compiler_params=_params("arbitrary"),
    )(dx2, y, ya, yb, p, p, mod, post_tm, w_a, w_b, w_o)


def _hgrn_bwd_call(p, o, doa, st, logits, gn):
    S = p.shape[0]
    nb = S // TB
    ncb = TB // CHUNK

    def body(q_ref, f_ref, v_ref, og_ref, o_ref, doa_ref, st_ref, lg_ref, gn_ref,
             dp_ref, bsum_ref, dlg_ref, dgn_ref, dst_scr, dlb_scr, dqe_s, dqt_s, dkt_s, dkd_s, dv_s, dog_s, dble_s):
        i = pl.program_id(0)

        @pl.when(i == 0)
        def _():
            dst_scr[...] = jnp.zeros_like(dst_scr)
            dlb_scr[...] = jnp.zeros_like(dlb_scr)
            bsum_ref[...] = jnp.zeros_like(bsum_ref)
            dgn_ref[...] = jnp.zeros_like(dgn_ref)

        lb = _lower_bound(lg_ref)
        tril, triu = _tri_masks()

        def chunk(tt, carry):
            ci = ncb - 1 - tt
            rows = pl.ds(pl.multiple_of(ci * CHUNK, CHUNK), CHUNK)
            q_r, f_r = q_ref[rows, :], f_ref[rows, :]
            t = _hg_gates(q_r, f_r, lb, tril)
            v = v_ref[rows, :]
            for h in range(HEADS):
                sl = slice(h * DK, (h + 1) * DK)
                stp = st_ref[ci, :, sl]
                stb = stp.astype(BF16)
                qeb = t["qe"][:, sl].astype(BF16)
                qtb = t["qt"][:, sl].astype(BF16)
                ktb = t["kt"][:, sl].astype(BF16)
                kdb = t["kd"][:, sl].astype(BF16)
                vb = v[:, sl].astype(BF16)
                a = _mm(qtb, ktb, NT) * tril
                o_h = o_ref[rows, sl]
                rinv = lax.rsqrt(jnp.mean(o_h * o_h, axis=-1, keepdims=True) + EPS)
                oh = o_h * rinv
                og = og_ref[rows, sl]
                so = _sig(og)
                d_oa = doa_ref[rows, sl]
                don = d_oa * (og * so)
                dog_s[:, sl] = d_oa * (oh * gn_ref[:, sl]) * _dsilu(og, so)
                dgn_ref[:, sl] += _rowsum(don * oh)
                doh = don * gn_ref[:, sl]
                do = (rinv * (doh - oh * jnp.mean(doh * oh, axis=-1, keepdims=True))).astype(BF16)
                dqe_s[:, sl] = _mm(do, stb, NN)
                dstp = _mm(do, qeb, TN)
                dab = (_mm(do, vb, NT) * tril).astype(BF16)
                dqt_s[:, sl] = _mm(dab, ktb, NN)
                dkt_s[:, sl] = _mm(dab, qtb, TN)
                dstn = dst_scr[:, sl]
                dsb = dstn.astype(BF16)
                dkd_s[:, sl] = _mm(vb, dsb, NN)
                dv_s[:, sl] = _mm(a.astype(BF16), do, TN) + _mm(kdb, dsb, NT)
                el = t["elast"][:, sl]
                dst_scr[:, sl] = dstn * el + dstp
                dble_s[:, sl] = el * _rowsum(stp * dstn)
            dqe, dqt, dkt, dkd = dqe_s[...], dqt_s[...], dkt_s[...], dkd_s[...]
            dq = dqe * t["e"] + dqt * t["eq"]
            dk = dkt * t["ek"] + dkd * t["dd"]
            dkk = dkd * t["kd"]
            dbv = dqe * t["qe"] + dqt * t["qt"] - dkt * t["kt"] - dkk
            dg = _mm(triu, dbv, NN, HI) + (_rowsum(dkk) + dble_s[...])
            df = dg / t["f"] - dk
            sf = t["sf"]
            dlb_scr[...] += _rowsum(df * (1.0 - sf))
            dqr = dq * _dsilu(q_r, t["sq"])
            dfr = df * (1.0 - lb) * (sf * (1.0 - sf))
            dvv, dog = dv_s[...], dog_s[...]
            dp_ref[rows, 0:D] = dqr.astype(BF16)
            dp_ref[rows, D:2 * D] = dfr.astype(BF16)
            dp_ref[rows, 2 * D:3 * D] = dvv.astype(BF16)
            dp_ref[rows, 3 * D:4 * D] = dog.astype(BF16)
            bsum_ref[:, 0:D] += _rowsum(dqr)
            bsum_ref[:, D:2 * D] += _rowsum(dfr)
            bsum_ref[:, 2 * D:3 * D] += _rowsum(dvv)
            bsum_ref[:, 3 * D:4 * D] += _rowsum(dog)
            return carry

        lax.fori_loop(0, ncb, chunk, 0)

        dl = dlb_scr[...] * lb * (1.0 - lb)
        dlg_ref[0:1, :] = dl
        dlg_ref[1:2, :] = -dl

    col = lambda j: BS((TB, D), lambda i, j=j: (nb - 1 - i, j))
    rev = BS((TB, D), lambda i: (nb - 1 - i, 0))
    cd = pltpu.VMEM((CHUNK, D), F32)
    return pl.pallas_call(
        body, name="hgrn_bwd", grid=(nb,),
        out_shape=(SDS((S, 4 * D), BF16), SDS((1, 4 * D), F32), SDS((2, D), F32), SDS((1, D), F32)),
        in_specs=[col(0), col(1), col(2), col(3), rev, rev, BS((ncb, DK, D), lambda i: (nb - 1 - i, 0, 0)),
                  BS((2, D), lambda i: (0, 0)), BS((1, D), lambda i: (0, 0))],
        out_specs=(BS((TB, 4 * D), lambda i: (nb - 1 - i, 0)), BS((1, 4 * D), lambda i: (0, 0)),
                   BS((2, D), lambda i: (0, 0)), BS((1, D), lambda i: (0, 0))),
        scratch_shapes=[pltpu.VMEM((DK, D), F32), pltpu.VMEM((1, D), F32), cd, cd, cd, cd, cd, cd,
                        pltpu.VMEM((1, D), F32)],
        compiler_params=_params("arbitrary"),
    )(p, p, p, p, o, doa, st, logits, gn)


def _conv_bwd_call(dcb, uc, u, p, dw, ln_g, ln_b):
    S = uc.shape[0]
    nb = S // TM
    hb = TM // HALO

    def body(dcb_ref, uc_ref, u_ref, uh_ref, cv_ref, cg_ref, dw_ref, g_ref, b_ref,
             dp_ref, bsum_ref, ddw_ref, acc_ref, uext, dext):
        i = pl.program_id(0)

        @pl.when(i == 0)
        def _():
            dext[TM:TM + HALO, :] = jnp.zeros((HALO, D), F32)
            bsum_ref[...] = jnp.zeros_like(bsum_ref)
            ddw_ref[...] = jnp.zeros_like(ddw_ref)
            acc_ref[...] = jnp.zeros_like(acc_ref)

        first_tile = (nb - 1 - i) == 0
        uext[0:HALO, :] = jnp.where(first_tile, 0.0, uh_ref[...])
        uext[HALO:HALO + TM, :] = u_ref[...]

        for rb in range(TM // SUB):
            rs_ = slice(rb * SUB, (rb + 1) * SUB)
            xh, rs = _layernorm_stats(uc_ref[rs_, :])
            ln = xh * g_ref[...] + b_ref[...]
            dln = dcb_ref[rs_, :] * _dsilu(ln, _sig(ln))
            acc_ref[1:2, :] += _rowsum(dln * xh)
            acc_ref[2:3, :] += _rowsum(dln)
            dxh = dln * g_ref[...]
            duc = rs * (dxh - jnp.mean(dxh, axis=-1, keepdims=True)
                        - xh * jnp.mean(dxh * xh, axis=-1, keepdims=True))
            dext[rs_, :] = duc
            acc_ref[0:1, :] += _rowsum(duc)

        for j in range(CONV_K):
            part = jnp.zeros((SUB, D), F32)
            for rb in range(TM // SUB):
                s0 = HALO - (CONV_K - 1) + j + rb * SUB
                part = part + dext[rb * SUB:(rb + 1) * SUB, :] * uext[s0:s0 + SUB, :]
            ddw_ref[j:j + 1, :] += _rowsum(part)

        for rb in range(TM // SUB):
            rs_ = slice(rb * SUB, (rb + 1) * SUB)
            du = jnp.zeros((SUB, D), F32)
            for j in range(CONV_K):
                s0 = rb * SUB + (CONV_K - 1) - j
                du = du + dw_ref[j:j + 1, :] * dext[s0:s0 + SUB, :]
            cg = cg_ref[rs_, :]
            sg = _sig(cg)
            dcv = du * sg
            dcg = du * cv_ref[rs_, :] * (sg * (1.0 - sg))
            dp_ref[rs_, 0:D] = dcv.astype(BF16)
            dp_ref[rs_, D:2 * D] = dcg.astype(BF16)
            bsum_ref[:, 0:D] += _rowsum(dcv)
            bsum_ref[:, D:2 * D] += _rowsum(dcg)

        dext[TM:TM + HALO, :] = dext[0:HALO, :]

    rev = BS((TM, D), lambda i: (nb - 1 - i, 0))
    vec = BS((1, D), lambda i: (0, 0))
    return pl.pallas_call(
        body, name="conv_bwd", grid=(nb,),
        out_shape=(SDS((S, 2 * D), BF16), SDS((1, 2 * D), F32), SDS((32, D), F32), SDS((8, D), F32)),
        in_specs=[rev, rev, rev, BS((HALO, D), lambda i: (jnp.maximum((nb - 1 - i) * hb - 1, 0), 0)),
                  BS((TM, D), lambda i: (nb - 1 - i, 4)), BS((TM, D), lambda i: (nb - 1 - i, 5)),
                  BS((CONV_K, D), lambda i: (0, 0)), vec, vec],
        out_specs=(BS((TM, 2 * D), lambda i: (nb - 1 - i, 0)), BS((1, 2 * D), lambda i: (0, 0)),
                   BS((32, D), lambda i: (0, 0)), BS((8, D), lambda i: (0, 0))),
        scratch_shapes=[pltpu.VMEM((HALO + TM, D), F32), pltpu.VMEM((TM + HALO, D), F32)],
        compiler_params=_params("arbitrary"),
    )(dcb, uc, u, u, p, p, dw, ln_g, ln_b)


def _in_bwd_call(dp_hg, dp_cv, dp_gt, x, dx2, mod, pre_tm, w_in_g):
    S = x.shape[0]
    wc = w_in_g.shape[2]

    def body(hg_ref, cv_ref, gt_ref, x_ref, dx2_ref, mod_ref, g_ref, w_hbm, gx_ref, acc_ref, w_vmem, sem):
        @pl.when(pl.program_id(0) == 0)
        def _():
            cp = pltpu.make_async_copy(w_hbm, w_vmem, sem)
            cp.start()
            cp.wait()
            acc_ref[...] = jnp.zeros_like(acc_ref)

        dh = _mm(hg_ref[:, 0:wc], w_vmem[0], NT) + _mm(hg_ref[:, wc:2 * wc], w_vmem[1], NT)
        dh = dh + _mm(cv_ref[...], w_vmem[2], NT) + _mm(gt_ref[...], w_vmem[3], NT)
        xv = x_ref[...]
        r = lax.rsqrt(jnp.mean(xv * xv, axis=-1, keepdims=True) + EPS)
        xn = xv * r
        yv = xn * g_ref[...]
        acc_ref[0:1, :] += _rowsum(dh)
        acc_ref[1:2, :] += _rowsum(dh * yv)
        dyv = dh * (1.0 + mod_ref[:, D:2 * D])
        acc_ref[2:3, :] += _rowsum(dyv * xn)
        dxn = dyv * g_ref[...]
        gx_ref[...] = dx2_ref[...] + r * (dxn - xn * jnp.mean(dxn * xn, axis=-1, keepdims=True))

    tile = BS((TM, D), lambda i: (i, 0))
    return pl.pallas_call(
        body, name="in_bwd", grid=(S // TM,),
        out_shape=(SDS((S, D), F32), SDS((8, D), F32)),
        in_specs=[BS((TM, 4 * D), lambda i: (i, 0)), BS((TM, 2 * D), lambda i: (i, 0)),
                  BS((TM, 2 * D), lambda i: (i, 0)), tile, tile, BS((1, 6 * D), lambda i: (0, 0)),
                  BS((1, D), lambda i: (0, 0)), BS(memory_space=pl.ANY)],
        out_specs=(tile, BS((8, D), lambda i: (0, 0))),
        scratch_shapes=[pltpu.VMEM(w_in_g.shape, BF16), pltpu.SemaphoreType.DMA],
        compiler_params=_params("arbitrary"),
    )(dp_hg, dp_cv, dp_gt, x, dx2, mod, pre_tm, w_in_g)


def _wgrad_call(a, b, name, n_blocks=1):
    S, M = a.shape
    N = b.shape[1]
    bm, bn, bk = min(M, 1024), min(N // n_blocks, 1024), min(S, 1024)
    per = (N // n_blocks) // bn
    nk = S // bk

    def body(a_ref, b_ref, o_ref, acc):
        k = pl.program_id(2)

        @pl.when(k == 0)
        def _():
            acc[...] = jnp.zeros_like(acc)

        acc[...] += _mm(a_ref[...], b_ref[...], TN)

        @pl.when(k == nk - 1)
        def _():
            o_ref[...] = acc[...].astype(BF16)

    return pl.pallas_call(
        body, name=name, grid=(M // bm, N // bn, nk),
        out_shape=SDS((n_blocks, M, N // n_blocks), BF16),
        in_specs=[BS((bk, bm), lambda i, j, k: (k, i)), BS((bk, bn), lambda i, j, k: (k, j))],
        out_specs=BS((None, bm, bn), lambda i, j, k: (j // per, i, j % per)),
        scratch_shapes=[pltpu.VMEM((bm, bn), F32)],
        compiler_params=_params("parallel", "parallel", "arbitrary"),
    )(a, b)


def _outer_call(cact, dmod):
    n = dmod.shape[1]

    def body(a_ref, b_ref, o_ref):
        o_ref[...] = _mm(a_ref[...], b_ref[...], TN, HI)

    return pl.pallas_call(
        body, name="wgrad_ada", out_shape=SDS((D, n), F32),
        compiler_params=pltpu.CompilerParams(vmem_limit_bytes=VMEM_LIMIT),
    )(cact, dmod)


def _adamw_call(w, g, m, v, name):
    R, C = w.shape
    tr = R
    while tr * C > 512 * 1024 and tr % 16 == 0:
        tr //= 2
    c1 = 1.0 - ADAM_B1 ** ADAM_STEP
    c2 = 1.0 - ADAM_B2 ** ADAM_STEP

    def body(w_ref, g_ref, m_ref, v_ref, d_ref, m2_ref, v2_ref):
        g = g_ref[...]
        m2 = ADAM_B1 * m_ref[...] + (1.0 - ADAM_B1) * g
        v2 = ADAM_B2 * v_ref[...] + (1.0 - ADAM_B2) * (g * g)
        m2_ref[...] = m2
        v2_ref[...] = v2
        d_ref[...] = -ADAM_LR * ((m2 / c1) / (jnp.sqrt(v2 / c2) + ADAM_EPS) + ADAM_WD * w_ref[...])

    tile = BS((tr, C), lambda i: (i, 0))
    return pl.pallas_call(
        body, name=name, grid=(R // tr,), out_shape=(SDS((R, C), F32),) * 3,
        in_specs=[tile] * 4, out_specs=(tile,) * 3, compiler_params=_params("parallel"),
    )(w, g, m, v)


def _local_step(x, c, target, wts, small):
    mod8, cact = _mod_call(c, wts["ada"], small["b_ada"])
    mod = mod8[0:1]
    p, h1 = _fwd_in_call(x, mod, small["pre_tm"], wts["in"], small["b_in"])
    o, oa, st = _hgrn_fwd_call(p, small["logits"], small["hg_norm"])
    u, uc, cb = _conv_fwd_call(p, small["conv_dw"], small["conv_db"], small["ln_g"], small["ln_b"])
    ya, yb, mg, y, x2, h2 = _merge_fwd_call(oa, cb, p, x, mod, small["post_tm"], small["pre_cm"],
                                           wts["br_a"], wts["br_b"], wts["out"])
    z, da, dy2, dx2, acc_f = _ffn_call(h2, x2, target, mod, small["post_cm"], small["pre_cm"],
                                      wts["ff1"], wts["ff2"])
    dy, dya, dyb, doa, dcb, dp_gt, acc_m, bs_gt = _merge_bwd_call(
        dx2, y, ya, yb, p, mod, small["post_tm"], wts["br_a"], wts["br_b"], wts["out"])
    dp_hg, bs_hg, dlg, dgn = _hgrn_bwd_call(p, o, doa, st, small["logits"], small["hg_norm"])
    dp_cv, bs_cv, ddw, acc_c = _conv_bwd_call(dcb, uc, u, p, small["conv_dw"], small["ln_g"], small["ln_b"])
    gx, acc_i = _in_bwd_call(dp_hg, dp_cv, dp_gt, x, dx2, mod, small["pre_tm"], wts["in"])

    grads = {
        "in": jnp.concatenate([_wgrad_call(h1, dp_hg, "wgrad_in_hg", 2), _wgrad_call(h1, dp_cv, "wgrad_in_cv"),
                               _wgrad_call(h1, dp_gt, "wgrad_in_gt")], axis=0),
        "br_a": _wgrad_call(oa, dya, "wgrad_br_a")[0],
        "br_b": _wgrad_call(cb, dyb, "wgrad_br_b")[0],
        "out": _wgrad_call(mg, dy, "wgrad_out")[0],
        "ff1": _wgrad_call(h2, da, "wgrad_ff1", 4),
        "ff2": _wgrad_call(z, dy2, "wgrad_ff2")[0],
    }
    zrow = jnp.zeros((1, D), F32)
    rows = [acc_i[0:1], acc_i[1:2], acc_m[0:1], acc_f[2:3], acc_f[3:4], acc_f[0:1],
            acc_i[2:3], acc_m[1:2], acc_f[4:5], acc_f[1:2],
            jnp.concatenate([bs_hg, bs_cv, bs_gt], axis=1).reshape(8, D),
            dlg, dgn, acc_c[0:1], acc_c[1:2], acc_c[2:3],
            ddw,
            cact, acc_f[5:6]] + [zrow] * 6
    return gx, jnp.concatenate(rows, axis=0), grads


def kernel(x, c, w_ada, b_ada, pre_norm_tm, post_norm_tm, pre_norm_cm, post_norm_cm, w_in, b_in, hg_lb_logits, hg_norm, conv_dw, conv_db, conv_ln_g, conv_ln_b, w_br_a, w_br_b, w_out, w_ff1, w_ff2, loss_target, m_w_ada, m_b_ada, m_pre_norm_tm, m_post_norm_tm, m_pre_norm_cm, m_post_norm_cm, m_w_in, m_b_in, m_hg_lb_logits, m_hg_norm, m_conv_dw, m_conv_db, m_conv_ln_g, m_conv_ln_b, m_w_br_a, m_w_br_b, m_w_out, m_w_ff1, m_w_ff2, v_w_ada, v_b_ada, v_pre_norm_tm, v_post_norm_tm, v_pre_norm_cm, v_post_norm_cm, v_w_in, v_b_in, v_hg_lb_logits, v_hg_norm, v_conv_dw, v_conv_db, v_conv_ln_g, v_conv_ln_b, v_w_br_a, v_w_br_b, v_w_out, v_w_ff1, v_w_ff2):
    xi, yi, ci = lax.axis_index("x"), lax.axis_index("y"), lax.axis_index("c")
    chip = 2 * xi + yi
    c_idx = jnp.reshape(ci, (1,)).astype(jnp.int32)
    chip_idx = jnp.reshape(chip, (1,)).astype(jnp.int32)

    def pack_big(w_in_, br_a_, br_b_, out_, ff1_, ff2_):
        return jnp.concatenate([w_in_[0].reshape(R_IN, D), br_a_[0], br_b_[0], out_[0], ff1_[0], ff2_[0]], axis=0)

    def pack_small(ada_b, pre_t, post_t, pre_c, post_c, in_b, lg, hgn, cdb, lng, lnb, cdw):
        flat = jnp.concatenate([cdw[0].reshape(-1), jnp.zeros((8 * D - CONV_K * 256,), F32)]).reshape(8, D)
        return jnp.concatenate([ada_b.reshape(6, D), pre_t, post_t, pre_c, post_c, in_b.reshape(8, D), lg, hgn,
                                cdb, lng, lnb, flat], axis=0)

    pack = jnp.concatenate([pack_big(w_in, w_br_a, w_br_b, w_out, w_ff1, w_ff2), w_ada[0].reshape(R_ADA, D)],
                           axis=0).astype(BF16)
    half = lax.dynamic_slice_in_dim(pack, ci * (PACK_W // 2), PACK_W // 2, axis=0)
    wg = _allgather_call(half, "gather_weights", in_vmem=False, with_sum=False)[0].reshape(N_CHIPS, PACK_W, D)
    o1, o2, o3, o4, o5 = R_IN, R_IN + R_BR, R_IN + 2 * R_BR, R_IN + 3 * R_BR, R_IN + 3 * R_BR + R_FF
    wts = {
        "in": wg[:, 0:o1].reshape(N_CHIPS, D, IN_COLS // N_CHIPS),
        "br_a": wg[:, o1:o2].reshape(D, D),
        "br_b": wg[:, o2:o3].reshape(D, D),
        "out": wg[:, o3:o4].reshape(D, D),
        "ff1": wg[:, o4:o5],
        "ff2": wg[:, o5:PACK_G].reshape(D_FF, D),
        "ada": wg[:, PACK_G:PACK_W].reshape(N_CHIPS, D, 6 * D // N_CHIPS),
    }
    dw_blk = jnp.concatenate([conv_dw[0].reshape(-1), jnp.zeros((8 * D - CONV_K * 256,), F32)]).reshape(8, D)
    dw_all = _allgather_call(dw_blk, "gather_conv_dw", in_vmem=True, with_sum=False)[0]
    dw_all = dw_all.reshape(N_CHIPS, 2, 8 * D)[:, 0, :CONV_K * 256].reshape(N_CHIPS, CONV_K, 256)
    dw_full = dw_all.transpose(1, 0, 2).reshape(CONV_K, D)

    small = dict(b_ada=b_ada, pre_tm=pre_norm_tm, post_tm=post_norm_tm, pre_cm=pre_norm_cm, post_cm=post_norm_cm,
                 b_in=b_in, logits=hg_lb_logits, hg_norm=hg_norm, conv_dw=dw_full, conv_db=conv_db,
                 ln_g=conv_ln_g, ln_b=conv_ln_b)

    gx, srows, grads = _local_step(x[0], c, loss_target[0], wts, small)

    sall, ssum = _allgather_call(srows, "gather_small", in_vmem=True, with_sum=True)
    sall = sall.reshape(N_DEV, SMALL_ROWS, D)
    loss = jnp.sum(ssum[57])
    dmod_all = sall[:, 0:6, :].reshape(N_DEV, 6 * D)
    wa = 6 * D // N_CHIPS
    g_ada = _outer_call(sall[:, 56, :], lax.dynamic_slice_in_dim(dmod_all, chip * wa, wa, axis=1))
    g_dw = lax.dynamic_slice_in_dim(ssum[24:24 + CONV_K], chip * 256, 256, axis=1)
    g_small = jnp.concatenate(
        [ssum[0:24], jnp.concatenate([g_dw.reshape(-1), jnp.zeros((8 * D - CONV_K * 256,), F32)]).reshape(8, D)],
        axis=0)

    gp = jnp.concatenate([grads["in"].reshape(N_CHIPS, R_IN, D), grads["br_a"].reshape(N_CHIPS, R_BR, D),
                          grads["br_b"].reshape(N_CHIPS, R_BR, D), grads["out"].reshape(N_CHIPS, R_BR, D),
                          grads["ff1"], grads["ff2"].reshape(N_CHIPS, R_FF, D)], axis=1)
    gp = gp.reshape(N_CHIPS, 2, PACK_G // 2, D)
    part = _add_halves_call(gp, _sibling_halves_call(gp), c_idx)
    red = _add_chips_call(part, _chip_exchange_call(part), chip_idx)
    other = _sibling_join_call(red)
    g_big = jnp.where(ci == 0, jnp.concatenate([red, other], axis=0), jnp.concatenate([other, red], axis=0))

    shapes = {"in": w_in.shape, "br_a": w_br_a.shape, "br_b": w_br_b.shape, "out": w_out.shape,
              "ff1": w_ff1.shape, "ff2": w_ff2.shape}
    offs = {"in": (0, o1), "br_a": (o1, o2), "br_b": (o2, o3), "out": (o3, o4), "ff1": (o4, o5), "ff2": (o5, PACK_G)}
    wmv = {"in": (w_in, m_w_in, v_w_in), "br_a": (w_br_a, m_w_br_a, v_w_br_a), "br_b": (w_br_b, m_w_br_b, v_w_br_b),
           "out": (w_out, m_w_out, v_w_out), "ff1": (w_ff1, m_w_ff1, v_w_ff1), "ff2": (w_ff2, m_w_ff2, v_w_ff2)}
    res = {}
    for n in offs:
        shp = shapes[n]
        g2d = g_big[offs[n][0]:offs[n][1]].reshape(shp[1], shp[2])
        w_, m_, v_ = (a[0] for a in wmv[n])
        d_, m2_, v2_ = _adamw_call(w_, g2d, m_, v_, "adamw_" + n)
        res[n] = tuple(a.reshape(shp) for a in (g2d, d_, m2_, v2_))
    d_, m2_, v2_ = _adamw_call(w_ada[0], g_ada, m_w_ada[0], v_w_ada[0], "adamw_ada")
    res["ada"] = tuple(a.reshape(w_ada.shape) for a in (g_ada, d_, m2_, v2_))

    ws = pack_small(b_ada, pre_norm_tm, post_norm_tm, pre_norm_cm, post_norm_cm, b_in, hg_lb_logits, hg_norm,
                    conv_db, conv_ln_g, conv_ln_b, conv_dw)
    ms = pack_small(m_b_ada, m_pre_norm_tm, m_post_norm_tm, m_pre_norm_cm, m_post_norm_cm, m_b_in, m_hg_lb_logits,
                    m_hg_norm, m_conv_db, m_conv_ln_g, m_conv_ln_b, m_conv_dw)
    vs = pack_small(v_b_ada, v_pre_norm_tm, v_post_norm_tm, v_pre_norm_cm, v_post_norm_cm, v_b_in, v_hg_lb_logits,
                    v_hg_norm, v_conv_db, v_conv_ln_g, v_conv_ln_b, v_conv_dw)
    sres = (g_small,) + tuple(_adamw_call(ws, g_small, ms, vs, "adamw_small"))

    def unpack_small(t):
        return {"b_ada": t[0:6].reshape(1, 6 * D), "pre_tm": t[6:7], "post_tm": t[7:8], "pre_cm": t[8:9],
                "post_cm": t[9:10], "b_in": t[10:18].reshape(1, IN_COLS), "logits": t[18:20], "hg_norm": t[20:21],
                "conv_db": t[21:22], "ln_g": t[22:23], "ln_b": t[23:24],
                "conv_dw": t[24:32].reshape(-1)[:CONV_K * 256].reshape(1, CONV_K, 256)}

    order = ["ada", "b_ada", "pre_tm", "post_tm", "pre_cm", "post_cm", "in", "b_in", "logits", "hg_norm", "conv_dw",
             "conv_db", "ln_g", "ln_b", "br_a", "br_b", "out", "ff1", "ff2"]
    outs = [loss, gx.reshape(x.shape)]
    for kind in range(4):
        sm = unpack_small(sres[kind])
        for n in order:
            outs.append(res[n][kind] if n in res else sm[n])
    return tuple(outs)
```

```python
import functools

import jax
import jax.numpy as jnp
from jax import lax
from jax.experimental import pallas as pl
from jax.experimental.pallas import tpu as pltpu

F32, BF16 = jnp.float32, jnp.bfloat16
SDS = jax.ShapeDtypeStruct
BS = pl.BlockSpec
MESH = pl.DeviceIdType.MESH
HI = lax.Precision.HIGHEST

D = 1024
D_FF = 4096
IN_COLS = 8192
HEADS, DK = 8, 128
CHUNK = 128
CONV_K = 31
HALO = 32
SUB = 32
EPS = 1e-6
N_CHIPS, N_DEV = 4, 8
TM = 256
TB = 256
VMEM_LIMIT = 56 * 1024 * 1024

R_IN, R_BR, R_FF, R_ADA = 2048, 256, 1024, 1536
PACK_W = R_IN + 3 * R_BR + 2 * R_FF + R_ADA
PACK_G = R_IN + 3 * R_BR + 2 * R_FF
SMALL_ROWS = 64

ADAM_LR, ADAM_B1, ADAM_B2, ADAM_EPS, ADAM_WD, ADAM_STEP = 0.001, 0.9, 0.999, 1e-08, 0.01, 10

NN = (((1,), (0,)), ((), ()))
NT = (((1,), (1,)), ((), ()))
TN = (((0,), (0,)), ((), ()))


def _mm(a, b, dims=NN, precision=None):
    return lax.dot_general(a, b, dims, preferred_element_type=F32, precision=precision)


def _sig(v):
    return jax.nn.sigmoid(v)


def _dsilu(v, s):
    return s * (1.0 + v * (1.0 - s))


def _params(*sem):
    return pltpu.CompilerParams(dimension_semantics=sem if sem else None, vmem_limit_bytes=VMEM_LIMIT)


def _rowsum(v):
    return jnp.sum(v, axis=0, keepdims=True)


def _mesh_pos():
    return lax.axis_index("x"), lax.axis_index("y"), lax.axis_index("c")


def _allgather_call(blk, name, in_vmem, with_sum):
    m_per, n = blk.shape

    def body(x_ref, out_ref, *rest):
        if with_sum:
            sum_ref, send_sems, recv_sems, local_sem = rest
        else:
            send_sems, recv_sems, local_sem = rest
        x, y, c = _mesh_pos()
        me, sibling = (x, y, c), (x, y, 1 - c)
        chips = [(1 - x, y), (x, 1 - y), (1 - x, 1 - y)]

        def rows(px, py, pc):
            return out_ref.at[pl.ds((4 * px + 2 * py + pc) * m_per, m_per), :]

        def copy(k, block, to, src=None):
            return pltpu.make_async_remote_copy(
                src_ref=rows(*block) if src is None else src, dst_ref=rows(*block),
                send_sem=send_sems.at[k], recv_sem=recv_sems.at[k], device_id=to, device_id_type=MESH)

        mine = pltpu.make_async_copy(x_ref, rows(*me), local_sem)
        mine.start()
        first = [copy(0, me, sibling, src=x_ref)]
        first += [copy(1 + j, me, (*chip, c), src=x_ref) for j, chip in enumerate(chips)]
        for cp in first:
            cp.start()
        passed = [copy(4 + j, (*chip, c), sibling) for j, chip in enumerate(chips)]
        for j, chip in enumerate(chips):
            copy(1 + j, (*chip, c), me).wait_recv()
            passed[j].start()
        copy(0, sibling, me).wait_recv()
        for j, chip in enumerate(chips):
            copy(4 + j, (*chip, 1 - c), me).wait_recv()
        for cp in first + passed:
            cp.wait_send()
        mine.wait()
        if with_sum:
            acc = out_ref[0:m_per, :]
            for d in range(1, N_DEV):
                acc = acc + out_ref[d * m_per:(d + 1) * m_per, :]
            sum_ref[...] = acc

    space = pltpu.VMEM if in_vmem else pl.ANY
    out_shape = [SDS((N_DEV * m_per, n), blk.dtype)]
    out_specs = [BS(memory_space=space)]
    if with_sum:
        out_shape.append(SDS((m_per, n), blk.dtype))
        out_specs.append(BS(memory_space=pltpu.VMEM))
    return pl.pallas_call(
        body, name=name, out_shape=out_shape, in_specs=[BS(memory_space=space)], out_specs=out_specs,
        scratch_shapes=[pltpu.SemaphoreType.DMA((7,)), pltpu.SemaphoreType.DMA((7,)), pltpu.SemaphoreType.DMA],
        compiler_params=pltpu.CompilerParams(vmem_limit_bytes=VMEM_LIMIT),
    )(blk)


def _sibling_halves_call(g):
    _, _, h, n = g.shape

    def body(g_ref, out_ref, send_sems, recv_sems):
        x, y, c = _mesh_pos()
        cps = [pltpu.make_async_remote_copy(
            src_ref=g_ref.at[k, 1 - c], dst_ref=out_ref.at[k], send_sem=send_sems.at[k], recv_sem=recv_sems.at[k],
            device_id=(x, y, 1 - c), device_id_type=MESH) for k in range(N_CHIPS)]
        for cp in cps:
            cp.start()
        for cp in cps:
            cp.wait()

    return pl.pallas_call(
        body, name="rs_sibling_halves", out_shape=SDS((N_CHIPS, h, n), g.dtype),
        in_specs=[BS(memory_space=pl.ANY)], out_specs=BS(memory_space=pl.ANY),
        scratch_shapes=[pltpu.SemaphoreType.DMA((N_CHIPS,)), pltpu.SemaphoreType.DMA((N_CHIPS,))],
    )(g)


def _chip_exchange_call(p):
    _, h, n = p.shape

    def body(p_ref, out_ref, send_sems, recv_sems):
        x, y, c = _mesh_pos()
        chips = [(1 - x, y), (x, 1 - y), (1 - x, 1 - y)]
        cps = [pltpu.make_async_remote_copy(
            src_ref=p_ref.at[2 * cx + cy], dst_ref=out_ref.at[j], send_sem=send_sems.at[j], recv_sem=recv_sems.at[j],
            device_id=(cx, cy, c), device_id_type=MESH) for j, (cx, cy) in enumerate(chips)]
        for cp in cps:
            cp.start()
        for cp in cps:
            cp.wait()

    return pl.pallas_call(
        body, name="rs_chip_exchange", out_shape=SDS((3, h, n), p.dtype),
        in_specs=[BS(memory_space=pl.ANY)], out_specs=BS(memory_space=pl.ANY),
        scratch_shapes=[pltpu.SemaphoreType.DMA((3,)), pltpu.SemaphoreType.DMA((3,))],
    )(p)


def _sibling_join_call(r):
    h, n = r.shape
    q = h // 4

    def body(r_ref, out_ref, send_sems, recv_sems):
        x, y, c = _mesh_pos()
        cps = [pltpu.make_async_remote_copy(
            src_ref=r_ref.at[pl.ds(k * q, q)], dst_ref=out_ref.at[pl.ds(k * q, q)],
            send_sem=send_sems.at[k], recv_sem=recv_sems.at[k],
            device_id=(x, y, 1 - c), device_id_type=MESH) for k in range(4)]
        for cp in cps:
            cp.start()
        for cp in cps:
            cp.wait()

    return pl.pallas_call(
        body, name="rs_sibling_join", out_shape=SDS((h, n), r.dtype),
        in_specs=[BS(memory_space=pl.ANY)], out_specs=BS(memory_space=pl.ANY),
        scratch_shapes=[pltpu.SemaphoreType.DMA((4,)), pltpu.SemaphoreType.DMA((4,))],
    )(r)


def _add_halves_call(g, recv, c_idx):
    _, _, h, n = g.shape
    tr = h // 8

    def body(c_ref, g_ref, r_ref, o_ref):
        o_ref[...] = (g_ref[...].astype(F32) + r_ref[...].astype(F32)).astype(BF16)

    return pl.pallas_call(
        body, name="rs_add_halves", out_shape=SDS((N_CHIPS, h, n), BF16),
        grid_spec=pltpu.PrefetchScalarGridSpec(
            num_scalar_prefetch=1, grid=(N_CHIPS, 8),
            in_specs=[BS((None, None, tr, n), lambda k, r, c_ref: (k, c_ref[0], r, 0)),
                      BS((None, tr, n), lambda k, r, c_ref: (k, r, 0))],
            out_specs=BS((None, tr, n), lambda k, r, c_ref: (k, r, 0))),
        compiler_params=_params("arbitrary", "arbitrary"),
    )(c_idx, g, recv)


def _add_chips_call(p, recv, chip_idx):
    _, h, n = p.shape
    tr = h // 8

    def body(k_ref, p_ref, r_ref, o_ref):
        acc = p_ref[...].astype(F32)
        for j in range(3):
            acc = acc + r_ref[j].astype(F32)
        o_ref[...] = acc

    return pl.pallas_call(
        body, name="rs_add_chips", out_shape=SDS((h, n), F32),
        grid_spec=pltpu.PrefetchScalarGridSpec(
            num_scalar_prefetch=1, grid=(8,),
            in_specs=[BS((None, tr, n), lambda r, k_ref: (k_ref[0], r, 0)),
                      BS((3, tr, n), lambda r, k_ref: (0, r, 0))],
            out_specs=BS((tr, n), lambda r, k_ref: (r, 0))),
        compiler_params=_params("arbitrary"),
    )(chip_idx, p, recv)


def _mod_call(c, w_ada_g, b_ada):
    wc = w_ada_g.shape[2]

    def body(c_ref, w_ref, b_ref, mod_ref, cact_ref):
        cv = c_ref[...]
        ca = cv * _sig(cv)
        cact_ref[...] = ca
        cb = jnp.broadcast_to(ca, (8, D)).astype(BF16)
        for k in range(N_CHIPS):
            mod_ref[:, k * wc:(k + 1) * wc] = _mm(cb, w_ref[k]) + b_ref[:, k * wc:(k + 1) * wc]

    return pl.pallas_call(
        body, name="adaln_mod", out_shape=(SDS((8, 6 * D), F32), SDS((1, D), F32)),
        compiler_params=pltpu.CompilerParams(vmem_limit_bytes=VMEM_LIMIT),
    )(c, w_ada_g, b_ada)


def _fwd_in_call(x, mod, pre_tm, w_in_g, b_in):
    S = x.shape[0]
    wc = w_in_g.shape[2]

    def body(x_ref, mod_ref, g_ref, w_hbm, b_ref, p_ref, h_ref, w_vmem, sem):
        @pl.when(pl.program_id(0) == 0)
        def _():
            cp = pltpu.make_async_copy(w_hbm, w_vmem, sem)
            cp.start()
            cp.wait()

        xv = x_ref[...]
        r = lax.rsqrt(jnp.mean(xv * xv, axis=-1, keepdims=True) + EPS)
        h = xv * r * g_ref[...] * (1.0 + mod_ref[:, D:2 * D]) + mod_ref[:, 0:D]
        hb = h.astype(BF16)
        h_ref[...] = hb
        for k in range(N_CHIPS):
            p_ref[:, k * wc:(k + 1) * wc] = _mm(hb, w_vmem[k]) + b_ref[:, k * wc:(k + 1) * wc]

    return pl.pallas_call(
        body, name="fwd_in", grid=(S // TM,),
        out_shape=(SDS((S, IN_COLS), F32), SDS((S, D), BF16)),
        in_specs=[BS((TM, D), lambda i: (i, 0)), BS((1, 6 * D), lambda i: (0, 0)), BS((1, D), lambda i: (0, 0)),
                  BS(memory_space=pl.ANY), BS((1, IN_COLS), lambda i: (0, 0))],
        out_specs=(BS((TM, IN_COLS), lambda i: (i, 0)), BS((TM, D), lambda i: (i, 0))),
        scratch_shapes=[pltpu.VMEM(w_in_g.shape, BF16), pltpu.SemaphoreType.DMA],
        compiler_params=_params("arbitrary"),
    )(x, mod, pre_tm, w_in_g, b_in)


def _lower_bound(lg_ref):
    l0, l1 = lg_ref[0:1, :], lg_ref[1:2, :]
    mx = jnp.maximum(l0, l1)
    e0, e1 = jnp.exp(l0 - mx), jnp.exp(l1 - mx)
    return e0 / (e0 + e1)


def _tri_masks():
    ri = lax.broadcasted_iota(jnp.int32, (CHUNK, CHUNK), 0)
    ci = lax.broadcasted_iota(jnp.int32, (CHUNK, CHUNK), 1)
    return (ri >= ci).astype(F32), (ci >= ri).astype(F32)


def _cumsum_mm(tri, g):
    tb = tri.astype(BF16)
    hi = g.astype(BF16)
    r1 = g - hi.astype(F32)
    mid = r1.astype(BF16)
    lo = (r1 - mid.astype(F32)).astype(BF16)
    return _mm(tb, hi) + _mm(tb, mid) + _mm(tb, lo)


def _hg_gates(q_r, f_r, lb, tril):
    sq = _sig(q_r)
    q = q_r * sq
    sf = _sig(f_r)
    f = lb + (1.0 - lb) * sf
    k = 1.0 - f
    g = jnp.log(f)
    b = _cumsum_mm(tril, g)
    b_last = _rowsum(g)
    row = lax.broadcasted_iota(jnp.int32, g.shape, 0)
    ref = _rowsum(jnp.where(row < CHUNK // 2, g, 0.0))
    e = jnp.exp(b)
    eq = jnp.exp(jnp.minimum(b - ref, 80.0))
    ek = jnp.exp(jnp.minimum(ref - b, 80.0))
    dd = jnp.exp(b_last - b)
    return dict(sq=sq, q=q, sf=sf, f=f, k=k, e=e, eq=eq, ek=ek, dd=dd, elast=jnp.exp(b_last),
                qe=q * e, qt=q * eq, kt=k * ek, kd=k * dd)


def _hgrn_fwd_call(p, logits, gn):
    S = p.shape[0]
    ncb = TB // CHUNK

    def body(q_ref, f_ref, v_ref, og_ref, lg_ref, gn_ref, o_ref, oa_ref, st_ref, st_scr):
        @pl.when(pl.program_id(0) == 0)
        def _():
            st_scr[...] = jnp.zeros_like(st_scr)

        lb = _lower_bound(lg_ref)
        tril, _ = _tri_masks()

        def chunk(ci, carry):
            rows = pl.ds(pl.multiple_of(ci * CHUNK, CHUNK), CHUNK)
            st_ref[ci] = st_scr[...]
            t = _hg_gates(q_ref[rows, :], f_ref[rows, :], lb, tril)
            v = v_ref[rows, :]
            for h in range(HEADS):
                sl = slice(h * DK, (h + 1) * DK)
                stp = st_scr[:, sl]
                vb = v[:, sl].astype(BF16)
                inter = _mm(t["qe"][:, sl].astype(BF16), stp.astype(BF16), NT)
                a = jnp.where(tril > 0.5, _mm(t["qt"][:, sl].astype(BF16), t["kt"][:, sl].astype(BF16), NT), 0.0)
                o = inter + _mm(a.astype(BF16), vb)
                st_scr[:, sl] = stp * t["elast"][:, sl] + _mm(vb, t["kd"][:, sl].astype(BF16), TN)
                oh = o * lax.rsqrt(jnp.mean(o * o, axis=-1, keepdims=True) + EPS)
                og = og_ref[rows, sl]
                o_ref[rows, sl] = o
                oa_ref[rows, sl] = (oh * gn_ref[:, sl] * (og * _sig(og))).astype(BF16)
            return carry

        lax.fori_loop(0, ncb, chunk, 0)

    col = lambda j: BS((TB, D), lambda i, j=j: (i, j))
    return pl.pallas_call(
        body, name="hgrn_fwd", grid=(S // TB,),
        out_shape=(SDS((S, D), F32), SDS((S, D), BF16), SDS((S // CHUNK, DK, D), F32)),
        in_specs=[col(0), col(1), col(2), col(3), BS((2, D), lambda i: (0, 0)), BS((1, D), lambda i: (0, 0))],
        out_specs=(BS((TB, D), lambda i: (i, 0)), BS((TB, D), lambda i: (i, 0)),
                   BS((ncb, DK, D), lambda i: (i, 0, 0))),
        scratch_shapes=[pltpu.VMEM((DK, D), F32)],
        compiler_params=_params("arbitrary"),
    )(p, p, p, p, logits, gn)


def _layernorm_stats(uc):
    mu = jnp.mean(uc, axis=-1, keepdims=True)
    xc = uc - mu
    rs = lax.rsqrt(jnp.mean(xc * xc, axis=-1, keepdims=True) + EPS)
    return xc * rs, rs


EXT = HALO + TM + 8


def _fill_shifted(ext, shifted):
    for m in range(1, 8):
        shifted[m - 1] = ext[m:m + HALO + TM, :]


def _window(ext, shifted, s0, n):
    m = s0 % 8
    q = s0 - m
    return ext[q:q + n, :] if m == 0 else shifted[m - 1, q:q + n, :]


def _conv_fwd_call(p, dw, db, ln_g, ln_b):
    S = p.shape[0]

    def body(cv_ref, cg_ref, dw_ref, db_ref, g_ref, b_ref, u_ref, uc_ref, cb_ref, uext, ush):
        @pl.when(pl.program_id(0) == 0)
        def _():
            uext[0:HALO, :] = jnp.zeros((HALO, D), F32)
            uext[HALO + TM:EXT, :] = jnp.zeros((EXT - HALO - TM, D), F32)

        u = cv_ref[...] * _sig(cg_ref[...])
        uext[HALO:HALO + TM, :] = u
        u_ref[...] = u
        _fill_shifted(uext, ush)
        for rb in range(TM // SUB):
            acc = jnp.broadcast_to(db_ref[...], (SUB, D))
            for j in range(CONV_K):
                s0 = HALO - (CONV_K - 1) + j + rb * SUB
                acc = acc + dw_ref[j:j + 1, :] * _window(uext, ush, s0, SUB)
            uc_ref[rb * SUB:(rb + 1) * SUB, :] = acc
            xh, _ = _layernorm_stats(acc)
            ln = xh * g_ref[...] + b_ref[...]
            cb_ref[rb * SUB:(rb + 1) * SUB, :] = (ln * _sig(ln)).astype(BF16)
        uext[0:HALO, :] = uext[TM:TM + HALO, :]

    vec = BS((1, D), lambda i: (0, 0))
    return pl.pallas_call(
        body, name="conv_fwd", grid=(S // TM,),
        out_shape=(SDS((S, D), F32), SDS((S, D), F32), SDS((S, D), BF16)),
        in_specs=[BS((TM, D), lambda i: (i, 4)), BS((TM, D), lambda i: (i, 5)),
                  BS((CONV_K, D), lambda i: (0, 0)), vec, vec, vec],
        out_specs=(BS((TM, D), lambda i: (i, 0)),) * 3,
        scratch_shapes=[pltpu.VMEM((EXT, D), F32), pltpu.VMEM((7, HALO + TM, D), F32)],
        compiler_params=_params("arbitrary"),
    )(p, p, dw, db, ln_g, ln_b)


def _merge_fwd_call(oa, cb, p, x, mod, post_tm, pre_cm, w_a, w_b, w_o):
    S = x.shape[0]

    def body(oa_ref, cb_ref, ga_ref, gb_ref, x_ref, mod_ref, post_ref, pre_ref, wa_ref, wb_ref, wo_ref,
             ya_ref, yb_ref, mg_ref, y_ref, x2_ref, h2_ref):
        ya = _mm(oa_ref[...], wa_ref[...])
        yb = _mm(cb_ref[...], wb_ref[...])
        ya_ref[...] = ya
        yb_ref[...] = yb
        mg = (_sig(ga_ref[...]) * ya + _sig(gb_ref[...]) * yb).astype(BF16)
        mg_ref[...] = mg
        y = _mm(mg, wo_ref[...])
        y_ref[...] = y
        n = y * lax.rsqrt(jnp.mean(y * y, axis=-1, keepdims=True) + EPS) * post_ref[...]
        x2 = x_ref[...] + mod_ref[:, 2 * D:3 * D] * n
        x2_ref[...] = x2
        r2 = lax.rsqrt(jnp.mean(x2 * x2, axis=-1, keepdims=True) + EPS)
        h2 = x2 * r2 * pre_ref[...] * (1.0 + mod_ref[:, 4 * D:5 * D]) + mod_ref[:, 3 * D:4 * D]
        h2_ref[...] = h2.astype(BF16)

    tile = BS((TM, D), lambda i: (i, 0))
    vec = BS((1, D), lambda i: (0, 0))
    wsp = BS((D, D), lambda i: (0, 0))
    return pl.pallas_call(
        body, name="merge_fwd", grid=(S // TM,),
        out_shape=(SDS((S, D), F32), SDS((S, D), F32), SDS((S, D), BF16), SDS((S, D), F32), SDS((S, D), F32),
                   SDS((S, D), BF16)),
        in_specs=[tile, tile, BS((TM, D), lambda i: (i, 6)), BS((TM, D), lambda i: (i, 7)), tile,
                  BS((1, 6 * D), lambda i: (0, 0)), vec, vec, wsp, wsp, wsp],
        out_specs=(tile,) * 6,
        compiler_params=_params("arbitrary"),
    )(oa, cb, p, p, x, mod, post_tm, pre_cm, w_a, w_b, w_o)


def _ffn_call(h2, x2, target, mod, post_cm, pre_cm, w1_g, w2):
    S = x2.shape[0]

    def body(h2_ref, x2_ref, t_ref, mod_ref, post_ref, pre_ref, w1_hbm, w2_hbm,
             z_ref, da_ref, dy2_ref, dx2_ref, acc_ref, w1_v, w2_v, ra_scr, sems):
        @pl.when(pl.program_id(0) == 0)
        def _():
            c1 = pltpu.make_async_copy(w1_hbm, w1_v, sems.at[0])
            c2 = pltpu.make_async_copy(w2_hbm, w2_v, sems.at[1])
            c1.start()
            c2.start()
            c1.wait()
            c2.wait()
            acc_ref[...] = jnp.zeros_like(acc_ref)

        h2 = h2_ref[...]
        for k in range(N_CHIPS):
            ra = jnp.maximum(_mm(h2, w1_v[k]), 0.0)
            ra_scr[:, k * D:(k + 1) * D] = ra
            z_ref[:, k * D:(k + 1) * D] = (ra * ra).astype(BF16)
        y2 = _mm(z_ref[...], w2_v[...])
        ry = lax.rsqrt(jnp.mean(y2 * y2, axis=-1, keepdims=True) + EPS)
        yn = y2 * ry
        n = yn * post_ref[...]
        g2 = mod_ref[:, 5 * D:6 * D]
        x2 = x2_ref[...]
        err = x2 + g2 * n - t_ref[...]
        acc_ref[5:6, :] += _rowsum(err * err) * (0.5 / D)
        dout = err * (1.0 / D)
        acc_ref[0:1, :] += _rowsum(dout * n)
        dn = dout * g2
        acc_ref[1:2, :] += _rowsum(dn * yn)
        dyn = dn * post_ref[...]
        dy2 = (ry * (dyn - yn * jnp.mean(dyn * yn, axis=-1, keepdims=True))).astype(BF16)
        dy2_ref[...] = dy2
        dz = _mm(dy2, w2_v[...], NT)
        da_ref[...] = (dz * (2.0 * ra_scr[...])).astype(BF16)
        dh2 = jnp.zeros((TM, D), F32)
        for k in range(N_CHIPS):
            dh2 = dh2 + _mm(da_ref[:, k * D:(k + 1) * D], w1_v[k], NT)
        r2 = lax.rsqrt(jnp.mean(x2 * x2, axis=-1, keepdims=True) + EPS)
        xn = x2 * r2
        yv = xn * pre_ref[...]
        acc_ref[2:3, :] += _rowsum(dh2)
        acc_ref[3:4, :] += _rowsum(dh2 * yv)
        dyv = dh2 * (1.0 + mod_ref[:, 4 * D:5 * D])
        acc_ref[4:5, :] += _rowsum(dyv * xn)
        dxn = dyv * pre_ref[...]
        dx2_ref[...] = dout + r2 * (dxn - xn * jnp.mean(dxn * xn, axis=-1, keepdims=True))

    tile = BS((TM, D), lambda i: (i, 0))
    wide = BS((TM, D_FF), lambda i: (i, 0))
    vec = BS((1, D), lambda i: (0, 0))
    return pl.pallas_call(
        body, name="ffn_fwd_bwd", grid=(S // TM,),
        out_shape=(SDS((S, D_FF), BF16), SDS((S, D_FF), BF16), SDS((S, D), BF16), SDS((S, D), F32),
                   SDS((8, D), F32)),
        in_specs=[tile, tile, tile, BS((1, 6 * D), lambda i: (0, 0)), vec, vec,
                  BS(memory_space=pl.ANY), BS(memory_space=pl.ANY)],
        out_specs=(wide, wide, tile, tile, BS((8, D), lambda i: (0, 0))),
        scratch_shapes=[pltpu.VMEM(w1_g.shape, BF16), pltpu.VMEM(w2.shape, BF16), pltpu.VMEM((TM, D_FF), F32),
                        pltpu.SemaphoreType.DMA((2,))],
        compiler_params=_params("arbitrary"),
    )(h2, x2, target, mod, post_cm, pre_cm, w1_g, w2)


def _merge_bwd_call(dx2, y, ya, yb, p, mod, post_tm, w_a, w_b, w_o):
    S = y.shape[0]

    def body(dx2_ref, y_ref, ya_ref, yb_ref, ga_ref, gb_ref, mod_ref, post_ref, wa_ref, wb_ref, wo_ref,
             dy_ref, dya_ref, dyb_ref, doa_ref, dcb_ref, dpg_ref, acc_ref, bsum_ref):
        @pl.when(pl.program_id(0) == 0)
        def _():
            acc_ref[...] = jnp.zeros_like(acc_ref)
            bsum_ref[...] = jnp.zeros_like(bsum_ref)

        y = y_ref[...]
        ry = lax.rsqrt(jnp.mean(y * y, axis=-1, keepdims=True) + EPS)
        yn = y * ry
        dx2 = dx2_ref[...]
        acc_ref[0:1, :] += _rowsum(dx2 * (yn * post_ref[...]))
        dn = dx2 * mod_ref[:, 2 * D:3 * D]
        acc_ref[1:2, :] += _rowsum(dn * yn)
        dyn = dn * post_ref[...]
        dy = (ry * (dyn - yn * jnp.mean(dyn * yn, axis=-1, keepdims=True))).astype(BF16)
        dy_ref[...] = dy
        dmg = _mm(dy, wo_ref[...], NT)
        sa, sb = _sig(ga_ref[...]), _sig(gb_ref[...])
        dya = (dmg * sa).astype(BF16)
        dyb = (dmg * sb).astype(BF16)
        dya_ref[...] = dya
        dyb_ref[...] = dyb
        dga = dmg * ya_ref[...] * (sa * (1.0 - sa))
        dgb = dmg * yb_ref[...] * (sb * (1.0 - sb))
        dpg_ref[:, 0:D] = dga.astype(BF16)
        dpg_ref[:, D:2 * D] = dgb.astype(BF16)
        bsum_ref[:, 0:D] += _rowsum(dga)
        bsum_ref[:, D:2 * D] += _rowsum(dgb)
        doa_ref[...] = _mm(dya, wa_ref[...], NT)
        dcb_ref[...] = _mm(dyb, wb_ref[...], NT)

    tile = BS((TM, D), lambda i: (i, 0))
    vec = BS((1, D), lambda i: (0, 0))
    wsp = BS((D, D), lambda i: (0, 0))
    return pl.pallas_call(
        body, name="merge_bwd", grid=(S // TM,),
        out_shape=(SDS((S, D), BF16), SDS((S, D), BF16), SDS((S, D), BF16), SDS((S, D), F32), SDS((S, D), F32),
                   SDS((S, 2 * D), BF16), SDS((8, D), F32), SDS((1, 2 * D), F32)),
        in_specs=[tile, tile, tile, tile, BS((TM, D), lambda i: (i, 6)), BS((TM, D), lambda i: (i, 7)),
                  BS((1, 6 * D), lambda i: (0, 0)), vec, wsp, wsp, wsp],
        out_specs=(tile, tile, tile, tile, tile, BS((TM, 2 * D), lambda i: (i, 0)),
                   BS((8, D), lambda i: (0, 0)), BS((1, 2 * D), lambda i: (0, 0))),
        compiler_params=_params("arbitrary"),
    )(dx2, y, ya, yb, p, p, mod, post_tm, w_a, w_b, w_o)


def _hgrn_bwd_call(p, o, doa, st, logits, gn):
    S = p.shape[0]
    nb = S // TB
    ncb = TB // CHUNK

    def body(q_ref, f_ref, v_ref, og_ref, o_ref, doa_ref, st_ref, lg_ref, gn_ref,
             dp_ref, bsum_ref, dlg_ref, dgn_ref, dst_scr, dlb_scr, dqe_s, dqt_s, dkt_s, dkd_s, dv_s, dog_s, dble_s):
        i = pl.program_id(0)

        @pl.when(i == 0)
        def _():
            dst_scr[...] = jnp.zeros_like(dst_scr)
            dlb_scr[...] = jnp.zeros_like(dlb_scr)
            bsum_ref[...] = jnp.zeros_like(bsum_ref)
            dgn_ref[...] = jnp.zeros_like(dgn_ref)

        lb = _lower_bound(lg_ref)
        tril, triu = _tri_masks()

        def chunk(tt, carry):
            ci = ncb - 1 - tt
            rows = pl.ds(pl.multiple_of(ci * CHUNK, CHUNK), CHUNK)
            q_r, f_r = q_ref[rows, :], f_ref[rows, :]
            t = _hg_gates(q_r, f_r, lb, tril)
            v = v_ref[rows, :]
            for h in range(HEADS):
                sl = slice(h * DK, (h + 1) * DK)
                stp = st_ref[ci, :, sl]
                stb = stp.astype(BF16)
                qeb = t["qe"][:, sl].astype(BF16)
                qtb = t["qt"][:, sl].astype(BF16)
                ktb = t["kt"][:, sl].astype(BF16)
                kdb = t["kd"][:, sl].astype(BF16)
                vb = v[:, sl].astype(BF16)
                a = jnp.where(tril > 0.5, _mm(qtb, ktb, NT), 0.0)
                o_h = o_ref[rows, sl]
                rinv = lax.rsqrt(jnp.mean(o_h * o_h, axis=-1, keepdims=True) + EPS)
                oh = o_h * rinv
                og = og_ref[rows, sl]
                so = _sig(og)
                d_oa = doa_ref[rows, sl]
                don = d_oa * (og * so)
                dog_s[:, sl] = d_oa * (oh * gn_ref[:, sl]) * _dsilu(og, so)
                dgn_ref[:, sl] += _rowsum(don * oh)
                doh = don * gn_ref[:, sl]
                do = (rinv * (doh - oh * jnp.mean(doh * oh, axis=-1, keepdims=True))).astype(BF16)
                dqe_s[:, sl] = _mm(do, stb, NN)
                dstp = _mm(do, qeb, TN)
                dab = jnp.where(tril > 0.5, _mm(do, vb, NT), 0.0).astype(BF16)
                dqt_s[:, sl] = _mm(dab, ktb, NN)
                dkt_s[:, sl] = _mm(dab, qtb, TN)
                dstn = dst_scr[:, sl]
                dsb = dstn.astype(BF16)
                dkd_s[:, sl] = _mm(vb, dsb, NN)
                dv_s[:, sl] = _mm(a.astype(BF16), do, TN) + _mm(kdb, dsb, NT)
                el = t["elast"][:, sl]
                dst_scr[:, sl] = dstn * el + dstp
                dble_s[:, sl] = el * _rowsum(stp * dstn)
            dqe, dqt, dkt, dkd = dqe_s[...], dqt_s[...], dkt_s[...], dkd_s[...]
            dq = dqe * t["e"] + dqt * t["eq"]
            dk = dkt * t["ek"] + dkd * t["dd"]
            dkk = dkd * t["kd"]
            qt_r = t["qt"].astype(BF16).astype(F32)
            kt_r = t["kt"].astype(BF16).astype(F32)
            dbv = dqe * t["qe"] + dqt * qt_r - dkt * kt_r - dkk
            dg = _cumsum_mm(triu, dbv) + (_rowsum(dkk) + dble_s[...])
            df = dg / t["f"] - dk
            sf = t["sf"]
            dlb_scr[...] += _rowsum(df * (1.0 - sf))
            dqr = dq * _dsilu(q_r, t["sq"])
            dfr = df * (1.0 - lb) * (sf * (1.0 - sf))
            dvv, dog = dv_s[...], dog_s[...]
            dp_ref[rows, 0:D] = dqr.astype(BF16)
            dp_ref[rows, D:2 * D] = dfr.astype(BF16)
            dp_ref[rows, 2 * D:3 * D] = dvv.astype(BF16)
            dp_ref[rows, 3 * D:4 * D] = dog.astype(BF16)
            bsum_ref[:, 0:D] += _rowsum(dqr)
            bsum_ref[:, D:2 * D] += _rowsum(dfr)
            bsum_ref[:, 2 * D:3 * D] += _rowsum(dvv)
            bsum_ref[:, 3 * D:4 * D] += _rowsum(dog)
            return carry

        lax.fori_loop(0, ncb, chunk, 0)

        dl = dlb_scr[...] * lb * (1.0 - lb)
        dlg_ref[0:1, :] = dl
        dlg_ref[1:2, :] = -dl

    col = lambda j: BS((TB, D), lambda i, j=j: (nb - 1 - i, j))
    rev = BS((TB, D), lambda i: (nb - 1 - i, 0))
    cd = pltpu.VMEM((CHUNK, D), F32)
    return pl.pallas_call(
        body, name="hgrn_bwd", grid=(nb,),
        out_shape=(SDS((S, 4 * D), BF16), SDS((1, 4 * D), F32), SDS((2, D), F32), SDS((1, D), F32)),
        in_specs=[col(0), col(1), col(2), col(3), rev, rev, BS((ncb, DK, D), lambda i: (nb - 1 - i, 0, 0)),
                  BS((2, D), lambda i: (0, 0)), BS((1, D), lambda i: (0, 0))],
        out_specs=(BS((TB, 4 * D), lambda i: (nb - 1 - i, 0)), BS((1, 4 * D), lambda i: (0, 0)),
                   BS((2, D), lambda i: (0, 0)), BS((1, D), lambda i: (0, 0))),
        scratch_shapes=[pltpu.VMEM((DK, D), F32), pltpu.VMEM((1, D), F32), cd, cd, cd, cd, cd, cd,
                        pltpu.VMEM((1, D), F32)],
        compiler_params=_params("arbitrary"),
    )(p, p, p, p, o, doa, st, logits, gn)


def _conv_bwd_call(dcb, uc, u, p, dw, ln_g, ln_b):
    S = uc.shape[0]
    nb = S // TM
    hb = TM // HALO

    def body(dcb_ref, uc_ref, u_ref, uh_ref, cv_ref, cg_ref, dw_ref, g_ref, b_ref,
             dp_ref, bsum_ref, ddw_ref, acc_ref, uext, dext, ush, dsh):
        i = pl.program_id(0)

        @pl.when(i == 0)
        def _():
            dext[TM:EXT, :] = jnp.zeros((EXT - TM, D), F32)
            uext[HALO + TM:EXT, :] = jnp.zeros((EXT - HALO - TM, D), F32)
            bsum_ref[...] = jnp.zeros_like(bsum_ref)
            ddw_ref[...] = jnp.zeros_like(ddw_ref)
            acc_ref[...] = jnp.zeros_like(acc_ref)

        first_tile = (nb - 1 - i) == 0
        uext[0:HALO, :] = jnp.where(first_tile, 0.0, uh_ref[...])
        uext[HALO:HALO + TM, :] = u_ref[...]
        _fill_shifted(uext, ush)

        for rb in range(TM // SUB):
            rs_ = slice(rb * SUB, (rb + 1) * SUB)
            xh, rs = _layernorm_stats(uc_ref[rs_, :])
            ln = xh * g_ref[...] + b_ref[...]
            dln = dcb_ref[rs_, :] * _dsilu(ln, _sig(ln))
            acc_ref[1:2, :] += _rowsum(dln * xh)
            acc_ref[2:3, :] += _rowsum(dln)
            dxh = dln * g_ref[...]
            duc = rs * (dxh - jnp.mean(dxh, axis=-1, keepdims=True)
                        - xh * jnp.mean(dxh * xh, axis=-1, keepdims=True))
            dext[rs_, :] = duc
            acc_ref[0:1, :] += _rowsum(duc)
        _fill_shifted(dext, dsh)

        for j in range(CONV_K):
            part = jnp.zeros((SUB, D), F32)
            for rb in range(TM // SUB):
                s0 = HALO - (CONV_K - 1) + j + rb * SUB
                part = part + dext[rb * SUB:(rb + 1) * SUB, :] * _window(uext, ush, s0, SUB)
            ddw_ref[j:j + 1, :] += _rowsum(part)

        for rb in range(TM // SUB):
            rs_ = slice(rb * SUB, (rb + 1) * SUB)
            du = jnp.zeros((SUB, D), F32)
            for j in range(CONV_K):
                s0 = rb * SUB + (CONV_K - 1) - j
                du = du + dw_ref[j:j + 1, :] * _window(dext, dsh, s0, SUB)
            cg = cg_ref[rs_, :]
            sg = _sig(cg)
            dcv = du * sg
            dcg = du * cv_ref[rs_, :] * (sg * (1.0 - sg))
            dp_ref[rs_, 0:D] = dcv.astype(BF16)
            dp_ref[rs_, D:2 * D] = dcg.astype(BF16)
            bsum_ref[:, 0:D] += _rowsum(dcv)
            bsum_ref[:, D:2 * D] += _rowsum(dcg)

        dext[TM:TM + HALO, :] = dext[0:HALO, :]

    rev = BS((TM, D), lambda i: (nb - 1 - i, 0))
    vec = BS((1, D), lambda i: (0, 0))
    return pl.pallas_call(
        body, name="conv_bwd", grid=(nb,),
        out_shape=(SDS((S, 2 * D), BF16), SDS((1, 2 * D), F32), SDS((32, D), F32), SDS((8, D), F32)),
        in_specs=[rev, rev, rev, BS((HALO, D), lambda i: (jnp.maximum((nb - 1 - i) * hb - 1, 0), 0)),
                  BS((TM, D), lambda i: (nb - 1 - i, 4)), BS((TM, D), lambda i: (nb - 1 - i, 5)),
                  BS((CONV_K, D), lambda i: (0, 0)), vec, vec],
        out_specs=(BS((TM, 2 * D), lambda i: (nb - 1 - i, 0)), BS((1, 2 * D), lambda i: (0, 0)),
                   BS((32, D), lambda i: (0, 0)), BS((8, D), lambda i: (0, 0))),
        scratch_shapes=[pltpu.VMEM((EXT, D), F32), pltpu.VMEM((EXT, D), F32),
                        pltpu.VMEM((7, HALO + TM, D), F32), pltpu.VMEM((7, HALO + TM, D), F32)],
        compiler_params=_params("arbitrary"),
    )(dcb, uc, u, u, p, p, dw, ln_g, ln_b)


def _in_bwd_call(dp_hg, dp_cv, dp_gt, x, dx2, mod, pre_tm, w_in_g):
    S = x.shape[0]
    wc = w_in_g.shape[2]

    def body(hg_ref, cv_ref, gt_ref, x_ref, dx2_ref, mod_ref, g_ref, w_hbm, gx_ref, acc_ref, w_vmem, sem):
        @pl.when(pl.program_id(0) == 0)
        def _():
            cp = pltpu.make_async_copy(w_hbm, w_vmem, sem)
            cp.start()
            cp.wait()
            acc_ref[...] = jnp.zeros_like(acc_ref)

        dh = _mm(hg_ref[:, 0:wc], w_vmem[0], NT) + _mm(hg_ref[:, wc:2 * wc], w_vmem[1], NT)
        dh = dh + _mm(cv_ref[...], w_vmem[2], NT) + _mm(gt_ref[...], w_vmem[3], NT)
        xv = x_ref[...]
        r = lax.rsqrt(jnp.mean(xv * xv, axis=-1, keepdims=True) + EPS)
        xn = xv * r
        yv = xn * g_ref[...]
        acc_ref[0:1, :] += _rowsum(dh)
        acc_ref[1:2, :] += _rowsum(dh * yv)
        dyv = dh * (1.0 + mod_ref[:, D:2 * D])
        acc_ref[2:3, :] += _rowsum(dyv * xn)
        dxn = dyv * g_ref[...]
        gx_ref[...] = dx2_ref[...] + r * (dxn - xn * jnp.mean(dxn * xn, axis=-1, keepdims=True))

    tile = BS((TM, D), lambda i: (i, 0))
    return pl.pallas_call(
        body, name="in_bwd", grid=(S // TM,),
        out_shape=(SDS((S, D), F32), SDS((8, D), F32)),
        in_specs=[BS((TM, 4 * D), lambda i: (i, 0)), BS((TM, 2 * D), lambda i: (i, 0)),
                  BS((TM, 2 * D), lambda i: (i, 0)), tile, tile, BS((1, 6 * D), lambda i: (0, 0)),
                  BS((1, D), lambda i: (0, 0)), BS(memory_space=pl.ANY)],
        out_specs=(tile, BS((8, D), lambda i: (0, 0))),
        scratch_shapes=[pltpu.VMEM(w_in_g.shape, BF16), pltpu.SemaphoreType.DMA],
        compiler_params=_params("arbitrary"),
    )(dp_hg, dp_cv, dp_gt, x, dx2, mod, pre_tm, w_in_g)


def _wgrad_call(a, b, name, n_blocks=1):
    S, M = a.shape
    N = b.shape[1]
    bm, bn, bk = min(M, 1024), min(N // n_blocks, 1024), min(S, 1024)
    per = (N // n_blocks) // bn
    nk = S // bk

    def body(a_ref, b_ref, o_ref, acc):
        k = pl.program_id(2)

        @pl.when(k == 0)
        def _():
            acc[...] = jnp.zeros_like(acc)

        acc[...] += _mm(a_ref[...], b_ref[...], TN)

        @pl.when(k == nk - 1)
        def _():
            o_ref[...] = acc[...].astype(BF16)

    return pl.pallas_call(
        body, name=name, grid=(M // bm, N // bn, nk),
        out_shape=SDS((n_blocks, M, N // n_blocks), BF16),
        in_specs=[BS((bk, bm), lambda i, j, k: (k, i)), BS((bk, bn), lambda i, j, k: (k, j))],
        out_specs=BS((None, bm, bn), lambda i, j, k: (j // per, i, j % per)),
        scratch_shapes=[pltpu.VMEM((bm, bn), F32)],
        compiler_params=_params("parallel", "parallel", "arbitrary"),
    )(a, b)


def _outer_call(cact, dmod):
    n = dmod.shape[1]

    def body(a_ref, b_ref, o_ref):
        o_ref[...] = _mm(a_ref[...], b_ref[...], TN, HI)

    return pl.pallas_call(
        body, name="wgrad_ada", out_shape=SDS((D, n), F32),
        compiler_params=pltpu.CompilerParams(vmem_limit_bytes=VMEM_LIMIT),
    )(cact, dmod)


def _adamw_call(w, g, m, v, name):
    R, C = w.shape
    tr = R
    while tr * C > 512 * 1024 and tr % 16 == 0:
        tr //= 2
    c1 = 1.0 - ADAM_B1 ** ADAM_STEP
    c2 = 1.0 - ADAM_B2 ** ADAM_STEP

    def body(w_ref, g_ref, m_ref, v_ref, d_ref, m2_ref, v2_ref):
        g = g_ref[...]
        m2 = ADAM_B1 * m_ref[...] + (1.0 - ADAM_B1) * g
        v2 = ADAM_B2 * v_ref[...] + (1.0 - ADAM_B2) * (g * g)
        m2_ref[...] = m2
        v2_ref[...] = v2
        d_ref[...] = -ADAM_LR * ((m2 / c1) / (jnp.sqrt(v2 / c2) + ADAM_EPS) + ADAM_WD * w_ref[...])

    tile = BS((tr, C), lambda i: (i, 0))
    return pl.pallas_call(
        body, name=name, grid=(R // tr,), out_shape=(SDS((R, C), F32),) * 3,
        in_specs=[tile] * 4, out_specs=(tile,) * 3, compiler_params=_params("parallel"),
    )(w, g, m, v)


def _local_step(x, c, target, wts, small):
    mod8, cact = _mod_call(c, wts["ada"], small["b_ada"])
    mod = mod8[0:1]
    p, h1 = _fwd_in_call(x, mod, small["pre_tm"], wts["in"], small["b_in"])
    o, oa, st = _hgrn_fwd_call(p, small["logits"], small["hg_norm"])
    u, uc, cb = _conv_fwd_call(p, small["conv_dw"], small["conv_db"], small["ln_g"], small["ln_b"])
    ya, yb, mg, y, x2, h2 = _merge_fwd_call(oa, cb, p, x, mod, small["post_tm"], small["pre_cm"],
                                           wts["br_a"], wts["br_b"], wts["out"])
    z, da, dy2, dx2, acc_f = _ffn_call(h2, x2, target, mod, small["post_cm"], small["pre_cm"],
                                      wts["ff1"], wts["ff2"])
    dy, dya, dyb, doa, dcb, dp_gt, acc_m, bs_gt = _merge_bwd_call(
        dx2, y, ya, yb, p, mod, small["post_tm"], wts["br_a"], wts["br_b"], wts["out"])
    dp_hg, bs_hg, dlg, dgn = _hgrn_bwd_call(p, o, doa, st, small["logits"], small["hg_norm"])
    dp_cv, bs_cv, ddw, acc_c = _conv_bwd_call(dcb, uc, u, p, small["conv_dw"], small["ln_g"], small["ln_b"])
    gx, acc_i = _in_bwd_call(dp_hg, dp_cv, dp_gt, x, dx2, mod, small["pre_tm"], wts["in"])

    grads = {
        "in": jnp.concatenate([_wgrad_call(h1, dp_hg, "wgrad_in_hg", 2), _wgrad_call(h1, dp_cv, "wgrad_in_cv"),
                               _wgrad_call(h1, dp_gt, "wgrad_in_gt")], axis=0),
        "br_a": _wgrad_call(oa, dya, "wgrad_br_a")[0],
        "br_b": _wgrad_call(cb, dyb, "wgrad_br_b")[0],
        "out": _wgrad_call(mg, dy, "wgrad_out")[0],
        "ff1": _wgrad_call(h2, da, "wgrad_ff1", 4),
        "ff2": _wgrad_call(z, dy2, "wgrad_ff2")[0],
    }
    zrow = jnp.zeros((1, D), F32)
    rows = [acc_i[0:1], acc_i[1:2], acc_m[0:1], acc_f[2:3], acc_f[3:4], acc_f[0:1],
            acc_i[2:3], acc_m[1:2], acc_f[4:5], acc_f[1:2],
            jnp.concatenate([bs_hg, bs_cv, bs_gt], axis=1).reshape(8, D),
            dlg, dgn, acc_c[0:1], acc_c[1:2], acc_c[2:3],
            ddw,
            cact, acc_f[5:6]] + [zrow] * 6
    return gx, jnp.concatenate(rows, axis=0), grads


def kernel(x, c, w_ada, b_ada, pre_norm_tm, post_norm_tm, pre_norm_cm, post_norm_cm, w_in, b_in, hg_lb_logits, hg_norm, conv_dw, conv_db, conv_ln_g, conv_ln_b, w_br_a, w_br_b, w_out, w_ff1, w_ff2, loss_target, m_w_ada, m_b_ada, m_pre_norm_tm, m_post_norm_tm, m_pre_norm_cm, m_post_norm_cm, m_w_in, m_b_in, m_hg_lb_logits, m_hg_norm, m_conv_dw, m_conv_db, m_conv_ln_g, m_conv_ln_b, m_w_br_a, m_w_br_b, m_w_out, m_w_ff1, m_w_ff2, v_w_ada, v_b_ada, v_pre_norm_tm, v_post_norm_tm, v_pre_norm_cm, v_post_norm_cm, v_w_in, v_b_in, v_hg_lb_logits, v_hg_norm, v_conv_dw, v_conv_db, v_conv_ln_g, v_conv_ln_b, v_w_br_a, v_w_br_b, v_w_out, v_w_ff1, v_w_ff2):
    xi, yi, ci = lax.axis_index("x"), lax.axis_index("y"), lax.axis_index("c")
    chip = 2 * xi + yi
    c_idx = jnp.reshape(ci, (1,)).astype(jnp.int32)
    chip_idx = jnp.reshape(chip, (1,)).astype(jnp.int32)

    def pack_big(w_in_, br_a_, br_b_, out_, ff1_, ff2_):
        return jnp.concatenate([w_in_[0].reshape(R_IN, D), br_a_[0], br_b_[0], out_[0], ff1_[0], ff2_[0]], axis=0)

    def pack_small(ada_b, pre_t, post_t, pre_c, post_c, in_b, lg, hgn, cdb, lng, lnb, cdw):
        flat = jnp.concatenate([cdw[0].reshape(-1), jnp.zeros((8 * D - CONV_K * 256,), F32)]).reshape(8, D)
        return jnp.concatenate([ada_b.reshape(6, D), pre_t, post_t, pre_c, post_c, in_b.reshape(8, D), lg, hgn,
                                cdb, lng, lnb, flat], axis=0)

    pack = jnp.concatenate([pack_big(w_in, w_br_a, w_br_b, w_out, w_ff1, w_ff2), w_ada[0].reshape(R_ADA, D)],
                           axis=0).astype(BF16)
    half = lax.dynamic_slice_in_dim(pack, ci * (PACK_W // 2), PACK_W // 2, axis=0)
    wg = _allgather_call(half, "gather_weights", in_vmem=False, with_sum=False)[0].reshape(N_CHIPS, PACK_W, D)
    o1, o2, o3, o4, o5 = R_IN, R_IN + R_BR, R_IN + 2 * R_BR, R_IN + 3 * R_BR, R_IN + 3 * R_BR + R_FF
    wts = {
        "in": wg[:, 0:o1].reshape(N_CHIPS, D, IN_COLS // N_CHIPS),
        "br_a": wg[:, o1:o2].reshape(D, D),
        "br_b": wg[:, o2:o3].reshape(D, D),
        "out": wg[:, o3:o4].reshape(D, D),
        "ff1": wg[:, o4:o5],
        "ff2": wg[:, o5:PACK_G].reshape(D_FF, D),
        "ada": wg[:, PACK_G:PACK_W].reshape(N_CHIPS, D, 6 * D // N_CHIPS),
    }
    dw_blk = jnp.concatenate([conv_dw[0].reshape(-1), jnp.zeros((8 * D - CONV_K * 256,), F32)]).reshape(8, D)
    dw_all = _allgather_call(dw_blk, "gather_conv_dw", in_vmem=True, with_sum=False)[0]
    dw_all = dw_all.reshape(N_CHIPS, 2, 8 * D)[:, 0, :CONV_K * 256].reshape(N_CHIPS, CONV_K, 256)
    dw_full = dw_all.transpose(1, 0, 2).reshape(CONV_K, D)

    small = dict(b_ada=b_ada, pre_tm=pre_norm_tm, post_tm=post_norm_tm, pre_cm=pre_norm_cm, post_cm=post_norm_cm,
                 b_in=b_in, logits=hg_lb_logits, hg_norm=hg_norm, conv_dw=dw_full, conv_db=conv_db,
                 ln_g=conv_ln_g, ln_b=conv_ln_b)

    gx, srows, grads = _local_step(x[0], c, loss_target[0], wts, small)

    sall, ssum = _allgather_call(srows, "gather_small", in_vmem=True, with_sum=True)
    sall = sall.reshape(N_DEV, SMALL_ROWS, D)
    loss = jnp.sum(ssum[57])
    dmod_all = sall[:, 0:6, :].reshape(N_DEV, 6 * D)
    wa = 6 * D // N_CHIPS
    g_ada = _outer_call(sall[:, 56, :], lax.dynamic_slice_in_dim(dmod_all, chip * wa, wa, axis=1))
    g_dw = lax.dynamic_slice_in_dim(ssum[24:24 + CONV_K], chip * 256, 256, axis=1)
    g_small = jnp.concatenate(
        [ssum[0:24], jnp.concatenate([g_dw.reshape(-1), jnp.zeros((8 * D - CONV_K * 256,), F32)]).reshape(8, D)],
        axis=0)

    gp = jnp.concatenate([grads["in"].reshape(N_CHIPS, R_IN, D), grads["br_a"].reshape(N_CHIPS, R_BR, D),
                          grads["br_b"].reshape(N_CHIPS, R_BR, D), grads["out"].reshape(N_CHIPS, R_BR, D),
                          grads["ff1"], grads["ff2"].reshape(N_CHIPS, R_FF, D)], axis=1)
    gp = gp.reshape(N_CHIPS, 2, PACK_G // 2, D)
    part = _add_halves_call(gp, _sibling_halves_call(gp), c_idx)
    red = _add_chips_call(part, _chip_exchange_call(part), chip_idx)
    other = _sibling_join_call(red)
    g_big = jnp.where(ci == 0, jnp.concatenate([red, other], axis=0), jnp.concatenate([other, red], axis=0))

    shapes = {"in": w_in.shape, "br_a": w_br_a.shape, "br_b": w_br_b.shape, "out": w_out.shape,
              "ff1": w_ff1.shape, "ff2": w_ff2.shape}
    offs = {"in": (0, o1), "br_a": (o1, o2), "br_b": (o2, o3), "out": (o3, o4), "ff1": (o4, o5), "ff2": (o5, PACK_G)}
    wmv = {"in": (w_in, m_w_in, v_w_in), "br_a": (w_br_a, m_w_br_a, v_w_br_a), "br_b": (w_br_b, m_w_br_b, v_w_br_b),
           "out": (w_out, m_w_out, v_w_out), "ff1": (w_ff1, m_w_ff1, v_w_ff1), "ff2": (w_ff2, m_w_ff2, v_w_ff2)}
    res = {}
    for n in offs:
        shp = shapes[n]
        g2d = g_big[offs[n][0]:offs[n][1]].reshape(shp[1], shp[2])
        w_, m_, v_ = (a[0] for a in wmv[n])
        d_, m2_, v2_ = _adamw_call(w_, g2d, m_, v_, "adamw_" + n)
        res[n] = tuple(a.reshape(shp) for a in (g2d, d_, m2_, v2_))
    d_, m2_, v2_ = _adamw_call(w_ada[0], g_ada, m_w_ada[0], v_w_ada[0], "adamw_ada")
    res["ada"] = tuple(a.reshape(w_ada.shape) for a in (g_ada, d_, m2_, v2_))

    ws = pack_small(b_ada, pre_norm_tm, post_norm_tm, pre_norm_cm, post_norm_cm, b_in, hg_lb_logits, hg_norm,
                    conv_db, conv_ln_g, conv_ln_b, conv_dw)
    ms = pack_small(m_b_ada, m_pre_norm_tm, m_post_norm_tm, m_pre_norm_cm, m_post_norm_cm, m_b_in, m_hg_lb_logits,
                    m_hg_norm, m_conv_db, m_conv_ln_g, m_conv_ln_b, m_conv_dw)
    vs = pack_small(v_b_ada, v_pre_norm_tm, v_post_norm_tm, v_pre_norm_cm, v_post_norm_cm, v_b_in, v_hg_lb_logits,
                    v_hg_norm, v_conv_db, v_conv_ln_g, v_conv_ln_b, v_conv_dw)
    sres = (g_small,) + tuple(_adamw_call(ws, g_small, ms, vs, "adamw_small"))

    def unpack_small(t):
        return {"b_ada": t[0:6].reshape(1, 6 * D), "pre_tm": t[6:7], "post_tm": t[7:8], "pre_cm": t[8:9],
                "post_cm": t[9:10], "b_in": t[10:18].reshape(1, IN_COLS), "logits": t[18:20], "hg_norm": t[20:21],
                "conv_db": t[21:22], "ln_g": t[22:23], "ln_b": t[23:24],
                "conv_dw": t[24:32].reshape(-1)[:CONV_K * 256].reshape(1, CONV_K, 256)}

    order = ["ada", "b_ada", "pre_tm", "post_tm", "pre_cm", "post_cm", "in", "b_in", "logits", "hg_norm", "conv_dw",
             "conv_db", "ln_g", "ln_b", "br_a", "br_b", "out", "ff1", "ff2"]
    outs = [loss, gx.reshape(x.shape)]
    for kind in range(4):
        sm = unpack_small(sres[kind])
        for n in order:
            outs.append(res[n][kind] if n in res else sm[n])
    return tuple(outs)
```

```python
import functools

import jax
import jax.numpy as jnp
from jax import lax
from jax.experimental import pallas as pl
from jax.experimental.pallas import tpu as pltpu

F32, BF16 = jnp.float32, jnp.bfloat16
SDS = jax.ShapeDtypeStruct
BS = pl.BlockSpec
MESH = pl.DeviceIdType.MESH
HI = lax.Precision.HIGHEST

D = 1024
D_FF = 4096
IN_COLS = 8192
HEADS, DK = 8, 128
CHUNK = 128
CONV_K = 31
HALO = 32
SUB = 32
EPS = 1e-6
N_CHIPS, N_DEV = 4, 8
TM = 256
TB = 256
VMEM_LIMIT = 56 * 1024 * 1024

R_IN, R_BR, R_FF, R_ADA = 2048, 256, 1024, 1536
PACK_W = R_IN + 3 * R_BR + 2 * R_FF + R_ADA
PACK_G = R_IN + 3 * R_BR + 2 * R_FF
O_IN, O_FF1, O_FF2, O_BRA, O_BRB, O_OUT, O_ADA = 0, 2048, 3072, 4096, 4352, 4608, 4864
SMALL_ROWS = 64

ADAM_LR, ADAM_B1, ADAM_B2, ADAM_EPS, ADAM_WD, ADAM_STEP = 0.001, 0.9, 0.999, 1e-08, 0.01, 10

NN = (((1,), (0,)), ((), ()))
NT = (((1,), (1,)), ((), ()))
TN = (((0,), (0,)), ((), ()))


def _mm(a, b, dims=NN, precision=None):
    return lax.dot_general(a, b, dims, preferred_element_type=F32, precision=precision)


def _sig(v):
    return jax.nn.sigmoid(v)


def _dsilu(v, s):
    return s * (1.0 + v * (1.0 - s))


def _params(*sem):
    return pltpu.CompilerParams(dimension_semantics=sem if sem else None, vmem_limit_bytes=VMEM_LIMIT)


def _rowsum(v):
    return jnp.sum(v, axis=0, keepdims=True)


def _mesh_pos():
    return lax.axis_index("x"), lax.axis_index("y"), lax.axis_index("c")


def _allgather_call(blk, name, in_vmem, with_sum):
    m_per, n = blk.shape

    def body(x_ref, out_ref, *rest):
        if with_sum:
            sum_ref, send_sems, recv_sems, local_sem = rest
        else:
            send_sems, recv_sems, local_sem = rest
        x, y, c = _mesh_pos()
        me, sibling = (x, y, c), (x, y, 1 - c)
        chips = [(1 - x, y), (x, 1 - y), (1 - x, 1 - y)]

        def rows(px, py, pc):
            return out_ref.at[pl.ds((4 * px + 2 * py + pc) * m_per, m_per), :]

        def copy(k, block, to, src=None):
            return pltpu.make_async_remote_copy(
                src_ref=rows(*block) if src is None else src, dst_ref=rows(*block),
                send_sem=send_sems.at[k], recv_sem=recv_sems.at[k], device_id=to, device_id_type=MESH)

        mine = pltpu.make_async_copy(x_ref, rows(*me), local_sem)
        mine.start()
        first = [copy(0, me, sibling, src=x_ref)]
        first += [copy(1 + j, me, (*chip, c), src=x_ref) for j, chip in enumerate(chips)]
        for cp in first:
            cp.start()
        passed = [copy(4 + j, (*chip, c), sibling) for j, chip in enumerate(chips)]
        for j, chip in enumerate(chips):
            copy(1 + j, (*chip, c), me).wait_recv()
            passed[j].start()
        copy(0, sibling, me).wait_recv()
        for j, chip in enumerate(chips):
            copy(4 + j, (*chip, 1 - c), me).wait_recv()
        for cp in first + passed:
            cp.wait_send()
        mine.wait()
        if with_sum:
            acc = out_ref[0:m_per, :]
            for d in range(1, N_DEV):
                acc = acc + out_ref[d * m_per:(d + 1) * m_per, :]
            sum_ref[...] = acc

    space = pltpu.VMEM if in_vmem else pl.ANY
    out_shape = [SDS((N_DEV * m_per, n), blk.dtype)]
    out_specs = [BS(memory_space=space)]
    if with_sum:
        out_shape.append(SDS((m_per, n), blk.dtype))
        out_specs.append(BS(memory_space=pltpu.VMEM))
    return pl.pallas_call(
        body, name=name, out_shape=out_shape, in_specs=[BS(memory_space=space)], out_specs=out_specs,
        scratch_shapes=[pltpu.SemaphoreType.DMA((7,)), pltpu.SemaphoreType.DMA((7,)), pltpu.SemaphoreType.DMA],
        compiler_params=pltpu.CompilerParams(vmem_limit_bytes=VMEM_LIMIT),
    )(blk)


def _sibling_halves_call(g):
    _, _, h, n = g.shape

    def body(g_ref, out_ref, send_sems, recv_sems):
        x, y, c = _mesh_pos()
        cps = [pltpu.make_async_remote_copy(
            src_ref=g_ref.at[k, 1 - c], dst_ref=out_ref.at[k], send_sem=send_sems.at[k], recv_sem=recv_sems.at[k],
            device_id=(x, y, 1 - c), device_id_type=MESH) for k in range(N_CHIPS)]
        for cp in cps:
            cp.start()
        for cp in cps:
            cp.wait()

    return pl.pallas_call(
        body, name="rs_sibling_halves", out_shape=SDS((N_CHIPS, h, n), g.dtype),
        in_specs=[BS(memory_space=pl.ANY)], out_specs=BS(memory_space=pl.ANY),
        scratch_shapes=[pltpu.SemaphoreType.DMA((N_CHIPS,)), pltpu.SemaphoreType.DMA((N_CHIPS,))],
    )(g)


def _chip_exchange_call(p):
    _, h, n = p.shape

    def body(p_ref, out_ref, send_sems, recv_sems):
        x, y, c = _mesh_pos()
        chips = [(1 - x, y), (x, 1 - y), (1 - x, 1 - y)]
        cps = [pltpu.make_async_remote_copy(
            src_ref=p_ref.at[2 * cx + cy], dst_ref=out_ref.at[j], send_sem=send_sems.at[j], recv_sem=recv_sems.at[j],
            device_id=(cx, cy, c), device_id_type=MESH) for j, (cx, cy) in enumerate(chips)]
        for cp in cps:
            cp.start()
        for cp in cps:
            cp.wait()

    return pl.pallas_call(
        body, name="rs_chip_exchange", out_shape=SDS((3, h, n), p.dtype),
        in_specs=[BS(memory_space=pl.ANY)], out_specs=BS(memory_space=pl.ANY),
        scratch_shapes=[pltpu.SemaphoreType.DMA((3,)), pltpu.SemaphoreType.DMA((3,))],
    )(p)


def _sibling_join_call(r):
    h, n = r.shape
    q = h // 4

    def body(r_ref, out_ref, send_sems, recv_sems):
        x, y, c = _mesh_pos()
        cps = [pltpu.make_async_remote_copy(
            src_ref=r_ref.at[pl.ds(k * q, q)], dst_ref=out_ref.at[pl.ds(k * q, q)],
            send_sem=send_sems.at[k], recv_sem=recv_sems.at[k],
            device_id=(x, y, 1 - c), device_id_type=MESH) for k in range(4)]
        for cp in cps:
            cp.start()
        for cp in cps:
            cp.wait()

    return pl.pallas_call(
        body, name="rs_sibling_join", out_shape=SDS((h, n), r.dtype),
        in_specs=[BS(memory_space=pl.ANY)], out_specs=BS(memory_space=pl.ANY),
        scratch_shapes=[pltpu.SemaphoreType.DMA((4,)), pltpu.SemaphoreType.DMA((4,))],
    )(r)


def _add_halves_call(g, recv, c_idx):
    _, _, h, n = g.shape
    tr = h // 8

    def body(c_ref, g_ref, r_ref, o_ref):
        o_ref[...] = (g_ref[...].astype(F32) + r_ref[...].astype(F32)).astype(BF16)

    return pl.pallas_call(
        body, name="rs_add_halves", out_shape=SDS((N_CHIPS, h, n), BF16),
        grid_spec=pltpu.PrefetchScalarGridSpec(
            num_scalar_prefetch=1, grid=(N_CHIPS, 8),
            in_specs=[BS((None, None, tr, n), lambda k, r, c_ref: (k, c_ref[0], r, 0)),
                      BS((None, tr, n), lambda k, r, c_ref: (k, r, 0))],
            out_specs=BS((None, tr, n), lambda k, r, c_ref: (k, r, 0))),
        compiler_params=_params("arbitrary", "arbitrary"),
    )(c_idx, g, recv)


def _add_chips_call(p, recv, chip_idx):
    _, h, n = p.shape
    tr = h // 8

    def body(k_ref, p_ref, r_ref, o_ref):
        acc = p_ref[...].astype(F32)
        for j in range(3):
            acc = acc + r_ref[j].astype(F32)
        o_ref[...] = acc

    return pl.pallas_call(
        body, name="rs_add_chips", out_shape=SDS((h, n), F32),
        grid_spec=pltpu.PrefetchScalarGridSpec(
            num_scalar_prefetch=1, grid=(8,),
            in_specs=[BS((None, tr, n), lambda r, k_ref: (k_ref[0], r, 0)),
                      BS((3, tr, n), lambda r, k_ref: (0, r, 0))],
            out_specs=BS((tr, n), lambda r, k_ref: (r, 0))),
        compiler_params=_params("arbitrary"),
    )(chip_idx, p, recv)


def _load_rows(wg_hbm, w_vmem, sem, off):
    cp = pltpu.make_async_copy(wg_hbm.at[:, pl.ds(off, w_vmem.shape[1]), :], w_vmem, sem)
    cp.start()
    return cp


def _mod_call(c, wg, b_ada):
    wc = R_ADA

    def body(c_ref, w_hbm, b_ref, mod_ref, cact_ref, w_vmem, sem):
        cp = _load_rows(w_hbm, w_vmem, sem, O_ADA)
        cv = c_ref[...]
        ca = cv * _sig(cv)
        cact_ref[...] = ca
        cb = jnp.broadcast_to(ca, (8, D)).astype(BF16)
        cp.wait()
        for k in range(N_CHIPS):
            mod_ref[:, k * wc:(k + 1) * wc] = _mm(cb, w_vmem[k], NT) + b_ref[:, k * wc:(k + 1) * wc]

    vm = BS(memory_space=pltpu.VMEM)
    return pl.pallas_call(
        body, name="adaln_mod", out_shape=(SDS((8, 6 * D), F32), SDS((1, D), F32)),
        in_specs=[vm, BS(memory_space=pl.ANY), vm], out_specs=(vm, vm),
        scratch_shapes=[pltpu.VMEM((N_CHIPS, R_ADA, D), BF16), pltpu.SemaphoreType.DMA],
        compiler_params=pltpu.CompilerParams(vmem_limit_bytes=VMEM_LIMIT),
    )(c, wg, b_ada)


def _fwd_in_call(x, mod, pre_tm, wg, b_in):
    S = x.shape[0]

    def body(x_ref, mod_ref, g_ref, w_hbm, b_ref, p_ref, h_ref, w_vmem, sem):
        @pl.when(pl.program_id(0) == 0)
        def _():
            _load_rows(w_hbm, w_vmem, sem, O_IN).wait()

        xv = x_ref[...]
        r = lax.rsqrt(jnp.mean(xv * xv, axis=-1, keepdims=True) + EPS)
        h = xv * r * g_ref[...] * (1.0 + mod_ref[:, D:2 * D]) + mod_ref[:, 0:D]
        hb = h.astype(BF16)
        h_ref[...] = hb
        for k in range(IN_COLS // D):
            w_blk = w_vmem[k // 2, (k % 2) * D:(k % 2 + 1) * D, :]
            p_ref[:, k * D:(k + 1) * D] = _mm(hb, w_blk) + b_ref[:, k * D:(k + 1) * D]

    return pl.pallas_call(
        body, name="fwd_in", grid=(S // TM,),
        out_shape=(SDS((S, IN_COLS), F32), SDS((S, D), BF16)),
        in_specs=[BS((TM, D), lambda i: (i, 0)), BS((1, 6 * D), lambda i: (0, 0)), BS((1, D), lambda i: (0, 0)),
                  BS(memory_space=pl.ANY), BS((1, IN_COLS), lambda i: (0, 0))],
        out_specs=(BS((TM, IN_COLS), lambda i: (i, 0)), BS((TM, D), lambda i: (i, 0))),
        scratch_shapes=[pltpu.VMEM((N_CHIPS, R_IN, D), BF16), pltpu.SemaphoreType.DMA],
        compiler_params=_params("arbitrary"),
    )(x, mod, pre_tm, wg, b_in)


def _lower_bound(lg_ref):
    l0, l1 = lg_ref[0:1, :], lg_ref[1:2, :]
    mx = jnp.maximum(l0, l1)
    e0, e1 = jnp.exp(l0 - mx), jnp.exp(l1 - mx)
    return e0 / (e0 + e1)


def _tri_masks():
    ri = lax.broadcasted_iota(jnp.int32, (CHUNK, CHUNK), 0)
    ci = lax.broadcasted_iota(jnp.int32, (CHUNK, CHUNK), 1)
    return (ri >= ci).astype(F32), (ci >= ri).astype(F32)


def _cumsum_mm(tri, g):
    tb = tri.astype(BF16)
    hi = g.astype(BF16)
    r1 = g - hi.astype(F32)
    mid = r1.astype(BF16)
    lo = (r1 - mid.astype(F32)).astype(BF16)
    return _mm(tb, hi) + _mm(tb, mid) + _mm(tb, lo)


def _hg_gates(q_r, f_r, lb, tril):
    sq = _sig(q_r)
    q = q_r * sq
    sf = _sig(f_r)
    f = lb + (1.0 - lb) * sf
    k = 1.0 - f
    g = jnp.log(f)
    b = _cumsum_mm(tril, g)
    b_last = _rowsum(g)
    row = lax.broadcasted_iota(jnp.int32, g.shape, 0)
    ref = _rowsum(jnp.where(row < CHUNK // 2, g, 0.0))
    e = jnp.exp(b)
    eq = jnp.exp(jnp.minimum(b - ref, 80.0))
    ek = jnp.exp(jnp.minimum(ref - b, 80.0))
    dd = jnp.exp(b_last - b)
    return dict(sq=sq, q=q, sf=sf, f=f, k=k, e=e, eq=eq, ek=ek, dd=dd, elast=jnp.exp(b_last),
                qe=q * e, qt=q * eq, kt=k * ek, kd=k * dd)


def _hgrn_fwd_call(p, logits, gn):
    S = p.shape[0]
    ncb = TB // CHUNK

    def body(q_ref, f_ref, v_ref, og_ref, lg_ref, gn_ref, o_ref, oa_ref, st_ref, st_scr):
        @pl.when(pl.program_id(0) == 0)
        def _():
            st_scr[...] = jnp.zeros_like(st_scr)

        lb = _lower_bound(lg_ref)
        tril, _ = _tri_masks()

        def chunk(ci, carry):
            rows = pl.ds(pl.multiple_of(ci * CHUNK, CHUNK), CHUNK)
            st_ref[ci] = st_scr[...]
            t = _hg_gates(q_ref[rows, :], f_ref[rows, :], lb, tril)
            v = v_ref[rows, :]
            for h in range(HEADS):
                sl = slice(h * DK, (h + 1) * DK)
                stp = st_scr[:, sl]
                vb = v[:, sl].astype(BF16)
                inter = _mm(t["qe"][:, sl].astype(BF16), stp.astype(BF16), NT)
                a = jnp.where(tril > 0.5, _mm(t["qt"][:, sl].astype(BF16), t["kt"][:, sl].astype(BF16), NT), 0.0)
                o = inter + _mm(a.astype(BF16), vb)
                st_scr[:, sl] = stp * t["elast"][:, sl] + _mm(vb, t["kd"][:, sl].astype(BF16), TN)
                oh = o * lax.rsqrt(jnp.mean(o * o, axis=-1, keepdims=True) + EPS)
                og = og_ref[rows, sl]
                o_ref[rows, sl] = o
                oa_ref[rows, sl] = (oh * gn_ref[:, sl] * (og * _sig(og))).astype(BF16)
            return carry

        lax.fori_loop(0, ncb, chunk, 0)

    col = lambda j: BS((TB, D), lambda i, j=j: (i, j))
    return pl.pallas_call(
        body, name="hgrn_fwd", grid=(S // TB,),
        out_shape=(SDS((S, D), F32), SDS((S, D), BF16), SDS((S // CHUNK, DK, D), F32)),
        in_specs=[col(0), col(1), col(2), col(3), BS((2, D), lambda i: (0, 0)), BS((1, D), lambda i: (0, 0))],
        out_specs=(BS((TB, D), lambda i: (i, 0)), BS((TB, D), lambda i: (i, 0)),
                   BS((ncb, DK, D), lambda i: (i, 0, 0))),
        scratch_shapes=[pltpu.VMEM((DK, D), F32)],
        compiler_params=_params("arbitrary"),
    )(p, p, p, p, logits, gn)


def _layernorm_stats(uc):
    mu = jnp.mean(uc, axis=-1, keepdims=True)
    xc = uc - mu
    rs = lax.rsqrt(jnp.mean(xc * xc, axis=-1, keepdims=True) + EPS)
    return xc * rs, rs


EXT = HALO + TM + 8


def _fill_shifted(ext, shifted):
    for m in range(1, 8):
        shifted[m - 1] = ext[m:m + HALO + TM, :]


def _window(ext, shifted, s0, n):
    m = s0 % 8
    q = s0 - m
    return ext[q:q + n, :] if m == 0 else shifted[m - 1, q:q + n, :]


def _conv_fwd_call(p, dw, db, ln_g, ln_b):
    S = p.shape[0]

    def body(cv_ref, cg_ref, dw_ref, db_ref, g_ref, b_ref, u_ref, uc_ref, cb_ref, uext, ush):
        @pl.when(pl.program_id(0) == 0)
        def _():
            uext[0:HALO, :] = jnp.zeros((HALO, D), F32)
            uext[HALO + TM:EXT, :] = jnp.zeros((EXT - HALO - TM, D), F32)

        u = cv_ref[...] * _sig(cg_ref[...])
        uext[HALO:HALO + TM, :] = u
        u_ref[...] = u
        _fill_shifted(uext, ush)
        for rb in range(TM // SUB):
            acc = jnp.broadcast_to(db_ref[...], (SUB, D))
            for j in range(CONV_K):
                s0 = HALO - (CONV_K - 1) + j + rb * SUB
                acc = acc + dw_ref[j:j + 1, :] * _window(uext, ush, s0, SUB)
            uc_ref[rb * SUB:(rb + 1) * SUB, :] = acc
            xh, _ = _layernorm_stats(acc)
            ln = xh * g_ref[...] + b_ref[...]
            cb_ref[rb * SUB:(rb + 1) * SUB, :] = (ln * _sig(ln)).astype(BF16)
        uext[0:HALO, :] = uext[TM:TM + HALO, :]

    vec = BS((1, D), lambda i: (0, 0))
    return pl.pallas_call(
        body, name="conv_fwd", grid=(S // TM,),
        out_shape=(SDS((S, D), F32), SDS((S, D), F32), SDS((S, D), BF16)),
        in_specs=[BS((TM, D), lambda i: (i, 4)), BS((TM, D), lambda i: (i, 5)),
                  BS((CONV_K, D), lambda i: (0, 0)), vec, vec, vec],
        out_specs=(BS((TM, D), lambda i: (i, 0)),) * 3,
        scratch_shapes=[pltpu.VMEM((EXT, D), F32), pltpu.VMEM((7, HALO + TM, D), F32)],
        compiler_params=_params("arbitrary"),
    )(p, p, dw, db, ln_g, ln_b)


def _mm_rows(a, w_ref):
    acc = _mm(a[:, 0:R_BR], w_ref[0])
    for k in range(1, N_CHIPS):
        acc = acc + _mm(a[:, k * R_BR:(k + 1) * R_BR], w_ref[k])
    return acc


def _mm_rows_t(a, w_ref):
    return jnp.concatenate([_mm(a, w_ref[k], NT) for k in range(N_CHIPS)], axis=1)


def _br_spec(off):
    return BS((N_CHIPS, R_BR, D), lambda i: (0, off // R_BR, 0))


def _merge_fwd_call(oa, cb, p, x, mod, post_tm, pre_cm, wg):
    S = x.shape[0]

    def body(oa_ref, cb_ref, ga_ref, gb_ref, x_ref, mod_ref, post_ref, pre_ref, wa_ref, wb_ref, wo_ref,
             ya_ref, yb_ref, mg_ref, y_ref, x2_ref, h2_ref):
        ya = _mm_rows(oa_ref[...], wa_ref)
        yb = _mm_rows(cb_ref[...], wb_ref)
        ya_ref[...] = ya
        yb_ref[...] = yb
        mg = (_sig(ga_ref[...]) * ya + _sig(gb_ref[...]) * yb).astype(BF16)
        mg_ref[...] = mg
        y = _mm_rows(mg, wo_ref)
        y_ref[...] = y
        n = y * lax.rsqrt(jnp.mean(y * y, axis=-1, keepdims=True) + EPS) * post_ref[...]
        x2 = x_ref[...] + mod_ref[:, 2 * D:3 * D] * n
        x2_ref[...] = x2
        r2 = lax.rsqrt(jnp.mean(x2 * x2, axis=-1, keepdims=True) + EPS)
        h2 = x2 * r2 * pre_ref[...] * (1.0 + mod_ref[:, 4 * D:5 * D]) + mod_ref[:, 3 * D:4 * D]
        h2_ref[...] = h2.astype(BF16)

    tile = BS((TM, D), lambda i: (i, 0))
    vec = BS((1, D), lambda i: (0, 0))
    return pl.pallas_call(
        body, name="merge_fwd", grid=(S // TM,),
        out_shape=(SDS((S, D), F32), SDS((S, D), F32), SDS((S, D), BF16), SDS((S, D), F32), SDS((S, D), F32),
                   SDS((S, D), BF16)),
        in_specs=[tile, tile, BS((TM, D), lambda i: (i, 6)), BS((TM, D), lambda i: (i, 7)), tile,
                  BS((1, 6 * D), lambda i: (0, 0)), vec, vec, _br_spec(O_BRA), _br_spec(O_BRB), _br_spec(O_OUT)],
        out_specs=(tile,) * 6,
        compiler_params=_params("arbitrary"),
    )(oa, cb, p, p, x, mod, post_tm, pre_cm, wg, wg, wg)


def _ffn_call(h2, x2, target, mod, post_cm, pre_cm, wg):
    S = x2.shape[0]

    def body(h2_ref, x2_ref, t_ref, mod_ref, post_ref, pre_ref, w_hbm,
             z_ref, da_ref, dy2_ref, dx2_ref, acc_ref, w1_v, w2_v, ra_scr, sems):
        @pl.when(pl.program_id(0) == 0)
        def _():
            c1 = _load_rows(w_hbm, w1_v, sems.at[0], O_FF1)
            c2 = _load_rows(w_hbm, w2_v, sems.at[1], O_FF2)
            c1.wait()
            c2.wait()
            acc_ref[...] = jnp.zeros_like(acc_ref)

        h2 = h2_ref[...]
        for k in range(N_CHIPS):
            ra = jnp.maximum(_mm(h2, w1_v[k]), 0.0)
            ra_scr[:, k * D:(k + 1) * D] = ra
            z_ref[:, k * D:(k + 1) * D] = (ra * ra).astype(BF16)
        y2 = _mm(z_ref[:, 0:D], w2_v[0])
        for k in range(1, N_CHIPS):
            y2 = y2 + _mm(z_ref[:, k * D:(k + 1) * D], w2_v[k])
        ry = lax.rsqrt(jnp.mean(y2 * y2, axis=-1, keepdims=True) + EPS)
        yn = y2 * ry
        n = yn * post_ref[...]
        g2 = mod_ref[:, 5 * D:6 * D]
        x2 = x2_ref[...]
        err = x2 + g2 * n - t_ref[...]
        acc_ref[5:6, :] += _rowsum(err * err) * (0.5 / D)
        dout = err * (1.0 / D)
        acc_ref[0:1, :] += _rowsum(dout * n)
        dn = dout * g2
        acc_ref[1:2, :] += _rowsum(dn * yn)
        dyn = dn * post_ref[...]
        dy2 = (ry * (dyn - yn * jnp.mean(dyn * yn, axis=-1, keepdims=True))).astype(BF16)
        dy2_ref[...] = dy2
        for k in range(N_CHIPS):
            dz = _mm(dy2, w2_v[k], NT)
            da_ref[:, k * D:(k + 1) * D] = (dz * (2.0 * ra_scr[:, k * D:(k + 1) * D])).astype(BF16)
        dh2 = jnp.zeros((TM, D), F32)
        for k in range(N_CHIPS):
            dh2 = dh2 + _mm(da_ref[:, k * D:(k + 1) * D], w1_v[k], NT)
        r2 = lax.rsqrt(jnp.mean(x2 * x2, axis=-1, keepdims=True) + EPS)
        xn = x2 * r2
        yv = xn * pre_ref[...]
        acc_ref[2:3, :] += _rowsum(dh2)
        acc_ref[3:4, :] += _rowsum(dh2 * yv)
        dyv = dh2 * (1.0 + mod_ref[:, 4 * D:5 * D])
        acc_ref[4:5, :] += _rowsum(dyv * xn)
        dxn = dyv * pre_ref[...]
        dx2_ref[...] = dout + r2 * (dxn - xn * jnp.mean(dxn * xn, axis=-1, keepdims=True))

    tile = BS((TM, D), lambda i: (i, 0))
    wide = BS((TM, D_FF), lambda i: (i, 0))
    vec = BS((1, D), lambda i: (0, 0))
    return pl.pallas_call(
        body, name="ffn_fwd_bwd", grid=(S // TM,),
        out_shape=(SDS((S, D_FF), BF16), SDS((S, D_FF), BF16), SDS((S, D), BF16), SDS((S, D), F32),
                   SDS((8, D), F32)),
        in_specs=[tile, tile, tile, BS((1, 6 * D), lambda i: (0, 0)), vec, vec, BS(memory_space=pl.ANY)],
        out_specs=(wide, wide, tile, tile, BS((8, D), lambda i: (0, 0))),
        scratch_shapes=[pltpu.VMEM((N_CHIPS, R_FF, D), BF16), pltpu.VMEM((N_CHIPS, R_FF, D), BF16),
                        pltpu.VMEM((TM, D_FF), F32),
                        pltpu.SemaphoreType.DMA((2,))],
        compiler_params=_params("arbitrary"),
    )(h2, x2, target, mod, post_cm, pre_cm, wg)


def _merge_bwd_call(dx2, y, ya, yb, p, mod, post_tm, wg):
    S = y.shape[0]

    def body(dx2_ref, y_ref, ya_ref, yb_ref, ga_ref, gb_ref, mod_ref, post_ref, wa_ref, wb_ref, wo_ref,
             dy_ref, dya_ref, dyb_ref, doa_ref, dcb_ref, dpg_ref, acc_ref, bsum_ref):
        @pl.when(pl.program_id(0) == 0)
        def _():
            acc_ref[...] = jnp.zeros_like(acc_ref)
            bsum_ref[...] = jnp.zeros_like(bsum_ref)

        y = y_ref[...]
        ry = lax.rsqrt(jnp.mean(y * y, axis=-1, keepdims=True) + EPS)
        yn = y * ry
        dx2 = dx2_ref[...]
        acc_ref[0:1, :] += _rowsum(dx2 * (yn * post_ref[...]))
        dn = dx2 * mod_ref[:, 2 * D:3 * D]
        acc_ref[1:2, :] += _rowsum(dn * yn)
        dyn = dn * post_ref[...]
        dy = (ry * (dyn - yn * jnp.mean(dyn * yn, axis=-1, keepdims=True))).astype(BF16)
        dy_ref[...] = dy
        dmg = _mm_rows_t(dy, wo_ref)
        sa, sb = _sig(ga_ref[...]), _sig(gb_ref[...])
        dya = (dmg * sa).astype(BF16)
        dyb = (dmg * sb).astype(BF16)
        dya_ref[...] = dya
        dyb_ref[...] = dyb
        dga = dmg * ya_ref[...] * (sa * (1.0 - sa))
        dgb = dmg * yb_ref[...] * (sb * (1.0 - sb))
        dpg_ref[:, 0:D] = dga.astype(BF16)
        dpg_ref[:, D:2 * D] = dgb.astype(BF16)
        bsum_ref[:, 0:D] += _rowsum(dga)
        bsum_ref[:, D:2 * D] += _rowsum(dgb)
        doa_ref[...] = _mm_rows_t(dya, wa_ref)
        dcb_ref[...] = _mm_rows_t(dyb, wb_ref)

    tile = BS((TM, D), lambda i: (i, 0))
    vec = BS((1, D), lambda i: (0, 0))
    return pl.pallas_call(
        body, name="merge_bwd", grid=(S // TM,),
        out_shape=(SDS((S, D), BF16), SDS((S, D), BF16), SDS((S, D), BF16), SDS((S, D), F32), SDS((S, D), F32),
                   SDS((S, 2 * D), BF16), SDS((8, D), F32), SDS((1, 2 * D), F32)),
        in_specs=[tile, tile, tile, tile, BS((TM, D), lambda i: (i, 6)), BS((TM, D), lambda i: (i, 7)),
                  BS((1, 6 * D), lambda i: (0, 0)), vec, _br_spec(O_BRA), _br_spec(O_BRB), _br_spec(O_OUT)],
        out_specs=(tile, tile, tile, tile, tile, BS((TM, 2 * D), lambda i: (i, 0)),
                   BS((8, D), lambda i: (0, 0)), BS((1, 2 * D), lambda i: (0, 0))),
        compiler_params=_params("arbitrary"),
    )(dx2, y, ya, yb, p, p, mod, post_tm, wg, wg, wg)


def _hgrn_bwd_call(p, o, doa, st, logits, gn):
    S = p.shape[0]
    nb = S // TB
    ncb = TB // CHUNK

    def body(q_ref, f_ref, v_ref, og_ref, o_ref, doa_ref, st_ref, lg_ref, gn_ref,
             dp_ref, bsum_ref, dlg_ref, dgn_ref, dst_scr, dlb_scr, dqe_s, dqt_s, dkt_s, dkd_s, dv_s, dog_s, dble_s):
        i = pl.program_id(0)

        @pl.when(i == 0)
        def _():
            dst_scr[...] = jnp.zeros_like(dst_scr)
            dlb_scr[...] = jnp.zeros_like(dlb_scr)
            bsum_ref[...] = jnp.zeros_like(bsum_ref)
            dgn_ref[...] = jnp.zeros_like(dgn_ref)

        lb = _lower_bound(lg_ref)
        tril, triu = _tri_masks()

        def chunk(tt, carry):
            ci = ncb - 1 - tt
            rows = pl.ds(pl.multiple_of(ci * CHUNK, CHUNK), CHUNK)
            q_r, f_r = q_ref[rows, :], f_ref[rows, :]
            t = _hg_gates(q_r, f_r, lb, tril)
            v = v_ref[rows, :]
            for h in range(HEADS):
                sl = slice(h * DK, (h + 1) * DK)
                stp = st_ref[ci, :, sl]
                stb = stp.astype(BF16)
                qeb = t["qe"][:, sl].astype(BF16)
                qtb = t["qt"][:, sl].astype(BF16)
                ktb = t["kt"][:, sl].astype(BF16)
                kdb = t["kd"][:, sl].astype(BF16)
                vb = v[:, sl].astype(BF16)
                a = jnp.where(tril > 0.5, _mm(qtb, ktb, NT), 0.0)
                o_h = o_ref[rows, sl]
                rinv = lax.rsqrt(jnp.mean(o_h * o_h, axis=-1, keepdims=True) + EPS)
                oh = o_h * rinv
                og = og_ref[rows, sl]
                so = _sig(og)
                d_oa = doa_ref[rows, sl]
                don = d_oa * (og * so)
                dog_s[:, sl] = d_oa * (oh * gn_ref[:, sl]) * _dsilu(og, so)
                dgn_ref[:, sl] += _rowsum(don * oh)
                doh = don * gn_ref[:, sl]
                do = (rinv * (doh - oh * jnp.mean(doh * oh, axis=-1, keepdims=True))).astype(BF16)
                dqe_s[:, sl] = _mm(do, stb, NN)
                dstp = _mm(do, qeb, TN)
                dab = jnp.where(tril > 0.5, _mm(do, vb, NT), 0.0).astype(BF16)
                dqt_s[:, sl] = _mm(dab, ktb, NN)
                dkt_s[:, sl] = _mm(dab, qtb, TN)
                dstn = dst_scr[:, sl]
                dsb = dstn.astype(BF16)
                dkd_s[:, sl] = _mm(vb, dsb, NN)
                dv_s[:, sl] = _mm(a.astype(BF16), do, TN) + _mm(kdb, dsb, NT)
                el = t["elast"][:, sl]
                dst_scr[:, sl] = dstn * el + dstp
                dble_s[:, sl] = el * _rowsum(stp * dstn)
            dqe, dqt, dkt, dkd = dqe_s[...], dqt_s[...], dkt_s[...], dkd_s[...]
            dq = dqe * t["e"] + dqt * t["eq"]
            dk = dkt * t["ek"] + dkd * t["dd"]
            dkk = dkd * t["kd"]
            qt_r = t["qt"].astype(BF16).astype(F32)
            kt_r = t["kt"].astype(BF16).astype(F32)
            dbv = dqe * t["qe"] + dqt * qt_r - dkt * kt_r - dkk
            dg = _cumsum_mm(triu, dbv) + (_rowsum(dkk) + dble_s[...])
            df = dg / t["f"] - dk
            sf = t["sf"]
            dlb_scr[...] += _rowsum(df * (1.0 - sf))
            dqr = dq * _dsilu(q_r, t["sq"])
            dfr = df * (1.0 - lb) * (sf * (1.0 - sf))
            dvv, dog = dv_s[...], dog_s[...]
            dp_ref[rows, 0:D] = dqr.astype(BF16)
            dp_ref[rows, D:2 * D] = dfr.astype(BF16)
            dp_ref[rows, 2 * D:3 * D] = dvv.astype(BF16)
            dp_ref[rows, 3 * D:4 * D] = dog.astype(BF16)
            bsum_ref[:, 0:D] += _rowsum(dqr)
            bsum_ref[:, D:2 * D] += _rowsum(dfr)
            bsum_ref[:, 2 * D:3 * D] += _rowsum(dvv)
            bsum_ref[:, 3 * D:4 * D] += _rowsum(dog)
            return carry

        lax.fori_loop(0, ncb, chunk, 0)

        dl = dlb_scr[...] * lb * (1.0 - lb)
        dlg_ref[0:1, :] = dl
        dlg_ref[1:2, :] = -dl

    col = lambda j: BS((TB, D), lambda i, j=j: (nb - 1 - i, j))
    rev = BS((TB, D), lambda i: (nb - 1 - i, 0))
    cd = pltpu.VMEM((CHUNK, D), F32)
    return pl.pallas_call(
        body, name="hgrn_bwd", grid=(nb,),
        out_shape=(SDS((S, 4 * D), BF16), SDS((1, 4 * D), F32), SDS((2, D), F32), SDS((1, D), F32)),
        in_specs=[col(0), col(1), col(2), col(3), rev, rev, BS((ncb, DK, D), lambda i: (nb - 1 - i, 0, 0)),
                  BS((2, D), lambda i: (0, 0)), BS((1, D), lambda i: (0, 0))],
        out_specs=(BS((TB, 4 * D), lambda i: (nb - 1 - i, 0)), BS((1, 4 * D), lambda i: (0, 0)),
                   BS((2, D), lambda i: (0, 0)), BS((1, D), lambda i: (0, 0))),
        scratch_shapes=[pltpu.VMEM((DK, D), F32), pltpu.VMEM((1, D), F32), cd, cd, cd, cd, cd, cd,
                        pltpu.VMEM((1, D), F32)],
        compiler_params=_params("arbitrary"),
    )(p, p, p, p, o, doa, st, logits, gn)


def _conv_bwd_call(dcb, uc, u, p, dw, ln_g, ln_b):
    S = uc.shape[0]
    nb = S // TM
    hb = TM // HALO

    def body(dcb_ref, uc_ref, u_ref, uh_ref, cv_ref, cg_ref, dw_ref, g_ref, b_ref,
             dp_ref, bsum_ref, ddw_ref, acc_ref, uext, dext, ush, dsh):
        i = pl.program_id(0)

        @pl.when(i == 0)
        def _():
            dext[TM:EXT, :] = jnp.zeros((EXT - TM, D), F32)
            uext[HALO + TM:EXT, :] = jnp.zeros((EXT - HALO - TM, D), F32)
            bsum_ref[...] = jnp.zeros_like(bsum_ref)
            ddw_ref[...] = jnp.zeros_like(ddw_ref)
            acc_ref[...] = jnp.zeros_like(acc_ref)

        first_tile = (nb - 1 - i) == 0
        uext[0:HALO, :] = jnp.where(first_tile, 0.0, uh_ref[...])
        uext[HALO:HALO + TM, :] = u_ref[...]
        _fill_shifted(uext, ush)

        for rb in range(TM // SUB):
            rs_ = slice(rb * SUB, (rb + 1) * SUB)
            xh, rs = _layernorm_stats(uc_ref[rs_, :])
            ln = xh * g_ref[...] + b_ref[...]
            dln = dcb_ref[rs_, :] * _dsilu(ln, _sig(ln))
            acc_ref[1:2, :] += _rowsum(dln * xh)
            acc_ref[2:3, :] += _rowsum(dln)
            dxh = dln * g_ref[...]
            duc = rs * (dxh - jnp.mean(dxh, axis=-1, keepdims=True)
                        - xh * jnp.mean(dxh * xh, axis=-1, keepdims=True))
            dext[rs_, :] = duc
            acc_ref[0:1, :] += _rowsum(duc)
        _fill_shifted(dext, dsh)

        for j in range(CONV_K):
            part = jnp.zeros((SUB, D), F32)
            for rb in range(TM // SUB):
                s0 = HALO - (CONV_K - 1) + j + rb * SUB
                part = part + dext[rb * SUB:(rb + 1) * SUB, :] * _window(uext, ush, s0, SUB)
            ddw_ref[j:j + 1, :] += _rowsum(part)

        for rb in range(TM // SUB):
            rs_ = slice(rb * SUB, (rb + 1) * SUB)
            du = jnp.zeros((SUB, D), F32)
            for j in range(CONV_K):
                s0 = rb * SUB + (CONV_K - 1) - j
                du = du + dw_ref[j:j + 1, :] * _window(dext, dsh, s0, SUB)
            cg = cg_ref[rs_, :]
            sg = _sig(cg)
            dcv = du * sg
            dcg = du * cv_ref[rs_, :] * (sg * (1.0 - sg))
            dp_ref[rs_, 0:D] = dcv.astype(BF16)
            dp_ref[rs_, D:2 * D] = dcg.astype(BF16)
            bsum_ref[:, 0:D] += _rowsum(dcv)
            bsum_ref[:, D:2 * D] += _rowsum(dcg)

        dext[TM:TM + HALO, :] = dext[0:HALO, :]

    rev = BS((TM, D), lambda i: (nb - 1 - i, 0))
    vec = BS((1, D), lambda i: (0, 0))
    return pl.pallas_call(
        body, name="conv_bwd", grid=(nb,),
        out_shape=(SDS((S, 2 * D), BF16), SDS((1, 2 * D), F32), SDS((32, D), F32), SDS((8, D), F32)),
        in_specs=[rev, rev, rev, BS((HALO, D), lambda i: (jnp.maximum((nb - 1 - i) * hb - 1, 0), 0)),
                  BS((TM, D), lambda i: (nb - 1 - i, 4)), BS((TM, D), lambda i: (nb - 1 - i, 5)),
                  BS((CONV_K, D), lambda i: (0, 0)), vec, vec],
        out_specs=(BS((TM, 2 * D), lambda i: (nb - 1 - i, 0)), BS((1, 2 * D), lambda i: (0, 0)),
                   BS((32, D), lambda i: (0, 0)), BS((8, D), lambda i: (0, 0))),
        scratch_shapes=[pltpu.VMEM((EXT, D), F32), pltpu.VMEM((EXT, D), F32),
                        pltpu.VMEM((7, HALO + TM, D), F32), pltpu.VMEM((7, HALO + TM, D), F32)],
        compiler_params=_params("arbitrary"),
    )(dcb, uc, u, u, p, p, dw, ln_g, ln_b)


def _in_bwd_call(dp_hg, dp_cv, dp_gt, x, dx2, mod, pre_tm, wg):
    S = x.shape[0]

    def body(hg_ref, cv_ref, gt_ref, x_ref, dx2_ref, mod_ref, g_ref, w_hbm, gx_ref, acc_ref, w_vmem, sem):
        @pl.when(pl.program_id(0) == 0)
        def _():
            _load_rows(w_hbm, w_vmem, sem, O_IN).wait()
            acc_ref[...] = jnp.zeros_like(acc_ref)

        dh = jnp.zeros((TM, D), F32)
        for k in range(IN_COLS // D):
            src, kk = ((hg_ref, k), (cv_ref, k - 4), (gt_ref, k - 6))[0 if k < 4 else (1 if k < 6 else 2)]
            dh = dh + _mm(src[:, kk * D:(kk + 1) * D], w_vmem[k // 2, (k % 2) * D:(k % 2 + 1) * D, :], NT)
        xv = x_ref[...]
        r = lax.rsqrt(jnp.mean(xv * xv, axis=-1, keepdims=True) + EPS)
        xn = xv * r
        yv = xn * g_ref[...]
        acc_ref[0:1, :] += _rowsum(dh)
        acc_ref[1:2, :] += _rowsum(dh * yv)
        dyv = dh * (1.0 + mod_ref[:, D:2 * D])
        acc_ref[2:3, :] += _rowsum(dyv * xn)
        dxn = dyv * g_ref[...]
        gx_ref[...] = dx2_ref[...] + r * (dxn - xn * jnp.mean(dxn * xn, axis=-1, keepdims=True))

    tile = BS((TM, D), lambda i: (i, 0))
    return pl.pallas_call(
        body, name="in_bwd", grid=(S // TM,),
        out_shape=(SDS((S, D), F32), SDS((8, D), F32)),
        in_specs=[BS((TM, 4 * D), lambda i: (i, 0)), BS((TM, 2 * D), lambda i: (i, 0)),
                  BS((TM, 2 * D), lambda i: (i, 0)), tile, tile, BS((1, 6 * D), lambda i: (0, 0)),
                  BS((1, D), lambda i: (0, 0)), BS(memory_space=pl.ANY)],
        out_specs=(tile, BS((8, D), lambda i: (0, 0))),
        scratch_shapes=[pltpu.VMEM((N_CHIPS, R_IN, D), BF16), pltpu.SemaphoreType.DMA],
        compiler_params=_params("arbitrary"),
    )(dp_hg, dp_cv, dp_gt, x, dx2, mod, pre_tm, wg)


def _wgrad_call(gp, a, b, name, bm, place):
    S, M = a.shape
    N = b.shape[1]
    bk = min(S, 1024)
    nk = S // bk

    def body(a_ref, b_ref, *rest):
        o_ref, acc = rest[-2], rest[-1]
        k = pl.program_id(2)

        @pl.when(k == 0)
        def _():
            acc[...] = jnp.zeros_like(acc)

        acc[...] += _mm(a_ref[...], b_ref[...], TN)

        @pl.when(k == nk - 1)
        def _():
            o_ref[...] = acc[...].astype(BF16)

    in_specs = [BS((bk, bm), lambda i, j, k: (k, i)), BS((bk, D), lambda i, j, k: (k, j))]
    args = [a, b]
    if gp is not None:
        in_specs.append(BS(memory_space=pl.ANY))
        args.append(gp)
    return pl.pallas_call(
        body, name=name, grid=(M // bm, N // D, nk),
        out_shape=SDS((N_CHIPS, PACK_G, D), BF16),
        in_specs=in_specs,
        out_specs=BS((None, bm, D), lambda i, j, k: (*place(i, j), 0)),
        scratch_shapes=[pltpu.VMEM((bm, D), F32)],
        input_output_aliases={} if gp is None else {2: 0},
        compiler_params=_params("parallel", "parallel", "arbitrary"),
    )(*args)


def _outer_call(cact, dmod):
    n = dmod.shape[1]

    def body(a_ref, b_ref, o_ref):
        o_ref[...] = _mm(a_ref[...], b_ref[...], TN, HI)

    return pl.pallas_call(
        body, name="wgrad_ada", out_shape=SDS((D, n), F32),
        compiler_params=pltpu.CompilerParams(vmem_limit_bytes=VMEM_LIMIT),
    )(cact, dmod)


def _adamw_call(w, g, m, v, name):
    R, C = w.shape
    tr = R
    while tr * C > 512 * 1024 and tr % 16 == 0:
        tr //= 2
    c1 = 1.0 - ADAM_B1 ** ADAM_STEP
    c2 = 1.0 - ADAM_B2 ** ADAM_STEP

    def body(w_ref, g_ref, m_ref, v_ref, d_ref, m2_ref, v2_ref):
        g = g_ref[...]
        m2 = ADAM_B1 * m_ref[...] + (1.0 - ADAM_B1) * g
        v2 = ADAM_B2 * v_ref[...] + (1.0 - ADAM_B2) * (g * g)
        m2_ref[...] = m2
        v2_ref[...] = v2
        d_ref[...] = -ADAM_LR * ((m2 / c1) / (jnp.sqrt(v2 / c2) + ADAM_EPS) + ADAM_WD * w_ref[...])

    tile = BS((tr, C), lambda i: (i, 0))
    return pl.pallas_call(
        body, name=name, grid=(R // tr,), out_shape=(SDS((R, C), F32),) * 3,
        in_specs=[tile] * 4, out_specs=(tile,) * 3, compiler_params=_params("parallel"),
    )(w, g, m, v)


def _local_step(x, c, target, wg, small):
    mod8, cact = _mod_call(c, wg, small["b_ada"])
    mod = mod8[0:1]
    p, h1 = _fwd_in_call(x, mod, small["pre_tm"], wg, small["b_in"])
    o, oa, st = _hgrn_fwd_call(p, small["logits"], small["hg_norm"])
    u, uc, cb = _conv_fwd_call(p, small["conv_dw"], small["conv_db"], small["ln_g"], small["ln_b"])
    ya, yb, mg, y, x2, h2 = _merge_fwd_call(oa, cb, p, x, mod, small["post_tm"], small["pre_cm"], wg)
    z, da, dy2, dx2, acc_f = _ffn_call(h2, x2, target, mod, small["post_cm"], small["pre_cm"], wg)
    dy, dya, dyb, doa, dcb, dp_gt, acc_m, bs_gt = _merge_bwd_call(dx2, y, ya, yb, p, mod, small["post_tm"], wg)
    dp_hg, bs_hg, dlg, dgn = _hgrn_bwd_call(p, o, doa, st, small["logits"], small["hg_norm"])
    dp_cv, bs_cv, ddw, acc_c = _conv_bwd_call(dcb, uc, u, p, small["conv_dw"], small["ln_g"], small["ln_b"])
    gx, acc_i = _in_bwd_call(dp_hg, dp_cv, dp_gt, x, dx2, mod, small["pre_tm"], wg)

    gp = _wgrad_call(None, h2, da, "wgrad_ff1", D, lambda i, j: (j, O_FF1 // D))
    gp = _wgrad_call(gp, z, dy2, "wgrad_ff2", D, lambda i, j: (i, O_FF2 // D))
    gp = _wgrad_call(gp, oa, dya, "wgrad_br_a", R_BR, lambda i, j: (i, O_BRA // R_BR))
    gp = _wgrad_call(gp, cb, dyb, "wgrad_br_b", R_BR, lambda i, j: (i, O_BRB // R_BR))
    gp = _wgrad_call(gp, mg, dy, "wgrad_out", R_BR, lambda i, j: (i, O_OUT // R_BR))
    gp = _wgrad_call(gp, h1, dp_hg, "wgrad_in_hg", D, lambda i, j: (j // 2, j % 2))
    gp = _wgrad_call(gp, h1, dp_cv, "wgrad_in_cv", D, lambda i, j: (2, j))
    gp = _wgrad_call(gp, h1, dp_gt, "wgrad_in_gt", D, lambda i, j: (3, j))
    zrow = jnp.zeros((1, D), F32)
    rows = [acc_i[0:1], acc_i[1:2], acc_m[0:1], acc_f[2:3], acc_f[3:4], acc_f[0:1],
            acc_i[2:3], acc_m[1:2], acc_f[4:5], acc_f[1:2],
            jnp.concatenate([bs_hg, bs_cv, bs_gt], axis=1).reshape(8, D),
            dlg, dgn, acc_c[0:1], acc_c[1:2], acc_c[2:3],
            ddw,
            cact, acc_f[5:6]] + [zrow] * 6
    return gx, jnp.concatenate(rows, axis=0), gp


def kernel(x, c, w_ada, b_ada, pre_norm_tm, post_norm_tm, pre_norm_cm, post_norm_cm, w_in, b_in, hg_lb_logits, hg_norm, conv_dw, conv_db, conv_ln_g, conv_ln_b, w_br_a, w_br_b, w_out, w_ff1, w_ff2, loss_target, m_w_ada, m_b_ada, m_pre_norm_tm, m_post_norm_tm, m_pre_norm_cm, m_post_norm_cm, m_w_in, m_b_in, m_hg_lb_logits, m_hg_norm, m_conv_dw, m_conv_db, m_conv_ln_g, m_conv_ln_b, m_w_br_a, m_w_br_b, m_w_out, m_w_ff1, m_w_ff2, v_w_ada, v_b_ada, v_pre_norm_tm, v_post_norm_tm, v_pre_norm_cm, v_post_norm_cm, v_w_in, v_b_in, v_hg_lb_logits, v_hg_norm, v_conv_dw, v_conv_db, v_conv_ln_g, v_conv_ln_b, v_w_br_a, v_w_br_b, v_w_out, v_w_ff1, v_w_ff2):
    xi, yi, ci = lax.axis_index("x"), lax.axis_index("y"), lax.axis_index("c")
    chip = 2 * xi + yi
    c_idx = jnp.reshape(ci, (1,)).astype(jnp.int32)
    chip_idx = jnp.reshape(chip, (1,)).astype(jnp.int32)

    def pack_small(ada_b, pre_t, post_t, pre_c, post_c, in_b, lg, hgn, cdb, lng, lnb, cdw):
        flat = jnp.concatenate([cdw[0].reshape(-1), jnp.zeros((8 * D - CONV_K * 256,), F32)]).reshape(8, D)
        return jnp.concatenate([ada_b.reshape(6, D), pre_t, post_t, pre_c, post_c, in_b.reshape(8, D), lg, hgn,
                                cdb, lng, lnb, flat], axis=0)

    w_in_halves = w_in[0].reshape(D, 2, D).transpose(1, 0, 2).reshape(R_IN, D)
    pack = jnp.concatenate([w_in_halves, w_ff1[0], w_ff2[0], w_br_a[0], w_br_b[0], w_out[0], w_ada[0].T],
                           axis=0).astype(BF16)
    half = lax.dynamic_slice_in_dim(pack, ci * (PACK_W // 2), PACK_W // 2, axis=0)
    wg = _allgather_call(half, "gather_weights", in_vmem=False, with_sum=False)[0].reshape(N_CHIPS, PACK_W, D)
    dw_blk = jnp.concatenate([conv_dw[0].reshape(-1), jnp.zeros((8 * D - CONV_K * 256,), F32)]).reshape(8, D)
    dw_all = _allgather_call(dw_blk, "gather_conv_dw", in_vmem=True, with_sum=False)[0]
    dw_all = dw_all.reshape(N_CHIPS, 2, 8 * D)[:, 0, :CONV_K * 256].reshape(N_CHIPS, CONV_K, 256)
    dw_full = dw_all.transpose(1, 0, 2).reshape(CONV_K, D)

    small = dict(b_ada=b_ada, pre_tm=pre_norm_tm, post_tm=post_norm_tm, pre_cm=pre_norm_cm, post_cm=post_norm_cm,
                 b_in=b_in, logits=hg_lb_logits, hg_norm=hg_norm, conv_dw=dw_full, conv_db=conv_db,
                 ln_g=conv_ln_g, ln_b=conv_ln_b)

    gx, srows, gp = _local_step(x[0], c, loss_target[0], wg, small)

    sall, ssum = _allgather_call(srows, "gather_small", in_vmem=True, with_sum=True)
    sall = sall.reshape(N_DEV, SMALL_ROWS, D)
    loss = jnp.sum(ssum[57])
    dmod_all = sall[:, 0:6, :].reshape(N_DEV, 6 * D)
    wa = 6 * D // N_CHIPS
    g_ada = _outer_call(sall[:, 56, :], lax.dynamic_slice_in_dim(dmod_all, chip * wa, wa, axis=1))
    g_dw = lax.dynamic_slice_in_dim(ssum[24:24 + CONV_K], chip * 256, 256, axis=1)
    g_small = jnp.concatenate(
        [ssum[0:24], jnp.concatenate([g_dw.reshape(-1), jnp.zeros((8 * D - CONV_K * 256,), F32)]).reshape(8, D)],
        axis=0)

    gp = gp.reshape(N_CHIPS, 2, PACK_G // 2, D)
    part = _add_halves_call(gp, _sibling_halves_call(gp), c_idx)
    red = _add_chips_call(part, _chip_exchange_call(part), chip_idx)
    other = _sibling_join_call(red)
    g_big = jnp.where(ci == 0, jnp.concatenate([red, other], axis=0), jnp.concatenate([other, red], axis=0))

    shapes = {"in": w_in.shape, "br_a": w_br_a.shape, "br_b": w_br_b.shape, "out": w_out.shape,
              "ff1": w_ff1.shape, "ff2": w_ff2.shape}
    offs = {"in": (O_IN, O_FF1), "ff1": (O_FF1, O_FF2), "ff2": (O_FF2, O_BRA), "br_a": (O_BRA, O_BRB),
            "br_b": (O_BRB, O_OUT), "out": (O_OUT, PACK_G)}
    wmv = {"in": (w_in, m_w_in, v_w_in), "br_a": (w_br_a, m_w_br_a, v_w_br_a), "br_b": (w_br_b, m_w_br_b, v_w_br_b),
           "out": (w_out, m_w_out, v_w_out), "ff1": (w_ff1, m_w_ff1, v_w_ff1), "ff2": (w_ff2, m_w_ff2, v_w_ff2)}
    res = {}
    for n in offs:
        shp = shapes[n]
        g2d = g_big[offs[n][0]:offs[n][1]]
        if n == "in":
            g2d = g2d.reshape(2, D, D).transpose(1, 0, 2)
        g2d = g2d.reshape(shp[1], shp[2])
        w_, m_, v_ = (a[0] for a in wmv[n])
        d_, m2_, v2_ = _adamw_call(w_, g2d, m_, v_, "adamw_" + n)
        res[n] = tuple(a.reshape(shp) for a in (g2d, d_, m2_, v2_))
    d_, m2_, v2_ = _adamw_call(w_ada[0], g_ada, m_w_ada[0], v_w_ada[0], "adamw_ada")
    res["ada"] = tuple(a.reshape(w_ada.shape) for a in (g_ada, d_, m2_, v2_))

    ws = pack_small(b_ada, pre_norm_tm, post_norm_tm, pre_norm_cm, post_norm_cm, b_in, hg_lb_logits, hg_norm,
                    conv_db, conv_ln_g, conv_ln_b, conv_dw)
    ms = pack_small(m_b_ada, m_pre_norm_tm, m_post_norm_tm, m_pre_norm_cm, m_post_norm_cm, m_b_in, m_hg_lb_logits,
                    m_hg_norm, m_conv_db, m_conv_ln_g, m_conv_ln_b, m_conv_dw)
    vs = pack_small(v_b_ada, v_pre_norm_tm, v_post_norm_tm, v_pre_norm_cm, v_post_norm_cm, v_b_in, v_hg_lb_logits,
                    v_hg_norm, v_conv_db, v_conv_ln_g, v_conv_ln_b, v_conv_dw)
    sres = (g_small,) + tuple(_adamw_call(ws, g_small, ms, vs, "adamw_small"))

    def unpack_small(t):
        return {"b_ada": t[0:6].reshape(1, 6 * D), "pre_tm": t[6:7], "post_tm": t[7:8], "pre_cm": t[8:9],
                "post_cm": t[9:10], "b_in": t[10:18].reshape(1, IN_COLS), "logits": t[18:20], "hg_norm": t[20:21],
                "conv_db": t[21:22], "ln_g": t[22:23], "ln_b": t[23:24],
                "conv_dw": t[24:32].reshape(-1)[:CONV_K * 256].reshape(1, CONV_K, 256)}

    order = ["ada", "b_ada", "pre_tm", "post_tm", "pre_cm", "post_cm", "in", "b_in", "logits", "hg_norm", "conv_dw",
             "conv_db", "ln_g", "ln_b", "br_a", "br_b", "out", "ff1", "ff2"]
    outs = [loss, gx.reshape(x.shape)]
    for kind in range(4):
        sm = unpack_small(sres[kind])
        for n in order:
            outs.append(res[n][kind] if n in res else sm[n])
    return tuple(outs)
```

```python
import functools

import jax
import jax.numpy as jnp
from jax import lax
from jax.experimental import pallas as pl
from jax.experimental.pallas import tpu as pltpu

F32, BF16 = jnp.float32, jnp.bfloat16
SDS = jax.ShapeDtypeStruct
BS = pl.BlockSpec
MESH = pl.DeviceIdType.MESH
HI = lax.Precision.HIGHEST

D = 1024
D_FF = 4096
IN_COLS = 8192
HEADS, DK = 8, 128
CHUNK = 128
CONV_K = 31
HALO = 32
SUB = 32
EPS = 1e-6
N_CHIPS, N_DEV = 4, 8
TM = 256
TB = 256
VMEM_LIMIT = 56 * 1024 * 1024

R_IN, R_BR, R_FF, R_ADA = 2048, 256, 1024, 1536
PACK_W = R_IN + 3 * R_BR + 2 * R_FF + R_ADA
PACK_G = R_IN + 3 * R_BR + 2 * R_FF
O_IN, O_FF1, O_FF2, O_BRA, O_BRB, O_OUT, O_ADA = 0, 2048, 3072, 4096, 4352, 4608, 4864
SMALL_ROWS = 64

ADAM_LR, ADAM_B1, ADAM_B2, ADAM_EPS, ADAM_WD, ADAM_STEP = 0.001, 0.9, 0.999, 1e-08, 0.01, 10

NN = (((1,), (0,)), ((), ()))
NT = (((1,), (1,)), ((), ()))
TN = (((0,), (0,)), ((), ()))


def _mm(a, b, dims=NN, precision=None):
    return lax.dot_general(a, b, dims, preferred_element_type=F32, precision=precision)


def _sig(v):
    return jax.nn.sigmoid(v)


def _dsilu(v, s):
    return s * (1.0 + v * (1.0 - s))


def _params(*sem):
    return pltpu.CompilerParams(dimension_semantics=sem if sem else None, vmem_limit_bytes=VMEM_LIMIT)


def _rowsum(v):
    return jnp.sum(v, axis=0, keepdims=True)


def _mesh_pos():
    return lax.axis_index("x"), lax.axis_index("y"), lax.axis_index("c")


def _allgather_call(blk, name, in_vmem, with_sum):
    m_per, n = blk.shape

    def body(x_ref, out_ref, *rest):
        if with_sum:
            sum_ref, send_sems, recv_sems, local_sem = rest
        else:
            send_sems, recv_sems, local_sem = rest
        x, y, c = _mesh_pos()
        me, sibling = (x, y, c), (x, y, 1 - c)
        chips = [(1 - x, y), (x, 1 - y), (1 - x, 1 - y)]

        def rows(px, py, pc):
            return out_ref.at[pl.ds((4 * px + 2 * py + pc) * m_per, m_per), :]

        def copy(k, block, to, src=None):
            return pltpu.make_async_remote_copy(
                src_ref=rows(*block) if src is None else src, dst_ref=rows(*block),
                send_sem=send_sems.at[k], recv_sem=recv_sems.at[k], device_id=to, device_id_type=MESH)

        mine = pltpu.make_async_copy(x_ref, rows(*me), local_sem)
        mine.start()
        first = [copy(0, me, sibling, src=x_ref)]
        first += [copy(1 + j, me, (*chip, c), src=x_ref) for j, chip in enumerate(chips)]
        for cp in first:
            cp.start()
        passed = [copy(4 + j, (*chip, c), sibling) for j, chip in enumerate(chips)]
        for j, chip in enumerate(chips):
            copy(1 + j, (*chip, c), me).wait_recv()
            passed[j].start()
        copy(0, sibling, me).wait_recv()
        for j, chip in enumerate(chips):
            copy(4 + j, (*chip, 1 - c), me).wait_recv()
        for cp in first + passed:
            cp.wait_send()
        mine.wait()
        if with_sum:
            acc = out_ref[0:m_per, :]
            for d in range(1, N_DEV):
                acc = acc + out_ref[d * m_per:(d + 1) * m_per, :]
            sum_ref[...] = acc

    space = pltpu.VMEM if in_vmem else pl.ANY
    out_shape = [SDS((N_DEV * m_per, n), blk.dtype)]
    out_specs = [BS(memory_space=space)]
    if with_sum:
        out_shape.append(SDS((m_per, n), blk.dtype))
        out_specs.append(BS(memory_space=pltpu.VMEM))
    return pl.pallas_call(
        body, name=name, out_shape=out_shape, in_specs=[BS(memory_space=space)], out_specs=out_specs,
        scratch_shapes=[pltpu.SemaphoreType.DMA((7,)), pltpu.SemaphoreType.DMA((7,)), pltpu.SemaphoreType.DMA],
        compiler_params=pltpu.CompilerParams(vmem_limit_bytes=VMEM_LIMIT),
    )(blk)


def _gather_sems(n_ranges):
    return [pltpu.SemaphoreType.DMA((6 * n_ranges,)), pltpu.SemaphoreType.DMA((6 * n_ranges,))]


def _pack_gather(pack_ref, wg_ref, send_sems, recv_sems, ranges):
    x, y, c = _mesh_pos()
    me, sibling = (x, y, c), (x, y, 1 - c)
    chips = [(1 - x, y), (x, 1 - y), (1 - x, 1 - y)]

    def land(r, px, py, pc):
        off, n = ranges[r]
        return wg_ref.at[2 * px + py, pl.ds(off + pc * (n // 2), n // 2), :]

    def mine(r):
        off, n = ranges[r]
        return pack_ref.at[pl.ds(off + c * (n // 2), n // 2), :]

    def copy(r, k, block, to, src=None):
        return pltpu.make_async_remote_copy(
            src_ref=land(r, *block) if src is None else src, dst_ref=land(r, *block),
            send_sem=send_sems.at[6 * r + k], recv_sem=recv_sems.at[6 * r + k], device_id=to, device_id_type=MESH)

    def start():
        for r in range(len(ranges)):
            for j, chip in enumerate(chips):
                copy(r, j, me, (*chip, c), src=mine(r)).start()

    def finish():
        for r in range(len(ranges)):
            for j, chip in enumerate(chips):
                copy(r, j, (*chip, c), me).wait_recv()
                copy(r, 3 + j, (*chip, c), sibling).start()
        for r in range(len(ranges)):
            for j, chip in enumerate(chips):
                copy(r, 3 + j, (*chip, 1 - c), me).wait_recv()
                copy(r, j, me, (*chip, c), src=mine(r)).wait_send()
                copy(r, 3 + j, (*chip, c), sibling).wait_send()

    return start, finish


def _gather_first_call(pack, wg):
    ranges = [(O_IN, R_IN), (O_ADA, R_ADA)]

    def body(pack_ref, wg_in, wg_out, send_sems, recv_sems):
        start, finish = _pack_gather(pack_ref, wg_out, send_sems, recv_sems, ranges)
        start()
        finish()

    hbm = BS(memory_space=pl.ANY)
    return pl.pallas_call(
        body, name="gather_w_in_ada", out_shape=SDS(wg.shape, wg.dtype), in_specs=[hbm, hbm], out_specs=hbm,
        scratch_shapes=_gather_sems(len(ranges)), input_output_aliases={1: 0},
    )(pack, wg)


def _sibling_halves_call(g):
    _, _, h, n = g.shape

    def body(g_ref, out_ref, send_sems, recv_sems):
        x, y, c = _mesh_pos()
        cps = [pltpu.make_async_remote_copy(
            src_ref=g_ref.at[k, 1 - c], dst_ref=out_ref.at[k], send_sem=send_sems.at[k], recv_sem=recv_sems.at[k],
            device_id=(x, y, 1 - c), device_id_type=MESH) for k in range(N_CHIPS)]
        for cp in cps:
            cp.start()
        for cp in cps:
            cp.wait()

    return pl.pallas_call(
        body, name="rs_sibling_halves", out_shape=SDS((N_CHIPS, h, n), g.dtype),
        in_specs=[BS(memory_space=pl.ANY)], out_specs=BS(memory_space=pl.ANY),
        scratch_shapes=[pltpu.SemaphoreType.DMA((N_CHIPS,)), pltpu.SemaphoreType.DMA((N_CHIPS,))],
    )(g)


def _chip_exchange_call(p):
    _, h, n = p.shape

    def body(p_ref, out_ref, send_sems, recv_sems):
        x, y, c = _mesh_pos()
        chips = [(1 - x, y), (x, 1 - y), (1 - x, 1 - y)]
        cps = [pltpu.make_async_remote_copy(
            src_ref=p_ref.at[2 * cx + cy], dst_ref=out_ref.at[j], send_sem=send_sems.at[j], recv_sem=recv_sems.at[j],
            device_id=(cx, cy, c), device_id_type=MESH) for j, (cx, cy) in enumerate(chips)]
        for cp in cps:
            cp.start()
        for cp in cps:
            cp.wait()

    return pl.pallas_call(
        body, name="rs_chip_exchange", out_shape=SDS((3, h, n), p.dtype),
        in_specs=[BS(memory_space=pl.ANY)], out_specs=BS(memory_space=pl.ANY),
        scratch_shapes=[pltpu.SemaphoreType.DMA((3,)), pltpu.SemaphoreType.DMA((3,))],
    )(p)


def _sibling_join_call(r):
    h, n = r.shape
    q = h // 4

    def body(r_ref, out_ref, send_sems, recv_sems):
        x, y, c = _mesh_pos()
        cps = [pltpu.make_async_remote_copy(
            src_ref=r_ref.at[pl.ds(k * q, q)], dst_ref=out_ref.at[pl.ds(k * q, q)],
            send_sem=send_sems.at[k], recv_sem=recv_sems.at[k],
            device_id=(x, y, 1 - c), device_id_type=MESH) for k in range(4)]
        for cp in cps:
            cp.start()
        for cp in cps:
            cp.wait()

    return pl.pallas_call(
        body, name="rs_sibling_join", out_shape=SDS((h, n), r.dtype),
        in_specs=[BS(memory_space=pl.ANY)], out_specs=BS(memory_space=pl.ANY),
        scratch_shapes=[pltpu.SemaphoreType.DMA((4,)), pltpu.SemaphoreType.DMA((4,))],
    )(r)


def _add_halves_call(g, recv, c_idx):
    _, _, h, n = g.shape
    tr = h // 8

    def body(c_ref, g_ref, r_ref, o_ref):
        o_ref[...] = (g_ref[...].astype(F32) + r_ref[...].astype(F32)).astype(BF16)

    return pl.pallas_call(
        body, name="rs_add_halves", out_shape=SDS((N_CHIPS, h, n), BF16),
        grid_spec=pltpu.PrefetchScalarGridSpec(
            num_scalar_prefetch=1, grid=(N_CHIPS, 8),
            in_specs=[BS((None, None, tr, n), lambda k, r, c_ref: (k, c_ref[0], r, 0)),
                      BS((None, tr, n), lambda k, r, c_ref: (k, r, 0))],
            out_specs=BS((None, tr, n), lambda k, r, c_ref: (k, r, 0))),
        compiler_params=_params("arbitrary", "arbitrary"),
    )(c_idx, g, recv)


def _add_chips_call(p, recv, chip_idx):
    _, h, n = p.shape
    tr = h // 8

    def body(k_ref, p_ref, r_ref, o_ref):
        acc = p_ref[...].astype(F32)
        for j in range(3):
            acc = acc + r_ref[j].astype(F32)
        o_ref[...] = acc

    return pl.pallas_call(
        body, name="rs_add_chips", out_shape=SDS((h, n), F32),
        grid_spec=pltpu.PrefetchScalarGridSpec(
            num_scalar_prefetch=1, grid=(8,),
            in_specs=[BS((None, tr, n), lambda r, k_ref: (k_ref[0], r, 0)),
                      BS((3, tr, n), lambda r, k_ref: (0, r, 0))],
            out_specs=BS((tr, n), lambda r, k_ref: (r, 0))),
        compiler_params=_params("arbitrary"),
    )(chip_idx, p, recv)


def _load_rows(wg_hbm, w_vmem, sem, off):
    cp = pltpu.make_async_copy(wg_hbm.at[:, pl.ds(off, w_vmem.shape[1]), :], w_vmem, sem)
    cp.start()
    return cp


def _mod_call(c, wg, b_ada):
    wc = R_ADA

    def body(c_ref, w_hbm, b_ref, mod_ref, cact_ref, w_vmem, sem):
        cp = _load_rows(w_hbm, w_vmem, sem, O_ADA)
        cv = c_ref[...]
        ca = cv * _sig(cv)
        cact_ref[...] = ca
        cb = jnp.broadcast_to(ca, (8, D)).astype(BF16)
        cp.wait()
        for k in range(N_CHIPS):
            mod_ref[:, k * wc:(k + 1) * wc] = _mm(cb, w_vmem[k], NT) + b_ref[:, k * wc:(k + 1) * wc]

    vm = BS(memory_space=pltpu.VMEM)
    return pl.pallas_call(
        body, name="adaln_mod", out_shape=(SDS((8, 6 * D), F32), SDS((1, D), F32)),
        in_specs=[vm, BS(memory_space=pl.ANY), vm], out_specs=(vm, vm),
        scratch_shapes=[pltpu.VMEM((N_CHIPS, R_ADA, D), BF16), pltpu.SemaphoreType.DMA],
        compiler_params=pltpu.CompilerParams(vmem_limit_bytes=VMEM_LIMIT),
    )(c, wg, b_ada)


def _fwd_in_call(x, mod, pre_tm, wg, b_in, pack):
    S = x.shape[0]
    ranges = [(O_BRA, 3 * R_BR)]

    def body(x_ref, mod_ref, g_ref, w_hbm, b_ref, pack_ref, p_ref, h_ref, wg_out, w_vmem, sem, send_sems, recv_sems):
        start, finish = _pack_gather(pack_ref, wg_out, send_sems, recv_sems, ranges)

        @pl.when(pl.program_id(0) == 0)
        def _():
            start()
            _load_rows(w_hbm, w_vmem, sem, O_IN).wait()

        xv = x_ref[...]
        r = lax.rsqrt(jnp.mean(xv * xv, axis=-1, keepdims=True) + EPS)
        h = xv * r * g_ref[...] * (1.0 + mod_ref[:, D:2 * D]) + mod_ref[:, 0:D]
        hb = h.astype(BF16)
        h_ref[...] = hb
        for k in range(IN_COLS // D):
            w_blk = w_vmem[k // 2, (k % 2) * D:(k % 2 + 1) * D, :]
            p_ref[:, k * D:(k + 1) * D] = _mm(hb, w_blk) + b_ref[:, k * D:(k + 1) * D]

        @pl.when(pl.program_id(0) == S // TM - 1)
        def _():
            finish()

    hbm = BS(memory_space=pl.ANY)
    return pl.pallas_call(
        body, name="fwd_in", grid=(S // TM,),
        out_shape=(SDS((S, IN_COLS), F32), SDS((S, D), BF16), SDS(wg.shape, wg.dtype)),
        in_specs=[BS((TM, D), lambda i: (i, 0)), BS((1, 6 * D), lambda i: (0, 0)), BS((1, D), lambda i: (0, 0)),
                  hbm, BS((1, IN_COLS), lambda i: (0, 0)), hbm],
        out_specs=(BS((TM, IN_COLS), lambda i: (i, 0)), BS((TM, D), lambda i: (i, 0)), hbm),
        scratch_shapes=[pltpu.VMEM((N_CHIPS, R_IN, D), BF16), pltpu.SemaphoreType.DMA] + _gather_sems(len(ranges)),
        input_output_aliases={3: 2},
        compiler_params=_params("arbitrary"),
    )(x, mod, pre_tm, wg, b_in, pack)


def _lower_bound(lg_ref):
    l0, l1 = lg_ref[0:1, :], lg_ref[1:2, :]
    mx = jnp.maximum(l0, l1)
    e0, e1 = jnp.exp(l0 - mx), jnp.exp(l1 - mx)
    return e0 / (e0 + e1)


def _tri_masks():
    ri = lax.broadcasted_iota(jnp.int32, (CHUNK, CHUNK), 0)
    ci = lax.broadcasted_iota(jnp.int32, (CHUNK, CHUNK), 1)
    return (ri >= ci).astype(F32), (ci >= ri).astype(F32)


def _cumsum_mm(tri, g):
    tb = tri.astype(BF16)
    hi = g.astype(BF16)
    r1 = g - hi.astype(F32)
    mid = r1.astype(BF16)
    lo = (r1 - mid.astype(F32)).astype(BF16)
    return _mm(tb, hi) + _mm(tb, mid) + _mm(tb, lo)


def _hg_gates(q_r, f_r, lb, tril):
    sq = _sig(q_r)
    q = q_r * sq
    sf = _sig(f_r)
    f = lb + (1.0 - lb) * sf
    k = 1.0 - f
    g = jnp.log(f)
    b = _cumsum_mm(tril, g)
    b_last = _rowsum(g)
    row = lax.broadcasted_iota(jnp.int32, g.shape, 0)
    ref = _rowsum(jnp.where(row < CHUNK // 2, g, 0.0))
    e = jnp.exp(b)
    eq = jnp.exp(jnp.minimum(b - ref, 80.0))
    ek = jnp.exp(jnp.minimum(ref - b, 80.0))
    dd = jnp.exp(b_last - b)
    return dict(sq=sq, q=q, sf=sf, f=f, k=k, e=e, eq=eq, ek=ek, dd=dd, elast=jnp.exp(b_last),
                qe=q * e, qt=q * eq, kt=k * ek, kd=k * dd)


def _hgrn_fwd_call(p, logits, gn, wg, pack):
    S = p.shape[0]
    ncb = TB // CHUNK
    ranges = [(O_FF1, R_FF)]

    def body(q_ref, f_ref, v_ref, og_ref, lg_ref, gn_ref, wg_in, pack_ref, o_ref, oa_ref, st_ref, wg_out,
             st_scr, send_sems, recv_sems):
        start, finish = _pack_gather(pack_ref, wg_out, send_sems, recv_sems, ranges)

        @pl.when(pl.program_id(0) == 0)
        def _():
            start()
            st_scr[...] = jnp.zeros_like(st_scr)

        lb = _lower_bound(lg_ref)
        tril, _ = _tri_masks()

        def chunk(ci, carry):
            rows = pl.ds(pl.multiple_of(ci * CHUNK, CHUNK), CHUNK)
            st_ref[ci] = st_scr[...]
            t = _hg_gates(q_ref[rows, :], f_ref[rows, :], lb, tril)
            v = v_ref[rows, :]
            for h in range(HEADS):
                sl = slice(h * DK, (h + 1) * DK)
                stp = st_scr[:, sl]
                vb = v[:, sl].astype(BF16)
                inter = _mm(t["qe"][:, sl].astype(BF16), stp.astype(BF16), NT)
                a = jnp.where(tril > 0.5, _mm(t["qt"][:, sl].astype(BF16), t["kt"][:, sl].astype(BF16), NT), 0.0)
                o = inter + _mm(a.astype(BF16), vb)
                st_scr[:, sl] = stp * t["elast"][:, sl] + _mm(vb, t["kd"][:, sl].astype(BF16), TN)
                oh = o * lax.rsqrt(jnp.mean(o * o, axis=-1, keepdims=True) + EPS)
                og = og_ref[rows, sl]
                o_ref[rows, sl] = o
                oa_ref[rows, sl] = (oh * gn_ref[:, sl] * (og * _sig(og))).astype(BF16)
            return carry

        lax.fori_loop(0, ncb, chunk, 0)

        @pl.when(pl.program_id(0) == S // TB - 1)
        def _():
            finish()

    col = lambda j: BS((TB, D), lambda i, j=j: (i, j))
    hbm = BS(memory_space=pl.ANY)
    return pl.pallas_call(
        body, name="hgrn_fwd", grid=(S // TB,),
        out_shape=(SDS((S, D), F32), SDS((S, D), BF16), SDS((S // CHUNK, DK, D), F32), SDS(wg.shape, wg.dtype)),
        in_specs=[col(0), col(1), col(2), col(3), BS((2, D), lambda i: (0, 0)), BS((1, D), lambda i: (0, 0)),
                  hbm, hbm],
        out_specs=(BS((TB, D), lambda i: (i, 0)), BS((TB, D), lambda i: (i, 0)),
                   BS((ncb, DK, D), lambda i: (i, 0, 0)), hbm),
        scratch_shapes=[pltpu.VMEM((DK, D), F32)] + _gather_sems(len(ranges)),
        input_output_aliases={6: 3},
        compiler_params=_params("arbitrary"),
    )(p, p, p, p, logits, gn, wg, pack)


def _layernorm_stats(uc):
    mu = jnp.mean(uc, axis=-1, keepdims=True)
    xc = uc - mu
    rs = lax.rsqrt(jnp.mean(xc * xc, axis=-1, keepdims=True) + EPS)
    return xc * rs, rs


EXT = HALO + TM + 8


def _fill_shifted(ext, shifted):
    for m in range(1, 8):
        shifted[m - 1] = ext[m:m + HALO + TM, :]


def _window(ext, shifted, s0, n):
    m = s0 % 8
    q = s0 - m
    return ext[q:q + n, :] if m == 0 else shifted[m - 1, q:q + n, :]


def _conv_fwd_call(p, dw, db, ln_g, ln_b, wg, pack):
    S = p.shape[0]
    ranges = [(O_FF2, R_FF)]

    def body(cv_ref, cg_ref, dw_ref, db_ref, g_ref, b_ref, wg_in, pack_ref, u_ref, uc_ref, cb_ref, wg_out,
             uext, ush, send_sems, recv_sems):
        start, finish = _pack_gather(pack_ref, wg_out, send_sems, recv_sems, ranges)

        @pl.when(pl.program_id(0) == 0)
        def _():
            start()
            uext[0:HALO, :] = jnp.zeros((HALO, D), F32)
            uext[HALO + TM:EXT, :] = jnp.zeros((EXT - HALO - TM, D), F32)

        u = cv_ref[...] * _sig(cg_ref[...])
        uext[HALO:HALO + TM, :] = u
        u_ref[...] = u
        _fill_shifted(uext, ush)
        for rb in range(TM // SUB):
            acc = jnp.broadcast_to(db_ref[...], (SUB, D))
            for j in range(CONV_K):
                s0 = HALO - (CONV_K - 1) + j + rb * SUB
                acc = acc + dw_ref[j:j + 1, :] * _window(uext, ush, s0, SUB)
            uc_ref[rb * SUB:(rb + 1) * SUB, :] = acc
            xh, _ = _layernorm_stats(acc)
            ln = xh * g_ref[...] + b_ref[...]
            cb_ref[rb * SUB:(rb + 1) * SUB, :] = (ln * _sig(ln)).astype(BF16)
        uext[0:HALO, :] = uext[TM:TM + HALO, :]

        @pl.when(pl.program_id(0) == S // TM - 1)
        def _():
            finish()

    vec = BS((1, D), lambda i: (0, 0))
    hbm = BS(memory_space=pl.ANY)
    return pl.pallas_call(
        body, name="conv_fwd", grid=(S // TM,),
        out_shape=(SDS((S, D), F32), SDS((S, D), F32), SDS((S, D), BF16), SDS(wg.shape, wg.dtype)),
        in_specs=[BS((TM, D), lambda i: (i, 4)), BS((TM, D), lambda i: (i, 5)),
                  BS((CONV_K, D), lambda i: (0, 0)), vec, vec, vec, hbm, hbm],
        out_specs=(BS((TM, D), lambda i: (i, 0)),) * 3 + (hbm,),
        scratch_shapes=[pltpu.VMEM((EXT, D), F32), pltpu.VMEM((7, HALO + TM, D), F32)] + _gather_sems(len(ranges)),
        input_output_aliases={6: 3},
        compiler_params=_params("arbitrary"),
    )(p, p, dw, db, ln_g, ln_b, wg, pack)


def _mm_rows(a, w_ref):
    acc = _mm(a[:, 0:R_BR], w_ref[0])
    for k in range(1, N_CHIPS):
        acc = acc + _mm(a[:, k * R_BR:(k + 1) * R_BR], w_ref[k])
    return acc


def _mm_rows_t(a, w_ref):
    return jnp.concatenate([_mm(a, w_ref[k], NT) for k in range(N_CHIPS)], axis=1)


def _br_spec(off):
    return BS((N_CHIPS, R_BR, D), lambda i: (0, off // R_BR, 0))


def _merge_fwd_call(oa, cb, p, x, mod, post_tm, pre_cm, wg):
    S = x.shape[0]

    def body(oa_ref, cb_ref, ga_ref, gb_ref, x_ref, mod_ref, post_ref, pre_ref, wa_ref, wb_ref, wo_ref,
             ya_ref, yb_ref, mg_ref, y_ref, x2_ref, h2_ref):
        ya = _mm_rows(oa_ref[...], wa_ref)
        yb = _mm_rows(cb_ref[...], wb_ref)
        ya_ref[...] = ya
        yb_ref[...] = yb
        mg = (_sig(ga_ref[...]) * ya + _sig(gb_ref[...]) * yb).astype(BF16)
        mg_ref[...] = mg
        y = _mm_rows(mg, wo_ref)
        y_ref[...] = y
        n = y * lax.rsqrt(jnp.mean(y * y, axis=-1, keepdims=True) + EPS) * post_ref[...]
        x2 = x_ref[...] + mod_ref[:, 2 * D:3 * D] * n
        x2_ref[...] = x2
        r2 = lax.rsqrt(jnp.mean(x2 * x2, axis=-1, keepdims=True) + EPS)
        h2 = x2 * r2 * pre_ref[...] * (1.0 + mod_ref[:, 4 * D:5 * D]) + mod_ref[:, 3 * D:4 * D]
        h2_ref[...] = h2.astype(BF16)

    tile = BS((TM, D), lambda i: (i, 0))
    vec = BS((1, D), lambda i: (0, 0))
    return pl.pallas_call(
        body, name="merge_fwd", grid=(S // TM,),
        out_shape=(SDS((S, D), F32), SDS((S, D), F32), SDS((S, D), BF16), SDS((S, D), F32), SDS((S, D), F32),
                   SDS((S, D), BF16)),
        in_specs=[tile, tile, BS((TM, D), lambda i: (i, 6)), BS((TM, D), lambda i: (i, 7)), tile,
                  BS((1, 6 * D), lambda i: (0, 0)), vec, vec, _br_spec(O_BRA), _br_spec(O_BRB), _br_spec(O_OUT)],
        out_specs=(tile,) * 6,
        compiler_params=_params("arbitrary"),
    )(oa, cb, p, p, x, mod, post_tm, pre_cm, wg, wg, wg)


def _ffn_call(h2, x2, target, mod, post_cm, pre_cm, wg):
    S = x2.shape[0]

    def body(h2_ref, x2_ref, t_ref, mod_ref, post_ref, pre_ref, w_hbm,
             z_ref, da_ref, dy2_ref, dx2_ref, acc_ref, w1_v, w2_v, ra_scr, sems):
        @pl.when(pl.program_id(0) == 0)
        def _():
            c1 = _load_rows(w_hbm, w1_v, sems.at[0], O_FF1)
            c2 = _load_rows(w_hbm, w2_v, sems.at[1], O_FF2)
            c1.wait()
            c2.wait()
            acc_ref[...] = jnp.zeros_like(acc_ref)

        h2 = h2_ref[...]
        for k in range(N_CHIPS):
            ra = jnp.maximum(_mm(h2, w1_v[k]), 0.0)
            ra_scr[:, k * D:(k + 1) * D] = ra
            z_ref[:, k * D:(k + 1) * D] = (ra * ra).astype(BF16)
        y2 = _mm(z_ref[:, 0:D], w2_v[0])
        for k in range(1, N_CHIPS):
            y2 = y2 + _mm(z_ref[:, k * D:(k + 1) * D], w2_v[k])
        ry = lax.rsqrt(jnp.mean(y2 * y2, axis=-1, keepdims=True) + EPS)
        yn = y2 * ry
        n = yn * post_ref[...]
        g2 = mod_ref[:, 5 * D:6 * D]
        x2 = x2_ref[...]
        err = x2 + g2 * n - t_ref[...]
        acc_ref[5:6, :] += _rowsum(err * err) * (0.5 / D)
        dout = err * (1.0 / D)
        acc_ref[0:1, :] += _rowsum(dout * n)
        dn = dout * g2
        acc_ref[1:2, :] += _rowsum(dn * yn)
        dyn = dn * post_ref[...]
        dy2 = (ry * (dyn - yn * jnp.mean(dyn * yn, axis=-1, keepdims=True))).astype(BF16)
        dy2_ref[...] = dy2
        for k in range(N_CHIPS):
            dz = _mm(dy2, w2_v[k], NT)
            da_ref[:, k * D:(k + 1) * D] = (dz * (2.0 * ra_scr[:, k * D:(k + 1) * D])).astype(BF16)
        dh2 = jnp.zeros((TM, D), F32)
        for k in range(N_CHIPS):
            dh2 = dh2 + _mm(da_ref[:, k * D:(k + 1) * D], w1_v[k], NT)
        r2 = lax.rsqrt(jnp.mean(x2 * x2, axis=-1, keepdims=True) + EPS)
        xn = x2 * r2
        yv = xn * pre_ref[...]
        acc_ref[2:3, :] += _rowsum(dh2)
        acc_ref[3:4, :] += _rowsum(dh2 * yv)
        dyv = dh2 * (1.0 + mod_ref[:, 4 * D:5 * D])
        acc_ref[4:5, :] += _rowsum(dyv * xn)
        dxn = dyv * pre_ref[...]
        dx2_ref[...] = dout + r2 * (dxn - xn * jnp.mean(dxn * xn, axis=-1, keepdims=True))

    tile = BS((TM, D), lambda i: (i, 0))
    wide = BS((TM, D_FF), lambda i: (i, 0))
    vec = BS((1, D), lambda i: (0, 0))
    return pl.pallas_call(
        body, name="ffn_fwd_bwd", grid=(S // TM,),
        out_shape=(SDS((S, D_FF), BF16), SDS((S, D_FF), BF16), SDS((S, D), BF16), SDS((S, D), F32),
                   SDS((8, D), F32)),
        in_specs=[tile, tile, tile, BS((1, 6 * D), lambda i: (0, 0)), vec, vec, BS(memory_space=pl.ANY)],
        out_specs=(wide, wide, tile, tile, BS((8, D), lambda i: (0, 0))),
        scratch_shapes=[pltpu.VMEM((N_CHIPS, R_FF, D), BF16), pltpu.VMEM((N_CHIPS, R_FF, D), BF16),
                        pltpu.VMEM((TM, D_FF), F32),
                        pltpu.SemaphoreType.DMA((2,))],
        compiler_params=_params("arbitrary"),
    )(h2, x2, target, mod, post_cm, pre_cm, wg)


def _merge_bwd_call(dx2, y, ya, yb, p, mod, post_tm, wg):
    S = y.shape[0]

    def body(dx2_ref, y_ref, ya_ref, yb_ref, ga_ref, gb_ref, mod_ref, post_ref, wa_ref, wb_ref, wo_ref,
             dy_ref, dya_ref, dyb_ref, doa_ref, dcb_ref, dpg_ref, acc_ref, bsum_ref):
        @pl.when(pl.program_id(0) == 0)
        def _():
            acc_ref[...] = jnp.zeros_like(acc_ref)
            bsum_ref[...] = jnp.zeros_like(bsum_ref)

        y = y_ref[...]
        ry = lax.rsqrt(jnp.mean(y * y, axis=-1, keepdims=True) + EPS)
        yn = y * ry
        dx2 = dx2_ref[...]
        acc_ref[0:1, :] += _rowsum(dx2 * (yn * post_ref[...]))
        dn = dx2 * mod_ref[:, 2 * D:3 * D]
        acc_ref[1:2, :] += _rowsum(dn * yn)
        dyn = dn * post_ref[...]
        dy = (ry * (dyn - yn * jnp.mean(dyn * yn, axis=-1, keepdims=True))).astype(BF16)
        dy_ref[...] = dy
        dmg = _mm_rows_t(dy, wo_ref)
        sa, sb = _sig(ga_ref[...]), _sig(gb_ref[...])
        dya = (dmg * sa).astype(BF16)
        dyb = (dmg * sb).astype(BF16)
        dya_ref[...] = dya
        dyb_ref[...] = dyb
        dga = dmg * ya_ref[...] * (sa * (1.0 - sa))
        dgb = dmg * yb_ref[...] * (sb * (1.0 - sb))
        dpg_ref[:, 0:D] = dga.astype(BF16)
        dpg_ref[:, D:2 * D] = dgb.astype(BF16)
        bsum_ref[:, 0:D] += _rowsum(dga)
        bsum_ref[:, D:2 * D] += _rowsum(dgb)
        doa_ref[...] = _mm_rows_t(dya, wa_ref)
        dcb_ref[...] = _mm_rows_t(dyb, wb_ref)

    tile = BS((TM, D), lambda i: (i, 0))
    vec = BS((1, D), lambda i: (0, 0))
    return pl.pallas_call(
        body, name="merge_bwd", grid=(S // TM,),
        out_shape=(SDS((S, D), BF16), SDS((S, D), BF16), SDS((S, D), BF16), SDS((S, D), F32), SDS((S, D), F32),
                   SDS((S, 2 * D), BF16), SDS((8, D), F32), SDS((1, 2 * D), F32)),
        in_specs=[tile, tile, tile, tile, BS((TM, D), lambda i: (i, 6)), BS((TM, D), lambda i: (i, 7)),
                  BS((1, 6 * D), lambda i: (0, 0)), vec, _br_spec(O_BRA), _br_spec(O_BRB), _br_spec(O_OUT)],
        out_specs=(tile, tile, tile, tile, tile, BS((TM, 2 * D), lambda i: (i, 0)),
                   BS((8, D), lambda i: (0, 0)), BS((1, 2 * D), lambda i: (0, 0))),
        compiler_params=_params("arbitrary"),
    )(dx2, y, ya, yb, p, p, mod, post_tm, wg, wg, wg)


def _hgrn_bwd_call(p, o, doa, st, logits, gn):
    S = p.shape[0]
    nb = S // TB
    ncb = TB // CHUNK

    def body(q_ref, f_ref, v_ref, og_ref, o_ref, doa_ref, st_ref, lg_ref, gn_ref,
             dp_ref, bsum_ref, dlg_ref, dgn_ref, dst_scr, dlb_scr, dqe_s, dqt_s, dkt_s, dkd_s, dv_s, dog_s, dble_s):
        i = pl.program_id(0)

        @pl.when(i == 0)
        def _():
            dst_scr[...] = jnp.zeros_like(dst_scr)
            dlb_scr[...] = jnp.zeros_like(dlb_scr)
            bsum_ref[...] = jnp.zeros_like(bsum_ref)
            dgn_ref[...] = jnp.zeros_like(dgn_ref)

        lb = _lower_bound(lg_ref)
        tril, triu = _tri_masks()

        def chunk(tt, carry):
            ci = ncb - 1 - tt
            rows = pl.ds(pl.multiple_of(ci * CHUNK, CHUNK), CHUNK)
            q_r, f_r = q_ref[rows, :], f_ref[rows, :]
            t = _hg_gates(q_r, f_r, lb, tril)
            v = v_ref[rows, :]
            for h in range(HEADS):
                sl = slice(h * DK, (h + 1) * DK)
                stp = st_ref[ci, :, sl]
                stb = stp.astype(BF16)
                qeb = t["qe"][:, sl].astype(BF16)
                qtb = t["qt"][:, sl].astype(BF16)
                ktb = t["kt"][:, sl].astype(BF16)
                kdb = t["kd"][:, sl].astype(BF16)
                vb = v[:, sl].astype(BF16)
                a = jnp.where(tril > 0.5, _mm(qtb, ktb, NT), 0.0)
                o_h = o_ref[rows, sl]
                rinv = lax.rsqrt(jnp.mean(o_h * o_h, axis=-1, keepdims=True) + EPS)
                oh = o_h * rinv
                og = og_ref[rows, sl]
                so = _sig(og)
                d_oa = doa_ref[rows, sl]
                don = d_oa * (og * so)
                dog_s[:, sl] = d_oa * (oh * gn_ref[:, sl]) * _dsilu(og, so)
                dgn_ref[:, sl] += _rowsum(don * oh)
                doh = don * gn_ref[:, sl]
                do = (rinv * (doh - oh * jnp.mean(doh * oh, axis=-1, keepdims=True))).astype(BF16)
                dqe_s[:, sl] = _mm(do, stb, NN)
                dstp = _mm(do, qeb, TN)
                dab = jnp.where(tril > 0.5, _mm(do, vb, NT), 0.0).astype(BF16)
                dqt_s[:, sl] = _mm(dab, ktb, NN)
                dkt_s[:, sl] = _mm(dab, qtb, TN)
                dstn = dst_scr[:, sl]
                dsb = dstn.astype(BF16)
                dkd_s[:, sl] = _mm(vb, dsb, NN)
                dv_s[:, sl] = _mm(a.astype(BF16), do, TN) + _mm(kdb, dsb, NT)
                el = t["elast"][:, sl]
                dst_scr[:, sl] = dstn * el + dstp
                dble_s[:, sl] = el * _rowsum(stp * dstn)
            dqe, dqt, dkt, dkd = dqe_s[...], dqt_s[...], dkt_s[...], dkd_s[...]
            dq = dqe * t["e"] + dqt * t["eq"]
            dk = dkt * t["ek"] + dkd * t["dd"]
            dkk = dkd * t["kd"]
            qt_r = t["qt"].astype(BF16).astype(F32)
            kt_r = t["kt"].astype(BF16).astype(F32)
            dbv = dqe * t["qe"] + dqt * qt_r - dkt * kt_r - dkk
            dg = _cumsum_mm(triu, dbv) + (_rowsum(dkk) + dble_s[...])
            df = dg / t["f"] - dk
            sf = t["sf"]
            dlb_scr[...] += _rowsum(df * (1.0 - sf))
            dqr = dq * _dsilu(q_r, t["sq"])
            dfr = df * (1.0 - lb) * (sf * (1.0 - sf))
            dvv, dog = dv_s[...], dog_s[...]
            dp_ref[rows, 0:D] = dqr.astype(BF16)
            dp_ref[rows, D:2 * D] = dfr.astype(BF16)
            dp_ref[rows, 2 * D:3 * D] = dvv.astype(BF16)
            dp_ref[rows, 3 * D:4 * D] = dog.astype(BF16)
            bsum_ref[:, 0:D] += _rowsum(dqr)
            bsum_ref[:, D:2 * D] += _rowsum(dfr)
            bsum_ref[:, 2 * D:3 * D] += _rowsum(dvv)
            bsum_ref[:, 3 * D:4 * D] += _rowsum(dog)
            return carry

        lax.fori_loop(0, ncb, chunk, 0)

        dl = dlb_scr[...] * lb * (1.0 - lb)
        dlg_ref[0:1, :] = dl
        dlg_ref[1:2, :] = -dl

    col = lambda j: BS((TB, D), lambda i, j=j: (nb - 1 - i, j))
    rev = BS((TB, D), lambda i: (nb - 1 - i, 0))
    cd = pltpu.VMEM((CHUNK, D), F32)
    return pl.pallas_call(
        body, name="hgrn_bwd", grid=(nb,),
        out_shape=(SDS((S, 4 * D), BF16), SDS((1, 4 * D), F32), SDS((2, D), F32), SDS((1, D), F32)),
        in_specs=[col(0), col(1), col(2), col(3), rev, rev, BS((ncb, DK, D), lambda i: (nb - 1 - i, 0, 0)),
                  BS((2, D), lambda i: (0, 0)), BS((1, D), lambda i: (0, 0))],
        out_specs=(BS((TB, 4 * D), lambda i: (nb - 1 - i, 0)), BS((1, 4 * D), lambda i: (0, 0)),
                   BS((2, D), lambda i: (0, 0)), BS((1, D), lambda i: (0, 0))),
        scratch_shapes=[pltpu.VMEM((DK, D), F32), pltpu.VMEM((1, D), F32), cd, cd, cd, cd, cd, cd,
                        pltpu.VMEM((1, D), F32)],
        compiler_params=_params("arbitrary"),
    )(p, p, p, p, o, doa, st, logits, gn)


def _conv_bwd_call(dcb, uc, u, p, dw, ln_g, ln_b):
    S = uc.shape[0]
    nb = S // TM
    hb = TM // HALO

    def body(dcb_ref, uc_ref, u_ref, uh_ref, cv_ref, cg_ref, dw_ref, g_ref, b_ref,
             dp_ref, bsum_ref, ddw_ref, acc_ref, uext, dext, ush, dsh):
        i = pl.program_id(0)

        @pl.when(i == 0)
        def _():
            dext[TM:EXT, :] = jnp.zeros((EXT - TM, D), F32)
            uext[HALO + TM:EXT, :] = jnp.zeros((EXT - HALO - TM, D), F32)
            bsum_ref[...] = jnp.zeros_like(bsum_ref)
            ddw_ref[...] = jnp.zeros_like(ddw_ref)
            acc_ref[...] = jnp.zeros_like(acc_ref)

        first_tile = (nb - 1 - i) == 0
        uext[0:HALO, :] = jnp.where(first_tile, 0.0, uh_ref[...])
        uext[HALO:HALO + TM, :] = u_ref[...]
        _fill_shifted(uext, ush)

        for rb in range(TM // SUB):
            rs_ = slice(rb * SUB, (rb + 1) * SUB)
            xh, rs = _layernorm_stats(uc_ref[rs_, :])
            ln = xh * g_ref[...] + b_ref[...]
            dln = dcb_ref[rs_, :] * _dsilu(ln, _sig(ln))
            acc_ref[1:2, :] += _rowsum(dln * xh)
            acc_ref[2:3, :] += _rowsum(dln)
            dxh = dln * g_ref[...]
            duc = rs * (dxh - jnp.mean(dxh, axis=-1, keepdims=True)
                        - xh * jnp.mean(dxh * xh, axis=-1, keepdims=True))
            dext[rs_, :] = duc
            acc_ref[0:1, :] += _rowsum(duc)
        _fill_shifted(dext, dsh)

        for j in range(CONV_K):
            part = jnp.zeros((SUB, D), F32)
            for rb in range(TM // SUB):
                s0 = HALO - (CONV_K - 1) + j + rb * SUB
                part = part + dext[rb * SUB:(rb + 1) * SUB, :] * _window(uext, ush, s0, SUB)
            ddw_ref[j:j + 1, :] += _rowsum(part)

        for rb in range(TM // SUB):
            rs_ = slice(rb * SUB, (rb + 1) * SUB)
            du = jnp.zeros((SUB, D), F32)
            for j in range(CONV_K):
                s0 = rb * SUB + (CONV_K - 1) - j
                du = du + dw_ref[j:j + 1, :] * _window(dext, dsh, s0, SUB)
            cg = cg_ref[rs_, :]
            sg = _sig(cg)
            dcv = du * sg
            dcg = du * cv_ref[rs_, :] * (sg * (1.0 - sg))
            dp_ref[rs_, 0:D] = dcv.astype(BF16)
            dp_ref[rs_, D:2 * D] = dcg.astype(BF16)
            bsum_ref[:, 0:D] += _rowsum(dcv)
            bsum_ref[:, D:2 * D] += _rowsum(dcg)

        dext[TM:TM + HALO, :] = dext[0:HALO, :]

    rev = BS((TM, D), lambda i: (nb - 1 - i, 0))
    vec = BS((1, D), lambda i: (0, 0))
    return pl.pallas_call(
        body, name="conv_bwd", grid=(nb,),
        out_shape=(SDS((S, 2 * D), BF16), SDS((1, 2 * D), F32), SDS((32, D), F32), SDS((8, D), F32)),
        in_specs=[rev, rev, rev, BS((HALO, D), lambda i: (jnp.maximum((nb - 1 - i) * hb - 1, 0), 0)),
                  BS((TM, D), lambda i: (nb - 1 - i, 4)), BS((TM, D), lambda i: (nb - 1 - i, 5)),
                  BS((CONV_K, D), lambda i: (0, 0)), vec, vec],
        out_specs=(BS((TM, 2 * D), lambda i: (nb - 1 - i, 0)), BS((1, 2 * D), lambda i: (0, 0)),
                   BS((32, D), lambda i: (0, 0)), BS((8, D), lambda i: (0, 0))),
        scratch_shapes=[pltpu.VMEM((EXT, D), F32), pltpu.VMEM((EXT, D), F32),
                        pltpu.VMEM((7, HALO + TM, D), F32), pltpu.VMEM((7, HALO + TM, D), F32)],
        compiler_params=_params("arbitrary"),
    )(dcb, uc, u, u, p, p, dw, ln_g, ln_b)


def _in_bwd_call(dp_hg, dp_cv, dp_gt, x, dx2, mod, pre_tm, wg):
    S = x.shape[0]

    def body(hg_ref, cv_ref, gt_ref, x_ref, dx2_ref, mod_ref, g_ref, w_hbm, gx_ref, acc_ref, w_vmem, sem):
        @pl.when(pl.program_id(0) == 0)
        def _():
            _load_rows(w_hbm, w_vmem, sem, O_IN).wait()
            acc_ref[...] = jnp.zeros_like(acc_ref)

        dh = jnp.zeros((TM, D), F32)
        for k in range(IN_COLS // D):
            src, kk = ((hg_ref, k), (cv_ref, k - 4), (gt_ref, k - 6))[0 if k < 4 else (1 if k < 6 else 2)]
            dh = dh + _mm(src[:, kk * D:(kk + 1) * D], w_vmem[k // 2, (k % 2) * D:(k % 2 + 1) * D, :], NT)
        xv = x_ref[...]
        r = lax.rsqrt(jnp.mean(xv * xv, axis=-1, keepdims=True) + EPS)
        xn = xv * r
        yv = xn * g_ref[...]
        acc_ref[0:1, :] += _rowsum(dh)
        acc_ref[1:2, :] += _rowsum(dh * yv)
        dyv = dh * (1.0 + mod_ref[:, D:2 * D])
        acc_ref[2:3, :] += _rowsum(dyv * xn)
        dxn = dyv * g_ref[...]
        gx_ref[...] = dx2_ref[...] + r * (dxn - xn * jnp.mean(dxn * xn, axis=-1, keepdims=True))

    tile = BS((TM, D), lambda i: (i, 0))
    return pl.pallas_call(
        body, name="in_bwd", grid=(S // TM,),
        out_shape=(SDS((S, D), F32), SDS((8, D), F32)),
        in_specs=[BS((TM, 4 * D), lambda i: (i, 0)), BS((TM, 2 * D), lambda i: (i, 0)),
                  BS((TM, 2 * D), lambda i: (i, 0)), tile, tile, BS((1, 6 * D), lambda i: (0, 0)),
                  BS((1, D), lambda i: (0, 0)), BS(memory_space=pl.ANY)],
        out_specs=(tile, BS((8, D), lambda i: (0, 0))),
        scratch_shapes=[pltpu.VMEM((N_CHIPS, R_IN, D), BF16), pltpu.SemaphoreType.DMA],
        compiler_params=_params("arbitrary"),
    )(dp_hg, dp_cv, dp_gt, x, dx2, mod, pre_tm, wg)


def _wgrad_call(gp, a, b, name, bm, place):
    S, M = a.shape
    N = b.shape[1]
    bk = min(S, 1024)
    nk = S // bk

    def body(a_ref, b_ref, *rest):
        o_ref, acc = rest[-2], rest[-1]
        k = pl.program_id(2)

        @pl.when(k == 0)
        def _():
            acc[...] = jnp.zeros_like(acc)

        acc[...] += _mm(a_ref[...], b_ref[...], TN)

        @pl.when(k == nk - 1)
        def _():
            o_ref[...] = acc[...].astype(BF16)

    in_specs = [BS((bk, bm), lambda i, j, k: (k, i)), BS((bk, D), lambda i, j, k: (k, j))]
    args = [a, b]
    if gp is not None:
        in_specs.append(BS(memory_space=pl.ANY))
        args.append(gp)
    return pl.pallas_call(
        body, name=name, grid=(M // bm, N // D, nk),
        out_shape=SDS((N_CHIPS, PACK_G, D), BF16),
        in_specs=in_specs,
        out_specs=BS((None, bm, D), lambda i, j, k: (*place(i, j), 0)),
        scratch_shapes=[pltpu.VMEM((bm, D), F32)],
        input_output_aliases={} if gp is None else {2: 0},
        compiler_params=_params("parallel", "parallel", "arbitrary"),
    )(*args)


def _outer_call(cact, dmod):
    n = dmod.shape[1]

    def body(a_ref, b_ref, o_ref):
        o_ref[...] = _mm(a_ref[...], b_ref[...], TN, HI)

    return pl.pallas_call(
        body, name="wgrad_ada", out_shape=SDS((D, n), F32),
        compiler_params=pltpu.CompilerParams(vmem_limit_bytes=VMEM_LIMIT),
    )(cact, dmod)


def _adamw_call(w, g, m, v, name):
    R, C = w.shape
    tr = R
    while tr * C > 512 * 1024 and tr % 16 == 0:
        tr //= 2
    c1 = 1.0 - ADAM_B1 ** ADAM_STEP
    c2 = 1.0 - ADAM_B2 ** ADAM_STEP

    def body(w_ref, g_ref, m_ref, v_ref, d_ref, m2_ref, v2_ref):
        g = g_ref[...]
        m2 = ADAM_B1 * m_ref[...] + (1.0 - ADAM_B1) * g
        v2 = ADAM_B2 * v_ref[...] + (1.0 - ADAM_B2) * (g * g)
        m2_ref[...] = m2
        v2_ref[...] = v2
        d_ref[...] = -ADAM_LR * ((m2 / c1) / (jnp.sqrt(v2 / c2) + ADAM_EPS) + ADAM_WD * w_ref[...])

    tile = BS((tr, C), lambda i: (i, 0))
    return pl.pallas_call(
        body, name=name, grid=(R // tr,), out_shape=(SDS((R, C), F32),) * 3,
        in_specs=[tile] * 4, out_specs=(tile,) * 3, compiler_params=_params("parallel"),
    )(w, g, m, v)


def _local_step(x, c, target, wg, pack, small):
    mod8, cact = _mod_call(c, wg, small["b_ada"])
    mod = mod8[0:1]
    p, h1, wg = _fwd_in_call(x, mod, small["pre_tm"], wg, small["b_in"], pack)
    o, oa, st, wg = _hgrn_fwd_call(p, small["logits"], small["hg_norm"], wg, pack)
    u, uc, cb, wg = _conv_fwd_call(p, small["conv_dw"], small["conv_db"], small["ln_g"], small["ln_b"], wg, pack)
    ya, yb, mg, y, x2, h2 = _merge_fwd_call(oa, cb, p, x, mod, small["post_tm"], small["pre_cm"], wg)
    z, da, dy2, dx2, acc_f = _ffn_call(h2, x2, target, mod, small["post_cm"], small["pre_cm"], wg)
    dy, dya, dyb, doa, dcb, dp_gt, acc_m, bs_gt = _merge_bwd_call(dx2, y, ya, yb, p, mod, small["post_tm"], wg)
    dp_hg, bs_hg, dlg, dgn = _hgrn_bwd_call(p, o, doa, st, small["logits"], small["hg_norm"])
    dp_cv, bs_cv, ddw, acc_c = _conv_bwd_call(dcb, uc, u, p, small["conv_dw"], small["ln_g"], small["ln_b"])
    gx, acc_i = _in_bwd_call(dp_hg, dp_cv, dp_gt, x, dx2, mod, small["pre_tm"], wg)

    gp = _wgrad_call(None, h2, da, "wgrad_ff1", D, lambda i, j: (j, O_FF1 // D))
    gp = _wgrad_call(gp, z, dy2, "wgrad_ff2", D, lambda i, j: (i, O_FF2 // D))
    gp = _wgrad_call(gp, oa, dya, "wgrad_br_a", R_BR, lambda i, j: (i, O_BRA // R_BR))
    gp = _wgrad_call(gp, cb, dyb, "wgrad_br_b", R_BR, lambda i, j: (i, O_BRB // R_BR))
    gp = _wgrad_call(gp, mg, dy, "wgrad_out", R_BR, lambda i, j: (i, O_OUT // R_BR))
    gp = _wgrad_call(gp, h1, dp_hg, "wgrad_in_hg", D, lambda i, j: (j // 2, j % 2))
    gp = _wgrad_call(gp, h1, dp_cv, "wgrad_in_cv", D, lambda i, j: (2, j))
    gp = _wgrad_call(gp, h1, dp_gt, "wgrad_in_gt", D, lambda i, j: (3, j))
    zrow = jnp.zeros((1, D), F32)
    rows = [acc_i[0:1], acc_i[1:2], acc_m[0:1], acc_f[2:3], acc_f[3:4], acc_f[0:1],
            acc_i[2:3], acc_m[1:2], acc_f[4:5], acc_f[1:2],
            jnp.concatenate([bs_hg, bs_cv, bs_gt], axis=1).reshape(8, D),
            dlg, dgn, acc_c[0:1], acc_c[1:2], acc_c[2:3],
            ddw,
            cact, acc_f[5:6]] + [zrow] * 6
    return gx, jnp.concatenate(rows, axis=0), gp


def kernel(x, c, w_ada, b_ada, pre_norm_tm, post_norm_tm, pre_norm_cm, post_norm_cm, w_in, b_in, hg_lb_logits, hg_norm, conv_dw, conv_db, conv_ln_g, conv_ln_b, w_br_a, w_br_b, w_out, w_ff1, w_ff2, loss_target, m_w_ada, m_b_ada, m_pre_norm_tm, m_post_norm_tm, m_pre_norm_cm, m_post_norm_cm, m_w_in, m_b_in, m_hg_lb_logits, m_hg_norm, m_conv_dw, m_conv_db, m_conv_ln_g, m_conv_ln_b, m_w_br_a, m_w_br_b, m_w_out, m_w_ff1, m_w_ff2, v_w_ada, v_b_ada, v_pre_norm_tm, v_post_norm_tm, v_pre_norm_cm, v_post_norm_cm, v_w_in, v_b_in, v_hg_lb_logits, v_hg_norm, v_conv_dw, v_conv_db, v_conv_ln_g, v_conv_ln_b, v_w_br_a, v_w_br_b, v_w_out, v_w_ff1, v_w_ff2):
    xi, yi, ci = lax.axis_index("x"), lax.axis_index("y"), lax.axis_index("c")
    chip = 2 * xi + yi
    c_idx = jnp.reshape(ci, (1,)).astype(jnp.int32)
    chip_idx = jnp.reshape(chip, (1,)).astype(jnp.int32)

    def pack_small(ada_b, pre_t, post_t, pre_c, post_c, in_b, lg, hgn, cdb, lng, lnb, cdw):
        flat = jnp.concatenate([cdw[0].reshape(-1), jnp.zeros((8 * D - CONV_K * 256,), F32)]).reshape(8, D)
        return jnp.concatenate([ada_b.reshape(6, D), pre_t, post_t, pre_c, post_c, in_b.reshape(8, D), lg, hgn,
                                cdb, lng, lnb, flat], axis=0)

    w_in_halves = w_in[0].reshape(D, 2, D).transpose(1, 0, 2).reshape(R_IN, D)
    pack = jnp.concatenate([w_in_halves, w_ff1[0], w_ff2[0], w_br_a[0], w_br_b[0], w_out[0], w_ada[0].T],
                           axis=0).astype(BF16)
    wg = lax.dynamic_update_slice(jnp.zeros((N_CHIPS, PACK_W, D), BF16), pack[None], (chip, 0, 0))
    wg = _gather_first_call(pack, wg)
    dw_blk = jnp.concatenate([conv_dw[0].reshape(-1), jnp.zeros((8 * D - CONV_K * 256,), F32)]).reshape(8, D)
    dw_all = _allgather_call(dw_blk, "gather_conv_dw", in_vmem=True, with_sum=False)[0]
    dw_all = dw_all.reshape(N_CHIPS, 2, 8 * D)[:, 0, :CONV_K * 256].reshape(N_CHIPS, CONV_K, 256)
    dw_full = dw_all.transpose(1, 0, 2).reshape(CONV_K, D)

    small = dict(b_ada=b_ada, pre_tm=pre_norm_tm, post_tm=post_norm_tm, pre_cm=pre_norm_cm, post_cm=post_norm_cm,
                 b_in=b_in, logits=hg_lb_logits, hg_norm=hg_norm, conv_dw=dw_full, conv_db=conv_db,
                 ln_g=conv_ln_g, ln_b=conv_ln_b)

    gx, srows, gp = _local_step(x[0], c, loss_target[0], wg, pack, small)

    sall, ssum = _allgather_call(srows, "gather_small", in_vmem=True, with_sum=True)
    sall = sall.reshape(N_DEV, SMALL_ROWS, D)
    loss = jnp.sum(ssum[57])
    dmod_all = sall[:, 0:6, :].reshape(N_DEV, 6 * D)
    wa = 6 * D // N_CHIPS
    g_ada = _outer_call(sall[:, 56, :], lax.dynamic_slice_in_dim(dmod_all, chip * wa, wa, axis=1))
    g_dw = lax.dynamic_slice_in_dim(ssum[24:24 + CONV_K], chip * 256, 256, axis=1)
    g_small = jnp.concatenate(
        [ssum[0:24], jnp.concatenate([g_dw.reshape(-1), jnp.zeros((8 * D - CONV_K * 256,), F32)]).reshape(8, D)],
        axis=0)

    gp = gp.reshape(N_CHIPS, 2, PACK_G // 2, D)
    part = _add_halves_call(gp, _sibling_halves_call(gp), c_idx)
    red = _add_chips_call(part, _chip_exchange_call(part), chip_idx)
    other = _sibling_join_call(red)
    g_big = jnp.where(ci == 0, jnp.concatenate([red, other], axis=0), jnp.concatenate([other, red], axis=0))

    shapes = {"in": w_in.shape, "br_a": w_br_a.shape, "br_b": w_br_b.shape, "out": w_out.shape,
              "ff1": w_ff1.shape, "ff2": w_ff2.shape}
    offs = {"in": (O_IN, O_FF1), "ff1": (O_FF1, O_FF2), "ff2": (O_FF2, O_BRA), "br_a": (O_BRA, O_BRB),
            "br_b": (O_BRB, O_OUT), "out": (O_OUT, PACK_G)}
    wmv = {"in": (w_in, m_w_in, v_w_in), "br_a": (w_br_a, m_w_br_a, v_w_br_a), "br_b": (w_br_b, m_w_br_b, v_w_br_b),
           "out": (w_out, m_w_out, v_w_out), "ff1": (w_ff1, m_w_ff1, v_w_ff1), "ff2": (w_ff2, m_w_ff2, v_w_ff2)}
    res = {}
    for n in offs:
        shp = shapes[n]
        g2d = g_big[offs[n][0]:offs[n][1]]
        if n == "in":
            g2d = g2d.reshape(2, D, D).transpose(1, 0, 2)
        g2d = g2d.reshape(shp[1], shp[2])
        w_, m_, v_ = (a[0] for a in wmv[n])
        d_, m2_, v2_ = _adamw_call(w_, g2d, m_, v_, "adamw_" + n)
        res[n] = tuple(a.reshape(shp) for a in (g2d, d_, m2_, v2_))
    d_, m2_, v2_ = _adamw_call(w_ada[0], g_ada, m_w_ada[0], v_w_ada[0], "adamw_ada")
    res["ada"] = tuple(a.reshape(w_ada.shape) for a in (g_ada, d_, m2_, v2_))

    ws = pack_small(b_ada, pre_norm_tm, post_norm_tm, pre_norm_cm, post_norm_cm, b_in, hg_lb_logits, hg_norm,
                    conv_db, conv_ln_g, conv_ln_b, conv_dw)
    ms = pack_small(m_b_ada, m_pre_norm_tm, m_post_norm_tm, m_pre_norm_cm, m_post_norm_cm, m_b_in, m_hg_lb_logits,
                    m_hg_norm, m_conv_db, m_conv_ln_g, m_conv_ln_b, m_conv_dw)
    vs = pack_small(v_b_ada, v_pre_norm_tm, v_post_norm_tm, v_pre_norm_cm, v_post_norm_cm, v_b_in, v_hg_lb_logits,
                    v_hg_norm, v_conv_db, v_conv_ln_g, v_conv_ln_b, v_conv_dw)
    sres = (g_small,) + tuple(_adamw_call(ws, g_small, ms, vs, "adamw_small"))

    def unpack_small(t):
        return {"b_ada": t[0:6].reshape(1, 6 * D), "pre_tm": t[6:7], "post_tm": t[7:8], "pre_cm": t[8:9],
                "post_cm": t[9:10], "b_in": t[10:18].reshape(1, IN_COLS), "logits": t[18:20], "hg_norm": t[20:21],
                "conv_db": t[21:22], "ln_g": t[22:23], "ln_b": t[23:24],
                "conv_dw": t[24:32].reshape(-1)[:CONV_K * 256].reshape(1, CONV_K, 256)}

    order = ["ada", "b_ada", "pre_tm", "post_tm", "pre_cm", "post_cm", "in", "b_in", "logits", "hg_norm", "conv_dw",
             "conv_db", "ln_g", "ln_b", "br_a", "br_b", "out", "ff1", "ff2"]
    outs = [loss, gx.reshape(x.shape)]
    for kind in range(4):
        sm = unpack_small(sres[kind])
        for n in order:
            outs.append(res[n][kind] if n in res else sm[n])
    return tuple(outs)
```

```python
import functools

import jax
import jax.numpy as jnp
from jax import lax
from jax.experimental import pallas as pl
from jax.experimental.pallas import tpu as pltpu

F32, BF16 = jnp.float32, jnp.bfloat16
SDS = jax.ShapeDtypeStruct
BS = pl.BlockSpec
MESH = pl.DeviceIdType.MESH
HI = lax.Precision.HIGHEST

D = 1024
D_FF = 4096
IN_COLS = 8192
HEADS, DK = 8, 128
CHUNK = 128
CONV_K = 31
HALO = 32
SUB = 32
EPS = 1e-6
N_CHIPS, N_DEV = 4, 8
TM = 256
TB = 256
VMEM_LIMIT = 56 * 1024 * 1024

R_IN, R_BR, R_FF, R_ADA = 2048, 256, 1024, 1536
PACK_W = R_IN + 3 * R_BR + 2 * R_FF + R_ADA
PACK_G = R_IN + 3 * R_BR + 2 * R_FF
O_IN, O_FF1, O_FF2, O_BRA, O_BRB, O_OUT, O_ADA = 0, 2048, 3072, 4096, 4352, 4608, 4864
SMALL_ROWS = 64

ADAM_LR, ADAM_B1, ADAM_B2, ADAM_EPS, ADAM_WD, ADAM_STEP = 0.001, 0.9, 0.999, 1e-08, 0.01, 10

NN = (((1,), (0,)), ((), ()))
NT = (((1,), (1,)), ((), ()))
TN = (((0,), (0,)), ((), ()))


def _mm(a, b, dims=NN, precision=None):
    return lax.dot_general(a, b, dims, preferred_element_type=F32, precision=precision)


def _sig(v):
    return jax.nn.sigmoid(v)


def _dsilu(v, s):
    return s * (1.0 + v * (1.0 - s))


def _params(*sem):
    return pltpu.CompilerParams(dimension_semantics=sem if sem else None, vmem_limit_bytes=VMEM_LIMIT)


def _rowsum(v):
    return jnp.sum(v, axis=0, keepdims=True)


def _mesh_pos():
    return lax.axis_index("x"), lax.axis_index("y"), lax.axis_index("c")


def _allgather_call(blk, name, in_vmem, with_sum):
    m_per, n = blk.shape

    def body(x_ref, out_ref, *rest):
        if with_sum:
            sum_ref, send_sems, recv_sems, local_sem = rest
        else:
            send_sems, recv_sems, local_sem = rest
        x, y, c = _mesh_pos()
        me, sibling = (x, y, c), (x, y, 1 - c)
        chips = [(1 - x, y), (x, 1 - y), (1 - x, 1 - y)]

        def rows(px, py, pc):
            return out_ref.at[pl.ds((4 * px + 2 * py + pc) * m_per, m_per), :]

        def copy(k, block, to, src=None):
            return pltpu.make_async_remote_copy(
                src_ref=rows(*block) if src is None else src, dst_ref=rows(*block),
                send_sem=send_sems.at[k], recv_sem=recv_sems.at[k], device_id=to, device_id_type=MESH)

        mine = pltpu.make_async_copy(x_ref, rows(*me), local_sem)
        mine.start()
        first = [copy(0, me, sibling, src=x_ref)]
        first += [copy(1 + j, me, (*chip, c), src=x_ref) for j, chip in enumerate(chips)]
        for cp in first:
            cp.start()
        passed = [copy(4 + j, (*chip, c), sibling) for j, chip in enumerate(chips)]
        for j, chip in enumerate(chips):
            copy(1 + j, (*chip, c), me).wait_recv()
            passed[j].start()
        copy(0, sibling, me).wait_recv()
        for j, chip in enumerate(chips):
            copy(4 + j, (*chip, 1 - c), me).wait_recv()
        for cp in first + passed:
            cp.wait_send()
        mine.wait()
        if with_sum:
            acc = out_ref[0:m_per, :]
            for d in range(1, N_DEV):
                acc = acc + out_ref[d * m_per:(d + 1) * m_per, :]
            sum_ref[...] = acc

    space = pltpu.VMEM if in_vmem else pl.ANY
    out_shape = [SDS((N_DEV * m_per, n), blk.dtype)]
    out_specs = [BS(memory_space=space)]
    if with_sum:
        out_shape.append(SDS((m_per, n), blk.dtype))
        out_specs.append(BS(memory_space=pltpu.VMEM))
    return pl.pallas_call(
        body, name=name, out_shape=out_shape, in_specs=[BS(memory_space=space)], out_specs=out_specs,
        scratch_shapes=[pltpu.SemaphoreType.DMA((7,)), pltpu.SemaphoreType.DMA((7,)), pltpu.SemaphoreType.DMA],
        compiler_params=pltpu.CompilerParams(vmem_limit_bytes=VMEM_LIMIT),
    )(blk)


def _gather_sems(n_ranges):
    return [pltpu.SemaphoreType.DMA((6 * n_ranges,)), pltpu.SemaphoreType.DMA((6 * n_ranges,))]


def _pack_gather(pack_ref, wg_ref, send_sems, recv_sems, ranges):
    x, y, c = _mesh_pos()
    me, sibling = (x, y, c), (x, y, 1 - c)
    chips = [(1 - x, y), (x, 1 - y), (1 - x, 1 - y)]

    def land(r, px, py, pc):
        off, n = ranges[r]
        return wg_ref.at[2 * px + py, pl.ds(off + pc * (n // 2), n // 2), :]

    def mine(r):
        off, n = ranges[r]
        return pack_ref.at[pl.ds(off + c * (n // 2), n // 2), :]

    def copy(r, k, block, to, src=None):
        return pltpu.make_async_remote_copy(
            src_ref=land(r, *block) if src is None else src, dst_ref=land(r, *block),
            send_sem=send_sems.at[6 * r + k], recv_sem=recv_sems.at[6 * r + k], device_id=to, device_id_type=MESH)

    def start():
        for r in range(len(ranges)):
            for j, chip in enumerate(chips):
                copy(r, j, me, (*chip, c), src=mine(r)).start()

    def finish():
        for r in range(len(ranges)):
            for j, chip in enumerate(chips):
                copy(r, j, (*chip, c), me).wait_recv()
                copy(r, 3 + j, (*chip, c), sibling).start()
        for r in range(len(ranges)):
            for j, chip in enumerate(chips):
                copy(r, 3 + j, (*chip, 1 - c), me).wait_recv()
                copy(r, j, me, (*chip, c), src=mine(r)).wait_send()
                copy(r, 3 + j, (*chip, c), sibling).wait_send()

    return start, finish


def _gather_first_call(pack, wg):
    ranges = [(O_IN, R_IN), (O_ADA, R_ADA)]

    def body(pack_ref, wg_in, wg_out, send_sems, recv_sems):
        start, finish = _pack_gather(pack_ref, wg_out, send_sems, recv_sems, ranges)
        start()
        finish()

    hbm = BS(memory_space=pl.ANY)
    return pl.pallas_call(
        body, name="gather_w_in_ada", out_shape=SDS(wg.shape, wg.dtype), in_specs=[hbm, hbm], out_specs=hbm,
        scratch_shapes=_gather_sems(len(ranges)), input_output_aliases={1: 0},
    )(pack, wg)


def _sibling_halves_call(g, tag):
    _, _, h, n = g.shape

    def body(g_ref, out_ref, send_sems, recv_sems):
        x, y, c = _mesh_pos()
        cps = [pltpu.make_async_remote_copy(
            src_ref=g_ref.at[k, 1 - c], dst_ref=out_ref.at[k], send_sem=send_sems.at[k], recv_sem=recv_sems.at[k],
            device_id=(x, y, 1 - c), device_id_type=MESH) for k in range(N_CHIPS)]
        for cp in cps:
            cp.start()
        for cp in cps:
            cp.wait()

    return pl.pallas_call(
        body, name="rs_sibling_halves_" + tag, out_shape=SDS((N_CHIPS, h, n), g.dtype),
        in_specs=[BS(memory_space=pl.ANY)], out_specs=BS(memory_space=pl.ANY),
        scratch_shapes=[pltpu.SemaphoreType.DMA((N_CHIPS,)), pltpu.SemaphoreType.DMA((N_CHIPS,))],
    )(g)


def _chip_exchange(p_ref, out_ref, send_sems, recv_sems):
    x, y, c = _mesh_pos()
    chips = [(1 - x, y), (x, 1 - y), (1 - x, 1 - y)]

    def copies():
        return [pltpu.make_async_remote_copy(
            src_ref=p_ref.at[2 * cx + cy], dst_ref=out_ref.at[j], send_sem=send_sems.at[j], recv_sem=recv_sems.at[j],
            device_id=(cx, cy, c), device_id_type=MESH) for j, (cx, cy) in enumerate(chips)]

    def start():
        for cp in copies():
            cp.start()

    def finish():
        for cp in copies():
            cp.wait()

    return start, finish


def _exchange_sems():
    return [pltpu.SemaphoreType.DMA((3,)), pltpu.SemaphoreType.DMA((3,))]


def _sibling_join_call(r, tag):
    h, n = r.shape
    q = h // 4

    def body(r_ref, out_ref, send_sems, recv_sems):
        x, y, c = _mesh_pos()
        cps = [pltpu.make_async_remote_copy(
            src_ref=r_ref.at[pl.ds(k * q, q)], dst_ref=out_ref.at[pl.ds(k * q, q)],
            send_sem=send_sems.at[k], recv_sem=recv_sems.at[k],
            device_id=(x, y, 1 - c), device_id_type=MESH) for k in range(4)]
        for cp in cps:
            cp.start()
        for cp in cps:
            cp.wait()

    return pl.pallas_call(
        body, name="rs_sibling_join_" + tag, out_shape=SDS((h, n), r.dtype),
        in_specs=[BS(memory_space=pl.ANY)], out_specs=BS(memory_space=pl.ANY),
        scratch_shapes=[pltpu.SemaphoreType.DMA((4,)), pltpu.SemaphoreType.DMA((4,))],
    )(r)


def _add_halves_call(g, recv, c_idx, tag):
    _, _, h, n = g.shape
    tr = h // 8

    def body(c_ref, g_ref, r_ref, o_ref):
        o_ref[...] = (g_ref[...].astype(F32) + r_ref[...].astype(F32)).astype(BF16)

    return pl.pallas_call(
        body, name="rs_add_halves_" + tag, out_shape=SDS((N_CHIPS, h, n), BF16),
        grid_spec=pltpu.PrefetchScalarGridSpec(
            num_scalar_prefetch=1, grid=(N_CHIPS, 8),
            in_specs=[BS((None, None, tr, n), lambda k, r, c_ref: (k, c_ref[0], r, 0)),
                      BS((None, tr, n), lambda k, r, c_ref: (k, r, 0))],
            out_specs=BS((None, tr, n), lambda k, r, c_ref: (k, r, 0))),
        compiler_params=_params("arbitrary", "arbitrary"),
    )(c_idx, g, recv)


def _add_chips_call(p, recv, chip_idx, tag):
    _, h, n = p.shape
    tr = h // 8

    def body(k_ref, p_ref, r_ref, o_ref):
        acc = p_ref[...].astype(F32)
        for j in range(3):
            acc = acc + r_ref[j].astype(F32)
        o_ref[...] = acc

    return pl.pallas_call(
        body, name="rs_add_chips_" + tag, out_shape=SDS((h, n), F32),
        grid_spec=pltpu.PrefetchScalarGridSpec(
            num_scalar_prefetch=1, grid=(8,),
            in_specs=[BS((None, tr, n), lambda r, k_ref: (k_ref[0], r, 0)),
                      BS((3, tr, n), lambda r, k_ref: (0, r, 0))],
            out_specs=BS((tr, n), lambda r, k_ref: (r, 0))),
        compiler_params=_params("arbitrary"),
    )(chip_idx, p, recv)


def _load_rows(wg_hbm, w_vmem, sem, off):
    cp = pltpu.make_async_copy(wg_hbm.at[:, pl.ds(off, w_vmem.shape[1]), :], w_vmem, sem)
    cp.start()
    return cp


def _mod_call(c, wg, b_ada):
    wc = R_ADA

    def body(c_ref, w_hbm, b_ref, mod_ref, cact_ref, w_vmem, sem):
        cp = _load_rows(w_hbm, w_vmem, sem, O_ADA)
        cv = c_ref[...]
        ca = cv * _sig(cv)
        cact_ref[...] = ca
        cb = jnp.broadcast_to(ca, (8, D)).astype(BF16)
        cp.wait()
        for k in range(N_CHIPS):
            mod_ref[:, k * wc:(k + 1) * wc] = _mm(cb, w_vmem[k], NT) + b_ref[:, k * wc:(k + 1) * wc]

    vm = BS(memory_space=pltpu.VMEM)
    return pl.pallas_call(
        body, name="adaln_mod", out_shape=(SDS((8, 6 * D), F32), SDS((1, D), F32)),
        in_specs=[vm, BS(memory_space=pl.ANY), vm], out_specs=(vm, vm),
        scratch_shapes=[pltpu.VMEM((N_CHIPS, R_ADA, D), BF16), pltpu.SemaphoreType.DMA],
        compiler_params=pltpu.CompilerParams(vmem_limit_bytes=VMEM_LIMIT),
    )(c, wg, b_ada)


def _fwd_in_call(x, mod, pre_tm, wg, b_in, pack):
    S = x.shape[0]
    ranges = [(O_BRA, 3 * R_BR)]

    def body(x_ref, mod_ref, g_ref, w_hbm, b_ref, pack_ref, p_ref, h_ref, wg_out, w_vmem, sem, send_sems, recv_sems):
        start, finish = _pack_gather(pack_ref, wg_out, send_sems, recv_sems, ranges)

        @pl.when(pl.program_id(0) == 0)
        def _():
            start()
            _load_rows(w_hbm, w_vmem, sem, O_IN).wait()

        xv = x_ref[...]
        r = lax.rsqrt(jnp.mean(xv * xv, axis=-1, keepdims=True) + EPS)
        h = xv * r * g_ref[...] * (1.0 + mod_ref[:, D:2 * D]) + mod_ref[:, 0:D]
        hb = h.astype(BF16)
        h_ref[...] = hb
        for k in range(IN_COLS // D):
            w_blk = w_vmem[k // 2, (k % 2) * D:(k % 2 + 1) * D, :]
            p_ref[:, k * D:(k + 1) * D] = _mm(hb, w_blk) + b_ref[:, k * D:(k + 1) * D]

        @pl.when(pl.program_id(0) == S // TM - 1)
        def _():
            finish()

    hbm = BS(memory_space=pl.ANY)
    return pl.pallas_call(
        body, name="fwd_in", grid=(S // TM,),
        out_shape=(SDS((S, IN_COLS), F32), SDS((S, D), BF16), SDS(wg.shape, wg.dtype)),
        in_specs=[BS((TM, D), lambda i: (i, 0)), BS((1, 6 * D), lambda i: (0, 0)), BS((1, D), lambda i: (0, 0)),
                  hbm, BS((1, IN_COLS), lambda i: (0, 0)), hbm],
        out_specs=(BS((TM, IN_COLS), lambda i: (i, 0)), BS((TM, D), lambda i: (i, 0)), hbm),
        scratch_shapes=[pltpu.VMEM((N_CHIPS, R_IN, D), BF16), pltpu.SemaphoreType.DMA] + _gather_sems(len(ranges)),
        input_output_aliases={3: 2},
        compiler_params=_params("arbitrary"),
    )(x, mod, pre_tm, wg, b_in, pack)


def _lower_bound(lg_ref):
    l0, l1 = lg_ref[0:1, :], lg_ref[1:2, :]
    mx = jnp.maximum(l0, l1)
    e0, e1 = jnp.exp(l0 - mx), jnp.exp(l1 - mx)
    return e0 / (e0 + e1)


def _tri_masks():
    ri = lax.broadcasted_iota(jnp.int32, (CHUNK, CHUNK), 0)
    ci = lax.broadcasted_iota(jnp.int32, (CHUNK, CHUNK), 1)
    return (ri >= ci).astype(F32), (ci >= ri).astype(F32)


def _cumsum_mm(tri, g):
    tb = tri.astype(BF16)
    hi = g.astype(BF16)
    r1 = g - hi.astype(F32)
    mid = r1.astype(BF16)
    lo = (r1 - mid.astype(F32)).astype(BF16)
    return _mm(tb, hi) + _mm(tb, mid) + _mm(tb, lo)


def _hg_gates(q_r, f_r, lb, tril):
    sq = _sig(q_r)
    q = q_r * sq
    sf = _sig(f_r)
    f = lb + (1.0 - lb) * sf
    k = 1.0 - f
    g = jnp.log(f)
    b = _cumsum_mm(tril, g)
    b_last = _rowsum(g)
    row = lax.broadcasted_iota(jnp.int32, g.shape, 0)
    ref = _rowsum(jnp.where(row < CHUNK // 2, g, 0.0))
    e = jnp.exp(b)
    eq = jnp.exp(jnp.minimum(b - ref, 80.0))
    ek = jnp.exp(jnp.minimum(ref - b, 80.0))
    dd = jnp.exp(b_last - b)
    return dict(sq=sq, q=q, sf=sf, f=f, k=k, e=e, eq=eq, ek=ek, dd=dd, elast=jnp.exp(b_last),
                qe=q * e, qt=q * eq, kt=k * ek, kd=k * dd)


def _hgrn_fwd_call(p, logits, gn, wg, pack):
    S = p.shape[0]
    ncb = TB // CHUNK
    ranges = [(O_FF1, R_FF)]

    def body(q_ref, f_ref, v_ref, og_ref, lg_ref, gn_ref, wg_in, pack_ref, o_ref, oa_ref, st_ref, wg_out,
             st_scr, send_sems, recv_sems):
        start, finish = _pack_gather(pack_ref, wg_out, send_sems, recv_sems, ranges)

        @pl.when(pl.program_id(0) == 0)
        def _():
            start()
            st_scr[...] = jnp.zeros_like(st_scr)

        lb = _lower_bound(lg_ref)
        tril, _ = _tri_masks()

        def chunk(ci, carry):
            rows = pl.ds(pl.multiple_of(ci * CHUNK, CHUNK), CHUNK)
            st_ref[ci] = st_scr[...]
            t = _hg_gates(q_ref[rows, :], f_ref[rows, :], lb, tril)
            v = v_ref[rows, :]
            for h in range(HEADS):
                sl = slice(h * DK, (h + 1) * DK)
                stp = st_scr[:, sl]
                vb = v[:, sl].astype(BF16)
                inter = _mm(t["qe"][:, sl].astype(BF16), stp.astype(BF16), NT)
                a = jnp.where(tril > 0.5, _mm(t["qt"][:, sl].astype(BF16), t["kt"][:, sl].astype(BF16), NT), 0.0)
                o = inter + _mm(a.astype(BF16), vb)
                st_scr[:, sl] = stp * t["elast"][:, sl] + _mm(vb, t["kd"][:, sl].astype(BF16), TN)
                oh = o * lax.rsqrt(jnp.mean(o * o, axis=-1, keepdims=True) + EPS)
                og = og_ref[rows, sl]
                o_ref[rows, sl] = o
                oa_ref[rows, sl] = (oh * gn_ref[:, sl] * (og * _sig(og))).astype(BF16)
            return carry

        lax.fori_loop(0, ncb, chunk, 0)

        @pl.when(pl.program_id(0) == S // TB - 1)
        def _():
            finish()

    col = lambda j: BS((TB, D), lambda i, j=j: (i, j))
    hbm = BS(memory_space=pl.ANY)
    return pl.pallas_call(
        body, name="hgrn_fwd", grid=(S // TB,),
        out_shape=(SDS((S, D), F32), SDS((S, D), BF16), SDS((S // CHUNK, DK, D), F32), SDS(wg.shape, wg.dtype)),
        in_specs=[col(0), col(1), col(2), col(3), BS((2, D), lambda i: (0, 0)), BS((1, D), lambda i: (0, 0)),
                  hbm, hbm],
        out_specs=(BS((TB, D), lambda i: (i, 0)), BS((TB, D), lambda i: (i, 0)),
                   BS((ncb, DK, D), lambda i: (i, 0, 0)), hbm),
        scratch_shapes=[pltpu.VMEM((DK, D), F32)] + _gather_sems(len(ranges)),
        input_output_aliases={6: 3},
        compiler_params=_params("arbitrary"),
    )(p, p, p, p, logits, gn, wg, pack)


def _layernorm_stats(uc):
    mu = jnp.mean(uc, axis=-1, keepdims=True)
    xc = uc - mu
    rs = lax.rsqrt(jnp.mean(xc * xc, axis=-1, keepdims=True) + EPS)
    return xc * rs, rs


EXT = HALO + TM + 8


def _fill_shifted(ext, shifted):
    for m in range(1, 8):
        shifted[m - 1] = ext[m:m + HALO + TM, :]


def _window(ext, shifted, s0, n):
    m = s0 % 8
    q = s0 - m
    return ext[q:q + n, :] if m == 0 else shifted[m - 1, q:q + n, :]


def _conv_fwd_call(p, dw, db, ln_g, ln_b, wg, pack):
    S = p.shape[0]
    ranges = [(O_FF2, R_FF)]

    def body(cv_ref, cg_ref, dw_ref, db_ref, g_ref, b_ref, wg_in, pack_ref, u_ref, uc_ref, cb_ref, wg_out,
             uext, ush, send_sems, recv_sems):
        start, finish = _pack_gather(pack_ref, wg_out, send_sems, recv_sems, ranges)

        @pl.when(pl.program_id(0) == 0)
        def _():
            start()
            uext[0:HALO, :] = jnp.zeros((HALO, D), F32)
            uext[HALO + TM:EXT, :] = jnp.zeros((EXT - HALO - TM, D), F32)

        u = cv_ref[...] * _sig(cg_ref[...])
        uext[HALO:HALO + TM, :] = u
        u_ref[...] = u
        _fill_shifted(uext, ush)
        for rb in range(TM // SUB):
            acc = jnp.broadcast_to(db_ref[...], (SUB, D))
            for j in range(CONV_K):
                s0 = HALO - (CONV_K - 1) + j + rb * SUB
                acc = acc + dw_ref[j:j + 1, :] * _window(uext, ush, s0, SUB)
            uc_ref[rb * SUB:(rb + 1) * SUB, :] = acc
            xh, _ = _layernorm_stats(acc)
            ln = xh * g_ref[...] + b_ref[...]
            cb_ref[rb * SUB:(rb + 1) * SUB, :] = (ln * _sig(ln)).astype(BF16)
        uext[0:HALO, :] = uext[TM:TM + HALO, :]

        @pl.when(pl.program_id(0) == S // TM - 1)
        def _():
            finish()

    vec = BS((1, D), lambda i: (0, 0))
    hbm = BS(memory_space=pl.ANY)
    return pl.pallas_call(
        body, name="conv_fwd", grid=(S // TM,),
        out_shape=(SDS((S, D), F32), SDS((S, D), F32), SDS((S, D), BF16), SDS(wg.shape, wg.dtype)),
        in_specs=[BS((TM, D), lambda i: (i, 4)), BS((TM, D), lambda i: (i, 5)),
                  BS((CONV_K, D), lambda i: (0, 0)), vec, vec, vec, hbm, hbm],
        out_specs=(BS((TM, D), lambda i: (i, 0)),) * 3 + (hbm,),
        scratch_shapes=[pltpu.VMEM((EXT, D), F32), pltpu.VMEM((7, HALO + TM, D), F32)] + _gather_sems(len(ranges)),
        input_output_aliases={6: 3},
        compiler_params=_params("arbitrary"),
    )(p, p, dw, db, ln_g, ln_b, wg, pack)


def _mm_rows(a, w_ref):
    acc = _mm(a[:, 0:R_BR], w_ref[0])
    for k in range(1, N_CHIPS):
        acc = acc + _mm(a[:, k * R_BR:(k + 1) * R_BR], w_ref[k])
    return acc


def _mm_rows_t(a, w_ref):
    return jnp.concatenate([_mm(a, w_ref[k], NT) for k in range(N_CHIPS)], axis=1)


def _br_spec(off):
    return BS((N_CHIPS, R_BR, D), lambda i: (0, off // R_BR, 0))


def _merge_fwd_call(oa, cb, p, x, mod, post_tm, pre_cm, wg):
    S = x.shape[0]

    def body(oa_ref, cb_ref, ga_ref, gb_ref, x_ref, mod_ref, post_ref, pre_ref, wa_ref, wb_ref, wo_ref,
             ya_ref, yb_ref, mg_ref, y_ref, x2_ref, h2_ref):
        ya = _mm_rows(oa_ref[...], wa_ref)
        yb = _mm_rows(cb_ref[...], wb_ref)
        ya_ref[...] = ya
        yb_ref[...] = yb
        mg = (_sig(ga_ref[...]) * ya + _sig(gb_ref[...]) * yb).astype(BF16)
        mg_ref[...] = mg
        y = _mm_rows(mg, wo_ref)
        y_ref[...] = y
        n = y * lax.rsqrt(jnp.mean(y * y, axis=-1, keepdims=True) + EPS) * post_ref[...]
        x2 = x_ref[...] + mod_ref[:, 2 * D:3 * D] * n
        x2_ref[...] = x2
        r2 = lax.rsqrt(jnp.mean(x2 * x2, axis=-1, keepdims=True) + EPS)
        h2 = x2 * r2 * pre_ref[...] * (1.0 + mod_ref[:, 4 * D:5 * D]) + mod_ref[:, 3 * D:4 * D]
        h2_ref[...] = h2.astype(BF16)

    tile = BS((TM, D), lambda i: (i, 0))
    vec = BS((1, D), lambda i: (0, 0))
    return pl.pallas_call(
        body, name="merge_fwd", grid=(S // TM,),
        out_shape=(SDS((S, D), F32), SDS((S, D), F32), SDS((S, D), BF16), SDS((S, D), F32), SDS((S, D), F32),
                   SDS((S, D), BF16)),
        in_specs=[tile, tile, BS((TM, D), lambda i: (i, 6)), BS((TM, D), lambda i: (i, 7)), tile,
                  BS((1, 6 * D), lambda i: (0, 0)), vec, vec, _br_spec(O_BRA), _br_spec(O_BRB), _br_spec(O_OUT)],
        out_specs=(tile,) * 6,
        compiler_params=_params("arbitrary"),
    )(oa, cb, p, p, x, mod, post_tm, pre_cm, wg, wg, wg)


def _ffn_call(h2, x2, target, mod, post_cm, pre_cm, wg):
    S = x2.shape[0]

    def body(h2_ref, x2_ref, t_ref, mod_ref, post_ref, pre_ref, w_hbm,
             z_ref, da_ref, dy2_ref, dx2_ref, acc_ref, w1_v, w2_v, ra_scr, sems):
        @pl.when(pl.program_id(0) == 0)
        def _():
            c1 = _load_rows(w_hbm, w1_v, sems.at[0], O_FF1)
            c2 = _load_rows(w_hbm, w2_v, sems.at[1], O_FF2)
            c1.wait()
            c2.wait()
            acc_ref[...] = jnp.zeros_like(acc_ref)

        h2 = h2_ref[...]
        for k in range(N_CHIPS):
            ra = jnp.maximum(_mm(h2, w1_v[k]), 0.0)
            ra_scr[:, k * D:(k + 1) * D] = ra
            z_ref[:, k * D:(k + 1) * D] = (ra * ra).astype(BF16)
        y2 = _mm(z_ref[:, 0:D], w2_v[0])
        for k in range(1, N_CHIPS):
            y2 = y2 + _mm(z_ref[:, k * D:(k + 1) * D], w2_v[k])
        ry = lax.rsqrt(jnp.mean(y2 * y2, axis=-1, keepdims=True) + EPS)
        yn = y2 * ry
        n = yn * post_ref[...]
        g2 = mod_ref[:, 5 * D:6 * D]
        x2 = x2_ref[...]
        err = x2 + g2 * n - t_ref[...]
        acc_ref[5:6, :] += _rowsum(err * err) * (0.5 / D)
        dout = err * (1.0 / D)
        acc_ref[0:1, :] += _rowsum(dout * n)
        dn = dout * g2
        acc_ref[1:2, :] += _rowsum(dn * yn)
        dyn = dn * post_ref[...]
        dy2 = (ry * (dyn - yn * jnp.mean(dyn * yn, axis=-1, keepdims=True))).astype(BF16)
        dy2_ref[...] = dy2
        for k in range(N_CHIPS):
            dz = _mm(dy2, w2_v[k], NT)
            da_ref[:, k * D:(k + 1) * D] = (dz * (2.0 * ra_scr[:, k * D:(k + 1) * D])).astype(BF16)
        dh2 = jnp.zeros((TM, D), F32)
        for k in range(N_CHIPS):
            dh2 = dh2 + _mm(da_ref[:, k * D:(k + 1) * D], w1_v[k], NT)
        r2 = lax.rsqrt(jnp.mean(x2 * x2, axis=-1, keepdims=True) + EPS)
        xn = x2 * r2
        yv = xn * pre_ref[...]
        acc_ref[2:3, :] += _rowsum(dh2)
        acc_ref[3:4, :] += _rowsum(dh2 * yv)
        dyv = dh2 * (1.0 + mod_ref[:, 4 * D:5 * D])
        acc_ref[4:5, :] += _rowsum(dyv * xn)
        dxn = dyv * pre_ref[...]
        dx2_ref[...] = dout + r2 * (dxn - xn * jnp.mean(dxn * xn, axis=-1, keepdims=True))

    tile = BS((TM, D), lambda i: (i, 0))
    wide = BS((TM, D_FF), lambda i: (i, 0))
    vec = BS((1, D), lambda i: (0, 0))
    return pl.pallas_call(
        body, name="ffn_fwd_bwd", grid=(S // TM,),
        out_shape=(SDS((S, D_FF), BF16), SDS((S, D_FF), BF16), SDS((S, D), BF16), SDS((S, D), F32),
                   SDS((8, D), F32)),
        in_specs=[tile, tile, tile, BS((1, 6 * D), lambda i: (0, 0)), vec, vec, BS(memory_space=pl.ANY)],
        out_specs=(wide, wide, tile, tile, BS((8, D), lambda i: (0, 0))),
        scratch_shapes=[pltpu.VMEM((N_CHIPS, R_FF, D), BF16), pltpu.VMEM((N_CHIPS, R_FF, D), BF16),
                        pltpu.VMEM((TM, D_FF), F32),
                        pltpu.SemaphoreType.DMA((2,))],
        compiler_params=_params("arbitrary"),
    )(h2, x2, target, mod, post_cm, pre_cm, wg)


def _merge_bwd_call(dx2, y, ya, yb, p, mod, post_tm, wg):
    S = y.shape[0]

    def body(dx2_ref, y_ref, ya_ref, yb_ref, ga_ref, gb_ref, mod_ref, post_ref, wa_ref, wb_ref, wo_ref,
             dy_ref, dya_ref, dyb_ref, doa_ref, dcb_ref, dpg_ref, acc_ref, bsum_ref):
        @pl.when(pl.program_id(0) == 0)
        def _():
            acc_ref[...] = jnp.zeros_like(acc_ref)
            bsum_ref[...] = jnp.zeros_like(bsum_ref)

        y = y_ref[...]
        ry = lax.rsqrt(jnp.mean(y * y, axis=-1, keepdims=True) + EPS)
        yn = y * ry
        dx2 = dx2_ref[...]
        acc_ref[0:1, :] += _rowsum(dx2 * (yn * post_ref[...]))
        dn = dx2 * mod_ref[:, 2 * D:3 * D]
        acc_ref[1:2, :] += _rowsum(dn * yn)
        dyn = dn * post_ref[...]
        dy = (ry * (dyn - yn * jnp.mean(dyn * yn, axis=-1, keepdims=True))).astype(BF16)
        dy_ref[...] = dy
        dmg = _mm_rows_t(dy, wo_ref)
        sa, sb = _sig(ga_ref[...]), _sig(gb_ref[...])
        dya = (dmg * sa).astype(BF16)
        dyb = (dmg * sb).astype(BF16)
        dya_ref[...] = dya
        dyb_ref[...] = dyb
        dga = dmg * ya_ref[...] * (sa * (1.0 - sa))
        dgb = dmg * yb_ref[...] * (sb * (1.0 - sb))
        dpg_ref[:, 0:D] = dga.astype(BF16)
        dpg_ref[:, D:2 * D] = dgb.astype(BF16)
        bsum_ref[:, 0:D] += _rowsum(dga)
        bsum_ref[:, D:2 * D] += _rowsum(dgb)
        doa_ref[...] = _mm_rows_t(dya, wa_ref)
        dcb_ref[...] = _mm_rows_t(dyb, wb_ref)

    tile = BS((TM, D), lambda i: (i, 0))
    vec = BS((1, D), lambda i: (0, 0))
    return pl.pallas_call(
        body, name="merge_bwd", grid=(S // TM,),
        out_shape=(SDS((S, D), BF16), SDS((S, D), BF16), SDS((S, D), BF16), SDS((S, D), F32), SDS((S, D), F32),
                   SDS((S, 2 * D), BF16), SDS((8, D), F32), SDS((1, 2 * D), F32)),
        in_specs=[tile, tile, tile, tile, BS((TM, D), lambda i: (i, 6)), BS((TM, D), lambda i: (i, 7)),
                  BS((1, 6 * D), lambda i: (0, 0)), vec, _br_spec(O_BRA), _br_spec(O_BRB), _br_spec(O_OUT)],
        out_specs=(tile, tile, tile, tile, tile, BS((TM, 2 * D), lambda i: (i, 0)),
                   BS((8, D), lambda i: (0, 0)), BS((1, 2 * D), lambda i: (0, 0))),
        compiler_params=_params("arbitrary"),
    )(dx2, y, ya, yb, p, p, mod, post_tm, wg, wg, wg)


def _hgrn_bwd_call(p, o, doa, st, logits, gn, part):
    S = p.shape[0]
    nb = S // TB
    ncb = TB // CHUNK

    def body(q_ref, f_ref, v_ref, og_ref, o_ref, doa_ref, st_ref, lg_ref, gn_ref, part_ref,
             dp_ref, bsum_ref, dlg_ref, dgn_ref, recv_ref,
             dst_scr, dlb_scr, dqe_s, dqt_s, dkt_s, dkd_s, dv_s, dog_s, dble_s, send_sems, recv_sems):
        i = pl.program_id(0)
        start, finish = _chip_exchange(part_ref, recv_ref, send_sems, recv_sems)

        @pl.when(i == 0)
        def _():
            start()
            dst_scr[...] = jnp.zeros_like(dst_scr)
            dlb_scr[...] = jnp.zeros_like(dlb_scr)
            bsum_ref[...] = jnp.zeros_like(bsum_ref)
            dgn_ref[...] = jnp.zeros_like(dgn_ref)

        lb = _lower_bound(lg_ref)
        tril, triu = _tri_masks()

        def chunk(tt, carry):
            ci = ncb - 1 - tt
            rows = pl.ds(pl.multiple_of(ci * CHUNK, CHUNK), CHUNK)
            q_r, f_r = q_ref[rows, :], f_ref[rows, :]
            t = _hg_gates(q_r, f_r, lb, tril)
            v = v_ref[rows, :]
            for h in range(HEADS):
                sl = slice(h * DK, (h + 1) * DK)
                stp = st_ref[ci, :, sl]
                stb = stp.astype(BF16)
                qeb = t["qe"][:, sl].astype(BF16)
                qtb = t["qt"][:, sl].astype(BF16)
                ktb = t["kt"][:, sl].astype(BF16)
                kdb = t["kd"][:, sl].astype(BF16)
                vb = v[:, sl].astype(BF16)
                a = jnp.where(tril > 0.5, _mm(qtb, ktb, NT), 0.0)
                o_h = o_ref[rows, sl]
                rinv = lax.rsqrt(jnp.mean(o_h * o_h, axis=-1, keepdims=True) + EPS)
                oh = o_h * rinv
                og = og_ref[rows, sl]
                so = _sig(og)
                d_oa = doa_ref[rows, sl]
                don = d_oa * (og * so)
                dog_s[:, sl] = d_oa * (oh * gn_ref[:, sl]) * _dsilu(og, so)
                dgn_ref[:, sl] += _rowsum(don * oh)
                doh = don * gn_ref[:, sl]
                do = (rinv * (doh - oh * jnp.mean(doh * oh, axis=-1, keepdims=True))).astype(BF16)
                dqe_s[:, sl] = _mm(do, stb, NN)
                dstp = _mm(do, qeb, TN)
                dab = jnp.where(tril > 0.5, _mm(do, vb, NT), 0.0).astype(BF16)
                dqt_s[:, sl] = _mm(dab, ktb, NN)
                dkt_s[:, sl] = _mm(dab, qtb, TN)
                dstn = dst_scr[:, sl]
                dsb = dstn.astype(BF16)
                dkd_s[:, sl] = _mm(vb, dsb, NN)
                dv_s[:, sl] = _mm(a.astype(BF16), do, TN) + _mm(kdb, dsb, NT)
                el = t["elast"][:, sl]
                dst_scr[:, sl] = dstn * el + dstp
                dble_s[:, sl] = el * _rowsum(stp * dstn)
            dqe, dqt, dkt, dkd = dqe_s[...], dqt_s[...], dkt_s[...], dkd_s[...]
            dq = dqe * t["e"] + dqt * t["eq"]
            dk = dkt * t["ek"] + dkd * t["dd"]
            dkk = dkd * t["kd"]
            qt_r = t["qt"].astype(BF16).astype(F32)
            kt_r = t["kt"].astype(BF16).astype(F32)
            dbv = dqe * t["qe"] + dqt * qt_r - dkt * kt_r - dkk
            dg = _cumsum_mm(triu, dbv) + (_rowsum(dkk) + dble_s[...])
            df = dg / t["f"] - dk
            sf = t["sf"]
            dlb_scr[...] += _rowsum(df * (1.0 - sf))
            dqr = dq * _dsilu(q_r, t["sq"])
            dfr = df * (1.0 - lb) * (sf * (1.0 - sf))
            dvv, dog = dv_s[...], dog_s[...]
            dp_ref[rows, 0:D] = dqr.astype(BF16)
            dp_ref[rows, D:2 * D] = dfr.astype(BF16)
            dp_ref[rows, 2 * D:3 * D] = dvv.astype(BF16)
            dp_ref[rows, 3 * D:4 * D] = dog.astype(BF16)
            bsum_ref[:, 0:D] += _rowsum(dqr)
            bsum_ref[:, D:2 * D] += _rowsum(dfr)
            bsum_ref[:, 2 * D:3 * D] += _rowsum(dvv)
            bsum_ref[:, 3 * D:4 * D] += _rowsum(dog)
            return carry

        lax.fori_loop(0, ncb, chunk, 0)

        dl = dlb_scr[...] * lb * (1.0 - lb)
        dlg_ref[0:1, :] = dl
        dlg_ref[1:2, :] = -dl

        @pl.when(i == nb - 1)
        def _():
            finish()

    col = lambda j: BS((TB, D), lambda i, j=j: (nb - 1 - i, j))
    rev = BS((TB, D), lambda i: (nb - 1 - i, 0))
    cd = pltpu.VMEM((CHUNK, D), F32)
    return pl.pallas_call(
        body, name="hgrn_bwd", grid=(nb,),
        out_shape=(SDS((S, 4 * D), BF16), SDS((1, 4 * D), F32), SDS((2, D), F32), SDS((1, D), F32),
                   SDS((3,) + part.shape[1:], part.dtype)),
        in_specs=[col(0), col(1), col(2), col(3), rev, rev, BS((ncb, DK, D), lambda i: (nb - 1 - i, 0, 0)),
                  BS((2, D), lambda i: (0, 0)), BS((1, D), lambda i: (0, 0)), BS(memory_space=pl.ANY)],
        out_specs=(BS((TB, 4 * D), lambda i: (nb - 1 - i, 0)), BS((1, 4 * D), lambda i: (0, 0)),
                   BS((2, D), lambda i: (0, 0)), BS((1, D), lambda i: (0, 0)), BS(memory_space=pl.ANY)),
        scratch_shapes=[pltpu.VMEM((DK, D), F32), pltpu.VMEM((1, D), F32), cd, cd, cd, cd, cd, cd,
                        pltpu.VMEM((1, D), F32)] + _exchange_sems(),
        compiler_params=_params("arbitrary"),
    )(p, p, p, p, o, doa, st, logits, gn, part)


def _conv_bwd_call(dcb, uc, u, p, dw, ln_g, ln_b, part):
    S = uc.shape[0]
    nb = S // TM
    hb = TM // HALO

    def body(dcb_ref, uc_ref, u_ref, uh_ref, cv_ref, cg_ref, dw_ref, g_ref, b_ref, part_ref,
             dp_ref, bsum_ref, ddw_ref, acc_ref, recv_ref, uext, dext, ush, dsh, send_sems, recv_sems):
        i = pl.program_id(0)
        start, finish = _chip_exchange(part_ref, recv_ref, send_sems, recv_sems)

        @pl.when(i == 0)
        def _():
            start()
            dext[TM:EXT, :] = jnp.zeros((EXT - TM, D), F32)
            uext[HALO + TM:EXT, :] = jnp.zeros((EXT - HALO - TM, D), F32)
            bsum_ref[...] = jnp.zeros_like(bsum_ref)
            ddw_ref[...] = jnp.zeros_like(ddw_ref)
            acc_ref[...] = jnp.zeros_like(acc_ref)

        first_tile = (nb - 1 - i) == 0
        uext[0:HALO, :] = jnp.where(first_tile, 0.0, uh_ref[...])
        uext[HALO:HALO + TM, :] = u_ref[...]
        _fill_shifted(uext, ush)

        for rb in range(TM // SUB):
            rs_ = slice(rb * SUB, (rb + 1) * SUB)
            xh, rs = _layernorm_stats(uc_ref[rs_, :])
            ln = xh * g_ref[...] + b_ref[...]
            dln = dcb_ref[rs_, :] * _dsilu(ln, _sig(ln))
            acc_ref[1:2, :] += _rowsum(dln * xh)
            acc_ref[2:3, :] += _rowsum(dln)
            dxh = dln * g_ref[...]
            duc = rs * (dxh - jnp.mean(dxh, axis=-1, keepdims=True)
                        - xh * jnp.mean(dxh * xh, axis=-1, keepdims=True))
            dext[rs_, :] = duc
            acc_ref[0:1, :] += _rowsum(duc)
        _fill_shifted(dext, dsh)

        for j in range(CONV_K):
            part = jnp.zeros((SUB, D), F32)
            for rb in range(TM // SUB):
                s0 = HALO - (CONV_K - 1) + j + rb * SUB
                part = part + dext[rb * SUB:(rb + 1) * SUB, :] * _window(uext, ush, s0, SUB)
            ddw_ref[j:j + 1, :] += _rowsum(part)

        for rb in range(TM // SUB):
            rs_ = slice(rb * SUB, (rb + 1) * SUB)
            du = jnp.zeros((SUB, D), F32)
            for j in range(CONV_K):
                s0 = rb * SUB + (CONV_K - 1) - j
                du = du + dw_ref[j:j + 1, :] * _window(dext, dsh, s0, SUB)
            cg = cg_ref[rs_, :]
            sg = _sig(cg)
            dcv = du * sg
            dcg = du * cv_ref[rs_, :] * (sg * (1.0 - sg))
            dp_ref[rs_, 0:D] = dcv.astype(BF16)
            dp_ref[rs_, D:2 * D] = dcg.astype(BF16)
            bsum_ref[:, 0:D] += _rowsum(dcv)
            bsum_ref[:, D:2 * D] += _rowsum(dcg)

        dext[TM:TM + HALO, :] = dext[0:HALO, :]

        @pl.when(i == nb - 1)
        def _():
            finish()

    rev = BS((TM, D), lambda i: (nb - 1 - i, 0))
    vec = BS((1, D), lambda i: (0, 0))
    return pl.pallas_call(
        body, name="conv_bwd", grid=(nb,),
        out_shape=(SDS((S, 2 * D), BF16), SDS((1, 2 * D), F32), SDS((32, D), F32), SDS((8, D), F32),
                   SDS((3,) + part.shape[1:], part.dtype)),
        in_specs=[rev, rev, rev, BS((HALO, D), lambda i: (jnp.maximum((nb - 1 - i) * hb - 1, 0), 0)),
                  BS((TM, D), lambda i: (nb - 1 - i, 4)), BS((TM, D), lambda i: (nb - 1 - i, 5)),
                  BS((CONV_K, D), lambda i: (0, 0)), vec, vec, BS(memory_space=pl.ANY)],
        out_specs=(BS((TM, 2 * D), lambda i: (nb - 1 - i, 0)), BS((1, 2 * D), lambda i: (0, 0)),
                   BS((32, D), lambda i: (0, 0)), BS((8, D), lambda i: (0, 0)), BS(memory_space=pl.ANY)),
        scratch_shapes=[pltpu.VMEM((EXT, D), F32), pltpu.VMEM((EXT, D), F32),
                        pltpu.VMEM((7, HALO + TM, D), F32), pltpu.VMEM((7, HALO + TM, D), F32)] + _exchange_sems(),
        compiler_params=_params("arbitrary"),
    )(dcb, uc, u, u, p, p, dw, ln_g, ln_b, part)


def _in_bwd_call(dp_hg, dp_cv, dp_gt, x, dx2, mod, pre_tm, wg, part):
    S = x.shape[0]

    def body(hg_ref, cv_ref, gt_ref, x_ref, dx2_ref, mod_ref, g_ref, w_hbm, part_ref, gx_ref, acc_ref, recv_ref,
             w_vmem, sem, send_sems, recv_sems):
        start, finish = _chip_exchange(part_ref, recv_ref, send_sems, recv_sems)

        @pl.when(pl.program_id(0) == 0)
        def _():
            start()
            _load_rows(w_hbm, w_vmem, sem, O_IN).wait()
            acc_ref[...] = jnp.zeros_like(acc_ref)

        dh = jnp.zeros((TM, D), F32)
        for k in range(IN_COLS // D):
            src, kk = ((hg_ref, k), (cv_ref, k - 4), (gt_ref, k - 6))[0 if k < 4 else (1 if k < 6 else 2)]
            dh = dh + _mm(src[:, kk * D:(kk + 1) * D], w_vmem[k // 2, (k % 2) * D:(k % 2 + 1) * D, :], NT)
        xv = x_ref[...]
        r = lax.rsqrt(jnp.mean(xv * xv, axis=-1, keepdims=True) + EPS)
        xn = xv * r
        yv = xn * g_ref[...]
        acc_ref[0:1, :] += _rowsum(dh)
        acc_ref[1:2, :] += _rowsum(dh * yv)
        dyv = dh * (1.0 + mod_ref[:, D:2 * D])
        acc_ref[2:3, :] += _rowsum(dyv * xn)
        dxn = dyv * g_ref[...]
        gx_ref[...] = dx2_ref[...] + r * (dxn - xn * jnp.mean(dxn * xn, axis=-1, keepdims=True))

        @pl.when(pl.program_id(0) == S // TM - 1)
        def _():
            finish()

    tile = BS((TM, D), lambda i: (i, 0))
    return pl.pallas_call(
        body, name="in_bwd", grid=(S // TM,),
        out_shape=(SDS((S, D), F32), SDS((8, D), F32), SDS((3,) + part.shape[1:], part.dtype)),
        in_specs=[BS((TM, 4 * D), lambda i: (i, 0)), BS((TM, 2 * D), lambda i: (i, 0)),
                  BS((TM, 2 * D), lambda i: (i, 0)), tile, tile, BS((1, 6 * D), lambda i: (0, 0)),
                  BS((1, D), lambda i: (0, 0)), BS(memory_space=pl.ANY), BS(memory_space=pl.ANY)],
        out_specs=(tile, BS((8, D), lambda i: (0, 0)), BS(memory_space=pl.ANY)),
        scratch_shapes=[pltpu.VMEM((N_CHIPS, R_IN, D), BF16), pltpu.SemaphoreType.DMA] + _exchange_sems(),
        compiler_params=_params("arbitrary"),
    )(dp_hg, dp_cv, dp_gt, x, dx2, mod, pre_tm, wg, part)


def _wgrad_call(gp, a, b, name, bm, place, rows):
    S, M = a.shape
    N = b.shape[1]
    bk = min(S, 1024)
    nk = S // bk

    def body(a_ref, b_ref, *rest):
        o_ref, acc = rest[-2], rest[-1]
        k = pl.program_id(2)

        @pl.when(k == 0)
        def _():
            acc[...] = jnp.zeros_like(acc)

        acc[...] += _mm(a_ref[...], b_ref[...], TN)

        @pl.when(k == nk - 1)
        def _():
            o_ref[...] = acc[...].astype(BF16)

    in_specs = [BS((bk, bm), lambda i, j, k: (k, i)), BS((bk, D), lambda i, j, k: (k, j))]
    args = [a, b]
    if gp is not None:
        in_specs.append(BS(memory_space=pl.ANY))
        args.append(gp)
    return pl.pallas_call(
        body, name=name, grid=(M // bm, N // D, nk),
        out_shape=SDS((N_CHIPS, rows, D), BF16),
        in_specs=in_specs,
        out_specs=BS((None, bm, D), lambda i, j, k: (*place(i, j), 0)),
        scratch_shapes=[pltpu.VMEM((bm, D), F32)],
        input_output_aliases={} if gp is None else {2: 0},
        compiler_params=_params("parallel", "parallel", "arbitrary"),
    )(*args)


def _outer_call(cact, dmod):
    n = dmod.shape[1]

    def body(a_ref, b_ref, o_ref):
        o_ref[...] = _mm(a_ref[...], b_ref[...], TN, HI)

    return pl.pallas_call(
        body, name="wgrad_ada", out_shape=SDS((D, n), F32),
        compiler_params=pltpu.CompilerParams(vmem_limit_bytes=VMEM_LIMIT),
    )(cact, dmod)


def _adamw_call(w, g, m, v, name):
    R, C = w.shape
    tr = R
    while tr * C > 512 * 1024 and tr % 16 == 0:
        tr //= 2
    c1 = 1.0 - ADAM_B1 ** ADAM_STEP
    c2 = 1.0 - ADAM_B2 ** ADAM_STEP

    def body(w_ref, g_ref, m_ref, v_ref, d_ref, m2_ref, v2_ref):
        g = g_ref[...]
        m2 = ADAM_B1 * m_ref[...] + (1.0 - ADAM_B1) * g
        v2 = ADAM_B2 * v_ref[...] + (1.0 - ADAM_B2) * (g * g)
        m2_ref[...] = m2
        v2_ref[...] = v2
        d_ref[...] = -ADAM_LR * ((m2 / c1) / (jnp.sqrt(v2 / c2) + ADAM_EPS) + ADAM_WD * w_ref[...])

    tile = BS((tr, C), lambda i: (i, 0))
    return pl.pallas_call(
        body, name=name, grid=(R // tr,), out_shape=(SDS((R, C), F32),) * 3,
        in_specs=[tile] * 4, out_specs=(tile,) * 3, compiler_params=_params("parallel"),
    )(w, g, m, v)


def _rs_begin(g, c_idx, tag):
    n = g.shape[1]
    g = g.reshape(N_CHIPS, 2, n // 2, D)
    return _add_halves_call(g, _sibling_halves_call(g, tag), c_idx, tag)


def _rs_end(part, recv, c_idx, chip_idx, tag):
    red = _add_chips_call(part, recv, chip_idx, tag)
    other = _sibling_join_call(red, tag)
    return jnp.where(c_idx[0] == 0, jnp.concatenate([red, other], axis=0), jnp.concatenate([other, red], axis=0))


def _local_step(x, c, target, wg, pack, small, c_idx, chip_idx):
    mod8, cact = _mod_call(c, wg, small["b_ada"])
    mod = mod8[0:1]
    p, h1, wg = _fwd_in_call(x, mod, small["pre_tm"], wg, small["b_in"], pack)
    o, oa, st, wg = _hgrn_fwd_call(p, small["logits"], small["hg_norm"], wg, pack)
    u, uc, cb, wg = _conv_fwd_call(p, small["conv_dw"], small["conv_db"], small["ln_g"], small["ln_b"], wg, pack)
    ya, yb, mg, y, x2, h2 = _merge_fwd_call(oa, cb, p, x, mod, small["post_tm"], small["pre_cm"], wg)
    z, da, dy2, dx2, acc_f = _ffn_call(h2, x2, target, mod, small["post_cm"], small["pre_cm"], wg)

    g_ff = _wgrad_call(None, h2, da, "wgrad_ff1", D, lambda i, j: (j, 0), 2 * R_FF)
    g_ff = _wgrad_call(g_ff, z, dy2, "wgrad_ff2", D, lambda i, j: (i, 1), 2 * R_FF)
    part_ff = _rs_begin(g_ff, c_idx, "ff")
    dy, dya, dyb, doa, dcb, dp_gt, acc_m, bs_gt = _merge_bwd_call(dx2, y, ya, yb, p, mod, small["post_tm"], wg)
    dp_hg, bs_hg, dlg, dgn, recv_ff = _hgrn_bwd_call(p, o, doa, st, small["logits"], small["hg_norm"], part_ff)

    g_br = _wgrad_call(None, oa, dya, "wgrad_br_a", R_BR, lambda i, j: (i, 0), 3 * R_BR)
    g_br = _wgrad_call(g_br, cb, dyb, "wgrad_br_b", R_BR, lambda i, j: (i, 1), 3 * R_BR)
    g_br = _wgrad_call(g_br, mg, dy, "wgrad_out", R_BR, lambda i, j: (i, 2), 3 * R_BR)
    part_br = _rs_begin(g_br, c_idx, "br")
    dp_cv, bs_cv, ddw, acc_c, recv_br = _conv_bwd_call(dcb, uc, u, p, small["conv_dw"], small["ln_g"], small["ln_b"],
                                                        part_br)

    g_in = _wgrad_call(None, h1, dp_hg, "wgrad_in_hg", D, lambda i, j: (j // 2, j % 2), R_IN)
    g_in = _wgrad_call(g_in, h1, dp_cv, "wgrad_in_cv", D, lambda i, j: (2, j), R_IN)
    g_in = _wgrad_call(g_in, h1, dp_gt, "wgrad_in_gt", D, lambda i, j: (3, j), R_IN)
    part_in = _rs_begin(g_in, c_idx, "in")
    gx, acc_i, recv_in = _in_bwd_call(dp_hg, dp_cv, dp_gt, x, dx2, mod, small["pre_tm"], wg, part_in)

    red_ff = _rs_end(part_ff, recv_ff, c_idx, chip_idx, "ff")
    red_br = _rs_end(part_br, recv_br, c_idx, chip_idx, "br")
    red_in = _rs_end(part_in, recv_in, c_idx, chip_idx, "in")

    zrow = jnp.zeros((1, D), F32)
    rows = [acc_i[0:1], acc_i[1:2], acc_m[0:1], acc_f[2:3], acc_f[3:4], acc_f[0:1],
            acc_i[2:3], acc_m[1:2], acc_f[4:5], acc_f[1:2],
            jnp.concatenate([bs_hg, bs_cv, bs_gt], axis=1).reshape(8, D),
            dlg, dgn, acc_c[0:1], acc_c[1:2], acc_c[2:3],
            ddw,
            cact, acc_f[5:6]] + [zrow] * 6
    return gx, jnp.concatenate(rows, axis=0), red_in, red_ff, red_br


def kernel(x, c, w_ada, b_ada, pre_norm_tm, post_norm_tm, pre_norm_cm, post_norm_cm, w_in, b_in, hg_lb_logits, hg_norm, conv_dw, conv_db, conv_ln_g, conv_ln_b, w_br_a, w_br_b, w_out, w_ff1, w_ff2, loss_target, m_w_ada, m_b_ada, m_pre_norm_tm, m_post_norm_tm, m_pre_norm_cm, m_post_norm_cm, m_w_in, m_b_in, m_hg_lb_logits, m_hg_norm, m_conv_dw, m_conv_db, m_conv_ln_g, m_conv_ln_b, m_w_br_a, m_w_br_b, m_w_out, m_w_ff1, m_w_ff2, v_w_ada, v_b_ada, v_pre_norm_tm, v_post_norm_tm, v_pre_norm_cm, v_post_norm_cm, v_w_in, v_b_in, v_hg_lb_logits, v_hg_norm, v_conv_dw, v_conv_db, v_conv_ln_g, v_conv_ln_b, v_w_br_a, v_w_br_b, v_w_out, v_w_ff1, v_w_ff2):
    xi, yi, ci = lax.axis_index("x"), lax.axis_index("y"), lax.axis_index("c")
    chip = 2 * xi + yi
    c_idx = jnp.reshape(ci, (1,)).astype(jnp.int32)
    chip_idx = jnp.reshape(chip, (1,)).astype(jnp.int32)

    def pack_small(ada_b, pre_t, post_t, pre_c, post_c, in_b, lg, hgn, cdb, lng, lnb, cdw):
        flat = jnp.concatenate([cdw[0].reshape(-1), jnp.zeros((8 * D - CONV_K * 256,), F32)]).reshape(8, D)
        return jnp.concatenate([ada_b.reshape(6, D), pre_t, post_t, pre_c, post_c, in_b.reshape(8, D), lg, hgn,
                                cdb, lng, lnb, flat], axis=0)

    w_in_halves = w_in[0].reshape(D, 2, D).transpose(1, 0, 2).reshape(R_IN, D)
    pack = jnp.concatenate([w_in_halves, w_ff1[0], w_ff2[0], w_br_a[0], w_br_b[0], w_out[0], w_ada[0].T],
                           axis=0).astype(BF16)
    wg = lax.dynamic_update_slice(jnp.zeros((N_CHIPS, PACK_W, D), BF16), pack[None], (chip, 0, 0))
    wg = _gather_first_call(pack, wg)
    dw_blk = jnp.concatenate([conv_dw[0].reshape(-1), jnp.zeros((8 * D - CONV_K * 256,), F32)]).reshape(8, D)
    dw_all = _allgather_call(dw_blk, "gather_conv_dw", in_vmem=True, with_sum=False)[0]
    dw_all = dw_all.reshape(N_CHIPS, 2, 8 * D)[:, 0, :CONV_K * 256].reshape(N_CHIPS, CONV_K, 256)
    dw_full = dw_all.transpose(1, 0, 2).reshape(CONV_K, D)

    small = dict(b_ada=b_ada, pre_tm=pre_norm_tm, post_tm=post_norm_tm, pre_cm=pre_norm_cm, post_cm=post_norm_cm,
                 b_in=b_in, logits=hg_lb_logits, hg_norm=hg_norm, conv_dw=dw_full, conv_db=conv_db,
                 ln_g=conv_ln_g, ln_b=conv_ln_b)

    gx, srows, red_in, red_ff, red_br = _local_step(x[0], c, loss_target[0], wg, pack, small, c_idx, chip_idx)

    sall, ssum = _allgather_call(srows, "gather_small", in_vmem=True, with_sum=True)
    sall = sall.reshape(N_DEV, SMALL_ROWS, D)
    loss = jnp.sum(ssum[57])
    dmod_all = sall[:, 0:6, :].reshape(N_DEV, 6 * D)
    wa = 6 * D // N_CHIPS
    g_ada = _outer_call(sall[:, 56, :], lax.dynamic_slice_in_dim(dmod_all, chip * wa, wa, axis=1))
    g_dw = lax.dynamic_slice_in_dim(ssum[24:24 + CONV_K], chip * 256, 256, axis=1)
    g_small = jnp.concatenate(
        [ssum[0:24], jnp.concatenate([g_dw.reshape(-1), jnp.zeros((8 * D - CONV_K * 256,), F32)]).reshape(8, D)],
        axis=0)

    shapes = {"in": w_in.shape, "br_a": w_br_a.shape, "br_b": w_br_b.shape, "out": w_out.shape,
              "ff1": w_ff1.shape, "ff2": w_ff2.shape}
    offs = {"in": (red_in, 0, R_IN), "ff1": (red_ff, 0, R_FF), "ff2": (red_ff, R_FF, 2 * R_FF),
            "br_a": (red_br, 0, R_BR), "br_b": (red_br, R_BR, 2 * R_BR), "out": (red_br, 2 * R_BR, 3 * R_BR)}
    wmv = {"in": (w_in, m_w_in, v_w_in), "br_a": (w_br_a, m_w_br_a, v_w_br_a), "br_b": (w_br_b, m_w_br_b, v_w_br_b),
           "out": (w_out, m_w_out, v_w_out), "ff1": (w_ff1, m_w_ff1, v_w_ff1), "ff2": (w_ff2, m_w_ff2, v_w_ff2)}
    res = {}
    for n in offs:
        shp = shapes[n]
        g2d = offs[n][0][offs[n][1]:offs[n][2]]
        if n == "in":
            g2d = g2d.reshape(2, D, D).transpose(1, 0, 2)
        g2d = g2d.reshape(shp[1], shp[2])
        w_, m_, v_ = (a[0] for a in wmv[n])
        d_, m2_, v2_ = _adamw_call(w_, g2d, m_, v_, "adamw_" + n)
        res[n] = tuple(a.reshape(shp) for a in (g2d, d_, m2_, v2_))
    d_, m2_, v2_ = _adamw_call(w_ada[0], g_ada, m_w_ada[0], v_w_ada[0], "adamw_ada")
    res["ada"] = tuple(a.reshape(w_ada.shape) for a in (g_ada, d_, m2_, v2_))

    ws = pack_small(b_ada, pre_norm_tm, post_norm_tm, pre_norm_cm, post_norm_cm, b_in, hg_lb_logits, hg_norm,
                    conv_db, conv_ln_g, conv_ln_b, conv_dw)
    ms = pack_small(m_b_ada, m_pre_norm_tm, m_post_norm_tm, m_pre_norm_cm, m_post_norm_cm, m_b_in, m_hg_lb_logits,
                    m_hg_norm, m_conv_db, m_conv_ln_g, m_conv_ln_b, m_conv_dw)
    vs = pack_small(v_b_ada, v_pre_norm_tm, v_post_norm_tm, v_pre_norm_cm, v_post_norm_cm, v_b_in, v_hg_lb_logits,
                    v_hg_norm, v_conv_db, v_conv_ln_g, v_conv_ln_b, v_conv_dw)
    sres = (g_small,) + tuple(_adamw_call(ws, g_small, ms, vs, "adamw_small"))

    def unpack_small(t):
        return {"b_ada": t[0:6].reshape(1, 6 * D), "pre_tm": t[6:7], "post_tm": t[7:8], "pre_cm": t[8:9],
                "post_cm": t[9:10], "b_in": t[10:18].reshape(1, IN_COLS), "logits": t[18:20], "hg_norm": t[20:21],
                "conv_db": t[21:22], "ln_g": t[22:23], "ln_b": t[23:24],
                "conv_dw": t[24:32].reshape(-1)[:CONV_K * 256].reshape(1, CONV_K, 256)}

    order = ["ada", "b_ada", "pre_tm", "post_tm", "pre_cm", "post_cm", "in", "b_in", "logits", "hg_norm", "conv_dw",
             "conv_db", "ln_g", "ln_b", "br_a", "br_b", "out", "ff1", "ff2"]
    outs = [loss, gx.reshape(x.shape)]
    for kind in range(4):
        sm = unpack_small(sres[kind])
        for n in order:
            outs.append(res[n][kind] if n in res else sm[n])
    return tuple(outs)
```

```python
import functools

import jax
import jax.numpy as jnp
from jax import lax
from jax.experimental import pallas as pl
from jax.experimental.pallas import tpu as pltpu

F32, BF16 = jnp.float32, jnp.bfloat16
SDS = jax.ShapeDtypeStruct
BS = pl.BlockSpec
MESH = pl.DeviceIdType.MESH
HI = lax.Precision.HIGHEST

D = 1024
D_FF = 4096
IN_COLS = 8192
HEADS, DK = 8, 128
CHUNK = 128
CONV_K = 31
HALO = 32
SUB = 32
EPS = 1e-6
N_CHIPS, N_DEV = 4, 8
TM = 256
TB = 256
VMEM_LIMIT = 56 * 1024 * 1024

R_IN, R_BR, R_FF = 2048, 256, 1024
PACK_W = R_IN + 3 * R_BR + 2 * R_FF
O_IN, O_FF1, O_FF2, O_BRA, O_BRB, O_OUT = 0, 2048, 3072, 4096, 4352, 4608
SMALL_ROWS = 64

ADAM_LR, ADAM_B1, ADAM_B2, ADAM_EPS, ADAM_WD, ADAM_STEP = 0.001, 0.9, 0.999, 1e-08, 0.01, 10

NN = (((1,), (0,)), ((), ()))
NT = (((1,), (1,)), ((), ()))
TN = (((0,), (0,)), ((), ()))


def _mm(a, b, dims=NN, precision=None):
    return lax.dot_general(a, b, dims, preferred_element_type=F32, precision=precision)


def _sig(v):
    return jax.nn.sigmoid(v)


def _dsilu(v, s):
    return s * (1.0 + v * (1.0 - s))


def _params(*sem):
    return pltpu.CompilerParams(dimension_semantics=sem if sem else None, vmem_limit_bytes=VMEM_LIMIT)


def _rowsum(v):
    return jnp.sum(v, axis=0, keepdims=True)


def _mesh_pos():
    return lax.axis_index("x"), lax.axis_index("y"), lax.axis_index("c")


def _allgather(x_ref, out_ref, send_sems, recv_sems, local_sem):
    m_per = x_ref.shape[0]
    x, y, c = _mesh_pos()
    me, sibling = (x, y, c), (x, y, 1 - c)
    chips = [(1 - x, y), (x, 1 - y), (1 - x, 1 - y)]

    def rows(px, py, pc):
        return out_ref.at[pl.ds((4 * px + 2 * py + pc) * m_per, m_per), :]

    def copy(k, block, to, src=None):
        return pltpu.make_async_remote_copy(
            src_ref=rows(*block) if src is None else src, dst_ref=rows(*block),
            send_sem=send_sems.at[k], recv_sem=recv_sems.at[k], device_id=to, device_id_type=MESH)

    mine = pltpu.make_async_copy(x_ref, rows(*me), local_sem)
    mine.start()
    first = [copy(0, me, sibling, src=x_ref)]
    first += [copy(1 + j, me, (*chip, c), src=x_ref) for j, chip in enumerate(chips)]
    for cp in first:
        cp.start()
    passed = [copy(4 + j, (*chip, c), sibling) for j, chip in enumerate(chips)]
    for j, chip in enumerate(chips):
        copy(1 + j, (*chip, c), me).wait_recv()
        passed[j].start()
    copy(0, sibling, me).wait_recv()
    for j, chip in enumerate(chips):
        copy(4 + j, (*chip, 1 - c), me).wait_recv()
    for cp in first + passed:
        cp.wait_send()
    mine.wait()


def _allgather_sems():
    return [pltpu.SemaphoreType.DMA((7,)), pltpu.SemaphoreType.DMA((7,)), pltpu.SemaphoreType.DMA]


def _allgather_call(blk, name, in_vmem, with_sum):
    m_per, n = blk.shape

    def body(x_ref, out_ref, *rest):
        if with_sum:
            sum_ref, send_sems, recv_sems, local_sem = rest
        else:
            send_sems, recv_sems, local_sem = rest
        _allgather(x_ref, out_ref, send_sems, recv_sems, local_sem)
        if with_sum:
            acc = out_ref[0:m_per, :]
            for d in range(1, N_DEV):
                acc = acc + out_ref[d * m_per:(d + 1) * m_per, :]
            sum_ref[...] = acc

    space = pltpu.VMEM if in_vmem else pl.ANY
    out_shape = [SDS((N_DEV * m_per, n), blk.dtype)]
    out_specs = [BS(memory_space=space)]
    if with_sum:
        out_shape.append(SDS((m_per, n), blk.dtype))
        out_specs.append(BS(memory_space=pltpu.VMEM))
    return pl.pallas_call(
        body, name=name, out_shape=out_shape, in_specs=[BS(memory_space=space)], out_specs=out_specs,
        scratch_shapes=[pltpu.SemaphoreType.DMA((7,)), pltpu.SemaphoreType.DMA((7,)), pltpu.SemaphoreType.DMA],
        compiler_params=pltpu.CompilerParams(vmem_limit_bytes=VMEM_LIMIT),
    )(blk)


def _gather_sems(n_ranges):
    return [pltpu.SemaphoreType.DMA((6 * n_ranges,)), pltpu.SemaphoreType.DMA((6 * n_ranges,))]


def _pack_gather(pack_ref, wg_ref, send_sems, recv_sems, ranges):
    x, y, c = _mesh_pos()
    me, sibling = (x, y, c), (x, y, 1 - c)
    chips = [(1 - x, y), (x, 1 - y), (1 - x, 1 - y)]

    def land(r, px, py, pc):
        off, n = ranges[r]
        return wg_ref.at[2 * px + py, pl.ds(off + pc * (n // 2), n // 2), :]

    def mine(r):
        off, n = ranges[r]
        return pack_ref.at[pl.ds(off + c * (n // 2), n // 2), :]

    def copy(r, k, block, to, src=None):
        return pltpu.make_async_remote_copy(
            src_ref=land(r, *block) if src is None else src, dst_ref=land(r, *block),
            send_sem=send_sems.at[6 * r + k], recv_sem=recv_sems.at[6 * r + k], device_id=to, device_id_type=MESH)

    def start():
        for r in range(len(ranges)):
            for j, chip in enumerate(chips):
                copy(r, j, me, (*chip, c), src=mine(r)).start()

    def finish():
        for r in range(len(ranges)):
            for j, chip in enumerate(chips):
                copy(r, j, (*chip, c), me).wait_recv()
                copy(r, 3 + j, (*chip, c), sibling).start()
        for r in range(len(ranges)):
            for j, chip in enumerate(chips):
                copy(r, 3 + j, (*chip, 1 - c), me).wait_recv()
                copy(r, j, me, (*chip, c), src=mine(r)).wait_send()
                copy(r, 3 + j, (*chip, c), sibling).wait_send()

    return start, finish


def _prologue_call(pack, wg, dw_blk, c_blk, w_ada, b_ada):
    ranges = [(O_IN, R_IN)]
    wa = w_ada.shape[1]

    def body(pack_ref, wg_in, dw_ref, c_ref, wa_ref, ba_ref, wg_out, dwg_ref, ca_ref, modg_ref,
             cg_scr, part_scr, bs, br, s1, r1, l1, s2, r2, l2, s3, r3, l3):
        start, finish = _pack_gather(pack_ref, wg_out, bs, br, ranges)
        start()
        _allgather(dw_ref, dwg_ref, s1, r1, l1)
        _allgather(c_ref, cg_scr, s2, r2, l2)
        cv = cg_scr[...]
        ca = cv * _sig(cv)
        ca_ref[...] = ca
        part_scr[...] = _mm(ca.astype(BF16), wa_ref[...]) + ba_ref[...]
        _allgather(part_scr, modg_ref, s3, r3, l3)
        finish()

    hbm = BS(memory_space=pl.ANY)
    vm = BS(memory_space=pltpu.VMEM)
    return pl.pallas_call(
        body, name="prologue_gather_w_in_adaln",
        out_shape=(SDS(wg.shape, wg.dtype), SDS((N_DEV * 8, D), F32), SDS((N_DEV * 8, D), F32),
                   SDS((N_DEV * N_DEV * 8, wa), F32)),
        in_specs=[hbm, hbm, vm, vm, vm, vm], out_specs=(hbm, vm, vm, vm),
        scratch_shapes=[pltpu.VMEM((N_DEV * 8, D), F32), pltpu.VMEM((N_DEV * 8, wa), F32)] + _gather_sems(len(ranges))
        + _allgather_sems() + _allgather_sems() + _allgather_sems(),
        input_output_aliases={1: 0},
        compiler_params=pltpu.CompilerParams(vmem_limit_bytes=VMEM_LIMIT),
    )(pack, wg, dw_blk, c_blk, w_ada, b_ada)


def _sibling_halves_call(g, tag):
    _, _, h, n = g.shape

    def body(g_ref, out_ref, send_sems, recv_sems):
        x, y, c = _mesh_pos()
        cps = [pltpu.make_async_remote_copy(
            src_ref=g_ref.at[k, 1 - c], dst_ref=out_ref.at[k], send_sem=send_sems.at[k], recv_sem=recv_sems.at[k],
            device_id=(x, y, 1 - c), device_id_type=MESH) for k in range(N_CHIPS)]
        for cp in cps:
            cp.start()
        for cp in cps:
            cp.wait()

    return pl.pallas_call(
        body, name="rs_sibling_halves_" + tag, out_shape=SDS((N_CHIPS, h, n), g.dtype),
        in_specs=[BS(memory_space=pl.ANY)], out_specs=BS(memory_space=pl.ANY),
        scratch_shapes=[pltpu.SemaphoreType.DMA((N_CHIPS,)), pltpu.SemaphoreType.DMA((N_CHIPS,))],
    )(g)


def _chip_exchange(p_ref, out_ref, send_sems, recv_sems):
    x, y, c = _mesh_pos()
    chips = [(1 - x, y), (x, 1 - y), (1 - x, 1 - y)]

    def copies():
        return [pltpu.make_async_remote_copy(
            src_ref=p_ref.at[2 * cx + cy], dst_ref=out_ref.at[j], send_sem=send_sems.at[j], recv_sem=recv_sems.at[j],
            device_id=(cx, cy, c), device_id_type=MESH) for j, (cx, cy) in enumerate(chips)]

    def start():
        for cp in copies():
            cp.start()

    def finish():
        for cp in copies():
            cp.wait()

    return start, finish


def _exchange_sems():
    return [pltpu.SemaphoreType.DMA((3,)), pltpu.SemaphoreType.DMA((3,))]


def _sibling_join_call(full, tag):
    _, h, n = full.shape
    q = h // 4

    def body(in_ref, out_ref, send_sems, recv_sems):
        x, y, c = _mesh_pos()

        def copy(k, half):
            return pltpu.make_async_remote_copy(
                src_ref=in_ref.at[half, pl.ds(k * q, q)], dst_ref=out_ref.at[half, pl.ds(k * q, q)],
                send_sem=send_sems.at[k], recv_sem=recv_sems.at[k],
                device_id=(x, y, 1 - c), device_id_type=MESH)

        for k in range(4):
            copy(k, c).start()
        for k in range(4):
            copy(k, c).wait_send()
            copy(k, 1 - c).wait_recv()

    return pl.pallas_call(
        body, name="rs_sibling_join_" + tag, out_shape=SDS(full.shape, full.dtype),
        in_specs=[BS(memory_space=pl.ANY)], out_specs=BS(memory_space=pl.ANY),
        scratch_shapes=[pltpu.SemaphoreType.DMA((4,)), pltpu.SemaphoreType.DMA((4,))],
        input_output_aliases={0: 0},
    )(full)


def _add_halves_call(g, recv, c_idx, tag):
    _, _, h, n = g.shape
    tr = h // 2

    def body(c_ref, g_ref, r_ref, o_ref):
        o_ref[...] = (g_ref[...].astype(F32) + r_ref[...].astype(F32)).astype(BF16)

    return pl.pallas_call(
        body, name="rs_add_halves_" + tag, out_shape=SDS((N_CHIPS, h, n), BF16),
        grid_spec=pltpu.PrefetchScalarGridSpec(
            num_scalar_prefetch=1, grid=(N_CHIPS, 2),
            in_specs=[BS((None, None, tr, n), lambda k, r, c_ref: (k, c_ref[0], r, 0)),
                      BS((None, tr, n), lambda k, r, c_ref: (k, r, 0))],
            out_specs=BS((None, tr, n), lambda k, r, c_ref: (k, r, 0))),
        compiler_params=_params("arbitrary", "arbitrary"),
    )(c_idx, g, recv)


def _add_chips_call(p, recv, chip_c_idx, tag):
    _, h, n = p.shape
    tr = h // 2

    def body(k_ref, p_ref, r_ref, o_ref):
        acc = p_ref[...].astype(F32)
        for j in range(3):
            acc = acc + r_ref[j].astype(F32)
        o_ref[...] = acc

    return pl.pallas_call(
        body, name="rs_add_chips_" + tag, out_shape=SDS((2, h, n), F32),
        grid_spec=pltpu.PrefetchScalarGridSpec(
            num_scalar_prefetch=1, grid=(2,),
            in_specs=[BS((None, tr, n), lambda r, k_ref: (k_ref[0], r, 0)),
                      BS((3, tr, n), lambda r, k_ref: (0, r, 0))],
            out_specs=BS((None, tr, n), lambda r, k_ref: (k_ref[1], r, 0))),
        compiler_params=_params("arbitrary"),
    )(chip_c_idx, p, recv)


def _load_rows(wg_hbm, w_vmem, sem, off):
    cp = pltpu.make_async_copy(wg_hbm.at[:, pl.ds(off, w_vmem.shape[1]), :], w_vmem, sem)
    cp.start()
    return cp


def _fwd_in_call(x, mod, pre_tm, wg, b_in, pack):
    S = x.shape[0]
    ranges = [(O_BRA, 3 * R_BR)]

    def body(x_ref, mod_ref, g_ref, w_hbm, b_ref, pack_ref, p_ref, h_ref, wg_out, w_vmem, sem, send_sems, recv_sems):
        start, finish = _pack_gather(pack_ref, wg_out, send_sems, recv_sems, ranges)

        @pl.when(pl.program_id(0) == 0)
        def _():
            start()
            _load_rows(w_hbm, w_vmem, sem, O_IN).wait()

        xv = x_ref[...]
        r = lax.rsqrt(jnp.mean(xv * xv, axis=-1, keepdims=True) + EPS)
        h = xv * r * g_ref[...] * (1.0 + mod_ref[:, D:2 * D]) + mod_ref[:, 0:D]
        hb = h.astype(BF16)
        h_ref[...] = hb
        for k in range(IN_COLS // D):
            w_blk = w_vmem[k // 2, (k % 2) * D:(k % 2 + 1) * D, :]
            p_ref[:, k * D:(k + 1) * D] = _mm(hb, w_blk) + b_ref[:, k * D:(k + 1) * D]

        @pl.when(pl.program_id(0) == S // TM - 1)
        def _():
            finish()

    hbm = BS(memory_space=pl.ANY)
    return pl.pallas_call(
        body, name="fwd_in", grid=(S // TM,),
        out_shape=(SDS((S, IN_COLS), F32), SDS((S, D), BF16), SDS(wg.shape, wg.dtype)),
        in_specs=[BS((TM, D), lambda i: (i, 0)), BS((1, 6 * D), lambda i: (0, 0)), BS((1, D), lambda i: (0, 0)),
                  hbm, BS((1, IN_COLS), lambda i: (0, 0)), hbm],
        out_specs=(BS((TM, IN_COLS), lambda i: (i, 0)), BS((TM, D), lambda i: (i, 0)), hbm),
        scratch_shapes=[pltpu.VMEM((N_CHIPS, R_IN, D), BF16), pltpu.SemaphoreType.DMA] + _gather_sems(len(ranges)),
        input_output_aliases={3: 2},
        compiler_params=_params("arbitrary"),
    )(x, mod, pre_tm, wg, b_in, pack)


def _lower_bound(lg_ref):
    l0, l1 = lg_ref[0:1, :], lg_ref[1:2, :]
    mx = jnp.maximum(l0, l1)
    e0, e1 = jnp.exp(l0 - mx), jnp.exp(l1 - mx)
    return e0 / (e0 + e1)


def _tri_masks():
    ri = lax.broadcasted_iota(jnp.int32, (CHUNK, CHUNK), 0)
    ci = lax.broadcasted_iota(jnp.int32, (CHUNK, CHUNK), 1)
    return (ri >= ci).astype(F32), (ci >= ri).astype(F32)


def _cumsum_mm(tri, g):
    tb = tri.astype(BF16)
    hi = g.astype(BF16)
    r1 = g - hi.astype(F32)
    mid = r1.astype(BF16)
    lo = (r1 - mid.astype(F32)).astype(BF16)
    return _mm(tb, hi) + _mm(tb, mid) + _mm(tb, lo)


def _hg_gates(q_r, f_r, lb, tril):
    sq = _sig(q_r)
    q = q_r * sq
    sf = _sig(f_r)
    f = lb + (1.0 - lb) * sf
    k = 1.0 - f
    g = jnp.log(f)
    b = _cumsum_mm(tril, g)
    b_last = _rowsum(g)
    row = lax.broadcasted_iota(jnp.int32, g.shape, 0)
    ref = _rowsum(jnp.where(row < CHUNK // 2, g, 0.0))
    e = jnp.exp(b)
    eq = jnp.exp(jnp.minimum(b - ref, 80.0))
    ek = jnp.exp(jnp.minimum(ref - b, 80.0))
    dd = jnp.exp(b_last - b)
    return dict(sq=sq, q=q, sf=sf, f=f, k=k, e=e, eq=eq, ek=ek, dd=dd, elast=jnp.exp(b_last),
                qe=q * e, qt=q * eq, kt=k * ek, kd=k * dd)


def _hgrn_fwd_call(p, logits, gn, wg, pack):
    S = p.shape[0]
    ncb = TB // CHUNK
    ranges = [(O_FF1, R_FF)]

    def body(q_ref, f_ref, v_ref, og_ref, lg_ref, gn_ref, wg_in, pack_ref, o_ref, oa_ref, st_ref, wg_out,
             st_scr, send_sems, recv_sems):
        start, finish = _pack_gather(pack_ref, wg_out, send_sems, recv_sems, ranges)

        @pl.when(pl.program_id(0) == 0)
        def _():
            start()
            st_scr[...] = jnp.zeros_like(st_scr)

        lb = _lower_bound(lg_ref)
        tril, _ = _tri_masks()

        def chunk(ci, carry):
            rows = pl.ds(pl.multiple_of(ci * CHUNK, CHUNK), CHUNK)
            st_ref[ci] = st_scr[...]
            t = _hg_gates(q_ref[rows, :], f_ref[rows, :], lb, tril)
            v = v_ref[rows, :]
            for h in range(HEADS):
                sl = slice(h * DK, (h + 1) * DK)
                stp = st_scr[:, sl]
                vb = v[:, sl].astype(BF16)
                inter = _mm(t["qe"][:, sl].astype(BF16), stp.astype(BF16), NT)
                a = jnp.where(tril > 0.5, _mm(t["qt"][:, sl].astype(BF16), t["kt"][:, sl].astype(BF16), NT), 0.0)
                o = inter + _mm(a.astype(BF16), vb)
                st_scr[:, sl] = stp * t["elast"][:, sl] + _mm(vb, t["kd"][:, sl].astype(BF16), TN)
                oh = o * lax.rsqrt(jnp.mean(o * o, axis=-1, keepdims=True) + EPS)
                og = og_ref[rows, sl]
                o_ref[rows, sl] = o
                oa_ref[rows, sl] = (oh * gn_ref[:, sl] * (og * _sig(og))).astype(BF16)
            return carry

        lax.fori_loop(0, ncb, chunk, 0)

        @pl.when(pl.program_id(0) == S // TB - 1)
        def _():
            finish()

    col = lambda j: BS((TB, D), lambda i, j=j: (i, j))
    hbm = BS(memory_space=pl.ANY)
    return pl.pallas_call(
        body, name="hgrn_fwd", grid=(S // TB,),
        out_shape=(SDS((S, D), F32), SDS((S, D), BF16), SDS((S // CHUNK, DK, D), F32), SDS(wg.shape, wg.dtype)),
        in_specs=[col(0), col(1), col(2), col(3), BS((2, D), lambda i: (0, 0)), BS((1, D), lambda i: (0, 0)),
                  hbm, hbm],
        out_specs=(BS((TB, D), lambda i: (i, 0)), BS((TB, D), lambda i: (i, 0)),
                   BS((ncb, DK, D), lambda i: (i, 0, 0)), hbm),
        scratch_shapes=[pltpu.VMEM((DK, D), F32)] + _gather_sems(len(ranges)),
        input_output_aliases={6: 3},
        compiler_params=_params("arbitrary"),
    )(p, p, p, p, logits, gn, wg, pack)


def _layernorm_stats(uc):
    mu = jnp.mean(uc, axis=-1, keepdims=True)
    xc = uc - mu
    rs = lax.rsqrt(jnp.mean(xc * xc, axis=-1, keepdims=True) + EPS)
    return xc * rs, rs


EXT = HALO + TM + 8


def _fill_shifted(ext, shifted):
    for m in range(1, 8):
        shifted[m - 1] = ext[m:m + HALO + TM, :]


def _window(ext, shifted, s0, n):
    m = s0 % 8
    q = s0 - m
    return ext[q:q + n, :] if m == 0 else shifted[m - 1, q:q + n, :]


def _conv_fwd_call(p, dw, db, ln_g, ln_b, wg, pack):
    S = p.shape[0]
    ranges = [(O_FF2, R_FF)]

    def body(cv_ref, cg_ref, dw_ref, db_ref, g_ref, b_ref, wg_in, pack_ref, u_ref, uc_ref, cb_ref, wg_out,
             uext, ush, send_sems, recv_sems):
        start, finish = _pack_gather(pack_ref, wg_out, send_sems, recv_sems, ranges)

        @pl.when(pl.program_id(0) == 0)
        def _():
            start()
            uext[0:HALO, :] = jnp.zeros((HALO, D), F32)
            uext[HALO + TM:EXT, :] = jnp.zeros((EXT - HALO - TM, D), F32)

        u = cv_ref[...] * _sig(cg_ref[...])
        uext[HALO:HALO + TM, :] = u
        u_ref[...] = u
        _fill_shifted(uext, ush)
        for rb in range(TM // SUB):
            acc = jnp.broadcast_to(db_ref[...], (SUB, D))
            for j in range(CONV_K):
                s0 = HALO - (CONV_K - 1) + j + rb * SUB
                acc = acc + dw_ref[j:j + 1, :] * _window(uext, ush, s0, SUB)
            uc_ref[rb * SUB:(rb + 1) * SUB, :] = acc
            xh, _ = _layernorm_stats(acc)
            ln = xh * g_ref[...] + b_ref[...]
            cb_ref[rb * SUB:(rb + 1) * SUB, :] = (ln * _sig(ln)).astype(BF16)
        uext[0:HALO, :] = uext[TM:TM + HALO, :]

        @pl.when(pl.program_id(0) == S // TM - 1)
        def _():
            finish()

    vec = BS((1, D), lambda i: (0, 0))
    hbm = BS(memory_space=pl.ANY)
    return pl.pallas_call(
        body, name="conv_fwd", grid=(S // TM,),
        out_shape=(SDS((S, D), F32), SDS((S, D), F32), SDS((S, D), BF16), SDS(wg.shape, wg.dtype)),
        in_specs=[BS((TM, D), lambda i: (i, 4)), BS((TM, D), lambda i: (i, 5)),
                  BS((CONV_K, D), lambda i: (0, 0)), vec, vec, vec, hbm, hbm],
        out_specs=(BS((TM, D), lambda i: (i, 0)),) * 3 + (hbm,),
        scratch_shapes=[pltpu.VMEM((EXT, D), F32), pltpu.VMEM((7, HALO + TM, D), F32)] + _gather_sems(len(ranges)),
        input_output_aliases={6: 3},
        compiler_params=_params("arbitrary"),
    )(p, p, dw, db, ln_g, ln_b, wg, pack)


def _mm_rows(a, w_ref):
    acc = _mm(a[:, 0:R_BR], w_ref[0])
    for k in range(1, N_CHIPS):
        acc = acc + _mm(a[:, k * R_BR:(k + 1) * R_BR], w_ref[k])
    return acc


def _mm_rows_t(a, w_ref):
    return jnp.concatenate([_mm(a, w_ref[k], NT) for k in range(N_CHIPS)], axis=1)


def _br_spec(off):
    return BS((N_CHIPS, R_BR, D), lambda i: (0, off // R_BR, 0))


def _merge_fwd_call(oa, cb, p, x, mod, post_tm, pre_cm, wg):
    S = x.shape[0]

    def body(oa_ref, cb_ref, ga_ref, gb_ref, x_ref, mod_ref, post_ref, pre_ref, wa_ref, wb_ref, wo_ref,
             ya_ref, yb_ref, mg_ref, y_ref, x2_ref, h2_ref):
        ya = _mm_rows(oa_ref[...], wa_ref)
        yb = _mm_rows(cb_ref[...], wb_ref)
        ya_ref[...] = ya
        yb_ref[...] = yb
        mg = (_sig(ga_ref[...]) * ya + _sig(gb_ref[...]) * yb).astype(BF16)
        mg_ref[...] = mg
        y = _mm_rows(mg, wo_ref)
        y_ref[...] = y
        n = y * lax.rsqrt(jnp.mean(y * y, axis=-1, keepdims=True) + EPS) * post_ref[...]
        x2 = x_ref[...] + mod_ref[:, 2 * D:3 * D] * n
        x2_ref[...] = x2
        r2 = lax.rsqrt(jnp.mean(x2 * x2, axis=-1, keepdims=True) + EPS)
        h2 = x2 * r2 * pre_ref[...] * (1.0 + mod_ref[:, 4 * D:5 * D]) + mod_ref[:, 3 * D:4 * D]
        h2_ref[...] = h2.astype(BF16)

    tile = BS((TM, D), lambda i: (i, 0))
    vec = BS((1, D), lambda i: (0, 0))
    return pl.pallas_call(
        body, name="merge_fwd", grid=(S // TM,),
        out_shape=(SDS((S, D), F32), SDS((S, D), F32), SDS((S, D), BF16), SDS((S, D), F32), SDS((S, D), F32),
                   SDS((S, D), BF16)),
        in_specs=[tile, tile, BS((TM, D), lambda i: (i, 6)), BS((TM, D), lambda i: (i, 7)), tile,
                  BS((1, 6 * D), lambda i: (0, 0)), vec, vec, _br_spec(O_BRA), _br_spec(O_BRB), _br_spec(O_OUT)],
        out_specs=(tile,) * 6,
        compiler_params=_params("arbitrary"),
    )(oa, cb, p, p, x, mod, post_tm, pre_cm, wg, wg, wg)


def _ffn_call(h2, x2, target, mod, post_cm, pre_cm, wg):
    S = x2.shape[0]

    def body(h2_ref, x2_ref, t_ref, mod_ref, post_ref, pre_ref, w_hbm,
             z_ref, da_ref, dy2_ref, dx2_ref, acc_ref, w1_v, w2_v, ra_scr, sems):
        @pl.when(pl.program_id(0) == 0)
        def _():
            c1 = _load_rows(w_hbm, w1_v, sems.at[0], O_FF1)
            c2 = _load_rows(w_hbm, w2_v, sems.at[1], O_FF2)
            c1.wait()
            c2.wait()
            acc_ref[...] = jnp.zeros_like(acc_ref)

        h2 = h2_ref[...]
        for k in range(N_CHIPS):
            ra = jnp.maximum(_mm(h2, w1_v[k]), 0.0)
            ra_scr[:, k * D:(k + 1) * D] = ra
            z_ref[:, k * D:(k + 1) * D] = (ra * ra).astype(BF16)
        y2 = _mm(z_ref[:, 0:D], w2_v[0])
        for k in range(1, N_CHIPS):
            y2 = y2 + _mm(z_ref[:, k * D:(k + 1) * D], w2_v[k])
        ry = lax.rsqrt(jnp.mean(y2 * y2, axis=-1, keepdims=True) + EPS)
        yn = y2 * ry
        n = yn * post_ref[...]
        g2 = mod_ref[:, 5 * D:6 * D]
        x2 = x2_ref[...]
        err = x2 + g2 * n - t_ref[...]
        acc_ref[5:6, :] += _rowsum(err * err) * (0.5 / D)
        dout = err * (1.0 / D)
        acc_ref[0:1, :] += _rowsum(dout * n)
        dn = dout * g2
        acc_ref[1:2, :] += _rowsum(dn * yn)
        dyn = dn * post_ref[...]
        dy2 = (ry * (dyn - yn * jnp.mean(dyn * yn, axis=-1, keepdims=True))).astype(BF16)
        dy2_ref[...] = dy2
        for k in range(N_CHIPS):
            dz = _mm(dy2, w2_v[k], NT)
            da_ref[:, k * D:(k + 1) * D] = (dz * (2.0 * ra_scr[:, k * D:(k + 1) * D])).astype(BF16)
        dh2 = jnp.zeros((TM, D), F32)
        for k in range(N_CHIPS):
            dh2 = dh2 + _mm(da_ref[:, k * D:(k + 1) * D], w1_v[k], NT)
        r2 = lax.rsqrt(jnp.mean(x2 * x2, axis=-1, keepdims=True) + EPS)
        xn = x2 * r2
        yv = xn * pre_ref[...]
        acc_ref[2:3, :] += _rowsum(dh2)
        acc_ref[3:4, :] += _rowsum(dh2 * yv)
        dyv = dh2 * (1.0 + mod_ref[:, 4 * D:5 * D])
        acc_ref[4:5, :] += _rowsum(dyv * xn)
        dxn = dyv * pre_ref[...]
        dx2_ref[...] = dout + r2 * (dxn - xn * jnp.mean(dxn * xn, axis=-1, keepdims=True))

    tile = BS((TM, D), lambda i: (i, 0))
    wide = BS((TM, D_FF), lambda i: (i, 0))
    vec = BS((1, D), lambda i: (0, 0))
    return pl.pallas_call(
        body, name="ffn_fwd_bwd", grid=(S // TM,),
        out_shape=(SDS((S, D_FF), BF16), SDS((S, D_FF), BF16), SDS((S, D), BF16), SDS((S, D), F32),
                   SDS((8, D), F32)),
        in_specs=[tile, tile, tile, BS((1, 6 * D), lambda i: (0, 0)), vec, vec, BS(memory_space=pl.ANY)],
        out_specs=(wide, wide, tile, tile, BS((8, D), lambda i: (0, 0))),
        scratch_shapes=[pltpu.VMEM((N_CHIPS, R_FF, D), BF16), pltpu.VMEM((N_CHIPS, R_FF, D), BF16),
                        pltpu.VMEM((TM, D_FF), F32),
                        pltpu.SemaphoreType.DMA((2,))],
        compiler_params=_params("arbitrary"),
    )(h2, x2, target, mod, post_cm, pre_cm, wg)


def _merge_bwd_call(dx2, y, ya, yb, p, mod, post_tm, wg):
    S = y.shape[0]

    def body(dx2_ref, y_ref, ya_ref, yb_ref, ga_ref, gb_ref, mod_ref, post_ref, wa_ref, wb_ref, wo_ref,
             dy_ref, dya_ref, dyb_ref, doa_ref, dcb_ref, dpg_ref, acc_ref, bsum_ref):
        @pl.when(pl.program_id(0) == 0)
        def _():
            acc_ref[...] = jnp.zeros_like(acc_ref)
            bsum_ref[...] = jnp.zeros_like(bsum_ref)

        y = y_ref[...]
        ry = lax.rsqrt(jnp.mean(y * y, axis=-1, keepdims=True) + EPS)
        yn = y * ry
        dx2 = dx2_ref[...]
        acc_ref[0:1, :] += _rowsum(dx2 * (yn * post_ref[...]))
        dn = dx2 * mod_ref[:, 2 * D:3 * D]
        acc_ref[1:2, :] += _rowsum(dn * yn)
        dyn = dn * post_ref[...]
        dy = (ry * (dyn - yn * jnp.mean(dyn * yn, axis=-1, keepdims=True))).astype(BF16)
        dy_ref[...] = dy
        dmg = _mm_rows_t(dy, wo_ref)
        sa, sb = _sig(ga_ref[...]), _sig(gb_ref[...])
        dya = (dmg * sa).astype(BF16)
        dyb = (dmg * sb).astype(BF16)
        dya_ref[...] = dya
        dyb_ref[...] = dyb
        dga = dmg * ya_ref[...] * (sa * (1.0 - sa))
        dgb = dmg * yb_ref[...] * (sb * (1.0 - sb))
        dpg_ref[:, 0:D] = dga.astype(BF16)
        dpg_ref[:, D:2 * D] = dgb.astype(BF16)
        bsum_ref[:, 0:D] += _rowsum(dga)
        bsum_ref[:, D:2 * D] += _rowsum(dgb)
        doa_ref[...] = _mm_rows_t(dya, wa_ref)
        dcb_ref[...] = _mm_rows_t(dyb, wb_ref)

    tile = BS((TM, D), lambda i: (i, 0))
    vec = BS((1, D), lambda i: (0, 0))
    return pl.pallas_call(
        body, name="merge_bwd", grid=(S // TM,),
        out_shape=(SDS((S, D), BF16), SDS((S, D), BF16), SDS((S, D), BF16), SDS((S, D), F32), SDS((S, D), F32),
                   SDS((S, 2 * D), BF16), SDS((8, D), F32), SDS((1, 2 * D), F32)),
        in_specs=[tile, tile, tile, tile, BS((TM, D), lambda i: (i, 6)), BS((TM, D), lambda i: (i, 7)),
                  BS((1, 6 * D), lambda i: (0, 0)), vec, _br_spec(O_BRA), _br_spec(O_BRB), _br_spec(O_OUT)],
        out_specs=(tile, tile, tile, tile, tile, BS((TM, 2 * D), lambda i: (i, 0)),
                   BS((8, D), lambda i: (0, 0)), BS((1, 2 * D), lambda i: (0, 0))),
        compiler_params=_params("arbitrary"),
    )(dx2, y, ya, yb, p, p, mod, post_tm, wg, wg, wg)


def _hgrn_bwd_call(p, o, doa, st, logits, gn, part):
    S = p.shape[0]
    nb = S // TB
    ncb = TB // CHUNK

    def body(q_ref, f_ref, v_ref, og_ref, o_ref, doa_ref, st_ref, lg_ref, gn_ref, part_ref,
             dp_ref, bsum_ref, dlg_ref, dgn_ref, recv_ref,
             dst_scr, dlb_scr, dqe_s, dqt_s, dkt_s, dkd_s, dv_s, dog_s, dble_s, send_sems, recv_sems):
        i = pl.program_id(0)
        start, finish = _chip_exchange(part_ref, recv_ref, send_sems, recv_sems)

        @pl.when(i == 0)
        def _():
            start()
            dst_scr[...] = jnp.zeros_like(dst_scr)
            dlb_scr[...] = jnp.zeros_like(dlb_scr)
            bsum_ref[...] = jnp.zeros_like(bsum_ref)
            dgn_ref[...] = jnp.zeros_like(dgn_ref)

        lb = _lower_bound(lg_ref)
        tril, triu = _tri_masks()

        def chunk(tt, carry):
            ci = ncb - 1 - tt
            rows = pl.ds(pl.multiple_of(ci * CHUNK, CHUNK), CHUNK)
            q_r, f_r = q_ref[rows, :], f_ref[rows, :]
            t = _hg_gates(q_r, f_r, lb, tril)
            v = v_ref[rows, :]
            for h in range(HEADS):
                sl = slice(h * DK, (h + 1) * DK)
                stp = st_ref[ci, :, sl]
                stb = stp.astype(BF16)
                qeb = t["qe"][:, sl].astype(BF16)
                qtb = t["qt"][:, sl].astype(BF16)
                ktb = t["kt"][:, sl].astype(BF16)
                kdb = t["kd"][:, sl].astype(BF16)
                vb = v[:, sl].astype(BF16)
                a = jnp.where(tril > 0.5, _mm(qtb, ktb, NT), 0.0)
                o_h = o_ref[rows, sl]
                rinv = lax.rsqrt(jnp.mean(o_h * o_h, axis=-1, keepdims=True) + EPS)
                oh = o_h * rinv
                og = og_ref[rows, sl]
                so = _sig(og)
                d_oa = doa_ref[rows, sl]
                don = d_oa * (og * so)
                dog_s[:, sl] = d_oa * (oh * gn_ref[:, sl]) * _dsilu(og, so)
                dgn_ref[:, sl] += _rowsum(don * oh)
                doh = don * gn_ref[:, sl]
                do = (rinv * (doh - oh * jnp.mean(doh * oh, axis=-1, keepdims=True))).astype(BF16)
                dqe_s[:, sl] = _mm(do, stb, NN)
                dstp = _mm(do, qeb, TN)
                dab = jnp.where(tril > 0.5, _mm(do, vb, NT), 0.0).astype(BF16)
                dqt_s[:, sl] = _mm(dab, ktb, NN)
                dkt_s[:, sl] = _mm(dab, qtb, TN)
                dstn = dst_scr[:, sl]
                dsb = dstn.astype(BF16)
                dkd_s[:, sl] = _mm(vb, dsb, NN)
                dv_s[:, sl] = _mm(a.astype(BF16), do, TN) + _mm(kdb, dsb, NT)
                el = t["elast"][:, sl]
                dst_scr[:, sl] = dstn * el + dstp
                dble_s[:, sl] = el * _rowsum(stp * dstn)
            dqe, dqt, dkt, dkd = dqe_s[...], dqt_s[...], dkt_s[...], dkd_s[...]
            dq = dqe * t["e"] + dqt * t["eq"]
            dk = dkt * t["ek"] + dkd * t["dd"]
            dkk = dkd * t["kd"]
            qt_r = t["qt"].astype(BF16).astype(F32)
            kt_r = t["kt"].astype(BF16).astype(F32)
            dbv = dqe * t["qe"] + dqt * qt_r - dkt * kt_r - dkk
            dg = _cumsum_mm(triu, dbv) + (_rowsum(dkk) + dble_s[...])
            df = dg / t["f"] - dk
            sf = t["sf"]
            dlb_scr[...] += _rowsum(df * (1.0 - sf))
            dqr = dq * _dsilu(q_r, t["sq"])
            dfr = df * (1.0 - lb) * (sf * (1.0 - sf))
            dvv, dog = dv_s[...], dog_s[...]
            dp_ref[rows, 0:D] = dqr.astype(BF16)
            dp_ref[rows, D:2 * D] = dfr.astype(BF16)
            dp_ref[rows, 2 * D:3 * D] = dvv.astype(BF16)
            dp_ref[rows, 3 * D:4 * D] = dog.astype(BF16)
            bsum_ref[:, 0:D] += _rowsum(dqr)
            bsum_ref[:, D:2 * D] += _rowsum(dfr)
            bsum_ref[:, 2 * D:3 * D] += _rowsum(dvv)
            bsum_ref[:, 3 * D:4 * D] += _rowsum(dog)
            return carry

        lax.fori_loop(0, ncb, chunk, 0)

        dl = dlb_scr[...] * lb * (1.0 - lb)
        dlg_ref[0:1, :] = dl
        dlg_ref[1:2, :] = -dl

        @pl.when(i == nb - 1)
        def _():
            finish()

    col = lambda j: BS((TB, D), lambda i, j=j: (nb - 1 - i, j))
    rev = BS((TB, D), lambda i: (nb - 1 - i, 0))
    cd = pltpu.VMEM((CHUNK, D), F32)
    return pl.pallas_call(
        body, name="hgrn_bwd", grid=(nb,),
        out_shape=(SDS((S, 4 * D), BF16), SDS((1, 4 * D), F32), SDS((2, D), F32), SDS((1, D), F32),
                   SDS((3,) + part.shape[1:], part.dtype)),
        in_specs=[col(0), col(1), col(2), col(3), rev, rev, BS((ncb, DK, D), lambda i: (nb - 1 - i, 0, 0)),
                  BS((2, D), lambda i: (0, 0)), BS((1, D), lambda i: (0, 0)), BS(memory_space=pl.ANY)],
        out_specs=(BS((TB, 4 * D), lambda i: (nb - 1 - i, 0)), BS((1, 4 * D), lambda i: (0, 0)),
                   BS((2, D), lambda i: (0, 0)), BS((1, D), lambda i: (0, 0)), BS(memory_space=pl.ANY)),
        scratch_shapes=[pltpu.VMEM((DK, D), F32), pltpu.VMEM((1, D), F32), cd, cd, cd, cd, cd, cd,
                        pltpu.VMEM((1, D), F32)] + _exchange_sems(),
        compiler_params=_params("arbitrary"),
    )(p, p, p, p, o, doa, st, logits, gn, part)


def _conv_bwd_call(dcb, uc, u, p, dw, ln_g, ln_b, part):
    S = uc.shape[0]
    nb = S // TM
    hb = TM // HALO

    def body(dcb_ref, uc_ref, u_ref, uh_ref, cv_ref, cg_ref, dw_ref, g_ref, b_ref, part_ref,
             dp_ref, bsum_ref, ddw_ref, acc_ref, recv_ref, uext, dext, ush, dsh, send_sems, recv_sems):
        i = pl.program_id(0)
        start, finish = _chip_exchange(part_ref, recv_ref, send_sems, recv_sems)

        @pl.when(i == 0)
        def _():
            start()
            dext[TM:EXT, :] = jnp.zeros((EXT - TM, D), F32)
            uext[HALO + TM:EXT, :] = jnp.zeros((EXT - HALO - TM, D), F32)
            bsum_ref[...] = jnp.zeros_like(bsum_ref)
            ddw_ref[...] = jnp.zeros_like(ddw_ref)
            acc_ref[...] = jnp.zeros_like(acc_ref)

        first_tile = (nb - 1 - i) == 0
        uext[0:HALO, :] = jnp.where(first_tile, 0.0, uh_ref[...])
        uext[HALO:HALO + TM, :] = u_ref[...]
        _fill_shifted(uext, ush)

        for rb in range(TM // SUB):
            rs_ = slice(rb * SUB, (rb + 1) * SUB)
            xh, rs = _layernorm_stats(uc_ref[rs_, :])
            ln = xh * g_ref[...] + b_ref[...]
            dln = dcb_ref[rs_, :] * _dsilu(ln, _sig(ln))
            acc_ref[1:2, :] += _rowsum(dln * xh)
            acc_ref[2:3, :] += _rowsum(dln)
            dxh = dln * g_ref[...]
            duc = rs * (dxh - jnp.mean(dxh, axis=-1, keepdims=True)
                        - xh * jnp.mean(dxh * xh, axis=-1, keepdims=True))
            dext[rs_, :] = duc
            acc_ref[0:1, :] += _rowsum(duc)
        _fill_shifted(dext, dsh)

        for j in range(CONV_K):
            part = jnp.zeros((SUB, D), F32)
            for rb in range(TM // SUB):
                s0 = HALO - (CONV_K - 1) + j + rb * SUB
                part = part + dext[rb * SUB:(rb + 1) * SUB, :] * _window(uext, ush, s0, SUB)
            ddw_ref[j:j + 1, :] += _rowsum(part)

        for rb in range(TM // SUB):
            rs_ = slice(rb * SUB, (rb + 1) * SUB)
            du = jnp.zeros((SUB, D), F32)
            for j in range(CONV_K):
                s0 = rb * SUB + (CONV_K - 1) - j
                du = du + dw_ref[j:j + 1, :] * _window(dext, dsh, s0, SUB)
            cg = cg_ref[rs_, :]
            sg = _sig(cg)
            dcv = du * sg
            dcg = du * cv_ref[rs_, :] * (sg * (1.0 - sg))
            dp_ref[rs_, 0:D] = dcv.astype(BF16)
            dp_ref[rs_, D:2 * D] = dcg.astype(BF16)
            bsum_ref[:, 0:D] += _rowsum(dcv)
            bsum_ref[:, D:2 * D] += _rowsum(dcg)

        dext[TM:TM + HALO, :] = dext[0:HALO, :]

        @pl.when(i == nb - 1)
        def _():
            finish()

    rev = BS((TM, D), lambda i: (nb - 1 - i, 0))
    vec = BS((1, D), lambda i: (0, 0))
    return pl.pallas_call(
        body, name="conv_bwd", grid=(nb,),
        out_shape=(SDS((S, 2 * D), BF16), SDS((1, 2 * D), F32), SDS((32, D), F32), SDS((8, D), F32),
                   SDS((3,) + part.shape[1:], part.dtype)),
        in_specs=[rev, rev, rev, BS((HALO, D), lambda i: (jnp.maximum((nb - 1 - i) * hb - 1, 0), 0)),
                  BS((TM, D), lambda i: (nb - 1 - i, 4)), BS((TM, D), lambda i: (nb - 1 - i, 5)),
                  BS((CONV_K, D), lambda i: (0, 0)), vec, vec, BS(memory_space=pl.ANY)],
        out_specs=(BS((TM, 2 * D), lambda i: (nb - 1 - i, 0)), BS((1, 2 * D), lambda i: (0, 0)),
                   BS((32, D), lambda i: (0, 0)), BS((8, D), lambda i: (0, 0)), BS(memory_space=pl.ANY)),
        scratch_shapes=[pltpu.VMEM((EXT, D), F32), pltpu.VMEM((EXT, D), F32),
                        pltpu.VMEM((7, HALO + TM, D), F32), pltpu.VMEM((7, HALO + TM, D), F32)] + _exchange_sems(),
        compiler_params=_params("arbitrary"),
    )(dcb, uc, u, u, p, p, dw, ln_g, ln_b, part)


def _in_bwd_call(dp_hg, dp_cv, dp_gt, x, dx2, mod, pre_tm, wg, part):
    S = x.shape[0]

    def body(hg_ref, cv_ref, gt_ref, x_ref, dx2_ref, mod_ref, g_ref, w_hbm, part_ref, gx_ref, acc_ref, recv_ref,
             w_vmem, sem, send_sems, recv_sems):
        start, finish = _chip_exchange(part_ref, recv_ref, send_sems, recv_sems)

        @pl.when(pl.program_id(0) == 0)
        def _():
            start()
            _load_rows(w_hbm, w_vmem, sem, O_IN).wait()
            acc_ref[...] = jnp.zeros_like(acc_ref)

        dh = jnp.zeros((TM, D), F32)
        for k in range(IN_COLS // D):
            src, kk = ((hg_ref, k), (cv_ref, k - 4), (gt_ref, k - 6))[0 if k < 4 else (1 if k < 6 else 2)]
            dh = dh + _mm(src[:, kk * D:(kk + 1) * D], w_vmem[k // 2, (k % 2) * D:(k % 2 + 1) * D, :], NT)
        xv = x_ref[...]
        r = lax.rsqrt(jnp.mean(xv * xv, axis=-1, keepdims=True) + EPS)
        xn = xv * r
        yv = xn * g_ref[...]
        acc_ref[0:1, :] += _rowsum(dh)
        acc_ref[1:2, :] += _rowsum(dh * yv)
        dyv = dh * (1.0 + mod_ref[:, D:2 * D])
        acc_ref[2:3, :] += _rowsum(dyv * xn)
        dxn = dyv * g_ref[...]
        gx_ref[...] = dx2_ref[...] + r * (dxn - xn * jnp.mean(dxn * xn, axis=-1, keepdims=True))

        @pl.when(pl.program_id(0) == S // TM - 1)
        def _():
            finish()

    tile = BS((TM, D), lambda i: (i, 0))
    return pl.pallas_call(
        body, name="in_bwd", grid=(S // TM,),
        out_shape=(SDS((S, D), F32), SDS((8, D), F32), SDS((3,) + part.shape[1:], part.dtype)),
        in_specs=[BS((TM, 4 * D), lambda i: (i, 0)), BS((TM, 2 * D), lambda i: (i, 0)),
                  BS((TM, 2 * D), lambda i: (i, 0)), tile, tile, BS((1, 6 * D), lambda i: (0, 0)),
                  BS((1, D), lambda i: (0, 0)), BS(memory_space=pl.ANY), BS(memory_space=pl.ANY)],
        out_specs=(tile, BS((8, D), lambda i: (0, 0)), BS(memory_space=pl.ANY)),
        scratch_shapes=[pltpu.VMEM((N_CHIPS, R_IN, D), BF16), pltpu.SemaphoreType.DMA] + _exchange_sems(),
        compiler_params=_params("arbitrary"),
    )(dp_hg, dp_cv, dp_gt, x, dx2, mod, pre_tm, wg, part)


def _wgrad_call(gp, a, b, name, bm, place, rows):
    S, M = a.shape
    N = b.shape[1]
    bk = min(S, 1024)
    nk = S // bk

    def body(a_ref, b_ref, *rest):
        o_ref, acc = rest[-2], rest[-1]
        k = pl.program_id(2)

        @pl.when(k == 0)
        def _():
            acc[...] = jnp.zeros_like(acc)

        acc[...] += _mm(a_ref[...], b_ref[...], TN)

        @pl.when(k == nk - 1)
        def _():
            o_ref[...] = acc[...].astype(BF16)

    in_specs = [BS((bk, bm), lambda i, j, k: (k, i)), BS((bk, D), lambda i, j, k: (k, j))]
    args = [a, b]
    if gp is not None:
        in_specs.append(BS(memory_space=pl.ANY))
        args.append(gp)
    return pl.pallas_call(
        body, name=name, grid=(M // bm, N // D, nk),
        out_shape=SDS((N_CHIPS, rows, D), BF16),
        in_specs=in_specs,
        out_specs=BS((None, bm, D), lambda i, j, k: (*place(i, j), 0)),
        scratch_shapes=[pltpu.VMEM((bm, D), F32)],
        input_output_aliases={} if gp is None else {2: 0},
        compiler_params=_params("parallel", "parallel", "arbitrary"),
    )(*args)


def _wgrad_rows_call(gp, a, b, name, blk):
    S = a.shape[0]
    bk = min(S, 1024)
    nk = S // bk

    def body(a_ref, b_ref, *rest):
        o_ref, acc = rest[-2], rest[-1]
        k = pl.program_id(0)

        @pl.when(k == 0)
        def _():
            acc[...] = jnp.zeros_like(acc)

        acc[...] += _mm(a_ref[...], b_ref[...], TN)

        @pl.when(k == nk - 1)
        def _():
            for c in range(N_CHIPS):
                o_ref[c] = acc[c * R_BR:(c + 1) * R_BR, :].astype(BF16)

    in_specs = [BS((bk, D), lambda k: (k, 0)), BS((bk, D), lambda k: (k, 0))]
    args = [a, b]
    if gp is not None:
        in_specs.append(BS(memory_space=pl.ANY))
        args.append(gp)
    return pl.pallas_call(
        body, name=name, grid=(nk,),
        out_shape=SDS((N_CHIPS, 3 * R_BR, D), BF16),
        in_specs=in_specs,
        out_specs=BS((N_CHIPS, R_BR, D), lambda k: (0, blk, 0)),
        scratch_shapes=[pltpu.VMEM((D, D), F32)],
        input_output_aliases={} if gp is None else {2: 0},
        compiler_params=_params("arbitrary"),
    )(*args)


def _outer_call(cact, dmod):
    n = dmod.shape[1]

    def body(a_ref, b_ref, o_ref):
        o_ref[...] = _mm(a_ref[...], b_ref[...], TN, HI)

    return pl.pallas_call(
        body, name="wgrad_ada", out_shape=SDS((D, n), F32),
        compiler_params=pltpu.CompilerParams(vmem_limit_bytes=VMEM_LIMIT),
    )(cact, dmod)


def _adamw_call(w, g, m, v, name):
    R, C = w.shape
    tr = R
    while tr * C > 512 * 1024 and tr % 16 == 0:
        tr //= 2
    c1 = 1.0 - ADAM_B1 ** ADAM_STEP
    c2 = 1.0 - ADAM_B2 ** ADAM_STEP

    def body(w_ref, g_ref, m_ref, v_ref, d_ref, m2_ref, v2_ref):
        g = g_ref[...]
        m2 = ADAM_B1 * m_ref[...] + (1.0 - ADAM_B1) * g
        v2 = ADAM_B2 * v_ref[...] + (1.0 - ADAM_B2) * (g * g)
        m2_ref[...] = m2
        v2_ref[...] = v2
        d_ref[...] = -ADAM_LR * ((m2 / c1) / (jnp.sqrt(v2 / c2) + ADAM_EPS) + ADAM_WD * w_ref[...])

    tile = BS((tr, C), lambda i: (i, 0))
    return pl.pallas_call(
        body, name=name, grid=(R // tr,), out_shape=(SDS((R, C), F32),) * 3,
        in_specs=[tile] * 4, out_specs=(tile,) * 3, compiler_params=_params("parallel"),
    )(w, g, m, v)


def _rs_begin(g, c_idx, tag):
    n = g.shape[1]
    g = g.reshape(N_CHIPS, 2, n // 2, D)
    return _add_halves_call(g, _sibling_halves_call(g, tag), c_idx, tag)


def _rs_end(part, recv, c_idx, chip_idx, tag):
    n = 2 * part.shape[1]
    full = _add_chips_call(part, recv, jnp.concatenate([chip_idx, c_idx]), tag)
    return _sibling_join_call(full, tag).reshape(n, D)


def _local_step(x, mod, cact, target, wg, pack, small, c_idx, chip_idx):
    p, h1, wg = _fwd_in_call(x, mod, small["pre_tm"], wg, small["b_in"], pack)
    o, oa, st, wg = _hgrn_fwd_call(p, small["logits"], small["hg_norm"], wg, pack)
    u, uc, cb, wg = _conv_fwd_call(p, small["conv_dw"], small["conv_db"], small["ln_g"], small["ln_b"], wg, pack)
    ya, yb, mg, y, x2, h2 = _merge_fwd_call(oa, cb, p, x, mod, small["post_tm"], small["pre_cm"], wg)
    z, da, dy2, dx2, acc_f = _ffn_call(h2, x2, target, mod, small["post_cm"], small["pre_cm"], wg)

    g_ff = _wgrad_call(None, h2, da, "wgrad_ff1", D, lambda i, j: (j, 0), 2 * R_FF)
    g_ff = _wgrad_call(g_ff, z, dy2, "wgrad_ff2", D, lambda i, j: (i, 1), 2 * R_FF)
    part_ff = _rs_begin(g_ff, c_idx, "ff")
    dy, dya, dyb, doa, dcb, dp_gt, acc_m, bs_gt = _merge_bwd_call(dx2, y, ya, yb, p, mod, small["post_tm"], wg)
    dp_hg, bs_hg, dlg, dgn, recv_ff = _hgrn_bwd_call(p, o, doa, st, small["logits"], small["hg_norm"], part_ff)

    g_br = _wgrad_rows_call(None, oa, dya, "wgrad_br_a", 0)
    g_br = _wgrad_rows_call(g_br, cb, dyb, "wgrad_br_b", 1)
    g_br = _wgrad_rows_call(g_br, mg, dy, "wgrad_out", 2)
    part_br = _rs_begin(g_br, c_idx, "br")
    dp_cv, bs_cv, ddw, acc_c, recv_br = _conv_bwd_call(dcb, uc, u, p, small["conv_dw"], small["ln_g"], small["ln_b"],
                                                        part_br)

    g_in = _wgrad_call(None, h1, dp_hg, "wgrad_in_hg", D, lambda i, j: (j // 2, j % 2), R_IN)
    g_in = _wgrad_call(g_in, h1, dp_cv, "wgrad_in_cv", D, lambda i, j: (2, j), R_IN)
    g_in = _wgrad_call(g_in, h1, dp_gt, "wgrad_in_gt", D, lambda i, j: (3, j), R_IN)
    part_in = _rs_begin(g_in, c_idx, "in")
    gx, acc_i, recv_in = _in_bwd_call(dp_hg, dp_cv, dp_gt, x, dx2, mod, small["pre_tm"], wg, part_in)

    red_ff = _rs_end(part_ff, recv_ff, c_idx, chip_idx, "ff")
    red_br = _rs_end(part_br, recv_br, c_idx, chip_idx, "br")
    red_in = _rs_end(part_in, recv_in, c_idx, chip_idx, "in")

    zrow = jnp.zeros((1, D), F32)
    rows = [acc_i[0:1], acc_i[1:2], acc_m[0:1], acc_f[2:3], acc_f[3:4], acc_f[0:1],
            acc_i[2:3], acc_m[1:2], acc_f[4:5], acc_f[1:2],
            jnp.concatenate([bs_hg, bs_cv, bs_gt], axis=1).reshape(8, D),
            dlg, dgn, acc_c[0:1], acc_c[1:2], acc_c[2:3],
            ddw,
            cact, acc_f[5:6]] + [zrow] * 6
    return gx, jnp.concatenate(rows, axis=0), red_in, red_ff, red_br


def kernel(x, c, w_ada, b_ada, pre_norm_tm, post_norm_tm, pre_norm_cm, post_norm_cm, w_in, b_in, hg_lb_logits, hg_norm, conv_dw, conv_db, conv_ln_g, conv_ln_b, w_br_a, w_br_b, w_out, w_ff1, w_ff2, loss_target, m_w_ada, m_b_ada, m_pre_norm_tm, m_post_norm_tm, m_pre_norm_cm, m_post_norm_cm, m_w_in, m_b_in, m_hg_lb_logits, m_hg_norm, m_conv_dw, m_conv_db, m_conv_ln_g, m_conv_ln_b, m_w_br_a, m_w_br_b, m_w_out, m_w_ff1, m_w_ff2, v_w_ada, v_b_ada, v_pre_norm_tm, v_post_norm_tm, v_pre_norm_cm, v_post_norm_cm, v_w_in, v_b_in, v_hg_lb_logits, v_hg_norm, v_conv_dw, v_conv_db, v_conv_ln_g, v_conv_ln_b, v_w_br_a, v_w_br_b, v_w_out, v_w_ff1, v_w_ff2):
    xi, yi, ci = lax.axis_index("x"), lax.axis_index("y"), lax.axis_index("c")
    chip = 2 * xi + yi
    c_idx = jnp.reshape(ci, (1,)).astype(jnp.int32)
    chip_idx = jnp.reshape(chip, (1,)).astype(jnp.int32)

    def pack_small(ada_b, pre_t, post_t, pre_c, post_c, in_b, lg, hgn, cdb, lng, lnb, cdw):
        flat = jnp.concatenate([cdw[0].reshape(-1), jnp.zeros((8 * D - CONV_K * 256,), F32)]).reshape(8, D)
        return jnp.concatenate([ada_b.reshape(6, D), pre_t, post_t, pre_c, post_c, in_b.reshape(8, D), lg, hgn,
                                cdb, lng, lnb, flat], axis=0)

    w_in_halves = w_in[0].reshape(D, 2, D).transpose(1, 0, 2).reshape(R_IN, D)
    pack = jnp.concatenate([w_in_halves, w_ff1[0], w_ff2[0], w_br_a[0], w_br_b[0], w_out[0]],
                           axis=0).astype(BF16)
    wg = lax.dynamic_update_slice(lax.empty((N_CHIPS, PACK_W, D), BF16), pack[None], (chip, 0, 0))
    wa = 6 * D // N_CHIPS
    me = 4 * xi + 2 * yi + ci
    dw_blk = jnp.concatenate([conv_dw[0].reshape(-1), jnp.zeros((8 * D - CONV_K * 256,), F32)]).reshape(8, D)
    wg, dw_all, ca_all, mod_all = _prologue_call(
        pack, wg, dw_blk, jnp.broadcast_to(c, (8, D)), w_ada[0].astype(BF16),
        lax.dynamic_slice_in_dim(b_ada, chip * wa, wa, axis=1))
    dw_all = dw_all.reshape(N_CHIPS, 2, 8 * D)[:, 0, :CONV_K * 256].reshape(N_CHIPS, CONV_K, 256)
    dw_full = dw_all.transpose(1, 0, 2).reshape(CONV_K, D)
    cact = lax.dynamic_slice_in_dim(ca_all, me * 8, 1, axis=0)
    mod_mine = lax.dynamic_index_in_dim(mod_all.reshape(N_CHIPS, 2, N_DEV, 8, wa)[:, 0, :, 0, :], me, axis=1,
                                        keepdims=False)
    mod = mod_mine.reshape(1, 6 * D)

    small = dict(b_ada=b_ada, pre_tm=pre_norm_tm, post_tm=post_norm_tm, pre_cm=pre_norm_cm, post_cm=post_norm_cm,
                 b_in=b_in, logits=hg_lb_logits, hg_norm=hg_norm, conv_dw=dw_full, conv_db=conv_db,
                 ln_g=conv_ln_g, ln_b=conv_ln_b)

    gx, srows, red_in, red_ff, red_br = _local_step(x[0], mod, cact, loss_target[0], wg, pack, small, c_idx,
                                                    chip_idx)

    sall, ssum = _allgather_call(srows, "gather_small", in_vmem=True, with_sum=True)
    sall = sall.reshape(N_DEV, SMALL_ROWS, D)
    loss = jnp.sum(ssum[57])
    dmod_all = sall[:, 0:6, :].reshape(N_DEV, 6 * D)
    g_ada = _outer_call(sall[:, 56, :], lax.dynamic_slice_in_dim(dmod_all, chip * wa, wa, axis=1))
    g_dw = lax.dynamic_slice_in_dim(ssum[24:24 + CONV_K], chip * 256, 256, axis=1)
    g_small = jnp.concatenate(
        [ssum[0:24], jnp.concatenate([g_dw.reshape(-1), jnp.zeros((8 * D - CONV_K * 256,), F32)]).reshape(8, D)],
        axis=0)

    shapes = {"in": w_in.shape, "br_a": w_br_a.shape, "br_b": w_br_b.shape, "out": w_out.shape,
              "ff1": w_ff1.shape, "ff2": w_ff2.shape}
    offs = {"in": (red_in, 0, R_IN), "ff1": (red_ff, 0, R_FF), "ff2": (red_ff, R_FF, 2 * R_FF),
            "br_a": (red_br, 0, R_BR), "br_b": (red_br, R_BR, 2 * R_BR), "out": (red_br, 2 * R_BR, 3 * R_BR)}
    wmv = {"in": (w_in, m_w_in, v_w_in), "br_a": (w_br_a, m_w_br_a, v_w_br_a), "br_b": (w_br_b, m_w_br_b, v_w_br_b),
           "out": (w_out, m_w_out, v_w_out), "ff1": (w_ff1, m_w_ff1, v_w_ff1), "ff2": (w_ff2, m_w_ff2, v_w_ff2)}
    res = {}
    for n in offs:
        shp = shapes[n]
        g2d = offs[n][0][offs[n][1]:offs[n][2]]
        if n == "in":
            g2d = g2d.reshape(2, D, D).transpose(1, 0, 2)
        g2d = g2d.reshape(shp[1], shp[2])
        w_, m_, v_ = (a[0] for a in wmv[n])
        d_, m2_, v2_ = _adamw_call(w_, g2d, m_, v_, "adamw_" + n)
        res[n] = tuple(a.reshape(shp) for a in (g2d, d_, m2_, v2_))
    d_, m2_, v2_ = _adamw_call(w_ada[0], g_ada, m_w_ada[0], v_w_ada[0], "adamw_ada")
    res["ada"] = tuple(a.reshape(w_ada.shape) for a in (g_ada, d_, m2_, v2_))

    ws = pack_small(b_ada, pre_norm_tm, post_norm_tm, pre_norm_cm, post_norm_cm, b_in, hg_lb_logits, hg_norm,
                    conv_db, conv_ln_g, conv_ln_b, conv_dw)
    ms = pack_small(m_b_ada, m_pre_norm_tm, m_post_norm_tm, m_pre_norm_cm, m_post_norm_cm, m_b_in, m_hg_lb_logits,
                    m_hg_norm, m_conv_db, m_conv_ln_g, m_conv_ln_b, m_conv_dw)
    vs = pack_small(v_b_ada, v_pre_norm_tm, v_post_norm_tm, v_pre_norm_cm, v_post_norm_cm, v_b_in, v_hg_lb_logits,
                    v_hg_norm, v_conv_db, v_conv_ln_g, v_conv_ln_b, v_conv_dw)
    sres = (g_small,) + tuple(_adamw_call(ws, g_small, ms, vs, "adamw_small"))

    def unpack_small(t):
        return {"b_ada": t[0:6].reshape(1, 6 * D), "pre_tm": t[6:7], "post_tm": t[7:8], "pre_cm": t[8:9],
                "post_cm": t[9:10], "b_in": t[10:18].reshape(1, IN_COLS), "logits": t[18:20], "hg_norm": t[20:21],
                "conv_db": t[21:22], "ln_g": t[22:23], "ln_b": t[23:24],
                "conv_dw": t[24:32].reshape(-1)[:CONV_K * 256].reshape(1, CONV_K, 256)}

    order = ["ada", "b_ada", "pre_tm", "post_tm", "pre_cm", "post_cm", "in", "b_in", "logits", "hg_norm", "conv_dw",
             "conv_db", "ln_g", "ln_b", "br_a", "br_b", "out", "ff1", "ff2"]
    outs = [loss, gx.reshape(x.shape)]
    for kind in range(4):
        sm = unpack_small(sres[kind])
        for n in order:
            outs.append(res[n][kind] if n in res else sm[n])
    return tuple(outs)
```

```python
import functools

import jax
import jax.numpy as jnp
from jax import lax
from jax.experimental import pallas as pl
from jax.experimental.pallas import tpu as pltpu

F32, BF16 = jnp.float32, jnp.bfloat16
SDS = jax.ShapeDtypeStruct
BS = pl.BlockSpec
MESH = pl.DeviceIdType.MESH
HI = lax.Precision.HIGHEST

D = 1024
D_FF = 4096
IN_COLS = 8192
HEADS, DK = 8, 128
CHUNK = 128
CONV_K = 31
HALO = 32
SUB = 32
EPS = 1e-6
N_CHIPS, N_DEV = 4, 8
TM = 256
TB = 256
VMEM_LIMIT = 56 * 1024 * 1024

R_IN, R_BR, R_FF = 2048, 256, 1024
PACK_W = R_IN + 3 * R_BR + 2 * R_FF
O_IN, O_FF1, O_FF2, O_BRA, O_BRB, O_OUT = 0, 2048, 3072, 4096, 4352, 4608
SMALL_ROWS = 64

ADAM_LR, ADAM_B1, ADAM_B2, ADAM_EPS, ADAM_WD, ADAM_STEP = 0.001, 0.9, 0.999, 1e-08, 0.01, 10

NN = (((1,), (0,)), ((), ()))
NT = (((1,), (1,)), ((), ()))
TN = (((0,), (0,)), ((), ()))


def _mm(a, b, dims=NN, precision=None):
    return lax.dot_general(a, b, dims, preferred_element_type=F32, precision=precision)


def _sig(v):
    return jax.nn.sigmoid(v)


def _dsilu(v, s):
    return s * (1.0 + v * (1.0 - s))


def _params(*sem):
    return pltpu.CompilerParams(dimension_semantics=sem if sem else None, vmem_limit_bytes=VMEM_LIMIT)


def _rowsum(v):
    return jnp.sum(v, axis=0, keepdims=True)


def _mesh_pos():
    return lax.axis_index("x"), lax.axis_index("y"), lax.axis_index("c")


def _allgather(x_ref, out_ref, send_sems, recv_sems, local_sem):
    m_per = x_ref.shape[0]
    x, y, c = _mesh_pos()
    me, sibling = (x, y, c), (x, y, 1 - c)
    chips = [(1 - x, y), (x, 1 - y), (1 - x, 1 - y)]

    def rows(px, py, pc):
        return out_ref.at[pl.ds((4 * px + 2 * py + pc) * m_per, m_per), :]

    def copy(k, block, to, src=None):
        return pltpu.make_async_remote_copy(
            src_ref=rows(*block) if src is None else src, dst_ref=rows(*block),
            send_sem=send_sems.at[k], recv_sem=recv_sems.at[k], device_id=to, device_id_type=MESH)

    mine = pltpu.make_async_copy(x_ref, rows(*me), local_sem)
    mine.start()
    first = [copy(0, me, sibling, src=x_ref)]
    first += [copy(1 + j, me, (*chip, c), src=x_ref) for j, chip in enumerate(chips)]
    for cp in first:
        cp.start()
    passed = [copy(4 + j, (*chip, c), sibling) for j, chip in enumerate(chips)]
    for j, chip in enumerate(chips):
        copy(1 + j, (*chip, c), me).wait_recv()
        passed[j].start()
    copy(0, sibling, me).wait_recv()
    for j, chip in enumerate(chips):
        copy(4 + j, (*chip, 1 - c), me).wait_recv()
    for cp in first + passed:
        cp.wait_send()
    mine.wait()


def _allgather_sems():
    return [pltpu.SemaphoreType.DMA((7,)), pltpu.SemaphoreType.DMA((7,)), pltpu.SemaphoreType.DMA]


def _allgather_call(blk, name, in_vmem, with_sum):
    m_per, n = blk.shape

    def body(x_ref, out_ref, *rest):
        if with_sum:
            sum_ref, send_sems, recv_sems, local_sem = rest
        else:
            send_sems, recv_sems, local_sem = rest
        _allgather(x_ref, out_ref, send_sems, recv_sems, local_sem)
        if with_sum:
            acc = out_ref[0:m_per, :]
            for d in range(1, N_DEV):
                acc = acc + out_ref[d * m_per:(d + 1) * m_per, :]
            sum_ref[...] = acc

    space = pltpu.VMEM if in_vmem else pl.ANY
    out_shape = [SDS((N_DEV * m_per, n), blk.dtype)]
    out_specs = [BS(memory_space=space)]
    if with_sum:
        out_shape.append(SDS((m_per, n), blk.dtype))
        out_specs.append(BS(memory_space=pltpu.VMEM))
    return pl.pallas_call(
        body, name=name, out_shape=out_shape, in_specs=[BS(memory_space=space)], out_specs=out_specs,
        scratch_shapes=[pltpu.SemaphoreType.DMA((7,)), pltpu.SemaphoreType.DMA((7,)), pltpu.SemaphoreType.DMA],
        compiler_params=pltpu.CompilerParams(vmem_limit_bytes=VMEM_LIMIT),
    )(blk)


def _gather_sems(n_ranges):
    return [pltpu.SemaphoreType.DMA((6 * n_ranges,)), pltpu.SemaphoreType.DMA((6 * n_ranges,))]


def _pack_gather(pack_ref, wg_ref, send_sems, recv_sems, ranges):
    x, y, c = _mesh_pos()
    me, sibling = (x, y, c), (x, y, 1 - c)
    chips = [(1 - x, y), (x, 1 - y), (1 - x, 1 - y)]

    def land(r, px, py, pc):
        off, n = ranges[r]
        return wg_ref.at[2 * px + py, pl.ds(off + pc * (n // 2), n // 2), :]

    def mine(r):
        off, n = ranges[r]
        return pack_ref.at[pl.ds(off + c * (n // 2), n // 2), :]

    def copy(r, k, block, to, src=None):
        return pltpu.make_async_remote_copy(
            src_ref=land(r, *block) if src is None else src, dst_ref=land(r, *block),
            send_sem=send_sems.at[6 * r + k], recv_sem=recv_sems.at[6 * r + k], device_id=to, device_id_type=MESH)

    def start():
        for r in range(len(ranges)):
            for j, chip in enumerate(chips):
                copy(r, j, me, (*chip, c), src=mine(r)).start()

    def arrive(j):
        for r in range(len(ranges)):
            copy(r, j, (*chips[j], c), me).wait_recv()
            copy(r, 3 + j, (*chips[j], c), sibling).start()
        for r in range(len(ranges)):
            copy(r, 3 + j, (*chips[j], 1 - c), me).wait_recv()

    def drain():
        for r in range(len(ranges)):
            for j, chip in enumerate(chips):
                copy(r, j, me, (*chip, c), src=mine(r)).wait_send()
                copy(r, 3 + j, (*chip, c), sibling).wait_send()

    def finish():
        for j in range(3):
            arrive(j)
        drain()

    return start, finish, arrive, drain


def _prologue_call(dw_blk, c_blk, w_ada, b_ada):
    wa = w_ada.shape[1]

    def body(dw_ref, c_ref, wa_ref, ba_ref, dwg_ref, ca_ref, modg_ref,
             cg_scr, part_scr, s1, r1, l1, s2, r2, l2, s3, r3, l3):
        _allgather(c_ref, cg_scr, s2, r2, l2)
        cv = cg_scr[...]
        ca = cv * _sig(cv)
        ca_ref[...] = ca
        part_scr[...] = _mm(ca.astype(BF16), wa_ref[...]) + ba_ref[...]
        _allgather(part_scr, modg_ref, s3, r3, l3)
        _allgather(dw_ref, dwg_ref, s1, r1, l1)

    vm = BS(memory_space=pltpu.VMEM)
    return pl.pallas_call(
        body, name="prologue_adaln_conv_dw",
        out_shape=(SDS((N_DEV * 8, D), F32), SDS((N_DEV * 8, D), F32), SDS((N_DEV * N_DEV * 8, wa), F32)),
        in_specs=[vm, vm, vm, vm], out_specs=(vm, vm, vm),
        scratch_shapes=[pltpu.VMEM((N_DEV * 8, D), F32), pltpu.VMEM((N_DEV * 8, wa), F32)]
        + _allgather_sems() + _allgather_sems() + _allgather_sems(),
        compiler_params=pltpu.CompilerParams(vmem_limit_bytes=VMEM_LIMIT),
    )(dw_blk, c_blk, w_ada, b_ada)


def _sibling_halves_call(g, tag):
    _, _, h, n = g.shape

    def body(g_ref, out_ref, send_sems, recv_sems):
        x, y, c = _mesh_pos()
        cps = [pltpu.make_async_remote_copy(
            src_ref=g_ref.at[k, 1 - c], dst_ref=out_ref.at[k], send_sem=send_sems.at[k], recv_sem=recv_sems.at[k],
            device_id=(x, y, 1 - c), device_id_type=MESH) for k in range(N_CHIPS)]
        for cp in cps:
            cp.start()
        for cp in cps:
            cp.wait()

    return pl.pallas_call(
        body, name="rs_sibling_halves_" + tag, out_shape=SDS((N_CHIPS, h, n), g.dtype),
        in_specs=[BS(memory_space=pl.ANY)], out_specs=BS(memory_space=pl.ANY),
        scratch_shapes=[pltpu.SemaphoreType.DMA((N_CHIPS,)), pltpu.SemaphoreType.DMA((N_CHIPS,))],
    )(g)


def _chip_exchange(p_ref, out_ref, send_sems, recv_sems):
    x, y, c = _mesh_pos()
    chips = [(1 - x, y), (x, 1 - y), (1 - x, 1 - y)]

    def copies():
        return [pltpu.make_async_remote_copy(
            src_ref=p_ref.at[2 * cx + cy], dst_ref=out_ref.at[j], send_sem=send_sems.at[j], recv_sem=recv_sems.at[j],
            device_id=(cx, cy, c), device_id_type=MESH) for j, (cx, cy) in enumerate(chips)]

    def start():
        for cp in copies():
            cp.start()

    def finish():
        for cp in copies():
            cp.wait()

    return start, finish


def _exchange_sems():
    return [pltpu.SemaphoreType.DMA((3,)), pltpu.SemaphoreType.DMA((3,))]


def _sibling_join_call(full, tag):
    _, h, n = full.shape
    q = h // 4

    def body(in_ref, out_ref, send_sems, recv_sems):
        x, y, c = _mesh_pos()

        def copy(k, half):
            return pltpu.make_async_remote_copy(
                src_ref=in_ref.at[half, pl.ds(k * q, q)], dst_ref=out_ref.at[half, pl.ds(k * q, q)],
                send_sem=send_sems.at[k], recv_sem=recv_sems.at[k],
                device_id=(x, y, 1 - c), device_id_type=MESH)

        for k in range(4):
            copy(k, c).start()
        for k in range(4):
            copy(k, c).wait_send()
            copy(k, 1 - c).wait_recv()

    return pl.pallas_call(
        body, name="rs_sibling_join_" + tag, out_shape=SDS(full.shape, full.dtype),
        in_specs=[BS(memory_space=pl.ANY)], out_specs=BS(memory_space=pl.ANY),
        scratch_shapes=[pltpu.SemaphoreType.DMA((4,)), pltpu.SemaphoreType.DMA((4,))],
        input_output_aliases={0: 0},
    )(full)


def _add_halves_call(g, recv, c_idx, tag):
    _, _, h, n = g.shape
    tr = h // 2

    def body(c_ref, g_ref, r_ref, o_ref):
        o_ref[...] = (g_ref[...].astype(F32) + r_ref[...].astype(F32)).astype(BF16)

    return pl.pallas_call(
        body, name="rs_add_halves_" + tag, out_shape=SDS((N_CHIPS, h, n), BF16),
        grid_spec=pltpu.PrefetchScalarGridSpec(
            num_scalar_prefetch=1, grid=(N_CHIPS, 2),
            in_specs=[BS((None, None, tr, n), lambda k, r, c_ref: (k, c_ref[0], r, 0)),
                      BS((None, tr, n), lambda k, r, c_ref: (k, r, 0))],
            out_specs=BS((None, tr, n), lambda k, r, c_ref: (k, r, 0))),
        compiler_params=_params("arbitrary", "arbitrary"),
    )(c_idx, g, recv)


def _add_chips_call(p, recv, chip_c_idx, tag):
    _, h, n = p.shape
    tr = h // 2

    def body(k_ref, p_ref, r_ref, o_ref):
        acc = p_ref[...].astype(F32)
        for j in range(3):
            acc = acc + r_ref[j].astype(F32)
        o_ref[...] = acc

    return pl.pallas_call(
        body, name="rs_add_chips_" + tag, out_shape=SDS((2, h, n), F32),
        grid_spec=pltpu.PrefetchScalarGridSpec(
            num_scalar_prefetch=1, grid=(2,),
            in_specs=[BS((None, tr, n), lambda r, k_ref: (k_ref[0], r, 0)),
                      BS((3, tr, n), lambda r, k_ref: (0, r, 0))],
            out_specs=BS((None, tr, n), lambda r, k_ref: (k_ref[1], r, 0))),
        compiler_params=_params("arbitrary"),
    )(chip_c_idx, p, recv)


def _load_rows(wg_hbm, w_vmem, sem, off):
    cp = pltpu.make_async_copy(wg_hbm.at[:, pl.ds(off, w_vmem.shape[1]), :], w_vmem, sem)
    cp.start()
    return cp


def _fwd_in_call(x, mod, pre_tm, wg, b_in, pack, order):
    S = x.shape[0]
    tmf = 2 * TM
    nt = S // tmf
    wc = IN_COLS // N_CHIPS

    def body(ord_ref, x_ref, mod_ref, g_ref, w_hbm, b_ref, pack_ref, p_ref, h_hbm, wg_out, w_vmem, h_scr, sem,
             send_sems, recv_sems):
        q, i = pl.program_id(0), pl.program_id(1)
        rows = pl.ds(pl.multiple_of(i * tmf, tmf), tmf)
        start, _, arrive, drain = _pack_gather(pack_ref, wg_out, send_sems, recv_sems, [(O_IN, R_IN)])

        def load_weights():
            cp = pltpu.make_async_copy(wg_out.at[ord_ref[q], pl.ds(O_IN, R_IN), :], w_vmem, sem)
            cp.start()
            cp.wait()

        @pl.when((q == 0) & (i == 0))
        def _():
            start()
            load_weights()

        for j in range(3):
            @pl.when((q == j + 1) & (i == 0))
            def _(j=j):
                arrive(j)
                load_weights()

        @pl.when(q == 0)
        def _():
            xv = x_ref[...]
            r = lax.rsqrt(jnp.mean(xv * xv, axis=-1, keepdims=True) + EPS)
            h = xv * r * g_ref[...] * (1.0 + mod_ref[:, D:2 * D]) + mod_ref[:, 0:D]
            h_scr[rows, :] = h.astype(BF16)

        hb = h_scr[rows, :]
        for k in range(wc // D):
            p_ref[:, k * D:(k + 1) * D] = _mm(hb, w_vmem[k * D:(k + 1) * D, :]) + b_ref[:, k * D:(k + 1) * D]

        @pl.when((q == N_CHIPS - 1) & (i == nt - 1))
        def _():
            cp = pltpu.make_async_copy(h_scr, h_hbm, sem)
            cp.start()
            drain()
            cp.wait()

    hbm = BS(memory_space=pl.ANY)
    return pl.pallas_call(
        body, name="fwd_in", out_shape=(SDS((S, IN_COLS), F32), SDS((S, D), BF16), SDS(wg.shape, wg.dtype)),
        grid_spec=pltpu.PrefetchScalarGridSpec(
            num_scalar_prefetch=1, grid=(N_CHIPS, nt),
            in_specs=[BS((tmf, D), lambda q, i, o: (jnp.where(q == 0, i, nt - 1), 0)),
                      BS((1, 6 * D), lambda q, i, o: (0, 0)),
                      BS((1, D), lambda q, i, o: (0, 0)), hbm, BS((1, wc), lambda q, i, o: (0, o[q])), hbm],
            out_specs=(BS((tmf, wc), lambda q, i, o: (i, o[q])), hbm, hbm),
            scratch_shapes=[pltpu.VMEM((R_IN, D), BF16), pltpu.VMEM((S, D), BF16), pltpu.SemaphoreType.DMA]
            + _gather_sems(1)),
        input_output_aliases={4: 2},
        compiler_params=_params("arbitrary", "arbitrary"),
    )(order, x, mod, pre_tm, wg, b_in, pack)


def _lower_bound(lg_ref):
    l0, l1 = lg_ref[0:1, :], lg_ref[1:2, :]
    mx = jnp.maximum(l0, l1)
    e0, e1 = jnp.exp(l0 - mx), jnp.exp(l1 - mx)
    return e0 / (e0 + e1)


def _tri_masks():
    ri = lax.broadcasted_iota(jnp.int32, (CHUNK, CHUNK), 0)
    ci = lax.broadcasted_iota(jnp.int32, (CHUNK, CHUNK), 1)
    return (ri >= ci).astype(F32), (ci >= ri).astype(F32)


def _cumsum_mm(tri, g):
    tb = tri.astype(BF16)
    hi = g.astype(BF16)
    r1 = g - hi.astype(F32)
    mid = r1.astype(BF16)
    lo = (r1 - mid.astype(F32)).astype(BF16)
    return _mm(tb, hi) + _mm(tb, mid) + _mm(tb, lo)


def _hg_gates(q_r, f_r, lb, tril):
    sq = _sig(q_r)
    q = q_r * sq
    sf = _sig(f_r)
    f = lb + (1.0 - lb) * sf
    k = 1.0 - f
    g = jnp.log(f)
    b = _cumsum_mm(tril, g)
    b_last = _rowsum(g)
    row = lax.broadcasted_iota(jnp.int32, g.shape, 0)
    ref = _rowsum(jnp.where(row < CHUNK // 2, g, 0.0))
    e = jnp.exp(b)
    eq = jnp.exp(jnp.minimum(b - ref, 80.0))
    ek = jnp.exp(jnp.minimum(ref - b, 80.0))
    dd = jnp.exp(b_last - b)
    return dict(sq=sq, q=q, sf=sf, f=f, k=k, e=e, eq=eq, ek=ek, dd=dd, elast=jnp.exp(b_last),
                qe=q * e, qt=q * eq, kt=k * ek, kd=k * dd)


def _hgrn_fwd_call(p, logits, gn, wg, pack):
    S = p.shape[0]
    ncb = TB // CHUNK
    ranges = [(O_FF1, R_FF)]

    def body(q_ref, f_ref, v_ref, og_ref, lg_ref, gn_ref, wg_in, pack_ref, o_ref, oa_ref, st_ref, wg_out,
             st_scr, send_sems, recv_sems):
        start, finish, _, _ = _pack_gather(pack_ref, wg_out, send_sems, recv_sems, ranges)

        @pl.when(pl.program_id(0) == 0)
        def _():
            start()
            st_scr[...] = jnp.zeros_like(st_scr)

        lb = _lower_bound(lg_ref)
        tril, _ = _tri_masks()

        def chunk(ci, carry):
            rows = pl.ds(pl.multiple_of(ci * CHUNK, CHUNK), CHUNK)
            st_ref[ci] = st_scr[...]
            t = _hg_gates(q_ref[rows, :], f_ref[rows, :], lb, tril)
            v = v_ref[rows, :]
            for h in range(HEADS):
                sl = slice(h * DK, (h + 1) * DK)
                stp = st_scr[:, sl]
                vb = v[:, sl].astype(BF16)
                inter = _mm(t["qe"][:, sl].astype(BF16), stp.astype(BF16), NT)
                a = jnp.where(tril > 0.5, _mm(t["qt"][:, sl].astype(BF16), t["kt"][:, sl].astype(BF16), NT), 0.0)
                o = inter + _mm(a.astype(BF16), vb)
                st_scr[:, sl] = stp * t["elast"][:, sl] + _mm(vb, t["kd"][:, sl].astype(BF16), TN)
                oh = o * lax.rsqrt(jnp.mean(o * o, axis=-1, keepdims=True) + EPS)
                og = og_ref[rows, sl]
                o_ref[rows, sl] = o
                oa_ref[rows, sl] = (oh * gn_ref[:, sl] * (og * _sig(og))).astype(BF16)
            return carry

        lax.fori_loop(0, ncb, chunk, 0)

        @pl.when(pl.program_id(0) == S // TB - 1)
        def _():
            finish()

    col = lambda j: BS((TB, D), lambda i, j=j: (i, j))
    hbm = BS(memory_space=pl.ANY)
    return pl.pallas_call(
        body, name="hgrn_fwd", grid=(S // TB,),
        out_shape=(SDS((S, D), F32), SDS((S, D), BF16), SDS((S // CHUNK, DK, D), F32), SDS(wg.shape, wg.dtype)),
        in_specs=[col(0), col(1), col(2), col(3), BS((2, D), lambda i: (0, 0)), BS((1, D), lambda i: (0, 0)),
                  hbm, hbm],
        out_specs=(BS((TB, D), lambda i: (i, 0)), BS((TB, D), lambda i: (i, 0)),
                   BS((ncb, DK, D), lambda i: (i, 0, 0)), hbm),
        scratch_shapes=[pltpu.VMEM((DK, D), F32)] + _gather_sems(len(ranges)),
        input_output_aliases={6: 3},
        compiler_params=_params("arbitrary"),
    )(p, p, p, p, logits, gn, wg, pack)


def _layernorm_stats(uc):
    mu = jnp.mean(uc, axis=-1, keepdims=True)
    xc = uc - mu
    rs = lax.rsqrt(jnp.mean(xc * xc, axis=-1, keepdims=True) + EPS)
    return xc * rs, rs


EXT = HALO + TM + 8


def _fill_shifted(ext, shifted):
    for m in range(1, 8):
        shifted[m - 1] = ext[m:m + HALO + TM, :]


def _window(ext, shifted, s0, n):
    m = s0 % 8
    q = s0 - m
    return ext[q:q + n, :] if m == 0 else shifted[m - 1, q:q + n, :]


def _conv_fwd_call(p, dw, db, ln_g, ln_b, wg, pack):
    S = p.shape[0]
    ranges = [(O_FF2, R_FF), (O_BRA, 3 * R_BR)]

    def body(cv_ref, cg_ref, dw_ref, db_ref, g_ref, b_ref, wg_in, pack_ref, u_ref, uc_ref, cb_ref, wg_out,
             uext, ush, send_sems, recv_sems):
        start, finish, _, _ = _pack_gather(pack_ref, wg_out, send_sems, recv_sems, ranges)

        @pl.when(pl.program_id(0) == 0)
        def _():
            start()
            uext[0:HALO, :] = jnp.zeros((HALO, D), F32)
            uext[HALO + TM:EXT, :] = jnp.zeros((EXT - HALO - TM, D), F32)

        u = cv_ref[...] * _sig(cg_ref[...])
        uext[HALO:HALO + TM, :] = u
        u_ref[...] = u
        _fill_shifted(uext, ush)
        for rb in range(TM // SUB):
            acc = jnp.broadcast_to(db_ref[...], (SUB, D))
            for j in range(CONV_K):
                s0 = HALO - (CONV_K - 1) + j + rb * SUB
                acc = acc + dw_ref[j:j + 1, :] * _window(uext, ush, s0, SUB)
            uc_ref[rb * SUB:(rb + 1) * SUB, :] = acc
            xh, _ = _layernorm_stats(acc)
            ln = xh * g_ref[...] + b_ref[...]
            cb_ref[rb * SUB:(rb + 1) * SUB, :] = (ln * _sig(ln)).astype(BF16)
        uext[0:HALO, :] = uext[TM:TM + HALO, :]

        @pl.when(pl.program_id(0) == S // TM - 1)
        def _():
            finish()

    vec = BS((1, D), lambda i: (0, 0))
    hbm = BS(memory_space=pl.ANY)
    return pl.pallas_call(
        body, name="conv_fwd", grid=(S // TM,),
        out_shape=(SDS((S, D), F32), SDS((S, D), F32), SDS((S, D), BF16), SDS(wg.shape, wg.dtype)),
        in_specs=[BS((TM, D), lambda i: (i, 4)), BS((TM, D), lambda i: (i, 5)),
                  BS((CONV_K, D), lambda i: (0, 0)), vec, vec, vec, hbm, hbm],
        out_specs=(BS((TM, D), lambda i: (i, 0)),) * 3 + (hbm,),
        scratch_shapes=[pltpu.VMEM((EXT, D), F32), pltpu.VMEM((7, HALO + TM, D), F32)] + _gather_sems(len(ranges)),
        input_output_aliases={6: 3},
        compiler_params=_params("arbitrary"),
    )(p, p, dw, db, ln_g, ln_b, wg, pack)


def _mm_rows(a, w_ref):
    acc = _mm(a[:, 0:R_BR], w_ref[0])
    for k in range(1, N_CHIPS):
        acc = acc + _mm(a[:, k * R_BR:(k + 1) * R_BR], w_ref[k])
    return acc


def _mm_rows_t(a, w_ref):
    return jnp.concatenate([_mm(a, w_ref[k], NT) for k in range(N_CHIPS)], axis=1)


def _br_spec(off):
    return BS((N_CHIPS, R_BR, D), lambda i: (0, off // R_BR, 0))


def _merge_fwd_call(oa, cb, p, x, mod, post_tm, pre_cm, wg):
    S = x.shape[0]

    def body(oa_ref, cb_ref, ga_ref, gb_ref, x_ref, mod_ref, post_ref, pre_ref, wa_ref, wb_ref, wo_ref,
             ya_ref, yb_ref, mg_ref, y_ref, x2_ref, h2_ref):
        ya = _mm_rows(oa_ref[...], wa_ref)
        yb = _mm_rows(cb_ref[...], wb_ref)
        ya_ref[...] = ya
        yb_ref[...] = yb
        mg = (_sig(ga_ref[...]) * ya + _sig(gb_ref[...]) * yb).astype(BF16)
        mg_ref[...] = mg
        y = _mm_rows(mg, wo_ref)
        y_ref[...] = y
        n = y * lax.rsqrt(jnp.mean(y * y, axis=-1, keepdims=True) + EPS) * post_ref[...]
        x2 = x_ref[...] + mod_ref[:, 2 * D:3 * D] * n
        x2_ref[...] = x2
        r2 = lax.rsqrt(jnp.mean(x2 * x2, axis=-1, keepdims=True) + EPS)
        h2 = x2 * r2 * pre_ref[...] * (1.0 + mod_ref[:, 4 * D:5 * D]) + mod_ref[:, 3 * D:4 * D]
        h2_ref[...] = h2.astype(BF16)

    tile = BS((TM, D), lambda i: (i, 0))
    vec = BS((1, D), lambda i: (0, 0))
    return pl.pallas_call(
        body, name="merge_fwd", grid=(S // TM,),
        out_shape=(SDS((S, D), F32), SDS((S, D), F32), SDS((S, D), BF16), SDS((S, D), F32), SDS((S, D), F32),
                   SDS((S, D), BF16)),
        in_specs=[tile, tile, BS((TM, D), lambda i: (i, 6)), BS((TM, D), lambda i: (i, 7)), tile,
                  BS((1, 6 * D), lambda i: (0, 0)), vec, vec, _br_spec(O_BRA), _br_spec(O_BRB), _br_spec(O_OUT)],
        out_specs=(tile,) * 6,
        compiler_params=_params("arbitrary"),
    )(oa, cb, p, p, x, mod, post_tm, pre_cm, wg, wg, wg)


def _ffn_call(h2, x2, target, mod, post_cm, pre_cm, wg):
    S = x2.shape[0]

    def body(h2_ref, x2_ref, t_ref, mod_ref, post_ref, pre_ref, w_hbm,
             z_ref, da_ref, dy2_ref, dx2_ref, acc_ref, w1_v, w2_v, ra_scr, sems):
        @pl.when(pl.program_id(0) == 0)
        def _():
            c1 = _load_rows(w_hbm, w1_v, sems.at[0], O_FF1)
            c2 = _load_rows(w_hbm, w2_v, sems.at[1], O_FF2)
            c1.wait()
            c2.wait()
            acc_ref[...] = jnp.zeros_like(acc_ref)

        h2 = h2_ref[...]
        for k in range(N_CHIPS):
            ra = jnp.maximum(_mm(h2, w1_v[k]), 0.0)
            ra_scr[:, k * D:(k + 1) * D] = ra
            z_ref[:, k * D:(k + 1) * D] = (ra * ra).astype(BF16)
        y2 = _mm(z_ref[:, 0:D], w2_v[0])
        for k in range(1, N_CHIPS):
            y2 = y2 + _mm(z_ref[:, k * D:(k + 1) * D], w2_v[k])
        ry = lax.rsqrt(jnp.mean(y2 * y2, axis=-1, keepdims=True) + EPS)
        yn = y2 * ry
        n = yn * post_ref[...]
        g2 = mod_ref[:, 5 * D:6 * D]
        x2 = x2_ref[...]
        err = x2 + g2 * n - t_ref[...]
        acc_ref[5:6, :] += _rowsum(err * err) * (0.5 / D)
        dout = err * (1.0 / D)
        acc_ref[0:1, :] += _rowsum(dout * n)
        dn = dout * g2
        acc_ref[1:2, :] += _rowsum(dn * yn)
        dyn = dn * post_ref[...]
        dy2 = (ry * (dyn - yn * jnp.mean(dyn * yn, axis=-1, keepdims=True))).astype(BF16)
        dy2_ref[...] = dy2
        for k in range(N_CHIPS):
            dz = _mm(dy2, w2_v[k], NT)
            da_ref[:, k * D:(k + 1) * D] = (dz * (2.0 * ra_scr[:, k * D:(k + 1) * D])).astype(BF16)
        dh2 = jnp.zeros((TM, D), F32)
        for k in range(N_CHIPS):
            dh2 = dh2 + _mm(da_ref[:, k * D:(k + 1) * D], w1_v[k], NT)
        r2 = lax.rsqrt(jnp.mean(x2 * x2, axis=-1, keepdims=True) + EPS)
        xn = x2 * r2
        yv = xn * pre_ref[...]
        acc_ref[2:3, :] += _rowsum(dh2)
        acc_ref[3:4, :] += _rowsum(dh2 * yv)
        dyv = dh2 * (1.0 + mod_ref[:, 4 * D:5 * D])
        acc_ref[4:5, :] += _rowsum(dyv * xn)
        dxn = dyv * pre_ref[...]
        dx2_ref[...] = dout + r2 * (dxn - xn * jnp.mean(dxn * xn, axis=-1, keepdims=True))

    tile = BS((TM, D), lambda i: (i, 0))
    wide = BS((TM, D_FF), lambda i: (i, 0))
    vec = BS((1, D), lambda i: (0, 0))
    return pl.pallas_call(
        body, name="ffn_fwd_bwd", grid=(S // TM,),
        out_shape=(SDS((S, D_FF), BF16), SDS((S, D_FF), BF16), SDS((S, D), BF16), SDS((S, D), F32),
                   SDS((8, D), F32)),
        in_specs=[tile, tile, tile, BS((1, 6 * D), lambda i: (0, 0)), vec, vec, BS(memory_space=pl.ANY)],
        out_specs=(wide, wide, tile, tile, BS((8, D), lambda i: (0, 0))),
        scratch_shapes=[pltpu.VMEM((N_CHIPS, R_FF, D), BF16), pltpu.VMEM((N_CHIPS, R_FF, D), BF16),
                        pltpu.VMEM((TM, D_FF), F32),
                        pltpu.SemaphoreType.DMA((2,))],
        compiler_params=_params("arbitrary"),
    )(h2, x2, target, mod, post_cm, pre_cm, wg)


def _merge_bwd_call(dx2, y, ya, yb, p, mod, post_tm, wg):
    S = y.shape[0]

    def body(dx2_ref, y_ref, ya_ref, yb_ref, ga_ref, gb_ref, mod_ref, post_ref, wa_ref, wb_ref, wo_ref,
             dy_ref, dya_ref, dyb_ref, doa_ref, dcb_ref, dpg_ref, acc_ref, bsum_ref):
        @pl.when(pl.program_id(0) == 0)
        def _():
            acc_ref[...] = jnp.zeros_like(acc_ref)
            bsum_ref[...] = jnp.zeros_like(bsum_ref)

        y = y_ref[...]
        ry = lax.rsqrt(jnp.mean(y * y, axis=-1, keepdims=True) + EPS)
        yn = y * ry
        dx2 = dx2_ref[...]
        acc_ref[0:1, :] += _rowsum(dx2 * (yn * post_ref[...]))
        dn = dx2 * mod_ref[:, 2 * D:3 * D]
        acc_ref[1:2, :] += _rowsum(dn * yn)
        dyn = dn * post_ref[...]
        dy = (ry * (dyn - yn * jnp.mean(dyn * yn, axis=-1, keepdims=True))).astype(BF16)
        dy_ref[...] = dy
        dmg = _mm_rows_t(dy, wo_ref)
        sa, sb = _sig(ga_ref[...]), _sig(gb_ref[...])
        dya = (dmg * sa).astype(BF16)
        dyb = (dmg * sb).astype(BF16)
        dya_ref[...] = dya
        dyb_ref[...] = dyb
        dga = dmg * ya_ref[...] * (sa * (1.0 - sa))
        dgb = dmg * yb_ref[...] * (sb * (1.0 - sb))
        dpg_ref[:, 0:D] = dga.astype(BF16)
        dpg_ref[:, D:2 * D] = dgb.astype(BF16)
        bsum_ref[:, 0:D] += _rowsum(dga)
        bsum_ref[:, D:2 * D] += _rowsum(dgb)
        doa_ref[...] = _mm_rows_t(dya, wa_ref)
        dcb_ref[...] = _mm_rows_t(dyb, wb_ref)

    tile = BS((TM, D), lambda i: (i, 0))
    vec = BS((1, D), lambda i: (0, 0))
    return pl.pallas_call(
        body, name="merge_bwd", grid=(S // TM,),
        out_shape=(SDS((S, D), BF16), SDS((S, D), BF16), SDS((S, D), BF16), SDS((S, D), F32), SDS((S, D), F32),
                   SDS((S, 2 * D), BF16), SDS((8, D), F32), SDS((1, 2 * D), F32)),
        in_specs=[tile, tile, tile, tile, BS((TM, D), lambda i: (i, 6)), BS((TM, D), lambda i: (i, 7)),
                  BS((1, 6 * D), lambda i: (0, 0)), vec, _br_spec(O_BRA), _br_spec(O_BRB), _br_spec(O_OUT)],
        out_specs=(tile, tile, tile, tile, tile, BS((TM, 2 * D), lambda i: (i, 0)),
                   BS((8, D), lambda i: (0, 0)), BS((1, 2 * D), lambda i: (0, 0))),
        compiler_params=_params("arbitrary"),
    )(dx2, y, ya, yb, p, p, mod, post_tm, wg, wg, wg)


def _hgrn_bwd_call(p, o, doa, st, logits, gn, part):
    S = p.shape[0]
    nb = S // TB
    ncb = TB // CHUNK

    def body(q_ref, f_ref, v_ref, og_ref, o_ref, doa_ref, st_ref, lg_ref, gn_ref, part_ref,
             dp_ref, bsum_ref, dlg_ref, dgn_ref, recv_ref,
             dst_scr, dlb_scr, dqe_s, dqt_s, dkt_s, dkd_s, dv_s, dog_s, dble_s, send_sems, recv_sems):
        i = pl.program_id(0)
        start, finish = _chip_exchange(part_ref, recv_ref, send_sems, recv_sems)

        @pl.when(i == 0)
        def _():
            start()
            dst_scr[...] = jnp.zeros_like(dst_scr)
            dlb_scr[...] = jnp.zeros_like(dlb_scr)
            bsum_ref[...] = jnp.zeros_like(bsum_ref)
            dgn_ref[...] = jnp.zeros_like(dgn_ref)

        lb = _lower_bound(lg_ref)
        tril, triu = _tri_masks()

        def chunk(tt, carry):
            ci = ncb - 1 - tt
            rows = pl.ds(pl.multiple_of(ci * CHUNK, CHUNK), CHUNK)
            q_r, f_r = q_ref[rows, :], f_ref[rows, :]
            t = _hg_gates(q_r, f_r, lb, tril)
            v = v_ref[rows, :]
            for h in range(HEADS):
                sl = slice(h * DK, (h + 1) * DK)
                stp = st_ref[ci, :, sl]
                stb = stp.astype(BF16)
                qeb = t["qe"][:, sl].astype(BF16)
                qtb = t["qt"][:, sl].astype(BF16)
                ktb = t["kt"][:, sl].astype(BF16)
                kdb = t["kd"][:, sl].astype(BF16)
                vb = v[:, sl].astype(BF16)
                a = jnp.where(tril > 0.5, _mm(qtb, ktb, NT), 0.0)
                o_h = o_ref[rows, sl]
                rinv = lax.rsqrt(jnp.mean(o_h * o_h, axis=-1, keepdims=True) + EPS)
                oh = o_h * rinv
                og = og_ref[rows, sl]
                so = _sig(og)
                d_oa = doa_ref[rows, sl]
                don = d_oa * (og * so)
                dog_s[:, sl] = d_oa * (oh * gn_ref[:, sl]) * _dsilu(og, so)
                dgn_ref[:, sl] += _rowsum(don * oh)
                doh = don * gn_ref[:, sl]
                do = (rinv * (doh - oh * jnp.mean(doh * oh, axis=-1, keepdims=True))).astype(BF16)
                dqe_s[:, sl] = _mm(do, stb, NN)
                dstp = _mm(do, qeb, TN)
                dab = jnp.where(tril > 0.5, _mm(do, vb, NT), 0.0).astype(BF16)
                dqt_s[:, sl] = _mm(dab, ktb, NN)
                dkt_s[:, sl] = _mm(dab, qtb, TN)
                dstn = dst_scr[:, sl]
                dsb = dstn.astype(BF16)
                dkd_s[:, sl] = _mm(vb, dsb, NN)
                dv_s[:, sl] = _mm(a.astype(BF16), do, TN) + _mm(kdb, dsb, NT)
                el = t["elast"][:, sl]
                dst_scr[:, sl] = dstn * el + dstp
                dble_s[:, sl] = el * _rowsum(stp * dstn)
            dqe, dqt, dkt, dkd = dqe_s[...], dqt_s[...], dkt_s[...], dkd_s[...]
            dq = dqe * t["e"] + dqt * t["eq"]
            dk = dkt * t["ek"] + dkd * t["dd"]
            dkk = dkd * t["kd"]
            qt_r = t["qt"].astype(BF16).astype(F32)
            kt_r = t["kt"].astype(BF16).astype(F32)
            dbv = dqe * t["qe"] + dqt * qt_r - dkt * kt_r - dkk
            dg = _cumsum_mm(triu, dbv) + (_rowsum(dkk) + dble_s[...])
            df = dg / t["f"] - dk
            sf = t["sf"]
            dlb_scr[...] += _rowsum(df * (1.0 - sf))
            dqr = dq * _dsilu(q_r, t["sq"])
            dfr = df * (1.0 - lb) * (sf * (1.0 - sf))
            dvv, dog = dv_s[...], dog_s[...]
            dp_ref[rows, 0:D] = dqr.astype(BF16)
            dp_ref[rows, D:2 * D] = dfr.astype(BF16)
            dp_ref[rows, 2 * D:3 * D] = dvv.astype(BF16)
            dp_ref[rows, 3 * D:4 * D] = dog.astype(BF16)
            bsum_ref[:, 0:D] += _rowsum(dqr)
            bsum_ref[:, D:2 * D] += _rowsum(dfr)
            bsum_ref[:, 2 * D:3 * D] += _rowsum(dvv)
            bsum_ref[:, 3 * D:4 * D] += _rowsum(dog)
            return carry

        lax.fori_loop(0, ncb, chunk, 0)

        dl = dlb_scr[...] * lb * (1.0 - lb)
        dlg_ref[0:1, :] = dl
        dlg_ref[1:2, :] = -dl

        @pl.when(i == nb - 1)
        def _():
            finish()

    col = lambda j: BS((TB, D), lambda i, j=j: (nb - 1 - i, j))
    rev = BS((TB, D), lambda i: (nb - 1 - i, 0))
    cd = pltpu.VMEM((CHUNK, D), F32)
    return pl.pallas_call(
        body, name="hgrn_bwd", grid=(nb,),
        out_shape=(SDS((S, 4 * D), BF16), SDS((1, 4 * D), F32), SDS((2, D), F32), SDS((1, D), F32),
                   SDS((3,) + part.shape[1:], part.dtype)),
        in_specs=[col(0), col(1), col(2), col(3), rev, rev, BS((ncb, DK, D), lambda i: (nb - 1 - i, 0, 0)),
                  BS((2, D), lambda i: (0, 0)), BS((1, D), lambda i: (0, 0)), BS(memory_space=pl.ANY)],
        out_specs=(BS((TB, 4 * D), lambda i: (nb - 1 - i, 0)), BS((1, 4 * D), lambda i: (0, 0)),
                   BS((2, D), lambda i: (0, 0)), BS((1, D), lambda i: (0, 0)), BS(memory_space=pl.ANY)),
        scratch_shapes=[pltpu.VMEM((DK, D), F32), pltpu.VMEM((1, D), F32), cd, cd, cd, cd, cd, cd,
                        pltpu.VMEM((1, D), F32)] + _exchange_sems(),
        compiler_params=_params("arbitrary"),
    )(p, p, p, p, o, doa, st, logits, gn, part)


def _conv_bwd_call(dcb, uc, u, p, dw, ln_g, ln_b, part):
    S = uc.shape[0]
    nb = S // TM
    hb = TM // HALO

    def body(dcb_ref, uc_ref, u_ref, uh_ref, cv_ref, cg_ref, dw_ref, g_ref, b_ref, part_ref,
             dp_ref, bsum_ref, ddw_ref, acc_ref, recv_ref, uext, dext, ush, dsh, send_sems, recv_sems):
        i = pl.program_id(0)
        start, finish = _chip_exchange(part_ref, recv_ref, send_sems, recv_sems)

        @pl.when(i == 0)
        def _():
            start()
            dext[TM:EXT, :] = jnp.zeros((EXT - TM, D), F32)
            uext[HALO + TM:EXT, :] = jnp.zeros((EXT - HALO - TM, D), F32)
            bsum_ref[...] = jnp.zeros_like(bsum_ref)
            ddw_ref[...] = jnp.zeros_like(ddw_ref)
            acc_ref[...] = jnp.zeros_like(acc_ref)

        first_tile = (nb - 1 - i) == 0
        uext[0:HALO, :] = jnp.where(first_tile, 0.0, uh_ref[...])
        uext[HALO:HALO + TM, :] = u_ref[...]
        _fill_shifted(uext, ush)

        for rb in range(TM // SUB):
            rs_ = slice(rb * SUB, (rb + 1) * SUB)
            xh, rs = _layernorm_stats(uc_ref[rs_, :])
            ln = xh * g_ref[...] + b_ref[...]
            dln = dcb_ref[rs_, :] * _dsilu(ln, _sig(ln))
            acc_ref[1:2, :] += _rowsum(dln * xh)
            acc_ref[2:3, :] += _rowsum(dln)
            dxh = dln * g_ref[...]
            duc = rs * (dxh - jnp.mean(dxh, axis=-1, keepdims=True)
                        - xh * jnp.mean(dxh * xh, axis=-1, keepdims=True))
            dext[rs_, :] = duc
            acc_ref[0:1, :] += _rowsum(duc)
        _fill_shifted(dext, dsh)

        for j in range(CONV_K):
            part = jnp.zeros((SUB, D), F32)
            for rb in range(TM // SUB):
                s0 = HALO - (CONV_K - 1) + j + rb * SUB
                part = part + dext[rb * SUB:(rb + 1) * SUB, :] * _window(uext, ush, s0, SUB)
            ddw_ref[j:j + 1, :] += _rowsum(part)

        for rb in range(TM // SUB):
            rs_ = slice(rb * SUB, (rb + 1) * SUB)
            du = jnp.zeros((SUB, D), F32)
            for j in range(CONV_K):
                s0 = rb * SUB + (CONV_K - 1) - j
                du = du + dw_ref[j:j + 1, :] * _window(dext, dsh, s0, SUB)
            cg = cg_ref[rs_, :]
            sg = _sig(cg)
            dcv = du * sg
            dcg = du * cv_ref[rs_, :] * (sg * (1.0 - sg))
            dp_ref[rs_, 0:D] = dcv.astype(BF16)
            dp_ref[rs_, D:2 * D] = dcg.astype(BF16)
            bsum_ref[:, 0:D] += _rowsum(dcv)
            bsum_ref[:, D:2 * D] += _rowsum(dcg)

        dext[TM:TM + HALO, :] = dext[0:HALO, :]

        @pl.when(i == nb - 1)
        def _():
            finish()

    rev = BS((TM, D), lambda i: (nb - 1 - i, 0))
    vec = BS((1, D), lambda i: (0, 0))
    return pl.pallas_call(
        body, name="conv_bwd", grid=(nb,),
        out_shape=(SDS((S, 2 * D), BF16), SDS((1, 2 * D), F32), SDS((32, D), F32), SDS((8, D), F32),
                   SDS((3,) + part.shape[1:], part.dtype)),
        in_specs=[rev, rev, rev, BS((HALO, D), lambda i: (jnp.maximum((nb - 1 - i) * hb - 1, 0), 0)),
                  BS((TM, D), lambda i: (nb - 1 - i, 4)), BS((TM, D), lambda i: (nb - 1 - i, 5)),
                  BS((CONV_K, D), lambda i: (0, 0)), vec, vec, BS(memory_space=pl.ANY)],
        out_specs=(BS((TM, 2 * D), lambda i: (nb - 1 - i, 0)), BS((1, 2 * D), lambda i: (0, 0)),
                   BS((32, D), lambda i: (0, 0)), BS((8, D), lambda i: (0, 0)), BS(memory_space=pl.ANY)),
        scratch_shapes=[pltpu.VMEM((EXT, D), F32), pltpu.VMEM((EXT, D), F32),
                        pltpu.VMEM((7, HALO + TM, D), F32), pltpu.VMEM((7, HALO + TM, D), F32)] + _exchange_sems(),
        compiler_params=_params("arbitrary"),
    )(dcb, uc, u, u, p, p, dw, ln_g, ln_b, part)


def _in_bwd_call(dp_hg, dp_cv, dp_gt, x, dx2, mod, pre_tm, wg, part):
    S = x.shape[0]

    def body(hg_ref, cv_ref, gt_ref, x_ref, dx2_ref, mod_ref, g_ref, w_hbm, part_ref, gx_ref, acc_ref, recv_ref,
             w_vmem, sem, send_sems, recv_sems):
        start, finish = _chip_exchange(part_ref, recv_ref, send_sems, recv_sems)

        @pl.when(pl.program_id(0) == 0)
        def _():
            start()
            _load_rows(w_hbm, w_vmem, sem, O_IN).wait()
            acc_ref[...] = jnp.zeros_like(acc_ref)

        dh = jnp.zeros((TM, D), F32)
        for k in range(IN_COLS // D):
            src, kk = ((hg_ref, k), (cv_ref, k - 4), (gt_ref, k - 6))[0 if k < 4 else (1 if k < 6 else 2)]
            dh = dh + _mm(src[:, kk * D:(kk + 1) * D], w_vmem[k // 2, (k % 2) * D:(k % 2 + 1) * D, :], NT)
        xv = x_ref[...]
        r = lax.rsqrt(jnp.mean(xv * xv, axis=-1, keepdims=True) + EPS)
        xn = xv * r
        yv = xn * g_ref[...]
        acc_ref[0:1, :] += _rowsum(dh)
        acc_ref[1:2, :] += _rowsum(dh * yv)
        dyv = dh * (1.0 + mod_ref[:, D:2 * D])
        acc_ref[2:3, :] += _rowsum(dyv * xn)
        dxn = dyv * g_ref[...]
        gx_ref[...] = dx2_ref[...] + r * (dxn - xn * jnp.mean(dxn * xn, axis=-1, keepdims=True))

        @pl.when(pl.program_id(0) == S // TM - 1)
        def _():
            finish()

    tile = BS((TM, D), lambda i: (i, 0))
    return pl.pallas_call(
        body, name="in_bwd", grid=(S // TM,),
        out_shape=(SDS((S, D), F32), SDS((8, D), F32), SDS((3,) + part.shape[1:], part.dtype)),
        in_specs=[BS((TM, 4 * D), lambda i: (i, 0)), BS((TM, 2 * D), lambda i: (i, 0)),
                  BS((TM, 2 * D), lambda i: (i, 0)), tile, tile, BS((1, 6 * D), lambda i: (0, 0)),
                  BS((1, D), lambda i: (0, 0)), BS(memory_space=pl.ANY), BS(memory_space=pl.ANY)],
        out_specs=(tile, BS((8, D), lambda i: (0, 0)), BS(memory_space=pl.ANY)),
        scratch_shapes=[pltpu.VMEM((N_CHIPS, R_IN, D), BF16), pltpu.SemaphoreType.DMA] + _exchange_sems(),
        compiler_params=_params("arbitrary"),
    )(dp_hg, dp_cv, dp_gt, x, dx2, mod, pre_tm, wg, part)


def _wgrad_call(gp, a, b, name, bm, place, rows):
    S, M = a.shape
    N = b.shape[1]
    bk = min(S, 1024)
    nk = S // bk

    def body(a_ref, b_ref, *rest):
        o_ref, acc = rest[-2], rest[-1]
        k = pl.program_id(2)

        @pl.when(k == 0)
        def _():
            acc[...] = jnp.zeros_like(acc)

        acc[...] += _mm(a_ref[...], b_ref[...], TN)

        @pl.when(k == nk - 1)
        def _():
            o_ref[...] = acc[...].astype(BF16)

    in_specs = [BS((bk, bm), lambda i, j, k: (k, i)), BS((bk, D), lambda i, j, k: (k, j))]
    args = [a, b]
    if gp is not None:
        in_specs.append(BS(memory_space=pl.ANY))
        args.append(gp)
    return pl.pallas_call(
        body, name=name, grid=(M // bm, N // D, nk),
        out_shape=SDS((N_CHIPS, rows, D), BF16),
        in_specs=in_specs,
        out_specs=BS((None, bm, D), lambda i, j, k: (*place(i, j), 0)),
        scratch_shapes=[pltpu.VMEM((bm, D), F32)],
        input_output_aliases={} if gp is None else {2: 0},
        compiler_params=_params("parallel", "parallel", "arbitrary"),
    )(*args)


def _wgrad_rows_call(gp, a, b, name, blk):
    S = a.shape[0]
    bk = min(S, 1024)
    nk = S // bk

    def body(a_ref, b_ref, *rest):
        o_ref, acc = rest[-2], rest[-1]
        k = pl.program_id(0)

        @pl.when(k == 0)
        def _():
            acc[...] = jnp.zeros_like(acc)

        acc[...] += _mm(a_ref[...], b_ref[...], TN)

        @pl.when(k == nk - 1)
        def _():
            for c in range(N_CHIPS):
                o_ref[c] = acc[c * R_BR:(c + 1) * R_BR, :].astype(BF16)

    in_specs = [BS((bk, D), lambda k: (k, 0)), BS((bk, D), lambda k: (k, 0))]
    args = [a, b]
    if gp is not None:
        in_specs.append(BS(memory_space=pl.ANY))
        args.append(gp)
    return pl.pallas_call(
        body, name=name, grid=(nk,),
        out_shape=SDS((N_CHIPS, 3 * R_BR, D), BF16),
        in_specs=in_specs,
        out_specs=BS((N_CHIPS, R_BR, D), lambda k: (0, blk, 0)),
        scratch_shapes=[pltpu.VMEM((D, D), F32)],
        input_output_aliases={} if gp is None else {2: 0},
        compiler_params=_params("arbitrary"),
    )(*args)


def _outer_call(cact, dmod):
    n = dmod.shape[1]

    def body(a_ref, b_ref, o_ref):
        o_ref[...] = _mm(a_ref[...], b_ref[...], TN, HI)

    return pl.pallas_call(
        body, name="wgrad_ada", out_shape=SDS((D, n), F32),
        compiler_params=pltpu.CompilerParams(vmem_limit_bytes=VMEM_LIMIT),
    )(cact, dmod)


def _adamw_call(w, g, m, v, name):
    R, C = w.shape
    tr = R
    while tr * C > 512 * 1024 and tr % 16 == 0:
        tr //= 2
    c1 = 1.0 - ADAM_B1 ** ADAM_STEP
    c2 = 1.0 - ADAM_B2 ** ADAM_STEP

    def body(w_ref, g_ref, m_ref, v_ref, d_ref, m2_ref, v2_ref):
        g = g_ref[...]
        m2 = ADAM_B1 * m_ref[...] + (1.0 - ADAM_B1) * g
        v2 = ADAM_B2 * v_ref[...] + (1.0 - ADAM_B2) * (g * g)
        m2_ref[...] = m2
        v2_ref[...] = v2
        d_ref[...] = -ADAM_LR * ((m2 / c1) / (jnp.sqrt(v2 / c2) + ADAM_EPS) + ADAM_WD * w_ref[...])

    tile = BS((tr, C), lambda i: (i, 0))
    return pl.pallas_call(
        body, name=name, grid=(R // tr,), out_shape=(SDS((R, C), F32),) * 3,
        in_specs=[tile] * 4, out_specs=(tile,) * 3, compiler_params=_params("parallel"),
    )(w, g, m, v)


def _rs_begin(g, c_idx, tag):
    n = g.shape[1]
    g = g.reshape(N_CHIPS, 2, n // 2, D)
    return _add_halves_call(g, _sibling_halves_call(g, tag), c_idx, tag)


def _rs_end(part, recv, c_idx, chip_idx, tag):
    n = 2 * part.shape[1]
    full = _add_chips_call(part, recv, jnp.concatenate([chip_idx, c_idx]), tag)
    return _sibling_join_call(full, tag).reshape(n, D)


def _local_step(x, mod, cact, target, wg, pack, small, c_idx, chip_idx):
    p, h1, wg = _fwd_in_call(x, mod, small["pre_tm"], wg, small["b_in"], pack, small["order"])
    o, oa, st, wg = _hgrn_fwd_call(p, small["logits"], small["hg_norm"], wg, pack)
    u, uc, cb, wg = _conv_fwd_call(p, small["conv_dw"], small["conv_db"], small["ln_g"], small["ln_b"], wg, pack)
    ya, yb, mg, y, x2, h2 = _merge_fwd_call(oa, cb, p, x, mod, small["post_tm"], small["pre_cm"], wg)
    z, da, dy2, dx2, acc_f = _ffn_call(h2, x2, target, mod, small["post_cm"], small["pre_cm"], wg)

    g_ff = _wgrad_call(None, h2, da, "wgrad_ff1", D, lambda i, j: (j, 0), 2 * R_FF)
    g_ff = _wgrad_call(g_ff, z, dy2, "wgrad_ff2", D, lambda i, j: (i, 1), 2 * R_FF)
    part_ff = _rs_begin(g_ff, c_idx, "ff")
    dy, dya, dyb, doa, dcb, dp_gt, acc_m, bs_gt = _merge_bwd_call(dx2, y, ya, yb, p, mod, small["post_tm"], wg)
    dp_hg, bs_hg, dlg, dgn, recv_ff = _hgrn_bwd_call(p, o, doa, st, small["logits"], small["hg_norm"], part_ff)

    g_br = _wgrad_rows_call(None, oa, dya, "wgrad_br_a", 0)
    g_br = _wgrad_rows_call(g_br, cb, dyb, "wgrad_br_b", 1)
    g_br = _wgrad_rows_call(g_br, mg, dy, "wgrad_out", 2)
    part_br = _rs_begin(g_br, c_idx, "br")
    dp_cv, bs_cv, ddw, acc_c, recv_br = _conv_bwd_call(dcb, uc, u, p, small["conv_dw"], small["ln_g"], small["ln_b"],
                                                        part_br)

    g_in = _wgrad_call(None, h1, dp_hg, "wgrad_in_hg", D, lambda i, j: (j // 2, j % 2), R_IN)
    g_in = _wgrad_call(g_in, h1, dp_cv, "wgrad_in_cv", D, lambda i, j: (2, j), R_IN)
    g_in = _wgrad_call(g_in, h1, dp_gt, "wgrad_in_gt", D, lambda i, j: (3, j), R_IN)
    part_in = _rs_begin(g_in, c_idx, "in")
    gx, acc_i, recv_in = _in_bwd_call(dp_hg, dp_cv, dp_gt, x, dx2, mod, small["pre_tm"], wg, part_in)

    red_ff = _rs_end(part_ff, recv_ff, c_idx, chip_idx, "ff")
    red_br = _rs_end(part_br, recv_br, c_idx, chip_idx, "br")
    red_in = _rs_end(part_in, recv_in, c_idx, chip_idx, "in")

    zrow = jnp.zeros((1, D), F32)
    rows = [acc_i[0:1], acc_i[1:2], acc_m[0:1], acc_f[2:3], acc_f[3:4], acc_f[0:1],
            acc_i[2:3], acc_m[1:2], acc_f[4:5], acc_f[1:2],
            jnp.concatenate([bs_hg, bs_cv, bs_gt], axis=1).reshape(8, D),
            dlg, dgn, acc_c[0:1], acc_c[1:2], acc_c[2:3],
            ddw,
            cact, acc_f[5:6]] + [zrow] * 6
    return gx, jnp.concatenate(rows, axis=0), red_in, red_ff, red_br


def kernel(x, c, w_ada, b_ada, pre_norm_tm, post_norm_tm, pre_norm_cm, post_norm_cm, w_in, b_in, hg_lb_logits, hg_norm, conv_dw, conv_db, conv_ln_g, conv_ln_b, w_br_a, w_br_b, w_out, w_ff1, w_ff2, loss_target, m_w_ada, m_b_ada, m_pre_norm_tm, m_post_norm_tm, m_pre_norm_cm, m_post_norm_cm, m_w_in, m_b_in, m_hg_lb_logits, m_hg_norm, m_conv_dw, m_conv_db, m_conv_ln_g, m_conv_ln_b, m_w_br_a, m_w_br_b, m_w_out, m_w_ff1, m_w_ff2, v_w_ada, v_b_ada, v_pre_norm_tm, v_post_norm_tm, v_pre_norm_cm, v_post_norm_cm, v_w_in, v_b_in, v_hg_lb_logits, v_hg_norm, v_conv_dw, v_conv_db, v_conv_ln_g, v_conv_ln_b, v_w_br_a, v_w_br_b, v_w_out, v_w_ff1, v_w_ff2):
    xi, yi, ci = lax.axis_index("x"), lax.axis_index("y"), lax.axis_index("c")
    chip = 2 * xi + yi
    c_idx = jnp.reshape(ci, (1,)).astype(jnp.int32)
    chip_idx = jnp.reshape(chip, (1,)).astype(jnp.int32)

    def pack_small(ada_b, pre_t, post_t, pre_c, post_c, in_b, lg, hgn, cdb, lng, lnb, cdw):
        flat = jnp.concatenate([cdw[0].reshape(-1), jnp.zeros((8 * D - CONV_K * 256,), F32)]).reshape(8, D)
        return jnp.concatenate([ada_b.reshape(6, D), pre_t, post_t, pre_c, post_c, in_b.reshape(8, D), lg, hgn,
                                cdb, lng, lnb, flat], axis=0)

    w_in_halves = w_in[0].reshape(D, 2, D).transpose(1, 0, 2).reshape(R_IN, D)
    pack = jnp.concatenate([w_in_halves, w_ff1[0], w_ff2[0], w_br_a[0], w_br_b[0], w_out[0]],
                           axis=0).astype(BF16)
    wg = lax.dynamic_update_slice(lax.empty((N_CHIPS, PACK_W, D), BF16), pack[None], (chip, 0, 0))
    wa = 6 * D // N_CHIPS
    me = 4 * xi + 2 * yi + ci
    dw_blk = jnp.concatenate([conv_dw[0].reshape(-1), jnp.zeros((8 * D - CONV_K * 256,), F32)]).reshape(8, D)
    dw_all, ca_all, mod_all = _prologue_call(
        dw_blk, jnp.broadcast_to(c, (8, D)), w_ada[0].astype(BF16),
        lax.dynamic_slice_in_dim(b_ada, chip * wa, wa, axis=1))
    order = jnp.stack([chip, 2 * (1 - xi) + yi, 2 * xi + (1 - yi), 2 * (1 - xi) + (1 - yi)]).astype(jnp.int32)
    dw_all = dw_all.reshape(N_CHIPS, 2, 8 * D)[:, 0, :CONV_K * 256].reshape(N_CHIPS, CONV_K, 256)
    dw_full = dw_all.transpose(1, 0, 2).reshape(CONV_K, D)
    cact = lax.dynamic_slice_in_dim(ca_all, me * 8, 1, axis=0)
    mod_mine = lax.dynamic_index_in_dim(mod_all.reshape(N_CHIPS, 2, N_DEV, 8, wa)[:, 0, :, 0, :], me, axis=1,
                                        keepdims=False)
    mod = mod_mine.reshape(1, 6 * D)

    small = dict(b_ada=b_ada, pre_tm=pre_norm_tm, post_tm=post_norm_tm, pre_cm=pre_norm_cm, post_cm=post_norm_cm,
                 b_in=b_in, logits=hg_lb_logits, hg_norm=hg_norm, conv_dw=dw_full, conv_db=conv_db,
                 ln_g=conv_ln_g, ln_b=conv_ln_b, order=order)

    gx, srows, red_in, red_ff, red_br = _local_step(x[0], mod, cact, loss_target[0], wg, pack, small, c_idx,
                                                    chip_idx)

    sall, ssum = _allgather_call(srows, "gather_small", in_vmem=True, with_sum=True)
    sall = sall.reshape(N_DEV, SMALL_ROWS, D)
    loss = jnp.sum(ssum[57])
    dmod_all = sall[:, 0:6, :].reshape(N_DEV, 6 * D)
    g_ada = _outer_call(sall[:, 56, :], lax.dynamic_slice_in_dim(dmod_all, chip * wa, wa, axis=1))
    g_dw = lax.dynamic_slice_in_dim(ssum[24:24 + CONV_K], chip * 256, 256, axis=1)
    g_small = jnp.concatenate(
        [ssum[0:24], jnp.concatenate([g_dw.reshape(-1), jnp.zeros((8 * D - CONV_K * 256,), F32)]).reshape(8, D)],
        axis=0)

    shapes = {"in": w_in.shape, "br_a": w_br_a.shape, "br_b": w_br_b.shape, "out": w_out.shape,
              "ff1": w_ff1.shape, "ff2": w_ff2.shape}
    offs = {"in": (red_in, 0, R_IN), "ff1": (red_ff, 0, R_FF), "ff2": (red_ff, R_FF, 2 * R_FF),
            "br_a": (red_br, 0, R_BR), "br_b": (red_br, R_BR, 2 * R_BR), "out": (red_br, 2 * R_BR, 3 * R_BR)}
    wmv = {"in": (w_in, m_w_in, v_w_in), "br_a": (w_br_a, m_w_br_a, v_w_br_a), "br_b": (w_br_b, m_w_br_b, v_w_br_b),
           "out": (w_out, m_w_out, v_w_out), "ff1": (w_ff1, m_w_ff1, v_w_ff1), "ff2": (w_ff2, m_w_ff2, v_w_ff2)}
    res = {}
    for n in offs:
        shp = shapes[n]
        g2d = offs[n][0][offs[n][1]:offs[n][2]]
        if n == "in":
            g2d = g2d.reshape(2, D, D).transpose(1, 0, 2)
        g2d = g2d.reshape(shp[1], shp[2])
        w_, m_, v_ = (a[0] for a in wmv[n])
        d_, m2_, v2_ = _adamw_call(w_, g2d, m_, v_, "adamw_" + n)
        res[n] = tuple(a.reshape(shp) for a in (g2d, d_, m2_, v2_))
    d_, m2_, v2_ = _adamw_call(w_ada[0], g_ada, m_w_ada[0], v_w_ada[0], "adamw_ada")
    res["ada"] = tuple(a.reshape(w_ada.shape) for a in (g_ada, d_, m2_, v2_))

    ws = pack_small(b_ada, pre_norm_tm, post_norm_tm, pre_norm_cm, post_norm_cm, b_in, hg_lb_logits, hg_norm,
                    conv_db, conv_ln_g, conv_ln_b, conv_dw)
    ms = pack_small(m_b_ada, m_pre_norm_tm, m_post_norm_tm, m_pre_norm_cm, m_post_norm_cm, m_b_in, m_hg_lb_logits,
                    m_hg_norm, m_conv_db, m_conv_ln_g, m_conv_ln_b, m_conv_dw)
    vs = pack_small(v_b_ada, v_pre_norm_tm, v_post_norm_tm, v_pre_norm_cm, v_post_norm_cm, v_b_in, v_hg_lb_logits,
                    v_hg_norm, v_conv_db, v_conv_ln_g, v_conv_ln_b, v_conv_dw)
    sres = (g_small,) + tuple(_adamw_call(ws, g_small, ms, vs, "adamw_small"))

    def unpack_small(t):
        return {"b_ada": t[0:6].reshape(1, 6 * D), "pre_tm": t[6:7], "post_tm": t[7:8], "pre_cm": t[8:9],
                "post_cm": t[9:10], "b_in": t[10:18].reshape(1, IN_COLS), "logits": t[18:20], "hg_norm": t[20:21],
                "conv_db": t[21:22], "ln_g": t[22:23], "ln_b": t[23:24],
                "conv_dw": t[24:32].reshape(-1)[:CONV_K * 256].reshape(1, CONV_K, 256)}

    order = ["ada", "b_ada", "pre_tm", "post_tm", "pre_cm", "post_cm", "in", "b_in", "logits", "hg_norm", "conv_dw",
             "conv_db", "ln_g", "ln_b", "br_a", "br_b", "out", "ff1", "ff2"]
    outs = [loss, gx.reshape(x.shape)]
    for kind in range(4):
        sm = unpack_small(sres[kind])
        for n in order:
            outs.append(res[n][kind] if n in res else sm[n])
    return tuple(outs)
```

```python
import functools

import jax
import jax.numpy as jnp
from jax import lax
from jax.experimental import pallas as pl
from jax.experimental.pallas import tpu as pltpu

F32, BF16 = jnp.float32, jnp.bfloat16
SDS = jax.ShapeDtypeStruct
BS = pl.BlockSpec
MESH = pl.DeviceIdType.MESH
HI = lax.Precision.HIGHEST

D = 1024
D_FF = 4096
IN_COLS = 8192
HEADS, DK = 8, 128
CHUNK = 128
CONV_K = 31
HALO = 32
SUB = 32
EPS = 1e-6
N_CHIPS, N_DEV = 4, 8
TM = 256
TB = 256
VMEM_LIMIT = 56 * 1024 * 1024

R_IN, R_BR, R_FF = 2048, 256, 1024
PACK_W = R_IN + 3 * R_BR + 2 * R_FF
O_IN, O_FF1, O_FF2, O_BRA, O_BRB, O_OUT = 0, 2048, 3072, 4096, 4352, 4608
SMALL_ROWS = 64

ADAM_LR, ADAM_B1, ADAM_B2, ADAM_EPS, ADAM_WD, ADAM_STEP = 0.001, 0.9, 0.999, 1e-08, 0.01, 10

NN = (((1,), (0,)), ((), ()))
NT = (((1,), (1,)), ((), ()))
TN = (((0,), (0,)), ((), ()))


def _mm(a, b, dims=NN, precision=None):
    return lax.dot_general(a, b, dims, preferred_element_type=F32, precision=precision)


def _sig(v):
    return jax.nn.sigmoid(v)


def _dsilu(v, s):
    return s * (1.0 + v * (1.0 - s))


def _params(*sem):
    return pltpu.CompilerParams(dimension_semantics=sem if sem else None, vmem_limit_bytes=VMEM_LIMIT)


def _rowsum(v):
    return jnp.sum(v, axis=0, keepdims=True)


def _mesh_pos():
    return lax.axis_index("x"), lax.axis_index("y"), lax.axis_index("c")


def _allgather_parts(x_ref, out_ref, send_sems, recv_sems, local_sem):
    m_per = x_ref.shape[0]
    x, y, c = _mesh_pos()
    me, sibling = (x, y, c), (x, y, 1 - c)
    chips = [(1 - x, y), (x, 1 - y), (1 - x, 1 - y)]

    def rows(px, py, pc):
        return out_ref.at[pl.ds((4 * px + 2 * py + pc) * m_per, m_per), :]

    def copy(k, block, to, src=None):
        return pltpu.make_async_remote_copy(
            src_ref=rows(*block) if src is None else src, dst_ref=rows(*block),
            send_sem=send_sems.at[k], recv_sem=recv_sems.at[k], device_id=to, device_id_type=MESH)

    def first():
        return [copy(0, me, sibling, src=x_ref)] + [copy(1 + j, me, (*chip, c), src=x_ref)
                                                    for j, chip in enumerate(chips)]

    def start():
        pltpu.make_async_copy(x_ref, rows(*me), local_sem).start()
        for cp in first():
            cp.start()

    def finish():
        passed = [copy(4 + j, (*chip, c), sibling) for j, chip in enumerate(chips)]
        for j, chip in enumerate(chips):
            copy(1 + j, (*chip, c), me).wait_recv()
            passed[j].start()
        copy(0, sibling, me).wait_recv()
        for j, chip in enumerate(chips):
            copy(4 + j, (*chip, 1 - c), me).wait_recv()
        for cp in first() + passed:
            cp.wait_send()
        pltpu.make_async_copy(x_ref, rows(*me), local_sem).wait()

    return start, finish


def _allgather(x_ref, out_ref, send_sems, recv_sems, local_sem):
    start, finish = _allgather_parts(x_ref, out_ref, send_sems, recv_sems, local_sem)
    start()
    finish()


def _allgather_sems():
    return [pltpu.SemaphoreType.DMA((7,)), pltpu.SemaphoreType.DMA((7,)), pltpu.SemaphoreType.DMA]


def _allgather_call(blk, name, in_vmem, with_sum):
    m_per, n = blk.shape

    def body(x_ref, out_ref, *rest):
        if with_sum:
            sum_ref, send_sems, recv_sems, local_sem = rest
        else:
            send_sems, recv_sems, local_sem = rest
        _allgather(x_ref, out_ref, send_sems, recv_sems, local_sem)
        if with_sum:
            acc = out_ref[0:m_per, :]
            for d in range(1, N_DEV):
                acc = acc + out_ref[d * m_per:(d + 1) * m_per, :]
            sum_ref[...] = acc

    space = pltpu.VMEM if in_vmem else pl.ANY
    out_shape = [SDS((N_DEV * m_per, n), blk.dtype)]
    out_specs = [BS(memory_space=space)]
    if with_sum:
        out_shape.append(SDS((m_per, n), blk.dtype))
        out_specs.append(BS(memory_space=pltpu.VMEM))
    return pl.pallas_call(
        body, name=name, out_shape=out_shape, in_specs=[BS(memory_space=space)], out_specs=out_specs,
        scratch_shapes=[pltpu.SemaphoreType.DMA((7,)), pltpu.SemaphoreType.DMA((7,)), pltpu.SemaphoreType.DMA],
        compiler_params=pltpu.CompilerParams(vmem_limit_bytes=VMEM_LIMIT),
    )(blk)


def _gather_sems(n_ranges):
    return [pltpu.SemaphoreType.DMA((6 * n_ranges,)), pltpu.SemaphoreType.DMA((6 * n_ranges,))]


def _pack_gather(pack_ref, wg_ref, send_sems, recv_sems, ranges):
    x, y, c = _mesh_pos()
    me, sibling = (x, y, c), (x, y, 1 - c)
    chips = [(1 - x, y), (x, 1 - y), (1 - x, 1 - y)]

    def land(r, px, py, pc):
        off, n = ranges[r]
        return wg_ref.at[2 * px + py, pl.ds(off + pc * (n // 2), n // 2), :]

    def mine(r):
        off, n = ranges[r]
        return pack_ref.at[pl.ds(off + c * (n // 2), n // 2), :]

    def copy(r, k, block, to, src=None):
        return pltpu.make_async_remote_copy(
            src_ref=land(r, *block) if src is None else src, dst_ref=land(r, *block),
            send_sem=send_sems.at[6 * r + k], recv_sem=recv_sems.at[6 * r + k], device_id=to, device_id_type=MESH)

    def start():
        for r in range(len(ranges)):
            for j, chip in enumerate(chips):
                copy(r, j, me, (*chip, c), src=mine(r)).start()

    def arrive(j):
        for r in range(len(ranges)):
            copy(r, j, (*chips[j], c), me).wait_recv()
            copy(r, 3 + j, (*chips[j], c), sibling).start()
        for r in range(len(ranges)):
            copy(r, 3 + j, (*chips[j], 1 - c), me).wait_recv()

    def drain():
        for r in range(len(ranges)):
            for j, chip in enumerate(chips):
                copy(r, j, me, (*chip, c), src=mine(r)).wait_send()
                copy(r, 3 + j, (*chip, c), sibling).wait_send()

    def finish():
        for j in range(3):
            arrive(j)
        drain()

    return start, finish, arrive, drain


def _prologue_call(dw_blk, c_blk, w_ada, b_ada):
    wa = w_ada.shape[1]

    def body(dw_ref, c_ref, wa_ref, ba_ref, dwg_ref, ca_ref, modg_ref,
             cg_scr, part_scr, s1, r1, l1, s2, r2, l2, s3, r3, l3):
        _allgather(c_ref, cg_scr, s2, r2, l2)
        cv = cg_scr[...]
        ca = cv * _sig(cv)
        ca_ref[...] = ca
        part_scr[...] = _mm(ca.astype(BF16), wa_ref[...]) + ba_ref[...]
        _allgather(part_scr, modg_ref, s3, r3, l3)
        _allgather(dw_ref, dwg_ref, s1, r1, l1)

    vm = BS(memory_space=pltpu.VMEM)
    return pl.pallas_call(
        body, name="prologue_adaln_conv_dw",
        out_shape=(SDS((N_DEV * 8, D), F32), SDS((N_DEV * 8, D), F32), SDS((N_DEV * N_DEV * 8, wa), F32)),
        in_specs=[vm, vm, vm, vm], out_specs=(vm, vm, vm),
        scratch_shapes=[pltpu.VMEM((N_DEV * 8, D), F32), pltpu.VMEM((N_DEV * 8, wa), F32)]
        + _allgather_sems() + _allgather_sems() + _allgather_sems(),
        compiler_params=pltpu.CompilerParams(vmem_limit_bytes=VMEM_LIMIT),
    )(dw_blk, c_blk, w_ada, b_ada)


def _halves_exchange(g_ref, out_ref, send_sems, recv_sems):
    x, y, c = _mesh_pos()

    def copies():
        return [pltpu.make_async_remote_copy(
            src_ref=g_ref.at[k, 1 - c], dst_ref=out_ref.at[k], send_sem=send_sems.at[k], recv_sem=recv_sems.at[k],
            device_id=(x, y, 1 - c), device_id_type=MESH) for k in range(N_CHIPS)]

    def start():
        for cp in copies():
            cp.start()

    def finish():
        for cp in copies():
            cp.wait()

    return start, finish


def _halves_sems():
    return [pltpu.SemaphoreType.DMA((N_CHIPS,)), pltpu.SemaphoreType.DMA((N_CHIPS,))]


def _sibling_halves_call(g, tag):
    _, _, h, n = g.shape

    def body(g_ref, out_ref, send_sems, recv_sems):
        start, finish = _halves_exchange(g_ref, out_ref, send_sems, recv_sems)
        start()
        finish()

    return pl.pallas_call(
        body, name="rs_sibling_halves_" + tag, out_shape=SDS((N_CHIPS, h, n), g.dtype),
        in_specs=[BS(memory_space=pl.ANY)], out_specs=BS(memory_space=pl.ANY),
        scratch_shapes=_halves_sems(),
    )(g)


def _chip_exchange(p_ref, out_ref, send_sems, recv_sems):
    x, y, c = _mesh_pos()
    chips = [(1 - x, y), (x, 1 - y), (1 - x, 1 - y)]

    def copies():
        return [pltpu.make_async_remote_copy(
            src_ref=p_ref.at[2 * cx + cy], dst_ref=out_ref.at[j], send_sem=send_sems.at[j], recv_sem=recv_sems.at[j],
            device_id=(cx, cy, c), device_id_type=MESH) for j, (cx, cy) in enumerate(chips)]

    def start():
        for cp in copies():
            cp.start()

    def finish():
        for cp in copies():
            cp.wait()

    return start, finish


def _exchange_sems():
    return [pltpu.SemaphoreType.DMA((3,)), pltpu.SemaphoreType.DMA((3,))]


def _sibling_join_call(full, tag):
    _, h, n = full.shape
    q = h // 4

    def body(in_ref, out_ref, send_sems, recv_sems):
        x, y, c = _mesh_pos()

        def copy(k, half):
            return pltpu.make_async_remote_copy(
                src_ref=in_ref.at[half, pl.ds(k * q, q)], dst_ref=out_ref.at[half, pl.ds(k * q, q)],
                send_sem=send_sems.at[k], recv_sem=recv_sems.at[k],
                device_id=(x, y, 1 - c), device_id_type=MESH)

        for k in range(4):
            copy(k, c).start()
        for k in range(4):
            copy(k, c).wait_send()
            copy(k, 1 - c).wait_recv()

    return pl.pallas_call(
        body, name="rs_sibling_join_" + tag, out_shape=SDS(full.shape, full.dtype),
        in_specs=[BS(memory_space=pl.ANY)], out_specs=BS(memory_space=pl.ANY),
        scratch_shapes=[pltpu.SemaphoreType.DMA((4,)), pltpu.SemaphoreType.DMA((4,))],
        input_output_aliases={0: 0},
    )(full)


def _add_halves_call(g, recv, c_idx, tag):
    _, _, h, n = g.shape
    tr = h // 2

    def body(c_ref, g_ref, r_ref, o_ref):
        o_ref[...] = (g_ref[...].astype(F32) + r_ref[...].astype(F32)).astype(BF16)

    return pl.pallas_call(
        body, name="rs_add_halves_" + tag, out_shape=SDS((N_CHIPS, h, n), BF16),
        grid_spec=pltpu.PrefetchScalarGridSpec(
            num_scalar_prefetch=1, grid=(N_CHIPS, 2),
            in_specs=[BS((None, None, tr, n), lambda k, r, c_ref: (k, c_ref[0], r, 0)),
                      BS((None, tr, n), lambda k, r, c_ref: (k, r, 0))],
            out_specs=BS((None, tr, n), lambda k, r, c_ref: (k, r, 0))),
        compiler_params=_params("arbitrary", "arbitrary"),
    )(c_idx, g, recv)


def _add_chips_call(p, recv, chip_c_idx, tag):
    _, h, n = p.shape
    tr = h // 2

    def body(k_ref, p_ref, r_ref, o_ref):
        acc = p_ref[...].astype(F32)
        for j in range(3):
            acc = acc + r_ref[j].astype(F32)
        o_ref[...] = acc

    return pl.pallas_call(
        body, name="rs_add_chips_" + tag, out_shape=SDS((2, h, n), F32),
        grid_spec=pltpu.PrefetchScalarGridSpec(
            num_scalar_prefetch=1, grid=(2,),
            in_specs=[BS((None, tr, n), lambda r, k_ref: (k_ref[0], r, 0)),
                      BS((3, tr, n), lambda r, k_ref: (0, r, 0))],
            out_specs=BS((None, tr, n), lambda r, k_ref: (k_ref[1], r, 0))),
        compiler_params=_params("arbitrary"),
    )(chip_c_idx, p, recv)


def _load_rows(wg_hbm, w_vmem, sem, off):
    cp = pltpu.make_async_copy(wg_hbm.at[:, pl.ds(off, w_vmem.shape[1]), :], w_vmem, sem)
    cp.start()
    return cp


def _fwd_in_call(x, mod, pre_tm, wg, b_in, pack, order):
    S = x.shape[0]
    tmf = 2 * TM
    nt = S // tmf
    wc = IN_COLS // N_CHIPS

    def body(ord_ref, x_ref, mod_ref, g_ref, w_hbm, b_ref, pack_ref, p_ref, h_hbm, wg_out, w_vmem, h_scr, sem,
             send_sems, recv_sems):
        q, i = pl.program_id(0), pl.program_id(1)
        rows = pl.ds(pl.multiple_of(i * tmf, tmf), tmf)
        start, _, arrive, drain = _pack_gather(pack_ref, wg_out, send_sems, recv_sems, [(O_IN, R_IN)])

        def load_weights():
            cp = pltpu.make_async_copy(wg_out.at[ord_ref[q], pl.ds(O_IN, R_IN), :], w_vmem, sem)
            cp.start()
            cp.wait()

        @pl.when((q == 0) & (i == 0))
        def _():
            start()
            load_weights()

        for j in range(3):
            @pl.when((q == j + 1) & (i == 0))
            def _(j=j):
                arrive(j)
                load_weights()

        @pl.when(q == 0)
        def _():
            xv = x_ref[...]
            r = lax.rsqrt(jnp.mean(xv * xv, axis=-1, keepdims=True) + EPS)
            h = xv * r * g_ref[...] * (1.0 + mod_ref[:, D:2 * D]) + mod_ref[:, 0:D]
            h_scr[rows, :] = h.astype(BF16)

        hb = h_scr[rows, :]
        for k in range(wc // D):
            p_ref[:, k * D:(k + 1) * D] = _mm(hb, w_vmem[k * D:(k + 1) * D, :]) + b_ref[:, k * D:(k + 1) * D]

        @pl.when((q == N_CHIPS - 1) & (i == nt - 1))
        def _():
            cp = pltpu.make_async_copy(h_scr, h_hbm, sem)
            cp.start()
            drain()
            cp.wait()

    hbm = BS(memory_space=pl.ANY)
    return pl.pallas_call(
        body, name="fwd_in", out_shape=(SDS((S, IN_COLS), F32), SDS((S, D), BF16), SDS(wg.shape, wg.dtype)),
        grid_spec=pltpu.PrefetchScalarGridSpec(
            num_scalar_prefetch=1, grid=(N_CHIPS, nt),
            in_specs=[BS((tmf, D), lambda q, i, o: (jnp.where(q == 0, i, nt - 1), 0)),
                      BS((1, 6 * D), lambda q, i, o: (0, 0)),
                      BS((1, D), lambda q, i, o: (0, 0)), hbm, BS((1, wc), lambda q, i, o: (0, o[q])), hbm],
            out_specs=(BS((tmf, wc), lambda q, i, o: (i, o[q])), hbm, hbm),
            scratch_shapes=[pltpu.VMEM((R_IN, D), BF16), pltpu.VMEM((S, D), BF16), pltpu.SemaphoreType.DMA]
            + _gather_sems(1)),
        input_output_aliases={4: 2},
        compiler_params=_params("arbitrary", "arbitrary"),
    )(order, x, mod, pre_tm, wg, b_in, pack)


def _lower_bound(lg_ref):
    l0, l1 = lg_ref[0:1, :], lg_ref[1:2, :]
    mx = jnp.maximum(l0, l1)
    e0, e1 = jnp.exp(l0 - mx), jnp.exp(l1 - mx)
    return e0 / (e0 + e1)


def _tri_masks():
    ri = lax.broadcasted_iota(jnp.int32, (CHUNK, CHUNK), 0)
    ci = lax.broadcasted_iota(jnp.int32, (CHUNK, CHUNK), 1)
    return (ri >= ci).astype(F32), (ci >= ri).astype(F32)


def _cumsum_mm(tri, g):
    tb = tri.astype(BF16)
    hi = g.astype(BF16)
    r1 = g - hi.astype(F32)
    mid = r1.astype(BF16)
    lo = (r1 - mid.astype(F32)).astype(BF16)
    return _mm(tb, hi) + _mm(tb, mid) + _mm(tb, lo)


def _hg_gates(q_r, f_r, lb, tril):
    sq = _sig(q_r)
    q = q_r * sq
    sf = _sig(f_r)
    f = lb + (1.0 - lb) * sf
    k = 1.0 - f
    g = jnp.log(f)
    b = _cumsum_mm(tril, g)
    b_last = _rowsum(g)
    row = lax.broadcasted_iota(jnp.int32, g.shape, 0)
    ref = _rowsum(jnp.where(row < CHUNK // 2, g, 0.0))
    e = jnp.exp(b)
    eq = jnp.exp(jnp.minimum(b - ref, 80.0))
    ek = jnp.exp(jnp.minimum(ref - b, 80.0))
    dd = jnp.exp(b_last - b)
    return dict(sq=sq, q=q, sf=sf, f=f, k=k, e=e, eq=eq, ek=ek, dd=dd, elast=jnp.exp(b_last),
                qe=q * e, qt=q * eq, kt=k * ek, kd=k * dd)


def _hgrn_fwd_call(p, logits, gn, wg, pack):
    S = p.shape[0]
    ncb = TB // CHUNK
    ranges = [(O_FF1, R_FF)]

    def body(q_ref, f_ref, v_ref, og_ref, lg_ref, gn_ref, wg_in, pack_ref, o_ref, oa_ref, st_ref, wg_out,
             st_scr, send_sems, recv_sems):
        start, finish, _, _ = _pack_gather(pack_ref, wg_out, send_sems, recv_sems, ranges)

        @pl.when(pl.program_id(0) == 0)
        def _():
            start()
            st_scr[...] = jnp.zeros_like(st_scr)

        lb = _lower_bound(lg_ref)
        tril, _ = _tri_masks()

        def chunk(ci, carry):
            rows = pl.ds(pl.multiple_of(ci * CHUNK, CHUNK), CHUNK)
            st_ref[ci] = st_scr[...]
            t = _hg_gates(q_ref[rows, :], f_ref[rows, :], lb, tril)
            v = v_ref[rows, :]
            for h in range(HEADS):
                sl = slice(h * DK, (h + 1) * DK)
                stp = st_scr[:, sl]
                vb = v[:, sl].astype(BF16)
                inter = _mm(t["qe"][:, sl].astype(BF16), stp.astype(BF16), NT)
                a = jnp.where(tril > 0.5, _mm(t["qt"][:, sl].astype(BF16), t["kt"][:, sl].astype(BF16), NT), 0.0)
                o = inter + _mm(a.astype(BF16), vb)
                st_scr[:, sl] = stp * t["elast"][:, sl] + _mm(vb, t["kd"][:, sl].astype(BF16), TN)
                oh = o * lax.rsqrt(jnp.mean(o * o, axis=-1, keepdims=True) + EPS)
                og = og_ref[rows, sl]
                o_ref[rows, sl] = o
                oa_ref[rows, sl] = (oh * gn_ref[:, sl] * (og * _sig(og))).astype(BF16)
            return carry

        lax.fori_loop(0, ncb, chunk, 0)

        @pl.when(pl.program_id(0) == S // TB - 1)
        def _():
            finish()

    col = lambda j: BS((TB, D), lambda i, j=j: (i, j))
    hbm = BS(memory_space=pl.ANY)
    return pl.pallas_call(
        body, name="hgrn_fwd", grid=(S // TB,),
        out_shape=(SDS((S, D), F32), SDS((S, D), BF16), SDS((S // CHUNK, DK, D), F32), SDS(wg.shape, wg.dtype)),
        in_specs=[col(0), col(1), col(2), col(3), BS((2, D), lambda i: (0, 0)), BS((1, D), lambda i: (0, 0)),
                  hbm, hbm],
        out_specs=(BS((TB, D), lambda i: (i, 0)), BS((TB, D), lambda i: (i, 0)),
                   BS((ncb, DK, D), lambda i: (i, 0, 0)), hbm),
        scratch_shapes=[pltpu.VMEM((DK, D), F32)] + _gather_sems(len(ranges)),
        input_output_aliases={6: 3},
        compiler_params=_params("arbitrary"),
    )(p, p, p, p, logits, gn, wg, pack)


def _layernorm_stats(uc):
    mu = jnp.mean(uc, axis=-1, keepdims=True)
    xc = uc - mu
    rs = lax.rsqrt(jnp.mean(xc * xc, axis=-1, keepdims=True) + EPS)
    return xc * rs, rs


EXT = HALO + TM + 8


def _fill_shifted(ext, shifted):
    for m in range(1, 8):
        shifted[m - 1] = ext[m:m + HALO + TM, :]


def _window(ext, shifted, s0, n):
    m = s0 % 8
    q = s0 - m
    return ext[q:q + n, :] if m == 0 else shifted[m - 1, q:q + n, :]


def _conv_fwd_call(p, dw, db, ln_g, ln_b, wg, pack):
    S = p.shape[0]
    ranges = [(O_FF2, R_FF), (O_BRA, 3 * R_BR)]

    def body(cv_ref, cg_ref, dw_ref, db_ref, g_ref, b_ref, wg_in, pack_ref, u_ref, uc_ref, cb_ref, wg_out,
             uext, ush, send_sems, recv_sems):
        start, finish, _, _ = _pack_gather(pack_ref, wg_out, send_sems, recv_sems, ranges)

        @pl.when(pl.program_id(0) == 0)
        def _():
            start()
            uext[0:HALO, :] = jnp.zeros((HALO, D), F32)
            uext[HALO + TM:EXT, :] = jnp.zeros((EXT - HALO - TM, D), F32)

        u = cv_ref[...] * _sig(cg_ref[...])
        uext[HALO:HALO + TM, :] = u
        u_ref[...] = u
        _fill_shifted(uext, ush)
        for rb in range(TM // SUB):
            acc = jnp.broadcast_to(db_ref[...], (SUB, D))
            for j in range(CONV_K):
                s0 = HALO - (CONV_K - 1) + j + rb * SUB
                acc = acc + dw_ref[j:j + 1, :] * _window(uext, ush, s0, SUB)
            uc_ref[rb * SUB:(rb + 1) * SUB, :] = acc
            xh, _ = _layernorm_stats(acc)
            ln = xh * g_ref[...] + b_ref[...]
            cb_ref[rb * SUB:(rb + 1) * SUB, :] = (ln * _sig(ln)).astype(BF16)
        uext[0:HALO, :] = uext[TM:TM + HALO, :]

        @pl.when(pl.program_id(0) == S // TM - 1)
        def _():
            finish()

    vec = BS((1, D), lambda i: (0, 0))
    hbm = BS(memory_space=pl.ANY)
    return pl.pallas_call(
        body, name="conv_fwd", grid=(S // TM,),
        out_shape=(SDS((S, D), F32), SDS((S, D), F32), SDS((S, D), BF16), SDS(wg.shape, wg.dtype)),
        in_specs=[BS((TM, D), lambda i: (i, 4)), BS((TM, D), lambda i: (i, 5)),
                  BS((CONV_K, D), lambda i: (0, 0)), vec, vec, vec, hbm, hbm],
        out_specs=(BS((TM, D), lambda i: (i, 0)),) * 3 + (hbm,),
        scratch_shapes=[pltpu.VMEM((EXT, D), F32), pltpu.VMEM((7, HALO + TM, D), F32)] + _gather_sems(len(ranges)),
        input_output_aliases={6: 3},
        compiler_params=_params("arbitrary"),
    )(p, p, dw, db, ln_g, ln_b, wg, pack)


def _mm_rows(a, w_ref):
    acc = _mm(a[:, 0:R_BR], w_ref[0])
    for k in range(1, N_CHIPS):
        acc = acc + _mm(a[:, k * R_BR:(k + 1) * R_BR], w_ref[k])
    return acc


def _mm_rows_t(a, w_ref):
    return jnp.concatenate([_mm(a, w_ref[k], NT) for k in range(N_CHIPS)], axis=1)


def _br_spec(off):
    return BS((N_CHIPS, R_BR, D), lambda i: (0, off // R_BR, 0))


def _merge_fwd_call(oa, cb, p, x, mod, post_tm, pre_cm, wg):
    S = x.shape[0]

    def body(oa_ref, cb_ref, ga_ref, gb_ref, x_ref, mod_ref, post_ref, pre_ref, wa_ref, wb_ref, wo_ref,
             ya_ref, yb_ref, mg_ref, y_ref, x2_ref, h2_ref):
        ya = _mm_rows(oa_ref[...], wa_ref)
        yb = _mm_rows(cb_ref[...], wb_ref)
        ya_ref[...] = ya.astype(BF16)
        yb_ref[...] = yb.astype(BF16)
        mg = (_sig(ga_ref[...]) * ya + _sig(gb_ref[...]) * yb).astype(BF16)
        mg_ref[...] = mg
        y = _mm_rows(mg, wo_ref)
        y_ref[...] = y
        n = y * lax.rsqrt(jnp.mean(y * y, axis=-1, keepdims=True) + EPS) * post_ref[...]
        x2 = x_ref[...] + mod_ref[:, 2 * D:3 * D] * n
        x2_ref[...] = x2
        r2 = lax.rsqrt(jnp.mean(x2 * x2, axis=-1, keepdims=True) + EPS)
        h2 = x2 * r2 * pre_ref[...] * (1.0 + mod_ref[:, 4 * D:5 * D]) + mod_ref[:, 3 * D:4 * D]
        h2_ref[...] = h2.astype(BF16)

    tile = BS((TM, D), lambda i: (i, 0))
    vec = BS((1, D), lambda i: (0, 0))
    return pl.pallas_call(
        body, name="merge_fwd", grid=(S // TM,),
        out_shape=(SDS((S, D), BF16), SDS((S, D), BF16), SDS((S, D), BF16), SDS((S, D), F32), SDS((S, D), F32),
                   SDS((S, D), BF16)),
        in_specs=[tile, tile, BS((TM, D), lambda i: (i, 6)), BS((TM, D), lambda i: (i, 7)), tile,
                  BS((1, 6 * D), lambda i: (0, 0)), vec, vec, _br_spec(O_BRA), _br_spec(O_BRB), _br_spec(O_OUT)],
        out_specs=(tile,) * 6,
        compiler_params=_params("arbitrary"),
    )(oa, cb, p, p, x, mod, post_tm, pre_cm, wg, wg, wg)


def _ffn_call(h2, x2, target, mod, post_cm, pre_cm, wg):
    S = x2.shape[0]

    def body(h2_ref, x2_ref, t_ref, mod_ref, post_ref, pre_ref, w_hbm,
             z_ref, da_ref, dy2_ref, dx2_ref, acc_ref, w1_v, w2_v, ra_scr, sems):
        @pl.when(pl.program_id(0) == 0)
        def _():
            c1 = _load_rows(w_hbm, w1_v, sems.at[0], O_FF1)
            c2 = _load_rows(w_hbm, w2_v, sems.at[1], O_FF2)
            c1.wait()
            c2.wait()
            acc_ref[...] = jnp.zeros_like(acc_ref)

        h2 = h2_ref[...]
        for k in range(N_CHIPS):
            ra = jnp.maximum(_mm(h2, w1_v[k]), 0.0)
            ra_scr[:, k * D:(k + 1) * D] = ra
            z_ref[:, k * D:(k + 1) * D] = (ra * ra).astype(BF16)
        y2 = _mm(z_ref[:, 0:D], w2_v[0])
        for k in range(1, N_CHIPS):
            y2 = y2 + _mm(z_ref[:, k * D:(k + 1) * D], w2_v[k])
        ry = lax.rsqrt(jnp.mean(y2 * y2, axis=-1, keepdims=True) + EPS)
        yn = y2 * ry
        n = yn * post_ref[...]
        g2 = mod_ref[:, 5 * D:6 * D]
        x2 = x2_ref[...]
        err = x2 + g2 * n - t_ref[...]
        acc_ref[5:6, :] += _rowsum(err * err) * (0.5 / D)
        dout = err * (1.0 / D)
        acc_ref[0:1, :] += _rowsum(dout * n)
        dn = dout * g2
        acc_ref[1:2, :] += _rowsum(dn * yn)
        dyn = dn * post_ref[...]
        dy2 = (ry * (dyn - yn * jnp.mean(dyn * yn, axis=-1, keepdims=True))).astype(BF16)
        dy2_ref[...] = dy2
        for k in range(N_CHIPS):
            dz = _mm(dy2, w2_v[k], NT)
            da_ref[:, k * D:(k + 1) * D] = (dz * (2.0 * ra_scr[:, k * D:(k + 1) * D])).astype(BF16)
        dh2 = jnp.zeros((TM, D), F32)
        for k in range(N_CHIPS):
            dh2 = dh2 + _mm(da_ref[:, k * D:(k + 1) * D], w1_v[k], NT)
        r2 = lax.rsqrt(jnp.mean(x2 * x2, axis=-1, keepdims=True) + EPS)
        xn = x2 * r2
        yv = xn * pre_ref[...]
        acc_ref[2:3, :] += _rowsum(dh2)
        acc_ref[3:4, :] += _rowsum(dh2 * yv)
        dyv = dh2 * (1.0 + mod_ref[:, 4 * D:5 * D])
        acc_ref[4:5, :] += _rowsum(dyv * xn)
        dxn = dyv * pre_ref[...]
        dx2_ref[...] = dout + r2 * (dxn - xn * jnp.mean(dxn * xn, axis=-1, keepdims=True))

    tile = BS((TM, D), lambda i: (i, 0))
    wide = BS((TM, D_FF), lambda i: (i, 0))
    vec = BS((1, D), lambda i: (0, 0))
    return pl.pallas_call(
        body, name="ffn_fwd_bwd", grid=(S // TM,),
        out_shape=(SDS((S, D_FF), BF16), SDS((S, D_FF), BF16), SDS((S, D), BF16), SDS((S, D), F32),
                   SDS((8, D), F32)),
        in_specs=[tile, tile, tile, BS((1, 6 * D), lambda i: (0, 0)), vec, vec, BS(memory_space=pl.ANY)],
        out_specs=(wide, wide, tile, tile, BS((8, D), lambda i: (0, 0))),
        scratch_shapes=[pltpu.VMEM((N_CHIPS, R_FF, D), BF16), pltpu.VMEM((N_CHIPS, R_FF, D), BF16),
                        pltpu.VMEM((TM, D_FF), F32),
                        pltpu.SemaphoreType.DMA((2,))],
        compiler_params=_params("arbitrary"),
    )(h2, x2, target, mod, post_cm, pre_cm, wg)


def _merge_bwd_call(dx2, y, ya, yb, p, mod, post_tm, wg, g):
    S = y.shape[0]

    def body(dx2_ref, y_ref, ya_ref, yb_ref, ga_ref, gb_ref, mod_ref, post_ref, wa_ref, wb_ref, wo_ref, g_ref,
             dy_ref, dya_ref, dyb_ref, doa_ref, dcb_ref, dpg_ref, acc_ref, bsum_ref, hr_ref, send_sems, recv_sems):
        start, finish = _halves_exchange(g_ref, hr_ref, send_sems, recv_sems)

        @pl.when(pl.program_id(0) == 0)
        def _():
            start()
            acc_ref[...] = jnp.zeros_like(acc_ref)
            bsum_ref[...] = jnp.zeros_like(bsum_ref)

        y = y_ref[...]
        ry = lax.rsqrt(jnp.mean(y * y, axis=-1, keepdims=True) + EPS)
        yn = y * ry
        dx2 = dx2_ref[...]
        acc_ref[0:1, :] += _rowsum(dx2 * (yn * post_ref[...]))
        dn = dx2 * mod_ref[:, 2 * D:3 * D]
        acc_ref[1:2, :] += _rowsum(dn * yn)
        dyn = dn * post_ref[...]
        dy = (ry * (dyn - yn * jnp.mean(dyn * yn, axis=-1, keepdims=True))).astype(BF16)
        dy_ref[...] = dy
        dmg = _mm_rows_t(dy, wo_ref)
        sa, sb = _sig(ga_ref[...]), _sig(gb_ref[...])
        dya = (dmg * sa).astype(BF16)
        dyb = (dmg * sb).astype(BF16)
        dya_ref[...] = dya
        dyb_ref[...] = dyb
        dga = dmg * ya_ref[...].astype(F32) * (sa * (1.0 - sa))
        dgb = dmg * yb_ref[...].astype(F32) * (sb * (1.0 - sb))
        dpg_ref[:, 0:D] = dga.astype(BF16)
        dpg_ref[:, D:2 * D] = dgb.astype(BF16)
        bsum_ref[:, 0:D] += _rowsum(dga)
        bsum_ref[:, D:2 * D] += _rowsum(dgb)
        doa_ref[...] = _mm_rows_t(dya, wa_ref)
        dcb_ref[...] = _mm_rows_t(dyb, wb_ref)

        @pl.when(pl.program_id(0) == S // TM - 1)
        def _():
            finish()

    tile = BS((TM, D), lambda i: (i, 0))
    vec = BS((1, D), lambda i: (0, 0))
    return pl.pallas_call(
        body, name="merge_bwd", grid=(S // TM,),
        out_shape=(SDS((S, D), BF16), SDS((S, D), BF16), SDS((S, D), BF16), SDS((S, D), F32), SDS((S, D), F32),
                   SDS((S, 2 * D), BF16), SDS((8, D), F32), SDS((1, 2 * D), F32),
                   SDS((N_CHIPS,) + g.shape[2:], g.dtype)),
        in_specs=[tile, tile, tile, tile, BS((TM, D), lambda i: (i, 6)), BS((TM, D), lambda i: (i, 7)),
                  BS((1, 6 * D), lambda i: (0, 0)), vec, _br_spec(O_BRA), _br_spec(O_BRB), _br_spec(O_OUT),
                  BS(memory_space=pl.ANY)],
        out_specs=(tile, tile, tile, tile, tile, BS((TM, 2 * D), lambda i: (i, 0)),
                   BS((8, D), lambda i: (0, 0)), BS((1, 2 * D), lambda i: (0, 0)), BS(memory_space=pl.ANY)),
        scratch_shapes=_halves_sems(),
        compiler_params=_params("arbitrary"),
    )(dx2, y, ya, yb, p, p, mod, post_tm, wg, wg, wg, g)


def _hgrn_bwd_call(p, o, doa, st, logits, gn, part, g):
    S = p.shape[0]
    nb = S // TB
    ncb = TB // CHUNK

    def body(q_ref, f_ref, v_ref, og_ref, o_ref, doa_ref, st_ref, lg_ref, gn_ref, part_ref, g_ref,
             dp_ref, bsum_ref, dlg_ref, dgn_ref, recv_ref, hr_ref,
             dst_scr, dlb_scr, dqe_s, dqt_s, dkt_s, dkd_s, dv_s, dog_s, dble_s, send_sems, recv_sems, hs, hr):
        i = pl.program_id(0)
        start, finish = _chip_exchange(part_ref, recv_ref, send_sems, recv_sems)
        start_h, finish_h = _halves_exchange(g_ref, hr_ref, hs, hr)

        @pl.when(i == 0)
        def _():
            start_h()
            start()
            dst_scr[...] = jnp.zeros_like(dst_scr)
            dlb_scr[...] = jnp.zeros_like(dlb_scr)
            bsum_ref[...] = jnp.zeros_like(bsum_ref)
            dgn_ref[...] = jnp.zeros_like(dgn_ref)

        lb = _lower_bound(lg_ref)
        tril, triu = _tri_masks()

        def chunk(tt, carry):
            ci = ncb - 1 - tt
            rows = pl.ds(pl.multiple_of(ci * CHUNK, CHUNK), CHUNK)
            q_r, f_r = q_ref[rows, :], f_ref[rows, :]
            t = _hg_gates(q_r, f_r, lb, tril)
            v = v_ref[rows, :]
            for h in range(HEADS):
                sl = slice(h * DK, (h + 1) * DK)
                stp = st_ref[ci, :, sl]
                stb = stp.astype(BF16)
                qeb = t["qe"][:, sl].astype(BF16)
                qtb = t["qt"][:, sl].astype(BF16)
                ktb = t["kt"][:, sl].astype(BF16)
                kdb = t["kd"][:, sl].astype(BF16)
                vb = v[:, sl].astype(BF16)
                a = jnp.where(tril > 0.5, _mm(qtb, ktb, NT), 0.0)
                o_h = o_ref[rows, sl]
                rinv = lax.rsqrt(jnp.mean(o_h * o_h, axis=-1, keepdims=True) + EPS)
                oh = o_h * rinv
                og = og_ref[rows, sl]
                so = _sig(og)
                d_oa = doa_ref[rows, sl]
                don = d_oa * (og * so)
                dog_s[:, sl] = d_oa * (oh * gn_ref[:, sl]) * _dsilu(og, so)
                dgn_ref[:, sl] += _rowsum(don * oh)
                doh = don * gn_ref[:, sl]
                do = (rinv * (doh - oh * jnp.mean(doh * oh, axis=-1, keepdims=True))).astype(BF16)
                dqe_s[:, sl] = _mm(do, stb, NN)
                dstp = _mm(do, qeb, TN)
                dab = jnp.where(tril > 0.5, _mm(do, vb, NT), 0.0).astype(BF16)
                dqt_s[:, sl] = _mm(dab, ktb, NN)
                dkt_s[:, sl] = _mm(dab, qtb, TN)
                dstn = dst_scr[:, sl]
                dsb = dstn.astype(BF16)
                dkd_s[:, sl] = _mm(vb, dsb, NN)
                dv_s[:, sl] = _mm(a.astype(BF16), do, TN) + _mm(kdb, dsb, NT)
                el = t["elast"][:, sl]
                dst_scr[:, sl] = dstn * el + dstp
                dble_s[:, sl] = el * _rowsum(stp * dstn)
            dqe, dqt, dkt, dkd = dqe_s[...], dqt_s[...], dkt_s[...], dkd_s[...]
            dq = dqe * t["e"] + dqt * t["eq"]
            dk = dkt * t["ek"] + dkd * t["dd"]
            dkk = dkd * t["kd"]
            qt_r = t["qt"].astype(BF16).astype(F32)
            kt_r = t["kt"].astype(BF16).astype(F32)
            dbv = dqe * t["qe"] + dqt * qt_r - dkt * kt_r - dkk
            dg = _cumsum_mm(triu, dbv) + (_rowsum(dkk) + dble_s[...])
            df = dg / t["f"] - dk
            sf = t["sf"]
            dlb_scr[...] += _rowsum(df * (1.0 - sf))
            dqr = dq * _dsilu(q_r, t["sq"])
            dfr = df * (1.0 - lb) * (sf * (1.0 - sf))
            dvv, dog = dv_s[...], dog_s[...]
            dp_ref[rows, 0:D] = dqr.astype(BF16)
            dp_ref[rows, D:2 * D] = dfr.astype(BF16)
            dp_ref[rows, 2 * D:3 * D] = dvv.astype(BF16)
            dp_ref[rows, 3 * D:4 * D] = dog.astype(BF16)
            bsum_ref[:, 0:D] += _rowsum(dqr)
            bsum_ref[:, D:2 * D] += _rowsum(dfr)
            bsum_ref[:, 2 * D:3 * D] += _rowsum(dvv)
            bsum_ref[:, 3 * D:4 * D] += _rowsum(dog)
            return carry

        lax.fori_loop(0, ncb, chunk, 0)

        dl = dlb_scr[...] * lb * (1.0 - lb)
        dlg_ref[0:1, :] = dl
        dlg_ref[1:2, :] = -dl

        @pl.when(i == nb - 1)
        def _():
            finish_h()
            finish()

    col = lambda j: BS((TB, D), lambda i, j=j: (nb - 1 - i, j))
    rev = BS((TB, D), lambda i: (nb - 1 - i, 0))
    cd = pltpu.VMEM((CHUNK, D), F32)
    return pl.pallas_call(
        body, name="hgrn_bwd", grid=(nb,),
        out_shape=(SDS((S, 4 * D), BF16), SDS((1, 4 * D), F32), SDS((2, D), F32), SDS((1, D), F32),
                   SDS((3,) + part.shape[1:], part.dtype), SDS((N_CHIPS,) + g.shape[2:], g.dtype)),
        in_specs=[col(0), col(1), col(2), col(3), rev, rev, BS((ncb, DK, D), lambda i: (nb - 1 - i, 0, 0)),
                  BS((2, D), lambda i: (0, 0)), BS((1, D), lambda i: (0, 0)), BS(memory_space=pl.ANY),
                  BS(memory_space=pl.ANY)],
        out_specs=(BS((TB, 4 * D), lambda i: (nb - 1 - i, 0)), BS((1, 4 * D), lambda i: (0, 0)),
                   BS((2, D), lambda i: (0, 0)), BS((1, D), lambda i: (0, 0)), BS(memory_space=pl.ANY),
                   BS(memory_space=pl.ANY)),
        scratch_shapes=[pltpu.VMEM((DK, D), F32), pltpu.VMEM((1, D), F32), cd, cd, cd, cd, cd, cd,
                        pltpu.VMEM((1, D), F32)] + _exchange_sems() + _halves_sems(),
        compiler_params=_params("arbitrary"),
    )(p, p, p, p, o, doa, st, logits, gn, part, g)


def _conv_bwd_call(dcb, uc, u, p, dw, ln_g, ln_b, part):
    S = uc.shape[0]
    nb = S // TM
    hb = TM // HALO

    def body(dcb_ref, uc_ref, u_ref, uh_ref, cv_ref, cg_ref, dw_ref, g_ref, b_ref, part_ref,
             dp_ref, bsum_ref, ddw_ref, acc_ref, recv_ref, uext, dext, ush, dsh, send_sems, recv_sems):
        i = pl.program_id(0)
        start, finish = _chip_exchange(part_ref, recv_ref, send_sems, recv_sems)

        @pl.when(i == 0)
        def _():
            start()
            dext[TM:EXT, :] = jnp.zeros((EXT - TM, D), F32)
            uext[HALO + TM:EXT, :] = jnp.zeros((EXT - HALO - TM, D), F32)
            bsum_ref[...] = jnp.zeros_like(bsum_ref)
            ddw_ref[...] = jnp.zeros_like(ddw_ref)
            acc_ref[...] = jnp.zeros_like(acc_ref)

        first_tile = (nb - 1 - i) == 0
        uext[0:HALO, :] = jnp.where(first_tile, 0.0, uh_ref[...])
        uext[HALO:HALO + TM, :] = u_ref[...]
        _fill_shifted(uext, ush)

        for rb in range(TM // SUB):
            rs_ = slice(rb * SUB, (rb + 1) * SUB)
            xh, rs = _layernorm_stats(uc_ref[rs_, :])
            ln = xh * g_ref[...] + b_ref[...]
            dln = dcb_ref[rs_, :] * _dsilu(ln, _sig(ln))
            acc_ref[1:2, :] += _rowsum(dln * xh)
            acc_ref[2:3, :] += _rowsum(dln)
            dxh = dln * g_ref[...]
            duc = rs * (dxh - jnp.mean(dxh, axis=-1, keepdims=True)
                        - xh * jnp.mean(dxh * xh, axis=-1, keepdims=True))
            dext[rs_, :] = duc
            acc_ref[0:1, :] += _rowsum(duc)
        _fill_shifted(dext, dsh)

        for j in range(CONV_K):
            part = jnp.zeros((SUB, D), F32)
            for rb in range(TM // SUB):
                s0 = HALO - (CONV_K - 1) + j + rb * SUB
                part = part + dext[rb * SUB:(rb + 1) * SUB, :] * _window(uext, ush, s0, SUB)
            ddw_ref[j:j + 1, :] += _rowsum(part)

        for rb in range(TM // SUB):
            rs_ = slice(rb * SUB, (rb + 1) * SUB)
            du = jnp.zeros((SUB, D), F32)
            for j in range(CONV_K):
                s0 = rb * SUB + (CONV_K - 1) - j
                du = du + dw_ref[j:j + 1, :] * _window(dext, dsh, s0, SUB)
            cg = cg_ref[rs_, :]
            sg = _sig(cg)
            dcv = du * sg
            dcg = du * cv_ref[rs_, :] * (sg * (1.0 - sg))
            dp_ref[rs_, 0:D] = dcv.astype(BF16)
            dp_ref[rs_, D:2 * D] = dcg.astype(BF16)
            bsum_ref[:, 0:D] += _rowsum(dcv)
            bsum_ref[:, D:2 * D] += _rowsum(dcg)

        dext[TM:TM + HALO, :] = dext[0:HALO, :]

        @pl.when(i == nb - 1)
        def _():
            finish()

    rev = BS((TM, D), lambda i: (nb - 1 - i, 0))
    vec = BS((1, D), lambda i: (0, 0))
    return pl.pallas_call(
        body, name="conv_bwd", grid=(nb,),
        out_shape=(SDS((S, 2 * D), BF16), SDS((1, 2 * D), F32), SDS((32, D), F32), SDS((8, D), F32),
                   SDS((3,) + part.shape[1:], part.dtype)),
        in_specs=[rev, rev, rev, BS((HALO, D), lambda i: (jnp.maximum((nb - 1 - i) * hb - 1, 0), 0)),
                  BS((TM, D), lambda i: (nb - 1 - i, 4)), BS((TM, D), lambda i: (nb - 1 - i, 5)),
                  BS((CONV_K, D), lambda i: (0, 0)), vec, vec, BS(memory_space=pl.ANY)],
        out_specs=(BS((TM, 2 * D), lambda i: (nb - 1 - i, 0)), BS((1, 2 * D), lambda i: (0, 0)),
                   BS((32, D), lambda i: (0, 0)), BS((8, D), lambda i: (0, 0)), BS(memory_space=pl.ANY)),
        scratch_shapes=[pltpu.VMEM((EXT, D), F32), pltpu.VMEM((EXT, D), F32),
                        pltpu.VMEM((7, HALO + TM, D), F32), pltpu.VMEM((7, HALO + TM, D), F32)] + _exchange_sems(),
        compiler_params=_params("arbitrary"),
    )(dcb, uc, u, u, p, p, dw, ln_g, ln_b, part)


def _in_bwd_call(dp_hg, dp_cv, dp_gt, x, dx2, mod, pre_tm, wg, part):
    S = x.shape[0]

    def body(hg_ref, cv_ref, gt_ref, x_ref, dx2_ref, mod_ref, g_ref, w_hbm, part_ref, gx_ref, acc_ref, recv_ref,
             w_vmem, sem, send_sems, recv_sems):
        start, finish = _chip_exchange(part_ref, recv_ref, send_sems, recv_sems)

        @pl.when(pl.program_id(0) == 0)
        def _():
            start()
            _load_rows(w_hbm, w_vmem, sem, O_IN).wait()
            acc_ref[...] = jnp.zeros_like(acc_ref)

        dh = jnp.zeros((TM, D), F32)
        for k in range(IN_COLS // D):
            src, kk = ((hg_ref, k), (cv_ref, k - 4), (gt_ref, k - 6))[0 if k < 4 else (1 if k < 6 else 2)]
            dh = dh + _mm(src[:, kk * D:(kk + 1) * D], w_vmem[k // 2, (k % 2) * D:(k % 2 + 1) * D, :], NT)
        xv = x_ref[...]
        r = lax.rsqrt(jnp.mean(xv * xv, axis=-1, keepdims=True) + EPS)
        xn = xv * r
        yv = xn * g_ref[...]
        acc_ref[0:1, :] += _rowsum(dh)
        acc_ref[1:2, :] += _rowsum(dh * yv)
        dyv = dh * (1.0 + mod_ref[:, D:2 * D])
        acc_ref[2:3, :] += _rowsum(dyv * xn)
        dxn = dyv * g_ref[...]
        gx_ref[...] = dx2_ref[...] + r * (dxn - xn * jnp.mean(dxn * xn, axis=-1, keepdims=True))

        @pl.when(pl.program_id(0) == S // TM - 1)
        def _():
            finish()

    tile = BS((TM, D), lambda i: (i, 0))
    return pl.pallas_call(
        body, name="in_bwd", grid=(S // TM,),
        out_shape=(SDS((S, D), F32), SDS((8, D), F32), SDS((3,) + part.shape[1:], part.dtype)),
        in_specs=[BS((TM, 4 * D), lambda i: (i, 0)), BS((TM, 2 * D), lambda i: (i, 0)),
                  BS((TM, 2 * D), lambda i: (i, 0)), tile, tile, BS((1, 6 * D), lambda i: (0, 0)),
                  BS((1, D), lambda i: (0, 0)), BS(memory_space=pl.ANY), BS(memory_space=pl.ANY)],
        out_specs=(tile, BS((8, D), lambda i: (0, 0)), BS(memory_space=pl.ANY)),
        scratch_shapes=[pltpu.VMEM((N_CHIPS, R_IN, D), BF16), pltpu.SemaphoreType.DMA] + _exchange_sems(),
        compiler_params=_params("arbitrary"),
    )(dp_hg, dp_cv, dp_gt, x, dx2, mod, pre_tm, wg, part)


def _wgrad_call(gp, a, b, name, bm, place, rows):
    S, M = a.shape
    N = b.shape[1]
    bk = min(S, 1024)
    nk = S // bk

    def body(a_ref, b_ref, *rest):
        o_ref, acc = rest[-2], rest[-1]
        k = pl.program_id(2)

        @pl.when(k == 0)
        def _():
            acc[...] = jnp.zeros_like(acc)

        acc[...] += _mm(a_ref[...], b_ref[...], TN)

        @pl.when(k == nk - 1)
        def _():
            o_ref[...] = acc[...].astype(BF16)

    in_specs = [BS((bk, bm), lambda i, j, k: (k, i)), BS((bk, D), lambda i, j, k: (k, j))]
    args = [a, b]
    if gp is not None:
        in_specs.append(BS(memory_space=pl.ANY))
        args.append(gp)
    return pl.pallas_call(
        body, name=name, grid=(M // bm, N // D, nk),
        out_shape=SDS((N_CHIPS, rows, D), BF16),
        in_specs=in_specs,
        out_specs=BS((None, bm, D), lambda i, j, k: (*place(i, j), 0)),
        scratch_shapes=[pltpu.VMEM((bm, D), F32)],
        input_output_aliases={} if gp is None else {2: 0},
        compiler_params=_params("parallel", "parallel", "arbitrary"),
    )(*args)


def _wgrad_rows_call(gp, a, b, name, blk):
    S = a.shape[0]
    bk = min(S, 1024)
    nk = S // bk

    def body(a_ref, b_ref, *rest):
        o_ref, acc = rest[-2], rest[-1]
        k = pl.program_id(0)

        @pl.when(k == 0)
        def _():
            acc[...] = jnp.zeros_like(acc)

        acc[...] += _mm(a_ref[...], b_ref[...], TN)

        @pl.when(k == nk - 1)
        def _():
            for c in range(N_CHIPS):
                o_ref[c] = acc[c * R_BR:(c + 1) * R_BR, :].astype(BF16)

    in_specs = [BS((bk, D), lambda k: (k, 0)), BS((bk, D), lambda k: (k, 0))]
    args = [a, b]
    if gp is not None:
        in_specs.append(BS(memory_space=pl.ANY))
        args.append(gp)
    return pl.pallas_call(
        body, name=name, grid=(nk,),
        out_shape=SDS((N_CHIPS, 3 * R_BR, D), BF16),
        in_specs=in_specs,
        out_specs=BS((N_CHIPS, R_BR, D), lambda k: (0, blk, 0)),
        scratch_shapes=[pltpu.VMEM((D, D), F32)],
        input_output_aliases={} if gp is None else {2: 0},
        compiler_params=_params("arbitrary"),
    )(*args)


def _outer_call(cact, dmod):
    n = dmod.shape[1]

    def body(a_ref, b_ref, o_ref):
        o_ref[...] = _mm(a_ref[...], b_ref[...], TN, HI)

    return pl.pallas_call(
        body, name="wgrad_ada", out_shape=SDS((D, n), F32),
        compiler_params=pltpu.CompilerParams(vmem_limit_bytes=VMEM_LIMIT),
    )(cact, dmod)


def _adamw_call(w, g, m, v, name):
    R, C = w.shape
    tr = R
    while tr * C > 512 * 1024 and tr % 16 == 0:
        tr //= 2
    c1 = 1.0 - ADAM_B1 ** ADAM_STEP
    c2 = 1.0 - ADAM_B2 ** ADAM_STEP

    def body(w_ref, g_ref, m_ref, v_ref, d_ref, m2_ref, v2_ref):
        g = g_ref[...]
        m2 = ADAM_B1 * m_ref[...] + (1.0 - ADAM_B1) * g
        v2 = ADAM_B2 * v_ref[...] + (1.0 - ADAM_B2) * (g * g)
        m2_ref[...] = m2
        v2_ref[...] = v2
        d_ref[...] = -ADAM_LR * ((m2 / c1) / (jnp.sqrt(v2 / c2) + ADAM_EPS) + ADAM_WD * w_ref[...])

    tile = BS((tr, C), lambda i: (i, 0))
    return pl.pallas_call(
        body, name=name, grid=(R // tr,), out_shape=(SDS((R, C), F32),) * 3,
        in_specs=[tile] * 4, out_specs=(tile,) * 3, compiler_params=_params("parallel"),
    )(w, g, m, v)


def _adamw_gather_call(w, g, m, v, srows, name):
    R, C = w.shape
    tr = R
    while tr * C > 512 * 1024 and tr % 16 == 0:
        tr //= 2
    nsteps = R // tr
    mr = srows.shape[0]
    c1 = 1.0 - ADAM_B1 ** ADAM_STEP
    c2 = 1.0 - ADAM_B2 ** ADAM_STEP

    def body(w_ref, g_ref, m_ref, v_ref, s_ref, d_ref, m2_ref, v2_ref, all_ref, sum_ref,
             x_scr, out_scr, send_sems, recv_sems, local_sem):
        i = pl.program_id(0)
        start, finish = _allgather_parts(x_scr, out_scr, send_sems, recv_sems, local_sem)

        @pl.when(i == 0)
        def _():
            x_scr[...] = s_ref[...]
            start()

        g = g_ref[...]
        m2 = ADAM_B1 * m_ref[...] + (1.0 - ADAM_B1) * g
        v2 = ADAM_B2 * v_ref[...] + (1.0 - ADAM_B2) * (g * g)
        m2_ref[...] = m2
        v2_ref[...] = v2
        d_ref[...] = -ADAM_LR * ((m2 / c1) / (jnp.sqrt(v2 / c2) + ADAM_EPS) + ADAM_WD * w_ref[...])

        @pl.when(i == nsteps - 1)
        def _():
            finish()
            all_ref[...] = out_scr[...]
            acc = out_scr[0:mr, :]
            for d in range(1, N_DEV):
                acc = acc + out_scr[d * mr:(d + 1) * mr, :]
            sum_ref[...] = acc

    tile = BS((tr, C), lambda i: (i, 0))
    return pl.pallas_call(
        body, name=name, grid=(nsteps,),
        out_shape=(SDS((R, C), F32),) * 3 + (SDS((N_DEV * mr, D), F32), SDS((mr, D), F32)),
        in_specs=[tile] * 4 + [BS((mr, D), lambda i: (0, 0))],
        out_specs=(tile,) * 3 + (BS((N_DEV * mr, D), lambda i: (0, 0)), BS((mr, D), lambda i: (0, 0))),
        scratch_shapes=[pltpu.VMEM((mr, D), F32), pltpu.VMEM((N_DEV * mr, D), F32)] + _allgather_sems(),
        compiler_params=_params("arbitrary"),
    )(w, g, m, v, srows)


def _rs_begin(g, c_idx, tag):
    n = g.shape[1]
    g = g.reshape(N_CHIPS, 2, n // 2, D)
    return _add_halves_call(g, _sibling_halves_call(g, tag), c_idx, tag)


def _rs_end(part, recv, c_idx, chip_idx, tag):
    n = 2 * part.shape[1]
    full = _add_chips_call(part, recv, jnp.concatenate([chip_idx, c_idx]), tag)
    return _sibling_join_call(full, tag).reshape(n, D)


def _local_step(x, mod, cact, target, wg, pack, small, c_idx, chip_idx):
    p, h1, wg = _fwd_in_call(x, mod, small["pre_tm"], wg, small["b_in"], pack, small["order"])
    o, oa, st, wg = _hgrn_fwd_call(p, small["logits"], small["hg_norm"], wg, pack)
    u, uc, cb, wg = _conv_fwd_call(p, small["conv_dw"], small["conv_db"], small["ln_g"], small["ln_b"], wg, pack)
    ya, yb, mg, y, x2, h2 = _merge_fwd_call(oa, cb, p, x, mod, small["post_tm"], small["pre_cm"], wg)
    z, da, dy2, dx2, acc_f = _ffn_call(h2, x2, target, mod, small["post_cm"], small["pre_cm"], wg)

    g_ff = _wgrad_call(None, h2, da, "wgrad_ff1", D, lambda i, j: (j, 0), 2 * R_FF)
    g_ff = _wgrad_call(g_ff, z, dy2, "wgrad_ff2", D, lambda i, j: (i, 1), 2 * R_FF)
    g_ff = g_ff.reshape(N_CHIPS, 2, R_FF, D)
    dy, dya, dyb, doa, dcb, dp_gt, acc_m, bs_gt, hr_ff = _merge_bwd_call(dx2, y, ya, yb, p, mod, small["post_tm"],
                                                                        wg, g_ff)
    part_ff = _add_halves_call(g_ff, hr_ff, c_idx, "ff")

    g_br = _wgrad_rows_call(None, oa, dya, "wgrad_br_a", 0)
    g_br = _wgrad_rows_call(g_br, cb, dyb, "wgrad_br_b", 1)
    g_br = _wgrad_rows_call(g_br, mg, dy, "wgrad_out", 2)
    g_br = g_br.reshape(N_CHIPS, 2, 3 * R_BR // 2, D)
    dp_hg, bs_hg, dlg, dgn, recv_ff, hr_br = _hgrn_bwd_call(p, o, doa, st, small["logits"], small["hg_norm"],
                                                            part_ff, g_br)
    part_br = _add_halves_call(g_br, hr_br, c_idx, "br")
    dp_cv, bs_cv, ddw, acc_c, recv_br = _conv_bwd_call(dcb, uc, u, p, small["conv_dw"], small["ln_g"], small["ln_b"],
                                                        part_br)

    g_in = _wgrad_call(None, h1, dp_hg, "wgrad_in_hg", D, lambda i, j: (j // 2, j % 2), R_IN)
    g_in = _wgrad_call(g_in, h1, dp_cv, "wgrad_in_cv", D, lambda i, j: (2, j), R_IN)
    g_in = _wgrad_call(g_in, h1, dp_gt, "wgrad_in_gt", D, lambda i, j: (3, j), R_IN)
    part_in = _rs_begin(g_in, c_idx, "in")
    gx, acc_i, recv_in = _in_bwd_call(dp_hg, dp_cv, dp_gt, x, dx2, mod, small["pre_tm"], wg, part_in)

    red_ff = _rs_end(part_ff, recv_ff, c_idx, chip_idx, "ff")
    red_br = _rs_end(part_br, recv_br, c_idx, chip_idx, "br")
    red_in = _rs_end(part_in, recv_in, c_idx, chip_idx, "in")

    zrow = jnp.zeros((1, D), F32)
    rows = [acc_i[0:1], acc_i[1:2], acc_m[0:1], acc_f[2:3], acc_f[3:4], acc_f[0:1],
            acc_i[2:3], acc_m[1:2], acc_f[4:5], acc_f[1:2],
            jnp.concatenate([bs_hg, bs_cv, bs_gt], axis=1).reshape(8, D),
            dlg, dgn, acc_c[0:1], acc_c[1:2], acc_c[2:3],
            ddw,
            cact, acc_f[5:6]] + [zrow] * 6
    return gx, jnp.concatenate(rows, axis=0), red_in, red_ff, red_br


def kernel(x, c, w_ada, b_ada, pre_norm_tm, post_norm_tm, pre_norm_cm, post_norm_cm, w_in, b_in, hg_lb_logits, hg_norm, conv_dw, conv_db, conv_ln_g, conv_ln_b, w_br_a, w_br_b, w_out, w_ff1, w_ff2, loss_target, m_w_ada, m_b_ada, m_pre_norm_tm, m_post_norm_tm, m_pre_norm_cm, m_post_norm_cm, m_w_in, m_b_in, m_hg_lb_logits, m_hg_norm, m_conv_dw, m_conv_db, m_conv_ln_g, m_conv_ln_b, m_w_br_a, m_w_br_b, m_w_out, m_w_ff1, m_w_ff2, v_w_ada, v_b_ada, v_pre_norm_tm, v_post_norm_tm, v_pre_norm_cm, v_post_norm_cm, v_w_in, v_b_in, v_hg_lb_logits, v_hg_norm, v_conv_dw, v_conv_db, v_conv_ln_g, v_conv_ln_b, v_w_br_a, v_w_br_b, v_w_out, v_w_ff1, v_w_ff2):
    xi, yi, ci = lax.axis_index("x"), lax.axis_index("y"), lax.axis_index("c")
    chip = 2 * xi + yi
    c_idx = jnp.reshape(ci, (1,)).astype(jnp.int32)
    chip_idx = jnp.reshape(chip, (1,)).astype(jnp.int32)

    def pack_small(ada_b, pre_t, post_t, pre_c, post_c, in_b, lg, hgn, cdb, lng, lnb, cdw):
        flat = jnp.concatenate([cdw[0].reshape(-1), jnp.zeros((8 * D - CONV_K * 256,), F32)]).reshape(8, D)
        return jnp.concatenate([ada_b.reshape(6, D), pre_t, post_t, pre_c, post_c, in_b.reshape(8, D), lg, hgn,
                                cdb, lng, lnb, flat], axis=0)

    w_in_halves = w_in[0].reshape(D, 2, D).transpose(1, 0, 2).reshape(R_IN, D)
    pack = jnp.concatenate([w_in_halves, w_ff1[0], w_ff2[0], w_br_a[0], w_br_b[0], w_out[0]],
                           axis=0).astype(BF16)
    wg = lax.dynamic_update_slice(lax.empty((N_CHIPS, PACK_W, D), BF16), pack[None], (chip, 0, 0))
    wa = 6 * D // N_CHIPS
    me = 4 * xi + 2 * yi + ci
    dw_blk = jnp.concatenate([conv_dw[0].reshape(-1), jnp.zeros((8 * D - CONV_K * 256,), F32)]).reshape(8, D)
    dw_all, ca_all, mod_all = _prologue_call(
        dw_blk, jnp.broadcast_to(c, (8, D)), w_ada[0].astype(BF16),
        lax.dynamic_slice_in_dim(b_ada, chip * wa, wa, axis=1))
    order = jnp.stack([chip, 2 * (1 - xi) + yi, 2 * xi + (1 - yi), 2 * (1 - xi) + (1 - yi)]).astype(jnp.int32)
    dw_all = dw_all.reshape(N_CHIPS, 2, 8 * D)[:, 0, :CONV_K * 256].reshape(N_CHIPS, CONV_K, 256)
    dw_full = dw_all.transpose(1, 0, 2).reshape(CONV_K, D)
    cact = lax.dynamic_slice_in_dim(ca_all, me * 8, 1, axis=0)
    mod_mine = lax.dynamic_index_in_dim(mod_all.reshape(N_CHIPS, 2, N_DEV, 8, wa)[:, 0, :, 0, :], me, axis=1,
                                        keepdims=False)
    mod = mod_mine.reshape(1, 6 * D)

    small = dict(b_ada=b_ada, pre_tm=pre_norm_tm, post_tm=post_norm_tm, pre_cm=pre_norm_cm, post_cm=post_norm_cm,
                 b_in=b_in, logits=hg_lb_logits, hg_norm=hg_norm, conv_dw=dw_full, conv_db=conv_db,
                 ln_g=conv_ln_g, ln_b=conv_ln_b, order=order)

    gx, srows, red_in, red_ff, red_br = _local_step(x[0], mod, cact, loss_target[0], wg, pack, small, c_idx,
                                                    chip_idx)

    shapes = {"in": w_in.shape, "br_a": w_br_a.shape, "br_b": w_br_b.shape, "out": w_out.shape,
              "ff1": w_ff1.shape, "ff2": w_ff2.shape}
    offs = {"in": (red_in, 0, R_IN), "ff1": (red_ff, 0, R_FF), "ff2": (red_ff, R_FF, 2 * R_FF),
            "br_a": (red_br, 0, R_BR), "br_b": (red_br, R_BR, 2 * R_BR), "out": (red_br, 2 * R_BR, 3 * R_BR)}
    wmv = {"in": (w_in, m_w_in, v_w_in), "br_a": (w_br_a, m_w_br_a, v_w_br_a), "br_b": (w_br_b, m_w_br_b, v_w_br_b),
           "out": (w_out, m_w_out, v_w_out), "ff1": (w_ff1, m_w_ff1, v_w_ff1), "ff2": (w_ff2, m_w_ff2, v_w_ff2)}
    res = {}
    for n in offs:
        shp = shapes[n]
        g2d = offs[n][0][offs[n][1]:offs[n][2]]
        if n == "in":
            g2d = g2d.reshape(2, D, D).transpose(1, 0, 2)
        g2d = g2d.reshape(shp[1], shp[2])
        w_, m_, v_ = (a[0] for a in wmv[n])
        if n == "in":
            d_, m2_, v2_, sall, ssum = _adamw_gather_call(w_, g2d, m_, v_, srows, "adamw_in")
        else:
            d_, m2_, v2_ = _adamw_call(w_, g2d, m_, v_, "adamw_" + n)
        res[n] = tuple(a.reshape(shp) for a in (g2d, d_, m2_, v2_))

    sall = sall.reshape(N_DEV, SMALL_ROWS, D)
    loss = jnp.sum(ssum[57])
    dmod_all = sall[:, 0:6, :].reshape(N_DEV, 6 * D)
    g_ada = _outer_call(sall[:, 56, :], lax.dynamic_slice_in_dim(dmod_all, chip * wa, wa, axis=1))
    g_dw = lax.dynamic_slice_in_dim(ssum[24:24 + CONV_K], chip * 256, 256, axis=1)
    g_small = jnp.concatenate(
        [ssum[0:24], jnp.concatenate([g_dw.reshape(-1), jnp.zeros((8 * D - CONV_K * 256,), F32)]).reshape(8, D)],
        axis=0)
    d_, m2_, v2_ = _adamw_call(w_ada[0], g_ada, m_w_ada[0], v_w_ada[0], "adamw_ada")
    res["ada"] = tuple(a.reshape(w_ada.shape) for a in (g_ada, d_, m2_, v2_))

    ws = pack_small(b_ada, pre_norm_tm, post_norm_tm, pre_norm_cm, post_norm_cm, b_in, hg_lb_logits, hg_norm,
                    conv_db, conv_ln_g, conv_ln_b, conv_dw)
    ms = pack_small(m_b_ada, m_pre_norm_tm, m_post_norm_tm, m_pre_norm_cm, m_post_norm_cm, m_b_in, m_hg_lb_logits,
                    m_hg_norm, m_conv_db, m_conv_ln_g, m_conv_ln_b, m_conv_dw)
    vs = pack_small(v_b_ada, v_pre_norm_tm, v_post_norm_tm, v_pre_norm_cm, v_post_norm_cm, v_b_in, v_hg_lb_logits,
                    v_hg_norm, v_conv_db, v_conv_ln_g, v_conv_ln_b, v_conv_dw)
    sres = (g_small,) + tuple(_adamw_call(ws, g_small, ms, vs, "adamw_small"))

    def unpack_small(t):
        return {"b_ada": t[0:6].reshape(1, 6 * D), "pre_tm": t[6:7], "post_tm": t[7:8], "pre_cm": t[8:9],
                "post_cm": t[9:10], "b_in": t[10:18].reshape(1, IN_COLS), "logits": t[18:20], "hg_norm": t[20:21],
                "conv_db": t[21:22], "ln_g": t[22:23], "ln_b": t[23:24],
                "conv_dw": t[24:32].reshape(-1)[:CONV_K * 256].reshape(1, CONV_K, 256)}

    order = ["ada", "b_ada", "pre_tm", "post_tm", "pre_cm", "post_cm", "in", "b_in", "logits", "hg_norm", "conv_dw",
             "conv_db", "ln_g", "ln_b", "br_a", "br_b", "out", "ff1", "ff2"]
    outs = [loss, gx.reshape(x.shape)]
    for kind in range(4):
        sm = unpack_small(sres[kind])
        for n in order:
            outs.append(res[n][kind] if n in res else sm[n])
    return tuple(outs)
```

```python
import functools

import jax
import jax.numpy as jnp
from jax import lax
from jax.experimental import pallas as pl
from jax.experimental.pallas import tpu as pltpu

F32, BF16 = jnp.float32, jnp.bfloat16
SDS = jax.ShapeDtypeStruct
BS = pl.BlockSpec
MESH = pl.DeviceIdType.MESH
HI = lax.Precision.HIGHEST

D = 1024
D_FF = 4096
IN_COLS = 8192
HEADS, DK = 8, 128
CHUNK = 128
CONV_K = 31
HALO = 32
SUB = 32
EPS = 1e-6
N_CHIPS, N_DEV = 4, 8
TM = 256
TB = 256
VMEM_LIMIT = 56 * 1024 * 1024

R_IN, R_BR, R_FF = 2048, 256, 1024
PACK_W = R_IN + 3 * R_BR + 2 * R_FF
O_IN, O_FF1, O_FF2, O_BRA, O_BRB, O_OUT = 0, 2048, 3072, 4096, 4352, 4608
SMALL_ROWS = 64

ADAM_LR, ADAM_B1, ADAM_B2, ADAM_EPS, ADAM_WD, ADAM_STEP = 0.001, 0.9, 0.999, 1e-08, 0.01, 10

NN = (((1,), (0,)), ((), ()))
NT = (((1,), (1,)), ((), ()))
TN = (((0,), (0,)), ((), ()))


def _mm(a, b, dims=NN, precision=None):
    return lax.dot_general(a, b, dims, preferred_element_type=F32, precision=precision)


def _sig(v):
    return jax.nn.sigmoid(v)


def _dsilu(v, s):
    return s * (1.0 + v * (1.0 - s))


def _params(*sem):
    return pltpu.CompilerParams(dimension_semantics=sem if sem else None, vmem_limit_bytes=VMEM_LIMIT)


def _rowsum(v):
    return jnp.sum(v, axis=0, keepdims=True)


def _mesh_pos():
    return lax.axis_index("x"), lax.axis_index("y"), lax.axis_index("c")


def _allgather_parts(x_ref, out_ref, send_sems, recv_sems, local_sem):
    m_per = x_ref.shape[0]
    x, y, c = _mesh_pos()
    me, sibling = (x, y, c), (x, y, 1 - c)
    chips = [(1 - x, y), (x, 1 - y), (1 - x, 1 - y)]

    def rows(px, py, pc):
        return out_ref.at[pl.ds((4 * px + 2 * py + pc) * m_per, m_per), :]

    def copy(k, block, to, src=None):
        return pltpu.make_async_remote_copy(
            src_ref=rows(*block) if src is None else src, dst_ref=rows(*block),
            send_sem=send_sems.at[k], recv_sem=recv_sems.at[k], device_id=to, device_id_type=MESH)

    def first():
        return [copy(0, me, sibling, src=x_ref)] + [copy(1 + j, me, (*chip, c), src=x_ref)
                                                    for j, chip in enumerate(chips)]

    def start():
        pltpu.make_async_copy(x_ref, rows(*me), local_sem).start()
        for cp in first():
            cp.start()

    def finish():
        passed = [copy(4 + j, (*chip, c), sibling) for j, chip in enumerate(chips)]
        for j, chip in enumerate(chips):
            copy(1 + j, (*chip, c), me).wait_recv()
            passed[j].start()
        copy(0, sibling, me).wait_recv()
        for j, chip in enumerate(chips):
            copy(4 + j, (*chip, 1 - c), me).wait_recv()
        for cp in first() + passed:
            cp.wait_send()
        pltpu.make_async_copy(x_ref, rows(*me), local_sem).wait()

    return start, finish


def _allgather(x_ref, out_ref, send_sems, recv_sems, local_sem):
    start, finish = _allgather_parts(x_ref, out_ref, send_sems, recv_sems, local_sem)
    start()
    finish()


def _allgather_sems():
    return [pltpu.SemaphoreType.DMA((7,)), pltpu.SemaphoreType.DMA((7,)), pltpu.SemaphoreType.DMA]


def _allgather_call(blk, name, in_vmem, with_sum):
    m_per, n = blk.shape

    def body(x_ref, out_ref, *rest):
        if with_sum:
            sum_ref, send_sems, recv_sems, local_sem = rest
        else:
            send_sems, recv_sems, local_sem = rest
        _allgather(x_ref, out_ref, send_sems, recv_sems, local_sem)
        if with_sum:
            acc = out_ref[0:m_per, :]
            for d in range(1, N_DEV):
                acc = acc + out_ref[d * m_per:(d + 1) * m_per, :]
            sum_ref[...] = acc

    space = pltpu.VMEM if in_vmem else pl.ANY
    out_shape = [SDS((N_DEV * m_per, n), blk.dtype)]
    out_specs = [BS(memory_space=space)]
    if with_sum:
        out_shape.append(SDS((m_per, n), blk.dtype))
        out_specs.append(BS(memory_space=pltpu.VMEM))
    return pl.pallas_call(
        body, name=name, out_shape=out_shape, in_specs=[BS(memory_space=space)], out_specs=out_specs,
        scratch_shapes=[pltpu.SemaphoreType.DMA((7,)), pltpu.SemaphoreType.DMA((7,)), pltpu.SemaphoreType.DMA],
        compiler_params=pltpu.CompilerParams(vmem_limit_bytes=VMEM_LIMIT),
    )(blk)


def _gather_sems(n_ranges):
    return [pltpu.SemaphoreType.DMA((6 * n_ranges,)), pltpu.SemaphoreType.DMA((6 * n_ranges,))]


def _pack_gather(pack_ref, wg_ref, send_sems, recv_sems, ranges):
    x, y, c = _mesh_pos()
    me, sibling = (x, y, c), (x, y, 1 - c)
    chips = [(1 - x, y), (x, 1 - y), (1 - x, 1 - y)]

    def land(r, px, py, pc):
        off, n = ranges[r]
        return wg_ref.at[2 * px + py, pl.ds(off + pc * (n // 2), n // 2), :]

    def mine(r):
        off, n = ranges[r]
        return pack_ref.at[pl.ds(off + c * (n // 2), n // 2), :]

    def copy(r, k, block, to, src=None):
        return pltpu.make_async_remote_copy(
            src_ref=land(r, *block) if src is None else src, dst_ref=land(r, *block),
            send_sem=send_sems.at[6 * r + k], recv_sem=recv_sems.at[6 * r + k], device_id=to, device_id_type=MESH)

    def start():
        for r in range(len(ranges)):
            for j, chip in enumerate(chips):
                copy(r, j, me, (*chip, c), src=mine(r)).start()

    def arrive(j):
        for r in range(len(ranges)):
            copy(r, j, (*chips[j], c), me).wait_recv()
            copy(r, 3 + j, (*chips[j], c), sibling).start()
        for r in range(len(ranges)):
            copy(r, 3 + j, (*chips[j], 1 - c), me).wait_recv()

    def drain():
        for r in range(len(ranges)):
            for j, chip in enumerate(chips):
                copy(r, j, me, (*chip, c), src=mine(r)).wait_send()
                copy(r, 3 + j, (*chip, c), sibling).wait_send()

    def finish():
        for j in range(3):
            arrive(j)
        drain()

    return start, finish, arrive, drain


def _prologue_call(dw_blk, c_blk, w_ada, b_ada):
    wa = w_ada.shape[1]

    def body(dw_ref, c_ref, wa_ref, ba_ref, dwg_ref, ca_ref, modg_ref,
             cg_scr, part_scr, s1, r1, l1, s2, r2, l2, s3, r3, l3):
        start_c, finish_c = _allgather_parts(c_ref, cg_scr, s2, r2, l2)
        start_dw, finish_dw = _allgather_parts(dw_ref, dwg_ref, s1, r1, l1)
        start_mod, finish_mod = _allgather_parts(part_scr, modg_ref, s3, r3, l3)
        start_c()
        start_dw()
        finish_c()
        cv = cg_scr[...]
        ca = cv * _sig(cv)
        ca_ref[...] = ca
        part_scr[...] = _mm(ca.astype(BF16), wa_ref[...]) + ba_ref[...]
        start_mod()
        finish_dw()
        finish_mod()

    vm = BS(memory_space=pltpu.VMEM)
    return pl.pallas_call(
        body, name="prologue_adaln_conv_dw",
        out_shape=(SDS((N_DEV * 8, D), F32), SDS((N_DEV * 8, D), F32), SDS((N_DEV * N_DEV * 8, wa), F32)),
        in_specs=[vm, vm, vm, vm], out_specs=(vm, vm, vm),
        scratch_shapes=[pltpu.VMEM((N_DEV * 8, D), F32), pltpu.VMEM((N_DEV * 8, wa), F32)]
        + _allgather_sems() + _allgather_sems() + _allgather_sems(),
        compiler_params=pltpu.CompilerParams(vmem_limit_bytes=VMEM_LIMIT),
    )(dw_blk, c_blk, w_ada, b_ada)


def _halves_exchange(g_ref, out_ref, send_sems, recv_sems):
    x, y, c = _mesh_pos()

    def copies():
        return [pltpu.make_async_remote_copy(
            src_ref=g_ref.at[k, 1 - c], dst_ref=out_ref.at[k], send_sem=send_sems.at[k], recv_sem=recv_sems.at[k],
            device_id=(x, y, 1 - c), device_id_type=MESH) for k in range(N_CHIPS)]

    def start():
        for cp in copies():
            cp.start()

    def finish():
        for cp in copies():
            cp.wait()

    return start, finish


def _halves_sems():
    return [pltpu.SemaphoreType.DMA((N_CHIPS,)), pltpu.SemaphoreType.DMA((N_CHIPS,))]


def _sibling_halves_call(g, tag):
    _, _, h, n = g.shape

    def body(g_ref, out_ref, send_sems, recv_sems):
        start, finish = _halves_exchange(g_ref, out_ref, send_sems, recv_sems)
        start()
        finish()

    return pl.pallas_call(
        body, name="rs_sibling_halves_" + tag, out_shape=SDS((N_CHIPS, h, n), g.dtype),
        in_specs=[BS(memory_space=pl.ANY)], out_specs=BS(memory_space=pl.ANY),
        scratch_shapes=_halves_sems(),
    )(g)


def _chip_exchange(p_ref, out_ref, send_sems, recv_sems):
    x, y, c = _mesh_pos()
    chips = [(1 - x, y), (x, 1 - y), (1 - x, 1 - y)]

    def copies():
        return [pltpu.make_async_remote_copy(
            src_ref=p_ref.at[2 * cx + cy], dst_ref=out_ref.at[j], send_sem=send_sems.at[j], recv_sem=recv_sems.at[j],
            device_id=(cx, cy, c), device_id_type=MESH) for j, (cx, cy) in enumerate(chips)]

    def start():
        for cp in copies():
            cp.start()

    def finish():
        for cp in copies():
            cp.wait()

    return start, finish


def _exchange_sems():
    return [pltpu.SemaphoreType.DMA((3,)), pltpu.SemaphoreType.DMA((3,))]


def _join_exchange(in_ref, out_ref, send_sems, recv_sems):
    h = in_ref.shape[1]
    q = h // 4
    x, y, c = _mesh_pos()

    def copy(k, half):
        return pltpu.make_async_remote_copy(
            src_ref=in_ref.at[half, pl.ds(k * q, q)], dst_ref=out_ref.at[half, pl.ds(k * q, q)],
            send_sem=send_sems.at[k], recv_sem=recv_sems.at[k],
            device_id=(x, y, 1 - c), device_id_type=MESH)

    def start():
        for k in range(4):
            copy(k, c).start()

    def finish():
        for k in range(4):
            copy(k, c).wait_send()
            copy(k, 1 - c).wait_recv()

    return start, finish


def _join_sems():
    return [pltpu.SemaphoreType.DMA((4,)), pltpu.SemaphoreType.DMA((4,))]


def _sibling_join_call(full, tag):
    def body(in_ref, out_ref, send_sems, recv_sems):
        start, finish = _join_exchange(in_ref, out_ref, send_sems, recv_sems)
        start()
        finish()

    return pl.pallas_call(
        body, name="rs_sibling_join_" + tag, out_shape=SDS(full.shape, full.dtype),
        in_specs=[BS(memory_space=pl.ANY)], out_specs=BS(memory_space=pl.ANY),
        scratch_shapes=_join_sems(), input_output_aliases={0: 0},
    )(full)


def _add_halves_call(g, recv, c_idx, tag):
    _, _, h, n = g.shape
    tr = h // 2

    def body(c_ref, g_ref, r_ref, o_ref):
        o_ref[...] = (g_ref[...].astype(F32) + r_ref[...].astype(F32)).astype(BF16)

    return pl.pallas_call(
        body, name="rs_add_halves_" + tag, out_shape=SDS((N_CHIPS, h, n), BF16),
        grid_spec=pltpu.PrefetchScalarGridSpec(
            num_scalar_prefetch=1, grid=(N_CHIPS, 2),
            in_specs=[BS((None, None, tr, n), lambda k, r, c_ref: (k, c_ref[0], r, 0)),
                      BS((None, tr, n), lambda k, r, c_ref: (k, r, 0))],
            out_specs=BS((None, tr, n), lambda k, r, c_ref: (k, r, 0))),
        compiler_params=_params("arbitrary", "arbitrary"),
    )(c_idx, g, recv)


def _add_chips_call(p, recv, chip_c_idx, tag):
    _, h, n = p.shape
    tr = h // 2

    def body(k_ref, p_ref, r_ref, o_ref):
        acc = p_ref[...].astype(F32)
        for j in range(3):
            acc = acc + r_ref[j].astype(F32)
        o_ref[...] = acc

    return pl.pallas_call(
        body, name="rs_add_chips_" + tag, out_shape=SDS((2, h, n), F32),
        grid_spec=pltpu.PrefetchScalarGridSpec(
            num_scalar_prefetch=1, grid=(2,),
            in_specs=[BS((None, tr, n), lambda r, k_ref: (k_ref[0], r, 0)),
                      BS((3, tr, n), lambda r, k_ref: (0, r, 0))],
            out_specs=BS((None, tr, n), lambda r, k_ref: (k_ref[1], r, 0))),
        compiler_params=_params("arbitrary"),
    )(chip_c_idx, p, recv)


def _load_rows(wg_hbm, w_vmem, sem, off):
    cp = pltpu.make_async_copy(wg_hbm.at[:, pl.ds(off, w_vmem.shape[1]), :], w_vmem, sem)
    cp.start()
    return cp


def _fwd_in_call(x, mod, pre_tm, wg, b_in, pack, order):
    S = x.shape[0]
    tmf = 2 * TM
    nt = S // tmf
    wc = IN_COLS // N_CHIPS

    def body(ord_ref, x_ref, mod_ref, g_ref, w_hbm, b_ref, pack_ref, p_ref, h_hbm, wg_out, w_vmem, h_scr, sem,
             send_sems, recv_sems):
        q, i = pl.program_id(0), pl.program_id(1)
        rows = pl.ds(pl.multiple_of(i * tmf, tmf), tmf)
        start, _, arrive, drain = _pack_gather(pack_ref, wg_out, send_sems, recv_sems, [(O_IN, R_IN)])

        def load_weights():
            cp = pltpu.make_async_copy(wg_out.at[ord_ref[q], pl.ds(O_IN, R_IN), :], w_vmem, sem)
            cp.start()
            cp.wait()

        @pl.when((q == 0) & (i == 0))
        def _():
            start()
            load_weights()

        for j in range(3):
            @pl.when((q == j + 1) & (i == 0))
            def _(j=j):
                arrive(j)
                load_weights()

        @pl.when(q == 0)
        def _():
            xv = x_ref[...]
            r = lax.rsqrt(jnp.mean(xv * xv, axis=-1, keepdims=True) + EPS)
            h = xv * r * g_ref[...] * (1.0 + mod_ref[:, D:2 * D]) + mod_ref[:, 0:D]
            h_scr[rows, :] = h.astype(BF16)

        hb = h_scr[rows, :]
        for k in range(wc // D):
            p_ref[:, k * D:(k + 1) * D] = _mm(hb, w_vmem[k * D:(k + 1) * D, :]) + b_ref[:, k * D:(k + 1) * D]

        @pl.when((q == N_CHIPS - 1) & (i == nt - 1))
        def _():
            cp = pltpu.make_async_copy(h_scr, h_hbm, sem)
            cp.start()
            drain()
            cp.wait()

    hbm = BS(memory_space=pl.ANY)
    return pl.pallas_call(
        body, name="fwd_in", out_shape=(SDS((S, IN_COLS), F32), SDS((S, D), BF16), SDS(wg.shape, wg.dtype)),
        grid_spec=pltpu.PrefetchScalarGridSpec(
            num_scalar_prefetch=1, grid=(N_CHIPS, nt),
            in_specs=[BS((tmf, D), lambda q, i, o: (jnp.where(q == 0, i, nt - 1), 0)),
                      BS((1, 6 * D), lambda q, i, o: (0, 0)),
                      BS((1, D), lambda q, i, o: (0, 0)), hbm, BS((1, wc), lambda q, i, o: (0, o[q])), hbm],
            out_specs=(BS((tmf, wc), lambda q, i, o: (i, o[q])), hbm, hbm),
            scratch_shapes=[pltpu.VMEM((R_IN, D), BF16), pltpu.VMEM((S, D), BF16), pltpu.SemaphoreType.DMA]
            + _gather_sems(1)),
        input_output_aliases={4: 2},
        compiler_params=_params("arbitrary", "arbitrary"),
    )(order, x, mod, pre_tm, wg, b_in, pack)


def _lower_bound(lg_ref):
    l0, l1 = lg_ref[0:1, :], lg_ref[1:2, :]
    mx = jnp.maximum(l0, l1)
    e0, e1 = jnp.exp(l0 - mx), jnp.exp(l1 - mx)
    return e0 / (e0 + e1)


def _tri_masks():
    ri = lax.broadcasted_iota(jnp.int32, (CHUNK, CHUNK), 0)
    ci = lax.broadcasted_iota(jnp.int32, (CHUNK, CHUNK), 1)
    return (ri >= ci).astype(F32), (ci >= ri).astype(F32)


def _cumsum_mm(tri, g):
    tb = tri.astype(BF16)
    hi = g.astype(BF16)
    r1 = g - hi.astype(F32)
    mid = r1.astype(BF16)
    lo = (r1 - mid.astype(F32)).astype(BF16)
    return _mm(tb, hi) + _mm(tb, mid) + _mm(tb, lo)


def _hg_gates(q_r, f_r, lb, tril):
    sq = _sig(q_r)
    q = q_r * sq
    sf = _sig(f_r)
    f = lb + (1.0 - lb) * sf
    k = 1.0 - f
    g = jnp.log(f)
    b = _cumsum_mm(tril, g)
    b_last = _rowsum(g)
    row = lax.broadcasted_iota(jnp.int32, g.shape, 0)
    ref = _rowsum(jnp.where(row < CHUNK // 2, g, 0.0))
    e = jnp.exp(b)
    eq = jnp.exp(jnp.minimum(b - ref, 80.0))
    ek = jnp.exp(jnp.minimum(ref - b, 80.0))
    dd = jnp.exp(b_last - b)
    return dict(sq=sq, q=q, sf=sf, f=f, k=k, e=e, eq=eq, ek=ek, dd=dd, elast=jnp.exp(b_last),
                qe=q * e, qt=q * eq, kt=k * ek, kd=k * dd)


def _hgrn_fwd_call(p, logits, gn, wg, pack):
    S = p.shape[0]
    ncb = TB // CHUNK
    ranges = [(O_FF1, R_FF)]

    def body(q_ref, f_ref, v_ref, og_ref, lg_ref, gn_ref, wg_in, pack_ref, o_ref, oa_ref, st_ref, wg_out,
             st_scr, send_sems, recv_sems):
        start, finish, _, _ = _pack_gather(pack_ref, wg_out, send_sems, recv_sems, ranges)

        @pl.when(pl.program_id(0) == 0)
        def _():
            start()
            st_scr[...] = jnp.zeros_like(st_scr)

        lb = _lower_bound(lg_ref)
        tril, _ = _tri_masks()

        def chunk(ci, carry):
            rows = pl.ds(pl.multiple_of(ci * CHUNK, CHUNK), CHUNK)
            st_ref[ci] = st_scr[...]
            t = _hg_gates(q_ref[rows, :], f_ref[rows, :], lb, tril)
            v = v_ref[rows, :]
            for h in range(HEADS):
                sl = slice(h * DK, (h + 1) * DK)
                stp = st_scr[:, sl]
                vb = v[:, sl].astype(BF16)
                inter = _mm(t["qe"][:, sl].astype(BF16), stp.astype(BF16), NT)
                a = jnp.where(tril > 0.5, _mm(t["qt"][:, sl].astype(BF16), t["kt"][:, sl].astype(BF16), NT), 0.0)
                o = inter + _mm(a.astype(BF16), vb)
                st_scr[:, sl] = stp * t["elast"][:, sl] + _mm(vb, t["kd"][:, sl].astype(BF16), TN)
                oh = o * lax.rsqrt(jnp.mean(o * o, axis=-1, keepdims=True) + EPS)
                og = og_ref[rows, sl]
                o_ref[rows, sl] = o
                oa_ref[rows, sl] = (oh * gn_ref[:, sl] * (og * _sig(og))).astype(BF16)
            return carry

        lax.fori_loop(0, ncb, chunk, 0)

        @pl.when(pl.program_id(0) == S // TB - 1)
        def _():
            finish()

    col = lambda j: BS((TB, D), lambda i, j=j: (i, j))
    hbm = BS(memory_space=pl.ANY)
    return pl.pallas_call(
        body, name="hgrn_fwd", grid=(S // TB,),
        out_shape=(SDS((S, D), F32), SDS((S, D), BF16), SDS((S // CHUNK, DK, D), F32), SDS(wg.shape, wg.dtype)),
        in_specs=[col(0), col(1), col(2), col(3), BS((2, D), lambda i: (0, 0)), BS((1, D), lambda i: (0, 0)),
                  hbm, hbm],
        out_specs=(BS((TB, D), lambda i: (i, 0)), BS((TB, D), lambda i: (i, 0)),
                   BS((ncb, DK, D), lambda i: (i, 0, 0)), hbm),
        scratch_shapes=[pltpu.VMEM((DK, D), F32)] + _gather_sems(len(ranges)),
        input_output_aliases={6: 3},
        compiler_params=_params("arbitrary"),
    )(p, p, p, p, logits, gn, wg, pack)


def _layernorm_stats(uc):
    mu = jnp.mean(uc, axis=-1, keepdims=True)
    xc = uc - mu
    rs = lax.rsqrt(jnp.mean(xc * xc, axis=-1, keepdims=True) + EPS)
    return xc * rs, rs


EXT = HALO + TM + 8


def _fill_shifted(ext, shifted):
    for m in range(1, 8):
        shifted[m - 1] = ext[m:m + HALO + TM, :]


def _window(ext, shifted, s0, n):
    m = s0 % 8
    q = s0 - m
    return ext[q:q + n, :] if m == 0 else shifted[m - 1, q:q + n, :]


def _conv_fwd_call(p, dw, db, ln_g, ln_b, wg, pack):
    S = p.shape[0]
    ranges = [(O_FF2, R_FF), (O_BRA, 3 * R_BR)]

    def body(cv_ref, cg_ref, dw_ref, db_ref, g_ref, b_ref, wg_in, pack_ref, u_ref, uc_ref, cb_ref, wg_out,
             uext, ush, send_sems, recv_sems):
        start, finish, _, _ = _pack_gather(pack_ref, wg_out, send_sems, recv_sems, ranges)

        @pl.when(pl.program_id(0) == 0)
        def _():
            start()
            uext[0:HALO, :] = jnp.zeros((HALO, D), F32)
            uext[HALO + TM:EXT, :] = jnp.zeros((EXT - HALO - TM, D), F32)

        u = cv_ref[...] * _sig(cg_ref[...])
        uext[HALO:HALO + TM, :] = u
        u_ref[...] = u
        _fill_shifted(uext, ush)
        for rb in range(TM // SUB):
            acc = jnp.broadcast_to(db_ref[...], (SUB, D))
            for j in range(CONV_K):
                s0 = HALO - (CONV_K - 1) + j + rb * SUB
                acc = acc + dw_ref[j:j + 1, :] * _window(uext, ush, s0, SUB)
            uc_ref[rb * SUB:(rb + 1) * SUB, :] = acc
            xh, _ = _layernorm_stats(acc)
            ln = xh * g_ref[...] + b_ref[...]
            cb_ref[rb * SUB:(rb + 1) * SUB, :] = (ln * _sig(ln)).astype(BF16)
        uext[0:HALO, :] = uext[TM:TM + HALO, :]

        @pl.when(pl.program_id(0) == S // TM - 1)
        def _():
            finish()

    vec = BS((1, D), lambda i: (0, 0))
    hbm = BS(memory_space=pl.ANY)
    return pl.pallas_call(
        body, name="conv_fwd", grid=(S // TM,),
        out_shape=(SDS((S, D), F32), SDS((S, D), F32), SDS((S, D), BF16), SDS(wg.shape, wg.dtype)),
        in_specs=[BS((TM, D), lambda i: (i, 4)), BS((TM, D), lambda i: (i, 5)),
                  BS((CONV_K, D), lambda i: (0, 0)), vec, vec, vec, hbm, hbm],
        out_specs=(BS((TM, D), lambda i: (i, 0)),) * 3 + (hbm,),
        scratch_shapes=[pltpu.VMEM((EXT, D), F32), pltpu.VMEM((7, HALO + TM, D), F32)] + _gather_sems(len(ranges)),
        input_output_aliases={6: 3},
        compiler_params=_params("arbitrary"),
    )(p, p, dw, db, ln_g, ln_b, wg, pack)


def _mm_rows(a, w_ref):
    acc = _mm(a[:, 0:R_BR], w_ref[0])
    for k in range(1, N_CHIPS):
        acc = acc + _mm(a[:, k * R_BR:(k + 1) * R_BR], w_ref[k])
    return acc


def _mm_rows_t(a, w_ref):
    return jnp.concatenate([_mm(a, w_ref[k], NT) for k in range(N_CHIPS)], axis=1)


def _br_spec(off):
    return BS((N_CHIPS, R_BR, D), lambda i: (0, off // R_BR, 0))


def _merge_fwd_call(oa, cb, p, x, mod, post_tm, pre_cm, wg):
    S = x.shape[0]

    def body(oa_ref, cb_ref, ga_ref, gb_ref, x_ref, mod_ref, post_ref, pre_ref, wa_ref, wb_ref, wo_ref,
             ya_ref, yb_ref, mg_ref, y_ref, x2_ref, h2_ref):
        ya = _mm_rows(oa_ref[...], wa_ref)
        yb = _mm_rows(cb_ref[...], wb_ref)
        ya_ref[...] = ya.astype(BF16)
        yb_ref[...] = yb.astype(BF16)
        mg = (_sig(ga_ref[...]) * ya + _sig(gb_ref[...]) * yb).astype(BF16)
        mg_ref[...] = mg
        y = _mm_rows(mg, wo_ref)
        y_ref[...] = y
        n = y * lax.rsqrt(jnp.mean(y * y, axis=-1, keepdims=True) + EPS) * post_ref[...]
        x2 = x_ref[...] + mod_ref[:, 2 * D:3 * D] * n
        x2_ref[...] = x2
        r2 = lax.rsqrt(jnp.mean(x2 * x2, axis=-1, keepdims=True) + EPS)
        h2 = x2 * r2 * pre_ref[...] * (1.0 + mod_ref[:, 4 * D:5 * D]) + mod_ref[:, 3 * D:4 * D]
        h2_ref[...] = h2.astype(BF16)

    tile = BS((TM, D), lambda i: (i, 0))
    vec = BS((1, D), lambda i: (0, 0))
    return pl.pallas_call(
        body, name="merge_fwd", grid=(S // TM,),
        out_shape=(SDS((S, D), BF16), SDS((S, D), BF16), SDS((S, D), BF16), SDS((S, D), F32), SDS((S, D), F32),
                   SDS((S, D), BF16)),
        in_specs=[tile, tile, BS((TM, D), lambda i: (i, 6)), BS((TM, D), lambda i: (i, 7)), tile,
                  BS((1, 6 * D), lambda i: (0, 0)), vec, vec, _br_spec(O_BRA), _br_spec(O_BRB), _br_spec(O_OUT)],
        out_specs=(tile,) * 6,
        compiler_params=_params("arbitrary"),
    )(oa, cb, p, p, x, mod, post_tm, pre_cm, wg, wg, wg)


def _ffn_call(h2, x2, target, mod, post_cm, pre_cm, wg):
    S = x2.shape[0]

    def body(h2_ref, x2_ref, t_ref, mod_ref, post_ref, pre_ref, w_hbm,
             z_ref, da_ref, dy2_ref, dx2_ref, acc_ref, w1_v, w2_v, ra_scr, sems):
        @pl.when(pl.program_id(0) == 0)
        def _():
            c1 = _load_rows(w_hbm, w1_v, sems.at[0], O_FF1)
            c2 = _load_rows(w_hbm, w2_v, sems.at[1], O_FF2)
            c1.wait()
            c2.wait()
            acc_ref[...] = jnp.zeros_like(acc_ref)

        h2 = h2_ref[...]
        for k in range(N_CHIPS):
            ra = jnp.maximum(_mm(h2, w1_v[k]), 0.0)
            ra_scr[:, k * D:(k + 1) * D] = ra
            z_ref[:, k * D:(k + 1) * D] = (ra * ra).astype(BF16)
        y2 = _mm(z_ref[:, 0:D], w2_v[0])
        for k in range(1, N_CHIPS):
            y2 = y2 + _mm(z_ref[:, k * D:(k + 1) * D], w2_v[k])
        ry = lax.rsqrt(jnp.mean(y2 * y2, axis=-1, keepdims=True) + EPS)
        yn = y2 * ry
        n = yn * post_ref[...]
        g2 = mod_ref[:, 5 * D:6 * D]
        x2 = x2_ref[...]
        err = x2 + g2 * n - t_ref[...]
        acc_ref[5:6, :] += _rowsum(err * err) * (0.5 / D)
        dout = err * (1.0 / D)
        acc_ref[0:1, :] += _rowsum(dout * n)
        dn = dout * g2
        acc_ref[1:2, :] += _rowsum(dn * yn)
        dyn = dn * post_ref[...]
        dy2 = (ry * (dyn - yn * jnp.mean(dyn * yn, axis=-1, keepdims=True))).astype(BF16)
        dy2_ref[...] = dy2
        for k in range(N_CHIPS):
            dz = _mm(dy2, w2_v[k], NT)
            da_ref[:, k * D:(k + 1) * D] = (dz * (2.0 * ra_scr[:, k * D:(k + 1) * D])).astype(BF16)
        dh2 = jnp.zeros((TM, D), F32)
        for k in range(N_CHIPS):
            dh2 = dh2 + _mm(da_ref[:, k * D:(k + 1) * D], w1_v[k], NT)
        r2 = lax.rsqrt(jnp.mean(x2 * x2, axis=-1, keepdims=True) + EPS)
        xn = x2 * r2
        yv = xn * pre_ref[...]
        acc_ref[2:3, :] += _rowsum(dh2)
        acc_ref[3:4, :] += _rowsum(dh2 * yv)
        dyv = dh2 * (1.0 + mod_ref[:, 4 * D:5 * D])
        acc_ref[4:5, :] += _rowsum(dyv * xn)
        dxn = dyv * pre_ref[...]
        dx2_ref[...] = dout + r2 * (dxn - xn * jnp.mean(dxn * xn, axis=-1, keepdims=True))

    tile = BS((TM, D), lambda i: (i, 0))
    wide = BS((TM, D_FF), lambda i: (i, 0))
    vec = BS((1, D), lambda i: (0, 0))
    return pl.pallas_call(
        body, name="ffn_fwd_bwd", grid=(S // TM,),
        out_shape=(SDS((S, D_FF), BF16), SDS((S, D_FF), BF16), SDS((S, D), BF16), SDS((S, D), F32),
                   SDS((8, D), F32)),
        in_specs=[tile, tile, tile, BS((1, 6 * D), lambda i: (0, 0)), vec, vec, BS(memory_space=pl.ANY)],
        out_specs=(wide, wide, tile, tile, BS((8, D), lambda i: (0, 0))),
        scratch_shapes=[pltpu.VMEM((N_CHIPS, R_FF, D), BF16), pltpu.VMEM((N_CHIPS, R_FF, D), BF16),
                        pltpu.VMEM((TM, D_FF), F32),
                        pltpu.SemaphoreType.DMA((2,))],
        compiler_params=_params("arbitrary"),
    )(h2, x2, target, mod, post_cm, pre_cm, wg)


def _merge_bwd_call(dx2, y, ya, yb, p, mod, post_tm, wg, g):
    S = y.shape[0]

    def body(dx2_ref, y_ref, ya_ref, yb_ref, ga_ref, gb_ref, mod_ref, post_ref, wa_ref, wb_ref, wo_ref, g_ref,
             dy_ref, dya_ref, dyb_ref, doa_ref, dcb_ref, dpg_ref, acc_ref, bsum_ref, hr_ref, send_sems, recv_sems):
        start, finish = _halves_exchange(g_ref, hr_ref, send_sems, recv_sems)

        @pl.when(pl.program_id(0) == 0)
        def _():
            start()
            acc_ref[...] = jnp.zeros_like(acc_ref)
            bsum_ref[...] = jnp.zeros_like(bsum_ref)

        y = y_ref[...]
        ry = lax.rsqrt(jnp.mean(y * y, axis=-1, keepdims=True) + EPS)
        yn = y * ry
        dx2 = dx2_ref[...]
        acc_ref[0:1, :] += _rowsum(dx2 * (yn * post_ref[...]))
        dn = dx2 * mod_ref[:, 2 * D:3 * D]
        acc_ref[1:2, :] += _rowsum(dn * yn)
        dyn = dn * post_ref[...]
        dy = (ry * (dyn - yn * jnp.mean(dyn * yn, axis=-1, keepdims=True))).astype(BF16)
        dy_ref[...] = dy
        dmg = _mm_rows_t(dy, wo_ref)
        sa, sb = _sig(ga_ref[...]), _sig(gb_ref[...])
        dya = (dmg * sa).astype(BF16)
        dyb = (dmg * sb).astype(BF16)
        dya_ref[...] = dya
        dyb_ref[...] = dyb
        dga = dmg * ya_ref[...].astype(F32) * (sa * (1.0 - sa))
        dgb = dmg * yb_ref[...].astype(F32) * (sb * (1.0 - sb))
        dpg_ref[:, 0:D] = dga.astype(BF16)
        dpg_ref[:, D:2 * D] = dgb.astype(BF16)
        bsum_ref[:, 0:D] += _rowsum(dga)
        bsum_ref[:, D:2 * D] += _rowsum(dgb)
        doa_ref[...] = _mm_rows_t(dya, wa_ref)
        dcb_ref[...] = _mm_rows_t(dyb, wb_ref)

        @pl.when(pl.program_id(0) == S // TM - 1)
        def _():
            finish()

    tile = BS((TM, D), lambda i: (i, 0))
    vec = BS((1, D), lambda i: (0, 0))
    return pl.pallas_call(
        body, name="merge_bwd", grid=(S // TM,),
        out_shape=(SDS((S, D), BF16), SDS((S, D), BF16), SDS((S, D), BF16), SDS((S, D), F32), SDS((S, D), F32),
                   SDS((S, 2 * D), BF16), SDS((8, D), F32), SDS((1, 2 * D), F32),
                   SDS((N_CHIPS,) + g.shape[2:], g.dtype)),
        in_specs=[tile, tile, tile, tile, BS((TM, D), lambda i: (i, 6)), BS((TM, D), lambda i: (i, 7)),
                  BS((1, 6 * D), lambda i: (0, 0)), vec, _br_spec(O_BRA), _br_spec(O_BRB), _br_spec(O_OUT),
                  BS(memory_space=pl.ANY)],
        out_specs=(tile, tile, tile, tile, tile, BS((TM, 2 * D), lambda i: (i, 0)),
                   BS((8, D), lambda i: (0, 0)), BS((1, 2 * D), lambda i: (0, 0)), BS(memory_space=pl.ANY)),
        scratch_shapes=_halves_sems(),
        compiler_params=_params("arbitrary"),
    )(dx2, y, ya, yb, p, p, mod, post_tm, wg, wg, wg, g)


def _hgrn_bwd_call(p, o, doa, st, logits, gn, part, g):
    S = p.shape[0]
    nb = S // TB
    ncb = TB // CHUNK

    def body(q_ref, f_ref, v_ref, og_ref, o_ref, doa_ref, st_ref, lg_ref, gn_ref, part_ref, g_ref,
             dp_ref, bsum_ref, dlg_ref, dgn_ref, recv_ref, hr_ref,
             dst_scr, dlb_scr, dqe_s, dqt_s, dkt_s, dkd_s, dv_s, dog_s, dble_s, send_sems, recv_sems, hs, hr):
        i = pl.program_id(0)
        start, finish = _chip_exchange(part_ref, recv_ref, send_sems, recv_sems)
        start_h, finish_h = _halves_exchange(g_ref, hr_ref, hs, hr)

        @pl.when(i == 0)
        def _():
            start_h()
            start()
            dst_scr[...] = jnp.zeros_like(dst_scr)
            dlb_scr[...] = jnp.zeros_like(dlb_scr)
            bsum_ref[...] = jnp.zeros_like(bsum_ref)
            dgn_ref[...] = jnp.zeros_like(dgn_ref)

        lb = _lower_bound(lg_ref)
        tril, triu = _tri_masks()

        def chunk(tt, carry):
            ci = ncb - 1 - tt
            rows = pl.ds(pl.multiple_of(ci * CHUNK, CHUNK), CHUNK)
            q_r, f_r = q_ref[rows, :], f_ref[rows, :]
            t = _hg_gates(q_r, f_r, lb, tril)
            v = v_ref[rows, :]
            for h in range(HEADS):
                sl = slice(h * DK, (h + 1) * DK)
                stp = st_ref[ci, :, sl]
                stb = stp.astype(BF16)
                qeb = t["qe"][:, sl].astype(BF16)
                qtb = t["qt"][:, sl].astype(BF16)
                ktb = t["kt"][:, sl].astype(BF16)
                kdb = t["kd"][:, sl].astype(BF16)
                vb = v[:, sl].astype(BF16)
                a = jnp.where(tril > 0.5, _mm(qtb, ktb, NT), 0.0)
                o_h = o_ref[rows, sl]
                rinv = lax.rsqrt(jnp.mean(o_h * o_h, axis=-1, keepdims=True) + EPS)
                oh = o_h * rinv
                og = og_ref[rows, sl]
                so = _sig(og)
                d_oa = doa_ref[rows, sl]
                don = d_oa * (og * so)
                dog_s[:, sl] = d_oa * (oh * gn_ref[:, sl]) * _dsilu(og, so)
                dgn_ref[:, sl] += _rowsum(don * oh)
                doh = don * gn_ref[:, sl]
                do = (rinv * (doh - oh * jnp.mean(doh * oh, axis=-1, keepdims=True))).astype(BF16)
                dqe_s[:, sl] = _mm(do, stb, NN)
                dstp = _mm(do, qeb, TN)
                dab = jnp.where(tril > 0.5, _mm(do, vb, NT), 0.0).astype(BF16)
                dqt_s[:, sl] = _mm(dab, ktb, NN)
                dkt_s[:, sl] = _mm(dab, qtb, TN)
                dstn = dst_scr[:, sl]
                dsb = dstn.astype(BF16)
                dkd_s[:, sl] = _mm(vb, dsb, NN)
                dv_s[:, sl] = _mm(a.astype(BF16), do, TN) + _mm(kdb, dsb, NT)
                el = t["elast"][:, sl]
                dst_scr[:, sl] = dstn * el + dstp
                dble_s[:, sl] = el * _rowsum(stp * dstn)
            dqe, dqt, dkt, dkd = dqe_s[...], dqt_s[...], dkt_s[...], dkd_s[...]
            dq = dqe * t["e"] + dqt * t["eq"]
            dk = dkt * t["ek"] + dkd * t["dd"]
            dkk = dkd * t["kd"]
            qt_r = t["qt"].astype(BF16).astype(F32)
            kt_r = t["kt"].astype(BF16).astype(F32)
            dbv = dqe * t["qe"] + dqt * qt_r - dkt * kt_r - dkk
            dg = _cumsum_mm(triu, dbv) + (_rowsum(dkk) + dble_s[...])
            df = dg / t["f"] - dk
            sf = t["sf"]
            dlb_scr[...] += _rowsum(df * (1.0 - sf))
            dqr = dq * _dsilu(q_r, t["sq"])
            dfr = df * (1.0 - lb) * (sf * (1.0 - sf))
            dvv, dog = dv_s[...], dog_s[...]
            dp_ref[rows, 0:D] = dqr.astype(BF16)
            dp_ref[rows, D:2 * D] = dfr.astype(BF16)
            dp_ref[rows, 2 * D:3 * D] = dvv.astype(BF16)
            dp_ref[rows, 3 * D:4 * D] = dog.astype(BF16)
            bsum_ref[:, 0:D] += _rowsum(dqr)
            bsum_ref[:, D:2 * D] += _rowsum(dfr)
            bsum_ref[:, 2 * D:3 * D] += _rowsum(dvv)
            bsum_ref[:, 3 * D:4 * D] += _rowsum(dog)
            return carry

        lax.fori_loop(0, ncb, chunk, 0)

        dl = dlb_scr[...] * lb * (1.0 - lb)
        dlg_ref[0:1, :] = dl
        dlg_ref[1:2, :] = -dl

        @pl.when(i == nb - 1)
        def _():
            finish_h()
            finish()

    col = lambda j: BS((TB, D), lambda i, j=j: (nb - 1 - i, j))
    rev = BS((TB, D), lambda i: (nb - 1 - i, 0))
    cd = pltpu.VMEM((CHUNK, D), F32)
    return pl.pallas_call(
        body, name="hgrn_bwd", grid=(nb,),
        out_shape=(SDS((S, 4 * D), BF16), SDS((1, 4 * D), F32), SDS((2, D), F32), SDS((1, D), F32),
                   SDS((3,) + part.shape[1:], part.dtype), SDS((N_CHIPS,) + g.shape[2:], g.dtype)),
        in_specs=[col(0), col(1), col(2), col(3), rev, rev, BS((ncb, DK, D), lambda i: (nb - 1 - i, 0, 0)),
                  BS((2, D), lambda i: (0, 0)), BS((1, D), lambda i: (0, 0)), BS(memory_space=pl.ANY),
                  BS(memory_space=pl.ANY)],
        out_specs=(BS((TB, 4 * D), lambda i: (nb - 1 - i, 0)), BS((1, 4 * D), lambda i: (0, 0)),
                   BS((2, D), lambda i: (0, 0)), BS((1, D), lambda i: (0, 0)), BS(memory_space=pl.ANY),
                   BS(memory_space=pl.ANY)),
        scratch_shapes=[pltpu.VMEM((DK, D), F32), pltpu.VMEM((1, D), F32), cd, cd, cd, cd, cd, cd,
                        pltpu.VMEM((1, D), F32)] + _exchange_sems() + _halves_sems(),
        compiler_params=_params("arbitrary"),
    )(p, p, p, p, o, doa, st, logits, gn, part, g)


def _conv_bwd_call(dcb, uc, u, p, dw, ln_g, ln_b, part):
    S = uc.shape[0]
    nb = S // TM
    hb = TM // HALO

    def body(dcb_ref, uc_ref, u_ref, uh_ref, cv_ref, cg_ref, dw_ref, g_ref, b_ref, part_ref,
             dp_ref, bsum_ref, ddw_ref, acc_ref, recv_ref, uext, dext, ush, dsh, send_sems, recv_sems):
        i = pl.program_id(0)
        start, finish = _chip_exchange(part_ref, recv_ref, send_sems, recv_sems)

        @pl.when(i == 0)
        def _():
            start()
            dext[TM:EXT, :] = jnp.zeros((EXT - TM, D), F32)
            uext[HALO + TM:EXT, :] = jnp.zeros((EXT - HALO - TM, D), F32)
            bsum_ref[...] = jnp.zeros_like(bsum_ref)
            ddw_ref[...] = jnp.zeros_like(ddw_ref)
            acc_ref[...] = jnp.zeros_like(acc_ref)

        first_tile = (nb - 1 - i) == 0
        uext[0:HALO, :] = jnp.where(first_tile, 0.0, uh_ref[...])
        uext[HALO:HALO + TM, :] = u_ref[...]
        _fill_shifted(uext, ush)

        for rb in range(TM // SUB):
            rs_ = slice(rb * SUB, (rb + 1) * SUB)
            xh, rs = _layernorm_stats(uc_ref[rs_, :])
            ln = xh * g_ref[...] + b_ref[...]
            dln = dcb_ref[rs_, :] * _dsilu(ln, _sig(ln))
            acc_ref[1:2, :] += _rowsum(dln * xh)
            acc_ref[2:3, :] += _rowsum(dln)
            dxh = dln * g_ref[...]
            duc = rs * (dxh - jnp.mean(dxh, axis=-1, keepdims=True)
                        - xh * jnp.mean(dxh * xh, axis=-1, keepdims=True))
            dext[rs_, :] = duc
            acc_ref[0:1, :] += _rowsum(duc)
        _fill_shifted(dext, dsh)

        for j in range(CONV_K):
            part = jnp.zeros((SUB, D), F32)
            for rb in range(TM // SUB):
                s0 = HALO - (CONV_K - 1) + j + rb * SUB
                part = part + dext[rb * SUB:(rb + 1) * SUB, :] * _window(uext, ush, s0, SUB)
            ddw_ref[j:j + 1, :] += _rowsum(part)

        for rb in range(TM // SUB):
            rs_ = slice(rb * SUB, (rb + 1) * SUB)
            du = jnp.zeros((SUB, D), F32)
            for j in range(CONV_K):
                s0 = rb * SUB + (CONV_K - 1) - j
                du = du + dw_ref[j:j + 1, :] * _window(dext, dsh, s0, SUB)
            cg = cg_ref[rs_, :]
            sg = _sig(cg)
            dcv = du * sg
            dcg = du * cv_ref[rs_, :] * (sg * (1.0 - sg))
            dp_ref[rs_, 0:D] = dcv.astype(BF16)
            dp_ref[rs_, D:2 * D] = dcg.astype(BF16)
            bsum_ref[:, 0:D] += _rowsum(dcv)
            bsum_ref[:, D:2 * D] += _rowsum(dcg)

        dext[TM:TM + HALO, :] = dext[0:HALO, :]

        @pl.when(i == nb - 1)
        def _():
            finish()

    rev = BS((TM, D), lambda i: (nb - 1 - i, 0))
    vec = BS((1, D), lambda i: (0, 0))
    return pl.pallas_call(
        body, name="conv_bwd", grid=(nb,),
        out_shape=(SDS((S, 2 * D), BF16), SDS((1, 2 * D), F32), SDS((32, D), F32), SDS((8, D), F32),
                   SDS((3,) + part.shape[1:], part.dtype)),
        in_specs=[rev, rev, rev, BS((HALO, D), lambda i: (jnp.maximum((nb - 1 - i) * hb - 1, 0), 0)),
                  BS((TM, D), lambda i: (nb - 1 - i, 4)), BS((TM, D), lambda i: (nb - 1 - i, 5)),
                  BS((CONV_K, D), lambda i: (0, 0)), vec, vec, BS(memory_space=pl.ANY)],
        out_specs=(BS((TM, 2 * D), lambda i: (nb - 1 - i, 0)), BS((1, 2 * D), lambda i: (0, 0)),
                   BS((32, D), lambda i: (0, 0)), BS((8, D), lambda i: (0, 0)), BS(memory_space=pl.ANY)),
        scratch_shapes=[pltpu.VMEM((EXT, D), F32), pltpu.VMEM((EXT, D), F32),
                        pltpu.VMEM((7, HALO + TM, D), F32), pltpu.VMEM((7, HALO + TM, D), F32)] + _exchange_sems(),
        compiler_params=_params("arbitrary"),
    )(dcb, uc, u, u, p, p, dw, ln_g, ln_b, part)


def _in_bwd_call(dp_hg, dp_cv, dp_gt, x, dx2, mod, pre_tm, wg, part, full_a, full_b):
    S = x.shape[0]

    def body(hg_ref, cv_ref, gt_ref, x_ref, dx2_ref, mod_ref, g_ref, w_hbm, part_ref, fa_in, fb_in,
             gx_ref, acc_ref, recv_ref, fa_out, fb_out, w_vmem, sem, send_sems, recv_sems, sa, ra, sb, rb):
        start, finish = _chip_exchange(part_ref, recv_ref, send_sems, recv_sems)
        start_a, finish_a = _join_exchange(fa_in, fa_out, sa, ra)
        start_b, finish_b = _join_exchange(fb_in, fb_out, sb, rb)

        @pl.when(pl.program_id(0) == 0)
        def _():
            start_a()
            start_b()
            start()
            _load_rows(w_hbm, w_vmem, sem, O_IN).wait()
            acc_ref[...] = jnp.zeros_like(acc_ref)

        dh = jnp.zeros((TM, D), F32)
        for k in range(IN_COLS // D):
            src, kk = ((hg_ref, k), (cv_ref, k - 4), (gt_ref, k - 6))[0 if k < 4 else (1 if k < 6 else 2)]
            dh = dh + _mm(src[:, kk * D:(kk + 1) * D], w_vmem[k // 2, (k % 2) * D:(k % 2 + 1) * D, :], NT)
        xv = x_ref[...]
        r = lax.rsqrt(jnp.mean(xv * xv, axis=-1, keepdims=True) + EPS)
        xn = xv * r
        yv = xn * g_ref[...]
        acc_ref[0:1, :] += _rowsum(dh)
        acc_ref[1:2, :] += _rowsum(dh * yv)
        dyv = dh * (1.0 + mod_ref[:, D:2 * D])
        acc_ref[2:3, :] += _rowsum(dyv * xn)
        dxn = dyv * g_ref[...]
        gx_ref[...] = dx2_ref[...] + r * (dxn - xn * jnp.mean(dxn * xn, axis=-1, keepdims=True))

        @pl.when(pl.program_id(0) == S // TM - 1)
        def _():
            finish_a()
            finish_b()
            finish()

    tile = BS((TM, D), lambda i: (i, 0))
    hbm = BS(memory_space=pl.ANY)
    return pl.pallas_call(
        body, name="in_bwd", grid=(S // TM,),
        out_shape=(SDS((S, D), F32), SDS((8, D), F32), SDS((3,) + part.shape[1:], part.dtype),
                   SDS(full_a.shape, full_a.dtype), SDS(full_b.shape, full_b.dtype)),
        in_specs=[BS((TM, 4 * D), lambda i: (i, 0)), BS((TM, 2 * D), lambda i: (i, 0)),
                  BS((TM, 2 * D), lambda i: (i, 0)), tile, tile, BS((1, 6 * D), lambda i: (0, 0)),
                  BS((1, D), lambda i: (0, 0)), hbm, hbm, hbm, hbm],
        out_specs=(tile, BS((8, D), lambda i: (0, 0)), hbm, hbm, hbm),
        scratch_shapes=[pltpu.VMEM((N_CHIPS, R_IN, D), BF16), pltpu.SemaphoreType.DMA] + _exchange_sems()
        + _join_sems() + _join_sems(),
        input_output_aliases={9: 3, 10: 4},
        compiler_params=_params("arbitrary"),
    )(dp_hg, dp_cv, dp_gt, x, dx2, mod, pre_tm, wg, part, full_a, full_b)


def _wgrad_call(gp, a, b, name, bm, place, rows):
    S, M = a.shape
    N = b.shape[1]
    bk = min(S, 1024)
    nk = S // bk

    def body(a_ref, b_ref, *rest):
        o_ref, acc = rest[-2], rest[-1]
        k = pl.program_id(2)

        @pl.when(k == 0)
        def _():
            acc[...] = jnp.zeros_like(acc)

        acc[...] += _mm(a_ref[...], b_ref[...], TN)

        @pl.when(k == nk - 1)
        def _():
            o_ref[...] = acc[...].astype(BF16)

    in_specs = [BS((bk, bm), lambda i, j, k: (k, i)), BS((bk, D), lambda i, j, k: (k, j))]
    args = [a, b]
    if gp is not None:
        in_specs.append(BS(memory_space=pl.ANY))
        args.append(gp)
    return pl.pallas_call(
        body, name=name, grid=(M // bm, N // D, nk),
        out_shape=SDS((N_CHIPS, rows, D), BF16),
        in_specs=in_specs,
        out_specs=BS((None, bm, D), lambda i, j, k: (*place(i, j), 0)),
        scratch_shapes=[pltpu.VMEM((bm, D), F32)],
        input_output_aliases={} if gp is None else {2: 0},
        compiler_params=_params("parallel", "parallel", "arbitrary"),
    )(*args)


def _wgrad_rows_call(gp, a, b, name, blk):
    S = a.shape[0]
    bk = min(S, 1024)
    nk = S // bk

    def body(a_ref, b_ref, *rest):
        o_ref, acc = rest[-2], rest[-1]
        k = pl.program_id(0)

        @pl.when(k == 0)
        def _():
            acc[...] = jnp.zeros_like(acc)

        acc[...] += _mm(a_ref[...], b_ref[...], TN)

        @pl.when(k == nk - 1)
        def _():
            for c in range(N_CHIPS):
                o_ref[c] = acc[c * R_BR:(c + 1) * R_BR, :].astype(BF16)

    in_specs = [BS((bk, D), lambda k: (k, 0)), BS((bk, D), lambda k: (k, 0))]
    args = [a, b]
    if gp is not None:
        in_specs.append(BS(memory_space=pl.ANY))
        args.append(gp)
    return pl.pallas_call(
        body, name=name, grid=(nk,),
        out_shape=SDS((N_CHIPS, 3 * R_BR, D), BF16),
        in_specs=in_specs,
        out_specs=BS((N_CHIPS, R_BR, D), lambda k: (0, blk, 0)),
        scratch_shapes=[pltpu.VMEM((D, D), F32)],
        input_output_aliases={} if gp is None else {2: 0},
        compiler_params=_params("arbitrary"),
    )(*args)


def _outer_call(cact, dmod):
    n = dmod.shape[1]

    def body(a_ref, b_ref, o_ref):
        o_ref[...] = _mm(a_ref[...], b_ref[...], TN, HI)

    return pl.pallas_call(
        body, name="wgrad_ada", out_shape=SDS((D, n), F32),
        compiler_params=pltpu.CompilerParams(vmem_limit_bytes=VMEM_LIMIT),
    )(cact, dmod)


def _adamw_call(w, g, m, v, name):
    R, C = w.shape
    tr = R
    while tr * C > 512 * 1024 and tr % 16 == 0:
        tr //= 2
    c1 = 1.0 - ADAM_B1 ** ADAM_STEP
    c2 = 1.0 - ADAM_B2 ** ADAM_STEP

    def body(w_ref, g_ref, m_ref, v_ref, d_ref, m2_ref, v2_ref):
        g = g_ref[...]
        m2 = ADAM_B1 * m_ref[...] + (1.0 - ADAM_B1) * g
        v2 = ADAM_B2 * v_ref[...] + (1.0 - ADAM_B2) * (g * g)
        m2_ref[...] = m2
        v2_ref[...] = v2
        d_ref[...] = -ADAM_LR * ((m2 / c1) / (jnp.sqrt(v2 / c2) + ADAM_EPS) + ADAM_WD * w_ref[...])

    tile = BS((tr, C), lambda i: (i, 0))
    return pl.pallas_call(
        body, name=name, grid=(R // tr,), out_shape=(SDS((R, C), F32),) * 3,
        in_specs=[tile] * 4, out_specs=(tile,) * 3, compiler_params=_params("parallel"),
    )(w, g, m, v)


def _adamw_gather_call(w, g, m, v, srows, name):
    R, C = w.shape
    tr = R
    while tr * C > 512 * 1024 and tr % 16 == 0:
        tr //= 2
    nsteps = R // tr
    mr = srows.shape[0]
    c1 = 1.0 - ADAM_B1 ** ADAM_STEP
    c2 = 1.0 - ADAM_B2 ** ADAM_STEP

    def body(w_ref, g_ref, m_ref, v_ref, s_ref, d_ref, m2_ref, v2_ref, all_ref, sum_ref,
             x_scr, out_scr, send_sems, recv_sems, local_sem):
        i = pl.program_id(0)
        start, finish = _allgather_parts(x_scr, out_scr, send_sems, recv_sems, local_sem)

        @pl.when(i == 0)
        def _():
            x_scr[...] = s_ref[...]
            start()

        g = g_ref[...]
        m2 = ADAM_B1 * m_ref[...] + (1.0 - ADAM_B1) * g
        v2 = ADAM_B2 * v_ref[...] + (1.0 - ADAM_B2) * (g * g)
        m2_ref[...] = m2
        v2_ref[...] = v2
        d_ref[...] = -ADAM_LR * ((m2 / c1) / (jnp.sqrt(v2 / c2) + ADAM_EPS) + ADAM_WD * w_ref[...])

        @pl.when(i == nsteps - 1)
        def _():
            finish()
            all_ref[...] = out_scr[...]
            acc = out_scr[0:mr, :]
            for d in range(1, N_DEV):
                acc = acc + out_scr[d * mr:(d + 1) * mr, :]
            sum_ref[...] = acc

    tile = BS((tr, C), lambda i: (i, 0))
    return pl.pallas_call(
        body, name=name, grid=(nsteps,),
        out_shape=(SDS((R, C), F32),) * 3 + (SDS((N_DEV * mr, D), F32), SDS((mr, D), F32)),
        in_specs=[tile] * 4 + [BS((mr, D), lambda i: (0, 0))],
        out_specs=(tile,) * 3 + (BS((N_DEV * mr, D), lambda i: (0, 0)), BS((mr, D), lambda i: (0, 0))),
        scratch_shapes=[pltpu.VMEM((mr, D), F32), pltpu.VMEM((N_DEV * mr, D), F32)] + _allgather_sems(),
        compiler_params=_params("arbitrary"),
    )(w, g, m, v, srows)


def _rs_begin(g, c_idx, tag):
    n = g.shape[1]
    g = g.reshape(N_CHIPS, 2, n // 2, D)
    return _add_halves_call(g, _sibling_halves_call(g, tag), c_idx, tag)


def _rs_end(part, recv, c_idx, chip_idx, tag):
    n = 2 * part.shape[1]
    full = _add_chips_call(part, recv, jnp.concatenate([chip_idx, c_idx]), tag)
    return _sibling_join_call(full, tag).reshape(n, D)


def _local_step(x, mod, cact, target, wg, pack, small, c_idx, chip_idx):
    p, h1, wg = _fwd_in_call(x, mod, small["pre_tm"], wg, small["b_in"], pack, small["order"])
    o, oa, st, wg = _hgrn_fwd_call(p, small["logits"], small["hg_norm"], wg, pack)
    u, uc, cb, wg = _conv_fwd_call(p, small["conv_dw"], small["conv_db"], small["ln_g"], small["ln_b"], wg, pack)
    ya, yb, mg, y, x2, h2 = _merge_fwd_call(oa, cb, p, x, mod, small["post_tm"], small["pre_cm"], wg)
    z, da, dy2, dx2, acc_f = _ffn_call(h2, x2, target, mod, small["post_cm"], small["pre_cm"], wg)

    g_ff = _wgrad_call(None, h2, da, "wgrad_ff1", D, lambda i, j: (j, 0), 2 * R_FF)
    g_ff = _wgrad_call(g_ff, z, dy2, "wgrad_ff2", D, lambda i, j: (i, 1), 2 * R_FF)
    g_ff = g_ff.reshape(N_CHIPS, 2, R_FF, D)
    dy, dya, dyb, doa, dcb, dp_gt, acc_m, bs_gt, hr_ff = _merge_bwd_call(dx2, y, ya, yb, p, mod, small["post_tm"],
                                                                        wg, g_ff)
    part_ff = _add_halves_call(g_ff, hr_ff, c_idx, "ff")

    g_br = _wgrad_rows_call(None, oa, dya, "wgrad_br_a", 0)
    g_br = _wgrad_rows_call(g_br, cb, dyb, "wgrad_br_b", 1)
    g_br = _wgrad_rows_call(g_br, mg, dy, "wgrad_out", 2)
    g_br = g_br.reshape(N_CHIPS, 2, 3 * R_BR // 2, D)
    dp_hg, bs_hg, dlg, dgn, recv_ff, hr_br = _hgrn_bwd_call(p, o, doa, st, small["logits"], small["hg_norm"],
                                                            part_ff, g_br)
    part_br = _add_halves_call(g_br, hr_br, c_idx, "br")
    dp_cv, bs_cv, ddw, acc_c, recv_br = _conv_bwd_call(dcb, uc, u, p, small["conv_dw"], small["ln_g"], small["ln_b"],
                                                        part_br)

    g_in = _wgrad_call(None, h1, dp_hg, "wgrad_in_hg", D, lambda i, j: (j // 2, j % 2), R_IN)
    g_in = _wgrad_call(g_in, h1, dp_cv, "wgrad_in_cv", D, lambda i, j: (2, j), R_IN)
    g_in = _wgrad_call(g_in, h1, dp_gt, "wgrad_in_gt", D, lambda i, j: (3, j), R_IN)
    part_in = _rs_begin(g_in, c_idx, "in")
    chip_c = jnp.concatenate([chip_idx, c_idx])
    full_ff = _add_chips_call(part_ff, recv_ff, chip_c, "ff")
    full_br = _add_chips_call(part_br, recv_br, chip_c, "br")
    gx, acc_i, recv_in, full_ff, full_br = _in_bwd_call(dp_hg, dp_cv, dp_gt, x, dx2, mod, small["pre_tm"], wg,
                                                        part_in, full_ff, full_br)
    red_ff = full_ff.reshape(2 * R_FF, D)
    red_br = full_br.reshape(3 * R_BR, D)
    red_in = _rs_end(part_in, recv_in, c_idx, chip_idx, "in")

    zrow = jnp.zeros((1, D), F32)
    rows = [acc_i[0:1], acc_i[1:2], acc_m[0:1], acc_f[2:3], acc_f[3:4], acc_f[0:1],
            acc_i[2:3], acc_m[1:2], acc_f[4:5], acc_f[1:2],
            jnp.concatenate([bs_hg, bs_cv, bs_gt], axis=1).reshape(8, D),
            dlg, dgn, acc_c[0:1], acc_c[1:2], acc_c[2:3],
            ddw,
            cact, acc_f[5:6]] + [zrow] * 6
    return gx, jnp.concatenate(rows, axis=0), red_in, red_ff, red_br


def kernel(x, c, w_ada, b_ada, pre_norm_tm, post_norm_tm, pre_norm_cm, post_norm_cm, w_in, b_in, hg_lb_logits, hg_norm, conv_dw, conv_db, conv_ln_g, conv_ln_b, w_br_a, w_br_b, w_out, w_ff1, w_ff2, loss_target, m_w_ada, m_b_ada, m_pre_norm_tm, m_post_norm_tm, m_pre_norm_cm, m_post_norm_cm, m_w_in, m_b_in, m_hg_lb_logits, m_hg_norm, m_conv_dw, m_conv_db, m_conv_ln_g, m_conv_ln_b, m_w_br_a, m_w_br_b, m_w_out, m_w_ff1, m_w_ff2, v_w_ada, v_b_ada, v_pre_norm_tm, v_post_norm_tm, v_pre_norm_cm, v_post_norm_cm, v_w_in, v_b_in, v_hg_lb_logits, v_hg_norm, v_conv_dw, v_conv_db, v_conv_ln_g, v_conv_ln_b, v_w_br_a, v_w_br_b, v_w_out, v_w_ff1, v_w_ff2):
    xi, yi, ci = lax.axis_index("x"), lax.axis_index("y"), lax.axis_index("c")
    chip = 2 * xi + yi
    c_idx = jnp.reshape(ci, (1,)).astype(jnp.int32)
    chip_idx = jnp.reshape(chip, (1,)).astype(jnp.int32)

    def pack_small(ada_b, pre_t, post_t, pre_c, post_c, in_b, lg, hgn, cdb, lng, lnb, cdw):
        flat = jnp.concatenate([cdw[0].reshape(-1), jnp.zeros((8 * D - CONV_K * 256,), F32)]).reshape(8, D)
        return jnp.concatenate([ada_b.reshape(6, D), pre_t, post_t, pre_c, post_c, in_b.reshape(8, D), lg, hgn,
                                cdb, lng, lnb, flat], axis=0)

    w_in_halves = w_in[0].reshape(D, 2, D).transpose(1, 0, 2).reshape(R_IN, D)
    pack = jnp.concatenate([w_in_halves, w_ff1[0], w_ff2[0], w_br_a[0], w_br_b[0], w_out[0]],
                           axis=0).astype(BF16)
    wg = lax.dynamic_update_slice(lax.empty((N_CHIPS, PACK_W, D), BF16), pack[None], (chip, 0, 0))
    wa = 6 * D // N_CHIPS
    me = 4 * xi + 2 * yi + ci
    dw_blk = jnp.concatenate([conv_dw[0].reshape(-1), jnp.zeros((8 * D - CONV_K * 256,), F32)]).reshape(8, D)
    dw_all, ca_all, mod_all = _prologue_call(
        dw_blk, jnp.broadcast_to(c, (8, D)), w_ada[0].astype(BF16),
        lax.dynamic_slice_in_dim(b_ada, chip * wa, wa, axis=1))
    order = jnp.stack([chip, 2 * (1 - xi) + yi, 2 * xi + (1 - yi), 2 * (1 - xi) + (1 - yi)]).astype(jnp.int32)
    dw_all = dw_all.reshape(N_CHIPS, 2, 8 * D)[:, 0, :CONV_K * 256].reshape(N_CHIPS, CONV_K, 256)
    dw_full = dw_all.transpose(1, 0, 2).reshape(CONV_K, D)
    cact = lax.dynamic_slice_in_dim(ca_all, me * 8, 1, axis=0)
    mod_mine = lax.dynamic_index_in_dim(mod_all.reshape(N_CHIPS, 2, N_DEV, 8, wa)[:, 0, :, 0, :], me, axis=1,
                                        keepdims=False)
    mod = mod_mine.reshape(1, 6 * D)

    small = dict(b_ada=b_ada, pre_tm=pre_norm_tm, post_tm=post_norm_tm, pre_cm=pre_norm_cm, post_cm=post_norm_cm,
                 b_in=b_in, logits=hg_lb_logits, hg_norm=hg_norm, conv_dw=dw_full, conv_db=conv_db,
                 ln_g=conv_ln_g, ln_b=conv_ln_b, order=order)

    gx, srows, red_in, red_ff, red_br = _local_step(x[0], mod, cact, loss_target[0], wg, pack, small, c_idx,
                                                    chip_idx)

    shapes = {"in": w_in.shape, "br_a": w_br_a.shape, "br_b": w_br_b.shape, "out": w_out.shape,
              "ff1": w_ff1.shape, "ff2": w_ff2.shape}
    offs = {"in": (red_in, 0, R_IN), "ff1": (red_ff, 0, R_FF), "ff2": (red_ff, R_FF, 2 * R_FF),
            "br_a": (red_br, 0, R_BR), "br_b": (red_br, R_BR, 2 * R_BR), "out": (red_br, 2 * R_BR, 3 * R_BR)}
    wmv = {"in": (w_in, m_w_in, v_w_in), "br_a": (w_br_a, m_w_br_a, v_w_br_a), "br_b": (w_br_b, m_w_br_b, v_w_br_b),
           "out": (w_out, m_w_out, v_w_out), "ff1": (w_ff1, m_w_ff1, v_w_ff1), "ff2": (w_ff2, m_w_ff2, v_w_ff2)}
    res = {}
    for n in offs:
        shp = shapes[n]
        g2d = offs[n][0][offs[n][1]:offs[n][2]]
        if n == "in":
            g2d = g2d.reshape(2, D, D).transpose(1, 0, 2)
        g2d = g2d.reshape(shp[1], shp[2])
        w_, m_, v_ = (a[0] for a in wmv[n])
        if n == "in":
            d_, m2_, v2_, sall, ssum = _adamw_gather_call(w_, g2d, m_, v_, srows, "adamw_in")
        else:
            d_, m2_, v2_ = _adamw_call(w_, g2d, m_, v_, "adamw_" + n)
        res[n] = tuple(a.reshape(shp) for a in (g2d, d_, m2_, v2_))

    sall = sall.reshape(N_DEV, SMALL_ROWS, D)
    loss = jnp.sum(ssum[57])
    dmod_all = sall[:, 0:6, :].reshape(N_DEV, 6 * D)
    g_ada = _outer_call(sall[:, 56, :], lax.dynamic_slice_in_dim(dmod_all, chip * wa, wa, axis=1))
    g_dw = lax.dynamic_slice_in_dim(ssum[24:24 + CONV_K], chip * 256, 256, axis=1)
    g_small = jnp.concatenate(
        [ssum[0:24], jnp.concatenate([g_dw.reshape(-1), jnp.zeros((8 * D - CONV_K * 256,), F32)]).reshape(8, D)],
        axis=0)
    d_, m2_, v2_ = _adamw_call(w_ada[0], g_ada, m_w_ada[0], v_w_ada[0], "adamw_ada")
    res["ada"] = tuple(a.reshape(w_ada.shape) for a in (g_ada, d_, m2_, v2_))

    ws = pack_small(b_ada, pre_norm_tm, post_norm_tm, pre_norm_cm, post_norm_cm, b_in, hg_lb_logits, hg_norm,
                    conv_db, conv_ln_g, conv_ln_b, conv_dw)
    ms = pack_small(m_b_ada, m_pre_norm_tm, m_post_norm_tm, m_pre_norm_cm, m_post_norm_cm, m_b_in, m_hg_lb_logits,
                    m_hg_norm, m_conv_db, m_conv_ln_g, m_conv_ln_b, m_conv_dw)
    vs = pack_small(v_b_ada, v_pre_norm_tm, v_post_norm_tm, v_pre_norm_cm, v_post_norm_cm, v_b_in, v_hg_lb_logits,
                    v_hg_norm, v_conv_db, v_conv_ln_g, v_conv_ln_b, v_conv_dw)
    sres = (g_small,) + tuple(_adamw_call(ws, g_small, ms, vs, "adamw_small"))

    def unpack_small(t):
        return {"b_ada": t[0:6].reshape(1, 6 * D), "pre_tm": t[6:7], "post_tm": t[7:8], "pre_cm": t[8:9],
                "post_cm": t[9:10], "b_in": t[10:18].reshape(1, IN_COLS), "logits": t[18:20], "hg_norm": t[20:21],
                "conv_db": t[21:22], "ln_g": t[22:23], "ln_b": t[23:24],
                "conv_dw": t[24:32].reshape(-1)[:CONV_K * 256].reshape(1, CONV_K, 256)}

    order = ["ada", "b_ada", "pre_tm", "post_tm", "pre_cm", "post_cm", "in", "b_in", "logits", "hg_norm", "conv_dw",
             "conv_db", "ln_g", "ln_b", "br_a", "br_b", "out", "ff1", "ff2"]
    outs = [loss, gx.reshape(x.shape)]
    for kind in range(4):
        sm = unpack_small(sres[kind])
        for n in order:
            outs.append(res[n][kind] if n in res else sm[n])
    return tuple(outs)
```

```python
import functools

import jax
import jax.numpy as jnp
from jax import lax
from jax.experimental import pallas as pl
from jax.experimental.pallas import tpu as pltpu

F32, BF16 = jnp.float32, jnp.bfloat16
SDS = jax.ShapeDtypeStruct
BS = pl.BlockSpec
MESH = pl.DeviceIdType.MESH
HI = lax.Precision.HIGHEST

D = 1024
D_FF = 4096
IN_COLS = 8192
HEADS, DK = 8, 128
CHUNK = 128
CONV_K = 31
HALO = 32
SUB = 32
EPS = 1e-6
N_CHIPS, N_DEV = 4, 8
TM = 256
TB = 256
VMEM_LIMIT = 56 * 1024 * 1024

R_IN, R_BR, R_FF = 2048, 256, 1024
PACK_W = R_IN + 3 * R_BR + 2 * R_FF
O_IN, O_FF1, O_FF2, O_BRA, O_BRB, O_OUT = 0, 2048, 3072, 4096, 4352, 4608
SMALL_ROWS = 64

ADAM_LR, ADAM_B1, ADAM_B2, ADAM_EPS, ADAM_WD, ADAM_STEP = 0.001, 0.9, 0.999, 1e-08, 0.01, 10

NN = (((1,), (0,)), ((), ()))
NT = (((1,), (1,)), ((), ()))
TN = (((0,), (0,)), ((), ()))


def _mm(a, b, dims=NN, precision=None):
    return lax.dot_general(a, b, dims, preferred_element_type=F32, precision=precision)


def _sig(v):
    return jax.nn.sigmoid(v)


def _dsilu(v, s):
    return s * (1.0 + v * (1.0 - s))


def _params(*sem):
    return pltpu.CompilerParams(dimension_semantics=sem if sem else None, vmem_limit_bytes=VMEM_LIMIT)


def _rowsum(v):
    return jnp.sum(v, axis=0, keepdims=True)


def _mesh_pos():
    return lax.axis_index("x"), lax.axis_index("y"), lax.axis_index("c")


def _allgather_parts(x_ref, out_ref, send_sems, recv_sems, local_sem):
    m_per = x_ref.shape[0]
    x, y, c = _mesh_pos()
    me, sibling = (x, y, c), (x, y, 1 - c)
    chips = [(1 - x, y), (x, 1 - y), (1 - x, 1 - y)]

    def rows(px, py, pc):
        return out_ref.at[pl.ds((4 * px + 2 * py + pc) * m_per, m_per), :]

    def copy(k, block, to, src=None):
        return pltpu.make_async_remote_copy(
            src_ref=rows(*block) if src is None else src, dst_ref=rows(*block),
            send_sem=send_sems.at[k], recv_sem=recv_sems.at[k], device_id=to, device_id_type=MESH)

    def first():
        return [copy(0, me, sibling, src=x_ref)] + [copy(1 + j, me, (*chip, c), src=x_ref)
                                                    for j, chip in enumerate(chips)]

    def start():
        pltpu.make_async_copy(x_ref, rows(*me), local_sem).start()
        for cp in first():
            cp.start()

    def finish():
        passed = [copy(4 + j, (*chip, c), sibling) for j, chip in enumerate(chips)]
        for j, chip in enumerate(chips):
            copy(1 + j, (*chip, c), me).wait_recv()
            passed[j].start()
        copy(0, sibling, me).wait_recv()
        for j, chip in enumerate(chips):
            copy(4 + j, (*chip, 1 - c), me).wait_recv()
        for cp in first() + passed:
            cp.wait_send()
        pltpu.make_async_copy(x_ref, rows(*me), local_sem).wait()

    return start, finish


def _allgather(x_ref, out_ref, send_sems, recv_sems, local_sem):
    start, finish = _allgather_parts(x_ref, out_ref, send_sems, recv_sems, local_sem)
    start()
    finish()


def _allgather_sems():
    return [pltpu.SemaphoreType.DMA((7,)), pltpu.SemaphoreType.DMA((7,)), pltpu.SemaphoreType.DMA]


def _allgather_call(blk, name, in_vmem, with_sum):
    m_per, n = blk.shape

    def body(x_ref, out_ref, *rest):
        if with_sum:
            sum_ref, send_sems, recv_sems, local_sem = rest
        else:
            send_sems, recv_sems, local_sem = rest
        _allgather(x_ref, out_ref, send_sems, recv_sems, local_sem)
        if with_sum:
            acc = out_ref[0:m_per, :]
            for d in range(1, N_DEV):
                acc = acc + out_ref[d * m_per:(d + 1) * m_per, :]
            sum_ref[...] = acc

    space = pltpu.VMEM if in_vmem else pl.ANY
    out_shape = [SDS((N_DEV * m_per, n), blk.dtype)]
    out_specs = [BS(memory_space=space)]
    if with_sum:
        out_shape.append(SDS((m_per, n), blk.dtype))
        out_specs.append(BS(memory_space=pltpu.VMEM))
    return pl.pallas_call(
        body, name=name, out_shape=out_shape, in_specs=[BS(memory_space=space)], out_specs=out_specs,
        scratch_shapes=[pltpu.SemaphoreType.DMA((7,)), pltpu.SemaphoreType.DMA((7,)), pltpu.SemaphoreType.DMA],
        compiler_params=pltpu.CompilerParams(vmem_limit_bytes=VMEM_LIMIT),
    )(blk)


def _gather_sems(n_ranges):
    return [pltpu.SemaphoreType.DMA((6 * n_ranges,)), pltpu.SemaphoreType.DMA((6 * n_ranges,))]


def _pack_gather(pack_ref, wg_ref, send_sems, recv_sems, ranges):
    x, y, c = _mesh_pos()
    me, sibling = (x, y, c), (x, y, 1 - c)
    chips = [(1 - x, y), (x, 1 - y), (1 - x, 1 - y)]

    def land(r, px, py, pc):
        off, n = ranges[r]
        return wg_ref.at[2 * px + py, pl.ds(off + pc * (n // 2), n // 2), :]

    def mine(r):
        off, n = ranges[r]
        return pack_ref.at[pl.ds(off + c * (n // 2), n // 2), :]

    def copy(r, k, block, to, src=None):
        return pltpu.make_async_remote_copy(
            src_ref=land(r, *block) if src is None else src, dst_ref=land(r, *block),
            send_sem=send_sems.at[6 * r + k], recv_sem=recv_sems.at[6 * r + k], device_id=to, device_id_type=MESH)

    def start():
        for r in range(len(ranges)):
            for j, chip in enumerate(chips):
                copy(r, j, me, (*chip, c), src=mine(r)).start()

    def arrive(j):
        for r in range(len(ranges)):
            copy(r, j, (*chips[j], c), me).wait_recv()
            copy(r, 3 + j, (*chips[j], c), sibling).start()
        for r in range(len(ranges)):
            copy(r, 3 + j, (*chips[j], 1 - c), me).wait_recv()

    def drain():
        for r in range(len(ranges)):
            for j, chip in enumerate(chips):
                copy(r, j, me, (*chip, c), src=mine(r)).wait_send()
                copy(r, 3 + j, (*chip, c), sibling).wait_send()

    def finish():
        for r in range(len(ranges)):
            for j, chip in enumerate(chips):
                copy(r, j, (*chip, c), me).wait_recv()
                copy(r, 3 + j, (*chip, c), sibling).start()
        for r in range(len(ranges)):
            for j, chip in enumerate(chips):
                copy(r, 3 + j, (*chip, 1 - c), me).wait_recv()
        drain()

    return start, finish, arrive, drain


def _prologue_call(dw_blk, c_blk, w_ada, b_ada):
    wa = w_ada.shape[1]

    def body(dw_ref, c_ref, wa_ref, ba_ref, dwg_ref, ca_ref, modg_ref,
             cg_scr, part_scr, s1, r1, l1, s2, r2, l2, s3, r3, l3):
        start_c, finish_c = _allgather_parts(c_ref, cg_scr, s2, r2, l2)
        start_dw, finish_dw = _allgather_parts(dw_ref, dwg_ref, s1, r1, l1)
        start_mod, finish_mod = _allgather_parts(part_scr, modg_ref, s3, r3, l3)
        start_c()
        start_dw()
        finish_c()
        cv = cg_scr[...]
        ca = cv * _sig(cv)
        ca_ref[...] = ca
        part_scr[...] = _mm(ca.astype(BF16), wa_ref[...]) + ba_ref[...]
        start_mod()
        finish_dw()
        finish_mod()

    vm = BS(memory_space=pltpu.VMEM)
    return pl.pallas_call(
        body, name="prologue_adaln_conv_dw",
        out_shape=(SDS((N_DEV * 8, D), F32), SDS((N_DEV * 8, D), F32), SDS((N_DEV * N_DEV * 8, wa), F32)),
        in_specs=[vm, vm, vm, vm], out_specs=(vm, vm, vm),
        scratch_shapes=[pltpu.VMEM((N_DEV * 8, D), F32), pltpu.VMEM((N_DEV * 8, wa), F32)]
        + _allgather_sems() + _allgather_sems() + _allgather_sems(),
        compiler_params=pltpu.CompilerParams(vmem_limit_bytes=VMEM_LIMIT),
    )(dw_blk, c_blk, w_ada, b_ada)


def _halves_exchange(g_ref, out_ref, send_sems, recv_sems):
    x, y, c = _mesh_pos()

    def copies():
        return [pltpu.make_async_remote_copy(
            src_ref=g_ref.at[k, 1 - c], dst_ref=out_ref.at[k], send_sem=send_sems.at[k], recv_sem=recv_sems.at[k],
            device_id=(x, y, 1 - c), device_id_type=MESH) for k in range(N_CHIPS)]

    def start():
        for cp in copies():
            cp.start()

    def finish():
        for cp in copies():
            cp.wait()

    return start, finish


def _halves_sems():
    return [pltpu.SemaphoreType.DMA((N_CHIPS,)), pltpu.SemaphoreType.DMA((N_CHIPS,))]


def _sibling_halves_call(g, tag):
    _, _, h, n = g.shape

    def body(g_ref, out_ref, send_sems, recv_sems):
        start, finish = _halves_exchange(g_ref, out_ref, send_sems, recv_sems)
        start()
        finish()

    return pl.pallas_call(
        body, name="rs_sibling_halves_" + tag, out_shape=SDS((N_CHIPS, h, n), g.dtype),
        in_specs=[BS(memory_space=pl.ANY)], out_specs=BS(memory_space=pl.ANY),
        scratch_shapes=_halves_sems(),
    )(g)


def _chip_exchange(p_ref, out_ref, send_sems, recv_sems):
    x, y, c = _mesh_pos()
    chips = [(1 - x, y), (x, 1 - y), (1 - x, 1 - y)]

    def copies():
        return [pltpu.make_async_remote_copy(
            src_ref=p_ref.at[2 * cx + cy], dst_ref=out_ref.at[j], send_sem=send_sems.at[j], recv_sem=recv_sems.at[j],
            device_id=(cx, cy, c), device_id_type=MESH) for j, (cx, cy) in enumerate(chips)]

    def start():
        for cp in copies():
            cp.start()

    def finish():
        for cp in copies():
            cp.wait()

    return start, finish


def _exchange_sems():
    return [pltpu.SemaphoreType.DMA((3,)), pltpu.SemaphoreType.DMA((3,))]


def _join_exchange(in_ref, out_ref, send_sems, recv_sems):
    h = in_ref.shape[1]
    q = h // 4
    x, y, c = _mesh_pos()

    def copy(k, half):
        return pltpu.make_async_remote_copy(
            src_ref=in_ref.at[half, pl.ds(k * q, q)], dst_ref=out_ref.at[half, pl.ds(k * q, q)],
            send_sem=send_sems.at[k], recv_sem=recv_sems.at[k],
            device_id=(x, y, 1 - c), device_id_type=MESH)

    def start():
        for k in range(4):
            copy(k, c).start()

    def finish():
        for k in range(4):
            copy(k, c).wait_send()
            copy(k, 1 - c).wait_recv()

    return start, finish


def _join_sems():
    return [pltpu.SemaphoreType.DMA((4,)), pltpu.SemaphoreType.DMA((4,))]


def _sibling_join_call(full, tag):
    def body(in_ref, out_ref, send_sems, recv_sems):
        start, finish = _join_exchange(in_ref, out_ref, send_sems, recv_sems)
        start()
        finish()

    return pl.pallas_call(
        body, name="rs_sibling_join_" + tag, out_shape=SDS(full.shape, full.dtype),
        in_specs=[BS(memory_space=pl.ANY)], out_specs=BS(memory_space=pl.ANY),
        scratch_shapes=_join_sems(), input_output_aliases={0: 0},
    )(full)


def _add_halves_call(g, recv, c_idx, tag):
    _, _, h, n = g.shape
    tr = h // 2

    def body(c_ref, g_ref, r_ref, o_ref):
        o_ref[...] = (g_ref[...].astype(F32) + r_ref[...].astype(F32)).astype(BF16)

    return pl.pallas_call(
        body, name="rs_add_halves_" + tag, out_shape=SDS((N_CHIPS, h, n), BF16),
        grid_spec=pltpu.PrefetchScalarGridSpec(
            num_scalar_prefetch=1, grid=(N_CHIPS, 2),
            in_specs=[BS((None, None, tr, n), lambda k, r, c_ref: (k, c_ref[0], r, 0)),
                      BS((None, tr, n), lambda k, r, c_ref: (k, r, 0))],
            out_specs=BS((None, tr, n), lambda k, r, c_ref: (k, r, 0))),
        compiler_params=_params("arbitrary", "arbitrary"),
    )(c_idx, g, recv)


def _add_chips_call(p, recv, chip_c_idx, tag):
    _, h, n = p.shape
    tr = h // 2

    def body(k_ref, p_ref, r_ref, o_ref):
        acc = p_ref[...].astype(F32)
        for j in range(3):
            acc = acc + r_ref[j].astype(F32)
        o_ref[...] = acc

    return pl.pallas_call(
        body, name="rs_add_chips_" + tag, out_shape=SDS((2, h, n), F32),
        grid_spec=pltpu.PrefetchScalarGridSpec(
            num_scalar_prefetch=1, grid=(2,),
            in_specs=[BS((None, tr, n), lambda r, k_ref: (k_ref[0], r, 0)),
                      BS((3, tr, n), lambda r, k_ref: (0, r, 0))],
            out_specs=BS((None, tr, n), lambda r, k_ref: (k_ref[1], r, 0))),
        compiler_params=_params("arbitrary"),
    )(chip_c_idx, p, recv)


def _load_rows(wg_hbm, w_vmem, sem, off):
    cp = pltpu.make_async_copy(wg_hbm.at[:, pl.ds(off, w_vmem.shape[1]), :], w_vmem, sem)
    cp.start()
    return cp


def _fwd_in_call(x, mod, pre_tm, wg, b_in, pack, order):
    S = x.shape[0]
    tmf = 2 * TM
    nt = S // tmf
    wc = IN_COLS // N_CHIPS

    def body(ord_ref, x_ref, mod_ref, g_ref, w_hbm, b_ref, pack_ref, p_ref, h_hbm, wg_out, w_vmem, h_scr, sem,
             send_sems, recv_sems, send_sems2, recv_sems2):
        q, i = pl.program_id(0), pl.program_id(1)
        rows = pl.ds(pl.multiple_of(i * tmf, tmf), tmf)
        start, _, arrive, drain = _pack_gather(pack_ref, wg_out, send_sems, recv_sems, [(O_IN, R_IN)])
        start2, finish2, _, _ = _pack_gather(pack_ref, wg_out, send_sems2, recv_sems2, [(O_BRA, 3 * R_BR)])

        def load_weights():
            cp = pltpu.make_async_copy(wg_out.at[ord_ref[q], pl.ds(O_IN, R_IN), :], w_vmem, sem)
            cp.start()
            cp.wait()

        @pl.when((q == 0) & (i == 0))
        def _():
            start()
            start2()
            load_weights()

        for j in range(3):
            @pl.when((q == j + 1) & (i == 0))
            def _(j=j):
                arrive(j)
                load_weights()

        @pl.when(q == 0)
        def _():
            xv = x_ref[...]
            r = lax.rsqrt(jnp.mean(xv * xv, axis=-1, keepdims=True) + EPS)
            h = xv * r * g_ref[...] * (1.0 + mod_ref[:, D:2 * D]) + mod_ref[:, 0:D]
            h_scr[rows, :] = h.astype(BF16)

        hb = h_scr[rows, :]
        for k in range(wc // D):
            p_ref[:, k * D:(k + 1) * D] = _mm(hb, w_vmem[k * D:(k + 1) * D, :]) + b_ref[:, k * D:(k + 1) * D]

        @pl.when((q == N_CHIPS - 1) & (i == nt - 1))
        def _():
            cp = pltpu.make_async_copy(h_scr, h_hbm, sem)
            cp.start()
            drain()
            finish2()
            cp.wait()

    hbm = BS(memory_space=pl.ANY)
    return pl.pallas_call(
        body, name="fwd_in", out_shape=(SDS((S, IN_COLS), F32), SDS((S, D), BF16), SDS(wg.shape, wg.dtype)),
        grid_spec=pltpu.PrefetchScalarGridSpec(
            num_scalar_prefetch=1, grid=(N_CHIPS, nt),
            in_specs=[BS((tmf, D), lambda q, i, o: (jnp.where(q == 0, i, nt - 1), 0)),
                      BS((1, 6 * D), lambda q, i, o: (0, 0)),
                      BS((1, D), lambda q, i, o: (0, 0)), hbm, BS((1, wc), lambda q, i, o: (0, o[q])), hbm],
            out_specs=(BS((tmf, wc), lambda q, i, o: (i, o[q])), hbm, hbm),
            scratch_shapes=[pltpu.VMEM((R_IN, D), BF16), pltpu.VMEM((S, D), BF16), pltpu.SemaphoreType.DMA]
            + _gather_sems(1) + _gather_sems(1)),
        input_output_aliases={4: 2},
        compiler_params=_params("arbitrary", "arbitrary"),
    )(order, x, mod, pre_tm, wg, b_in, pack)


def _lower_bound(lg_ref):
    l0, l1 = lg_ref[0:1, :], lg_ref[1:2, :]
    mx = jnp.maximum(l0, l1)
    e0, e1 = jnp.exp(l0 - mx), jnp.exp(l1 - mx)
    return e0 / (e0 + e1)


def _tri_masks():
    ri = lax.broadcasted_iota(jnp.int32, (CHUNK, CHUNK), 0)
    ci = lax.broadcasted_iota(jnp.int32, (CHUNK, CHUNK), 1)
    return (ri >= ci).astype(F32), (ci >= ri).astype(F32)


def _cumsum_mm(tri, g):
    tb = tri.astype(BF16)
    hi = g.astype(BF16)
    r1 = g - hi.astype(F32)
    mid = r1.astype(BF16)
    lo = (r1 - mid.astype(F32)).astype(BF16)
    return _mm(tb, hi) + _mm(tb, mid) + _mm(tb, lo)


def _hg_gates(q_r, f_r, lb, tril):
    sq = _sig(q_r)
    q = q_r * sq
    sf = _sig(f_r)
    f = lb + (1.0 - lb) * sf
    k = 1.0 - f
    g = jnp.log(f)
    b = _cumsum_mm(tril, g)
    b_last = _rowsum(g)
    row = lax.broadcasted_iota(jnp.int32, g.shape, 0)
    ref = _rowsum(jnp.where(row < CHUNK // 2, g, 0.0))
    e = jnp.exp(b)
    eq = jnp.exp(jnp.minimum(b - ref, 80.0))
    ek = jnp.exp(jnp.minimum(ref - b, 80.0))
    dd = jnp.exp(b_last - b)
    return dict(sq=sq, q=q, sf=sf, f=f, k=k, e=e, eq=eq, ek=ek, dd=dd, elast=jnp.exp(b_last),
                qe=q * e, qt=q * eq, kt=k * ek, kd=k * dd)


def _hgrn_fwd_call(p, logits, gn, wg, pack):
    S = p.shape[0]
    ncb = TB // CHUNK
    ranges = [(O_FF1, R_FF)]

    def body(q_ref, f_ref, v_ref, og_ref, lg_ref, gn_ref, wg_in, pack_ref, o_ref, oa_ref, st_ref, wg_out,
             st_scr, send_sems, recv_sems):
        start, finish, _, _ = _pack_gather(pack_ref, wg_out, send_sems, recv_sems, ranges)

        @pl.when(pl.program_id(0) == 0)
        def _():
            start()
            st_scr[...] = jnp.zeros_like(st_scr)

        lb = _lower_bound(lg_ref)
        tril, _ = _tri_masks()

        def chunk(ci, carry):
            rows = pl.ds(pl.multiple_of(ci * CHUNK, CHUNK), CHUNK)
            st_ref[ci] = st_scr[...]
            t = _hg_gates(q_ref[rows, :], f_ref[rows, :], lb, tril)
            v = v_ref[rows, :]
            for h in range(HEADS):
                sl = slice(h * DK, (h + 1) * DK)
                stp = st_scr[:, sl]
                vb = v[:, sl].astype(BF16)
                inter = _mm(t["qe"][:, sl].astype(BF16), stp.astype(BF16), NT)
                a = jnp.where(tril > 0.5, _mm(t["qt"][:, sl].astype(BF16), t["kt"][:, sl].astype(BF16), NT), 0.0)
                o = inter + _mm(a.astype(BF16), vb)
                st_scr[:, sl] = stp * t["elast"][:, sl] + _mm(vb, t["kd"][:, sl].astype(BF16), TN)
                oh = o * lax.rsqrt(jnp.mean(o * o, axis=-1, keepdims=True) + EPS)
                og = og_ref[rows, sl]
                o_ref[rows, sl] = o
                oa_ref[rows, sl] = (oh * gn_ref[:, sl] * (og * _sig(og))).astype(BF16)
            return carry

        lax.fori_loop(0, ncb, chunk, 0)

        @pl.when(pl.program_id(0) == S // TB - 1)
        def _():
            finish()

    col = lambda j: BS((TB, D), lambda i, j=j: (i, j))
    hbm = BS(memory_space=pl.ANY)
    return pl.pallas_call(
        body, name="hgrn_fwd", grid=(S // TB,),
        out_shape=(SDS((S, D), F32), SDS((S, D), BF16), SDS((S // CHUNK, DK, D), F32), SDS(wg.shape, wg.dtype)),
        in_specs=[col(0), col(1), col(2), col(3), BS((2, D), lambda i: (0, 0)), BS((1, D), lambda i: (0, 0)),
                  hbm, hbm],
        out_specs=(BS((TB, D), lambda i: (i, 0)), BS((TB, D), lambda i: (i, 0)),
                   BS((ncb, DK, D), lambda i: (i, 0, 0)), hbm),
        scratch_shapes=[pltpu.VMEM((DK, D), F32)] + _gather_sems(len(ranges)),
        input_output_aliases={6: 3},
        compiler_params=_params("arbitrary"),
    )(p, p, p, p, logits, gn, wg, pack)


def _layernorm_stats(uc):
    mu = jnp.mean(uc, axis=-1, keepdims=True)
    xc = uc - mu
    rs = lax.rsqrt(jnp.mean(xc * xc, axis=-1, keepdims=True) + EPS)
    return xc * rs, rs


EXT = HALO + TM + 8


def _fill_shifted(ext, shifted):
    for m in range(1, 8):
        shifted[m - 1] = ext[m:m + HALO + TM, :]


def _window(ext, shifted, s0, n):
    m = s0 % 8
    q = s0 - m
    return ext[q:q + n, :] if m == 0 else shifted[m - 1, q:q + n, :]


def _conv_fwd_call(p, dw, db, ln_g, ln_b, wg, pack):
    S = p.shape[0]
    ranges = [(O_FF2, R_FF)]

    def body(cv_ref, cg_ref, dw_ref, db_ref, g_ref, b_ref, wg_in, pack_ref, u_ref, uc_ref, cb_ref, wg_out,
             uext, ush, send_sems, recv_sems):
        start, finish, _, _ = _pack_gather(pack_ref, wg_out, send_sems, recv_sems, ranges)

        @pl.when(pl.program_id(0) == 0)
        def _():
            start()
            uext[0:HALO, :] = jnp.zeros((HALO, D), F32)
            uext[HALO + TM:EXT, :] = jnp.zeros((EXT - HALO - TM, D), F32)

        u = cv_ref[...] * _sig(cg_ref[...])
        uext[HALO:HALO + TM, :] = u
        u_ref[...] = u
        _fill_shifted(uext, ush)
        for rb in range(TM // SUB):
            acc = jnp.broadcast_to(db_ref[...], (SUB, D))
            for j in range(CONV_K):
                s0 = HALO - (CONV_K - 1) + j + rb * SUB
                acc = acc + dw_ref[j:j + 1, :] * _window(uext, ush, s0, SUB)
            uc_ref[rb * SUB:(rb + 1) * SUB, :] = acc
            xh, _ = _layernorm_stats(acc)
            ln = xh * g_ref[...] + b_ref[...]
            cb_ref[rb * SUB:(rb + 1) * SUB, :] = (ln * _sig(ln)).astype(BF16)
        uext[0:HALO, :] = uext[TM:TM + HALO, :]

        @pl.when(pl.program_id(0) == S // TM - 1)
        def _():
            finish()

    vec = BS((1, D), lambda i: (0, 0))
    hbm = BS(memory_space=pl.ANY)
    return pl.pallas_call(
        body, name="conv_fwd", grid=(S // TM,),
        out_shape=(SDS((S, D), F32), SDS((S, D), F32), SDS((S, D), BF16), SDS(wg.shape, wg.dtype)),
        in_specs=[BS((TM, D), lambda i: (i, 4)), BS((TM, D), lambda i: (i, 5)),
                  BS((CONV_K, D), lambda i: (0, 0)), vec, vec, vec, hbm, hbm],
        out_specs=(BS((TM, D), lambda i: (i, 0)),) * 3 + (hbm,),
        scratch_shapes=[pltpu.VMEM((EXT, D), F32), pltpu.VMEM((7, HALO + TM, D), F32)] + _gather_sems(len(ranges)),
        input_output_aliases={6: 3},
        compiler_params=_params("arbitrary"),
    )(p, p, dw, db, ln_g, ln_b, wg, pack)


def _mm_rows(a, w_ref):
    acc = _mm(a[:, 0:R_BR], w_ref[0])
    for k in range(1, N_CHIPS):
        acc = acc + _mm(a[:, k * R_BR:(k + 1) * R_BR], w_ref[k])
    return acc


def _mm_rows_t(a, w_ref):
    return jnp.concatenate([_mm(a, w_ref[k], NT) for k in range(N_CHIPS)], axis=1)


def _br_spec(off):
    return BS((N_CHIPS, R_BR, D), lambda i: (0, off // R_BR, 0))


def _merge_fwd_call(oa, cb, p, x, mod, post_tm, pre_cm, wg):
    S = x.shape[0]

    def body(oa_ref, cb_ref, ga_ref, gb_ref, x_ref, mod_ref, post_ref, pre_ref, wa_ref, wb_ref, wo_ref,
             ya_ref, yb_ref, mg_ref, y_ref, x2_ref, h2_ref):
        ya = _mm_rows(oa_ref[...], wa_ref)
        yb = _mm_rows(cb_ref[...], wb_ref)
        ya_ref[...] = ya.astype(BF16)
        yb_ref[...] = yb.astype(BF16)
        mg = (_sig(ga_ref[...]) * ya + _sig(gb_ref[...]) * yb).astype(BF16)
        mg_ref[...] = mg
        y = _mm_rows(mg, wo_ref)
        y_ref[...] = y
        n = y * lax.rsqrt(jnp.mean(y * y, axis=-1, keepdims=True) + EPS) * post_ref[...]
        x2 = x_ref[...] + mod_ref[:, 2 * D:3 * D] * n
        x2_ref[...] = x2
        r2 = lax.rsqrt(jnp.mean(x2 * x2, axis=-1, keepdims=True) + EPS)
        h2 = x2 * r2 * pre_ref[...] * (1.0 + mod_ref[:, 4 * D:5 * D]) + mod_ref[:, 3 * D:4 * D]
        h2_ref[...] = h2.astype(BF16)

    tile = BS((TM, D), lambda i: (i, 0))
    vec = BS((1, D), lambda i: (0, 0))
    return pl.pallas_call(
        body, name="merge_fwd", grid=(S // TM,),
        out_shape=(SDS((S, D), BF16), SDS((S, D), BF16), SDS((S, D), BF16), SDS((S, D), F32), SDS((S, D), F32),
                   SDS((S, D), BF16)),
        in_specs=[tile, tile, BS((TM, D), lambda i: (i, 6)), BS((TM, D), lambda i: (i, 7)), tile,
                  BS((1, 6 * D), lambda i: (0, 0)), vec, vec, _br_spec(O_BRA), _br_spec(O_BRB), _br_spec(O_OUT)],
        out_specs=(tile,) * 6,
        compiler_params=_params("arbitrary"),
    )(oa, cb, p, p, x, mod, post_tm, pre_cm, wg, wg, wg)


def _ffn_call(h2, x2, target, mod, post_cm, pre_cm, wg):
    S = x2.shape[0]

    def body(h2_ref, x2_ref, t_ref, mod_ref, post_ref, pre_ref, w_hbm,
             z_ref, da_ref, dy2_ref, dx2_ref, acc_ref, w1_v, w2_v, ra_scr, sems):
        @pl.when(pl.program_id(0) == 0)
        def _():
            c1 = _load_rows(w_hbm, w1_v, sems.at[0], O_FF1)
            c2 = _load_rows(w_hbm, w2_v, sems.at[1], O_FF2)
            c1.wait()
            c2.wait()
            acc_ref[...] = jnp.zeros_like(acc_ref)

        h2 = h2_ref[...]
        for k in range(N_CHIPS):
            ra = jnp.maximum(_mm(h2, w1_v[k]), 0.0)
            ra_scr[:, k * D:(k + 1) * D] = ra
            z_ref[:, k * D:(k + 1) * D] = (ra * ra).astype(BF16)
        y2 = _mm(z_ref[:, 0:D], w2_v[0])
        for k in range(1, N_CHIPS):
            y2 = y2 + _mm(z_ref[:, k * D:(k + 1) * D], w2_v[k])
        ry = lax.rsqrt(jnp.mean(y2 * y2, axis=-1, keepdims=True) + EPS)
        yn = y2 * ry
        n = yn * post_ref[...]
        g2 = mod_ref[:, 5 * D:6 * D]
        x2 = x2_ref[...]
        err = x2 + g2 * n - t_ref[...]
        acc_ref[5:6, :] += _rowsum(err * err) * (0.5 / D)
        dout = err * (1.0 / D)
        acc_ref[0:1, :] += _rowsum(dout * n)
        dn = dout * g2
        acc_ref[1:2, :] += _rowsum(dn * yn)
        dyn = dn * post_ref[...]
        dy2 = (ry * (dyn - yn * jnp.mean(dyn * yn, axis=-1, keepdims=True))).astype(BF16)
        dy2_ref[...] = dy2
        for k in range(N_CHIPS):
            dz = _mm(dy2, w2_v[k], NT)
            da_ref[:, k * D:(k + 1) * D] = (dz * (2.0 * ra_scr[:, k * D:(k + 1) * D])).astype(BF16)
        dh2 = jnp.zeros((TM, D), F32)
        for k in range(N_CHIPS):
            dh2 = dh2 + _mm(da_ref[:, k * D:(k + 1) * D], w1_v[k], NT)
        r2 = lax.rsqrt(jnp.mean(x2 * x2, axis=-1, keepdims=True) + EPS)
        xn = x2 * r2
        yv = xn * pre_ref[...]
        acc_ref[2:3, :] += _rowsum(dh2)
        acc_ref[3:4, :] += _rowsum(dh2 * yv)
        dyv = dh2 * (1.0 + mod_ref[:, 4 * D:5 * D])
        acc_ref[4:5, :] += _rowsum(dyv * xn)
        dxn = dyv * pre_ref[...]
        dx2_ref[...] = dout + r2 * (dxn - xn * jnp.mean(dxn * xn, axis=-1, keepdims=True))

    tile = BS((TM, D), lambda i: (i, 0))
    wide = BS((TM, D_FF), lambda i: (i, 0))
    vec = BS((1, D), lambda i: (0, 0))
    return pl.pallas_call(
        body, name="ffn_fwd_bwd", grid=(S // TM,),
        out_shape=(SDS((S, D_FF), BF16), SDS((S, D_FF), BF16), SDS((S, D), BF16), SDS((S, D), F32),
                   SDS((8, D), F32)),
        in_specs=[tile, tile, tile, BS((1, 6 * D), lambda i: (0, 0)), vec, vec, BS(memory_space=pl.ANY)],
        out_specs=(wide, wide, tile, tile, BS((8, D), lambda i: (0, 0))),
        scratch_shapes=[pltpu.VMEM((N_CHIPS, R_FF, D), BF16), pltpu.VMEM((N_CHIPS, R_FF, D), BF16),
                        pltpu.VMEM((TM, D_FF), F32),
                        pltpu.SemaphoreType.DMA((2,))],
        compiler_params=_params("arbitrary"),
    )(h2, x2, target, mod, post_cm, pre_cm, wg)


def _merge_bwd_call(dx2, y, ya, yb, p, mod, post_tm, wg, g):
    S = y.shape[0]

    def body(dx2_ref, y_ref, ya_ref, yb_ref, ga_ref, gb_ref, mod_ref, post_ref, wa_ref, wb_ref, wo_ref, g_ref,
             dy_ref, dya_ref, dyb_ref, doa_ref, dcb_ref, dpg_ref, acc_ref, bsum_ref, hr_ref, send_sems, recv_sems):
        start, finish = _halves_exchange(g_ref, hr_ref, send_sems, recv_sems)

        @pl.when(pl.program_id(0) == 0)
        def _():
            start()
            acc_ref[...] = jnp.zeros_like(acc_ref)
            bsum_ref[...] = jnp.zeros_like(bsum_ref)

        y = y_ref[...]
        ry = lax.rsqrt(jnp.mean(y * y, axis=-1, keepdims=True) + EPS)
        yn = y * ry
        dx2 = dx2_ref[...]
        acc_ref[0:1, :] += _rowsum(dx2 * (yn * post_ref[...]))
        dn = dx2 * mod_ref[:, 2 * D:3 * D]
        acc_ref[1:2, :] += _rowsum(dn * yn)
        dyn = dn * post_ref[...]
        dy = (ry * (dyn - yn * jnp.mean(dyn * yn, axis=-1, keepdims=True))).astype(BF16)
        dy_ref[...] = dy
        dmg = _mm_rows_t(dy, wo_ref)
        sa, sb = _sig(ga_ref[...]), _sig(gb_ref[...])
        dya = (dmg * sa).astype(BF16)
        dyb = (dmg * sb).astype(BF16)
        dya_ref[...] = dya
        dyb_ref[...] = dyb
        dga = dmg * ya_ref[...].astype(F32) * (sa * (1.0 - sa))
        dgb = dmg * yb_ref[...].astype(F32) * (sb * (1.0 - sb))
        dpg_ref[:, 0:D] = dga.astype(BF16)
        dpg_ref[:, D:2 * D] = dgb.astype(BF16)
        bsum_ref[:, 0:D] += _rowsum(dga)
        bsum_ref[:, D:2 * D] += _rowsum(dgb)
        doa_ref[...] = _mm_rows_t(dya, wa_ref)
        dcb_ref[...] = _mm_rows_t(dyb, wb_ref)

        @pl.when(pl.program_id(0) == S // TM - 1)
        def _():
            finish()

    tile = BS((TM, D), lambda i: (i, 0))
    vec = BS((1, D), lambda i: (0, 0))
    return pl.pallas_call(
        body, name="merge_bwd", grid=(S // TM,),
        out_shape=(SDS((S, D), BF16), SDS((S, D), BF16), SDS((S, D), BF16), SDS((S, D), F32), SDS((S, D), F32),
                   SDS((S, 2 * D), BF16), SDS((8, D), F32), SDS((1, 2 * D), F32),
                   SDS((N_CHIPS,) + g.shape[2:], g.dtype)),
        in_specs=[tile, tile, tile, tile, BS((TM, D), lambda i: (i, 6)), BS((TM, D), lambda i: (i, 7)),
                  BS((1, 6 * D), lambda i: (0, 0)), vec, _br_spec(O_BRA), _br_spec(O_BRB), _br_spec(O_OUT),
                  BS(memory_space=pl.ANY)],
        out_specs=(tile, tile, tile, tile, tile, BS((TM, 2 * D), lambda i: (i, 0)),
                   BS((8, D), lambda i: (0, 0)), BS((1, 2 * D), lambda i: (0, 0)), BS(memory_space=pl.ANY)),
        scratch_shapes=_halves_sems(),
        compiler_params=_params("arbitrary"),
    )(dx2, y, ya, yb, p, p, mod, post_tm, wg, wg, wg, g)


def _hgrn_bwd_call(p, o, doa, st, logits, gn, part, g):
    S = p.shape[0]
    nb = S // TB
    ncb = TB // CHUNK

    def body(q_ref, f_ref, v_ref, og_ref, o_ref, doa_ref, st_ref, lg_ref, gn_ref, part_ref, g_ref,
             dp_ref, bsum_ref, dlg_ref, dgn_ref, recv_ref, hr_ref,
             dst_scr, dlb_scr, dqe_s, dqt_s, dkt_s, dkd_s, dv_s, dog_s, dble_s, send_sems, recv_sems, hs, hr):
        i = pl.program_id(0)
        start, finish = _chip_exchange(part_ref, recv_ref, send_sems, recv_sems)
        start_h, finish_h = _halves_exchange(g_ref, hr_ref, hs, hr)

        @pl.when(i == 0)
        def _():
            start_h()
            start()
            dst_scr[...] = jnp.zeros_like(dst_scr)
            dlb_scr[...] = jnp.zeros_like(dlb_scr)
            bsum_ref[...] = jnp.zeros_like(bsum_ref)
            dgn_ref[...] = jnp.zeros_like(dgn_ref)

        lb = _lower_bound(lg_ref)
        tril, triu = _tri_masks()

        def chunk(tt, carry):
            ci = ncb - 1 - tt
            rows = pl.ds(pl.multiple_of(ci * CHUNK, CHUNK), CHUNK)
            q_r, f_r = q_ref[rows, :], f_ref[rows, :]
            t = _hg_gates(q_r, f_r, lb, tril)
            v = v_ref[rows, :]
            for h in range(HEADS):
                sl = slice(h * DK, (h + 1) * DK)
                stp = st_ref[ci, :, sl]
                stb = stp.astype(BF16)
                qeb = t["qe"][:, sl].astype(BF16)
                qtb = t["qt"][:, sl].astype(BF16)
                ktb = t["kt"][:, sl].astype(BF16)
                kdb = t["kd"][:, sl].astype(BF16)
                vb = v[:, sl].astype(BF16)
                a = jnp.where(tril > 0.5, _mm(qtb, ktb, NT), 0.0)
                o_h = o_ref[rows, sl]
                rinv = lax.rsqrt(jnp.mean(o_h * o_h, axis=-1, keepdims=True) + EPS)
                oh = o_h * rinv
                og = og_ref[rows, sl]
                so = _sig(og)
                d_oa = doa_ref[rows, sl]
                don = d_oa * (og * so)
                dog_s[:, sl] = d_oa * (oh * gn_ref[:, sl]) * _dsilu(og, so)
                dgn_ref[:, sl] += _rowsum(don * oh)
                doh = don * gn_ref[:, sl]
                do = (rinv * (doh - oh * jnp.mean(doh * oh, axis=-1, keepdims=True))).astype(BF16)
                dqe_s[:, sl] = _mm(do, stb, NN)
                dstp = _mm(do, qeb, TN)
                dab = jnp.where(tril > 0.5, _mm(do, vb, NT), 0.0).astype(BF16)
                dqt_s[:, sl] = _mm(dab, ktb, NN)
                dkt_s[:, sl] = _mm(dab, qtb, TN)
                dstn = dst_scr[:, sl]
                dsb = dstn.astype(BF16)
                dkd_s[:, sl] = _mm(vb, dsb, NN)
                dv_s[:, sl] = _mm(a.astype(BF16), do, TN) + _mm(kdb, dsb, NT)
                el = t["elast"][:, sl]
                dst_scr[:, sl] = dstn * el + dstp
                dble_s[:, sl] = el * _rowsum(stp * dstn)
            dqe, dqt, dkt, dkd = dqe_s[...], dqt_s[...], dkt_s[...], dkd_s[...]
            dq = dqe * t["e"] + dqt * t["eq"]
            dk = dkt * t["ek"] + dkd * t["dd"]
            dkk = dkd * t["kd"]
            qt_r = t["qt"].astype(BF16).astype(F32)
            kt_r = t["kt"].astype(BF16).astype(F32)
            dbv = dqe * t["qe"] + dqt * qt_r - dkt * kt_r - dkk
            dg = _cumsum_mm(triu, dbv) + (_rowsum(dkk) + dble_s[...])
            df = dg / t["f"] - dk
            sf = t["sf"]
            dlb_scr[...] += _rowsum(df * (1.0 - sf))
            dqr = dq * _dsilu(q_r, t["sq"])
            dfr = df * (1.0 - lb) * (sf * (1.0 - sf))
            dvv, dog = dv_s[...], dog_s[...]
            dp_ref[rows, 0:D] = dqr.astype(BF16)
            dp_ref[rows, D:2 * D] = dfr.astype(BF16)
            dp_ref[rows, 2 * D:3 * D] = dvv.astype(BF16)
            dp_ref[rows, 3 * D:4 * D] = dog.astype(BF16)
            bsum_ref[:, 0:D] += _rowsum(dqr)
            bsum_ref[:, D:2 * D] += _rowsum(dfr)
            bsum_ref[:, 2 * D:3 * D] += _rowsum(dvv)
            bsum_ref[:, 3 * D:4 * D] += _rowsum(dog)
            return carry

        lax.fori_loop(0, ncb, chunk, 0)

        dl = dlb_scr[...] * lb * (1.0 - lb)
        dlg_ref[0:1, :] = dl
        dlg_ref[1:2, :] = -dl

        @pl.when(i == nb - 1)
        def _():
            finish_h()
            finish()

    col = lambda j: BS((TB, D), lambda i, j=j: (nb - 1 - i, j))
    rev = BS((TB, D), lambda i: (nb - 1 - i, 0))
    cd = pltpu.VMEM((CHUNK, D), F32)
    return pl.pallas_call(
        body, name="hgrn_bwd", grid=(nb,),
        out_shape=(SDS((S, 4 * D), BF16), SDS((1, 4 * D), F32), SDS((2, D), F32), SDS((1, D), F32),
                   SDS((3,) + part.shape[1:], part.dtype), SDS((N_CHIPS,) + g.shape[2:], g.dtype)),
        in_specs=[col(0), col(1), col(2), col(3), rev, rev, BS((ncb, DK, D), lambda i: (nb - 1 - i, 0, 0)),
                  BS((2, D), lambda i: (0, 0)), BS((1, D), lambda i: (0, 0)), BS(memory_space=pl.ANY),
                  BS(memory_space=pl.ANY)],
        out_specs=(BS((TB, 4 * D), lambda i: (nb - 1 - i, 0)), BS((1, 4 * D), lambda i: (0, 0)),
                   BS((2, D), lambda i: (0, 0)), BS((1, D), lambda i: (0, 0)), BS(memory_space=pl.ANY),
                   BS(memory_space=pl.ANY)),
        scratch_shapes=[pltpu.VMEM((DK, D), F32), pltpu.VMEM((1, D), F32), cd, cd, cd, cd, cd, cd,
                        pltpu.VMEM((1, D), F32)] + _exchange_sems() + _halves_sems(),
        compiler_params=_params("arbitrary"),
    )(p, p, p, p, o, doa, st, logits, gn, part, g)


def _conv_bwd_call(dcb, uc, u, p, dw, ln_g, ln_b, part):
    S = uc.shape[0]
    nb = S // TM
    hb = TM // HALO

    def body(dcb_ref, uc_ref, u_ref, uh_ref, cv_ref, cg_ref, dw_ref, g_ref, b_ref, part_ref,
             dp_ref, bsum_ref, ddw_ref, acc_ref, recv_ref, uext, dext, ush, dsh, send_sems, recv_sems):
        i = pl.program_id(0)
        start, finish = _chip_exchange(part_ref, recv_ref, send_sems, recv_sems)

        @pl.when(i == 0)
        def _():
            start()
            dext[TM:EXT, :] = jnp.zeros((EXT - TM, D), F32)
            uext[HALO + TM:EXT, :] = jnp.zeros((EXT - HALO - TM, D), F32)
            bsum_ref[...] = jnp.zeros_like(bsum_ref)
            ddw_ref[...] = jnp.zeros_like(ddw_ref)
            acc_ref[...] = jnp.zeros_like(acc_ref)

        first_tile = (nb - 1 - i) == 0
        uext[0:HALO, :] = jnp.where(first_tile, 0.0, uh_ref[...])
        uext[HALO:HALO + TM, :] = u_ref[...]
        _fill_shifted(uext, ush)

        for rb in range(TM // SUB):
            rs_ = slice(rb * SUB, (rb + 1) * SUB)
            xh, rs = _layernorm_stats(uc_ref[rs_, :])
            ln = xh * g_ref[...] + b_ref[...]
            dln = dcb_ref[rs_, :] * _dsilu(ln, _sig(ln))
            acc_ref[1:2, :] += _rowsum(dln * xh)
            acc_ref[2:3, :] += _rowsum(dln)
            dxh = dln * g_ref[...]
            duc = rs * (dxh - jnp.mean(dxh, axis=-1, keepdims=True)
                        - xh * jnp.mean(dxh * xh, axis=-1, keepdims=True))
            dext[rs_, :] = duc
            acc_ref[0:1, :] += _rowsum(duc)
        _fill_shifted(dext, dsh)

        for j in range(CONV_K):
            part = jnp.zeros((SUB, D), F32)
            for rb in range(TM // SUB):
                s0 = HALO - (CONV_K - 1) + j + rb * SUB
                part = part + dext[rb * SUB:(rb + 1) * SUB, :] * _window(uext, ush, s0, SUB)
            ddw_ref[j:j + 1, :] += _rowsum(part)

        for rb in range(TM // SUB):
            rs_ = slice(rb * SUB, (rb + 1) * SUB)
            du = jnp.zeros((SUB, D), F32)
            for j in range(CONV_K):
                s0 = rb * SUB + (CONV_K - 1) - j
                du = du + dw_ref[j:j + 1, :] * _window(dext, dsh, s0, SUB)
            cg = cg_ref[rs_, :]
            sg = _sig(cg)
            dcv = du * sg
            dcg = du * cv_ref[rs_, :] * (sg * (1.0 - sg))
            dp_ref[rs_, 0:D] = dcv.astype(BF16)
            dp_ref[rs_, D:2 * D] = dcg.astype(BF16)
            bsum_ref[:, 0:D] += _rowsum(dcv)
            bsum_ref[:, D:2 * D] += _rowsum(dcg)

        dext[TM:TM + HALO, :] = dext[0:HALO, :]

        @pl.when(i == nb - 1)
        def _():
            finish()

    rev = BS((TM, D), lambda i: (nb - 1 - i, 0))
    vec = BS((1, D), lambda i: (0, 0))
    return pl.pallas_call(
        body, name="conv_bwd", grid=(nb,),
        out_shape=(SDS((S, 2 * D), BF16), SDS((1, 2 * D), F32), SDS((32, D), F32), SDS((8, D), F32),
                   SDS((3,) + part.shape[1:], part.dtype)),
        in_specs=[rev, rev, rev, BS((HALO, D), lambda i: (jnp.maximum((nb - 1 - i) * hb - 1, 0), 0)),
                  BS((TM, D), lambda i: (nb - 1 - i, 4)), BS((TM, D), lambda i: (nb - 1 - i, 5)),
                  BS((CONV_K, D), lambda i: (0, 0)), vec, vec, BS(memory_space=pl.ANY)],
        out_specs=(BS((TM, 2 * D), lambda i: (nb - 1 - i, 0)), BS((1, 2 * D), lambda i: (0, 0)),
                   BS((32, D), lambda i: (0, 0)), BS((8, D), lambda i: (0, 0)), BS(memory_space=pl.ANY)),
        scratch_shapes=[pltpu.VMEM((EXT, D), F32), pltpu.VMEM((EXT, D), F32),
                        pltpu.VMEM((7, HALO + TM, D), F32), pltpu.VMEM((7, HALO + TM, D), F32)] + _exchange_sems(),
        compiler_params=_params("arbitrary"),
    )(dcb, uc, u, u, p, p, dw, ln_g, ln_b, part)


def _in_bwd_call(dp_hg, dp_cv, dp_gt, x, dx2, mod, pre_tm, wg, part, full_a, full_b):
    S = x.shape[0]

    def body(hg_ref, cv_ref, gt_ref, x_ref, dx2_ref, mod_ref, g_ref, w_hbm, part_ref, fa_in, fb_in,
             gx_ref, acc_ref, recv_ref, fa_out, fb_out, w_vmem, sem, send_sems, recv_sems, sa, ra, sb, rb):
        start, finish = _chip_exchange(part_ref, recv_ref, send_sems, recv_sems)
        start_a, finish_a = _join_exchange(fa_in, fa_out, sa, ra)
        start_b, finish_b = _join_exchange(fb_in, fb_out, sb, rb)

        @pl.when(pl.program_id(0) == 0)
        def _():
            start_a()
            start_b()
            start()
            _load_rows(w_hbm, w_vmem, sem, O_IN).wait()
            acc_ref[...] = jnp.zeros_like(acc_ref)

        dh = jnp.zeros((TM, D), F32)
        for k in range(IN_COLS // D):
            src, kk = ((hg_ref, k), (cv_ref, k - 4), (gt_ref, k - 6))[0 if k < 4 else (1 if k < 6 else 2)]
            dh = dh + _mm(src[:, kk * D:(kk + 1) * D], w_vmem[k // 2, (k % 2) * D:(k % 2 + 1) * D, :], NT)
        xv = x_ref[...]
        r = lax.rsqrt(jnp.mean(xv * xv, axis=-1, keepdims=True) + EPS)
        xn = xv * r
        yv = xn * g_ref[...]
        acc_ref[0:1, :] += _rowsum(dh)
        acc_ref[1:2, :] += _rowsum(dh * yv)
        dyv = dh * (1.0 + mod_ref[:, D:2 * D])
        acc_ref[2:3, :] += _rowsum(dyv * xn)
        dxn = dyv * g_ref[...]
        gx_ref[...] = dx2_ref[...] + r * (dxn - xn * jnp.mean(dxn * xn, axis=-1, keepdims=True))

        @pl.when(pl.program_id(0) == S // TM - 1)
        def _():
            finish_a()
            finish_b()
            finish()

    tile = BS((TM, D), lambda i: (i, 0))
    hbm = BS(memory_space=pl.ANY)
    return pl.pallas_call(
        body, name="in_bwd", grid=(S // TM,),
        out_shape=(SDS((S, D), F32), SDS((8, D), F32), SDS((3,) + part.shape[1:], part.dtype),
                   SDS(full_a.shape, full_a.dtype), SDS(full_b.shape, full_b.dtype)),
        in_specs=[BS((TM, 4 * D), lambda i: (i, 0)), BS((TM, 2 * D), lambda i: (i, 0)),
                  BS((TM, 2 * D), lambda i: (i, 0)), tile, tile, BS((1, 6 * D), lambda i: (0, 0)),
                  BS((1, D), lambda i: (0, 0)), hbm, hbm, hbm, hbm],
        out_specs=(tile, BS((8, D), lambda i: (0, 0)), hbm, hbm, hbm),
        scratch_shapes=[pltpu.VMEM((N_CHIPS, R_IN, D), BF16), pltpu.SemaphoreType.DMA] + _exchange_sems()
        + _join_sems() + _join_sems(),
        input_output_aliases={9: 3, 10: 4},
        compiler_params=_params("arbitrary"),
    )(dp_hg, dp_cv, dp_gt, x, dx2, mod, pre_tm, wg, part, full_a, full_b)


def _wgrad_call(gp, a, b, name, bm, place, rows):
    S, M = a.shape
    N = b.shape[1]
    bk = min(S, 1024)
    nk = S // bk

    def body(a_ref, b_ref, *rest):
        o_ref, acc = rest[-2], rest[-1]
        k = pl.program_id(2)

        @pl.when(k == 0)
        def _():
            acc[...] = jnp.zeros_like(acc)

        acc[...] += _mm(a_ref[...], b_ref[...], TN)

        @pl.when(k == nk - 1)
        def _():
            o_ref[...] = acc[...].astype(BF16)

    in_specs = [BS((bk, bm), lambda i, j, k: (k, i)), BS((bk, D), lambda i, j, k: (k, j))]
    args = [a, b]
    if gp is not None:
        in_specs.append(BS(memory_space=pl.ANY))
        args.append(gp)
    return pl.pallas_call(
        body, name=name, grid=(M // bm, N // D, nk),
        out_shape=SDS((N_CHIPS, rows, D), BF16),
        in_specs=in_specs,
        out_specs=BS((None, bm, D), lambda i, j, k: (*place(i, j), 0)),
        scratch_shapes=[pltpu.VMEM((bm, D), F32)],
        input_output_aliases={} if gp is None else {2: 0},
        compiler_params=_params("parallel", "parallel", "arbitrary"),
    )(*args)


def _wgrad_rows_call(gp, a, b, name, blk):
    S = a.shape[0]
    bk = min(S, 1024)
    nk = S // bk

    def body(a_ref, b_ref, *rest):
        o_ref, acc = rest[-2], rest[-1]
        k = pl.program_id(0)

        @pl.when(k == 0)
        def _():
            acc[...] = jnp.zeros_like(acc)

        acc[...] += _mm(a_ref[...], b_ref[...], TN)

        @pl.when(k == nk - 1)
        def _():
            for c in range(N_CHIPS):
                o_ref[c] = acc[c * R_BR:(c + 1) * R_BR, :].astype(BF16)

    in_specs = [BS((bk, D), lambda k: (k, 0)), BS((bk, D), lambda k: (k, 0))]
    args = [a, b]
    if gp is not None:
        in_specs.append(BS(memory_space=pl.ANY))
        args.append(gp)
    return pl.pallas_call(
        body, name=name, grid=(nk,),
        out_shape=SDS((N_CHIPS, 3 * R_BR, D), BF16),
        in_specs=in_specs,
        out_specs=BS((N_CHIPS, R_BR, D), lambda k: (0, blk, 0)),
        scratch_shapes=[pltpu.VMEM((D, D), F32)],
        input_output_aliases={} if gp is None else {2: 0},
        compiler_params=_params("arbitrary"),
    )(*args)


def _outer_call(cact, dmod):
    n = dmod.shape[1]

    def body(a_ref, b_ref, o_ref):
        o_ref[...] = _mm(a_ref[...], b_ref[...], TN, HI)

    return pl.pallas_call(
        body, name="wgrad_ada", out_shape=SDS((D, n), F32),
        compiler_params=pltpu.CompilerParams(vmem_limit_bytes=VMEM_LIMIT),
    )(cact, dmod)


def _adamw_call(w, g, m, v, name):
    R, C = w.shape
    tr = R
    while tr * C > 512 * 1024 and tr % 16 == 0:
        tr //= 2
    c1 = 1.0 - ADAM_B1 ** ADAM_STEP
    c2 = 1.0 - ADAM_B2 ** ADAM_STEP

    def body(w_ref, g_ref, m_ref, v_ref, d_ref, m2_ref, v2_ref):
        g = g_ref[...]
        m2 = ADAM_B1 * m_ref[...] + (1.0 - ADAM_B1) * g
        v2 = ADAM_B2 * v_ref[...] + (1.0 - ADAM_B2) * (g * g)
        m2_ref[...] = m2
        v2_ref[...] = v2
        d_ref[...] = -ADAM_LR * ((m2 / c1) / (jnp.sqrt(v2 / c2) + ADAM_EPS) + ADAM_WD * w_ref[...])

    tile = BS((tr, C), lambda i: (i, 0))
    return pl.pallas_call(
        body, name=name, grid=(R // tr,), out_shape=(SDS((R, C), F32),) * 3,
        in_specs=[tile] * 4, out_specs=(tile,) * 3, compiler_params=_params("parallel"),
    )(w, g, m, v)


def _adamw_gather_call(w, g, m, v, srows, name):
    R, C = w.shape
    tr = R
    while tr * C > 512 * 1024 and tr % 16 == 0:
        tr //= 2
    nsteps = R // tr
    mr = srows.shape[0]
    c1 = 1.0 - ADAM_B1 ** ADAM_STEP
    c2 = 1.0 - ADAM_B2 ** ADAM_STEP

    def body(w_ref, g_ref, m_ref, v_ref, s_ref, d_ref, m2_ref, v2_ref, all_ref, sum_ref,
             x_scr, out_scr, send_sems, recv_sems, local_sem):
        i = pl.program_id(0)
        start, finish = _allgather_parts(x_scr, out_scr, send_sems, recv_sems, local_sem)

        @pl.when(i == 0)
        def _():
            x_scr[...] = s_ref[...]
            start()

        g = g_ref[...]
        m2 = ADAM_B1 * m_ref[...] + (1.0 - ADAM_B1) * g
        v2 = ADAM_B2 * v_ref[...] + (1.0 - ADAM_B2) * (g * g)
        m2_ref[...] = m2
        v2_ref[...] = v2
        d_ref[...] = -ADAM_LR * ((m2 / c1) / (jnp.sqrt(v2 / c2) + ADAM_EPS) + ADAM_WD * w_ref[...])

        @pl.when(i == nsteps - 1)
        def _():
            finish()
            all_ref[...] = out_scr[...]
            acc = out_scr[0:mr, :]
            for d in range(1, N_DEV):
                acc = acc + out_scr[d * mr:(d + 1) * mr, :]
            sum_ref[...] = acc

    tile = BS((tr, C), lambda i: (i, 0))
    return pl.pallas_call(
        body, name=name, grid=(nsteps,),
        out_shape=(SDS((R, C), F32),) * 3 + (SDS((N_DEV * mr, D), F32), SDS((mr, D), F32)),
        in_specs=[tile] * 4 + [BS((mr, D), lambda i: (0, 0))],
        out_specs=(tile,) * 3 + (BS((N_DEV * mr, D), lambda i: (0, 0)), BS((mr, D), lambda i: (0, 0))),
        scratch_shapes=[pltpu.VMEM((mr, D), F32), pltpu.VMEM((N_DEV * mr, D), F32)] + _allgather_sems(),
        compiler_params=_params("arbitrary"),
    )(w, g, m, v, srows)


def _rs_begin(g, c_idx, tag):
    n = g.shape[1]
    g = g.reshape(N_CHIPS, 2, n // 2, D)
    return _add_halves_call(g, _sibling_halves_call(g, tag), c_idx, tag)


def _rs_end(part, recv, c_idx, chip_idx, tag):
    n = 2 * part.shape[1]
    full = _add_chips_call(part, recv, jnp.concatenate([chip_idx, c_idx]), tag)
    return _sibling_join_call(full, tag).reshape(n, D)


def _local_step(x, mod, cact, target, wg, pack, small, c_idx, chip_idx):
    p, h1, wg = _fwd_in_call(x, mod, small["pre_tm"], wg, small["b_in"], pack, small["order"])
    o, oa, st, wg = _hgrn_fwd_call(p, small["logits"], small["hg_norm"], wg, pack)
    u, uc, cb, wg = _conv_fwd_call(p, small["conv_dw"], small["conv_db"], small["ln_g"], small["ln_b"], wg, pack)
    ya, yb, mg, y, x2, h2 = _merge_fwd_call(oa, cb, p, x, mod, small["post_tm"], small["pre_cm"], wg)
    z, da, dy2, dx2, acc_f = _ffn_call(h2, x2, target, mod, small["post_cm"], small["pre_cm"], wg)

    g_ff = _wgrad_call(None, h2, da, "wgrad_ff1", D, lambda i, j: (j, 0), 2 * R_FF)
    g_ff = _wgrad_call(g_ff, z, dy2, "wgrad_ff2", D, lambda i, j: (i, 1), 2 * R_FF)
    g_ff = g_ff.reshape(N_CHIPS, 2, R_FF, D)
    dy, dya, dyb, doa, dcb, dp_gt, acc_m, bs_gt, hr_ff = _merge_bwd_call(dx2, y, ya, yb, p, mod, small["post_tm"],
                                                                        wg, g_ff)
    part_ff = _add_halves_call(g_ff, hr_ff, c_idx, "ff")

    g_br = _wgrad_rows_call(None, oa, dya, "wgrad_br_a", 0)
    g_br = _wgrad_rows_call(g_br, cb, dyb, "wgrad_br_b", 1)
    g_br = _wgrad_rows_call(g_br, mg, dy, "wgrad_out", 2)
    g_br = g_br.reshape(N_CHIPS, 2, 3 * R_BR // 2, D)
    dp_hg, bs_hg, dlg, dgn, recv_ff, hr_br = _hgrn_bwd_call(p, o, doa, st, small["logits"], small["hg_norm"],
                                                            part_ff, g_br)
    part_br = _add_halves_call(g_br, hr_br, c_idx, "br")
    dp_cv, bs_cv, ddw, acc_c, recv_br = _conv_bwd_call(dcb, uc, u, p, small["conv_dw"], small["ln_g"], small["ln_b"],
                                                        part_br)

    g_in = _wgrad_call(None, h1, dp_hg, "wgrad_in_hg", D, lambda i, j: (j // 2, j % 2), R_IN)
    g_in = _wgrad_call(g_in, h1, dp_cv, "wgrad_in_cv", D, lambda i, j: (2, j), R_IN)
    g_in = _wgrad_call(g_in, h1, dp_gt, "wgrad_in_gt", D, lambda i, j: (3, j), R_IN)
    part_in = _rs_begin(g_in, c_idx, "in")
    chip_c = jnp.concatenate([chip_idx, c_idx])
    full_ff = _add_chips_call(part_ff, recv_ff, chip_c, "ff")
    full_br = _add_chips_call(part_br, recv_br, chip_c, "br")
    gx, acc_i, recv_in, full_ff, full_br = _in_bwd_call(dp_hg, dp_cv, dp_gt, x, dx2, mod, small["pre_tm"], wg,
                                                        part_in, full_ff, full_br)
    red_ff = full_ff.reshape(2 * R_FF, D)
    red_br = full_br.reshape(3 * R_BR, D)
    red_in = _rs_end(part_in, recv_in, c_idx, chip_idx, "in")

    zrow = jnp.zeros((1, D), F32)
    rows = [acc_i[0:1], acc_i[1:2], acc_m[0:1], acc_f[2:3], acc_f[3:4], acc_f[0:1],
            acc_i[2:3], acc_m[1:2], acc_f[4:5], acc_f[1:2],
            jnp.concatenate([bs_hg, bs_cv, bs_gt], axis=1).reshape(8, D),
            dlg, dgn, acc_c[0:1], acc_c[1:2], acc_c[2:3],
            ddw,
            cact, acc_f[5:6]] + [zrow] * 6
    return gx, jnp.concatenate(rows, axis=0), red_in, red_ff, red_br


def kernel(x, c, w_ada, b_ada, pre_norm_tm, post_norm_tm, pre_norm_cm, post_norm_cm, w_in, b_in, hg_lb_logits, hg_norm, conv_dw, conv_db, conv_ln_g, conv_ln_b, w_br_a, w_br_b, w_out, w_ff1, w_ff2, loss_target, m_w_ada, m_b_ada, m_pre_norm_tm, m_post_norm_tm, m_pre_norm_cm, m_post_norm_cm, m_w_in, m_b_in, m_hg_lb_logits, m_hg_norm, m_conv_dw, m_conv_db, m_conv_ln_g, m_conv_ln_b, m_w_br_a, m_w_br_b, m_w_out, m_w_ff1, m_w_ff2, v_w_ada, v_b_ada, v_pre_norm_tm, v_post_norm_tm, v_pre_norm_cm, v_post_norm_cm, v_w_in, v_b_in, v_hg_lb_logits, v_hg_norm, v_conv_dw, v_conv_db, v_conv_ln_g, v_conv_ln_b, v_w_br_a, v_w_br_b, v_w_out, v_w_ff1, v_w_ff2):
    xi, yi, ci = lax.axis_index("x"), lax.axis_index("y"), lax.axis_index("c")
    chip = 2 * xi + yi
    c_idx = jnp.reshape(ci, (1,)).astype(jnp.int32)
    chip_idx = jnp.reshape(chip, (1,)).astype(jnp.int32)

    def pack_small(ada_b, pre_t, post_t, pre_c, post_c, in_b, lg, hgn, cdb, lng, lnb, cdw):
        flat = jnp.concatenate([cdw[0].reshape(-1), jnp.zeros((8 * D - CONV_K * 256,), F32)]).reshape(8, D)
        return jnp.concatenate([ada_b.reshape(6, D), pre_t, post_t, pre_c, post_c, in_b.reshape(8, D), lg, hgn,
                                cdb, lng, lnb, flat], axis=0)

    w_in_halves = w_in[0].reshape(D, 2, D).transpose(1, 0, 2).reshape(R_IN, D)
    pack = jnp.concatenate([w_in_halves, w_ff1[0], w_ff2[0], w_br_a[0], w_br_b[0], w_out[0]],
                           axis=0).astype(BF16)
    wg = lax.dynamic_update_slice(lax.empty((N_CHIPS, PACK_W, D), BF16), pack[None], (chip, 0, 0))
    wa = 6 * D // N_CHIPS
    me = 4 * xi + 2 * yi + ci
    dw_blk = jnp.concatenate([conv_dw[0].reshape(-1), jnp.zeros((8 * D - CONV_K * 256,), F32)]).reshape(8, D)
    dw_all, ca_all, mod_all = _prologue_call(
        dw_blk, jnp.broadcast_to(c, (8, D)), w_ada[0].astype(BF16),
        lax.dynamic_slice_in_dim(b_ada, chip * wa, wa, axis=1))
    order = jnp.stack([chip, 2 * (1 - xi) + yi, 2 * xi + (1 - yi), 2 * (1 - xi) + (1 - yi)]).astype(jnp.int32)
    dw_all = dw_all.reshape(N_CHIPS, 2, 8 * D)[:, 0, :CONV_K * 256].reshape(N_CHIPS, CONV_K, 256)
    dw_full = dw_all.transpose(1, 0, 2).reshape(CONV_K, D)
    cact = lax.dynamic_slice_in_dim(ca_all, me * 8, 1, axis=0)
    mod_mine = lax.dynamic_index_in_dim(mod_all.reshape(N_CHIPS, 2, N_DEV, 8, wa)[:, 0, :, 0, :], me, axis=1,
                                        keepdims=False)
    mod = mod_mine.reshape(1, 6 * D)

    small = dict(b_ada=b_ada, pre_tm=pre_norm_tm, post_tm=post_norm_tm, pre_cm=pre_norm_cm, post_cm=post_norm_cm,
                 b_in=b_in, logits=hg_lb_logits, hg_norm=hg_norm, conv_dw=dw_full, conv_db=conv_db,
                 ln_g=conv_ln_g, ln_b=conv_ln_b, order=order)

    gx, srows, red_in, red_ff, red_br = _local_step(x[0], mod, cact, loss_target[0], wg, pack, small, c_idx,
                                                    chip_idx)

    shapes = {"in": w_in.shape, "br_a": w_br_a.shape, "br_b": w_br_b.shape, "out": w_out.shape,
              "ff1": w_ff1.shape, "ff2": w_ff2.shape}
    offs = {"in": (red_in, 0, R_IN), "ff1": (red_ff, 0, R_FF), "ff2": (red_ff, R_FF, 2 * R_FF),
            "br_a": (red_br, 0, R_BR), "br_b": (red_br, R_BR, 2 * R_BR), "out": (red_br, 2 * R_BR, 3 * R_BR)}
    wmv = {"in": (w_in, m_w_in, v_w_in), "br_a": (w_br_a, m_w_br_a, v_w_br_a), "br_b": (w_br_b, m_w_br_b, v_w_br_b),
           "out": (w_out, m_w_out, v_w_out), "ff1": (w_ff1, m_w_ff1, v_w_ff1), "ff2": (w_ff2, m_w_ff2, v_w_ff2)}
    res = {}
    for n in offs:
        shp = shapes[n]
        g2d = offs[n][0][offs[n][1]:offs[n][2]]
        if n == "in":
            g2d = g2d.reshape(2, D, D).transpose(1, 0, 2)
        g2d = g2d.reshape(shp[1], shp[2])
        w_, m_, v_ = (a[0] for a in wmv[n])
        if n == "in":
            d_, m2_, v2_, sall, ssum = _adamw_gather_call(w_, g2d, m_, v_, srows, "adamw_in")
        else:
            d_, m2_, v2_ = _adamw_call(w_, g2d, m_, v_, "adamw_" + n)
        res[n] = tuple(a.reshape(shp) for a in (g2d, d_, m2_, v2_))

    sall = sall.reshape(N_DEV, SMALL_ROWS, D)
    loss = jnp.sum(ssum[57])
    dmod_all = sall[:, 0:6, :].reshape(N_DEV, 6 * D)
    g_ada = _outer_call(sall[:, 56, :], lax.dynamic_slice_in_dim(dmod_all, chip * wa, wa, axis=1))
    g_dw = lax.dynamic_slice_in_dim(ssum[24:24 + CONV_K], chip * 256, 256, axis=1)
    g_small = jnp.concatenate(
        [ssum[0:24], jnp.concatenate([g_dw.reshape(-1), jnp.zeros((8 * D - CONV_K * 256,), F32)]).reshape(8, D)],
        axis=0)
    d_, m2_, v2_ = _adamw_call(w_ada[0], g_ada, m_w_ada[0], v_w_ada[0], "adamw_ada")
    res["ada"] = tuple(a.reshape(w_ada.shape) for a in (g_ada, d_, m2_, v2_))

    ws = pack_small(b_ada, pre_norm_tm, post_norm_tm, pre_norm_cm, post_norm_cm, b_in, hg_lb_logits, hg_norm,
                    conv_db, conv_ln_g, conv_ln_b, conv_dw)
    ms = pack_small(m_b_ada, m_pre_norm_tm, m_post_norm_tm, m_pre_norm_cm, m_post_norm_cm, m_b_in, m_hg_lb_logits,
                    m_hg_norm, m_conv_db, m_conv_ln_g, m_conv_ln_b, m_conv_dw)
    vs = pack_small(v_b_ada, v_pre_norm_tm, v_post_norm_tm, v_pre_norm_cm, v_post_norm_cm, v_b_in, v_hg_lb_logits,
                    v_hg_norm, v_conv_db, v_conv_ln_g, v_conv_ln_b, v_conv_dw)
    sres = (g_small,) + tuple(_adamw_call(ws, g_small, ms, vs, "adamw_small"))

    def unpack_small(t):
        return {"b_ada": t[0:6].reshape(1, 6 * D), "pre_tm": t[6:7], "post_tm": t[7:8], "pre_cm": t[8:9],
                "post_cm": t[9:10], "b_in": t[10:18].reshape(1, IN_COLS), "logits": t[18:20], "hg_norm": t[20:21],
                "conv_db": t[21:22], "ln_g": t[22:23], "ln_b": t[23:24],
                "conv_dw": t[24:32].reshape(-1)[:CONV_K * 256].reshape(1, CONV_K, 256)}

    order = ["ada", "b_ada", "pre_tm", "post_tm", "pre_cm", "post_cm", "in", "b_in", "logits", "hg_norm", "conv_dw",
             "conv_db", "ln_g", "ln_b", "br_a", "br_b", "out", "ff1", "ff2"]
    outs = [loss, gx.reshape(x.shape)]
    for kind in range(4):
        sm = unpack_small(sres[kind])
        for n in order:
            outs.append(res[n][kind] if n in res else sm[n])
    return tuple(outs)
```

```python
import functools

import jax
import jax.numpy as jnp
from jax import lax
from jax.experimental import pallas as pl
from jax.experimental.pallas import tpu as pltpu

F32, BF16 = jnp.float32, jnp.bfloat16
SDS = jax.ShapeDtypeStruct
BS = pl.BlockSpec
MESH = pl.DeviceIdType.MESH
HI = lax.Precision.HIGHEST

D = 1024
D_FF = 4096
IN_COLS = 8192
HEADS, DK = 8, 128
CHUNK = 128
CONV_K = 31
HALO = 32
SUB = 32
EPS = 1e-6
N_CHIPS, N_DEV = 4, 8
TM = 256
TB = 256
VMEM_LIMIT = 56 * 1024 * 1024

R_IN, R_BR, R_FF = 2048, 256, 1024
PACK_W = R_IN + 3 * R_BR + 2 * R_FF
O_IN, O_FF1, O_FF2, O_BRA, O_BRB, O_OUT = 0, 2048, 3072, 4096, 4352, 4608
SMALL_ROWS = 64

ADAM_LR, ADAM_B1, ADAM_B2, ADAM_EPS, ADAM_WD, ADAM_STEP = 0.001, 0.9, 0.999, 1e-08, 0.01, 10

NN = (((1,), (0,)), ((), ()))
NT = (((1,), (1,)), ((), ()))
TN = (((0,), (0,)), ((), ()))


def _mm(a, b, dims=NN, precision=None):
    return lax.dot_general(a, b, dims, preferred_element_type=F32, precision=precision)


def _sig(v):
    return jax.nn.sigmoid(v)


def _dsilu(v, s):
    return s * (1.0 + v * (1.0 - s))


def _params(*sem):
    return pltpu.CompilerParams(dimension_semantics=sem if sem else None, vmem_limit_bytes=VMEM_LIMIT)


def _rowsum(v):
    return jnp.sum(v, axis=0, keepdims=True)


def _mesh_pos():
    return lax.axis_index("x"), lax.axis_index("y"), lax.axis_index("c")


def _allgather_parts(x_ref, out_ref, send_sems, recv_sems, local_sem):
    m_per = x_ref.shape[0]
    x, y, c = _mesh_pos()
    me, sibling = (x, y, c), (x, y, 1 - c)
    chips = [(1 - x, y), (x, 1 - y), (1 - x, 1 - y)]

    def rows(px, py, pc):
        return out_ref.at[pl.ds((4 * px + 2 * py + pc) * m_per, m_per), :]

    def copy(k, block, to, src=None):
        return pltpu.make_async_remote_copy(
            src_ref=rows(*block) if src is None else src, dst_ref=rows(*block),
            send_sem=send_sems.at[k], recv_sem=recv_sems.at[k], device_id=to, device_id_type=MESH)

    def first():
        return [copy(0, me, sibling, src=x_ref)] + [copy(1 + j, me, (*chip, c), src=x_ref)
                                                    for j, chip in enumerate(chips)]

    def start():
        pltpu.make_async_copy(x_ref, rows(*me), local_sem).start()
        for cp in first():
            cp.start()

    def finish():
        passed = [copy(4 + j, (*chip, c), sibling) for j, chip in enumerate(chips)]
        for j, chip in enumerate(chips):
            copy(1 + j, (*chip, c), me).wait_recv()
            passed[j].start()
        copy(0, sibling, me).wait_recv()
        for j, chip in enumerate(chips):
            copy(4 + j, (*chip, 1 - c), me).wait_recv()
        for cp in first() + passed:
            cp.wait_send()
        pltpu.make_async_copy(x_ref, rows(*me), local_sem).wait()

    return start, finish


def _allgather(x_ref, out_ref, send_sems, recv_sems, local_sem):
    start, finish = _allgather_parts(x_ref, out_ref, send_sems, recv_sems, local_sem)
    start()
    finish()


def _allgather_sems():
    return [pltpu.SemaphoreType.DMA((7,)), pltpu.SemaphoreType.DMA((7,)), pltpu.SemaphoreType.DMA]


def _allgather_call(blk, name, in_vmem, with_sum):
    m_per, n = blk.shape

    def body(x_ref, out_ref, *rest):
        if with_sum:
            sum_ref, send_sems, recv_sems, local_sem = rest
        else:
            send_sems, recv_sems, local_sem = rest
        _allgather(x_ref, out_ref, send_sems, recv_sems, local_sem)
        if with_sum:
            acc = out_ref[0:m_per, :]
            for d in range(1, N_DEV):
                acc = acc + out_ref[d * m_per:(d + 1) * m_per, :]
            sum_ref[...] = acc

    space = pltpu.VMEM if in_vmem else pl.ANY
    out_shape = [SDS((N_DEV * m_per, n), blk.dtype)]
    out_specs = [BS(memory_space=space)]
    if with_sum:
        out_shape.append(SDS((m_per, n), blk.dtype))
        out_specs.append(BS(memory_space=pltpu.VMEM))
    return pl.pallas_call(
        body, name=name, out_shape=out_shape, in_specs=[BS(memory_space=space)], out_specs=out_specs,
        scratch_shapes=[pltpu.SemaphoreType.DMA((7,)), pltpu.SemaphoreType.DMA((7,)), pltpu.SemaphoreType.DMA],
        compiler_params=pltpu.CompilerParams(vmem_limit_bytes=VMEM_LIMIT),
    )(blk)


def _gather_sems(n_ranges):
    return [pltpu.SemaphoreType.DMA((6 * n_ranges,)), pltpu.SemaphoreType.DMA((6 * n_ranges,))]


def _pack_gather(pack_ref, wg_ref, send_sems, recv_sems, ranges):
    x, y, c = _mesh_pos()
    me, sibling = (x, y, c), (x, y, 1 - c)
    chips = [(1 - x, y), (x, 1 - y), (1 - x, 1 - y)]

    def land(r, px, py, pc):
        off, n = ranges[r]
        return wg_ref.at[2 * px + py, pl.ds(off + pc * (n // 2), n // 2), :]

    def mine(r):
        off, n = ranges[r]
        return pack_ref.at[pl.ds(off + c * (n // 2), n // 2), :]

    def copy(r, k, block, to, src=None):
        return pltpu.make_async_remote_copy(
            src_ref=land(r, *block) if src is None else src, dst_ref=land(r, *block),
            send_sem=send_sems.at[6 * r + k], recv_sem=recv_sems.at[6 * r + k], device_id=to, device_id_type=MESH)

    def start():
        for r in range(len(ranges)):
            for j, chip in enumerate(chips):
                copy(r, j, me, (*chip, c), src=mine(r)).start()

    def arrive(j):
        for r in range(len(ranges)):
            copy(r, j, (*chips[j], c), me).wait_recv()
            copy(r, 3 + j, (*chips[j], c), sibling).start()
        for r in range(len(ranges)):
            copy(r, 3 + j, (*chips[j], 1 - c), me).wait_recv()

    def drain():
        for r in range(len(ranges)):
            for j, chip in enumerate(chips):
                copy(r, j, me, (*chip, c), src=mine(r)).wait_send()
                copy(r, 3 + j, (*chip, c), sibling).wait_send()

    def finish():
        for r in range(len(ranges)):
            for j, chip in enumerate(chips):
                copy(r, j, (*chip, c), me).wait_recv()
                copy(r, 3 + j, (*chip, c), sibling).start()
        for r in range(len(ranges)):
            for j, chip in enumerate(chips):
                copy(r, 3 + j, (*chip, 1 - c), me).wait_recv()
        drain()

    return start, finish, arrive, drain


def _relay_sems():
    return [pltpu.SemaphoreType.DMA((8,)), pltpu.SemaphoreType.DMA((8,))]


def _relay_gather(pack_ref, wg_ref, send_sems, recv_sems, off, n):
    x, y, c = _mesh_pos()
    me, sibling = (x, y, c), (x, y, 1 - c)
    chips = [(1 - x, y), (x, 1 - y), (1 - x, 1 - y)]
    h, q = n // 2, n // 4

    def land(px, py, pc, piece=None):
        if piece is None:
            return wg_ref.at[2 * px + py, pl.ds(off + pc * h, h), :]
        return wg_ref.at[2 * px + py, pl.ds(off + pc * h + piece * q, q), :]

    def copy(k, ref, to, src=None):
        return pltpu.make_async_remote_copy(
            src_ref=ref if src is None else src, dst_ref=ref, send_sem=send_sems.at[k], recv_sem=recv_sems.at[k],
            device_id=to, device_id_type=MESH)

    def direct(j):
        return copy(j, land(x, y, c), (*chips[j], c), src=pack_ref.at[pl.ds(off + c * h, h), :])

    def relayed(j):
        if j == 0:
            return copy(6, land(*chips[0], c, 1), (x, 1 - y, c))
        return copy(7, land(*chips[1], c, 0), (1 - x, y, c))

    def start():
        direct(0).start()
        direct(1).start()

    def arrive(j):
        if j < 2:
            copy(j, land(*chips[j], c), me).wait_recv()
            relayed(j).start()
        else:
            copy(7, land(*chips[2], c, 0), me).wait_recv()
            copy(6, land(*chips[2], c, 1), me).wait_recv()
        copy(3 + j, land(*chips[j], c), sibling).start()
        copy(3 + j, land(*chips[j], 1 - c), me).wait_recv()

    def drain():
        for j in range(2):
            direct(j).wait_send()
            relayed(j).wait_send()
        for j in range(3):
            copy(3 + j, land(*chips[j], c), sibling).wait_send()

    return start, arrive, drain


def _prologue_call(dw_blk, c_blk, w_ada, b_ada):
    wa = w_ada.shape[1]

    def body(dw_ref, c_ref, wa_ref, ba_ref, dwg_ref, ca_ref, modg_ref,
             cg_scr, part_scr, s1, r1, l1, s2, r2, l2, s3, r3, l3):
        start_c, finish_c = _allgather_parts(c_ref, cg_scr, s2, r2, l2)
        start_dw, finish_dw = _allgather_parts(dw_ref, dwg_ref, s1, r1, l1)
        start_mod, finish_mod = _allgather_parts(part_scr, modg_ref, s3, r3, l3)
        start_c()
        start_dw()
        finish_c()
        cv = cg_scr[...]
        ca = cv * _sig(cv)
        ca_ref[...] = ca
        part_scr[...] = _mm(ca.astype(BF16), wa_ref[...]) + ba_ref[...]
        start_mod()
        finish_dw()
        finish_mod()

    vm = BS(memory_space=pltpu.VMEM)
    return pl.pallas_call(
        body, name="prologue_adaln_conv_dw",
        out_shape=(SDS((N_DEV * 8, D), F32), SDS((N_DEV * 8, D), F32), SDS((N_DEV * N_DEV * 8, wa), F32)),
        in_specs=[vm, vm, vm, vm], out_specs=(vm, vm, vm),
        scratch_shapes=[pltpu.VMEM((N_DEV * 8, D), F32), pltpu.VMEM((N_DEV * 8, wa), F32)]
        + _allgather_sems() + _allgather_sems() + _allgather_sems(),
        compiler_params=pltpu.CompilerParams(vmem_limit_bytes=VMEM_LIMIT),
    )(dw_blk, c_blk, w_ada, b_ada)


def _halves_exchange(g_ref, out_ref, send_sems, recv_sems):
    x, y, c = _mesh_pos()

    def copies():
        return [pltpu.make_async_remote_copy(
            src_ref=g_ref.at[k, 1 - c], dst_ref=out_ref.at[k], send_sem=send_sems.at[k], recv_sem=recv_sems.at[k],
            device_id=(x, y, 1 - c), device_id_type=MESH) for k in range(N_CHIPS)]

    def start():
        for cp in copies():
            cp.start()

    def finish():
        for cp in copies():
            cp.wait()

    return start, finish


def _halves_sems():
    return [pltpu.SemaphoreType.DMA((N_CHIPS,)), pltpu.SemaphoreType.DMA((N_CHIPS,))]


def _sibling_halves_call(g, tag):
    _, _, h, n = g.shape

    def body(g_ref, out_ref, send_sems, recv_sems):
        start, finish = _halves_exchange(g_ref, out_ref, send_sems, recv_sems)
        start()
        finish()

    return pl.pallas_call(
        body, name="rs_sibling_halves_" + tag, out_shape=SDS((N_CHIPS, h, n), g.dtype),
        in_specs=[BS(memory_space=pl.ANY)], out_specs=BS(memory_space=pl.ANY),
        scratch_shapes=_halves_sems(),
    )(g)


def _chip_exchange(p_ref, out_ref, send_sems, recv_sems):
    x, y, c = _mesh_pos()
    chips = [(1 - x, y), (x, 1 - y), (1 - x, 1 - y)]

    def copies():
        return [pltpu.make_async_remote_copy(
            src_ref=p_ref.at[2 * cx + cy], dst_ref=out_ref.at[j], send_sem=send_sems.at[j], recv_sem=recv_sems.at[j],
            device_id=(cx, cy, c), device_id_type=MESH) for j, (cx, cy) in enumerate(chips)]

    def start():
        for cp in copies():
            cp.start()

    def finish():
        for cp in copies():
            cp.wait()

    return start, finish


def _exchange_sems():
    return [pltpu.SemaphoreType.DMA((3,)), pltpu.SemaphoreType.DMA((3,))]


def _join_exchange(in_ref, out_ref, send_sems, recv_sems):
    h = in_ref.shape[1]
    q = h // 4
    x, y, c = _mesh_pos()

    def copy(k, half):
        return pltpu.make_async_remote_copy(
            src_ref=in_ref.at[half, pl.ds(k * q, q)], dst_ref=out_ref.at[half, pl.ds(k * q, q)],
            send_sem=send_sems.at[k], recv_sem=recv_sems.at[k],
            device_id=(x, y, 1 - c), device_id_type=MESH)

    def start():
        for k in range(4):
            copy(k, c).start()

    def finish():
        for k in range(4):
            copy(k, c).wait_send()
            copy(k, 1 - c).wait_recv()

    return start, finish


def _join_sems():
    return [pltpu.SemaphoreType.DMA((4,)), pltpu.SemaphoreType.DMA((4,))]


def _sibling_join_call(full, tag):
    def body(in_ref, out_ref, send_sems, recv_sems):
        start, finish = _join_exchange(in_ref, out_ref, send_sems, recv_sems)
        start()
        finish()

    return pl.pallas_call(
        body, name="rs_sibling_join_" + tag, out_shape=SDS(full.shape, full.dtype),
        in_specs=[BS(memory_space=pl.ANY)], out_specs=BS(memory_space=pl.ANY),
        scratch_shapes=_join_sems(), input_output_aliases={0: 0},
    )(full)


def _add_halves_call(g, recv, c_idx, tag):
    _, _, h, n = g.shape
    tr = h // 2

    def body(c_ref, g_ref, r_ref, o_ref):
        o_ref[...] = (g_ref[...].astype(F32) + r_ref[...].astype(F32)).astype(BF16)

    return pl.pallas_call(
        body, name="rs_add_halves_" + tag, out_shape=SDS((N_CHIPS, h, n), BF16),
        grid_spec=pltpu.PrefetchScalarGridSpec(
            num_scalar_prefetch=1, grid=(N_CHIPS, 2),
            in_specs=[BS((None, None, tr, n), lambda k, r, c_ref: (k, c_ref[0], r, 0)),
                      BS((None, tr, n), lambda k, r, c_ref: (k, r, 0))],
            out_specs=BS((None, tr, n), lambda k, r, c_ref: (k, r, 0))),
        compiler_params=_params("arbitrary", "arbitrary"),
    )(c_idx, g, recv)


def _add_chips_call(p, recv, chip_c_idx, tag):
    _, h, n = p.shape
    tr = h // 2

    def body(k_ref, p_ref, r_ref, o_ref):
        acc = p_ref[...].astype(F32)
        for j in range(3):
            acc = acc + r_ref[j].astype(F32)
        o_ref[...] = acc

    return pl.pallas_call(
        body, name="rs_add_chips_" + tag, out_shape=SDS((2, h, n), F32),
        grid_spec=pltpu.PrefetchScalarGridSpec(
            num_scalar_prefetch=1, grid=(2,),
            in_specs=[BS((None, tr, n), lambda r, k_ref: (k_ref[0], r, 0)),
                      BS((3, tr, n), lambda r, k_ref: (0, r, 0))],
            out_specs=BS((None, tr, n), lambda r, k_ref: (k_ref[1], r, 0))),
        compiler_params=_params("arbitrary"),
    )(chip_c_idx, p, recv)


def _load_rows(wg_hbm, w_vmem, sem, off):
    cp = pltpu.make_async_copy(wg_hbm.at[:, pl.ds(off, w_vmem.shape[1]), :], w_vmem, sem)
    cp.start()
    return cp


def _fwd_in_call(x, mod, pre_tm, wg, b_in, pack, order):
    S = x.shape[0]
    tmf = 2 * TM
    nt = S // tmf
    wc = IN_COLS // N_CHIPS

    def body(ord_ref, x_ref, mod_ref, g_ref, w_hbm, b_ref, pack_ref, p_ref, h_hbm, wg_out, w_vmem, h_scr, sem,
             send_sems, recv_sems, send_sems2, recv_sems2):
        q, i = pl.program_id(0), pl.program_id(1)
        rows = pl.ds(pl.multiple_of(i * tmf, tmf), tmf)
        start, arrive, drain = _relay_gather(pack_ref, wg_out, send_sems, recv_sems, O_IN, R_IN)
        start2, finish2, _, _ = _pack_gather(pack_ref, wg_out, send_sems2, recv_sems2, [(O_BRA, 3 * R_BR)])

        def load_weights():
            cp = pltpu.make_async_copy(wg_out.at[ord_ref[q], pl.ds(O_IN, R_IN), :], w_vmem, sem)
            cp.start()
            cp.wait()

        @pl.when((q == 0) & (i == 0))
        def _():
            start()
            load_weights()

        for j in range(3):
            @pl.when((q == j + 1) & (i == 0))
            def _(j=j):
                arrive(j)
                if j == 1:
                    start2()
                load_weights()

        @pl.when(q == 0)
        def _():
            xv = x_ref[...]
            r = lax.rsqrt(jnp.mean(xv * xv, axis=-1, keepdims=True) + EPS)
            h = xv * r * g_ref[...] * (1.0 + mod_ref[:, D:2 * D]) + mod_ref[:, 0:D]
            h_scr[rows, :] = h.astype(BF16)

        hb = h_scr[rows, :]
        for k in range(wc // D):
            p_ref[:, k * D:(k + 1) * D] = _mm(hb, w_vmem[k * D:(k + 1) * D, :]) + b_ref[:, k * D:(k + 1) * D]

        @pl.when((q == N_CHIPS - 1) & (i == nt - 1))
        def _():
            cp = pltpu.make_async_copy(h_scr, h_hbm, sem)
            cp.start()
            drain()
            finish2()
            cp.wait()

    hbm = BS(memory_space=pl.ANY)
    return pl.pallas_call(
        body, name="fwd_in", out_shape=(SDS((S, IN_COLS), F32), SDS((S, D), BF16), SDS(wg.shape, wg.dtype)),
        grid_spec=pltpu.PrefetchScalarGridSpec(
            num_scalar_prefetch=1, grid=(N_CHIPS, nt),
            in_specs=[BS((tmf, D), lambda q, i, o: (jnp.where(q == 0, i, nt - 1), 0)),
                      BS((1, 6 * D), lambda q, i, o: (0, 0)),
                      BS((1, D), lambda q, i, o: (0, 0)), hbm, BS((1, wc), lambda q, i, o: (0, o[q])), hbm],
            out_specs=(BS((tmf, wc), lambda q, i, o: (i, o[q])), hbm, hbm),
            scratch_shapes=[pltpu.VMEM((R_IN, D), BF16), pltpu.VMEM((S, D), BF16), pltpu.SemaphoreType.DMA]
            + _relay_sems() + _gather_sems(1)),
        input_output_aliases={4: 2},
        compiler_params=_params("arbitrary", "arbitrary"),
    )(order, x, mod, pre_tm, wg, b_in, pack)


def _lower_bound(lg_ref):
    l0, l1 = lg_ref[0:1, :], lg_ref[1:2, :]
    mx = jnp.maximum(l0, l1)
    e0, e1 = jnp.exp(l0 - mx), jnp.exp(l1 - mx)
    return e0 / (e0 + e1)


def _tri_masks():
    ri = lax.broadcasted_iota(jnp.int32, (CHUNK, CHUNK), 0)
    ci = lax.broadcasted_iota(jnp.int32, (CHUNK, CHUNK), 1)
    return (ri >= ci).astype(F32), (ci >= ri).astype(F32)


def _cumsum_mm(tri, g):
    tb = tri.astype(BF16)
    hi = g.astype(BF16)
    r1 = g - hi.astype(F32)
    mid = r1.astype(BF16)
    lo = (r1 - mid.astype(F32)).astype(BF16)
    return _mm(tb, hi) + _mm(tb, mid) + _mm(tb, lo)


def _hg_gates(q_r, f_r, lb, tril):
    sq = _sig(q_r)
    q = q_r * sq
    sf = _sig(f_r)
    f = lb + (1.0 - lb) * sf
    k = 1.0 - f
    g = jnp.log(f)
    b = _cumsum_mm(tril, g)
    b_last = _rowsum(g)
    row = lax.broadcasted_iota(jnp.int32, g.shape, 0)
    ref = _rowsum(jnp.where(row < CHUNK // 2, g, 0.0))
    e = jnp.exp(b)
    eq = jnp.exp(jnp.minimum(b - ref, 80.0))
    ek = jnp.exp(jnp.minimum(ref - b, 80.0))
    dd = jnp.exp(b_last - b)
    return dict(sq=sq, q=q, sf=sf, f=f, k=k, e=e, eq=eq, ek=ek, dd=dd, elast=jnp.exp(b_last),
                qe=q * e, qt=q * eq, kt=k * ek, kd=k * dd)


def _hgrn_fwd_call(p, logits, gn, wg, pack):
    S = p.shape[0]
    ncb = TB // CHUNK
    ranges = [(O_FF1, R_FF)]

    def body(q_ref, f_ref, v_ref, og_ref, lg_ref, gn_ref, wg_in, pack_ref, o_ref, oa_ref, st_ref, wg_out,
             st_scr, send_sems, recv_sems):
        start, finish, _, _ = _pack_gather(pack_ref, wg_out, send_sems, recv_sems, ranges)

        @pl.when(pl.program_id(0) == 0)
        def _():
            start()
            st_scr[...] = jnp.zeros_like(st_scr)

        lb = _lower_bound(lg_ref)
        tril, _ = _tri_masks()

        def chunk(ci, carry):
            rows = pl.ds(pl.multiple_of(ci * CHUNK, CHUNK), CHUNK)
            st_ref[ci] = st_scr[...]
            t = _hg_gates(q_ref[rows, :], f_ref[rows, :], lb, tril)
            v = v_ref[rows, :]
            for h in range(HEADS):
                sl = slice(h * DK, (h + 1) * DK)
                stp = st_scr[:, sl]
                vb = v[:, sl].astype(BF16)
                inter = _mm(t["qe"][:, sl].astype(BF16), stp.astype(BF16), NT)
                a = jnp.where(tril > 0.5, _mm(t["qt"][:, sl].astype(BF16), t["kt"][:, sl].astype(BF16), NT), 0.0)
                o = inter + _mm(a.astype(BF16), vb)
                st_scr[:, sl] = stp * t["elast"][:, sl] + _mm(vb, t["kd"][:, sl].astype(BF16), TN)
                oh = o * lax.rsqrt(jnp.mean(o * o, axis=-1, keepdims=True) + EPS)
                og = og_ref[rows, sl]
                o_ref[rows, sl] = o
                oa_ref[rows, sl] = (oh * gn_ref[:, sl] * (og * _sig(og))).astype(BF16)
            return carry

        lax.fori_loop(0, ncb, chunk, 0)

        @pl.when(pl.program_id(0) == S // TB - 1)
        def _():
            finish()

    col = lambda j: BS((TB, D), lambda i, j=j: (i, j))
    hbm = BS(memory_space=pl.ANY)
    return pl.pallas_call(
        body, name="hgrn_fwd", grid=(S // TB,),
        out_shape=(SDS((S, D), F32), SDS((S, D), BF16), SDS((S // CHUNK, DK, D), F32), SDS(wg.shape, wg.dtype)),
        in_specs=[col(0), col(1), col(2), col(3), BS((2, D), lambda i: (0, 0)), BS((1, D), lambda i: (0, 0)),
                  hbm, hbm],
        out_specs=(BS((TB, D), lambda i: (i, 0)), BS((TB, D), lambda i: (i, 0)),
                   BS((ncb, DK, D), lambda i: (i, 0, 0)), hbm),
        scratch_shapes=[pltpu.VMEM((DK, D), F32)] + _gather_sems(len(ranges)),
        input_output_aliases={6: 3},
        compiler_params=_params("arbitrary"),
    )(p, p, p, p, logits, gn, wg, pack)


def _layernorm_stats(uc):
    mu = jnp.mean(uc, axis=-1, keepdims=True)
    xc = uc - mu
    rs = lax.rsqrt(jnp.mean(xc * xc, axis=-1, keepdims=True) + EPS)
    return xc * rs, rs


EXT = HALO + TM + 8


def _fill_shifted(ext, shifted):
    for m in range(1, 8):
        shifted[m - 1] = ext[m:m + HALO + TM, :]


def _window(ext, shifted, s0, n):
    m = s0 % 8
    q = s0 - m
    return ext[q:q + n, :] if m == 0 else shifted[m - 1, q:q + n, :]


def _conv_fwd_call(p, dw, db, ln_g, ln_b, wg, pack):
    S = p.shape[0]
    ranges = [(O_FF2, R_FF)]

    def body(cv_ref, cg_ref, dw_ref, db_ref, g_ref, b_ref, wg_in, pack_ref, u_ref, uc_ref, cb_ref, wg_out,
             uext, ush, send_sems, recv_sems):
        start, finish, _, _ = _pack_gather(pack_ref, wg_out, send_sems, recv_sems, ranges)

        @pl.when(pl.program_id(0) == 0)
        def _():
            start()
            uext[0:HALO, :] = jnp.zeros((HALO, D), F32)
            uext[HALO + TM:EXT, :] = jnp.zeros((EXT - HALO - TM, D), F32)

        u = cv_ref[...] * _sig(cg_ref[...])
        uext[HALO:HALO + TM, :] = u
        u_ref[...] = u
        _fill_shifted(uext, ush)
        for rb in range(TM // SUB):
            acc = jnp.broadcast_to(db_ref[...], (SUB, D))
            for j in range(CONV_K):
                s0 = HALO - (CONV_K - 1) + j + rb * SUB
                acc = acc + dw_ref[j:j + 1, :] * _window(uext, ush, s0, SUB)
            uc_ref[rb * SUB:(rb + 1) * SUB, :] = acc
            xh, _ = _layernorm_stats(acc)
            ln = xh * g_ref[...] + b_ref[...]
            cb_ref[rb * SUB:(rb + 1) * SUB, :] = (ln * _sig(ln)).astype(BF16)
        uext[0:HALO, :] = uext[TM:TM + HALO, :]

        @pl.when(pl.program_id(0) == S // TM - 1)
        def _():
            finish()

    vec = BS((1, D), lambda i: (0, 0))
    hbm = BS(memory_space=pl.ANY)
    return pl.pallas_call(
        body, name="conv_fwd", grid=(S // TM,),
        out_shape=(SDS((S, D), F32), SDS((S, D), F32), SDS((S, D), BF16), SDS(wg.shape, wg.dtype)),
        in_specs=[BS((TM, D), lambda i: (i, 4)), BS((TM, D), lambda i: (i, 5)),
                  BS((CONV_K, D), lambda i: (0, 0)), vec, vec, vec, hbm, hbm],
        out_specs=(BS((TM, D), lambda i: (i, 0)),) * 3 + (hbm,),
        scratch_shapes=[pltpu.VMEM((EXT, D), F32), pltpu.VMEM((7, HALO + TM, D), F32)] + _gather_sems(len(ranges)),
        input_output_aliases={6: 3},
        compiler_params=_params("arbitrary"),
    )(p, p, dw, db, ln_g, ln_b, wg, pack)


def _mm_rows(a, w_ref):
    acc = _mm(a[:, 0:R_BR], w_ref[0])
    for k in range(1, N_CHIPS):
        acc = acc + _mm(a[:, k * R_BR:(k + 1) * R_BR], w_ref[k])
    return acc


def _mm_rows_t(a, w_ref):
    return jnp.concatenate([_mm(a, w_ref[k], NT) for k in range(N_CHIPS)], axis=1)


def _br_spec(off):
    return BS((N_CHIPS, R_BR, D), lambda i: (0, off // R_BR, 0))


def _merge_fwd_call(oa, cb, p, x, mod, post_tm, pre_cm, wg):
    S = x.shape[0]

    def body(oa_ref, cb_ref, ga_ref, gb_ref, x_ref, mod_ref, post_ref, pre_ref, wa_ref, wb_ref, wo_ref,
             ya_ref, yb_ref, mg_ref, y_ref, x2_ref, h2_ref):
        ya = _mm_rows(oa_ref[...], wa_ref)
        yb = _mm_rows(cb_ref[...], wb_ref)
        ya_ref[...] = ya.astype(BF16)
        yb_ref[...] = yb.astype(BF16)
        mg = (_sig(ga_ref[...]) * ya + _sig(gb_ref[...]) * yb).astype(BF16)
        mg_ref[...] = mg
        y = _mm_rows(mg, wo_ref)
        y_ref[...] = y
        n = y * lax.rsqrt(jnp.mean(y * y, axis=-1, keepdims=True) + EPS) * post_ref[...]
        x2 = x_ref[...] + mod_ref[:, 2 * D:3 * D] * n
        x2_ref[...] = x2
        r2 = lax.rsqrt(jnp.mean(x2 * x2, axis=-1, keepdims=True) + EPS)
        h2 = x2 * r2 * pre_ref[...] * (1.0 + mod_ref[:, 4 * D:5 * D]) + mod_ref[:, 3 * D:4 * D]
        h2_ref[...] = h2.astype(BF16)

    tile = BS((TM, D), lambda i: (i, 0))
    vec = BS((1, D), lambda i: (0, 0))
    return pl.pallas_call(
        body, name="merge_fwd", grid=(S // TM,),
        out_shape=(SDS((S, D), BF16), SDS((S, D), BF16), SDS((S, D), BF16), SDS((S, D), F32), SDS((S, D), F32),
                   SDS((S, D), BF16)),
        in_specs=[tile, tile, BS((TM, D), lambda i: (i, 6)), BS((TM, D), lambda i: (i, 7)), tile,
                  BS((1, 6 * D), lambda i: (0, 0)), vec, vec, _br_spec(O_BRA), _br_spec(O_BRB), _br_spec(O_OUT)],
        out_specs=(tile,) * 6,
        compiler_params=_params("arbitrary"),
    )(oa, cb, p, p, x, mod, post_tm, pre_cm, wg, wg, wg)


def _ffn_call(h2, x2, target, mod, post_cm, pre_cm, wg):
    S = x2.shape[0]

    def body(h2_ref, x2_ref, t_ref, mod_ref, post_ref, pre_ref, w_hbm,
             z_ref, da_ref, dy2_ref, dx2_ref, acc_ref, w1_v, w2_v, ra_scr, sems):
        @pl.when(pl.program_id(0) == 0)
        def _():
            c1 = _load_rows(w_hbm, w1_v, sems.at[0], O_FF1)
            c2 = _load_rows(w_hbm, w2_v, sems.at[1], O_FF2)
            c1.wait()
            c2.wait()
            acc_ref[...] = jnp.zeros_like(acc_ref)

        h2 = h2_ref[...]
        for k in range(N_CHIPS):
            ra = jnp.maximum(_mm(h2, w1_v[k]), 0.0)
            ra_scr[:, k * D:(k + 1) * D] = ra
            z_ref[:, k * D:(k + 1) * D] = (ra * ra).astype(BF16)
        y2 = _mm(z_ref[:, 0:D], w2_v[0])
        for k in range(1, N_CHIPS):
            y2 = y2 + _mm(z_ref[:, k * D:(k + 1) * D], w2_v[k])
        ry = lax.rsqrt(jnp.mean(y2 * y2, axis=-1, keepdims=True) + EPS)
        yn = y2 * ry
        n = yn * post_ref[...]
        g2 = mod_ref[:, 5 * D:6 * D]
        x2 = x2_ref[...]
        err = x2 + g2 * n - t_ref[...]
        acc_ref[5:6, :] += _rowsum(err * err) * (0.5 / D)
        dout = err * (1.0 / D)
        acc_ref[0:1, :] += _rowsum(dout * n)
        dn = dout * g2
        acc_ref[1:2, :] += _rowsum(dn * yn)
        dyn = dn * post_ref[...]
        dy2 = (ry * (dyn - yn * jnp.mean(dyn * yn, axis=-1, keepdims=True))).astype(BF16)
        dy2_ref[...] = dy2
        for k in range(N_CHIPS):
            dz = _mm(dy2, w2_v[k], NT)
            da_ref[:, k * D:(k + 1) * D] = (dz * (2.0 * ra_scr[:, k * D:(k + 1) * D])).astype(BF16)
        dh2 = jnp.zeros((TM, D), F32)
        for k in range(N_CHIPS):
            dh2 = dh2 + _mm(da_ref[:, k * D:(k + 1) * D], w1_v[k], NT)
        r2 = lax.rsqrt(jnp.mean(x2 * x2, axis=-1, keepdims=True) + EPS)
        xn = x2 * r2
        yv = xn * pre_ref[...]
        acc_ref[2:3, :] += _rowsum(dh2)
        acc_ref[3:4, :] += _rowsum(dh2 * yv)
        dyv = dh2 * (1.0 + mod_ref[:, 4 * D:5 * D])
        acc_ref[4:5, :] += _rowsum(dyv * xn)
        dxn = dyv * pre_ref[...]
        dx2_ref[...] = dout + r2 * (dxn - xn * jnp.mean(dxn * xn, axis=-1, keepdims=True))

    tile = BS((TM, D), lambda i: (i, 0))
    wide = BS((TM, D_FF), lambda i: (i, 0))
    vec = BS((1, D), lambda i: (0, 0))
    return pl.pallas_call(
        body, name="ffn_fwd_bwd", grid=(S // TM,),
        out_shape=(SDS((S, D_FF), BF16), SDS((S, D_FF), BF16), SDS((S, D), BF16), SDS((S, D), F32),
                   SDS((8, D), F32)),
        in_specs=[tile, tile, tile, BS((1, 6 * D), lambda i: (0, 0)), vec, vec, BS(memory_space=pl.ANY)],
        out_specs=(wide, wide, tile, tile, BS((8, D), lambda i: (0, 0))),
        scratch_shapes=[pltpu.VMEM((N_CHIPS, R_FF, D), BF16), pltpu.VMEM((N_CHIPS, R_FF, D), BF16),
                        pltpu.VMEM((TM, D_FF), F32),
                        pltpu.SemaphoreType.DMA((2,))],
        compiler_params=_params("arbitrary"),
    )(h2, x2, target, mod, post_cm, pre_cm, wg)


def _merge_bwd_call(dx2, y, ya, yb, p, mod, post_tm, wg, g):
    S = y.shape[0]

    def body(dx2_ref, y_ref, ya_ref, yb_ref, ga_ref, gb_ref, mod_ref, post_ref, wa_ref, wb_ref, wo_ref, g_ref,
             dy_ref, dya_ref, dyb_ref, doa_ref, dcb_ref, dpg_ref, acc_ref, bsum_ref, hr_ref, send_sems, recv_sems):
        start, finish = _halves_exchange(g_ref, hr_ref, send_sems, recv_sems)

        @pl.when(pl.program_id(0) == 0)
        def _():
            start()
            acc_ref[...] = jnp.zeros_like(acc_ref)
            bsum_ref[...] = jnp.zeros_like(bsum_ref)

        y = y_ref[...]
        ry = lax.rsqrt(jnp.mean(y * y, axis=-1, keepdims=True) + EPS)
        yn = y * ry
        dx2 = dx2_ref[...]
        acc_ref[0:1, :] += _rowsum(dx2 * (yn * post_ref[...]))
        dn = dx2 * mod_ref[:, 2 * D:3 * D]
        acc_ref[1:2, :] += _rowsum(dn * yn)
        dyn = dn * post_ref[...]
        dy = (ry * (dyn - yn * jnp.mean(dyn * yn, axis=-1, keepdims=True))).astype(BF16)
        dy_ref[...] = dy
        dmg = _mm_rows_t(dy, wo_ref)
        sa, sb = _sig(ga_ref[...]), _sig(gb_ref[...])
        dya = (dmg * sa).astype(BF16)
        dyb = (dmg * sb).astype(BF16)
        dya_ref[...] = dya
        dyb_ref[...] = dyb
        dga = dmg * ya_ref[...].astype(F32) * (sa * (1.0 - sa))
        dgb = dmg * yb_ref[...].astype(F32) * (sb * (1.0 - sb))
        dpg_ref[:, 0:D] = dga.astype(BF16)
        dpg_ref[:, D:2 * D] = dgb.astype(BF16)
        bsum_ref[:, 0:D] += _rowsum(dga)
        bsum_ref[:, D:2 * D] += _rowsum(dgb)
        doa_ref[...] = _mm_rows_t(dya, wa_ref)
        dcb_ref[...] = _mm_rows_t(dyb, wb_ref)

        @pl.when(pl.program_id(0) == S // TM - 1)
        def _():
            finish()

    tile = BS((TM, D), lambda i: (i, 0))
    vec = BS((1, D), lambda i: (0, 0))
    return pl.pallas_call(
        body, name="merge_bwd", grid=(S // TM,),
        out_shape=(SDS((S, D), BF16), SDS((S, D), BF16), SDS((S, D), BF16), SDS((S, D), F32), SDS((S, D), F32),
                   SDS((S, 2 * D), BF16), SDS((8, D), F32), SDS((1, 2 * D), F32),
                   SDS((N_CHIPS,) + g.shape[2:], g.dtype)),
        in_specs=[tile, tile, tile, tile, BS((TM, D), lambda i: (i, 6)), BS((TM, D), lambda i: (i, 7)),
                  BS((1, 6 * D), lambda i: (0, 0)), vec, _br_spec(O_BRA), _br_spec(O_BRB), _br_spec(O_OUT),
                  BS(memory_space=pl.ANY)],
        out_specs=(tile, tile, tile, tile, tile, BS((TM, 2 * D), lambda i: (i, 0)),
                   BS((8, D), lambda i: (0, 0)), BS((1, 2 * D), lambda i: (0, 0)), BS(memory_space=pl.ANY)),
        scratch_shapes=_halves_sems(),
        compiler_params=_params("arbitrary"),
    )(dx2, y, ya, yb, p, p, mod, post_tm, wg, wg, wg, g)


def _hgrn_bwd_call(p, o, doa, st, logits, gn, part, g):
    S = p.shape[0]
    nb = S // TB
    ncb = TB // CHUNK

    def body(q_ref, f_ref, v_ref, og_ref, o_ref, doa_ref, st_ref, lg_ref, gn_ref, part_ref, g_ref,
             dp_ref, bsum_ref, dlg_ref, dgn_ref, recv_ref, hr_ref,
             dst_scr, dlb_scr, dqe_s, dqt_s, dkt_s, dkd_s, dv_s, dog_s, dble_s, send_sems, recv_sems, hs, hr):
        i = pl.program_id(0)
        start, finish = _chip_exchange(part_ref, recv_ref, send_sems, recv_sems)
        start_h, finish_h = _halves_exchange(g_ref, hr_ref, hs, hr)

        @pl.when(i == 0)
        def _():
            start_h()
            start()
            dst_scr[...] = jnp.zeros_like(dst_scr)
            dlb_scr[...] = jnp.zeros_like(dlb_scr)
            bsum_ref[...] = jnp.zeros_like(bsum_ref)
            dgn_ref[...] = jnp.zeros_like(dgn_ref)

        lb = _lower_bound(lg_ref)
        tril, triu = _tri_masks()

        def chunk(tt, carry):
            ci = ncb - 1 - tt
            rows = pl.ds(pl.multiple_of(ci * CHUNK, CHUNK), CHUNK)
            q_r, f_r = q_ref[rows, :], f_ref[rows, :]
            t = _hg_gates(q_r, f_r, lb, tril)
            v = v_ref[rows, :]
            for h in range(HEADS):
                sl = slice(h * DK, (h + 1) * DK)
                stp = st_ref[ci, :, sl]
                stb = stp.astype(BF16)
                qeb = t["qe"][:, sl].astype(BF16)
                qtb = t["qt"][:, sl].astype(BF16)
                ktb = t["kt"][:, sl].astype(BF16)
                kdb = t["kd"][:, sl].astype(BF16)
                vb = v[:, sl].astype(BF16)
                a = jnp.where(tril > 0.5, _mm(qtb, ktb, NT), 0.0)
                o_h = o_ref[rows, sl]
                rinv = lax.rsqrt(jnp.mean(o_h * o_h, axis=-1, keepdims=True) + EPS)
                oh = o_h * rinv
                og = og_ref[rows, sl]
                so = _sig(og)
                d_oa = doa_ref[rows, sl]
                don = d_oa * (og * so)
                dog_s[:, sl] = d_oa * (oh * gn_ref[:, sl]) * _dsilu(og, so)
                dgn_ref[:, sl] += _rowsum(don * oh)
                doh = don * gn_ref[:, sl]
                do = (rinv * (doh - oh * jnp.mean(doh * oh, axis=-1, keepdims=True))).astype(BF16)
                dqe_s[:, sl] = _mm(do, stb, NN)
                dstp = _mm(do, qeb, TN)
                dab = jnp.where(tril > 0.5, _mm(do, vb, NT), 0.0).astype(BF16)
                dqt_s[:, sl] = _mm(dab, ktb, NN)
                dkt_s[:, sl] = _mm(dab, qtb, TN)
                dstn = dst_scr[:, sl]
                dsb = dstn.astype(BF16)
                dkd_s[:, sl] = _mm(vb, dsb, NN)
                dv_s[:, sl] = _mm(a.astype(BF16), do, TN) + _mm(kdb, dsb, NT)
                el = t["elast"][:, sl]
                dst_scr[:, sl] = dstn * el + dstp
                dble_s[:, sl] = el * _rowsum(stp * dstn)
            dqe, dqt, dkt, dkd = dqe_s[...], dqt_s[...], dkt_s[...], dkd_s[...]
            dq = dqe * t["e"] + dqt * t["eq"]
            dk = dkt * t["ek"] + dkd * t["dd"]
            dkk = dkd * t["kd"]
            qt_r = t["qt"].astype(BF16).astype(F32)
            kt_r = t["kt"].astype(BF16).astype(F32)
            dbv = dqe * t["qe"] + dqt * qt_r - dkt * kt_r - dkk
            dg = _cumsum_mm(triu, dbv) + (_rowsum(dkk) + dble_s[...])
            df = dg / t["f"] - dk
            sf = t["sf"]
            dlb_scr[...] += _rowsum(df * (1.0 - sf))
            dqr = dq * _dsilu(q_r, t["sq"])
            dfr = df * (1.0 - lb) * (sf * (1.0 - sf))
            dvv, dog = dv_s[...], dog_s[...]
            dp_ref[rows, 0:D] = dqr.astype(BF16)
            dp_ref[rows, D:2 * D] = dfr.astype(BF16)
            dp_ref[rows, 2 * D:3 * D] = dvv.astype(BF16)
            dp_ref[rows, 3 * D:4 * D] = dog.astype(BF16)
            bsum_ref[:, 0:D] += _rowsum(dqr)
            bsum_ref[:, D:2 * D] += _rowsum(dfr)
            bsum_ref[:, 2 * D:3 * D] += _rowsum(dvv)
            bsum_ref[:, 3 * D:4 * D] += _rowsum(dog)
            return carry

        lax.fori_loop(0, ncb, chunk, 0)

        dl = dlb_scr[...] * lb * (1.0 - lb)
        dlg_ref[0:1, :] = dl
        dlg_ref[1:2, :] = -dl

        @pl.when(i == nb - 1)
        def _():
            finish_h()
            finish()

    col = lambda j: BS((TB, D), lambda i, j=j: (nb - 1 - i, j))
    rev = BS((TB, D), lambda i: (nb - 1 - i, 0))
    cd = pltpu.VMEM((CHUNK, D), F32)
    return pl.pallas_call(
        body, name="hgrn_bwd", grid=(nb,),
        out_shape=(SDS((S, 4 * D), BF16), SDS((1, 4 * D), F32), SDS((2, D), F32), SDS((1, D), F32),
                   SDS((3,) + part.shape[1:], part.dtype), SDS((N_CHIPS,) + g.shape[2:], g.dtype)),
        in_specs=[col(0), col(1), col(2), col(3), rev, rev, BS((ncb, DK, D), lambda i: (nb - 1 - i, 0, 0)),
                  BS((2, D), lambda i: (0, 0)), BS((1, D), lambda i: (0, 0)), BS(memory_space=pl.ANY),
                  BS(memory_space=pl.ANY)],
        out_specs=(BS((TB, 4 * D), lambda i: (nb - 1 - i, 0)), BS((1, 4 * D), lambda i: (0, 0)),
                   BS((2, D), lambda i: (0, 0)), BS((1, D), lambda i: (0, 0)), BS(memory_space=pl.ANY),
                   BS(memory_space=pl.ANY)),
        scratch_shapes=[pltpu.VMEM((DK, D), F32), pltpu.VMEM((1, D), F32), cd, cd, cd, cd, cd, cd,
                        pltpu.VMEM((1, D), F32)] + _exchange_sems() + _halves_sems(),
        compiler_params=_params("arbitrary"),
    )(p, p, p, p, o, doa, st, logits, gn, part, g)


def _conv_bwd_call(dcb, uc, u, p, dw, ln_g, ln_b, part):
    S = uc.shape[0]
    nb = S // TM
    hb = TM // HALO

    def body(dcb_ref, uc_ref, u_ref, uh_ref, cv_ref, cg_ref, dw_ref, g_ref, b_ref, part_ref,
             dp_ref, bsum_ref, ddw_ref, acc_ref, recv_ref, uext, dext, ush, dsh, send_sems, recv_sems):
        i = pl.program_id(0)
        start, finish = _chip_exchange(part_ref, recv_ref, send_sems, recv_sems)

        @pl.when(i == 0)
        def _():
            start()
            dext[TM:EXT, :] = jnp.zeros((EXT - TM, D), F32)
            uext[HALO + TM:EXT, :] = jnp.zeros((EXT - HALO - TM, D), F32)
            bsum_ref[...] = jnp.zeros_like(bsum_ref)
            ddw_ref[...] = jnp.zeros_like(ddw_ref)
            acc_ref[...] = jnp.zeros_like(acc_ref)

        first_tile = (nb - 1 - i) == 0
        uext[0:HALO, :] = jnp.where(first_tile, 0.0, uh_ref[...])
        uext[HALO:HALO + TM, :] = u_ref[...]
        _fill_shifted(uext, ush)

        for rb in range(TM // SUB):
            rs_ = slice(rb * SUB, (rb + 1) * SUB)
            xh, rs = _layernorm_stats(uc_ref[rs_, :])
            ln = xh * g_ref[...] + b_ref[...]
            dln = dcb_ref[rs_, :] * _dsilu(ln, _sig(ln))
            acc_ref[1:2, :] += _rowsum(dln * xh)
            acc_ref[2:3, :] += _rowsum(dln)
            dxh = dln * g_ref[...]
            duc = rs * (dxh - jnp.mean(dxh, axis=-1, keepdims=True)
                        - xh * jnp.mean(dxh * xh, axis=-1, keepdims=True))
            dext[rs_, :] = duc
            acc_ref[0:1, :] += _rowsum(duc)
        _fill_shifted(dext, dsh)

        for j in range(CONV_K):
            part = jnp.zeros((SUB, D), F32)
            for rb in range(TM // SUB):
                s0 = HALO - (CONV_K - 1) + j + rb * SUB
                part = part + dext[rb * SUB:(rb + 1) * SUB, :] * _window(uext, ush, s0, SUB)
            ddw_ref[j:j + 1, :] += _rowsum(part)

        for rb in range(TM // SUB):
            rs_ = slice(rb * SUB, (rb + 1) * SUB)
            du = jnp.zeros((SUB, D), F32)
            for j in range(CONV_K):
                s0 = rb * SUB + (CONV_K - 1) - j
                du = du + dw_ref[j:j + 1, :] * _window(dext, dsh, s0, SUB)
            cg = cg_ref[rs_, :]
            sg = _sig(cg)
            dcv = du * sg
            dcg = du * cv_ref[rs_, :] * (sg * (1.0 - sg))
            dp_ref[rs_, 0:D] = dcv.astype(BF16)
            dp_ref[rs_, D:2 * D] = dcg.astype(BF16)
            bsum_ref[:, 0:D] += _rowsum(dcv)
            bsum_ref[:, D:2 * D] += _rowsum(dcg)

        dext[TM:TM + HALO, :] = dext[0:HALO, :]

        @pl.when(i == nb - 1)
        def _():
            finish()

    rev = BS((TM, D), lambda i: (nb - 1 - i, 0))
    vec = BS((1, D), lambda i: (0, 0))
    return pl.pallas_call(
        body, name="conv_bwd", grid=(nb,),
        out_shape=(SDS((S, 2 * D), BF16), SDS((1, 2 * D), F32), SDS((32, D), F32), SDS((8, D), F32),
                   SDS((3,) + part.shape[1:], part.dtype)),
        in_specs=[rev, rev, rev, BS((HALO, D), lambda i: (jnp.maximum((nb - 1 - i) * hb - 1, 0), 0)),
                  BS((TM, D), lambda i: (nb - 1 - i, 4)), BS((TM, D), lambda i: (nb - 1 - i, 5)),
                  BS((CONV_K, D), lambda i: (0, 0)), vec, vec, BS(memory_space=pl.ANY)],
        out_specs=(BS((TM, 2 * D), lambda i: (nb - 1 - i, 0)), BS((1, 2 * D), lambda i: (0, 0)),
                   BS((32, D), lambda i: (0, 0)), BS((8, D), lambda i: (0, 0)), BS(memory_space=pl.ANY)),
        scratch_shapes=[pltpu.VMEM((EXT, D), F32), pltpu.VMEM((EXT, D), F32),
                        pltpu.VMEM((7, HALO + TM, D), F32), pltpu.VMEM((7, HALO + TM, D), F32)] + _exchange_sems(),
        compiler_params=_params("arbitrary"),
    )(dcb, uc, u, u, p, p, dw, ln_g, ln_b, part)


def _in_bwd_call(dp_hg, dp_cv, dp_gt, x, dx2, mod, pre_tm, wg, part, full_a, full_b):
    S = x.shape[0]

    def body(hg_ref, cv_ref, gt_ref, x_ref, dx2_ref, mod_ref, g_ref, w_hbm, part_ref, fa_in, fb_in,
             gx_ref, acc_ref, recv_ref, fa_out, fb_out, w_vmem, sem, send_sems, recv_sems, sa, ra, sb, rb):
        start, finish = _chip_exchange(part_ref, recv_ref, send_sems, recv_sems)
        start_a, finish_a = _join_exchange(fa_in, fa_out, sa, ra)
        start_b, finish_b = _join_exchange(fb_in, fb_out, sb, rb)

        @pl.when(pl.program_id(0) == 0)
        def _():
            start_a()
            start_b()
            start()
            _load_rows(w_hbm, w_vmem, sem, O_IN).wait()
            acc_ref[...] = jnp.zeros_like(acc_ref)

        dh = jnp.zeros((TM, D), F32)
        for k in range(IN_COLS // D):
            src, kk = ((hg_ref, k), (cv_ref, k - 4), (gt_ref, k - 6))[0 if k < 4 else (1 if k < 6 else 2)]
            dh = dh + _mm(src[:, kk * D:(kk + 1) * D], w_vmem[k // 2, (k % 2) * D:(k % 2 + 1) * D, :], NT)
        xv = x_ref[...]
        r = lax.rsqrt(jnp.mean(xv * xv, axis=-1, keepdims=True) + EPS)
        xn = xv * r
        yv = xn * g_ref[...]
        acc_ref[0:1, :] += _rowsum(dh)
        acc_ref[1:2, :] += _rowsum(dh * yv)
        dyv = dh * (1.0 + mod_ref[:, D:2 * D])
        acc_ref[2:3, :] += _rowsum(dyv * xn)
        dxn = dyv * g_ref[...]
        gx_ref[...] = dx2_ref[...] + r * (dxn - xn * jnp.mean(dxn * xn, axis=-1, keepdims=True))

        @pl.when(pl.program_id(0) == S // TM - 1)
        def _():
            finish_a()
            finish_b()
            finish()

    tile = BS((TM, D), lambda i: (i, 0))
    hbm = BS(memory_space=pl.ANY)
    return pl.pallas_call(
        body, name="in_bwd", grid=(S // TM,),
        out_shape=(SDS((S, D), F32), SDS((8, D), F32), SDS((3,) + part.shape[1:], part.dtype),
                   SDS(full_a.shape, full_a.dtype), SDS(full_b.shape, full_b.dtype)),
        in_specs=[BS((TM, 4 * D), lambda i: (i, 0)), BS((TM, 2 * D), lambda i: (i, 0)),
                  BS((TM, 2 * D), lambda i: (i, 0)), tile, tile, BS((1, 6 * D), lambda i: (0, 0)),
                  BS((1, D), lambda i: (0, 0)), hbm, hbm, hbm, hbm],
        out_specs=(tile, BS((8, D), lambda i: (0, 0)), hbm, hbm, hbm),
        scratch_shapes=[pltpu.VMEM((N_CHIPS, R_IN, D), BF16), pltpu.SemaphoreType.DMA] + _exchange_sems()
        + _join_sems() + _join_sems(),
        input_output_aliases={9: 3, 10: 4},
        compiler_params=_params("arbitrary"),
    )(dp_hg, dp_cv, dp_gt, x, dx2, mod, pre_tm, wg, part, full_a, full_b)


def _wgrad_call(gp, a, b, name, bm, place, rows):
    S, M = a.shape
    N = b.shape[1]
    bk = min(S, 1024)
    nk = S // bk

    def body(a_ref, b_ref, *rest):
        o_ref, acc = rest[-2], rest[-1]
        k = pl.program_id(2)

        @pl.when(k == 0)
        def _():
            acc[...] = jnp.zeros_like(acc)

        acc[...] += _mm(a_ref[...], b_ref[...], TN)

        @pl.when(k == nk - 1)
        def _():
            o_ref[...] = acc[...].astype(BF16)

    in_specs = [BS((bk, bm), lambda i, j, k: (k, i)), BS((bk, D), lambda i, j, k: (k, j))]
    args = [a, b]
    if gp is not None:
        in_specs.append(BS(memory_space=pl.ANY))
        args.append(gp)
    return pl.pallas_call(
        body, name=name, grid=(M // bm, N // D, nk),
        out_shape=SDS((N_CHIPS, rows, D), BF16),
        in_specs=in_specs,
        out_specs=BS((None, bm, D), lambda i, j, k: (*place(i, j), 0)),
        scratch_shapes=[pltpu.VMEM((bm, D), F32)],
        input_output_aliases={} if gp is None else {2: 0},
        compiler_params=_params("parallel", "parallel", "arbitrary"),
    )(*args)


def _wgrad_rows_call(gp, a, b, name, blk):
    S = a.shape[0]
    bk = min(S, 1024)
    nk = S // bk

    def body(a_ref, b_ref, *rest):
        o_ref, acc = rest[-2], rest[-1]
        k = pl.program_id(0)

        @pl.when(k == 0)
        def _():
            acc[...] = jnp.zeros_like(acc)

        acc[...] += _mm(a_ref[...], b_ref[...], TN)

        @pl.when(k == nk - 1)
        def _():
            for c in range(N_CHIPS):
                o_ref[c] = acc[c * R_BR:(c + 1) * R_BR, :].astype(BF16)

    in_specs = [BS((bk, D), lambda k: (k, 0)), BS((bk, D), lambda k: (k, 0))]
    args = [a, b]
    if gp is not None:
        in_specs.append(BS(memory_space=pl.ANY))
        args.append(gp)
    return pl.pallas_call(
        body, name=name, grid=(nk,),
        out_shape=SDS((N_CHIPS, 3 * R_BR, D), BF16),
        in_specs=in_specs,
        out_specs=BS((N_CHIPS, R_BR, D), lambda k: (0, blk, 0)),
        scratch_shapes=[pltpu.VMEM((D, D), F32)],
        input_output_aliases={} if gp is None else {2: 0},
        compiler_params=_params("arbitrary"),
    )(*args)


def _outer_call(cact, dmod):
    n = dmod.shape[1]

    def body(a_ref, b_ref, o_ref):
        o_ref[...] = _mm(a_ref[...], b_ref[...], TN, HI)

    return pl.pallas_call(
        body, name="wgrad_ada", out_shape=SDS((D, n), F32),
        compiler_params=pltpu.CompilerParams(vmem_limit_bytes=VMEM_LIMIT),
    )(cact, dmod)


def _adamw_call(w, g, m, v, name):
    R, C = w.shape
    tr = R
    while tr * C > 512 * 1024 and tr % 16 == 0:
        tr //= 2
    c1 = 1.0 - ADAM_B1 ** ADAM_STEP
    c2 = 1.0 - ADAM_B2 ** ADAM_STEP

    def body(w_ref, g_ref, m_ref, v_ref, d_ref, m2_ref, v2_ref):
        g = g_ref[...]
        m2 = ADAM_B1 * m_ref[...] + (1.0 - ADAM_B1) * g
        v2 = ADAM_B2 * v_ref[...] + (1.0 - ADAM_B2) * (g * g)
        m2_ref[...] = m2
        v2_ref[...] = v2
        d_ref[...] = -ADAM_LR * ((m2 / c1) / (jnp.sqrt(v2 / c2) + ADAM_EPS) + ADAM_WD * w_ref[...])

    tile = BS((tr, C), lambda i: (i, 0))
    return pl.pallas_call(
        body, name=name, grid=(R // tr,), out_shape=(SDS((R, C), F32),) * 3,
        in_specs=[tile] * 4, out_specs=(tile,) * 3, compiler_params=_params("parallel"),
    )(w, g, m, v)


def _adamw_gather_call(w, g, m, v, srows, name):
    R, C = w.shape
    tr = R
    while tr * C > 512 * 1024 and tr % 16 == 0:
        tr //= 2
    nsteps = R // tr
    mr = srows.shape[0]
    c1 = 1.0 - ADAM_B1 ** ADAM_STEP
    c2 = 1.0 - ADAM_B2 ** ADAM_STEP

    def body(w_ref, g_ref, m_ref, v_ref, s_ref, d_ref, m2_ref, v2_ref, all_ref, sum_ref,
             x_scr, out_scr, send_sems, recv_sems, local_sem):
        i = pl.program_id(0)
        start, finish = _allgather_parts(x_scr, out_scr, send_sems, recv_sems, local_sem)

        @pl.when(i == 0)
        def _():
            x_scr[...] = s_ref[...]
            start()

        g = g_ref[...]
        m2 = ADAM_B1 * m_ref[...] + (1.0 - ADAM_B1) * g
        v2 = ADAM_B2 * v_ref[...] + (1.0 - ADAM_B2) * (g * g)
        m2_ref[...] = m2
        v2_ref[...] = v2
        d_ref[...] = -ADAM_LR * ((m2 / c1) / (jnp.sqrt(v2 / c2) + ADAM_EPS) + ADAM_WD * w_ref[...])

        @pl.when(i == nsteps - 1)
        def _():
            finish()
            all_ref[...] = out_scr[...]
            acc = out_scr[0:mr, :]
            for d in range(1, N_DEV):
                acc = acc + out_scr[d * mr:(d + 1) * mr, :]
            sum_ref[...] = acc

    tile = BS((tr, C), lambda i: (i, 0))
    return pl.pallas_call(
        body, name=name, grid=(nsteps,),
        out_shape=(SDS((R, C), F32),) * 3 + (SDS((N_DEV * mr, D), F32), SDS((mr, D), F32)),
        in_specs=[tile] * 4 + [BS((mr, D), lambda i: (0, 0))],
        out_specs=(tile,) * 3 + (BS((N_DEV * mr, D), lambda i: (0, 0)), BS((mr, D), lambda i: (0, 0))),
        scratch_shapes=[pltpu.VMEM((mr, D), F32), pltpu.VMEM((N_DEV * mr, D), F32)] + _allgather_sems(),
        compiler_params=_params("arbitrary"),
    )(w, g, m, v, srows)


def _rs_begin(g, c_idx, tag):
    n = g.shape[1]
    g = g.reshape(N_CHIPS, 2, n // 2, D)
    return _add_halves_call(g, _sibling_halves_call(g, tag), c_idx, tag)


def _rs_end(part, recv, c_idx, chip_idx, tag):
    n = 2 * part.shape[1]
    full = _add_chips_call(part, recv, jnp.concatenate([chip_idx, c_idx]), tag)
    return _sibling_join_call(full, tag).reshape(n, D)


def _local_step(x, mod, cact, target, wg, pack, small, c_idx, chip_idx):
    p, h1, wg = _fwd_in_call(x, mod, small["pre_tm"], wg, small["b_in"], pack, small["order"])
    o, oa, st, wg = _hgrn_fwd_call(p, small["logits"], small["hg_norm"], wg, pack)
    u, uc, cb, wg = _conv_fwd_call(p, small["conv_dw"], small["conv_db"], small["ln_g"], small["ln_b"], wg, pack)
    ya, yb, mg, y, x2, h2 = _merge_fwd_call(oa, cb, p, x, mod, small["post_tm"], small["pre_cm"], wg)
    z, da, dy2, dx2, acc_f = _ffn_call(h2, x2, target, mod, small["post_cm"], small["pre_cm"], wg)

    g_ff = _wgrad_call(None, h2, da, "wgrad_ff1", D, lambda i, j: (j, 0), 2 * R_FF)
    g_ff = _wgrad_call(g_ff, z, dy2, "wgrad_ff2", D, lambda i, j: (i, 1), 2 * R_FF)
    g_ff = g_ff.reshape(N_CHIPS, 2, R_FF, D)
    dy, dya, dyb, doa, dcb, dp_gt, acc_m, bs_gt, hr_ff = _merge_bwd_call(dx2, y, ya, yb, p, mod, small["post_tm"],
                                                                        wg, g_ff)
    part_ff = _add_halves_call(g_ff, hr_ff, c_idx, "ff")

    g_br = _wgrad_rows_call(None, oa, dya, "wgrad_br_a", 0)
    g_br = _wgrad_rows_call(g_br, cb, dyb, "wgrad_br_b", 1)
    g_br = _wgrad_rows_call(g_br, mg, dy, "wgrad_out", 2)
    g_br = g_br.reshape(N_CHIPS, 2, 3 * R_BR // 2, D)
    dp_hg, bs_hg, dlg, dgn, recv_ff, hr_br = _hgrn_bwd_call(p, o, doa, st, small["logits"], small["hg_norm"],
                                                            part_ff, g_br)
    part_br = _add_halves_call(g_br, hr_br, c_idx, "br")
    dp_cv, bs_cv, ddw, acc_c, recv_br = _conv_bwd_call(dcb, uc, u, p, small["conv_dw"], small["ln_g"], small["ln_b"],
                                                        part_br)

    g_in = _wgrad_call(None, h1, dp_hg, "wgrad_in_hg", D, lambda i, j: (j // 2, j % 2), R_IN)
    g_in = _wgrad_call(g_in, h1, dp_cv, "wgrad_in_cv", D, lambda i, j: (2, j), R_IN)
    g_in = _wgrad_call(g_in, h1, dp_gt, "wgrad_in_gt", D, lambda i, j: (3, j), R_IN)
    part_in = _rs_begin(g_in, c_idx, "in")
    chip_c = jnp.concatenate([chip_idx, c_idx])
    full_ff = _add_chips_call(part_ff, recv_ff, chip_c, "ff")
    full_br = _add_chips_call(part_br, recv_br, chip_c, "br")
    gx, acc_i, recv_in, full_ff, full_br = _in_bwd_call(dp_hg, dp_cv, dp_gt, x, dx2, mod, small["pre_tm"], wg,
                                                        part_in, full_ff, full_br)
    red_ff = full_ff.reshape(2 * R_FF, D)
    red_br = full_br.reshape(3 * R_BR, D)
    red_in = _rs_end(part_in, recv_in, c_idx, chip_idx, "in")

    zrow = jnp.zeros((1, D), F32)
    rows = [acc_i[0:1], acc_i[1:2], acc_m[0:1], acc_f[2:3], acc_f[3:4], acc_f[0:1],
            acc_i[2:3], acc_m[1:2], acc_f[4:5], acc_f[1:2],
            jnp.concatenate([bs_hg, bs_cv, bs_gt], axis=1).reshape(8, D),
            dlg, dgn, acc_c[0:1], acc_c[1:2], acc_c[2:3],
            ddw,
            cact, acc_f[5:6]] + [zrow] * 6
    return gx, jnp.concatenate(rows, axis=0), red_in, red_ff, red_br


def kernel(x, c, w_ada, b_ada, pre_norm_tm, post_norm_tm, pre_norm_cm, post_norm_cm, w_in, b_in, hg_lb_logits, hg_norm, conv_dw, conv_db, conv_ln_g, conv_ln_b, w_br_a, w_br_b, w_out, w_ff1, w_ff2, loss_target, m_w_ada, m_b_ada, m_pre_norm_tm, m_post_norm_tm, m_pre_norm_cm, m_post_norm_cm, m_w_in, m_b_in, m_hg_lb_logits, m_hg_norm, m_conv_dw, m_conv_db, m_conv_ln_g, m_conv_ln_b, m_w_br_a, m_w_br_b, m_w_out, m_w_ff1, m_w_ff2, v_w_ada, v_b_ada, v_pre_norm_tm, v_post_norm_tm, v_pre_norm_cm, v_post_norm_cm, v_w_in, v_b_in, v_hg_lb_logits, v_hg_norm, v_conv_dw, v_conv_db, v_conv_ln_g, v_conv_ln_b, v_w_br_a, v_w_br_b, v_w_out, v_w_ff1, v_w_ff2):
    xi, yi, ci = lax.axis_index("x"), lax.axis_index("y"), lax.axis_index("c")
    chip = 2 * xi + yi
    c_idx = jnp.reshape(ci, (1,)).astype(jnp.int32)
    chip_idx = jnp.reshape(chip, (1,)).astype(jnp.int32)

    def pack_small(ada_b, pre_t, post_t, pre_c, post_c, in_b, lg, hgn, cdb, lng, lnb, cdw):
        flat = jnp.concatenate([cdw[0].reshape(-1), jnp.zeros((8 * D - CONV_K * 256,), F32)]).reshape(8, D)
        return jnp.concatenate([ada_b.reshape(6, D), pre_t, post_t, pre_c, post_c, in_b.reshape(8, D), lg, hgn,
                                cdb, lng, lnb, flat], axis=0)

    w_in_halves = w_in[0].reshape(D, 2, D).transpose(1, 0, 2).reshape(R_IN, D)
    pack = jnp.concatenate([w_in_halves, w_ff1[0], w_ff2[0], w_br_a[0], w_br_b[0], w_out[0]],
                           axis=0).astype(BF16)
    wg = lax.dynamic_update_slice(lax.empty((N_CHIPS, PACK_W, D), BF16), pack[None], (chip, 0, 0))
    wa = 6 * D // N_CHIPS
    me = 4 * xi + 2 * yi + ci
    dw_blk = jnp.concatenate([conv_dw[0].reshape(-1), jnp.zeros((8 * D - CONV_K * 256,), F32)]).reshape(8, D)
    dw_all, ca_all, mod_all = _prologue_call(
        dw_blk, jnp.broadcast_to(c, (8, D)), w_ada[0].astype(BF16),
        lax.dynamic_slice_in_dim(b_ada, chip * wa, wa, axis=1))
    order = jnp.stack([chip, 2 * (1 - xi) + yi, 2 * xi + (1 - yi), 2 * (1 - xi) + (1 - yi)]).astype(jnp.int32)
    dw_all = dw_all.reshape(N_CHIPS, 2, 8 * D)[:, 0, :CONV_K * 256].reshape(N_CHIPS, CONV_K, 256)
    dw_full = dw_all.transpose(1, 0, 2).reshape(CONV_K, D)
    cact = lax.dynamic_slice_in_dim(ca_all, me * 8, 1, axis=0)
    mod_mine = lax.dynamic_index_in_dim(mod_all.reshape(N_CHIPS, 2, N_DEV, 8, wa)[:, 0, :, 0, :], me, axis=1,
                                        keepdims=False)
    mod = mod_mine.reshape(1, 6 * D)

    small = dict(b_ada=b_ada, pre_tm=pre_norm_tm, post_tm=post_norm_tm, pre_cm=pre_norm_cm, post_cm=post_norm_cm,
                 b_in=b_in, logits=hg_lb_logits, hg_norm=hg_norm, conv_dw=dw_full, conv_db=conv_db,
                 ln_g=conv_ln_g, ln_b=conv_ln_b, order=order)

    gx, srows, red_in, red_ff, red_br = _local_step(x[0], mod, cact, loss_target[0], wg, pack, small, c_idx,
                                                    chip_idx)

    shapes = {"in": w_in.shape, "br_a": w_br_a.shape, "br_b": w_br_b.shape, "out": w_out.shape,
              "ff1": w_ff1.shape, "ff2": w_ff2.shape}
    offs = {"in": (red_in, 0, R_IN), "ff1": (red_ff, 0, R_FF), "ff2": (red_ff, R_FF, 2 * R_FF),
            "br_a": (red_br, 0, R_BR), "br_b": (red_br, R_BR, 2 * R_BR), "out": (red_br, 2 * R_BR, 3 * R_BR)}
    wmv = {"in": (w_in, m_w_in, v_w_in), "br_a": (w_br_a, m_w_br_a, v_w_br_a), "br_b": (w_br_b, m_w_br_b, v_w_br_b),
           "out": (w_out, m_w_out, v_w_out), "ff1": (w_ff1, m_w_ff1, v_w_ff1), "ff2": (w_ff2, m_w_ff2, v_w_ff2)}
    res = {}
    for n in offs:
        shp = shapes[n]
        g2d = offs[n][0][offs[n][1]:offs[n][2]]
        if n == "in":
            g2d = g2d.reshape(2, D, D).transpose(1, 0, 2)
        g2d = g2d.reshape(shp[1], shp[2])
        w_, m_, v_ = (a[0] for a in wmv[n])
        if n == "in":
            d_, m2_, v2_, sall, ssum = _adamw_gather_call(w_, g2d, m_, v_, srows, "adamw_in")
        else:
            d_, m2_, v2_ = _adamw_call(w_, g2d, m_, v_, "adamw_" + n)
        res[n] = tuple(a.reshape(shp) for a in (g2d, d_, m2_, v2_))

    sall = sall.reshape(N_DEV, SMALL_ROWS, D)
    loss = jnp.sum(ssum[57])
    dmod_all = sall[:, 0:6, :].reshape(N_DEV, 6 * D)
    g_ada = _outer_call(sall[:, 56, :], lax.dynamic_slice_in_dim(dmod_all, chip * wa, wa, axis=1))
    g_dw = lax.dynamic_slice_in_dim(ssum[24:24 + CONV_K], chip * 256, 256, axis=1)
    g_small = jnp.concatenate(
        [ssum[0:24], jnp.concatenate([g_dw.reshape(-1), jnp.zeros((8 * D - CONV_K * 256,), F32)]).reshape(8, D)],
        axis=0)
    d_, m2_, v2_ = _adamw_call(w_ada[0], g_ada, m_w_ada[0], v_w_ada[0], "adamw_ada")
    res["ada"] = tuple(a.reshape(w_ada.shape) for a in (g_ada, d_, m2_, v2_))

    ws = pack_small(b_ada, pre_norm_tm, post_norm_tm, pre_norm_cm, post_norm_cm, b_in, hg_lb_logits, hg_norm,
                    conv_db, conv_ln_g, conv_ln_b, conv_dw)
    ms = pack_small(m_b_ada, m_pre_norm_tm, m_post_norm_tm, m_pre_norm_cm, m_post_norm_cm, m_b_in, m_hg_lb_logits,
                    m_hg_norm, m_conv_db, m_conv_ln_g, m_conv_ln_b, m_conv_dw)
    vs = pack_small(v_b_ada, v_pre_norm_tm, v_post_norm_tm, v_pre_norm_cm, v_post_norm_cm, v_b_in, v_hg_lb_logits,
                    v_hg_norm, v_conv_db, v_conv_ln_g, v_conv_ln_b, v_conv_dw)
    sres = (g_small,) + tuple(_adamw_call(ws, g_small, ms, vs, "adamw_small"))

    def unpack_small(t):
        return {"b_ada": t[0:6].reshape(1, 6 * D), "pre_tm": t[6:7], "post_tm": t[7:8], "pre_cm": t[8:9],
                "post_cm": t[9:10], "b_in": t[10:18].reshape(1, IN_COLS), "logits": t[18:20], "hg_norm": t[20:21],
                "conv_db": t[21:22], "ln_g": t[22:23], "ln_b": t[23:24],
                "conv_dw": t[24:32].reshape(-1)[:CONV_K * 256].reshape(1, CONV_K, 256)}

    order = ["ada", "b_ada", "pre_tm", "post_tm", "pre_cm", "post_cm", "in", "b_in", "logits", "hg_norm", "conv_dw",
             "conv_db", "ln_g", "ln_b", "br_a", "br_b", "out", "ff1", "ff2"]
    outs = [loss, gx.reshape(x.shape)]
    for kind in range(4):
        sm = unpack_small(sres[kind])
        for n in order:
            outs.append(res[n][kind] if n in res else sm[n])
    return tuple(outs)
```

```python
import functools

import jax
import jax.numpy as jnp
from jax import lax
from jax.experimental import pallas as pl
from jax.experimental.pallas import tpu as pltpu

F32, BF16 = jnp.float32, jnp.bfloat16
SDS = jax.ShapeDtypeStruct
BS = pl.BlockSpec
MESH = pl.DeviceIdType.MESH
HI = lax.Precision.HIGHEST

D = 1024
D_FF = 4096
IN_COLS = 8192
HEADS, DK = 8, 128
CHUNK = 128
CONV_K = 31
HALO = 32
SUB = 32
EPS = 1e-6
N_CHIPS, N_DEV = 4, 8
TM = 256
TB = 256
VMEM_LIMIT = 56 * 1024 * 1024

R_IN, R_BR, R_FF = 2048, 256, 1024
PACK_W = R_IN + 3 * R_BR + 2 * R_FF
O_IN, O_FF1, O_FF2, O_BRA, O_BRB, O_OUT = 0, 2048, 3072, 4096, 4352, 4608
SMALL_ROWS = 64

ADAM_LR, ADAM_B1, ADAM_B2, ADAM_EPS, ADAM_WD, ADAM_STEP = 0.001, 0.9, 0.999, 1e-08, 0.01, 10

NN = (((1,), (0,)), ((), ()))
NT = (((1,), (1,)), ((), ()))
TN = (((0,), (0,)), ((), ()))


def _mm(a, b, dims=NN, precision=None):
    return lax.dot_general(a, b, dims, preferred_element_type=F32, precision=precision)


def _sig(v):
    return jax.nn.sigmoid(v)


def _dsilu(v, s):
    return s * (1.0 + v * (1.0 - s))


def _params(*sem):
    return pltpu.CompilerParams(dimension_semantics=sem if sem else None, vmem_limit_bytes=VMEM_LIMIT)


def _rowsum(v):
    return jnp.sum(v, axis=0, keepdims=True)


def _mesh_pos():
    return lax.axis_index("x"), lax.axis_index("y"), lax.axis_index("c")


def _allgather_parts(x_ref, out_ref, send_sems, recv_sems, local_sem):
    m_per = x_ref.shape[0]
    x, y, c = _mesh_pos()
    me, sibling = (x, y, c), (x, y, 1 - c)
    chips = [(1 - x, y), (x, 1 - y), (1 - x, 1 - y)]

    def rows(px, py, pc):
        return out_ref.at[pl.ds((4 * px + 2 * py + pc) * m_per, m_per), :]

    def copy(k, block, to, src=None):
        return pltpu.make_async_remote_copy(
            src_ref=rows(*block) if src is None else src, dst_ref=rows(*block),
            send_sem=send_sems.at[k], recv_sem=recv_sems.at[k], device_id=to, device_id_type=MESH)

    def first():
        return [copy(0, me, sibling, src=x_ref)] + [copy(1 + j, me, (*chip, c), src=x_ref)
                                                    for j, chip in enumerate(chips)]

    def start():
        pltpu.make_async_copy(x_ref, rows(*me), local_sem).start()
        for cp in first():
            cp.start()

    def finish():
        passed = [copy(4 + j, (*chip, c), sibling) for j, chip in enumerate(chips)]
        for j, chip in enumerate(chips):
            copy(1 + j, (*chip, c), me).wait_recv()
            passed[j].start()
        copy(0, sibling, me).wait_recv()
        for j, chip in enumerate(chips):
            copy(4 + j, (*chip, 1 - c), me).wait_recv()
        for cp in first() + passed:
            cp.wait_send()
        pltpu.make_async_copy(x_ref, rows(*me), local_sem).wait()

    return start, finish


def _allgather(x_ref, out_ref, send_sems, recv_sems, local_sem):
    start, finish = _allgather_parts(x_ref, out_ref, send_sems, recv_sems, local_sem)
    start()
    finish()


def _allgather_sems():
    return [pltpu.SemaphoreType.DMA((7,)), pltpu.SemaphoreType.DMA((7,)), pltpu.SemaphoreType.DMA]


def _allgather_call(blk, name, in_vmem, with_sum):
    m_per, n = blk.shape

    def body(x_ref, out_ref, *rest):
        if with_sum:
            sum_ref, send_sems, recv_sems, local_sem = rest
        else:
            send_sems, recv_sems, local_sem = rest
        _allgather(x_ref, out_ref, send_sems, recv_sems, local_sem)
        if with_sum:
            acc = out_ref[0:m_per, :]
            for d in range(1, N_DEV):
                acc = acc + out_ref[d * m_per:(d + 1) * m_per, :]
            sum_ref[...] = acc

    space = pltpu.VMEM if in_vmem else pl.ANY
    out_shape = [SDS((N_DEV * m_per, n), blk.dtype)]
    out_specs = [BS(memory_space=space)]
    if with_sum:
        out_shape.append(SDS((m_per, n), blk.dtype))
        out_specs.append(BS(memory_space=pltpu.VMEM))
    return pl.pallas_call(
        body, name=name, out_shape=out_shape, in_specs=[BS(memory_space=space)], out_specs=out_specs,
        scratch_shapes=[pltpu.SemaphoreType.DMA((7,)), pltpu.SemaphoreType.DMA((7,)), pltpu.SemaphoreType.DMA],
        compiler_params=pltpu.CompilerParams(vmem_limit_bytes=VMEM_LIMIT),
    )(blk)


def _gather_sems(n_ranges):
    return [pltpu.SemaphoreType.DMA((6 * n_ranges,)), pltpu.SemaphoreType.DMA((6 * n_ranges,))]


def _pack_gather(pack_ref, wg_ref, send_sems, recv_sems, ranges):
    x, y, c = _mesh_pos()
    me, sibling = (x, y, c), (x, y, 1 - c)
    chips = [(1 - x, y), (x, 1 - y), (1 - x, 1 - y)]

    def land(r, px, py, pc):
        off, n = ranges[r]
        return wg_ref.at[2 * px + py, pl.ds(off + pc * (n // 2), n // 2), :]

    def mine(r):
        off, n = ranges[r]
        return pack_ref.at[pl.ds(off + c * (n // 2), n // 2), :]

    def copy(r, k, block, to, src=None):
        return pltpu.make_async_remote_copy(
            src_ref=land(r, *block) if src is None else src, dst_ref=land(r, *block),
            send_sem=send_sems.at[6 * r + k], recv_sem=recv_sems.at[6 * r + k], device_id=to, device_id_type=MESH)

    def start():
        for r in range(len(ranges)):
            for j, chip in enumerate(chips):
                copy(r, j, me, (*chip, c), src=mine(r)).start()

    def arrive(j):
        for r in range(len(ranges)):
            copy(r, j, (*chips[j], c), me).wait_recv()
            copy(r, 3 + j, (*chips[j], c), sibling).start()
        for r in range(len(ranges)):
            copy(r, 3 + j, (*chips[j], 1 - c), me).wait_recv()

    def drain():
        for r in range(len(ranges)):
            for j, chip in enumerate(chips):
                copy(r, j, me, (*chip, c), src=mine(r)).wait_send()
                copy(r, 3 + j, (*chip, c), sibling).wait_send()

    def finish():
        for r in range(len(ranges)):
            for j, chip in enumerate(chips):
                copy(r, j, (*chip, c), me).wait_recv()
                copy(r, 3 + j, (*chip, c), sibling).start()
        for r in range(len(ranges)):
            for j, chip in enumerate(chips):
                copy(r, 3 + j, (*chip, 1 - c), me).wait_recv()
        drain()

    return start, finish, arrive, drain


def _relay_sems():
    return [pltpu.SemaphoreType.DMA((8,)), pltpu.SemaphoreType.DMA((8,))]


def _relay_gather(pack_ref, wg_ref, send_sems, recv_sems, off, n):
    x, y, c = _mesh_pos()
    me, sibling = (x, y, c), (x, y, 1 - c)
    chips = [(1 - x, y), (x, 1 - y), (1 - x, 1 - y)]
    h, q = n // 2, n // 4

    def land(px, py, pc, piece=None):
        if piece is None:
            return wg_ref.at[2 * px + py, pl.ds(off + pc * h, h), :]
        return wg_ref.at[2 * px + py, pl.ds(off + pc * h + piece * q, q), :]

    def copy(k, ref, to, src=None):
        return pltpu.make_async_remote_copy(
            src_ref=ref if src is None else src, dst_ref=ref, send_sem=send_sems.at[k], recv_sem=recv_sems.at[k],
            device_id=to, device_id_type=MESH)

    def direct(j):
        return copy(j, land(x, y, c), (*chips[j], c), src=pack_ref.at[pl.ds(off + c * h, h), :])

    def relayed(j):
        if j == 0:
            return copy(6, land(*chips[0], c, 1), (x, 1 - y, c))
        return copy(7, land(*chips[1], c, 0), (1 - x, y, c))

    def start():
        direct(0).start()
        direct(1).start()

    def arrive(j):
        if j == 0:
            for k in range(2):
                copy(k, land(*chips[k], c), me).wait_recv()
                relayed(k).start()
                copy(3 + k, land(*chips[k], c), sibling).start()
        if j == 2:
            copy(7, land(*chips[2], c, 0), me).wait_recv()
            copy(6, land(*chips[2], c, 1), me).wait_recv()
            copy(5, land(*chips[2], c), sibling).start()
        copy(3 + j, land(*chips[j], 1 - c), me).wait_recv()

    def drain():
        for j in range(2):
            direct(j).wait_send()
            relayed(j).wait_send()
        for j in range(3):
            copy(3 + j, land(*chips[j], c), sibling).wait_send()

    return start, arrive, drain


def _prologue_call(dw_blk, c_blk, w_ada, b_ada):
    wa = w_ada.shape[1]

    def body(dw_ref, c_ref, wa_ref, ba_ref, dwg_ref, ca_ref, modg_ref,
             cg_scr, part_scr, s1, r1, l1, s2, r2, l2, s3, r3, l3):
        start_c, finish_c = _allgather_parts(c_ref, cg_scr, s2, r2, l2)
        start_dw, finish_dw = _allgather_parts(dw_ref, dwg_ref, s1, r1, l1)
        start_mod, finish_mod = _allgather_parts(part_scr, modg_ref, s3, r3, l3)
        start_c()
        start_dw()
        finish_c()
        cv = cg_scr[...]
        ca = cv * _sig(cv)
        ca_ref[...] = ca
        part_scr[...] = _mm(ca.astype(BF16), wa_ref[...]) + ba_ref[...]
        start_mod()
        finish_dw()
        finish_mod()

    vm = BS(memory_space=pltpu.VMEM)
    return pl.pallas_call(
        body, name="prologue_adaln_conv_dw",
        out_shape=(SDS((N_DEV * 8, D), F32), SDS((N_DEV * 8, D), F32), SDS((N_DEV * N_DEV * 8, wa), F32)),
        in_specs=[vm, vm, vm, vm], out_specs=(vm, vm, vm),
        scratch_shapes=[pltpu.VMEM((N_DEV * 8, D), F32), pltpu.VMEM((N_DEV * 8, wa), F32)]
        + _allgather_sems() + _allgather_sems() + _allgather_sems(),
        compiler_params=pltpu.CompilerParams(vmem_limit_bytes=VMEM_LIMIT),
    )(dw_blk, c_blk, w_ada, b_ada)


def _halves_exchange(g_ref, out_ref, send_sems, recv_sems):
    x, y, c = _mesh_pos()

    def copies():
        return [pltpu.make_async_remote_copy(
            src_ref=g_ref.at[k, 1 - c], dst_ref=out_ref.at[k], send_sem=send_sems.at[k], recv_sem=recv_sems.at[k],
            device_id=(x, y, 1 - c), device_id_type=MESH) for k in range(N_CHIPS)]

    def start():
        for cp in copies():
            cp.start()

    def finish():
        for cp in copies():
            cp.wait()

    return start, finish


def _halves_sems():
    return [pltpu.SemaphoreType.DMA((N_CHIPS,)), pltpu.SemaphoreType.DMA((N_CHIPS,))]


def _sibling_halves_call(g, tag):
    _, _, h, n = g.shape

    def body(g_ref, out_ref, send_sems, recv_sems):
        start, finish = _halves_exchange(g_ref, out_ref, send_sems, recv_sems)
        start()
        finish()

    return pl.pallas_call(
        body, name="rs_sibling_halves_" + tag, out_shape=SDS((N_CHIPS, h, n), g.dtype),
        in_specs=[BS(memory_space=pl.ANY)], out_specs=BS(memory_space=pl.ANY),
        scratch_shapes=_halves_sems(),
    )(g)


def _chip_exchange(p_ref, out_ref, send_sems, recv_sems):
    x, y, c = _mesh_pos()
    chips = [(1 - x, y), (x, 1 - y), (1 - x, 1 - y)]

    def copies():
        return [pltpu.make_async_remote_copy(
            src_ref=p_ref.at[2 * cx + cy], dst_ref=out_ref.at[j], send_sem=send_sems.at[j], recv_sem=recv_sems.at[j],
            device_id=(cx, cy, c), device_id_type=MESH) for j, (cx, cy) in enumerate(chips)]

    def start():
        for cp in copies():
            cp.start()

    def finish():
        for cp in copies():
            cp.wait()

    return start, finish


def _exchange_sems():
    return [pltpu.SemaphoreType.DMA((3,)), pltpu.SemaphoreType.DMA((3,))]


def _join_exchange(in_ref, out_ref, send_sems, recv_sems):
    h = in_ref.shape[1]
    q = h // 4
    x, y, c = _mesh_pos()

    def copy(k, half):
        return pltpu.make_async_remote_copy(
            src_ref=in_ref.at[half, pl.ds(k * q, q)], dst_ref=out_ref.at[half, pl.ds(k * q, q)],
            send_sem=send_sems.at[k], recv_sem=recv_sems.at[k],
            device_id=(x, y, 1 - c), device_id_type=MESH)

    def start():
        for k in range(4):
            copy(k, c).start()

    def finish():
        for k in range(4):
            copy(k, c).wait_send()
            copy(k, 1 - c).wait_recv()

    return start, finish


def _join_sems():
    return [pltpu.SemaphoreType.DMA((4,)), pltpu.SemaphoreType.DMA((4,))]


def _sibling_join_call(full, tag):
    def body(in_ref, out_ref, send_sems, recv_sems):
        start, finish = _join_exchange(in_ref, out_ref, send_sems, recv_sems)
        start()
        finish()

    return pl.pallas_call(
        body, name="rs_sibling_join_" + tag, out_shape=SDS(full.shape, full.dtype),
        in_specs=[BS(memory_space=pl.ANY)], out_specs=BS(memory_space=pl.ANY),
        scratch_shapes=_join_sems(), input_output_aliases={0: 0},
    )(full)


def _add_halves_call(g, recv, c_idx, tag):
    _, _, h, n = g.shape
    tr = h // 2

    def body(c_ref, g_ref, r_ref, o_ref):
        o_ref[...] = (g_ref[...].astype(F32) + r_ref[...].astype(F32)).astype(BF16)

    return pl.pallas_call(
        body, name="rs_add_halves_" + tag, out_shape=SDS((N_CHIPS, h, n), BF16),
        grid_spec=pltpu.PrefetchScalarGridSpec(
            num_scalar_prefetch=1, grid=(N_CHIPS, 2),
            in_specs=[BS((None, None, tr, n), lambda k, r, c_ref: (k, c_ref[0], r, 0)),
                      BS((None, tr, n), lambda k, r, c_ref: (k, r, 0))],
            out_specs=BS((None, tr, n), lambda k, r, c_ref: (k, r, 0))),
        compiler_params=_params("arbitrary", "arbitrary"),
    )(c_idx, g, recv)


def _add_chips_call(p, recv, chip_c_idx, tag):
    _, h, n = p.shape
    tr = h // 2

    def body(k_ref, p_ref, r_ref, o_ref):
        acc = p_ref[...].astype(F32)
        for j in range(3):
            acc = acc + r_ref[j].astype(F32)
        o_ref[...] = acc

    return pl.pallas_call(
        body, name="rs_add_chips_" + tag, out_shape=SDS((2, h, n), F32),
        grid_spec=pltpu.PrefetchScalarGridSpec(
            num_scalar_prefetch=1, grid=(2,),
            in_specs=[BS((None, tr, n), lambda r, k_ref: (k_ref[0], r, 0)),
                      BS((3, tr, n), lambda r, k_ref: (0, r, 0))],
            out_specs=BS((None, tr, n), lambda r, k_ref: (k_ref[1], r, 0))),
        compiler_params=_params("arbitrary"),
    )(chip_c_idx, p, recv)


def _load_rows(wg_hbm, w_vmem, sem, off):
    cp = pltpu.make_async_copy(wg_hbm.at[:, pl.ds(off, w_vmem.shape[1]), :], w_vmem, sem)
    cp.start()
    return cp


def _fwd_in_call(x, mod, pre_tm, wg, b_in, pack, order):
    S = x.shape[0]
    tmf = 2 * TM
    nt = S // tmf
    wc = IN_COLS // N_CHIPS

    def body(ord_ref, x_ref, mod_ref, g_ref, w_hbm, b_ref, pack_ref, p_ref, h_hbm, wg_out, w_vmem, h_scr, sem,
             send_sems, recv_sems):
        q, i = pl.program_id(0), pl.program_id(1)
        rows = pl.ds(pl.multiple_of(i * tmf, tmf), tmf)
        start, arrive, drain = _relay_gather(pack_ref, wg_out, send_sems, recv_sems, O_IN, R_IN)

        def load_weights():
            cp = pltpu.make_async_copy(wg_out.at[ord_ref[q], pl.ds(O_IN, R_IN), :], w_vmem, sem)
            cp.start()
            cp.wait()

        @pl.when((q == 0) & (i == 0))
        def _():
            start()
            load_weights()

        for j in range(3):
            @pl.when((q == j + 1) & (i == 0))
            def _(j=j):
                arrive(j)
                load_weights()

        @pl.when(q == 0)
        def _():
            xv = x_ref[...]
            r = lax.rsqrt(jnp.mean(xv * xv, axis=-1, keepdims=True) + EPS)
            h = xv * r * g_ref[...] * (1.0 + mod_ref[:, D:2 * D]) + mod_ref[:, 0:D]
            h_scr[rows, :] = h.astype(BF16)

        hb = h_scr[rows, :]
        for k in range(wc // D):
            p_ref[:, k * D:(k + 1) * D] = _mm(hb, w_vmem[k * D:(k + 1) * D, :]) + b_ref[:, k * D:(k + 1) * D]

        @pl.when((q == N_CHIPS - 1) & (i == nt - 1))
        def _():
            cp = pltpu.make_async_copy(h_scr, h_hbm, sem)
            cp.start()
            drain()
            cp.wait()

    hbm = BS(memory_space=pl.ANY)
    return pl.pallas_call(
        body, name="fwd_in", out_shape=(SDS((S, IN_COLS), F32), SDS((S, D), BF16), SDS(wg.shape, wg.dtype)),
        grid_spec=pltpu.PrefetchScalarGridSpec(
            num_scalar_prefetch=1, grid=(N_CHIPS, nt),
            in_specs=[BS((tmf, D), lambda q, i, o: (jnp.where(q == 0, i, nt - 1), 0)),
                      BS((1, 6 * D), lambda q, i, o: (0, 0)),
                      BS((1, D), lambda q, i, o: (0, 0)), hbm, BS((1, wc), lambda q, i, o: (0, o[q])), hbm],
            out_specs=(BS((tmf, wc), lambda q, i, o: (i, o[q])), hbm, hbm),
            scratch_shapes=[pltpu.VMEM((R_IN, D), BF16), pltpu.VMEM((S, D), BF16), pltpu.SemaphoreType.DMA]
            + _relay_sems()),
        input_output_aliases={4: 2},
        compiler_params=_params("arbitrary", "arbitrary"),
    )(order, x, mod, pre_tm, wg, b_in, pack)


def _lower_bound(lg_ref):
    l0, l1 = lg_ref[0:1, :], lg_ref[1:2, :]
    mx = jnp.maximum(l0, l1)
    e0, e1 = jnp.exp(l0 - mx), jnp.exp(l1 - mx)
    return e0 / (e0 + e1)


def _tri_masks():
    ri = lax.broadcasted_iota(jnp.int32, (CHUNK, CHUNK), 0)
    ci = lax.broadcasted_iota(jnp.int32, (CHUNK, CHUNK), 1)
    return (ri >= ci).astype(F32), (ci >= ri).astype(F32)


def _cumsum_mm(tri, g):
    tb = tri.astype(BF16)
    hi = g.astype(BF16)
    r1 = g - hi.astype(F32)
    mid = r1.astype(BF16)
    lo = (r1 - mid.astype(F32)).astype(BF16)
    return _mm(tb, hi) + _mm(tb, mid) + _mm(tb, lo)


def _hg_gates(q_r, f_r, lb, tril):
    sq = _sig(q_r)
    q = q_r * sq
    sf = _sig(f_r)
    f = lb + (1.0 - lb) * sf
    k = 1.0 - f
    g = jnp.log(f)
    b = _cumsum_mm(tril, g)
    b_last = _rowsum(g)
    row = lax.broadcasted_iota(jnp.int32, g.shape, 0)
    ref = _rowsum(jnp.where(row < CHUNK // 2, g, 0.0))
    e = jnp.exp(b)
    eq = jnp.exp(jnp.minimum(b - ref, 80.0))
    ek = jnp.exp(jnp.minimum(ref - b, 80.0))
    dd = jnp.exp(b_last - b)
    return dict(sq=sq, q=q, sf=sf, f=f, k=k, e=e, eq=eq, ek=ek, dd=dd, elast=jnp.exp(b_last),
                qe=q * e, qt=q * eq, kt=k * ek, kd=k * dd)


def _hgrn_fwd_call(p, logits, gn, wg, pack):
    S = p.shape[0]
    ncb = TB // CHUNK
    ranges = [(O_FF1, R_FF), (O_BRA, 2 * R_BR)]

    def body(q_ref, f_ref, v_ref, og_ref, lg_ref, gn_ref, wg_in, pack_ref, o_ref, oa_ref, st_ref, wg_out,
             st_scr, send_sems, recv_sems):
        start, finish, _, _ = _pack_gather(pack_ref, wg_out, send_sems, recv_sems, ranges)

        @pl.when(pl.program_id(0) == 0)
        def _():
            start()
            st_scr[...] = jnp.zeros_like(st_scr)

        lb = _lower_bound(lg_ref)
        tril, _ = _tri_masks()

        def chunk(ci, carry):
            rows = pl.ds(pl.multiple_of(ci * CHUNK, CHUNK), CHUNK)
            st_ref[ci] = st_scr[...]
            t = _hg_gates(q_ref[rows, :], f_ref[rows, :], lb, tril)
            v = v_ref[rows, :]
            for h in range(HEADS):
                sl = slice(h * DK, (h + 1) * DK)
                stp = st_scr[:, sl]
                vb = v[:, sl].astype(BF16)
                inter = _mm(t["qe"][:, sl].astype(BF16), stp.astype(BF16), NT)
                a = jnp.where(tril > 0.5, _mm(t["qt"][:, sl].astype(BF16), t["kt"][:, sl].astype(BF16), NT), 0.0)
                o = inter + _mm(a.astype(BF16), vb)
                st_scr[:, sl] = stp * t["elast"][:, sl] + _mm(vb, t["kd"][:, sl].astype(BF16), TN)
                oh = o * lax.rsqrt(jnp.mean(o * o, axis=-1, keepdims=True) + EPS)
                og = og_ref[rows, sl]
                o_ref[rows, sl] = o
                oa_ref[rows, sl] = (oh * gn_ref[:, sl] * (og * _sig(og))).astype(BF16)
            return carry

        lax.fori_loop(0, ncb, chunk, 0)

        @pl.when(pl.program_id(0) == S // TB - 1)
        def _():
            finish()

    col = lambda j: BS((TB, D), lambda i, j=j: (i, j))
    hbm = BS(memory_space=pl.ANY)
    return pl.pallas_call(
        body, name="hgrn_fwd", grid=(S // TB,),
        out_shape=(SDS((S, D), F32), SDS((S, D), BF16), SDS((S // CHUNK, DK, D), F32), SDS(wg.shape, wg.dtype)),
        in_specs=[col(0), col(1), col(2), col(3), BS((2, D), lambda i: (0, 0)), BS((1, D), lambda i: (0, 0)),
                  hbm, hbm],
        out_specs=(BS((TB, D), lambda i: (i, 0)), BS((TB, D), lambda i: (i, 0)),
                   BS((ncb, DK, D), lambda i: (i, 0, 0)), hbm),
        scratch_shapes=[pltpu.VMEM((DK, D), F32)] + _gather_sems(len(ranges)),
        input_output_aliases={6: 3},
        compiler_params=_params("arbitrary"),
    )(p, p, p, p, logits, gn, wg, pack)


def _layernorm_stats(uc):
    mu = jnp.mean(uc, axis=-1, keepdims=True)
    xc = uc - mu
    rs = lax.rsqrt(jnp.mean(xc * xc, axis=-1, keepdims=True) + EPS)
    return xc * rs, rs


EXT = HALO + TM + 8


def _fill_shifted(ext, shifted):
    for m in range(1, 8):
        shifted[m - 1] = ext[m:m + HALO + TM, :]


def _window(ext, shifted, s0, n):
    m = s0 % 8
    q = s0 - m
    return ext[q:q + n, :] if m == 0 else shifted[m - 1, q:q + n, :]


def _conv_fwd_call(p, dw, db, ln_g, ln_b, wg, pack):
    S = p.shape[0]
    ranges = [(O_FF2, R_FF), (O_OUT, R_BR)]

    def body(cv_ref, cg_ref, dw_ref, db_ref, g_ref, b_ref, wg_in, pack_ref, u_ref, uc_ref, cb_ref, wg_out,
             uext, ush, send_sems, recv_sems):
        start, finish, _, _ = _pack_gather(pack_ref, wg_out, send_sems, recv_sems, ranges)

        @pl.when(pl.program_id(0) == 0)
        def _():
            start()
            uext[0:HALO, :] = jnp.zeros((HALO, D), F32)
            uext[HALO + TM:EXT, :] = jnp.zeros((EXT - HALO - TM, D), F32)

        u = cv_ref[...] * _sig(cg_ref[...])
        uext[HALO:HALO + TM, :] = u
        u_ref[...] = u
        _fill_shifted(uext, ush)
        for rb in range(TM // SUB):
            acc = jnp.broadcast_to(db_ref[...], (SUB, D))
            for j in range(CONV_K):
                s0 = HALO - (CONV_K - 1) + j + rb * SUB
                acc = acc + dw_ref[j:j + 1, :] * _window(uext, ush, s0, SUB)
            uc_ref[rb * SUB:(rb + 1) * SUB, :] = acc
            xh, _ = _layernorm_stats(acc)
            ln = xh * g_ref[...] + b_ref[...]
            cb_ref[rb * SUB:(rb + 1) * SUB, :] = (ln * _sig(ln)).astype(BF16)
        uext[0:HALO, :] = uext[TM:TM + HALO, :]

        @pl.when(pl.program_id(0) == S // TM - 1)
        def _():
            finish()

    vec = BS((1, D), lambda i: (0, 0))
    hbm = BS(memory_space=pl.ANY)
    return pl.pallas_call(
        body, name="conv_fwd", grid=(S // TM,),
        out_shape=(SDS((S, D), F32), SDS((S, D), F32), SDS((S, D), BF16), SDS(wg.shape, wg.dtype)),
        in_specs=[BS((TM, D), lambda i: (i, 4)), BS((TM, D), lambda i: (i, 5)),
                  BS((CONV_K, D), lambda i: (0, 0)), vec, vec, vec, hbm, hbm],
        out_specs=(BS((TM, D), lambda i: (i, 0)),) * 3 + (hbm,),
        scratch_shapes=[pltpu.VMEM((EXT, D), F32), pltpu.VMEM((7, HALO + TM, D), F32)] + _gather_sems(len(ranges)),
        input_output_aliases={6: 3},
        compiler_params=_params("arbitrary"),
    )(p, p, dw, db, ln_g, ln_b, wg, pack)


def _mm_rows(a, w_ref):
    acc = _mm(a[:, 0:R_BR], w_ref[0])
    for k in range(1, N_CHIPS):
        acc = acc + _mm(a[:, k * R_BR:(k + 1) * R_BR], w_ref[k])
    return acc


def _mm_rows_t(a, w_ref):
    return jnp.concatenate([_mm(a, w_ref[k], NT) for k in range(N_CHIPS)], axis=1)


def _br_spec(off):
    return BS((N_CHIPS, R_BR, D), lambda i: (0, off // R_BR, 0))


def _merge_fwd_call(oa, cb, p, x, mod, post_tm, pre_cm, wg):
    S = x.shape[0]

    def body(oa_ref, cb_ref, ga_ref, gb_ref, x_ref, mod_ref, post_ref, pre_ref, wa_ref, wb_ref, wo_ref,
             ya_ref, yb_ref, mg_ref, y_ref, x2_ref, h2_ref):
        ya = _mm_rows(oa_ref[...], wa_ref)
        yb = _mm_rows(cb_ref[...], wb_ref)
        ya_ref[...] = ya.astype(BF16)
        yb_ref[...] = yb.astype(BF16)
        mg = (_sig(ga_ref[...]) * ya + _sig(gb_ref[...]) * yb).astype(BF16)
        mg_ref[...] = mg
        y = _mm_rows(mg, wo_ref)
        y_ref[...] = y
        n = y * lax.rsqrt(jnp.mean(y * y, axis=-1, keepdims=True) + EPS) * post_ref[...]
        x2 = x_ref[...] + mod_ref[:, 2 * D:3 * D] * n
        x2_ref[...] = x2
        r2 = lax.rsqrt(jnp.mean(x2 * x2, axis=-1, keepdims=True) + EPS)
        h2 = x2 * r2 * pre_ref[...] * (1.0 + mod_ref[:, 4 * D:5 * D]) + mod_ref[:, 3 * D:4 * D]
        h2_ref[...] = h2.astype(BF16)

    tile = BS((TM, D), lambda i: (i, 0))
    vec = BS((1, D), lambda i: (0, 0))
    return pl.pallas_call(
        body, name="merge_fwd", grid=(S // TM,),
        out_shape=(SDS((S, D), BF16), SDS((S, D), BF16), SDS((S, D), BF16), SDS((S, D), F32), SDS((S, D), F32),
                   SDS((S, D), BF16)),
        in_specs=[tile, tile, BS((TM, D), lambda i: (i, 6)), BS((TM, D), lambda i: (i, 7)), tile,
                  BS((1, 6 * D), lambda i: (0, 0)), vec, vec, _br_spec(O_BRA), _br_spec(O_BRB), _br_spec(O_OUT)],
        out_specs=(tile,) * 6,
        compiler_params=_params("arbitrary"),
    )(oa, cb, p, p, x, mod, post_tm, pre_cm, wg, wg, wg)


def _ffn_call(h2, x2, target, mod, post_cm, pre_cm, wg):
    S = x2.shape[0]

    def body(h2_ref, x2_ref, t_ref, mod_ref, post_ref, pre_ref, w_hbm,
             z_ref, da_ref, dy2_ref, dx2_ref, acc_ref, w1_v, w2_v, ra_scr, sems):
        @pl.when(pl.program_id(0) == 0)
        def _():
            c1 = _load_rows(w_hbm, w1_v, sems.at[0], O_FF1)
            c2 = _load_rows(w_hbm, w2_v, sems.at[1], O_FF2)
            c1.wait()
            c2.wait()
            acc_ref[...] = jnp.zeros_like(acc_ref)

        h2 = h2_ref[...]
        for k in range(N_CHIPS):
            ra = jnp.maximum(_mm(h2, w1_v[k]), 0.0)
            ra_scr[:, k * D:(k + 1) * D] = ra
            z_ref[:, k * D:(k + 1) * D] = (ra * ra).astype(BF16)
        y2 = _mm(z_ref[:, 0:D], w2_v[0])
        for k in range(1, N_CHIPS):
            y2 = y2 + _mm(z_ref[:, k * D:(k + 1) * D], w2_v[k])
        ry = lax.rsqrt(jnp.mean(y2 * y2, axis=-1, keepdims=True) + EPS)
        yn = y2 * ry
        n = yn * post_ref[...]
        g2 = mod_ref[:, 5 * D:6 * D]
        x2 = x2_ref[...]
        err = x2 + g2 * n - t_ref[...]
        acc_ref[5:6, :] += _rowsum(err * err) * (0.5 / D)
        dout = err * (1.0 / D)
        acc_ref[0:1, :] += _rowsum(dout * n)
        dn = dout * g2
        acc_ref[1:2, :] += _rowsum(dn * yn)
        dyn = dn * post_ref[...]
        dy2 = (ry * (dyn - yn * jnp.mean(dyn * yn, axis=-1, keepdims=True))).astype(BF16)
        dy2_ref[...] = dy2
        for k in range(N_CHIPS):
            dz = _mm(dy2, w2_v[k], NT)
            da_ref[:, k * D:(k + 1) * D] = (dz * (2.0 * ra_scr[:, k * D:(k + 1) * D])).astype(BF16)
        dh2 = jnp.zeros((TM, D), F32)
        for k in range(N_CHIPS):
            dh2 = dh2 + _mm(da_ref[:, k * D:(k + 1) * D], w1_v[k], NT)
        r2 = lax.rsqrt(jnp.mean(x2 * x2, axis=-1, keepdims=True) + EPS)
        xn = x2 * r2
        yv = xn * pre_ref[...]
        acc_ref[2:3, :] += _rowsum(dh2)
        acc_ref[3:4, :] += _rowsum(dh2 * yv)
        dyv = dh2 * (1.0 + mod_ref[:, 4 * D:5 * D])
        acc_ref[4:5, :] += _rowsum(dyv * xn)
        dxn = dyv * pre_ref[...]
        dx2_ref[...] = dout + r2 * (dxn - xn * jnp.mean(dxn * xn, axis=-1, keepdims=True))

    tile = BS((TM, D), lambda i: (i, 0))
    wide = BS((TM, D_FF), lambda i: (i, 0))
    vec = BS((1, D), lambda i: (0, 0))
    return pl.pallas_call(
        body, name="ffn_fwd_bwd", grid=(S // TM,),
        out_shape=(SDS((S, D_FF), BF16), SDS((S, D_FF), BF16), SDS((S, D), BF16), SDS((S, D), F32),
                   SDS((8, D), F32)),
        in_specs=[tile, tile, tile, BS((1, 6 * D), lambda i: (0, 0)), vec, vec, BS(memory_space=pl.ANY)],
        out_specs=(wide, wide, tile, tile, BS((8, D), lambda i: (0, 0))),
        scratch_shapes=[pltpu.VMEM((N_CHIPS, R_FF, D), BF16), pltpu.VMEM((N_CHIPS, R_FF, D), BF16),
                        pltpu.VMEM((TM, D_FF), F32),
                        pltpu.SemaphoreType.DMA((2,))],
        compiler_params=_params("arbitrary"),
    )(h2, x2, target, mod, post_cm, pre_cm, wg)


def _merge_bwd_call(dx2, y, ya, yb, p, mod, post_tm, wg, g):
    S = y.shape[0]

    def body(dx2_ref, y_ref, ya_ref, yb_ref, ga_ref, gb_ref, mod_ref, post_ref, wa_ref, wb_ref, wo_ref, g_ref,
             dy_ref, dya_ref, dyb_ref, doa_ref, dcb_ref, dpg_ref, acc_ref, bsum_ref, hr_ref, send_sems, recv_sems):
        start, finish = _halves_exchange(g_ref, hr_ref, send_sems, recv_sems)

        @pl.when(pl.program_id(0) == 0)
        def _():
            start()
            acc_ref[...] = jnp.zeros_like(acc_ref)
            bsum_ref[...] = jnp.zeros_like(bsum_ref)

        y = y_ref[...]
        ry = lax.rsqrt(jnp.mean(y * y, axis=-1, keepdims=True) + EPS)
        yn = y * ry
        dx2 = dx2_ref[...]
        acc_ref[0:1, :] += _rowsum(dx2 * (yn * post_ref[...]))
        dn = dx2 * mod_ref[:, 2 * D:3 * D]
        acc_ref[1:2, :] += _rowsum(dn * yn)
        dyn = dn * post_ref[...]
        dy = (ry * (dyn - yn * jnp.mean(dyn * yn, axis=-1, keepdims=True))).astype(BF16)
        dy_ref[...] = dy
        dmg = _mm_rows_t(dy, wo_ref)
        sa, sb = _sig(ga_ref[...]), _sig(gb_ref[...])
        dya = (dmg * sa).astype(BF16)
        dyb = (dmg * sb).astype(BF16)
        dya_ref[...] = dya
        dyb_ref[...] = dyb
        dga = dmg * ya_ref[...].astype(F32) * (sa * (1.0 - sa))
        dgb = dmg * yb_ref[...].astype(F32) * (sb * (1.0 - sb))
        dpg_ref[:, 0:D] = dga.astype(BF16)
        dpg_ref[:, D:2 * D] = dgb.astype(BF16)
        bsum_ref[:, 0:D] += _rowsum(dga)
        bsum_ref[:, D:2 * D] += _rowsum(dgb)
        doa_ref[...] = _mm_rows_t(dya, wa_ref)
        dcb_ref[...] = _mm_rows_t(dyb, wb_ref)

        @pl.when(pl.program_id(0) == S // TM - 1)
        def _():
            finish()

    tile = BS((TM, D), lambda i: (i, 0))
    vec = BS((1, D), lambda i: (0, 0))
    return pl.pallas_call(
        body, name="merge_bwd", grid=(S // TM,),
        out_shape=(SDS((S, D), BF16), SDS((S, D), BF16), SDS((S, D), BF16), SDS((S, D), F32), SDS((S, D), F32),
                   SDS((S, 2 * D), BF16), SDS((8, D), F32), SDS((1, 2 * D), F32),
                   SDS((N_CHIPS,) + g.shape[2:], g.dtype)),
        in_specs=[tile, tile, tile, tile, BS((TM, D), lambda i: (i, 6)), BS((TM, D), lambda i: (i, 7)),
                  BS((1, 6 * D), lambda i: (0, 0)), vec, _br_spec(O_BRA), _br_spec(O_BRB), _br_spec(O_OUT),
                  BS(memory_space=pl.ANY)],
        out_specs=(tile, tile, tile, tile, tile, BS((TM, 2 * D), lambda i: (i, 0)),
                   BS((8, D), lambda i: (0, 0)), BS((1, 2 * D), lambda i: (0, 0)), BS(memory_space=pl.ANY)),
        scratch_shapes=_halves_sems(),
        compiler_params=_params("arbitrary"),
    )(dx2, y, ya, yb, p, p, mod, post_tm, wg, wg, wg, g)


def _hgrn_bwd_call(p, o, doa, st, logits, gn, part, g):
    S = p.shape[0]
    nb = S // TB
    ncb = TB // CHUNK

    def body(q_ref, f_ref, v_ref, og_ref, o_ref, doa_ref, st_ref, lg_ref, gn_ref, part_ref, g_ref,
             dp_ref, bsum_ref, dlg_ref, dgn_ref, recv_ref, hr_ref,
             dst_scr, dlb_scr, dqe_s, dqt_s, dkt_s, dkd_s, dv_s, dog_s, dble_s, send_sems, recv_sems, hs, hr):
        i = pl.program_id(0)
        start, finish = _chip_exchange(part_ref, recv_ref, send_sems, recv_sems)
        start_h, finish_h = _halves_exchange(g_ref, hr_ref, hs, hr)

        @pl.when(i == 0)
        def _():
            start_h()
            start()
            dst_scr[...] = jnp.zeros_like(dst_scr)
            dlb_scr[...] = jnp.zeros_like(dlb_scr)
            bsum_ref[...] = jnp.zeros_like(bsum_ref)
            dgn_ref[...] = jnp.zeros_like(dgn_ref)

        lb = _lower_bound(lg_ref)
        tril, triu = _tri_masks()

        def chunk(tt, carry):
            ci = ncb - 1 - tt
            rows = pl.ds(pl.multiple_of(ci * CHUNK, CHUNK), CHUNK)
            q_r, f_r = q_ref[rows, :], f_ref[rows, :]
            t = _hg_gates(q_r, f_r, lb, tril)
            v = v_ref[rows, :]
            for h in range(HEADS):
                sl = slice(h * DK, (h + 1) * DK)
                stp = st_ref[ci, :, sl]
                stb = stp.astype(BF16)
                qeb = t["qe"][:, sl].astype(BF16)
                qtb = t["qt"][:, sl].astype(BF16)
                ktb = t["kt"][:, sl].astype(BF16)
                kdb = t["kd"][:, sl].astype(BF16)
                vb = v[:, sl].astype(BF16)
                a = jnp.where(tril > 0.5, _mm(qtb, ktb, NT), 0.0)
                o_h = o_ref[rows, sl]
                rinv = lax.rsqrt(jnp.mean(o_h * o_h, axis=-1, keepdims=True) + EPS)
                oh = o_h * rinv
                og = og_ref[rows, sl]
                so = _sig(og)
                d_oa = doa_ref[rows, sl]
                don = d_oa * (og * so)
                dog_s[:, sl] = d_oa * (oh * gn_ref[:, sl]) * _dsilu(og, so)
                dgn_ref[:, sl] += _rowsum(don * oh)
                doh = don * gn_ref[:, sl]
                do = (rinv * (doh - oh * jnp.mean(doh * oh, axis=-1, keepdims=True))).astype(BF16)
                dqe_s[:, sl] = _mm(do, stb, NN)
                dstp = _mm(do, qeb, TN)
                dab = jnp.where(tril > 0.5, _mm(do, vb, NT), 0.0).astype(BF16)
                dqt_s[:, sl] = _mm(dab, ktb, NN)
                dkt_s[:, sl] = _mm(dab, qtb, TN)
                dstn = dst_scr[:, sl]
                dsb = dstn.astype(BF16)
                dkd_s[:, sl] = _mm(vb, dsb, NN)
                dv_s[:, sl] = _mm(a.astype(BF16), do, TN) + _mm(kdb, dsb, NT)
                el = t["elast"][:, sl]
                dst_scr[:, sl] = dstn * el + dstp
                dble_s[:, sl] = el * _rowsum(stp * dstn)
            dqe, dqt, dkt, dkd = dqe_s[...], dqt_s[...], dkt_s[...], dkd_s[...]
            dq = dqe * t["e"] + dqt * t["eq"]
            dk = dkt * t["ek"] + dkd * t["dd"]
            dkk = dkd * t["kd"]
            qt_r = t["qt"].astype(BF16).astype(F32)
            kt_r = t["kt"].astype(BF16).astype(F32)
            dbv = dqe * t["qe"] + dqt * qt_r - dkt * kt_r - dkk
            dg = _cumsum_mm(triu, dbv) + (_rowsum(dkk) + dble_s[...])
            df = dg / t["f"] - dk
            sf = t["sf"]
            dlb_scr[...] += _rowsum(df * (1.0 - sf))
            dqr = dq * _dsilu(q_r, t["sq"])
            dfr = df * (1.0 - lb) * (sf * (1.0 - sf))
            dvv, dog = dv_s[...], dog_s[...]
            dp_ref[rows, 0:D] = dqr.astype(BF16)
            dp_ref[rows, D:2 * D] = dfr.astype(BF16)
            dp_ref[rows, 2 * D:3 * D] = dvv.astype(BF16)
            dp_ref[rows, 3 * D:4 * D] = dog.astype(BF16)
            bsum_ref[:, 0:D] += _rowsum(dqr)
            bsum_ref[:, D:2 * D] += _rowsum(dfr)
            bsum_ref[:, 2 * D:3 * D] += _rowsum(dvv)
            bsum_ref[:, 3 * D:4 * D] += _rowsum(dog)
            return carry

        lax.fori_loop(0, ncb, chunk, 0)

        dl = dlb_scr[...] * lb * (1.0 - lb)
        dlg_ref[0:1, :] = dl
        dlg_ref[1:2, :] = -dl

        @pl.when(i == nb - 1)
        def _():
            finish_h()
            finish()

    col = lambda j: BS((TB, D), lambda i, j=j: (nb - 1 - i, j))
    rev = BS((TB, D), lambda i: (nb - 1 - i, 0))
    cd = pltpu.VMEM((CHUNK, D), F32)
    return pl.pallas_call(
        body, name="hgrn_bwd", grid=(nb,),
        out_shape=(SDS((S, 4 * D), BF16), SDS((1, 4 * D), F32), SDS((2, D), F32), SDS((1, D), F32),
                   SDS((3,) + part.shape[1:], part.dtype), SDS((N_CHIPS,) + g.shape[2:], g.dtype)),
        in_specs=[col(0), col(1), col(2), col(3), rev, rev, BS((ncb, DK, D), lambda i: (nb - 1 - i, 0, 0)),
                  BS((2, D), lambda i: (0, 0)), BS((1, D), lambda i: (0, 0)), BS(memory_space=pl.ANY),
                  BS(memory_space=pl.ANY)],
        out_specs=(BS((TB, 4 * D), lambda i: (nb - 1 - i, 0)), BS((1, 4 * D), lambda i: (0, 0)),
                   BS((2, D), lambda i: (0, 0)), BS((1, D), lambda i: (0, 0)), BS(memory_space=pl.ANY),
                   BS(memory_space=pl.ANY)),
        scratch_shapes=[pltpu.VMEM((DK, D), F32), pltpu.VMEM((1, D), F32), cd, cd, cd, cd, cd, cd,
                        pltpu.VMEM((1, D), F32)] + _exchange_sems() + _halves_sems(),
        compiler_params=_params("arbitrary"),
    )(p, p, p, p, o, doa, st, logits, gn, part, g)


def _conv_bwd_call(dcb, uc, u, p, dw, ln_g, ln_b, part):
    S = uc.shape[0]
    nb = S // TM
    hb = TM // HALO

    def body(dcb_ref, uc_ref, u_ref, uh_ref, cv_ref, cg_ref, dw_ref, g_ref, b_ref, part_ref,
             dp_ref, bsum_ref, ddw_ref, acc_ref, recv_ref, uext, dext, ush, dsh, send_sems, recv_sems):
        i = pl.program_id(0)
        start, finish = _chip_exchange(part_ref, recv_ref, send_sems, recv_sems)

        @pl.when(i == 0)
        def _():
            start()
            dext[TM:EXT, :] = jnp.zeros((EXT - TM, D), F32)
            uext[HALO + TM:EXT, :] = jnp.zeros((EXT - HALO - TM, D), F32)
            bsum_ref[...] = jnp.zeros_like(bsum_ref)
            ddw_ref[...] = jnp.zeros_like(ddw_ref)
            acc_ref[...] = jnp.zeros_like(acc_ref)

        first_tile = (nb - 1 - i) == 0
        uext[0:HALO, :] = jnp.where(first_tile, 0.0, uh_ref[...])
        uext[HALO:HALO + TM, :] = u_ref[...]
        _fill_shifted(uext, ush)

        for rb in range(TM // SUB):
            rs_ = slice(rb * SUB, (rb + 1) * SUB)
            xh, rs = _layernorm_stats(uc_ref[rs_, :])
            ln = xh * g_ref[...] + b_ref[...]
            dln = dcb_ref[rs_, :] * _dsilu(ln, _sig(ln))
            acc_ref[1:2, :] += _rowsum(dln * xh)
            acc_ref[2:3, :] += _rowsum(dln)
            dxh = dln * g_ref[...]
            duc = rs * (dxh - jnp.mean(dxh, axis=-1, keepdims=True)
                        - xh * jnp.mean(dxh * xh, axis=-1, keepdims=True))
            dext[rs_, :] = duc
            acc_ref[0:1, :] += _rowsum(duc)
        _fill_shifted(dext, dsh)

        for j in range(CONV_K):
            part = jnp.zeros((SUB, D), F32)
            for rb in range(TM // SUB):
                s0 = HALO - (CONV_K - 1) + j + rb * SUB
                part = part + dext[rb * SUB:(rb + 1) * SUB, :] * _window(uext, ush, s0, SUB)
            ddw_ref[j:j + 1, :] += _rowsum(part)

        for rb in range(TM // SUB):
            rs_ = slice(rb * SUB, (rb + 1) * SUB)
            du = jnp.zeros((SUB, D), F32)
            for j in range(CONV_K):
                s0 = rb * SUB + (CONV_K - 1) - j
                du = du + dw_ref[j:j + 1, :] * _window(dext, dsh, s0, SUB)
            cg = cg_ref[rs_, :]
            sg = _sig(cg)
            dcv = du * sg
            dcg = du * cv_ref[rs_, :] * (sg * (1.0 - sg))
            dp_ref[rs_, 0:D] = dcv.astype(BF16)
            dp_ref[rs_, D:2 * D] = dcg.astype(BF16)
            bsum_ref[:, 0:D] += _rowsum(dcv)
            bsum_ref[:, D:2 * D] += _rowsum(dcg)

        dext[TM:TM + HALO, :] = dext[0:HALO, :]

        @pl.when(i == nb - 1)
        def _():
            finish()

    rev = BS((TM, D), lambda i: (nb - 1 - i, 0))
    vec = BS((1, D), lambda i: (0, 0))
    return pl.pallas_call(
        body, name="conv_bwd", grid=(nb,),
        out_shape=(SDS((S, 2 * D), BF16), SDS((1, 2 * D), F32), SDS((32, D), F32), SDS((8, D), F32),
                   SDS((3,) + part.shape[1:], part.dtype)),
        in_specs=[rev, rev, rev, BS((HALO, D), lambda i: (jnp.maximum((nb - 1 - i) * hb - 1, 0), 0)),
                  BS((TM, D), lambda i: (nb - 1 - i, 4)), BS((TM, D), lambda i: (nb - 1 - i, 5)),
                  BS((CONV_K, D), lambda i: (0, 0)), vec, vec, BS(memory_space=pl.ANY)],
        out_specs=(BS((TM, 2 * D), lambda i: (nb - 1 - i, 0)), BS((1, 2 * D), lambda i: (0, 0)),
                   BS((32, D), lambda i: (0, 0)), BS((8, D), lambda i: (0, 0)), BS(memory_space=pl.ANY)),
        scratch_shapes=[pltpu.VMEM((EXT, D), F32), pltpu.VMEM((EXT, D), F32),
                        pltpu.VMEM((7, HALO + TM, D), F32), pltpu.VMEM((7, HALO + TM, D), F32)] + _exchange_sems(),
        compiler_params=_params("arbitrary"),
    )(dcb, uc, u, u, p, p, dw, ln_g, ln_b, part)


def _in_bwd_call(dp_hg, dp_cv, dp_gt, x, dx2, mod, pre_tm, wg, part, full_a, full_b):
    S = x.shape[0]

    def body(hg_ref, cv_ref, gt_ref, x_ref, dx2_ref, mod_ref, g_ref, w_hbm, part_ref, fa_in, fb_in,
             gx_ref, acc_ref, recv_ref, fa_out, fb_out, w_vmem, sem, send_sems, recv_sems, sa, ra, sb, rb):
        start, finish = _chip_exchange(part_ref, recv_ref, send_sems, recv_sems)
        start_a, finish_a = _join_exchange(fa_in, fa_out, sa, ra)
        start_b, finish_b = _join_exchange(fb_in, fb_out, sb, rb)

        @pl.when(pl.program_id(0) == 0)
        def _():
            start_a()
            start_b()
            start()
            _load_rows(w_hbm, w_vmem, sem, O_IN).wait()
            acc_ref[...] = jnp.zeros_like(acc_ref)

        dh = jnp.zeros((TM, D), F32)
        for k in range(IN_COLS // D):
            src, kk = ((hg_ref, k), (cv_ref, k - 4), (gt_ref, k - 6))[0 if k < 4 else (1 if k < 6 else 2)]
            dh = dh + _mm(src[:, kk * D:(kk + 1) * D], w_vmem[k // 2, (k % 2) * D:(k % 2 + 1) * D, :], NT)
        xv = x_ref[...]
        r = lax.rsqrt(jnp.mean(xv * xv, axis=-1, keepdims=True) + EPS)
        xn = xv * r
        yv = xn * g_ref[...]
        acc_ref[0:1, :] += _rowsum(dh)
        acc_ref[1:2, :] += _rowsum(dh * yv)
        dyv = dh * (1.0 + mod_ref[:, D:2 * D])
        acc_ref[2:3, :] += _rowsum(dyv * xn)
        dxn = dyv * g_ref[...]
        gx_ref[...] = dx2_ref[...] + r * (dxn - xn * jnp.mean(dxn * xn, axis=-1, keepdims=True))

        @pl.when(pl.program_id(0) == S // TM - 1)
        def _():
            finish_a()
            finish_b()
            finish()

    tile = BS((TM, D), lambda i: (i, 0))
    hbm = BS(memory_space=pl.ANY)
    return pl.pallas_call(
        body, name="in_bwd", grid=(S // TM,),
        out_shape=(SDS((S, D), F32), SDS((8, D), F32), SDS((3,) + part.shape[1:], part.dtype),
                   SDS(full_a.shape, full_a.dtype), SDS(full_b.shape, full_b.dtype)),
        in_specs=[BS((TM, 4 * D), lambda i: (i, 0)), BS((TM, 2 * D), lambda i: (i, 0)),
                  BS((TM, 2 * D), lambda i: (i, 0)), tile, tile, BS((1, 6 * D), lambda i: (0, 0)),
                  BS((1, D), lambda i: (0, 0)), hbm, hbm, hbm, hbm],
        out_specs=(tile, BS((8, D), lambda i: (0, 0)), hbm, hbm, hbm),
        scratch_shapes=[pltpu.VMEM((N_CHIPS, R_IN, D), BF16), pltpu.SemaphoreType.DMA] + _exchange_sems()
        + _join_sems() + _join_sems(),
        input_output_aliases={9: 3, 10: 4},
        compiler_params=_params("arbitrary"),
    )(dp_hg, dp_cv, dp_gt, x, dx2, mod, pre_tm, wg, part, full_a, full_b)


def _wgrad_call(gp, a, b, name, bm, place, rows):
    S, M = a.shape
    N = b.shape[1]
    bk = min(S, 1024)
    nk = S // bk

    def body(a_ref, b_ref, *rest):
        o_ref, acc = rest[-2], rest[-1]
        k = pl.program_id(2)

        @pl.when(k == 0)
        def _():
            acc[...] = jnp.zeros_like(acc)

        acc[...] += _mm(a_ref[...], b_ref[...], TN)

        @pl.when(k == nk - 1)
        def _():
            o_ref[...] = acc[...].astype(BF16)

    in_specs = [BS((bk, bm), lambda i, j, k: (k, i)), BS((bk, D), lambda i, j, k: (k, j))]
    args = [a, b]
    if gp is not None:
        in_specs.append(BS(memory_space=pl.ANY))
        args.append(gp)
    return pl.pallas_call(
        body, name=name, grid=(M // bm, N // D, nk),
        out_shape=SDS((N_CHIPS, rows, D), BF16),
        in_specs=in_specs,
        out_specs=BS((None, bm, D), lambda i, j, k: (*place(i, j), 0)),
        scratch_shapes=[pltpu.VMEM((bm, D), F32)],
        input_output_aliases={} if gp is None else {2: 0},
        compiler_params=_params("parallel", "parallel", "arbitrary"),
    )(*args)


def _wgrad_rows_call(gp, a, b, name, blk):
    S = a.shape[0]
    bk = min(S, 1024)
    nk = S // bk

    def body(a_ref, b_ref, *rest):
        o_ref, acc = rest[-2], rest[-1]
        k = pl.program_id(0)

        @pl.when(k == 0)
        def _():
            acc[...] = jnp.zeros_like(acc)

        acc[...] += _mm(a_ref[...], b_ref[...], TN)

        @pl.when(k == nk - 1)
        def _():
            for c in range(N_CHIPS):
                o_ref[c] = acc[c * R_BR:(c + 1) * R_BR, :].astype(BF16)

    in_specs = [BS((bk, D), lambda k: (k, 0)), BS((bk, D), lambda k: (k, 0))]
    args = [a, b]
    if gp is not None:
        in_specs.append(BS(memory_space=pl.ANY))
        args.append(gp)
    return pl.pallas_call(
        body, name=name, grid=(nk,),
        out_shape=SDS((N_CHIPS, 3 * R_BR, D), BF16),
        in_specs=in_specs,
        out_specs=BS((N_CHIPS, R_BR, D), lambda k: (0, blk, 0)),
        scratch_shapes=[pltpu.VMEM((D, D), F32)],
        input_output_aliases={} if gp is None else {2: 0},
        compiler_params=_params("arbitrary"),
    )(*args)


def _outer_call(cact, dmod):
    n = dmod.shape[1]

    def body(a_ref, b_ref, o_ref):
        o_ref[...] = _mm(a_ref[...], b_ref[...], TN, HI)

    return pl.pallas_call(
        body, name="wgrad_ada", out_shape=SDS((D, n), F32),
        compiler_params=pltpu.CompilerParams(vmem_limit_bytes=VMEM_LIMIT),
    )(cact, dmod)


def _adamw_call(w, g, m, v, name):
    R, C = w.shape
    tr = R
    while tr * C > 512 * 1024 and tr % 16 == 0:
        tr //= 2
    c1 = 1.0 - ADAM_B1 ** ADAM_STEP
    c2 = 1.0 - ADAM_B2 ** ADAM_STEP

    def body(w_ref, g_ref, m_ref, v_ref, d_ref, m2_ref, v2_ref):
        g = g_ref[...]
        m2 = ADAM_B1 * m_ref[...] + (1.0 - ADAM_B1) * g
        v2 = ADAM_B2 * v_ref[...] + (1.0 - ADAM_B2) * (g * g)
        m2_ref[...] = m2
        v2_ref[...] = v2
        d_ref[...] = -ADAM_LR * ((m2 / c1) / (jnp.sqrt(v2 / c2) + ADAM_EPS) + ADAM_WD * w_ref[...])

    tile = BS((tr, C), lambda i: (i, 0))
    return pl.pallas_call(
        body, name=name, grid=(R // tr,), out_shape=(SDS((R, C), F32),) * 3,
        in_specs=[tile] * 4, out_specs=(tile,) * 3, compiler_params=_params("parallel"),
    )(w, g, m, v)


def _adamw_gather_call(w, g, m, v, srows, name):
    R, C = w.shape
    tr = R
    while tr * C > 512 * 1024 and tr % 16 == 0:
        tr //= 2
    nsteps = R // tr
    mr = srows.shape[0]
    c1 = 1.0 - ADAM_B1 ** ADAM_STEP
    c2 = 1.0 - ADAM_B2 ** ADAM_STEP

    def body(w_ref, g_ref, m_ref, v_ref, s_ref, d_ref, m2_ref, v2_ref, all_ref, sum_ref,
             x_scr, out_scr, send_sems, recv_sems, local_sem):
        i = pl.program_id(0)
        start, finish = _allgather_parts(x_scr, out_scr, send_sems, recv_sems, local_sem)

        @pl.when(i == 0)
        def _():
            x_scr[...] = s_ref[...]
            start()

        g = g_ref[...]
        m2 = ADAM_B1 * m_ref[...] + (1.0 - ADAM_B1) * g
        v2 = ADAM_B2 * v_ref[...] + (1.0 - ADAM_B2) * (g * g)
        m2_ref[...] = m2
        v2_ref[...] = v2
        d_ref[...] = -ADAM_LR * ((m2 / c1) / (jnp.sqrt(v2 / c2) + ADAM_EPS) + ADAM_WD * w_ref[...])

        @pl.when(i == nsteps - 1)
        def _():
            finish()
            all_ref[...] = out_scr[...]
            acc = out_scr[0:mr, :]
            for d in range(1, N_DEV):
                acc = acc + out_scr[d * mr:(d + 1) * mr, :]
            sum_ref[...] = acc

    tile = BS((tr, C), lambda i: (i, 0))
    return pl.pallas_call(
        body, name=name, grid=(nsteps,),
        out_shape=(SDS((R, C), F32),) * 3 + (SDS((N_DEV * mr, D), F32), SDS((mr, D), F32)),
        in_specs=[tile] * 4 + [BS((mr, D), lambda i: (0, 0))],
        out_specs=(tile,) * 3 + (BS((N_DEV * mr, D), lambda i: (0, 0)), BS((mr, D), lambda i: (0, 0))),
        scratch_shapes=[pltpu.VMEM((mr, D), F32), pltpu.VMEM((N_DEV * mr, D), F32)] + _allgather_sems(),
        compiler_params=_params("arbitrary"),
    )(w, g, m, v, srows)


def _rs_begin(g, c_idx, tag):
    n = g.shape[1]
    g = g.reshape(N_CHIPS, 2, n // 2, D)
    return _add_halves_call(g, _sibling_halves_call(g, tag), c_idx, tag)


def _rs_end(part, recv, c_idx, chip_idx, tag):
    n = 2 * part.shape[1]
    full = _add_chips_call(part, recv, jnp.concatenate([chip_idx, c_idx]), tag)
    return _sibling_join_call(full, tag).reshape(n, D)


def _local_step(x, mod, cact, target, wg, pack, small, c_idx, chip_idx):
    p, h1, wg = _fwd_in_call(x, mod, small["pre_tm"], wg, small["b_in"], pack, small["order"])
    o, oa, st, wg = _hgrn_fwd_call(p, small["logits"], small["hg_norm"], wg, pack)
    u, uc, cb, wg = _conv_fwd_call(p, small["conv_dw"], small["conv_db"], small["ln_g"], small["ln_b"], wg, pack)
    ya, yb, mg, y, x2, h2 = _merge_fwd_call(oa, cb, p, x, mod, small["post_tm"], small["pre_cm"], wg)
    z, da, dy2, dx2, acc_f = _ffn_call(h2, x2, target, mod, small["post_cm"], small["pre_cm"], wg)

    g_ff = _wgrad_call(None, h2, da, "wgrad_ff1", D, lambda i, j: (j, 0), 2 * R_FF)
    g_ff = _wgrad_call(g_ff, z, dy2, "wgrad_ff2", D, lambda i, j: (i, 1), 2 * R_FF)
    g_ff = g_ff.reshape(N_CHIPS, 2, R_FF, D)
    dy, dya, dyb, doa, dcb, dp_gt, acc_m, bs_gt, hr_ff = _merge_bwd_call(dx2, y, ya, yb, p, mod, small["post_tm"],
                                                                        wg, g_ff)
    part_ff = _add_halves_call(g_ff, hr_ff, c_idx, "ff")

    g_br = _wgrad_rows_call(None, oa, dya, "wgrad_br_a", 0)
    g_br = _wgrad_rows_call(g_br, cb, dyb, "wgrad_br_b", 1)
    g_br = _wgrad_rows_call(g_br, mg, dy, "wgrad_out", 2)
    g_br = g_br.reshape(N_CHIPS, 2, 3 * R_BR // 2, D)
    dp_hg, bs_hg, dlg, dgn, recv_ff, hr_br = _hgrn_bwd_call(p, o, doa, st, small["logits"], small["hg_norm"],
                                                            part_ff, g_br)
    part_br = _add_halves_call(g_br, hr_br, c_idx, "br")
    dp_cv, bs_cv, ddw, acc_c, recv_br = _conv_bwd_call(dcb, uc, u, p, small["conv_dw"], small["ln_g"], small["ln_b"],
                                                        part_br)

    g_in = _wgrad_call(None, h1, dp_hg, "wgrad_in_hg", D, lambda i, j: (j // 2, j % 2), R_IN)
    g_in = _wgrad_call(g_in, h1, dp_cv, "wgrad_in_cv", D, lambda i, j: (2, j), R_IN)
    g_in = _wgrad_call(g_in, h1, dp_gt, "wgrad_in_gt", D, lambda i, j: (3, j), R_IN)
    part_in = _rs_begin(g_in, c_idx, "in")
    chip_c = jnp.concatenate([chip_idx, c_idx])
    full_ff = _add_chips_call(part_ff, recv_ff, chip_c, "ff")
    full_br = _add_chips_call(part_br, recv_br, chip_c, "br")
    gx, acc_i, recv_in, full_ff, full_br = _in_bwd_call(dp_hg, dp_cv, dp_gt, x, dx2, mod, small["pre_tm"], wg,
                                                        part_in, full_ff, full_br)
    red_ff = full_ff.reshape(2 * R_FF, D)
    red_br = full_br.reshape(3 * R_BR, D)
    red_in = _rs_end(part_in, recv_in, c_idx, chip_idx, "in")

    zrow = jnp.zeros((1, D), F32)
    rows = [acc_i[0:1], acc_i[1:2], acc_m[0:1], acc_f[2:3], acc_f[3:4], acc_f[0:1],
            acc_i[2:3], acc_m[1:2], acc_f[4:5], acc_f[1:2],
            jnp.concatenate([bs_hg, bs_cv, bs_gt], axis=1).reshape(8, D),
            dlg, dgn, acc_c[0:1], acc_c[1:2], acc_c[2:3],
            ddw,
            cact, acc_f[5:6]] + [zrow] * 6
    return gx, jnp.concatenate(rows, axis=0), red_in, red_ff, red_br


def kernel(x, c, w_ada, b_ada, pre_norm_tm, post_norm_tm, pre_norm_cm, post_norm_cm, w_in, b_in, hg_lb_logits, hg_norm, conv_dw, conv_db, conv_ln_g, conv_ln_b, w_br_a, w_br_b, w_out, w_ff1, w_ff2, loss_target, m_w_ada, m_b_ada, m_pre_norm_tm, m_post_norm_tm, m_pre_norm_cm, m_post_norm_cm, m_w_in, m_b_in, m_hg_lb_logits, m_hg_norm, m_conv_dw, m_conv_db, m_conv_ln_g, m_conv_ln_b, m_w_br_a, m_w_br_b, m_w_out, m_w_ff1, m_w_ff2, v_w_ada, v_b_ada, v_pre_norm_tm, v_post_norm_tm, v_pre_norm_cm, v_post_norm_cm, v_w_in, v_b_in, v_hg_lb_logits, v_hg_norm, v_conv_dw, v_conv_db, v_conv_ln_g, v_conv_ln_b, v_w_br_a, v_w_br_b, v_w_out, v_w_ff1, v_w_ff2):
    xi, yi, ci = lax.axis_index("x"), lax.axis_index("y"), lax.axis_index("c")
    chip = 2 * xi + yi
    c_idx = jnp.reshape(ci, (1,)).astype(jnp.int32)
    chip_idx = jnp.reshape(chip, (1,)).astype(jnp.int32)

    def pack_small(ada_b, pre_t, post_t, pre_c, post_c, in_b, lg, hgn, cdb, lng, lnb, cdw):
        flat = jnp.concatenate([cdw[0].reshape(-1), jnp.zeros((8 * D - CONV_K * 256,), F32)]).reshape(8, D)
        return jnp.concatenate([ada_b.reshape(6, D), pre_t, post_t, pre_c, post_c, in_b.reshape(8, D), lg, hgn,
                                cdb, lng, lnb, flat], axis=0)

    w_in_halves = w_in[0].reshape(D, 2, D).transpose(1, 0, 2).reshape(R_IN, D)
    pack = jnp.concatenate([w_in_halves, w_ff1[0], w_ff2[0], w_br_a[0], w_br_b[0], w_out[0]],
                           axis=0).astype(BF16)
    wg = lax.dynamic_update_slice(lax.empty((N_CHIPS, PACK_W, D), BF16), pack[None], (chip, 0, 0))
    wa = 6 * D // N_CHIPS
    me = 4 * xi + 2 * yi + ci
    dw_blk = jnp.concatenate([conv_dw[0].reshape(-1), jnp.zeros((8 * D - CONV_K * 256,), F32)]).reshape(8, D)
    dw_all, ca_all, mod_all = _prologue_call(
        dw_blk, jnp.broadcast_to(c, (8, D)), w_ada[0].astype(BF16),
        lax.dynamic_slice_in_dim(b_ada, chip * wa, wa, axis=1))
    order = jnp.stack([chip, 2 * (1 - xi) + yi, 2 * xi + (1 - yi), 2 * (1 - xi) + (1 - yi)]).astype(jnp.int32)
    dw_all = dw_all.reshape(N_CHIPS, 2, 8 * D)[:, 0, :CONV_K * 256].reshape(N_CHIPS, CONV_K, 256)
    dw_full = dw_all.transpose(1, 0, 2).reshape(CONV_K, D)
    cact = lax.dynamic_slice_in_dim(ca_all, me * 8, 1, axis=0)
    mod_mine = lax.dynamic_index_in_dim(mod_all.reshape(N_CHIPS, 2, N_DEV, 8, wa)[:, 0, :, 0, :], me, axis=1,
                                        keepdims=False)
    mod = mod_mine.reshape(1, 6 * D)

    small = dict(b_ada=b_ada, pre_tm=pre_norm_tm, post_tm=post_norm_tm, pre_cm=pre_norm_cm, post_cm=post_norm_cm,
                 b_in=b_in, logits=hg_lb_logits, hg_norm=hg_norm, conv_dw=dw_full, conv_db=conv_db,
                 ln_g=conv_ln_g, ln_b=conv_ln_b, order=order)

    gx, srows, red_in, red_ff, red_br = _local_step(x[0], mod, cact, loss_target[0], wg, pack, small, c_idx,
                                                    chip_idx)

    shapes = {"in": w_in.shape, "br_a": w_br_a.shape, "br_b": w_br_b.shape, "out": w_out.shape,
              "ff1": w_ff1.shape, "ff2": w_ff2.shape}
    offs = {"in": (red_in, 0, R_IN), "ff1": (red_ff, 0, R_FF), "ff2": (red_ff, R_FF, 2 * R_FF),
            "br_a": (red_br, 0, R_BR), "br_b": (red_br, R_BR, 2 * R_BR), "out": (red_br, 2 * R_BR, 3 * R_BR)}
    wmv = {"in": (w_in, m_w_in, v_w_in), "br_a": (w_br_a, m_w_br_a, v_w_br_a), "br_b": (w_br_b, m_w_br_b, v_w_br_b),
           "out": (w_out, m_w_out, v_w_out), "ff1": (w_ff1, m_w_ff1, v_w_ff1), "ff2": (w_ff2, m_w_ff2, v_w_ff2)}
    res = {}
    for n in offs:
        shp = shapes[n]
        g2d = offs[n][0][offs[n][1]:offs[n][2]]
        if n == "in":
            g2d = g2d.reshape(2, D, D).transpose(1, 0, 2)
        g2d = g2d.reshape(shp[1], shp[2])
        w_, m_, v_ = (a[0] for a in wmv[n])
        if n == "in":
            d_, m2_, v2_, sall, ssum = _adamw_gather_call(w_, g2d, m_, v_, srows, "adamw_in")
        else:
            d_, m2_, v2_ = _adamw_call(w_, g2d, m_, v_, "adamw_" + n)
        res[n] = tuple(a.reshape(shp) for a in (g2d, d_, m2_, v2_))

    sall = sall.reshape(N_DEV, SMALL_ROWS, D)
    loss = jnp.sum(ssum[57])
    dmod_all = sall[:, 0:6, :].reshape(N_DEV, 6 * D)
    g_ada = _outer_call(sall[:, 56, :], lax.dynamic_slice_in_dim(dmod_all, chip * wa, wa, axis=1))
    g_dw = lax.dynamic_slice_in_dim(ssum[24:24 + CONV_K], chip * 256, 256, axis=1)
    g_small = jnp.concatenate(
        [ssum[0:24], jnp.concatenate([g_dw.reshape(-1), jnp.zeros((8 * D - CONV_K * 256,), F32)]).reshape(8, D)],
        axis=0)
    d_, m2_, v2_ = _adamw_call(w_ada[0], g_ada, m_w_ada[0], v_w_ada[0], "adamw_ada")
    res["ada"] = tuple(a.reshape(w_ada.shape) for a in (g_ada, d_, m2_, v2_))

    ws = pack_small(b_ada, pre_norm_tm, post_norm_tm, pre_norm_cm, post_norm_cm, b_in, hg_lb_logits, hg_norm,
                    conv_db, conv_ln_g, conv_ln_b, conv_dw)
    ms = pack_small(m_b_ada, m_pre_norm_tm, m_post_norm_tm, m_pre_norm_cm, m_post_norm_cm, m_b_in, m_hg_lb_logits,
                    m_hg_norm, m_conv_db, m_conv_ln_g, m_conv_ln_b, m_conv_dw)
    vs = pack_small(v_b_ada, v_pre_norm_tm, v_post_norm_tm, v_pre_norm_cm, v_post_norm_cm, v_b_in, v_hg_lb_logits,
                    v_hg_norm, v_conv_db, v_conv_ln_g, v_conv_ln_b, v_conv_dw)
    sres = (g_small,) + tuple(_adamw_call(ws, g_small, ms, vs, "adamw_small"))

    def unpack_small(t):
        return {"b_ada": t[0:6].reshape(1, 6 * D), "pre_tm": t[6:7], "post_tm": t[7:8], "pre_cm": t[8:9],
                "post_cm": t[9:10], "b_in": t[10:18].reshape(1, IN_COLS), "logits": t[18:20], "hg_norm": t[20:21],
                "conv_db": t[21:22], "ln_g": t[22:23], "ln_b": t[23:24],
                "conv_dw": t[24:32].reshape(-1)[:CONV_K * 256].reshape(1, CONV_K, 256)}

    order = ["ada", "b_ada", "pre_tm", "post_tm", "pre_cm", "post_cm", "in", "b_in", "logits", "hg_norm", "conv_dw",
             "conv_db", "ln_g", "ln_b", "br_a", "br_b", "out", "ff1", "ff2"]
    outs = [loss, gx.reshape(x.shape)]
    for kind in range(4):
        sm = unpack_small(sres[kind])
        for n in order:
            outs.append(res[n][kind] if n in res else sm[n])
    return tuple(outs)
```

```python
import functools

import jax
import jax.numpy as jnp
from jax import lax
from jax.experimental import pallas as pl
from jax.experimental.pallas import tpu as pltpu

F32, BF16 = jnp.float32, jnp.bfloat16
SDS = jax.ShapeDtypeStruct
BS = pl.BlockSpec
MESH = pl.DeviceIdType.MESH
HI = lax.Precision.HIGHEST

D = 1024
D_FF = 4096
IN_COLS = 8192
HEADS, DK = 8, 128
CHUNK = 128
CONV_K = 31
HALO = 32
SUB = 32
EPS = 1e-6
N_CHIPS, N_DEV = 4, 8
TM = 256
TB = 256
VMEM_LIMIT = 56 * 1024 * 1024

R_IN, R_BR, R_FF = 2048, 256, 1024
PACK_W = R_IN + 3 * R_BR + 2 * R_FF
O_IN, O_FF1, O_FF2, O_BRA, O_BRB, O_OUT = 0, 2048, 3072, 4096, 4352, 4608
SMALL_ROWS = 64

ADAM_LR, ADAM_B1, ADAM_B2, ADAM_EPS, ADAM_WD, ADAM_STEP = 0.001, 0.9, 0.999, 1e-08, 0.01, 10

NN = (((1,), (0,)), ((), ()))
NT = (((1,), (1,)), ((), ()))
TN = (((0,), (0,)), ((), ()))


def _mm(a, b, dims=NN, precision=None):
    return lax.dot_general(a, b, dims, preferred_element_type=F32, precision=precision)


def _sig(v):
    return jax.nn.sigmoid(v)


def _dsilu(v, s):
    return s * (1.0 + v * (1.0 - s))


def _params(*sem):
    return pltpu.CompilerParams(dimension_semantics=sem if sem else None, vmem_limit_bytes=VMEM_LIMIT)


def _rowsum(v):
    return jnp.sum(v, axis=0, keepdims=True)


def _mesh_pos():
    return lax.axis_index("x"), lax.axis_index("y"), lax.axis_index("c")


def _allgather_parts(x_ref, out_ref, send_sems, recv_sems, local_sem):
    m_per = x_ref.shape[0]
    x, y, c = _mesh_pos()
    me, sibling = (x, y, c), (x, y, 1 - c)
    chips = [(1 - x, y), (x, 1 - y), (1 - x, 1 - y)]

    def rows(px, py, pc):
        return out_ref.at[pl.ds((4 * px + 2 * py + pc) * m_per, m_per), :]

    def copy(k, block, to, src=None):
        return pltpu.make_async_remote_copy(
            src_ref=rows(*block) if src is None else src, dst_ref=rows(*block),
            send_sem=send_sems.at[k], recv_sem=recv_sems.at[k], device_id=to, device_id_type=MESH)

    def first():
        return [copy(0, me, sibling, src=x_ref)] + [copy(1 + j, me, (*chip, c), src=x_ref)
                                                    for j, chip in enumerate(chips)]

    def start():
        pltpu.make_async_copy(x_ref, rows(*me), local_sem).start()
        for cp in first():
            cp.start()

    def finish():
        passed = [copy(4 + j, (*chip, c), sibling) for j, chip in enumerate(chips)]
        for j, chip in enumerate(chips):
            copy(1 + j, (*chip, c), me).wait_recv()
            passed[j].start()
        copy(0, sibling, me).wait_recv()
        for j, chip in enumerate(chips):
            copy(4 + j, (*chip, 1 - c), me).wait_recv()
        for cp in first() + passed:
            cp.wait_send()
        pltpu.make_async_copy(x_ref, rows(*me), local_sem).wait()

    return start, finish


def _allgather(x_ref, out_ref, send_sems, recv_sems, local_sem):
    start, finish = _allgather_parts(x_ref, out_ref, send_sems, recv_sems, local_sem)
    start()
    finish()


def _allgather_sems():
    return [pltpu.SemaphoreType.DMA((7,)), pltpu.SemaphoreType.DMA((7,)), pltpu.SemaphoreType.DMA]


def _allgather_call(blk, name, in_vmem, with_sum):
    m_per, n = blk.shape

    def body(x_ref, out_ref, *rest):
        if with_sum:
            sum_ref, send_sems, recv_sems, local_sem = rest
        else:
            send_sems, recv_sems, local_sem = rest
        _allgather(x_ref, out_ref, send_sems, recv_sems, local_sem)
        if with_sum:
            acc = out_ref[0:m_per, :]
            for d in range(1, N_DEV):
                acc = acc + out_ref[d * m_per:(d + 1) * m_per, :]
            sum_ref[...] = acc

    space = pltpu.VMEM if in_vmem else pl.ANY
    out_shape = [SDS((N_DEV * m_per, n), blk.dtype)]
    out_specs = [BS(memory_space=space)]
    if with_sum:
        out_shape.append(SDS((m_per, n), blk.dtype))
        out_specs.append(BS(memory_space=pltpu.VMEM))
    return pl.pallas_call(
        body, name=name, out_shape=out_shape, in_specs=[BS(memory_space=space)], out_specs=out_specs,
        scratch_shapes=[pltpu.SemaphoreType.DMA((7,)), pltpu.SemaphoreType.DMA((7,)), pltpu.SemaphoreType.DMA],
        compiler_params=pltpu.CompilerParams(vmem_limit_bytes=VMEM_LIMIT),
    )(blk)


def _gather_sems(n_ranges):
    return [pltpu.SemaphoreType.DMA((6 * n_ranges,)), pltpu.SemaphoreType.DMA((6 * n_ranges,))]


def _pack_gather(pack_ref, wg_ref, send_sems, recv_sems, ranges):
    x, y, c = _mesh_pos()
    me, sibling = (x, y, c), (x, y, 1 - c)
    chips = [(1 - x, y), (x, 1 - y), (1 - x, 1 - y)]

    def land(r, px, py, pc):
        off, n = ranges[r]
        return wg_ref.at[2 * px + py, pl.ds(off + pc * (n // 2), n // 2), :]

    def mine(r):
        off, n = ranges[r]
        return pack_ref.at[pl.ds(off + c * (n // 2), n // 2), :]

    def copy(r, k, block, to, src=None):
        return pltpu.make_async_remote_copy(
            src_ref=land(r, *block) if src is None else src, dst_ref=land(r, *block),
            send_sem=send_sems.at[6 * r + k], recv_sem=recv_sems.at[6 * r + k], device_id=to, device_id_type=MESH)

    def start():
        for r in range(len(ranges)):
            for j, chip in enumerate(chips):
                copy(r, j, me, (*chip, c), src=mine(r)).start()

    def arrive(j):
        for r in range(len(ranges)):
            copy(r, j, (*chips[j], c), me).wait_recv()
            copy(r, 3 + j, (*chips[j], c), sibling).start()
        for r in range(len(ranges)):
            copy(r, 3 + j, (*chips[j], 1 - c), me).wait_recv()

    def drain():
        for r in range(len(ranges)):
            for j, chip in enumerate(chips):
                copy(r, j, me, (*chip, c), src=mine(r)).wait_send()
                copy(r, 3 + j, (*chip, c), sibling).wait_send()

    def finish():
        for r in range(len(ranges)):
            for j, chip in enumerate(chips):
                copy(r, j, (*chip, c), me).wait_recv()
                copy(r, 3 + j, (*chip, c), sibling).start()
        for r in range(len(ranges)):
            for j, chip in enumerate(chips):
                copy(r, 3 + j, (*chip, 1 - c), me).wait_recv()
        drain()

    return start, finish, arrive, drain


def _relay_sems():
    return [pltpu.SemaphoreType.DMA((8,)), pltpu.SemaphoreType.DMA((8,))]


def _relay_gather(pack_ref, wg_ref, send_sems, recv_sems, off, n):
    x, y, c = _mesh_pos()
    me, sibling = (x, y, c), (x, y, 1 - c)
    chips = [(1 - x, y), (x, 1 - y), (1 - x, 1 - y)]
    h, q = n // 2, n // 4

    def land(px, py, pc, piece=None):
        if piece is None:
            return wg_ref.at[2 * px + py, pl.ds(off + pc * h, h), :]
        return wg_ref.at[2 * px + py, pl.ds(off + pc * h + piece * q, q), :]

    def copy(k, ref, to, src=None):
        return pltpu.make_async_remote_copy(
            src_ref=ref if src is None else src, dst_ref=ref, send_sem=send_sems.at[k], recv_sem=recv_sems.at[k],
            device_id=to, device_id_type=MESH)

    def direct(j):
        return copy(j, land(x, y, c), (*chips[j], c), src=pack_ref.at[pl.ds(off + c * h, h), :])

    def relayed(j):
        if j == 0:
            return copy(6, land(*chips[0], c, 1), (x, 1 - y, c))
        return copy(7, land(*chips[1], c, 0), (1 - x, y, c))

    def start():
        direct(0).start()
        direct(1).start()

    def arrive(j):
        if j == 0:
            for k in range(2):
                copy(k, land(*chips[k], c), me).wait_recv()
                relayed(k).start()
                copy(3 + k, land(*chips[k], c), sibling).start()
        if j == 2:
            copy(7, land(*chips[2], c, 0), me).wait_recv()
            copy(6, land(*chips[2], c, 1), me).wait_recv()
            copy(5, land(*chips[2], c), sibling).start()
        copy(3 + j, land(*chips[j], 1 - c), me).wait_recv()

    def drain():
        for j in range(2):
            direct(j).wait_send()
            relayed(j).wait_send()
        for j in range(3):
            copy(3 + j, land(*chips[j], c), sibling).wait_send()

    return start, arrive, drain


def _prologue_call(dw_blk, c_blk, w_ada, b_ada):
    wa = w_ada.shape[1]

    def body(dw_ref, c_ref, wa_ref, ba_ref, dwg_ref, ca_ref, modg_ref,
             cg_scr, part_scr, s1, r1, l1, s2, r2, l2, s3, r3, l3):
        start_c, finish_c = _allgather_parts(c_ref, cg_scr, s2, r2, l2)
        start_dw, finish_dw = _allgather_parts(dw_ref, dwg_ref, s1, r1, l1)
        start_mod, finish_mod = _allgather_parts(part_scr, modg_ref, s3, r3, l3)
        start_c()
        start_dw()
        finish_c()
        cv = cg_scr[...]
        ca = cv * _sig(cv)
        ca_ref[...] = ca
        part_scr[...] = _mm(ca.astype(BF16), wa_ref[...]) + ba_ref[...]
        start_mod()
        finish_dw()
        finish_mod()

    vm = BS(memory_space=pltpu.VMEM)
    return pl.pallas_call(
        body, name="prologue_adaln_conv_dw",
        out_shape=(SDS((N_DEV * 8, D), F32), SDS((N_DEV * 8, D), F32), SDS((N_DEV * N_DEV * 8, wa), F32)),
        in_specs=[vm, vm, vm, vm], out_specs=(vm, vm, vm),
        scratch_shapes=[pltpu.VMEM((N_DEV * 8, D), F32), pltpu.VMEM((N_DEV * 8, wa), F32)]
        + _allgather_sems() + _allgather_sems() + _allgather_sems(),
        compiler_params=pltpu.CompilerParams(vmem_limit_bytes=VMEM_LIMIT),
    )(dw_blk, c_blk, w_ada, b_ada)


def _halves_exchange(g_ref, out_ref, send_sems, recv_sems):
    x, y, c = _mesh_pos()

    def copies():
        return [pltpu.make_async_remote_copy(
            src_ref=g_ref.at[k, 1 - c], dst_ref=out_ref.at[k], send_sem=send_sems.at[k], recv_sem=recv_sems.at[k],
            device_id=(x, y, 1 - c), device_id_type=MESH) for k in range(N_CHIPS)]

    def start():
        for cp in copies():
            cp.start()

    def finish():
        for cp in copies():
            cp.wait()

    return start, finish


def _halves_sems():
    return [pltpu.SemaphoreType.DMA((N_CHIPS,)), pltpu.SemaphoreType.DMA((N_CHIPS,))]


def _sibling_halves_call(g, tag):
    _, _, h, n = g.shape

    def body(g_ref, out_ref, send_sems, recv_sems):
        start, finish = _halves_exchange(g_ref, out_ref, send_sems, recv_sems)
        start()
        finish()

    return pl.pallas_call(
        body, name="rs_sibling_halves_" + tag, out_shape=SDS((N_CHIPS, h, n), g.dtype),
        in_specs=[BS(memory_space=pl.ANY)], out_specs=BS(memory_space=pl.ANY),
        scratch_shapes=_halves_sems(),
    )(g)


def _chip_exchange(p_ref, out_ref, send_sems, recv_sems):
    x, y, c = _mesh_pos()
    chips = [(1 - x, y), (x, 1 - y), (1 - x, 1 - y)]

    def copies():
        return [pltpu.make_async_remote_copy(
            src_ref=p_ref.at[2 * cx + cy], dst_ref=out_ref.at[j], send_sem=send_sems.at[j], recv_sem=recv_sems.at[j],
            device_id=(cx, cy, c), device_id_type=MESH) for j, (cx, cy) in enumerate(chips)]

    def start():
        for cp in copies():
            cp.start()

    def finish():
        for cp in copies():
            cp.wait()

    return start, finish


def _exchange_sems():
    return [pltpu.SemaphoreType.DMA((3,)), pltpu.SemaphoreType.DMA((3,))]


def _join_exchange(in_ref, out_ref, send_sems, recv_sems):
    h = in_ref.shape[1]
    q = h // 4
    x, y, c = _mesh_pos()

    def copy(k, half):
        return pltpu.make_async_remote_copy(
            src_ref=in_ref.at[half, pl.ds(k * q, q)], dst_ref=out_ref.at[half, pl.ds(k * q, q)],
            send_sem=send_sems.at[k], recv_sem=recv_sems.at[k],
            device_id=(x, y, 1 - c), device_id_type=MESH)

    def start():
        for k in range(4):
            copy(k, c).start()

    def finish():
        for k in range(4):
            copy(k, c).wait_send()
            copy(k, 1 - c).wait_recv()

    return start, finish


def _join_sems():
    return [pltpu.SemaphoreType.DMA((4,)), pltpu.SemaphoreType.DMA((4,))]


def _sibling_join_call(full, tag):
    def body(in_ref, out_ref, send_sems, recv_sems):
        start, finish = _join_exchange(in_ref, out_ref, send_sems, recv_sems)
        start()
        finish()

    return pl.pallas_call(
        body, name="rs_sibling_join_" + tag, out_shape=SDS(full.shape, full.dtype),
        in_specs=[BS(memory_space=pl.ANY)], out_specs=BS(memory_space=pl.ANY),
        scratch_shapes=_join_sems(), input_output_aliases={0: 0},
    )(full)


def _add_halves_call(g, recv, c_idx, tag):
    _, _, h, n = g.shape
    tr = h // 2

    def body(c_ref, g_ref, r_ref, o_ref):
        o_ref[...] = (g_ref[...].astype(F32) + r_ref[...].astype(F32)).astype(BF16)

    return pl.pallas_call(
        body, name="rs_add_halves_" + tag, out_shape=SDS((N_CHIPS, h, n), BF16),
        grid_spec=pltpu.PrefetchScalarGridSpec(
            num_scalar_prefetch=1, grid=(N_CHIPS, 2),
            in_specs=[BS((None, None, tr, n), lambda k, r, c_ref: (k, c_ref[0], r, 0)),
                      BS((None, tr, n), lambda k, r, c_ref: (k, r, 0))],
            out_specs=BS((None, tr, n), lambda k, r, c_ref: (k, r, 0))),
        compiler_params=_params("arbitrary", "arbitrary"),
    )(c_idx, g, recv)


def _add_chips_call(p, recv, chip_c_idx, tag):
    _, h, n = p.shape
    tr = h // 2

    def body(k_ref, p_ref, r_ref, o_ref):
        acc = p_ref[...].astype(F32)
        for j in range(3):
            acc = acc + r_ref[j].astype(F32)
        o_ref[...] = acc

    return pl.pallas_call(
        body, name="rs_add_chips_" + tag, out_shape=SDS((2, h, n), F32),
        grid_spec=pltpu.PrefetchScalarGridSpec(
            num_scalar_prefetch=1, grid=(2,),
            in_specs=[BS((None, tr, n), lambda r, k_ref: (k_ref[0], r, 0)),
                      BS((3, tr, n), lambda r, k_ref: (0, r, 0))],
            out_specs=BS((None, tr, n), lambda r, k_ref: (k_ref[1], r, 0))),
        compiler_params=_params("arbitrary"),
    )(chip_c_idx, p, recv)


def _load_rows(wg_hbm, w_vmem, sem, off):
    cp = pltpu.make_async_copy(wg_hbm.at[:, pl.ds(off, w_vmem.shape[1]), :], w_vmem, sem)
    cp.start()
    return cp


def _fwd_in_call(x, mod, pre_tm, wg, b_in, pack, order):
    S = x.shape[0]
    tmf = 2 * TM
    nt = S // tmf
    wc = IN_COLS // N_CHIPS

    def body(ord_ref, x_ref, mod_ref, g_ref, w_hbm, b_ref, pack_ref, p_ref, h_hbm, wg_out, w_vmem, h_scr, sems,
             send_sems, recv_sems, send_sems2, recv_sems2):
        q, i = pl.program_id(0), pl.program_id(1)
        rows = pl.ds(pl.multiple_of(i * tmf, tmf), tmf)
        start, arrive, drain = _relay_gather(pack_ref, wg_out, send_sems, recv_sems, O_IN, R_IN)
        start2, finish2, _, _ = _pack_gather(pack_ref, wg_out, send_sems2, recv_sems2, [(O_OUT, R_BR)])

        def weights(phase):
            return pltpu.make_async_copy(wg_out.at[ord_ref[phase], pl.ds(O_IN, R_IN), :], w_vmem.at[phase % 2],
                                         sems.at[phase % 2])

        @pl.when((q == 0) & (i == 0))
        def _():
            start()
            weights(0).start()
            weights(0).wait()

        @pl.when((q == 1) & (i == 0))
        def _():
            arrive(0)
            start2()
            weights(1).start()
            weights(1).wait()
            arrive(1)
            weights(2).start()

        @pl.when((q == 2) & (i == 0))
        def _():
            weights(2).wait()
            arrive(2)
            weights(3).start()

        @pl.when((q == 3) & (i == 0))
        def _():
            weights(3).wait()

        @pl.when(q == 0)
        def _():
            xv = x_ref[...]
            r = lax.rsqrt(jnp.mean(xv * xv, axis=-1, keepdims=True) + EPS)
            h = xv * r * g_ref[...] * (1.0 + mod_ref[:, D:2 * D]) + mod_ref[:, 0:D]
            h_scr[rows, :] = h.astype(BF16)

        hb = h_scr[rows, :]
        slot = q % 2
        for k in range(wc // D):
            p_ref[:, k * D:(k + 1) * D] = _mm(hb, w_vmem[slot, k * D:(k + 1) * D, :]) + b_ref[:, k * D:(k + 1) * D]

        @pl.when((q == N_CHIPS - 1) & (i == nt - 1))
        def _():
            cp = pltpu.make_async_copy(h_scr, h_hbm, sems.at[0])
            cp.start()
            drain()
            finish2()
            cp.wait()

    hbm = BS(memory_space=pl.ANY)
    return pl.pallas_call(
        body, name="fwd_in", out_shape=(SDS((S, IN_COLS), F32), SDS((S, D), BF16), SDS(wg.shape, wg.dtype)),
        grid_spec=pltpu.PrefetchScalarGridSpec(
            num_scalar_prefetch=1, grid=(N_CHIPS, nt),
            in_specs=[BS((tmf, D), lambda q, i, o: (jnp.where(q == 0, i, nt - 1), 0)),
                      BS((1, 6 * D), lambda q, i, o: (0, 0)),
                      BS((1, D), lambda q, i, o: (0, 0)), hbm, BS((1, wc), lambda q, i, o: (0, o[q])), hbm],
            out_specs=(BS((tmf, wc), lambda q, i, o: (i, o[q])), hbm, hbm),
            scratch_shapes=[pltpu.VMEM((2, R_IN, D), BF16), pltpu.VMEM((S, D), BF16), pltpu.SemaphoreType.DMA((2,))]
            + _relay_sems() + _gather_sems(1)),
        input_output_aliases={4: 2},
        compiler_params=_params("arbitrary", "arbitrary"),
    )(order, x, mod, pre_tm, wg, b_in, pack)


def _lower_bound(lg_ref):
    l0, l1 = lg_ref[0:1, :], lg_ref[1:2, :]
    mx = jnp.maximum(l0, l1)
    e0, e1 = jnp.exp(l0 - mx), jnp.exp(l1 - mx)
    return e0 / (e0 + e1)


def _tri_masks():
    ri = lax.broadcasted_iota(jnp.int32, (CHUNK, CHUNK), 0)
    ci = lax.broadcasted_iota(jnp.int32, (CHUNK, CHUNK), 1)
    return (ri >= ci).astype(F32), (ci >= ri).astype(F32)


def _cumsum_mm(tri, g):
    tb = tri.astype(BF16)
    hi = g.astype(BF16)
    r1 = g - hi.astype(F32)
    mid = r1.astype(BF16)
    lo = (r1 - mid.astype(F32)).astype(BF16)
    return _mm(tb, hi) + _mm(tb, mid) + _mm(tb, lo)


def _hg_gates(q_r, f_r, lb, tril):
    sq = _sig(q_r)
    q = q_r * sq
    sf = _sig(f_r)
    f = lb + (1.0 - lb) * sf
    k = 1.0 - f
    g = jnp.log(f)
    b = _cumsum_mm(tril, g)
    b_last = _rowsum(g)
    row = lax.broadcasted_iota(jnp.int32, g.shape, 0)
    ref = _rowsum(jnp.where(row < CHUNK // 2, g, 0.0))
    e = jnp.exp(b)
    eq = jnp.exp(jnp.minimum(b - ref, 80.0))
    ek = jnp.exp(jnp.minimum(ref - b, 80.0))
    dd = jnp.exp(b_last - b)
    return dict(sq=sq, q=q, sf=sf, f=f, k=k, e=e, eq=eq, ek=ek, dd=dd, elast=jnp.exp(b_last),
                qe=q * e, qt=q * eq, kt=k * ek, kd=k * dd)


def _hgrn_fwd_call(p, logits, gn, wg, pack):
    S = p.shape[0]
    ncb = TB // CHUNK
    ranges = [(O_FF1, R_FF)]

    def body(q_ref, f_ref, v_ref, og_ref, lg_ref, gn_ref, wg_in, pack_ref, o_ref, oa_ref, st_ref, wg_out,
             st_scr, send_sems, recv_sems):
        start, finish, _, _ = _pack_gather(pack_ref, wg_out, send_sems, recv_sems, ranges)

        @pl.when(pl.program_id(0) == 0)
        def _():
            start()
            st_scr[...] = jnp.zeros_like(st_scr)

        lb = _lower_bound(lg_ref)
        tril, _ = _tri_masks()

        def chunk(ci, carry):
            rows = pl.ds(pl.multiple_of(ci * CHUNK, CHUNK), CHUNK)
            st_ref[ci] = st_scr[...]
            t = _hg_gates(q_ref[rows, :], f_ref[rows, :], lb, tril)
            v = v_ref[rows, :]
            for h in range(HEADS):
                sl = slice(h * DK, (h + 1) * DK)
                stp = st_scr[:, sl]
                vb = v[:, sl].astype(BF16)
                inter = _mm(t["qe"][:, sl].astype(BF16), stp.astype(BF16), NT)
                a = jnp.where(tril > 0.5, _mm(t["qt"][:, sl].astype(BF16), t["kt"][:, sl].astype(BF16), NT), 0.0)
                o = inter + _mm(a.astype(BF16), vb)
                st_scr[:, sl] = stp * t["elast"][:, sl] + _mm(vb, t["kd"][:, sl].astype(BF16), TN)
                oh = o * lax.rsqrt(jnp.mean(o * o, axis=-1, keepdims=True) + EPS)
                og = og_ref[rows, sl]
                o_ref[rows, sl] = o
                oa_ref[rows, sl] = (oh * gn_ref[:, sl] * (og * _sig(og))).astype(BF16)
            return carry

        lax.fori_loop(0, ncb, chunk, 0)

        @pl.when(pl.program_id(0) == S // TB - 1)
        def _():
            finish()

    col = lambda j: BS((TB, D), lambda i, j=j: (i, j))
    hbm = BS(memory_space=pl.ANY)
    return pl.pallas_call(
        body, name="hgrn_fwd", grid=(S // TB,),
        out_shape=(SDS((S, D), F32), SDS((S, D), BF16), SDS((S // CHUNK, DK, D), F32), SDS(wg.shape, wg.dtype)),
        in_specs=[col(0), col(1), col(2), col(3), BS((2, D), lambda i: (0, 0)), BS((1, D), lambda i: (0, 0)),
                  hbm, hbm],
        out_specs=(BS((TB, D), lambda i: (i, 0)), BS((TB, D), lambda i: (i, 0)),
                   BS((ncb, DK, D), lambda i: (i, 0, 0)), hbm),
        scratch_shapes=[pltpu.VMEM((DK, D), F32)] + _gather_sems(len(ranges)),
        input_output_aliases={6: 3},
        compiler_params=_params("arbitrary"),
    )(p, p, p, p, logits, gn, wg, pack)


def _layernorm_stats(uc):
    mu = jnp.mean(uc, axis=-1, keepdims=True)
    xc = uc - mu
    rs = lax.rsqrt(jnp.mean(xc * xc, axis=-1, keepdims=True) + EPS)
    return xc * rs, rs


EXT = HALO + TM + 8


def _fill_shifted(ext, shifted):
    for m in range(1, 8):
        shifted[m - 1] = ext[m:m + HALO + TM, :]


def _window(ext, shifted, s0, n):
    m = s0 % 8
    q = s0 - m
    return ext[q:q + n, :] if m == 0 else shifted[m - 1, q:q + n, :]


def _conv_fwd_call(p, dw, db, ln_g, ln_b, wg, pack):
    S = p.shape[0]
    ranges = [(O_FF2, R_FF), (O_BRA, 2 * R_BR)]

    def body(cv_ref, cg_ref, dw_ref, db_ref, g_ref, b_ref, wg_in, pack_ref, u_ref, uc_ref, cb_ref, wg_out,
             uext, ush, send_sems, recv_sems):
        start, finish, _, _ = _pack_gather(pack_ref, wg_out, send_sems, recv_sems, ranges)

        @pl.when(pl.program_id(0) == 0)
        def _():
            start()
            uext[0:HALO, :] = jnp.zeros((HALO, D), F32)
            uext[HALO + TM:EXT, :] = jnp.zeros((EXT - HALO - TM, D), F32)

        u = cv_ref[...] * _sig(cg_ref[...])
        uext[HALO:HALO + TM, :] = u
        u_ref[...] = u
        _fill_shifted(uext, ush)
        for rb in range(TM // SUB):
            acc = jnp.broadcast_to(db_ref[...], (SUB, D))
            for j in range(CONV_K):
                s0 = HALO - (CONV_K - 1) + j + rb * SUB
                acc = acc + dw_ref[j:j + 1, :] * _window(uext, ush, s0, SUB)
            uc_ref[rb * SUB:(rb + 1) * SUB, :] = acc
            xh, _ = _layernorm_stats(acc)
            ln = xh * g_ref[...] + b_ref[...]
            cb_ref[rb * SUB:(rb + 1) * SUB, :] = (ln * _sig(ln)).astype(BF16)
        uext[0:HALO, :] = uext[TM:TM + HALO, :]

        @pl.when(pl.program_id(0) == S // TM - 1)
        def _():
            finish()

    vec = BS((1, D), lambda i: (0, 0))
    hbm = BS(memory_space=pl.ANY)
    return pl.pallas_call(
        body, name="conv_fwd", grid=(S // TM,),
        out_shape=(SDS((S, D), F32), SDS((S, D), F32), SDS((S, D), BF16), SDS(wg.shape, wg.dtype)),
        in_specs=[BS((TM, D), lambda i: (i, 4)), BS((TM, D), lambda i: (i, 5)),
                  BS((CONV_K, D), lambda i: (0, 0)), vec, vec, vec, hbm, hbm],
        out_specs=(BS((TM, D), lambda i: (i, 0)),) * 3 + (hbm,),
        scratch_shapes=[pltpu.VMEM((EXT, D), F32), pltpu.VMEM((7, HALO + TM, D), F32)] + _gather_sems(len(ranges)),
        input_output_aliases={6: 3},
        compiler_params=_params("arbitrary"),
    )(p, p, dw, db, ln_g, ln_b, wg, pack)


def _mm_rows(a, w_ref):
    acc = _mm(a[:, 0:R_BR], w_ref[0])
    for k in range(1, N_CHIPS):
        acc = acc + _mm(a[:, k * R_BR:(k + 1) * R_BR], w_ref[k])
    return acc


def _mm_rows_t(a, w_ref):
    return jnp.concatenate([_mm(a, w_ref[k], NT) for k in range(N_CHIPS)], axis=1)


def _br_spec(off):
    return BS((N_CHIPS, R_BR, D), lambda i: (0, off // R_BR, 0))


def _merge_fwd_call(oa, cb, p, x, mod, post_tm, pre_cm, wg):
    S = x.shape[0]

    def body(oa_ref, cb_ref, ga_ref, gb_ref, x_ref, mod_ref, post_ref, pre_ref, wa_ref, wb_ref, wo_ref,
             ya_ref, yb_ref, mg_ref, y_ref, x2_ref, h2_ref):
        ya = _mm_rows(oa_ref[...], wa_ref)
        yb = _mm_rows(cb_ref[...], wb_ref)
        ya_ref[...] = ya.astype(BF16)
        yb_ref[...] = yb.astype(BF16)
        mg = (_sig(ga_ref[...]) * ya + _sig(gb_ref[...]) * yb).astype(BF16)
        mg_ref[...] = mg
        y = _mm_rows(mg, wo_ref)
        y_ref[...] = y
        n = y * lax.rsqrt(jnp.mean(y * y, axis=-1, keepdims=True) + EPS) * post_ref[...]
        x2 = x_ref[...] + mod_ref[:, 2 * D:3 * D] * n
        x2_ref[...] = x2
        r2 = lax.rsqrt(jnp.mean(x2 * x2, axis=-1, keepdims=True) + EPS)
        h2 = x2 * r2 * pre_ref[...] * (1.0 + mod_ref[:, 4 * D:5 * D]) + mod_ref[:, 3 * D:4 * D]
        h2_ref[...] = h2.astype(BF16)

    tile = BS((TM, D), lambda i: (i, 0))
    vec = BS((1, D), lambda i: (0, 0))
    return pl.pallas_call(
        body, name="merge_fwd", grid=(S // TM,),
        out_shape=(SDS((S, D), BF16), SDS((S, D), BF16), SDS((S, D), BF16), SDS((S, D), F32), SDS((S, D), F32),
                   SDS((S, D), BF16)),
        in_specs=[tile, tile, BS((TM, D), lambda i: (i, 6)), BS((TM, D), lambda i: (i, 7)), tile,
                  BS((1, 6 * D), lambda i: (0, 0)), vec, vec, _br_spec(O_BRA), _br_spec(O_BRB), _br_spec(O_OUT)],
        out_specs=(tile,) * 6,
        compiler_params=_params("arbitrary"),
    )(oa, cb, p, p, x, mod, post_tm, pre_cm, wg, wg, wg)


def _ffn_call(h2, x2, target, mod, post_cm, pre_cm, wg):
    S = x2.shape[0]

    def body(h2_ref, x2_ref, t_ref, mod_ref, post_ref, pre_ref, w_hbm,
             z_ref, da_ref, dy2_ref, dx2_ref, acc_ref, w1_v, w2_v, ra_scr, sems):
        @pl.when(pl.program_id(0) == 0)
        def _():
            c1 = _load_rows(w_hbm, w1_v, sems.at[0], O_FF1)
            c2 = _load_rows(w_hbm, w2_v, sems.at[1], O_FF2)
            c1.wait()
            c2.wait()
            acc_ref[...] = jnp.zeros_like(acc_ref)

        h2 = h2_ref[...]
        for k in range(N_CHIPS):
            ra = jnp.maximum(_mm(h2, w1_v[k]), 0.0)
            ra_scr[:, k * D:(k + 1) * D] = ra
            z_ref[:, k * D:(k + 1) * D] = (ra * ra).astype(BF16)
        y2 = _mm(z_ref[:, 0:D], w2_v[0])
        for k in range(1, N_CHIPS):
            y2 = y2 + _mm(z_ref[:, k * D:(k + 1) * D], w2_v[k])
        ry = lax.rsqrt(jnp.mean(y2 * y2, axis=-1, keepdims=True) + EPS)
        yn = y2 * ry
        n = yn * post_ref[...]
        g2 = mod_ref[:, 5 * D:6 * D]
        x2 = x2_ref[...]
        err = x2 + g2 * n - t_ref[...]
        acc_ref[5:6, :] += _rowsum(err * err) * (0.5 / D)
        dout = err * (1.0 / D)
        acc_ref[0:1, :] += _rowsum(dout * n)
        dn = dout * g2
        acc_ref[1:2, :] += _rowsum(dn * yn)
        dyn = dn * post_ref[...]
        dy2 = (ry * (dyn - yn * jnp.mean(dyn * yn, axis=-1, keepdims=True))).astype(BF16)
        dy2_ref[...] = dy2
        for k in range(N_CHIPS):
            dz = _mm(dy2, w2_v[k], NT)
            da_ref[:, k * D:(k + 1) * D] = (dz * (2.0 * ra_scr[:, k * D:(k + 1) * D])).astype(BF16)
        dh2 = jnp.zeros((TM, D), F32)
        for k in range(N_CHIPS):
            dh2 = dh2 + _mm(da_ref[:, k * D:(k + 1) * D], w1_v[k], NT)
        r2 = lax.rsqrt(jnp.mean(x2 * x2, axis=-1, keepdims=True) + EPS)
        xn = x2 * r2
        yv = xn * pre_ref[...]
        acc_ref[2:3, :] += _rowsum(dh2)
        acc_ref[3:4, :] += _rowsum(dh2 * yv)
        dyv = dh2 * (1.0 + mod_ref[:, 4 * D:5 * D])
        acc_ref[4:5, :] += _rowsum(dyv * xn)
        dxn = dyv * pre_ref[...]
        dx2_ref[...] = dout + r2 * (dxn - xn * jnp.mean(dxn * xn, axis=-1, keepdims=True))

    tile = BS((TM, D), lambda i: (i, 0))
    wide = BS((TM, D_FF), lambda i: (i, 0))
    vec = BS((1, D), lambda i: (0, 0))
    return pl.pallas_call(
        body, name="ffn_fwd_bwd", grid=(S // TM,),
        out_shape=(SDS((S, D_FF), BF16), SDS((S, D_FF), BF16), SDS((S, D), BF16), SDS((S, D), F32),
                   SDS((8, D), F32)),
        in_specs=[tile, tile, tile, BS((1, 6 * D), lambda i: (0, 0)), vec, vec, BS(memory_space=pl.ANY)],
        out_specs=(wide, wide, tile, tile, BS((8, D), lambda i: (0, 0))),
        scratch_shapes=[pltpu.VMEM((N_CHIPS, R_FF, D), BF16), pltpu.VMEM((N_CHIPS, R_FF, D), BF16),
                        pltpu.VMEM((TM, D_FF), F32),
                        pltpu.SemaphoreType.DMA((2,))],
        compiler_params=_params("arbitrary"),
    )(h2, x2, target, mod, post_cm, pre_cm, wg)


def _merge_bwd_call(dx2, y, ya, yb, p, mod, post_tm, wg, g):
    S = y.shape[0]

    def body(dx2_ref, y_ref, ya_ref, yb_ref, ga_ref, gb_ref, mod_ref, post_ref, wa_ref, wb_ref, wo_ref, g_ref,
             dy_ref, dya_ref, dyb_ref, doa_ref, dcb_ref, dpg_ref, acc_ref, bsum_ref, hr_ref, send_sems, recv_sems):
        start, finish = _halves_exchange(g_ref, hr_ref, send_sems, recv_sems)

        @pl.when(pl.program_id(0) == 0)
        def _():
            start()
            acc_ref[...] = jnp.zeros_like(acc_ref)
            bsum_ref[...] = jnp.zeros_like(bsum_ref)

        y = y_ref[...]
        ry = lax.rsqrt(jnp.mean(y * y, axis=-1, keepdims=True) + EPS)
        yn = y * ry
        dx2 = dx2_ref[...]
        acc_ref[0:1, :] += _rowsum(dx2 * (yn * post_ref[...]))
        dn = dx2 * mod_ref[:, 2 * D:3 * D]
        acc_ref[1:2, :] += _rowsum(dn * yn)
        dyn = dn * post_ref[...]
        dy = (ry * (dyn - yn * jnp.mean(dyn * yn, axis=-1, keepdims=True))).astype(BF16)
        dy_ref[...] = dy
        dmg = _mm_rows_t(dy, wo_ref)
        sa, sb = _sig(ga_ref[...]), _sig(gb_ref[...])
        dya = (dmg * sa).astype(BF16)
        dyb = (dmg * sb).astype(BF16)
        dya_ref[...] = dya
        dyb_ref[...] = dyb
        dga = dmg * ya_ref[...].astype(F32) * (sa * (1.0 - sa))
        dgb = dmg * yb_ref[...].astype(F32) * (sb * (1.0 - sb))
        dpg_ref[:, 0:D] = dga.astype(BF16)
        dpg_ref[:, D:2 * D] = dgb.astype(BF16)
        bsum_ref[:, 0:D] += _rowsum(dga)
        bsum_ref[:, D:2 * D] += _rowsum(dgb)
        doa_ref[...] = _mm_rows_t(dya, wa_ref)
        dcb_ref[...] = _mm_rows_t(dyb, wb_ref)

        @pl.when(pl.program_id(0) == S // TM - 1)
        def _():
            finish()

    tile = BS((TM, D), lambda i: (i, 0))
    vec = BS((1, D), lambda i: (0, 0))
    return pl.pallas_call(
        body, name="merge_bwd", grid=(S // TM,),
        out_shape=(SDS((S, D), BF16), SDS((S, D), BF16), SDS((S, D), BF16), SDS((S, D), F32), SDS((S, D), F32),
                   SDS((S, 2 * D), BF16), SDS((8, D), F32), SDS((1, 2 * D), F32),
                   SDS((N_CHIPS,) + g.shape[2:], g.dtype)),
        in_specs=[tile, tile, tile, tile, BS((TM, D), lambda i: (i, 6)), BS((TM, D), lambda i: (i, 7)),
                  BS((1, 6 * D), lambda i: (0, 0)), vec, _br_spec(O_BRA), _br_spec(O_BRB), _br_spec(O_OUT),
                  BS(memory_space=pl.ANY)],
        out_specs=(tile, tile, tile, tile, tile, BS((TM, 2 * D), lambda i: (i, 0)),
                   BS((8, D), lambda i: (0, 0)), BS((1, 2 * D), lambda i: (0, 0)), BS(memory_space=pl.ANY)),
        scratch_shapes=_halves_sems(),
        compiler_params=_params("arbitrary"),
    )(dx2, y, ya, yb, p, p, mod, post_tm, wg, wg, wg, g)


def _hgrn_bwd_call(p, o, doa, st, logits, gn, part, g):
    S = p.shape[0]
    nb = S // TB
    ncb = TB // CHUNK

    def body(q_ref, f_ref, v_ref, og_ref, o_ref, doa_ref, st_ref, lg_ref, gn_ref, part_ref, g_ref,
             dp_ref, bsum_ref, dlg_ref, dgn_ref, recv_ref, hr_ref,
             dst_scr, dlb_scr, dqe_s, dqt_s, dkt_s, dkd_s, dv_s, dog_s, dble_s, send_sems, recv_sems, hs, hr):
        i = pl.program_id(0)
        start, finish = _chip_exchange(part_ref, recv_ref, send_sems, recv_sems)
        start_h, finish_h = _halves_exchange(g_ref, hr_ref, hs, hr)

        @pl.when(i == 0)
        def _():
            start_h()
            start()
            dst_scr[...] = jnp.zeros_like(dst_scr)
            dlb_scr[...] = jnp.zeros_like(dlb_scr)
            bsum_ref[...] = jnp.zeros_like(bsum_ref)
            dgn_ref[...] = jnp.zeros_like(dgn_ref)

        lb = _lower_bound(lg_ref)
        tril, triu = _tri_masks()

        def chunk(tt, carry):
            ci = ncb - 1 - tt
            rows = pl.ds(pl.multiple_of(ci * CHUNK, CHUNK), CHUNK)
            q_r, f_r = q_ref[rows, :], f_ref[rows, :]
            t = _hg_gates(q_r, f_r, lb, tril)
            v = v_ref[rows, :]
            for h in range(HEADS):
                sl = slice(h * DK, (h + 1) * DK)
                stp = st_ref[ci, :, sl]
                stb = stp.astype(BF16)
                qeb = t["qe"][:, sl].astype(BF16)
                qtb = t["qt"][:, sl].astype(BF16)
                ktb = t["kt"][:, sl].astype(BF16)
                kdb = t["kd"][:, sl].astype(BF16)
                vb = v[:, sl].astype(BF16)
                a = jnp.where(tril > 0.5, _mm(qtb, ktb, NT), 0.0)
                o_h = o_ref[rows, sl]
                rinv = lax.rsqrt(jnp.mean(o_h * o_h, axis=-1, keepdims=True) + EPS)
                oh = o_h * rinv
                og = og_ref[rows, sl]
                so = _sig(og)
                d_oa = doa_ref[rows, sl]
                don = d_oa * (og * so)
                dog_s[:, sl] = d_oa * (oh * gn_ref[:, sl]) * _dsilu(og, so)
                dgn_ref[:, sl] += _rowsum(don * oh)
                doh = don * gn_ref[:, sl]
                do = (rinv * (doh - oh * jnp.mean(doh * oh, axis=-1, keepdims=True))).astype(BF16)
                dqe_s[:, sl] = _mm(do, stb, NN)
                dstp = _mm(do, qeb, TN)
                dab = jnp.where(tril > 0.5, _mm(do, vb, NT), 0.0).astype(BF16)
                dqt_s[:, sl] = _mm(dab, ktb, NN)
                dkt_s[:, sl] = _mm(dab, qtb, TN)
                dstn = dst_scr[:, sl]
                dsb = dstn.astype(BF16)
                dkd_s[:, sl] = _mm(vb, dsb, NN)
                dv_s[:, sl] = _mm(a.astype(BF16), do, TN) + _mm(kdb, dsb, NT)
                el = t["elast"][:, sl]
                dst_scr[:, sl] = dstn * el + dstp
                dble_s[:, sl] = el * _rowsum(stp * dstn)
            dqe, dqt, dkt, dkd = dqe_s[...], dqt_s[...], dkt_s[...], dkd_s[...]
            dq = dqe * t["e"] + dqt * t["eq"]
            dk = dkt * t["ek"] + dkd * t["dd"]
            dkk = dkd * t["kd"]
            qt_r = t["qt"].astype(BF16).astype(F32)
            kt_r = t["kt"].astype(BF16).astype(F32)
            dbv = dqe * t["qe"] + dqt * qt_r - dkt * kt_r - dkk
            dg = _cumsum_mm(triu, dbv) + (_rowsum(dkk) + dble_s[...])
            df = dg / t["f"] - dk
            sf = t["sf"]
            dlb_scr[...] += _rowsum(df * (1.0 - sf))
            dqr = dq * _dsilu(q_r, t["sq"])
            dfr = df * (1.0 - lb) * (sf * (1.0 - sf))
            dvv, dog = dv_s[...], dog_s[...]
            dp_ref[rows, 0:D] = dqr.astype(BF16)
            dp_ref[rows, D:2 * D] = dfr.astype(BF16)
            dp_ref[rows, 2 * D:3 * D] = dvv.astype(BF16)
            dp_ref[rows, 3 * D:4 * D] = dog.astype(BF16)
            bsum_ref[:, 0:D] += _rowsum(dqr)
            bsum_ref[:, D:2 * D] += _rowsum(dfr)
            bsum_ref[:, 2 * D:3 * D] += _rowsum(dvv)
            bsum_ref[:, 3 * D:4 * D] += _rowsum(dog)
            return carry

        lax.fori_loop(0, ncb, chunk, 0)

        dl = dlb_scr[...] * lb * (1.0 - lb)
        dlg_ref[0:1, :] = dl
        dlg_ref[1:2, :] = -dl

        @pl.when(i == nb - 1)
        def _():
            finish_h()
            finish()

    col = lambda j: BS((TB, D), lambda i, j=j: (nb - 1 - i, j))
    rev = BS((TB, D), lambda i: (nb - 1 - i, 0))
    cd = pltpu.VMEM((CHUNK, D), F32)
    return pl.pallas_call(
        body, name="hgrn_bwd", grid=(nb,),
        out_shape=(SDS((S, 4 * D), BF16), SDS((1, 4 * D), F32), SDS((2, D), F32), SDS((1, D), F32),
                   SDS((3,) + part.shape[1:], part.dtype), SDS((N_CHIPS,) + g.shape[2:], g.dtype)),
        in_specs=[col(0), col(1), col(2), col(3), rev, rev, BS((ncb, DK, D), lambda i: (nb - 1 - i, 0, 0)),
                  BS((2, D), lambda i: (0, 0)), BS((1, D), lambda i: (0, 0)), BS(memory_space=pl.ANY),
                  BS(memory_space=pl.ANY)],
        out_specs=(BS((TB, 4 * D), lambda i: (nb - 1 - i, 0)), BS((1, 4 * D), lambda i: (0, 0)),
                   BS((2, D), lambda i: (0, 0)), BS((1, D), lambda i: (0, 0)), BS(memory_space=pl.ANY),
                   BS(memory_space=pl.ANY)),
        scratch_shapes=[pltpu.VMEM((DK, D), F32), pltpu.VMEM((1, D), F32), cd, cd, cd, cd, cd, cd,
                        pltpu.VMEM((1, D), F32)] + _exchange_sems() + _halves_sems(),
        compiler_params=_params("arbitrary"),
    )(p, p, p, p, o, doa, st, logits, gn, part, g)


def _conv_bwd_call(dcb, uc, u, p, dw, ln_g, ln_b, part):
    S = uc.shape[0]
    nb = S // TM
    hb = TM // HALO

    def body(dcb_ref, uc_ref, u_ref, uh_ref, cv_ref, cg_ref, dw_ref, g_ref, b_ref, part_ref,
             dp_ref, bsum_ref, ddw_ref, acc_ref, recv_ref, uext, dext, ush, dsh, send_sems, recv_sems):
        i = pl.program_id(0)
        start, finish = _chip_exchange(part_ref, recv_ref, send_sems, recv_sems)

        @pl.when(i == 0)
        def _():
            start()
            dext[TM:EXT, :] = jnp.zeros((EXT - TM, D), F32)
            uext[HALO + TM:EXT, :] = jnp.zeros((EXT - HALO - TM, D), F32)
            bsum_ref[...] = jnp.zeros_like(bsum_ref)
            ddw_ref[...] = jnp.zeros_like(ddw_ref)
            acc_ref[...] = jnp.zeros_like(acc_ref)

        first_tile = (nb - 1 - i) == 0
        uext[0:HALO, :] = jnp.where(first_tile, 0.0, uh_ref[...])
        uext[HALO:HALO + TM, :] = u_ref[...]
        _fill_shifted(uext, ush)

        for rb in range(TM // SUB):
            rs_ = slice(rb * SUB, (rb + 1) * SUB)
            xh, rs = _layernorm_stats(uc_ref[rs_, :])
            ln = xh * g_ref[...] + b_ref[...]
            dln = dcb_ref[rs_, :] * _dsilu(ln, _sig(ln))
            acc_ref[1:2, :] += _rowsum(dln * xh)
            acc_ref[2:3, :] += _rowsum(dln)
            dxh = dln * g_ref[...]
            duc = rs * (dxh - jnp.mean(dxh, axis=-1, keepdims=True)
                        - xh * jnp.mean(dxh * xh, axis=-1, keepdims=True))
            dext[rs_, :] = duc
            acc_ref[0:1, :] += _rowsum(duc)
        _fill_shifted(dext, dsh)

        for j in range(CONV_K):
            part = jnp.zeros((SUB, D), F32)
            for rb in range(TM // SUB):
                s0 = HALO - (CONV_K - 1) + j + rb * SUB
                part = part + dext[rb * SUB:(rb + 1) * SUB, :] * _window(uext, ush, s0, SUB)
            ddw_ref[j:j + 1, :] += _rowsum(part)

        for rb in range(TM // SUB):
            rs_ = slice(rb * SUB, (rb + 1) * SUB)
            du = jnp.zeros((SUB, D), F32)
            for j in range(CONV_K):
                s0 = rb * SUB + (CONV_K - 1) - j
                du = du + dw_ref[j:j + 1, :] * _window(dext, dsh, s0, SUB)
            cg = cg_ref[rs_, :]
            sg = _sig(cg)
            dcv = du * sg
            dcg = du * cv_ref[rs_, :] * (sg * (1.0 - sg))
            dp_ref[rs_, 0:D] = dcv.astype(BF16)
            dp_ref[rs_, D:2 * D] = dcg.astype(BF16)
            bsum_ref[:, 0:D] += _rowsum(dcv)
            bsum_ref[:, D:2 * D] += _rowsum(dcg)

        dext[TM:TM + HALO, :] = dext[0:HALO, :]

        @pl.when(i == nb - 1)
        def _():
            finish()

    rev = BS((TM, D), lambda i: (nb - 1 - i, 0))
    vec = BS((1, D), lambda i: (0, 0))
    return pl.pallas_call(
        body, name="conv_bwd", grid=(nb,),
        out_shape=(SDS((S, 2 * D), BF16), SDS((1, 2 * D), F32), SDS((32, D), F32), SDS((8, D), F32),
                   SDS((3,) + part.shape[1:], part.dtype)),
        in_specs=[rev, rev, rev, BS((HALO, D), lambda i: (jnp.maximum((nb - 1 - i) * hb - 1, 0), 0)),
                  BS((TM, D), lambda i: (nb - 1 - i, 4)), BS((TM, D), lambda i: (nb - 1 - i, 5)),
                  BS((CONV_K, D), lambda i: (0, 0)), vec, vec, BS(memory_space=pl.ANY)],
        out_specs=(BS((TM, 2 * D), lambda i: (nb - 1 - i, 0)), BS((1, 2 * D), lambda i: (0, 0)),
                   BS((32, D), lambda i: (0, 0)), BS((8, D), lambda i: (0, 0)), BS(memory_space=pl.ANY)),
        scratch_shapes=[pltpu.VMEM((EXT, D), F32), pltpu.VMEM((EXT, D), F32),
                        pltpu.VMEM((7, HALO + TM, D), F32), pltpu.VMEM((7, HALO + TM, D), F32)] + _exchange_sems(),
        compiler_params=_params("arbitrary"),
    )(dcb, uc, u, u, p, p, dw, ln_g, ln_b, part)


def _in_bwd_call(dp_hg, dp_cv, dp_gt, x, dx2, mod, pre_tm, wg, part, full_a, full_b):
    S = x.shape[0]
    tm = 2 * TM

    def body(hg_ref, cv_ref, gt_ref, x_ref, dx2_ref, mod_ref, g_ref, w_hbm, part_ref, fa_in, fb_in,
             gx_ref, acc_ref, recv_ref, fa_out, fb_out, w_vmem, sem, send_sems, recv_sems, sa, ra, sb, rb):
        start, finish = _chip_exchange(part_ref, recv_ref, send_sems, recv_sems)
        start_a, finish_a = _join_exchange(fa_in, fa_out, sa, ra)
        start_b, finish_b = _join_exchange(fb_in, fb_out, sb, rb)

        @pl.when(pl.program_id(0) == 0)
        def _():
            start_a()
            start_b()
            start()
            _load_rows(w_hbm, w_vmem, sem, O_IN).wait()
            acc_ref[...] = jnp.zeros_like(acc_ref)

        dh = jnp.zeros((tm, D), F32)
        for k in range(IN_COLS // D):
            src, kk = ((hg_ref, k), (cv_ref, k - 4), (gt_ref, k - 6))[0 if k < 4 else (1 if k < 6 else 2)]
            dh = dh + _mm(src[:, kk * D:(kk + 1) * D], w_vmem[k // 2, (k % 2) * D:(k % 2 + 1) * D, :], NT)
        xv = x_ref[...]
        r = lax.rsqrt(jnp.mean(xv * xv, axis=-1, keepdims=True) + EPS)
        xn = xv * r
        yv = xn * g_ref[...]
        acc_ref[0:1, :] += _rowsum(dh)
        acc_ref[1:2, :] += _rowsum(dh * yv)
        dyv = dh * (1.0 + mod_ref[:, D:2 * D])
        acc_ref[2:3, :] += _rowsum(dyv * xn)
        dxn = dyv * g_ref[...]
        gx_ref[...] = dx2_ref[...] + r * (dxn - xn * jnp.mean(dxn * xn, axis=-1, keepdims=True))

        @pl.when(pl.program_id(0) == S // tm - 1)
        def _():
            finish_a()
            finish_b()
            finish()

    tile = BS((tm, D), lambda i: (i, 0))
    hbm = BS(memory_space=pl.ANY)
    return pl.pallas_call(
        body, name="in_bwd", grid=(S // tm,),
        out_shape=(SDS((S, D), F32), SDS((8, D), F32), SDS((3,) + part.shape[1:], part.dtype),
                   SDS(full_a.shape, full_a.dtype), SDS(full_b.shape, full_b.dtype)),
        in_specs=[BS((tm, 4 * D), lambda i: (i, 0)), BS((tm, 2 * D), lambda i: (i, 0)),
                  BS((tm, 2 * D), lambda i: (i, 0)), tile, tile, BS((1, 6 * D), lambda i: (0, 0)),
                  BS((1, D), lambda i: (0, 0)), hbm, hbm, hbm, hbm],
        out_specs=(tile, BS((8, D), lambda i: (0, 0)), hbm, hbm, hbm),
        scratch_shapes=[pltpu.VMEM((N_CHIPS, R_IN, D), BF16), pltpu.SemaphoreType.DMA] + _exchange_sems()
        + _join_sems() + _join_sems(),
        input_output_aliases={9: 3, 10: 4},
        compiler_params=_params("arbitrary"),
    )(dp_hg, dp_cv, dp_gt, x, dx2, mod, pre_tm, wg, part, full_a, full_b)


def _wgrad_call(gp, a, b, name, bm, place, rows):
    S, M = a.shape
    N = b.shape[1]
    bk = min(S, 1024)
    nk = S // bk

    def body(a_ref, b_ref, *rest):
        o_ref, acc = rest[-2], rest[-1]
        k = pl.program_id(2)

        @pl.when(k == 0)
        def _():
            acc[...] = jnp.zeros_like(acc)

        acc[...] += _mm(a_ref[...], b_ref[...], TN)

        @pl.when(k == nk - 1)
        def _():
            o_ref[...] = acc[...].astype(BF16)

    in_specs = [BS((bk, bm), lambda i, j, k: (k, i)), BS((bk, D), lambda i, j, k: (k, j))]
    args = [a, b]
    if gp is not None:
        in_specs.append(BS(memory_space=pl.ANY))
        args.append(gp)
    return pl.pallas_call(
        body, name=name, grid=(M // bm, N // D, nk),
        out_shape=SDS((N_CHIPS, rows, D), BF16),
        in_specs=in_specs,
        out_specs=BS((None, bm, D), lambda i, j, k: (*place(i, j), 0)),
        scratch_shapes=[pltpu.VMEM((bm, D), F32)],
        input_output_aliases={} if gp is None else {2: 0},
        compiler_params=_params("parallel", "parallel", "arbitrary"),
    )(*args)


def _wgrad_rows_call(gp, a, b, name, blk):
    S = a.shape[0]
    bk = min(S, 1024)
    nk = S // bk

    def body(a_ref, b_ref, *rest):
        o_ref, acc = rest[-2], rest[-1]
        k = pl.program_id(0)

        @pl.when(k == 0)
        def _():
            acc[...] = jnp.zeros_like(acc)

        acc[...] += _mm(a_ref[...], b_ref[...], TN)

        @pl.when(k == nk - 1)
        def _():
            for c in range(N_CHIPS):
                o_ref[c] = acc[c * R_BR:(c + 1) * R_BR, :].astype(BF16)

    in_specs = [BS((bk, D), lambda k: (k, 0)), BS((bk, D), lambda k: (k, 0))]
    args = [a, b]
    if gp is not None:
        in_specs.append(BS(memory_space=pl.ANY))
        args.append(gp)
    return pl.pallas_call(
        body, name=name, grid=(nk,),
        out_shape=SDS((N_CHIPS, 3 * R_BR, D), BF16),
        in_specs=in_specs,
        out_specs=BS((N_CHIPS, R_BR, D), lambda k: (0, blk, 0)),
        scratch_shapes=[pltpu.VMEM((D, D), F32)],
        input_output_aliases={} if gp is None else {2: 0},
        compiler_params=_params("arbitrary"),
    )(*args)


def _outer_call(cact, dmod):
    n = dmod.shape[1]

    def body(a_ref, b_ref, o_ref):
        o_ref[...] = _mm(a_ref[...], b_ref[...], TN, HI)

    return pl.pallas_call(
        body, name="wgrad_ada", out_shape=SDS((D, n), F32),
        compiler_params=pltpu.CompilerParams(vmem_limit_bytes=VMEM_LIMIT),
    )(cact, dmod)


def _adamw_call(w, g, m, v, name):
    R, C = w.shape
    tr = R
    while tr * C > 512 * 1024 and tr % 16 == 0:
        tr //= 2
    c1 = 1.0 - ADAM_B1 ** ADAM_STEP
    c2 = 1.0 - ADAM_B2 ** ADAM_STEP

    def body(w_ref, g_ref, m_ref, v_ref, d_ref, m2_ref, v2_ref):
        g = g_ref[...]
        m2 = ADAM_B1 * m_ref[...] + (1.0 - ADAM_B1) * g
        v2 = ADAM_B2 * v_ref[...] + (1.0 - ADAM_B2) * (g * g)
        m2_ref[...] = m2
        v2_ref[...] = v2
        d_ref[...] = -ADAM_LR * ((m2 / c1) / (jnp.sqrt(v2 / c2) + ADAM_EPS) + ADAM_WD * w_ref[...])

    tile = BS((tr, C), lambda i: (i, 0))
    return pl.pallas_call(
        body, name=name, grid=(R // tr,), out_shape=(SDS((R, C), F32),) * 3,
        in_specs=[tile] * 4, out_specs=(tile,) * 3, compiler_params=_params("parallel"),
    )(w, g, m, v)


def _adamw_gather_call(w, g, m, v, srows, name):
    R, C = w.shape
    tr = R
    while tr * C > 512 * 1024 and tr % 16 == 0:
        tr //= 2
    nsteps = R // tr
    mr = srows.shape[0]
    c1 = 1.0 - ADAM_B1 ** ADAM_STEP
    c2 = 1.0 - ADAM_B2 ** ADAM_STEP

    def body(w_ref, g_ref, m_ref, v_ref, s_ref, d_ref, m2_ref, v2_ref, all_ref, sum_ref,
             x_scr, out_scr, send_sems, recv_sems, local_sem):
        i = pl.program_id(0)
        start, finish = _allgather_parts(x_scr, out_scr, send_sems, recv_sems, local_sem)

        @pl.when(i == 0)
        def _():
            x_scr[...] = s_ref[...]
            start()

        g = g_ref[...]
        m2 = ADAM_B1 * m_ref[...] + (1.0 - ADAM_B1) * g
        v2 = ADAM_B2 * v_ref[...] + (1.0 - ADAM_B2) * (g * g)
        m2_ref[...] = m2
        v2_ref[...] = v2
        d_ref[...] = -ADAM_LR * ((m2 / c1) / (jnp.sqrt(v2 / c2) + ADAM_EPS) + ADAM_WD * w_ref[...])

        @pl.when(i == nsteps - 1)
        def _():
            finish()
            all_ref[...] = out_scr[...]
            acc = out_scr[0:mr, :]
            for d in range(1, N_DEV):
                acc = acc + out_scr[d * mr:(d + 1) * mr, :]
            sum_ref[...] = acc

    tile = BS((tr, C), lambda i: (i, 0))
    return pl.pallas_call(
        body, name=name, grid=(nsteps,),
        out_shape=(SDS((R, C), F32),) * 3 + (SDS((N_DEV * mr, D), F32), SDS((mr, D), F32)),
        in_specs=[tile] * 4 + [BS((mr, D), lambda i: (0, 0))],
        out_specs=(tile,) * 3 + (BS((N_DEV * mr, D), lambda i: (0, 0)), BS((mr, D), lambda i: (0, 0))),
        scratch_shapes=[pltpu.VMEM((mr, D), F32), pltpu.VMEM((N_DEV * mr, D), F32)] + _allgather_sems(),
        compiler_params=_params("arbitrary"),
    )(w, g, m, v, srows)


def _rs_begin(g, c_idx, tag):
    n = g.shape[1]
    g = g.reshape(N_CHIPS, 2, n // 2, D)
    return _add_halves_call(g, _sibling_halves_call(g, tag), c_idx, tag)


def _rs_end(part, recv, c_idx, chip_idx, tag):
    n = 2 * part.shape[1]
    full = _add_chips_call(part, recv, jnp.concatenate([chip_idx, c_idx]), tag)
    return _sibling_join_call(full, tag).reshape(n, D)


def _local_step(x, mod, cact, target, wg, pack, small, c_idx, chip_idx):
    p, h1, wg = _fwd_in_call(x, mod, small["pre_tm"], wg, small["b_in"], pack, small["order"])
    o, oa, st, wg = _hgrn_fwd_call(p, small["logits"], small["hg_norm"], wg, pack)
    u, uc, cb, wg = _conv_fwd_call(p, small["conv_dw"], small["conv_db"], small["ln_g"], small["ln_b"], wg, pack)
    ya, yb, mg, y, x2, h2 = _merge_fwd_call(oa, cb, p, x, mod, small["post_tm"], small["pre_cm"], wg)
    z, da, dy2, dx2, acc_f = _ffn_call(h2, x2, target, mod, small["post_cm"], small["pre_cm"], wg)

    g_ff = _wgrad_call(None, h2, da, "wgrad_ff1", D, lambda i, j: (j, 0), 2 * R_FF)
    g_ff = _wgrad_call(g_ff, z, dy2, "wgrad_ff2", D, lambda i, j: (i, 1), 2 * R_FF)
    g_ff = g_ff.reshape(N_CHIPS, 2, R_FF, D)
    dy, dya, dyb, doa, dcb, dp_gt, acc_m, bs_gt, hr_ff = _merge_bwd_call(dx2, y, ya, yb, p, mod, small["post_tm"],
                                                                        wg, g_ff)
    part_ff = _add_halves_call(g_ff, hr_ff, c_idx, "ff")

    g_br = _wgrad_rows_call(None, oa, dya, "wgrad_br_a", 0)
    g_br = _wgrad_rows_call(g_br, cb, dyb, "wgrad_br_b", 1)
    g_br = _wgrad_rows_call(g_br, mg, dy, "wgrad_out", 2)
    g_br = g_br.reshape(N_CHIPS, 2, 3 * R_BR // 2, D)
    dp_hg, bs_hg, dlg, dgn, recv_ff, hr_br = _hgrn_bwd_call(p, o, doa, st, small["logits"], small["hg_norm"],
                                                            part_ff, g_br)
    part_br = _add_halves_call(g_br, hr_br, c_idx, "br")
    dp_cv, bs_cv, ddw, acc_c, recv_br = _conv_bwd_call(dcb, uc, u, p, small["conv_dw"], small["ln_g"], small["ln_b"],
                                                        part_br)

    g_in = _wgrad_call(None, h1, dp_hg, "wgrad_in_hg", D, lambda i, j: (j // 2, j % 2), R_IN)
    g_in = _wgrad_call(g_in, h1, dp_cv, "wgrad_in_cv", D, lambda i, j: (2, j), R_IN)
    g_in = _wgrad_call(g_in, h1, dp_gt, "wgrad_in_gt", D, lambda i, j: (3, j), R_IN)
    part_in = _rs_begin(g_in, c_idx, "in")
    chip_c = jnp.concatenate([chip_idx, c_idx])
    full_ff = _add_chips_call(part_ff, recv_ff, chip_c, "ff")
    full_br = _add_chips_call(part_br, recv_br, chip_c, "br")
    gx, acc_i, recv_in, full_ff, full_br = _in_bwd_call(dp_hg, dp_cv, dp_gt, x, dx2, mod, small["pre_tm"], wg,
                                                        part_in, full_ff, full_br)
    red_ff = full_ff.reshape(2 * R_FF, D)
    red_br = full_br.reshape(3 * R_BR, D)
    red_in = _rs_end(part_in, recv_in, c_idx, chip_idx, "in")

    zrow = jnp.zeros((1, D), F32)
    rows = [acc_i[0:1], acc_i[1:2], acc_m[0:1], acc_f[2:3], acc_f[3:4], acc_f[0:1],
            acc_i[2:3], acc_m[1:2], acc_f[4:5], acc_f[1:2],
            jnp.concatenate([bs_hg, bs_cv, bs_gt], axis=1).reshape(8, D),
            dlg, dgn, acc_c[0:1], acc_c[1:2], acc_c[2:3],
            ddw,
            cact, acc_f[5:6]] + [zrow] * 6
    return gx, jnp.concatenate(rows, axis=0), red_in, red_ff, red_br


def kernel(x, c, w_ada, b_ada, pre_norm_tm, post_norm_tm, pre_norm_cm, post_norm_cm, w_in, b_in, hg_lb_logits, hg_norm, conv_dw, conv_db, conv_ln_g, conv_ln_b, w_br_a, w_br_b, w_out, w_ff1, w_ff2, loss_target, m_w_ada, m_b_ada, m_pre_norm_tm, m_post_norm_tm, m_pre_norm_cm, m_post_norm_cm, m_w_in, m_b_in, m_hg_lb_logits, m_hg_norm, m_conv_dw, m_conv_db, m_conv_ln_g, m_conv_ln_b, m_w_br_a, m_w_br_b, m_w_out, m_w_ff1, m_w_ff2, v_w_ada, v_b_ada, v_pre_norm_tm, v_post_norm_tm, v_pre_norm_cm, v_post_norm_cm, v_w_in, v_b_in, v_hg_lb_logits, v_hg_norm, v_conv_dw, v_conv_db, v_conv_ln_g, v_conv_ln_b, v_w_br_a, v_w_br_b, v_w_out, v_w_ff1, v_w_ff2):
    xi, yi, ci = lax.axis_index("x"), lax.axis_index("y"), lax.axis_index("c")
    chip = 2 * xi + yi
    c_idx = jnp.reshape(ci, (1,)).astype(jnp.int32)
    chip_idx = jnp.reshape(chip, (1,)).astype(jnp.int32)

    def pack_small(ada_b, pre_t, post_t, pre_c, post_c, in_b, lg, hgn, cdb, lng, lnb, cdw):
        flat = jnp.concatenate([cdw[0].reshape(-1), jnp.zeros((8 * D - CONV_K * 256,), F32)]).reshape(8, D)
        return jnp.concatenate([ada_b.reshape(6, D), pre_t, post_t, pre_c, post_c, in_b.reshape(8, D), lg, hgn,
                                cdb, lng, lnb, flat], axis=0)

    w_in_halves = w_in[0].reshape(D, 2, D).transpose(1, 0, 2).reshape(R_IN, D)
    pack = jnp.concatenate([w_in_halves, w_ff1[0], w_ff2[0], w_br_a[0], w_br_b[0], w_out[0]],
                           axis=0).astype(BF16)
    wg = lax.dynamic_update_slice(lax.empty((N_CHIPS, PACK_W, D), BF16), pack[None], (chip, 0, 0))
    wa = 6 * D // N_CHIPS
    me = 4 * xi + 2 * yi + ci
    dw_blk = jnp.concatenate([conv_dw[0].reshape(-1), jnp.zeros((8 * D - CONV_K * 256,), F32)]).reshape(8, D)
    dw_all, ca_all, mod_all = _prologue_call(
        dw_blk, jnp.broadcast_to(c, (8, D)), w_ada[0].astype(BF16),
        lax.dynamic_slice_in_dim(b_ada, chip * wa, wa, axis=1))
    order = jnp.stack([chip, 2 * (1 - xi) + yi, 2 * xi + (1 - yi), 2 * (1 - xi) + (1 - yi)]).astype(jnp.int32)
    dw_all = dw_all.reshape(N_CHIPS, 2, 8 * D)[:, 0, :CONV_K * 256].reshape(N_CHIPS, CONV_K, 256)
    dw_full = dw_all.transpose(1, 0, 2).reshape(CONV_K, D)
    cact = lax.dynamic_slice_in_dim(ca_all, me * 8, 1, axis=0)
    mod_mine = lax.dynamic_index_in_dim(mod_all.reshape(N_CHIPS, 2, N_DEV, 8, wa)[:, 0, :, 0, :], me, axis=1,
                                        keepdims=False)
    mod = mod_mine.reshape(1, 6 * D)

    small = dict(b_ada=b_ada, pre_tm=pre_norm_tm, post_tm=post_norm_tm, pre_cm=pre_norm_cm, post_cm=post_norm_cm,
                 b_in=b_in, logits=hg_lb_logits, hg_norm=hg_norm, conv_dw=dw_full, conv_db=conv_db,
                 ln_g=conv_ln_g, ln_b=conv_ln_b, order=order)

    gx, srows, red_in, red_ff, red_br = _local_step(x[0], mod, cact, loss_target[0], wg, pack, small, c_idx,
                                                    chip_idx)

    shapes = {"in": w_in.shape, "br_a": w_br_a.shape, "br_b": w_br_b.shape, "out": w_out.shape,
              "ff1": w_ff1.shape, "ff2": w_ff2.shape}
    offs = {"in": (red_in, 0, R_IN), "ff1": (red_ff, 0, R_FF), "ff2": (red_ff, R_FF, 2 * R_FF),
            "br_a": (red_br, 0, R_BR), "br_b": (red_br, R_BR, 2 * R_BR), "out": (red_br, 2 * R_BR, 3 * R_BR)}
    wmv = {"in": (w_in, m_w_in, v_w_in), "br_a": (w_br_a, m_w_br_a, v_w_br_a), "br_b": (w_br_b, m_w_br_b, v_w_br_b),
           "out": (w_out, m_w_out, v_w_out), "ff1": (w_ff1, m_w_ff1, v_w_ff1), "ff2": (w_ff2, m_w_ff2, v_w_ff2)}
    res = {}
    for n in offs:
        shp = shapes[n]
        g2d = offs[n][0][offs[n][1]:offs[n][2]]
        if n == "in":
            g2d = g2d.reshape(2, D, D).transpose(1, 0, 2)
        g2d = g2d.reshape(shp[1], shp[2])
        w_, m_, v_ = (a[0] for a in wmv[n])
        if n == "in":
            d_, m2_, v2_, sall, ssum = _adamw_gather_call(w_, g2d, m_, v_, srows, "adamw_in")
        else:
            d_, m2_, v2_ = _adamw_call(w_, g2d, m_, v_, "adamw_" + n)
        res[n] = tuple(a.reshape(shp) for a in (g2d, d_, m2_, v2_))

    sall = sall.reshape(N_DEV, SMALL_ROWS, D)
    loss = jnp.sum(ssum[57])
    dmod_all = sall[:, 0:6, :].reshape(N_DEV, 6 * D)
    g_ada = _outer_call(sall[:, 56, :], lax.dynamic_slice_in_dim(dmod_all, chip * wa, wa, axis=1))
    g_dw = lax.dynamic_slice_in_dim(ssum[24:24 + CONV_K], chip * 256, 256, axis=1)
    g_small = jnp.concatenate(
        [ssum[0:24], jnp.concatenate([g_dw.reshape(-1), jnp.zeros((8 * D - CONV_K * 256,), F32)]).reshape(8, D)],
        axis=0)
    d_, m2_, v2_ = _adamw_call(w_ada[0], g_ada, m_w_ada[0], v_w_ada[0], "adamw_ada")
    res["ada"] = tuple(a.reshape(w_ada.shape) for a in (g_ada, d_, m2_, v2_))

    ws = pack_small(b_ada, pre_norm_tm, post_norm_tm, pre_norm_cm, post_norm_cm, b_in, hg_lb_logits, hg_norm,
                    conv_db, conv_ln_g, conv_ln_b, conv_dw)
    ms = pack_small(m_b_ada, m_pre_norm_tm, m_post_norm_tm, m_pre_norm_cm, m_post_norm_cm, m_b_in, m_hg_lb_logits,
                    m_hg_norm, m_conv_db, m_conv_ln_g, m_conv_ln_b, m_conv_dw)
    vs = pack_small(v_b_ada, v_pre_norm_tm, v_post_norm_tm, v_pre_norm_cm, v_post_norm_cm, v_b_in, v_hg_lb_logits,
                    v_hg_norm, v_conv_db, v_conv_ln_g, v_conv_ln_b, v_conv_dw)
    sres = (g_small,) + tuple(_adamw_call(ws, g_small, ms, vs, "adamw_small"))

    def unpack_small(t):
        return {"b_ada": t[0:6].reshape(1, 6 * D), "pre_tm": t[6:7], "post_tm": t[7:8], "pre_cm": t[8:9],
                "post_cm": t[9:10], "b_in": t[10:18].reshape(1, IN_COLS), "logits": t[18:20], "hg_norm": t[20:21],
                "conv_db": t[21:22], "ln_g": t[22:23], "ln_b": t[23:24],
                "conv_dw": t[24:32].reshape(-1)[:CONV_K * 256].reshape(1, CONV_K, 256)}

    order = ["ada", "b_ada", "pre_tm", "post_tm", "pre_cm", "post_cm", "in", "b_in", "logits", "hg_norm", "conv_dw",
             "conv_db", "ln_g", "ln_b", "br_a", "br_b", "out", "ff1", "ff2"]
    outs = [loss, gx.reshape(x.shape)]
    for kind in range(4):
        sm = unpack_small(sres[kind])
        for n in order:
            outs.append(res[n][kind] if n in res else sm[n])
    return tuple(outs)
```

```python
import jax
import jax.numpy as jnp
from jax import lax
from jax.experimental import pallas as pl
from jax.experimental.pallas import tpu as pltpu

F32, BF16 = jnp.float32, jnp.bfloat16
SDS = jax.ShapeDtypeStruct
BS = pl.BlockSpec
MESH = pl.DeviceIdType.MESH
HI = lax.Precision.HIGHEST

D = 1024
D_FF = 4096
IN_COLS = 8192
HEADS, DK = 8, 128
CHUNK = 128
CONV_K = 31
HALO = 32
SUB = 32
EPS = 1e-6
N_CHIPS, N_DEV = 4, 8
TM = 256
TB = 512
VMEM_LIMIT = 56 * 1024 * 1024

R_IN, R_BR, R_FF = 2048, 256, 1024
PACK_W = R_IN + 3 * R_BR + 2 * R_FF
O_IN, O_FF1, O_FF2, O_BRA, O_BRB, O_OUT = 0, 2048, 3072, 4096, 4352, 4608
SMALL_ROWS = 64

ADAM_LR, ADAM_B1, ADAM_B2, ADAM_EPS, ADAM_WD, ADAM_STEP = 0.001, 0.9, 0.999, 1e-08, 0.01, 10

NN = (((1,), (0,)), ((), ()))
NT = (((1,), (1,)), ((), ()))
TN = (((0,), (0,)), ((), ()))


def _mm(a, b, dims=NN, precision=None):
    return lax.dot_general(a, b, dims, preferred_element_type=F32, precision=precision)


def _sig(v):
    return jax.nn.sigmoid(v)


def _dsilu(v, s):
    return s * (1.0 + v * (1.0 - s))


def _params(*sem):
    return pltpu.CompilerParams(dimension_semantics=sem if sem else None, vmem_limit_bytes=VMEM_LIMIT)


def _rowsum(v):
    return jnp.sum(v, axis=0, keepdims=True)


def _mesh_pos():
    return lax.axis_index("x"), lax.axis_index("y"), lax.axis_index("c")


def _allgather_parts(x_ref, out_ref, send_sems, recv_sems, local_sem):
    m_per = x_ref.shape[0]
    x, y, c = _mesh_pos()
    me, sibling = (x, y, c), (x, y, 1 - c)
    chips = [(1 - x, y), (x, 1 - y), (1 - x, 1 - y)]

    def rows(px, py, pc):
        return out_ref.at[pl.ds((4 * px + 2 * py + pc) * m_per, m_per), :]

    def copy(k, block, to, src=None):
        return pltpu.make_async_remote_copy(
            src_ref=rows(*block) if src is None else src, dst_ref=rows(*block),
            send_sem=send_sems.at[k], recv_sem=recv_sems.at[k], device_id=to, device_id_type=MESH)

    def first():
        return [copy(0, me, sibling, src=x_ref)] + [copy(1 + j, me, (*chip, c), src=x_ref)
                                                    for j, chip in enumerate(chips)]

    def start():
        pltpu.make_async_copy(x_ref, rows(*me), local_sem).start()
        for cp in first():
            cp.start()

    def finish():
        passed = [copy(4 + j, (*chip, c), sibling) for j, chip in enumerate(chips)]
        for j, chip in enumerate(chips):
            copy(1 + j, (*chip, c), me).wait_recv()
            passed[j].start()
        copy(0, sibling, me).wait_recv()
        for j, chip in enumerate(chips):
            copy(4 + j, (*chip, 1 - c), me).wait_recv()
        for cp in first() + passed:
            cp.wait_send()
        pltpu.make_async_copy(x_ref, rows(*me), local_sem).wait()

    return start, finish


def _allgather_sems():
    return [pltpu.SemaphoreType.DMA((7,)), pltpu.SemaphoreType.DMA((7,)), pltpu.SemaphoreType.DMA]
def _gather_sems(n_ranges):
    return [pltpu.SemaphoreType.DMA((6 * n_ranges,)), pltpu.SemaphoreType.DMA((6 * n_ranges,))]


def _pack_gather(pack_ref, wg_ref, send_sems, recv_sems, ranges):
    x, y, c = _mesh_pos()
    me, sibling = (x, y, c), (x, y, 1 - c)
    chips = [(1 - x, y), (x, 1 - y), (1 - x, 1 - y)]

    def land(r, px, py, pc):
        off, n = ranges[r]
        return wg_ref.at[2 * px + py, pl.ds(off + pc * (n // 2), n // 2), :]

    def mine(r):
        off, n = ranges[r]
        return pack_ref.at[pl.ds(off + c * (n // 2), n // 2), :]

    def copy(r, k, block, to, src=None):
        return pltpu.make_async_remote_copy(
            src_ref=land(r, *block) if src is None else src, dst_ref=land(r, *block),
            send_sem=send_sems.at[6 * r + k], recv_sem=recv_sems.at[6 * r + k], device_id=to, device_id_type=MESH)

    def start():
        for r in range(len(ranges)):
            for j, chip in enumerate(chips):
                copy(r, j, me, (*chip, c), src=mine(r)).start()

    def finish():
        for r in range(len(ranges)):
            for j, chip in enumerate(chips):
                copy(r, j, (*chip, c), me).wait_recv()
                copy(r, 3 + j, (*chip, c), sibling).start()
        for r in range(len(ranges)):
            for j, chip in enumerate(chips):
                copy(r, 3 + j, (*chip, 1 - c), me).wait_recv()
        for r in range(len(ranges)):
            for j, chip in enumerate(chips):
                copy(r, j, me, (*chip, c), src=mine(r)).wait_send()
                copy(r, 3 + j, (*chip, c), sibling).wait_send()

    return start, finish


def _relay_sems():
    return [pltpu.SemaphoreType.DMA((8,)), pltpu.SemaphoreType.DMA((8,))]


def _relay_gather(pack_ref, wg_ref, send_sems, recv_sems, off, n):
    x, y, c = _mesh_pos()
    me, sibling = (x, y, c), (x, y, 1 - c)
    chips = [(1 - x, y), (x, 1 - y), (1 - x, 1 - y)]
    h, q = n // 2, n // 4

    def land(px, py, pc, piece=None):
        if piece is None:
            return wg_ref.at[2 * px + py, pl.ds(off + pc * h, h), :]
        return wg_ref.at[2 * px + py, pl.ds(off + pc * h + piece * q, q), :]

    def copy(k, ref, to, src=None):
        return pltpu.make_async_remote_copy(
            src_ref=ref if src is None else src, dst_ref=ref, send_sem=send_sems.at[k], recv_sem=recv_sems.at[k],
            device_id=to, device_id_type=MESH)

    def direct(j):
        return copy(j, land(x, y, c), (*chips[j], c), src=pack_ref.at[pl.ds(off + c * h, h), :])

    def relayed(j):
        if j == 0:
            return copy(6, land(*chips[0], c, 1), (x, 1 - y, c))
        return copy(7, land(*chips[1], c, 0), (1 - x, y, c))

    def start():
        direct(0).start()
        direct(1).start()

    def arrive(j):
        if j == 0:
            for k in range(2):
                copy(k, land(*chips[k], c), me).wait_recv()
                relayed(k).start()
                copy(3 + k, land(*chips[k], c), sibling).start()
        if j == 2:
            copy(7, land(*chips[2], c, 0), me).wait_recv()
            copy(6, land(*chips[2], c, 1), me).wait_recv()
            copy(5, land(*chips[2], c), sibling).start()
        copy(3 + j, land(*chips[j], 1 - c), me).wait_recv()

    def drain():
        for j in range(2):
            direct(j).wait_send()
            relayed(j).wait_send()
        for j in range(3):
            copy(3 + j, land(*chips[j], c), sibling).wait_send()

    return start, arrive, drain


def _prologue_call(dw_blk, c_blk, w_ada, b_ada):
    wa = w_ada.shape[1]

    def body(dw_ref, c_ref, wa_ref, ba_ref, dwg_ref, ca_ref, modg_ref,
             cg_scr, part_scr, s1, r1, l1, s2, r2, l2, s3, r3, l3):
        start_c, finish_c = _allgather_parts(c_ref, cg_scr, s2, r2, l2)
        start_dw, finish_dw = _allgather_parts(dw_ref, dwg_ref, s1, r1, l1)
        start_mod, finish_mod = _allgather_parts(part_scr, modg_ref, s3, r3, l3)
        start_c()
        start_dw()
        finish_c()
        cv = cg_scr[...]
        ca = cv * _sig(cv)
        ca_ref[...] = ca
        part_scr[...] = _mm(ca.astype(BF16), wa_ref[...]) + ba_ref[...]
        start_mod()
        finish_dw()
        finish_mod()

    vm = BS(memory_space=pltpu.VMEM)
    return pl.pallas_call(
        body, name="prologue_adaln_conv_dw",
        out_shape=(SDS((N_DEV * 8, D), F32), SDS((N_DEV * 8, D), F32), SDS((N_DEV * N_DEV * 8, wa), F32)),
        in_specs=[vm, vm, vm, vm], out_specs=(vm, vm, vm),
        scratch_shapes=[pltpu.VMEM((N_DEV * 8, D), F32), pltpu.VMEM((N_DEV * 8, wa), F32)]
        + _allgather_sems() + _allgather_sems() + _allgather_sems(),
        compiler_params=pltpu.CompilerParams(vmem_limit_bytes=VMEM_LIMIT),
    )(dw_blk, c_blk, w_ada, b_ada)


def _halves_exchange(g_ref, out_ref, send_sems, recv_sems):
    x, y, c = _mesh_pos()

    def copies():
        return [pltpu.make_async_remote_copy(
            src_ref=g_ref.at[k, 1 - c], dst_ref=out_ref.at[k], send_sem=send_sems.at[k], recv_sem=recv_sems.at[k],
            device_id=(x, y, 1 - c), device_id_type=MESH) for k in range(N_CHIPS)]

    def start():
        for cp in copies():
            cp.start()

    def finish():
        for cp in copies():
            cp.wait()

    return start, finish


def _halves_sems():
    return [pltpu.SemaphoreType.DMA((N_CHIPS,)), pltpu.SemaphoreType.DMA((N_CHIPS,))]


def _sibling_halves_call(g, tag):
    _, _, h, n = g.shape

    def body(g_ref, out_ref, send_sems, recv_sems):
        start, finish = _halves_exchange(g_ref, out_ref, send_sems, recv_sems)
        start()
        finish()

    return pl.pallas_call(
        body, name="rs_sibling_halves_" + tag, out_shape=SDS((N_CHIPS, h, n), g.dtype),
        in_specs=[BS(memory_space=pl.ANY)], out_specs=BS(memory_space=pl.ANY),
        scratch_shapes=_halves_sems(),
    )(g)


def _chip_exchange(p_ref, out_ref, send_sems, recv_sems):
    x, y, c = _mesh_pos()
    chips = [(1 - x, y), (x, 1 - y), (1 - x, 1 - y)]

    def copies():
        return [pltpu.make_async_remote_copy(
            src_ref=p_ref.at[2 * cx + cy], dst_ref=out_ref.at[j], send_sem=send_sems.at[j], recv_sem=recv_sems.at[j],
            device_id=(cx, cy, c), device_id_type=MESH) for j, (cx, cy) in enumerate(chips)]

    def start():
        for cp in copies():
            cp.start()

    def finish():
        for cp in copies():
            cp.wait()

    return start, finish


def _exchange_sems():
    return [pltpu.SemaphoreType.DMA((3,)), pltpu.SemaphoreType.DMA((3,))]


def _join_exchange(in_ref, out_ref, send_sems, recv_sems):
    h = in_ref.shape[1]
    q = h // 4
    x, y, c = _mesh_pos()

    def copy(k, half):
        return pltpu.make_async_remote_copy(
            src_ref=in_ref.at[half, pl.ds(k * q, q)], dst_ref=out_ref.at[half, pl.ds(k * q, q)],
            send_sem=send_sems.at[k], recv_sem=recv_sems.at[k],
            device_id=(x, y, 1 - c), device_id_type=MESH)

    def start():
        for k in range(4):
            copy(k, c).start()

    def finish():
        for k in range(4):
            copy(k, c).wait_send()
            copy(k, 1 - c).wait_recv()

    return start, finish


def _join_sems():
    return [pltpu.SemaphoreType.DMA((4,)), pltpu.SemaphoreType.DMA((4,))]


def _sibling_join_call(full, tag):
    def body(in_ref, out_ref, send_sems, recv_sems):
        start, finish = _join_exchange(in_ref, out_ref, send_sems, recv_sems)
        start()
        finish()

    return pl.pallas_call(
        body, name="rs_sibling_join_" + tag, out_shape=SDS(full.shape, full.dtype),
        in_specs=[BS(memory_space=pl.ANY)], out_specs=BS(memory_space=pl.ANY),
        scratch_shapes=_join_sems(), input_output_aliases={0: 0},
    )(full)


def _add_halves_call(g, recv, c_idx, tag):
    _, _, h, n = g.shape
    tr = h // 2

    def body(c_ref, g_ref, r_ref, o_ref):
        o_ref[...] = (g_ref[...].astype(F32) + r_ref[...].astype(F32)).astype(BF16)

    return pl.pallas_call(
        body, name="rs_add_halves_" + tag, out_shape=SDS((N_CHIPS, h, n), BF16),
        grid_spec=pltpu.PrefetchScalarGridSpec(
            num_scalar_prefetch=1, grid=(N_CHIPS, 2),
            in_specs=[BS((None, None, tr, n), lambda k, r, c_ref: (k, c_ref[0], r, 0)),
                      BS((None, tr, n), lambda k, r, c_ref: (k, r, 0))],
            out_specs=BS((None, tr, n), lambda k, r, c_ref: (k, r, 0))),
        compiler_params=_params("arbitrary", "arbitrary"),
    )(c_idx, g, recv)


def _add_chips_call(p, recv, chip_c_idx, tag):
    _, h, n = p.shape
    tr = h // 2

    def body(k_ref, p_ref, r_ref, o_ref):
        acc = p_ref[...].astype(F32)
        for j in range(3):
            acc = acc + r_ref[j].astype(F32)
        o_ref[...] = acc

    return pl.pallas_call(
        body, name="rs_add_chips_" + tag, out_shape=SDS((2, h, n), F32),
        grid_spec=pltpu.PrefetchScalarGridSpec(
            num_scalar_prefetch=1, grid=(2,),
            in_specs=[BS((None, tr, n), lambda r, k_ref: (k_ref[0], r, 0)),
                      BS((3, tr, n), lambda r, k_ref: (0, r, 0))],
            out_specs=BS((None, tr, n), lambda r, k_ref: (k_ref[1], r, 0))),
        compiler_params=_params("arbitrary"),
    )(chip_c_idx, p, recv)


def _load_rows(wg_hbm, w_vmem, sem, off):
    cp = pltpu.make_async_copy(wg_hbm.at[:, pl.ds(off, w_vmem.shape[1]), :], w_vmem, sem)
    cp.start()
    return cp


def _fwd_in_call(x, mod, pre_tm, wg, b_in, pack, order):
    S = x.shape[0]
    tmf = 2 * TM
    nt = S // tmf
    wc = IN_COLS // N_CHIPS

    def body(ord_ref, x_ref, mod_ref, g_ref, w_hbm, b_ref, pack_ref, p_ref, h_hbm, wg_out, w_vmem, h_scr, sems,
             send_sems, recv_sems, send_sems2, recv_sems2):
        q, i = pl.program_id(0), pl.program_id(1)
        rows = pl.ds(pl.multiple_of(i * tmf, tmf), tmf)
        start, arrive, drain = _relay_gather(pack_ref, wg_out, send_sems, recv_sems, O_IN, R_IN)
        start2, finish2 = _pack_gather(pack_ref, wg_out, send_sems2, recv_sems2, [(O_OUT, R_BR)])

        def weights(phase):
            return pltpu.make_async_copy(wg_out.at[ord_ref[phase], pl.ds(O_IN, R_IN), :], w_vmem.at[phase % 2],
                                         sems.at[phase % 2])

        @pl.when((q == 0) & (i == 0))
        def _():
            start()
            weights(0).start()
            weights(0).wait()

        @pl.when((q == 1) & (i == 0))
        def _():
            arrive(0)
            start2()
            weights(1).start()
            weights(1).wait()
            arrive(1)
            weights(2).start()

        @pl.when((q == 2) & (i == 0))
        def _():
            weights(2).wait()
            arrive(2)
            weights(3).start()

        @pl.when((q == 3) & (i == 0))
        def _():
            weights(3).wait()

        @pl.when(q == 0)
        def _():
            xv = x_ref[...]
            r = lax.rsqrt(jnp.mean(xv * xv, axis=-1, keepdims=True) + EPS)
            h = xv * r * g_ref[...] * (1.0 + mod_ref[:, D:2 * D]) + mod_ref[:, 0:D]
            h_scr[rows, :] = h.astype(BF16)

        hb = h_scr[rows, :]
        slot = q % 2
        for k in range(wc // D):
            p_ref[:, k * D:(k + 1) * D] = _mm(hb, w_vmem[slot, k * D:(k + 1) * D, :]) + b_ref[:, k * D:(k + 1) * D]

        @pl.when((q == N_CHIPS - 1) & (i == nt - 1))
        def _():
            cp = pltpu.make_async_copy(h_scr, h_hbm, sems.at[0])
            cp.start()
            drain()
            finish2()
            cp.wait()

    hbm = BS(memory_space=pl.ANY)
    return pl.pallas_call(
        body, name="fwd_in", out_shape=(SDS((S, IN_COLS), F32), SDS((S, D), BF16), SDS(wg.shape, wg.dtype)),
        grid_spec=pltpu.PrefetchScalarGridSpec(
            num_scalar_prefetch=1, grid=(N_CHIPS, nt),
            in_specs=[BS((tmf, D), lambda q, i, o: (jnp.where(q == 0, i, nt - 1), 0)),
                      BS((1, 6 * D), lambda q, i, o: (0, 0)),
                      BS((1, D), lambda q, i, o: (0, 0)), hbm, BS((1, wc), lambda q, i, o: (0, o[q])), hbm],
            out_specs=(BS((tmf, wc), lambda q, i, o: (i, o[q])), hbm, hbm),
            scratch_shapes=[pltpu.VMEM((2, R_IN, D), BF16), pltpu.VMEM((S, D), BF16), pltpu.SemaphoreType.DMA((2,))]
            + _relay_sems() + _gather_sems(1)),
        input_output_aliases={4: 2},
        compiler_params=_params("arbitrary", "arbitrary"),
    )(order, x, mod, pre_tm, wg, b_in, pack)


def _lower_bound(lg_ref):
    l0, l1 = lg_ref[0:1, :], lg_ref[1:2, :]
    mx = jnp.maximum(l0, l1)
    e0, e1 = jnp.exp(l0 - mx), jnp.exp(l1 - mx)
    return e0 / (e0 + e1)


def _tri_masks():
    ri = lax.broadcasted_iota(jnp.int32, (CHUNK, CHUNK), 0)
    ci = lax.broadcasted_iota(jnp.int32, (CHUNK, CHUNK), 1)
    return (ri >= ci).astype(F32), (ci >= ri).astype(F32)


def _cumsum_mm(tri, g):
    tb = tri.astype(BF16)
    hi = g.astype(BF16)
    r1 = g - hi.astype(F32)
    mid = r1.astype(BF16)
    lo = (r1 - mid.astype(F32)).astype(BF16)
    return _mm(tb, hi) + _mm(tb, mid) + _mm(tb, lo)


def _hg_gates(q_r, f_r, lb, tril):
    sq = _sig(q_r)
    q = q_r * sq
    sf = _sig(f_r)
    f = lb + (1.0 - lb) * sf
    k = 1.0 - f
    g = jnp.log(f)
    b = _cumsum_mm(tril, g)
    b_last = _rowsum(g)
    row = lax.broadcasted_iota(jnp.int32, g.shape, 0)
    ref = _rowsum(jnp.where(row < CHUNK // 2, g, 0.0))
    e = jnp.exp(b)
    eq = jnp.exp(jnp.minimum(b - ref, 80.0))
    ek = jnp.exp(jnp.minimum(ref - b, 80.0))
    dd = jnp.exp(b_last - b)
    return dict(sq=sq, q=q, sf=sf, f=f, k=k, e=e, eq=eq, ek=ek, dd=dd, elast=jnp.exp(b_last),
                qe=q * e, qt=q * eq, kt=k * ek, kd=k * dd)


def _hgrn_fwd_call(p, logits, gn, wg, pack):
    S = p.shape[0]
    ncb = TB // CHUNK
    ranges = [(O_FF1, R_FF)]

    def body(q_ref, f_ref, v_ref, og_ref, lg_ref, gn_ref, wg_in, pack_ref, o_ref, oa_ref, st_ref, wg_out,
             st_scr, send_sems, recv_sems):
        start, finish = _pack_gather(pack_ref, wg_out, send_sems, recv_sems, ranges)

        @pl.when(pl.program_id(0) == 0)
        def _():
            start()
            st_scr[...] = jnp.zeros_like(st_scr)

        lb = _lower_bound(lg_ref)
        tril, _ = _tri_masks()

        def chunk(ci, carry):
            rows = pl.ds(pl.multiple_of(ci * CHUNK, CHUNK), CHUNK)
            st_ref[ci] = st_scr[...]
            t = _hg_gates(q_ref[rows, :], f_ref[rows, :], lb, tril)
            v = v_ref[rows, :]
            for h in range(HEADS):
                sl = slice(h * DK, (h + 1) * DK)
                stp = st_scr[:, sl]
                vb = v[:, sl].astype(BF16)
                inter = _mm(t["qe"][:, sl].astype(BF16), stp.astype(BF16), NT)
                a = jnp.where(tril > 0.5, _mm(t["qt"][:, sl].astype(BF16), t["kt"][:, sl].astype(BF16), NT), 0.0)
                o = inter + _mm(a.astype(BF16), vb)
                st_scr[:, sl] = stp * t["elast"][:, sl] + _mm(vb, t["kd"][:, sl].astype(BF16), TN)
                oh = o * lax.rsqrt(jnp.mean(o * o, axis=-1, keepdims=True) + EPS)
                og = og_ref[rows, sl]
                o_ref[rows, sl] = o
                oa_ref[rows, sl] = (oh * gn_ref[:, sl] * (og * _sig(og))).astype(BF16)
            return carry

        lax.fori_loop(0, ncb, chunk, 0)

        @pl.when(pl.program_id(0) == S // TB - 1)
        def _():
            finish()

    col = lambda j: BS((TB, D), lambda i, j=j: (i, j))
    hbm = BS(memory_space=pl.ANY)
    return pl.pallas_call(
        body, name="hgrn_fwd", grid=(S // TB,),
        out_shape=(SDS((S, D), F32), SDS((S, D), BF16), SDS((S // CHUNK, DK, D), F32), SDS(wg.shape, wg.dtype)),
        in_specs=[col(0), col(1), col(2), col(3), BS((2, D), lambda i: (0, 0)), BS((1, D), lambda i: (0, 0)),
                  hbm, hbm],
        out_specs=(BS((TB, D), lambda i: (i, 0)), BS((TB, D), lambda i: (i, 0)),
                   BS((ncb, DK, D), lambda i: (i, 0, 0)), hbm),
        scratch_shapes=[pltpu.VMEM((DK, D), F32)] + _gather_sems(len(ranges)),
        input_output_aliases={6: 3},
        compiler_params=_params("arbitrary"),
    )(p, p, p, p, logits, gn, wg, pack)


def _layernorm_stats(uc):
    mu = jnp.mean(uc, axis=-1, keepdims=True)
    xc = uc - mu
    rs = lax.rsqrt(jnp.mean(xc * xc, axis=-1, keepdims=True) + EPS)
    return xc * rs, rs


EXT = HALO + TM + 8


def _fill_shifted(ext, shifted):
    for m in range(1, 8):
        shifted[m - 1] = ext[m:m + HALO + TM, :]


def _window(ext, shifted, s0, n):
    m = s0 % 8
    q = s0 - m
    return ext[q:q + n, :] if m == 0 else shifted[m - 1, q:q + n, :]


def _conv_fwd_call(p, dw, db, ln_g, ln_b, wg, pack):
    S = p.shape[0]
    ranges = [(O_FF2, R_FF), (O_BRA, 2 * R_BR)]

    def body(cv_ref, cg_ref, dw_ref, db_ref, g_ref, b_ref, wg_in, pack_ref, u_ref, uc_ref, cb_ref, wg_out,
             uext, ush, send_sems, recv_sems):
        start, finish = _pack_gather(pack_ref, wg_out, send_sems, recv_sems, ranges)

        @pl.when(pl.program_id(0) == 0)
        def _():
            start()
            uext[0:HALO, :] = jnp.zeros((HALO, D), F32)
            uext[HALO + TM:EXT, :] = jnp.zeros((EXT - HALO - TM, D), F32)

        u = cv_ref[...] * _sig(cg_ref[...])
        uext[HALO:HALO + TM, :] = u
        u_ref[...] = u
        _fill_shifted(uext, ush)
        for rb in range(TM // SUB):
            acc = jnp.broadcast_to(db_ref[...], (SUB, D))
            for j in range(CONV_K):
                s0 = HALO - (CONV_K - 1) + j + rb * SUB
                acc = acc + dw_ref[j:j + 1, :] * _window(uext, ush, s0, SUB)
            uc_ref[rb * SUB:(rb + 1) * SUB, :] = acc
            xh, _ = _layernorm_stats(acc)
            ln = xh * g_ref[...] + b_ref[...]
            cb_ref[rb * SUB:(rb + 1) * SUB, :] = (ln * _sig(ln)).astype(BF16)
        uext[0:HALO, :] = uext[TM:TM + HALO, :]

        @pl.when(pl.program_id(0) == S // TM - 1)
        def _():
            finish()

    vec = BS((1, D), lambda i: (0, 0))
    hbm = BS(memory_space=pl.ANY)
    return pl.pallas_call(
        body, name="conv_fwd", grid=(S // TM,),
        out_shape=(SDS((S, D), F32), SDS((S, D), F32), SDS((S, D), BF16), SDS(wg.shape, wg.dtype)),
        in_specs=[BS((TM, D), lambda i: (i, 4)), BS((TM, D), lambda i: (i, 5)),
                  BS((CONV_K, D), lambda i: (0, 0)), vec, vec, vec, hbm, hbm],
        out_specs=(BS((TM, D), lambda i: (i, 0)),) * 3 + (hbm,),
        scratch_shapes=[pltpu.VMEM((EXT, D), F32), pltpu.VMEM((7, HALO + TM, D), F32)] + _gather_sems(len(ranges)),
        input_output_aliases={6: 3},
        compiler_params=_params("arbitrary"),
    )(p, p, dw, db, ln_g, ln_b, wg, pack)


def _mm_rows(a, w_ref):
    acc = _mm(a[:, 0:R_BR], w_ref[0])
    for k in range(1, N_CHIPS):
        acc = acc + _mm(a[:, k * R_BR:(k + 1) * R_BR], w_ref[k])
    return acc


def _mm_rows_t(a, w_ref):
    return jnp.concatenate([_mm(a, w_ref[k], NT) for k in range(N_CHIPS)], axis=1)


def _br_spec(off):
    return BS((N_CHIPS, R_BR, D), lambda i: (0, off // R_BR, 0))


def _merge_fwd_call(oa, cb, p, x, mod, post_tm, pre_cm, wg):
    S = x.shape[0]

    def body(oa_ref, cb_ref, ga_ref, gb_ref, x_ref, mod_ref, post_ref, pre_ref, wa_ref, wb_ref, wo_ref,
             ya_ref, yb_ref, mg_ref, y_ref, x2_ref, h2_ref):
        ya = _mm_rows(oa_ref[...], wa_ref)
        yb = _mm_rows(cb_ref[...], wb_ref)
        ya_ref[...] = ya.astype(BF16)
        yb_ref[...] = yb.astype(BF16)
        mg = (_sig(ga_ref[...]) * ya + _sig(gb_ref[...]) * yb).astype(BF16)
        mg_ref[...] = mg
        y = _mm_rows(mg, wo_ref)
        y_ref[...] = y
        n = y * lax.rsqrt(jnp.mean(y * y, axis=-1, keepdims=True) + EPS) * post_ref[...]
        x2 = x_ref[...] + mod_ref[:, 2 * D:3 * D] * n
        x2_ref[...] = x2
        r2 = lax.rsqrt(jnp.mean(x2 * x2, axis=-1, keepdims=True) + EPS)
        h2 = x2 * r2 * pre_ref[...] * (1.0 + mod_ref[:, 4 * D:5 * D]) + mod_ref[:, 3 * D:4 * D]
        h2_ref[...] = h2.astype(BF16)

    tile = BS((TM, D), lambda i: (i, 0))
    vec = BS((1, D), lambda i: (0, 0))
    return pl.pallas_call(
        body, name="merge_fwd", grid=(S // TM,),
        out_shape=(SDS((S, D), BF16), SDS((S, D), BF16), SDS((S, D), BF16), SDS((S, D), F32), SDS((S, D), F32),
                   SDS((S, D), BF16)),
        in_specs=[tile, tile, BS((TM, D), lambda i: (i, 6)), BS((TM, D), lambda i: (i, 7)), tile,
                  BS((1, 6 * D), lambda i: (0, 0)), vec, vec, _br_spec(O_BRA), _br_spec(O_BRB), _br_spec(O_OUT)],
        out_specs=(tile,) * 6,
        compiler_params=_params("arbitrary"),
    )(oa, cb, p, p, x, mod, post_tm, pre_cm, wg, wg, wg)


def _ffn_call(h2, x2, target, mod, post_cm, pre_cm, wg):
    S = x2.shape[0]

    def body(h2_ref, x2_ref, t_ref, mod_ref, post_ref, pre_ref, w_hbm,
             z_ref, da_ref, dy2_ref, dx2_ref, acc_ref, w1_v, w2_v, ra_scr, sems):
        @pl.when(pl.program_id(0) == 0)
        def _():
            c1 = _load_rows(w_hbm, w1_v, sems.at[0], O_FF1)
            c2 = _load_rows(w_hbm, w2_v, sems.at[1], O_FF2)
            c1.wait()
            c2.wait()
            acc_ref[...] = jnp.zeros_like(acc_ref)

        h2 = h2_ref[...]
        for k in range(N_CHIPS):
            ra = jnp.maximum(_mm(h2, w1_v[k]), 0.0)
            ra_scr[:, k * D:(k + 1) * D] = ra
            z_ref[:, k * D:(k + 1) * D] = (ra * ra).astype(BF16)
        y2 = _mm(z_ref[:, 0:D], w2_v[0])
        for k in range(1, N_CHIPS):
            y2 = y2 + _mm(z_ref[:, k * D:(k + 1) * D], w2_v[k])
        ry = lax.rsqrt(jnp.mean(y2 * y2, axis=-1, keepdims=True) + EPS)
        yn = y2 * ry
        n = yn * post_ref[...]
        g2 = mod_ref[:, 5 * D:6 * D]
        x2 = x2_ref[...]
        err = x2 + g2 * n - t_ref[...]
        acc_ref[5:6, :] += _rowsum(err * err) * (0.5 / D)
        dout = err * (1.0 / D)
        acc_ref[0:1, :] += _rowsum(dout * n)
        dn = dout * g2
        acc_ref[1:2, :] += _rowsum(dn * yn)
        dyn = dn * post_ref[...]
        dy2 = (ry * (dyn - yn * jnp.mean(dyn * yn, axis=-1, keepdims=True))).astype(BF16)
        dy2_ref[...] = dy2
        for k in range(N_CHIPS):
            dz = _mm(dy2, w2_v[k], NT)
            da_ref[:, k * D:(k + 1) * D] = (dz * (2.0 * ra_scr[:, k * D:(k + 1) * D])).astype(BF16)
        dh2 = jnp.zeros((TM, D), F32)
        for k in range(N_CHIPS):
            dh2 = dh2 + _mm(da_ref[:, k * D:(k + 1) * D], w1_v[k], NT)
        r2 = lax.rsqrt(jnp.mean(x2 * x2, axis=-1, keepdims=True) + EPS)
        xn = x2 * r2
        yv = xn * pre_ref[...]
        acc_ref[2:3, :] += _rowsum(dh2)
        acc_ref[3:4, :] += _rowsum(dh2 * yv)
        dyv = dh2 * (1.0 + mod_ref[:, 4 * D:5 * D])
        acc_ref[4:5, :] += _rowsum(dyv * xn)
        dxn = dyv * pre_ref[...]
        dx2_ref[...] = dout + r2 * (dxn - xn * jnp.mean(dxn * xn, axis=-1, keepdims=True))

    tile = BS((TM, D), lambda i: (i, 0))
    wide = BS((TM, D_FF), lambda i: (i, 0))
    vec = BS((1, D), lambda i: (0, 0))
    return pl.pallas_call(
        body, name="ffn_fwd_bwd", grid=(S // TM,),
        out_shape=(SDS((S, D_FF), BF16), SDS((S, D_FF), BF16), SDS((S, D), BF16), SDS((S, D), F32),
                   SDS((8, D), F32)),
        in_specs=[tile, tile, tile, BS((1, 6 * D), lambda i: (0, 0)), vec, vec, BS(memory_space=pl.ANY)],
        out_specs=(wide, wide, tile, tile, BS((8, D), lambda i: (0, 0))),
        scratch_shapes=[pltpu.VMEM((N_CHIPS, R_FF, D), BF16), pltpu.VMEM((N_CHIPS, R_FF, D), BF16),
                        pltpu.VMEM((TM, D_FF), F32),
                        pltpu.SemaphoreType.DMA((2,))],
        compiler_params=_params("arbitrary"),
    )(h2, x2, target, mod, post_cm, pre_cm, wg)


def _merge_bwd_call(dx2, y, ya, yb, p, mod, post_tm, wg, g):
    S = y.shape[0]

    def body(dx2_ref, y_ref, ya_ref, yb_ref, ga_ref, gb_ref, mod_ref, post_ref, wa_ref, wb_ref, wo_ref, g_ref,
             dy_ref, dya_ref, dyb_ref, doa_ref, dcb_ref, dpg_ref, acc_ref, bsum_ref, hr_ref, send_sems, recv_sems):
        start, finish = _halves_exchange(g_ref, hr_ref, send_sems, recv_sems)

        @pl.when(pl.program_id(0) == 0)
        def _():
            start()
            acc_ref[...] = jnp.zeros_like(acc_ref)
            bsum_ref[...] = jnp.zeros_like(bsum_ref)

        y = y_ref[...]
        ry = lax.rsqrt(jnp.mean(y * y, axis=-1, keepdims=True) + EPS)
        yn = y * ry
        dx2 = dx2_ref[...]
        acc_ref[0:1, :] += _rowsum(dx2 * (yn * post_ref[...]))
        dn = dx2 * mod_ref[:, 2 * D:3 * D]
        acc_ref[1:2, :] += _rowsum(dn * yn)
        dyn = dn * post_ref[...]
        dy = (ry * (dyn - yn * jnp.mean(dyn * yn, axis=-1, keepdims=True))).astype(BF16)
        dy_ref[...] = dy
        dmg = _mm_rows_t(dy, wo_ref)
        sa, sb = _sig(ga_ref[...]), _sig(gb_ref[...])
        dya = (dmg * sa).astype(BF16)
        dyb = (dmg * sb).astype(BF16)
        dya_ref[...] = dya
        dyb_ref[...] = dyb
        dga = dmg * ya_ref[...].astype(F32) * (sa * (1.0 - sa))
        dgb = dmg * yb_ref[...].astype(F32) * (sb * (1.0 - sb))
        dpg_ref[:, 0:D] = dga.astype(BF16)
        dpg_ref[:, D:2 * D] = dgb.astype(BF16)
        bsum_ref[:, 0:D] += _rowsum(dga)
        bsum_ref[:, D:2 * D] += _rowsum(dgb)
        doa_ref[...] = _mm_rows_t(dya, wa_ref)
        dcb_ref[...] = _mm_rows_t(dyb, wb_ref)

        @pl.when(pl.program_id(0) == S // TM - 1)
        def _():
            finish()

    tile = BS((TM, D), lambda i: (i, 0))
    vec = BS((1, D), lambda i: (0, 0))
    return pl.pallas_call(
        body, name="merge_bwd", grid=(S // TM,),
        out_shape=(SDS((S, D), BF16), SDS((S, D), BF16), SDS((S, D), BF16), SDS((S, D), F32), SDS((S, D), F32),
                   SDS((S, 2 * D), BF16), SDS((8, D), F32), SDS((1, 2 * D), F32),
                   SDS((N_CHIPS,) + g.shape[2:], g.dtype)),
        in_specs=[tile, tile, tile, tile, BS((TM, D), lambda i: (i, 6)), BS((TM, D), lambda i: (i, 7)),
                  BS((1, 6 * D), lambda i: (0, 0)), vec, _br_spec(O_BRA), _br_spec(O_BRB), _br_spec(O_OUT),
                  BS(memory_space=pl.ANY)],
        out_specs=(tile, tile, tile, tile, tile, BS((TM, 2 * D), lambda i: (i, 0)),
                   BS((8, D), lambda i: (0, 0)), BS((1, 2 * D), lambda i: (0, 0)), BS(memory_space=pl.ANY)),
        scratch_shapes=_halves_sems(),
        compiler_params=_params("arbitrary"),
    )(dx2, y, ya, yb, p, p, mod, post_tm, wg, wg, wg, g)


def _hgrn_bwd_call(p, o, doa, st, logits, gn, part, g):
    S = p.shape[0]
    nb = S // TB
    ncb = TB // CHUNK

    def body(q_ref, f_ref, v_ref, og_ref, o_ref, doa_ref, st_ref, lg_ref, gn_ref, part_ref, g_ref,
             dp_ref, bsum_ref, dlg_ref, dgn_ref, recv_ref, hr_ref,
             dst_scr, dlb_scr, dqe_s, dqt_s, dkt_s, dkd_s, dv_s, dog_s, dble_s, send_sems, recv_sems, hs, hr):
        i = pl.program_id(0)
        start, finish = _chip_exchange(part_ref, recv_ref, send_sems, recv_sems)
        start_h, finish_h = _halves_exchange(g_ref, hr_ref, hs, hr)

        @pl.when(i == 0)
        def _():
            start_h()
            start()
            dst_scr[...] = jnp.zeros_like(dst_scr)
            dlb_scr[...] = jnp.zeros_like(dlb_scr)
            bsum_ref[...] = jnp.zeros_like(bsum_ref)
            dgn_ref[...] = jnp.zeros_like(dgn_ref)

        lb = _lower_bound(lg_ref)
        tril, triu = _tri_masks()

        def chunk(tt, carry):
            ci = ncb - 1 - tt
            rows = pl.ds(pl.multiple_of(ci * CHUNK, CHUNK), CHUNK)
            q_r, f_r = q_ref[rows, :], f_ref[rows, :]
            t = _hg_gates(q_r, f_r, lb, tril)
            v = v_ref[rows, :]
            for h in range(HEADS):
                sl = slice(h * DK, (h + 1) * DK)
                stp = st_ref[ci, :, sl]
                stb = stp.astype(BF16)
                qeb = t["qe"][:, sl].astype(BF16)
                qtb = t["qt"][:, sl].astype(BF16)
                ktb = t["kt"][:, sl].astype(BF16)
                kdb = t["kd"][:, sl].astype(BF16)
                vb = v[:, sl].astype(BF16)
                a = jnp.where(tril > 0.5, _mm(qtb, ktb, NT), 0.0)
                o_h = o_ref[rows, sl]
                rinv = lax.rsqrt(jnp.mean(o_h * o_h, axis=-1, keepdims=True) + EPS)
                oh = o_h * rinv
                og = og_ref[rows, sl]
                so = _sig(og)
                d_oa = doa_ref[rows, sl]
                don = d_oa * (og * so)
                dog_s[:, sl] = d_oa * (oh * gn_ref[:, sl]) * _dsilu(og, so)
                dgn_ref[:, sl] += _rowsum(don * oh)
                doh = don * gn_ref[:, sl]
                do = (rinv * (doh - oh * jnp.mean(doh * oh, axis=-1, keepdims=True))).astype(BF16)
                dqe_s[:, sl] = _mm(do, stb, NN)
                dstp = _mm(do, qeb, TN)
                dab = jnp.where(tril > 0.5, _mm(do, vb, NT), 0.0).astype(BF16)
                dqt_s[:, sl] = _mm(dab, ktb, NN)
                dkt_s[:, sl] = _mm(dab, qtb, TN)
                dstn = dst_scr[:, sl]
                dsb = dstn.astype(BF16)
                dkd_s[:, sl] = _mm(vb, dsb, NN)
                dv_s[:, sl] = _mm(a.astype(BF16), do, TN) + _mm(kdb, dsb, NT)
                el = t["elast"][:, sl]
                dst_scr[:, sl] = dstn * el + dstp
                dble_s[:, sl] = el * _rowsum(stp * dstn)
            dqe, dqt, dkt, dkd = dqe_s[...], dqt_s[...], dkt_s[...], dkd_s[...]
            dq = dqe * t["e"] + dqt * t["eq"]
            dk = dkt * t["ek"] + dkd * t["dd"]
            dkk = dkd * t["kd"]
            qt_r = t["qt"].astype(BF16).astype(F32)
            kt_r = t["kt"].astype(BF16).astype(F32)
            dbv = dqe * t["qe"] + dqt * qt_r - dkt * kt_r - dkk
            dg = _cumsum_mm(triu, dbv) + (_rowsum(dkk) + dble_s[...])
            df = dg / t["f"] - dk
            sf = t["sf"]
            dlb_scr[...] += _rowsum(df * (1.0 - sf))
            dqr = dq * _dsilu(q_r, t["sq"])
            dfr = df * (1.0 - lb) * (sf * (1.0 - sf))
            dvv, dog = dv_s[...], dog_s[...]
            dp_ref[rows, 0:D] = dqr.astype(BF16)
            dp_ref[rows, D:2 * D] = dfr.astype(BF16)
            dp_ref[rows, 2 * D:3 * D] = dvv.astype(BF16)
            dp_ref[rows, 3 * D:4 * D] = dog.astype(BF16)
            bsum_ref[:, 0:D] += _rowsum(dqr)
            bsum_ref[:, D:2 * D] += _rowsum(dfr)
            bsum_ref[:, 2 * D:3 * D] += _rowsum(dvv)
            bsum_ref[:, 3 * D:4 * D] += _rowsum(dog)
            return carry

        lax.fori_loop(0, ncb, chunk, 0)

        dl = dlb_scr[...] * lb * (1.0 - lb)
        dlg_ref[0:1, :] = dl
        dlg_ref[1:2, :] = -dl

        @pl.when(i == nb - 1)
        def _():
            finish_h()
            finish()

    col = lambda j: BS((TB, D), lambda i, j=j: (nb - 1 - i, j))
    rev = BS((TB, D), lambda i: (nb - 1 - i, 0))
    cd = pltpu.VMEM((CHUNK, D), F32)
    return pl.pallas_call(
        body, name="hgrn_bwd", grid=(nb,),
        out_shape=(SDS((S, 4 * D), BF16), SDS((1, 4 * D), F32), SDS((2, D), F32), SDS((1, D), F32),
                   SDS((3,) + part.shape[1:], part.dtype), SDS((N_CHIPS,) + g.shape[2:], g.dtype)),
        in_specs=[col(0), col(1), col(2), col(3), rev, rev, BS((ncb, DK, D), lambda i: (nb - 1 - i, 0, 0)),
                  BS((2, D), lambda i: (0, 0)), BS((1, D), lambda i: (0, 0)), BS(memory_space=pl.ANY),
                  BS(memory_space=pl.ANY)],
        out_specs=(BS((TB, 4 * D), lambda i: (nb - 1 - i, 0)), BS((1, 4 * D), lambda i: (0, 0)),
                   BS((2, D), lambda i: (0, 0)), BS((1, D), lambda i: (0, 0)), BS(memory_space=pl.ANY),
                   BS(memory_space=pl.ANY)),
        scratch_shapes=[pltpu.VMEM((DK, D), F32), pltpu.VMEM((1, D), F32), cd, cd, cd, cd, cd, cd,
                        pltpu.VMEM((1, D), F32)] + _exchange_sems() + _halves_sems(),
        compiler_params=_params("arbitrary"),
    )(p, p, p, p, o, doa, st, logits, gn, part, g)


def _conv_bwd_call(dcb, uc, u, p, dw, ln_g, ln_b, part):
    S = uc.shape[0]
    nb = S // TM
    hb = TM // HALO

    def body(dcb_ref, uc_ref, u_ref, uh_ref, cv_ref, cg_ref, dw_ref, g_ref, b_ref, part_ref,
             dp_ref, bsum_ref, ddw_ref, acc_ref, recv_ref, uext, dext, ush, dsh, send_sems, recv_sems):
        i = pl.program_id(0)
        start, finish = _chip_exchange(part_ref, recv_ref, send_sems, recv_sems)

        @pl.when(i == 0)
        def _():
            start()
            dext[TM:EXT, :] = jnp.zeros((EXT - TM, D), F32)
            uext[HALO + TM:EXT, :] = jnp.zeros((EXT - HALO - TM, D), F32)
            bsum_ref[...] = jnp.zeros_like(bsum_ref)
            ddw_ref[...] = jnp.zeros_like(ddw_ref)
            acc_ref[...] = jnp.zeros_like(acc_ref)

        first_tile = (nb - 1 - i) == 0
        uext[0:HALO, :] = jnp.where(first_tile, 0.0, uh_ref[...])
        uext[HALO:HALO + TM, :] = u_ref[...]
        _fill_shifted(uext, ush)

        for rb in range(TM // SUB):
            rs_ = slice(rb * SUB, (rb + 1) * SUB)
            xh, rs = _layernorm_stats(uc_ref[rs_, :])
            ln = xh * g_ref[...] + b_ref[...]
            dln = dcb_ref[rs_, :] * _dsilu(ln, _sig(ln))
            acc_ref[1:2, :] += _rowsum(dln * xh)
            acc_ref[2:3, :] += _rowsum(dln)
            dxh = dln * g_ref[...]
            duc = rs * (dxh - jnp.mean(dxh, axis=-1, keepdims=True)
                        - xh * jnp.mean(dxh * xh, axis=-1, keepdims=True))
            dext[rs_, :] = duc
            acc_ref[0:1, :] += _rowsum(duc)
        _fill_shifted(dext, dsh)

        for j in range(CONV_K):
            part = jnp.zeros((SUB, D), F32)
            for rb in range(TM // SUB):
                s0 = HALO - (CONV_K - 1) + j + rb * SUB
                part = part + dext[rb * SUB:(rb + 1) * SUB, :] * _window(uext, ush, s0, SUB)
            ddw_ref[j:j + 1, :] += _rowsum(part)

        for rb in range(TM // SUB):
            rs_ = slice(rb * SUB, (rb + 1) * SUB)
            du = jnp.zeros((SUB, D), F32)
            for j in range(CONV_K):
                s0 = rb * SUB + (CONV_K - 1) - j
                du = du + dw_ref[j:j + 1, :] * _window(dext, dsh, s0, SUB)
            cg = cg_ref[rs_, :]
            sg = _sig(cg)
            dcv = du * sg
            dcg = du * cv_ref[rs_, :] * (sg * (1.0 - sg))
            dp_ref[rs_, 0:D] = dcv.astype(BF16)
            dp_ref[rs_, D:2 * D] = dcg.astype(BF16)
            bsum_ref[:, 0:D] += _rowsum(dcv)
            bsum_ref[:, D:2 * D] += _rowsum(dcg)

        dext[TM:TM + HALO, :] = dext[0:HALO, :]

        @pl.when(i == nb - 1)
        def _():
            finish()

    rev = BS((TM, D), lambda i: (nb - 1 - i, 0))
    vec = BS((1, D), lambda i: (0, 0))
    return pl.pallas_call(
        body, name="conv_bwd", grid=(nb,),
        out_shape=(SDS((S, 2 * D), BF16), SDS((1, 2 * D), F32), SDS((32, D), F32), SDS((8, D), F32),
                   SDS((3,) + part.shape[1:], part.dtype)),
        in_specs=[rev, rev, rev, BS((HALO, D), lambda i: (jnp.maximum((nb - 1 - i) * hb - 1, 0), 0)),
                  BS((TM, D), lambda i: (nb - 1 - i, 4)), BS((TM, D), lambda i: (nb - 1 - i, 5)),
                  BS((CONV_K, D), lambda i: (0, 0)), vec, vec, BS(memory_space=pl.ANY)],
        out_specs=(BS((TM, 2 * D), lambda i: (nb - 1 - i, 0)), BS((1, 2 * D), lambda i: (0, 0)),
                   BS((32, D), lambda i: (0, 0)), BS((8, D), lambda i: (0, 0)), BS(memory_space=pl.ANY)),
        scratch_shapes=[pltpu.VMEM((EXT, D), F32), pltpu.VMEM((EXT, D), F32),
                        pltpu.VMEM((7, HALO + TM, D), F32), pltpu.VMEM((7, HALO + TM, D), F32)] + _exchange_sems(),
        compiler_params=_params("arbitrary"),
    )(dcb, uc, u, u, p, p, dw, ln_g, ln_b, part)


def _in_bwd_call(dp_hg, dp_cv, dp_gt, x, dx2, mod, pre_tm, wg, part, full_a, full_b):
    S = x.shape[0]
    tm = 2 * TM

    def body(hg_ref, cv_ref, gt_ref, x_ref, dx2_ref, mod_ref, g_ref, w_hbm, part_ref, fa_in, fb_in,
             gx_ref, acc_ref, recv_ref, fa_out, fb_out, w_vmem, sem, send_sems, recv_sems, sa, ra, sb, rb):
        start, finish = _chip_exchange(part_ref, recv_ref, send_sems, recv_sems)
        start_a, finish_a = _join_exchange(fa_in, fa_out, sa, ra)
        start_b, finish_b = _join_exchange(fb_in, fb_out, sb, rb)

        @pl.when(pl.program_id(0) == 0)
        def _():
            start_a()
            start_b()
            start()
            _load_rows(w_hbm, w_vmem, sem, O_IN).wait()
            acc_ref[...] = jnp.zeros_like(acc_ref)

        dh = jnp.zeros((tm, D), F32)
        for k in range(IN_COLS // D):
            src, kk = ((hg_ref, k), (cv_ref, k - 4), (gt_ref, k - 6))[0 if k < 4 else (1 if k < 6 else 2)]
            dh = dh + _mm(src[:, kk * D:(kk + 1) * D], w_vmem[k // 2, (k % 2) * D:(k % 2 + 1) * D, :], NT)
        xv = x_ref[...]
        r = lax.rsqrt(jnp.mean(xv * xv, axis=-1, keepdims=True) + EPS)
        xn = xv * r
        yv = xn * g_ref[...]
        acc_ref[0:1, :] += _rowsum(dh)
        acc_ref[1:2, :] += _rowsum(dh * yv)
        dyv = dh * (1.0 + mod_ref[:, D:2 * D])
        acc_ref[2:3, :] += _rowsum(dyv * xn)
        dxn = dyv * g_ref[...]
        gx_ref[...] = dx2_ref[...] + r * (dxn - xn * jnp.mean(dxn * xn, axis=-1, keepdims=True))

        @pl.when(pl.program_id(0) == S // tm - 1)
        def _():
            finish_a()
            finish_b()
            finish()

    tile = BS((tm, D), lambda i: (i, 0))
    hbm = BS(memory_space=pl.ANY)
    return pl.pallas_call(
        body, name="in_bwd", grid=(S // tm,),
        out_shape=(SDS((S, D), F32), SDS((8, D), F32), SDS((3,) + part.shape[1:], part.dtype),
                   SDS(full_a.shape, full_a.dtype), SDS(full_b.shape, full_b.dtype)),
        in_specs=[BS((tm, 4 * D), lambda i: (i, 0)), BS((tm, 2 * D), lambda i: (i, 0)),
                  BS((tm, 2 * D), lambda i: (i, 0)), tile, tile, BS((1, 6 * D), lambda i: (0, 0)),
                  BS((1, D), lambda i: (0, 0)), hbm, hbm, hbm, hbm],
        out_specs=(tile, BS((8, D), lambda i: (0, 0)), hbm, hbm, hbm),
        scratch_shapes=[pltpu.VMEM((N_CHIPS, R_IN, D), BF16), pltpu.SemaphoreType.DMA] + _exchange_sems()
        + _join_sems() + _join_sems(),
        input_output_aliases={9: 3, 10: 4},
        compiler_params=_params("arbitrary"),
    )(dp_hg, dp_cv, dp_gt, x, dx2, mod, pre_tm, wg, part, full_a, full_b)


def _wgrad_call(gp, a, b, name, bm, place, rows):
    S, M = a.shape
    N = b.shape[1]
    bk = min(S, 1024)
    nk = S // bk

    def body(a_ref, b_ref, *rest):
        o_ref, acc = rest[-2], rest[-1]
        k = pl.program_id(2)

        @pl.when(k == 0)
        def _():
            acc[...] = jnp.zeros_like(acc)

        acc[...] += _mm(a_ref[...], b_ref[...], TN)

        @pl.when(k == nk - 1)
        def _():
            o_ref[...] = acc[...].astype(BF16)

    in_specs = [BS((bk, bm), lambda i, j, k: (k, i)), BS((bk, D), lambda i, j, k: (k, j))]
    args = [a, b]
    if gp is not None:
        in_specs.append(BS(memory_space=pl.ANY))
        args.append(gp)
    return pl.pallas_call(
        body, name=name, grid=(M // bm, N // D, nk),
        out_shape=SDS((N_CHIPS, rows, D), BF16),
        in_specs=in_specs,
        out_specs=BS((None, bm, D), lambda i, j, k: (*place(i, j), 0)),
        scratch_shapes=[pltpu.VMEM((bm, D), F32)],
        input_output_aliases={} if gp is None else {2: 0},
        compiler_params=_params("parallel", "parallel", "arbitrary"),
    )(*args)


def _wgrad_rows_call(gp, a, b, name, blk):
    S = a.shape[0]
    bk = min(S, 1024)
    nk = S // bk

    def body(a_ref, b_ref, *rest):
        o_ref, acc = rest[-2], rest[-1]
        k = pl.program_id(0)

        @pl.when(k == 0)
        def _():
            acc[...] = jnp.zeros_like(acc)

        acc[...] += _mm(a_ref[...], b_ref[...], TN)

        @pl.when(k == nk - 1)
        def _():
            for c in range(N_CHIPS):
                o_ref[c] = acc[c * R_BR:(c + 1) * R_BR, :].astype(BF16)

    in_specs = [BS((bk, D), lambda k: (k, 0)), BS((bk, D), lambda k: (k, 0))]
    args = [a, b]
    if gp is not None:
        in_specs.append(BS(memory_space=pl.ANY))
        args.append(gp)
    return pl.pallas_call(
        body, name=name, grid=(nk,),
        out_shape=SDS((N_CHIPS, 3 * R_BR, D), BF16),
        in_specs=in_specs,
        out_specs=BS((N_CHIPS, R_BR, D), lambda k: (0, blk, 0)),
        scratch_shapes=[pltpu.VMEM((D, D), F32)],
        input_output_aliases={} if gp is None else {2: 0},
        compiler_params=_params("arbitrary"),
    )(*args)


def _outer_call(cact, dmod):
    n = dmod.shape[1]

    def body(a_ref, b_ref, o_ref):
        o_ref[...] = _mm(a_ref[...], b_ref[...], TN, HI)

    return pl.pallas_call(
        body, name="wgrad_ada", out_shape=SDS((D, n), F32),
        compiler_params=pltpu.CompilerParams(vmem_limit_bytes=VMEM_LIMIT),
    )(cact, dmod)


def _adamw_call(w, g, m, v, name):
    R, C = w.shape
    tr = R
    while tr * C > 512 * 1024 and tr % 16 == 0:
        tr //= 2
    c1 = 1.0 - ADAM_B1 ** ADAM_STEP
    c2 = 1.0 - ADAM_B2 ** ADAM_STEP

    def body(w_ref, g_ref, m_ref, v_ref, d_ref, m2_ref, v2_ref):
        g = g_ref[...]
        m2 = ADAM_B1 * m_ref[...] + (1.0 - ADAM_B1) * g
        v2 = ADAM_B2 * v_ref[...] + (1.0 - ADAM_B2) * (g * g)
        m2_ref[...] = m2
        v2_ref[...] = v2
        d_ref[...] = -ADAM_LR * ((m2 / c1) / (jnp.sqrt(v2 / c2) + ADAM_EPS) + ADAM_WD * w_ref[...])

    tile = BS((tr, C), lambda i: (i, 0))
    return pl.pallas_call(
        body, name=name, grid=(R // tr,), out_shape=(SDS((R, C), F32),) * 3,
        in_specs=[tile] * 4, out_specs=(tile,) * 3, compiler_params=_params("parallel"),
    )(w, g, m, v)


def _adamw_gather_call(w, g, m, v, srows, name):
    R, C = w.shape
    tr = R
    while tr * C > 512 * 1024 and tr % 16 == 0:
        tr //= 2
    nsteps = R // tr
    mr = srows.shape[0]
    c1 = 1.0 - ADAM_B1 ** ADAM_STEP
    c2 = 1.0 - ADAM_B2 ** ADAM_STEP

    def body(w_ref, g_ref, m_ref, v_ref, s_ref, d_ref, m2_ref, v2_ref, all_ref, sum_ref,
             x_scr, out_scr, send_sems, recv_sems, local_sem):
        i = pl.program_id(0)
        start, finish = _allgather_parts(x_scr, out_scr, send_sems, recv_sems, local_sem)

        @pl.when(i == 0)
        def _():
            x_scr[...] = s_ref[...]
            start()

        g = g_ref[...]
        m2 = ADAM_B1 * m_ref[...] + (1.0 - ADAM_B1) * g
        v2 = ADAM_B2 * v_ref[...] + (1.0 - ADAM_B2) * (g * g)
        m2_ref[...] = m2
        v2_ref[...] = v2
        d_ref[...] = -ADAM_LR * ((m2 / c1) / (jnp.sqrt(v2 / c2) + ADAM_EPS) + ADAM_WD * w_ref[...])

        @pl.when(i == nsteps - 1)
        def _():
            finish()
            all_ref[...] = out_scr[...]
            acc = out_scr[0:mr, :]
            for d in range(1, N_DEV):
                acc = acc + out_scr[d * mr:(d + 1) * mr, :]
            sum_ref[...] = acc

    tile = BS((tr, C), lambda i: (i, 0))
    return pl.pallas_call(
        body, name=name, grid=(nsteps,),
        out_shape=(SDS((R, C), F32),) * 3 + (SDS((N_DEV * mr, D), F32), SDS((mr, D), F32)),
        in_specs=[tile] * 4 + [BS((mr, D), lambda i: (0, 0))],
        out_specs=(tile,) * 3 + (BS((N_DEV * mr, D), lambda i: (0, 0)), BS((mr, D), lambda i: (0, 0))),
        scratch_shapes=[pltpu.VMEM((mr, D), F32), pltpu.VMEM((N_DEV * mr, D), F32)] + _allgather_sems(),
        compiler_params=_params("arbitrary"),
    )(w, g, m, v, srows)


def _rs_begin(g, c_idx, tag):
    n = g.shape[1]
    g = g.reshape(N_CHIPS, 2, n // 2, D)
    return _add_halves_call(g, _sibling_halves_call(g, tag), c_idx, tag)


def _rs_end(part, recv, c_idx, chip_idx, tag):
    n = 2 * part.shape[1]
    full = _add_chips_call(part, recv, jnp.concatenate([chip_idx, c_idx]), tag)
    return _sibling_join_call(full, tag).reshape(n, D)


def _local_step(x, mod, cact, target, wg, pack, small, c_idx, chip_idx):
    p, h1, wg = _fwd_in_call(x, mod, small["pre_tm"], wg, small["b_in"], pack, small["order"])
    o, oa, st, wg = _hgrn_fwd_call(p, small["logits"], small["hg_norm"], wg, pack)
    u, uc, cb, wg = _conv_fwd_call(p, small["conv_dw"], small["conv_db"], small["ln_g"], small["ln_b"], wg, pack)
    ya, yb, mg, y, x2, h2 = _merge_fwd_call(oa, cb, p, x, mod, small["post_tm"], small["pre_cm"], wg)
    z, da, dy2, dx2, acc_f = _ffn_call(h2, x2, target, mod, small["post_cm"], small["pre_cm"], wg)

    g_ff = _wgrad_call(None, h2, da, "wgrad_ff1", D, lambda i, j: (j, 0), 2 * R_FF)
    g_ff = _wgrad_call(g_ff, z, dy2, "wgrad_ff2", D, lambda i, j: (i, 1), 2 * R_FF)
    g_ff = g_ff.reshape(N_CHIPS, 2, R_FF, D)
    dy, dya, dyb, doa, dcb, dp_gt, acc_m, bs_gt, hr_ff = _merge_bwd_call(dx2, y, ya, yb, p, mod, small["post_tm"],
                                                                        wg, g_ff)
    part_ff = _add_halves_call(g_ff, hr_ff, c_idx, "ff")

    g_br = _wgrad_rows_call(None, oa, dya, "wgrad_br_a", 0)
    g_br = _wgrad_rows_call(g_br, cb, dyb, "wgrad_br_b", 1)
    g_br = _wgrad_rows_call(g_br, mg, dy, "wgrad_out", 2)
    g_br = g_br.reshape(N_CHIPS, 2, 3 * R_BR // 2, D)
    dp_hg, bs_hg, dlg, dgn, recv_ff, hr_br = _hgrn_bwd_call(p, o, doa, st, small["logits"], small["hg_norm"],
                                                            part_ff, g_br)
    part_br = _add_halves_call(g_br, hr_br, c_idx, "br")
    dp_cv, bs_cv, ddw, acc_c, recv_br = _conv_bwd_call(dcb, uc, u, p, small["conv_dw"], small["ln_g"], small["ln_b"],
                                                        part_br)

    g_in = _wgrad_call(None, h1, dp_hg, "wgrad_in_hg", D, lambda i, j: (j // 2, j % 2), R_IN)
    g_in = _wgrad_call(g_in, h1, dp_cv, "wgrad_in_cv", D, lambda i, j: (2, j), R_IN)
    g_in = _wgrad_call(g_in, h1, dp_gt, "wgrad_in_gt", D, lambda i, j: (3, j), R_IN)
    part_in = _rs_begin(g_in, c_idx, "in")
    chip_c = jnp.concatenate([chip_idx, c_idx])
    full_ff = _add_chips_call(part_ff, recv_ff, chip_c, "ff")
    full_br = _add_chips_call(part_br, recv_br, chip_c, "br")
    gx, acc_i, recv_in, full_ff, full_br = _in_bwd_call(dp_hg, dp_cv, dp_gt, x, dx2, mod, small["pre_tm"], wg,
                                                        part_in, full_ff, full_br)
    red_ff = full_ff.reshape(2 * R_FF, D)
    red_br = full_br.reshape(3 * R_BR, D)
    red_in = _rs_end(part_in, recv_in, c_idx, chip_idx, "in")

    zrow = jnp.zeros((1, D), F32)
    rows = [acc_i[0:1], acc_i[1:2], acc_m[0:1], acc_f[2:3], acc_f[3:4], acc_f[0:1],
            acc_i[2:3], acc_m[1:2], acc_f[4:5], acc_f[1:2],
            jnp.concatenate([bs_hg, bs_cv, bs_gt], axis=1).reshape(8, D),
            dlg, dgn, acc_c[0:1], acc_c[1:2], acc_c[2:3],
            ddw,
            cact, acc_f[5:6]] + [zrow] * 6
    return gx, jnp.concatenate(rows, axis=0), red_in, red_ff, red_br


def kernel(x, c, w_ada, b_ada, pre_norm_tm, post_norm_tm, pre_norm_cm, post_norm_cm, w_in, b_in, hg_lb_logits, hg_norm, conv_dw, conv_db, conv_ln_g, conv_ln_b, w_br_a, w_br_b, w_out, w_ff1, w_ff2, loss_target, m_w_ada, m_b_ada, m_pre_norm_tm, m_post_norm_tm, m_pre_norm_cm, m_post_norm_cm, m_w_in, m_b_in, m_hg_lb_logits, m_hg_norm, m_conv_dw, m_conv_db, m_conv_ln_g, m_conv_ln_b, m_w_br_a, m_w_br_b, m_w_out, m_w_ff1, m_w_ff2, v_w_ada, v_b_ada, v_pre_norm_tm, v_post_norm_tm, v_pre_norm_cm, v_post_norm_cm, v_w_in, v_b_in, v_hg_lb_logits, v_hg_norm, v_conv_dw, v_conv_db, v_conv_ln_g, v_conv_ln_b, v_w_br_a, v_w_br_b, v_w_out, v_w_ff1, v_w_ff2):
    xi, yi, ci = lax.axis_index("x"), lax.axis_index("y"), lax.axis_index("c")
    chip = 2 * xi + yi
    c_idx = jnp.reshape(ci, (1,)).astype(jnp.int32)
    chip_idx = jnp.reshape(chip, (1,)).astype(jnp.int32)

    def pack_small(ada_b, pre_t, post_t, pre_c, post_c, in_b, lg, hgn, cdb, lng, lnb, cdw):
        flat = jnp.concatenate([cdw[0].reshape(-1), jnp.zeros((8 * D - CONV_K * 256,), F32)]).reshape(8, D)
        return jnp.concatenate([ada_b.reshape(6, D), pre_t, post_t, pre_c, post_c, in_b.reshape(8, D), lg, hgn,
                                cdb, lng, lnb, flat], axis=0)

    w_in_halves = w_in[0].reshape(D, 2, D).transpose(1, 0, 2).reshape(R_IN, D)
    pack = jnp.concatenate([w_in_halves, w_ff1[0], w_ff2[0], w_br_a[0], w_br_b[0], w_out[0]],
                           axis=0).astype(BF16)
    wg = lax.dynamic_update_slice(lax.empty((N_CHIPS, PACK_W, D), BF16), pack[None], (chip, 0, 0))
    wa = 6 * D // N_CHIPS
    me = 4 * xi + 2 * yi + ci
    dw_blk = jnp.concatenate([conv_dw[0].reshape(-1), jnp.zeros((8 * D - CONV_K * 256,), F32)]).reshape(8, D)
    dw_all, ca_all, mod_all = _prologue_call(
        dw_blk, jnp.broadcast_to(c, (8, D)), w_ada[0].astype(BF16),
        lax.dynamic_slice_in_dim(b_ada, chip * wa, wa, axis=1))
    order = jnp.stack([chip, 2 * (1 - xi) + yi, 2 * xi + (1 - yi), 2 * (1 - xi) + (1 - yi)]).astype(jnp.int32)
    dw_all = dw_all.reshape(N_CHIPS, 2, 8 * D)[:, 0, :CONV_K * 256].reshape(N_CHIPS, CONV_K, 256)
    dw_full = dw_all.transpose(1, 0, 2).reshape(CONV_K, D)
    cact = lax.dynamic_slice_in_dim(ca_all, me * 8, 1, axis=0)
    mod_mine = lax.dynamic_index_in_dim(mod_all.reshape(N_CHIPS, 2, N_DEV, 8, wa)[:, 0, :, 0, :], me, axis=1,
                                        keepdims=False)
    mod = mod_mine.reshape(1, 6 * D)

    small = dict(pre_tm=pre_norm_tm, post_tm=post_norm_tm, pre_cm=pre_norm_cm, post_cm=post_norm_cm,
                 b_in=b_in, logits=hg_lb_logits, hg_norm=hg_norm, conv_dw=dw_full, conv_db=conv_db,
                 ln_g=conv_ln_g, ln_b=conv_ln_b, order=order)

    gx, srows, red_in, red_ff, red_br = _local_step(x[0], mod, cact, loss_target[0], wg, pack, small, c_idx,
                                                    chip_idx)

    shapes = {"in": w_in.shape, "br_a": w_br_a.shape, "br_b": w_br_b.shape, "out": w_out.shape,
              "ff1": w_ff1.shape, "ff2": w_ff2.shape}
    offs = {"in": (red_in, 0, R_IN), "ff1": (red_ff, 0, R_FF), "ff2": (red_ff, R_FF, 2 * R_FF),
            "br_a": (red_br, 0, R_BR), "br_b": (red_br, R_BR, 2 * R_BR), "out": (red_br, 2 * R_BR, 3 * R_BR)}
    wmv = {"in": (w_in, m_w_in, v_w_in), "br_a": (w_br_a, m_w_br_a, v_w_br_a), "br_b": (w_br_b, m_w_br_b, v_w_br_b),
           "out": (w_out, m_w_out, v_w_out), "ff1": (w_ff1, m_w_ff1, v_w_ff1), "ff2": (w_ff2, m_w_ff2, v_w_ff2)}
    res = {}
    for n in offs:
        shp = shapes[n]
        g2d = offs[n][0][offs[n][1]:offs[n][2]]
        if n == "in":
            g2d = g2d.reshape(2, D, D).transpose(1, 0, 2)
        g2d = g2d.reshape(shp[1], shp[2])
        w_, m_, v_ = (a[0] for a in wmv[n])
        if n == "in":
            d_, m2_, v2_, sall, ssum = _adamw_gather_call(w_, g2d, m_, v_, srows, "adamw_in")
        else:
            d_, m2_, v2_ = _adamw_call(w_, g2d, m_, v_, "adamw_" + n)
        res[n] = tuple(a.reshape(shp) for a in (g2d, d_, m2_, v2_))

    sall = sall.reshape(N_DEV, SMALL_ROWS, D)
    loss = jnp.sum(ssum[57])
    dmod_all = sall[:, 0:6, :].reshape(N_DEV, 6 * D)
    g_ada = _outer_call(sall[:, 56, :], lax.dynamic_slice_in_dim(dmod_all, chip * wa, wa, axis=1))
    g_dw = lax.dynamic_slice_in_dim(ssum[24:24 + CONV_K], chip * 256, 256, axis=1)
    g_small = jnp.concatenate(
        [ssum[0:24], jnp.concatenate([g_dw.reshape(-1), jnp.zeros((8 * D - CONV_K * 256,), F32)]).reshape(8, D)],
        axis=0)
    d_, m2_, v2_ = _adamw_call(w_ada[0], g_ada, m_w_ada[0], v_w_ada[0], "adamw_ada")
    res["ada"] = tuple(a.reshape(w_ada.shape) for a in (g_ada, d_, m2_, v2_))

    ws = pack_small(b_ada, pre_norm_tm, post_norm_tm, pre_norm_cm, post_norm_cm, b_in, hg_lb_logits, hg_norm,
                    conv_db, conv_ln_g, conv_ln_b, conv_dw)
    ms = pack_small(m_b_ada, m_pre_norm_tm, m_post_norm_tm, m_pre_norm_cm, m_post_norm_cm, m_b_in, m_hg_lb_logits,
                    m_hg_norm, m_conv_db, m_conv_ln_g, m_conv_ln_b, m_conv_dw)
    vs = pack_small(v_b_ada, v_pre_norm_tm, v_post_norm_tm, v_pre_norm_cm, v_post_norm_cm, v_b_in, v_hg_lb_logits,
                    v_hg_norm, v_conv_db, v_conv_ln_g, v_conv_ln_b, v_conv_dw)
    sres = (g_small,) + tuple(_adamw_call(ws, g_small, ms, vs, "adamw_small"))

    def unpack_small(t):
        return {"b_ada": t[0:6].reshape(1, 6 * D), "pre_tm": t[6:7], "post_tm": t[7:8], "pre_cm": t[8:9],
                "post_cm": t[9:10], "b_in": t[10:18].reshape(1, IN_COLS), "logits": t[18:20], "hg_norm": t[20:21],
                "conv_db": t[21:22], "ln_g": t[22:23], "ln_b": t[23:24],
                "conv_dw": t[24:32].reshape(-1)[:CONV_K * 256].reshape(1, CONV_K, 256)}

    order = ["ada", "b_ada", "pre_tm", "post_tm", "pre_cm", "post_cm", "in", "b_in", "logits", "hg_norm", "conv_dw",
             "conv_db", "ln_g", "ln_b", "br_a", "br_b", "out", "ff1", "ff2"]
    outs = [loss, gx.reshape(x.shape)]
    for kind in range(4):
        sm = unpack_small(sres[kind])
        for n in order:
            outs.append(res[n][kind] if n in res else sm[n])
    return tuple(outs)
```

```python
import jax
import jax.numpy as jnp
from jax import lax
from jax.experimental import pallas as pl
from jax.experimental.pallas import tpu as pltpu

F32, BF16 = jnp.float32, jnp.bfloat16
SDS = jax.ShapeDtypeStruct
BS = pl.BlockSpec
MESH = pl.DeviceIdType.MESH
HI = lax.Precision.HIGHEST

D = 1024
D_FF = 4096
IN_COLS = 8192
HEADS, DK = 8, 128
CHUNK = 128
CONV_K = 31
HALO = 32
SUB = 32
EPS = 1e-6
N_CHIPS, N_DEV = 4, 8
TM = 256
TB = 256
VMEM_LIMIT = 56 * 1024 * 1024

R_IN, R_BR, R_FF = 2048, 256, 1024
PACK_W = R_IN + 3 * R_BR + 2 * R_FF
O_IN, O_FF1, O_FF2, O_BRA, O_BRB, O_OUT = 0, 2048, 3072, 4096, 4352, 4608
SMALL_ROWS = 64

ADAM_LR, ADAM_B1, ADAM_B2, ADAM_EPS, ADAM_WD, ADAM_STEP = 0.001, 0.9, 0.999, 1e-08, 0.01, 10

NN = (((1,), (0,)), ((), ()))
NT = (((1,), (1,)), ((), ()))
TN = (((0,), (0,)), ((), ()))


def _mm(a, b, dims=NN, precision=None):
    return lax.dot_general(a, b, dims, preferred_element_type=F32, precision=precision)


def _sig(v):
    return jax.nn.sigmoid(v)


def _dsilu(v, s):
    return s * (1.0 + v * (1.0 - s))


def _params(*sem):
    return pltpu.CompilerParams(dimension_semantics=sem if sem else None, vmem_limit_bytes=VMEM_LIMIT)


def _rowsum(v):
    return jnp.sum(v, axis=0, keepdims=True)


def _mesh_pos():
    return lax.axis_index("x"), lax.axis_index("y"), lax.axis_index("c")


def _allgather_parts(x_ref, out_ref, send_sems, recv_sems, local_sem):
    m_per = x_ref.shape[0]
    x, y, c = _mesh_pos()
    me, sibling = (x, y, c), (x, y, 1 - c)
    chips = [(1 - x, y), (x, 1 - y), (1 - x, 1 - y)]

    def rows(px, py, pc):
        return out_ref.at[pl.ds((4 * px + 2 * py + pc) * m_per, m_per), :]

    def copy(k, block, to, src=None):
        return pltpu.make_async_remote_copy(
            src_ref=rows(*block) if src is None else src, dst_ref=rows(*block),
            send_sem=send_sems.at[k], recv_sem=recv_sems.at[k], device_id=to, device_id_type=MESH)

    def first():
        return [copy(0, me, sibling, src=x_ref)] + [copy(1 + j, me, (*chip, c), src=x_ref)
                                                    for j, chip in enumerate(chips)]

    def start():
        pltpu.make_async_copy(x_ref, rows(*me), local_sem).start()
        for cp in first():
            cp.start()

    def finish():
        passed = [copy(4 + j, (*chip, c), sibling) for j, chip in enumerate(chips)]
        for j, chip in enumerate(chips):
            copy(1 + j, (*chip, c), me).wait_recv()
            passed[j].start()
        copy(0, sibling, me).wait_recv()
        for j, chip in enumerate(chips):
            copy(4 + j, (*chip, 1 - c), me).wait_recv()
        for cp in first() + passed:
            cp.wait_send()
        pltpu.make_async_copy(x_ref, rows(*me), local_sem).wait()

    return start, finish


def _allgather_sems():
    return [pltpu.SemaphoreType.DMA((7,)), pltpu.SemaphoreType.DMA((7,)), pltpu.SemaphoreType.DMA]
def _gather_sems(n_ranges):
    return [pltpu.SemaphoreType.DMA((6 * n_ranges,)), pltpu.SemaphoreType.DMA((6 * n_ranges,))]


def _pack_gather(pack_ref, wg_ref, send_sems, recv_sems, ranges):
    x, y, c = _mesh_pos()
    me, sibling = (x, y, c), (x, y, 1 - c)
    chips = [(1 - x, y), (x, 1 - y), (1 - x, 1 - y)]

    def land(r, px, py, pc):
        off, n = ranges[r]
        return wg_ref.at[2 * px + py, pl.ds(off + pc * (n // 2), n // 2), :]

    def mine(r):
        off, n = ranges[r]
        return pack_ref.at[pl.ds(off + c * (n // 2), n // 2), :]

    def copy(r, k, block, to, src=None):
        return pltpu.make_async_remote_copy(
            src_ref=land(r, *block) if src is None else src, dst_ref=land(r, *block),
            send_sem=send_sems.at[6 * r + k], recv_sem=recv_sems.at[6 * r + k], device_id=to, device_id_type=MESH)

    def start():
        for r in range(len(ranges)):
            for j, chip in enumerate(chips):
                copy(r, j, me, (*chip, c), src=mine(r)).start()

    def finish():
        for r in range(len(ranges)):
            for j, chip in enumerate(chips):
                copy(r, j, (*chip, c), me).wait_recv()
                copy(r, 3 + j, (*chip, c), sibling).start()
        for r in range(len(ranges)):
            for j, chip in enumerate(chips):
                copy(r, 3 + j, (*chip, 1 - c), me).wait_recv()
        for r in range(len(ranges)):
            for j, chip in enumerate(chips):
                copy(r, j, me, (*chip, c), src=mine(r)).wait_send()
                copy(r, 3 + j, (*chip, c), sibling).wait_send()

    return start, finish


def _relay_sems():
    return [pltpu.SemaphoreType.DMA((8,)), pltpu.SemaphoreType.DMA((8,))]


def _relay_gather(pack_ref, wg_ref, send_sems, recv_sems, off, n):
    x, y, c = _mesh_pos()
    me, sibling = (x, y, c), (x, y, 1 - c)
    chips = [(1 - x, y), (x, 1 - y), (1 - x, 1 - y)]
    h, q = n // 2, n // 4

    def land(px, py, pc, piece=None):
        if piece is None:
            return wg_ref.at[2 * px + py, pl.ds(off + pc * h, h), :]
        return wg_ref.at[2 * px + py, pl.ds(off + pc * h + piece * q, q), :]

    def copy(k, ref, to, src=None):
        return pltpu.make_async_remote_copy(
            src_ref=ref if src is None else src, dst_ref=ref, send_sem=send_sems.at[k], recv_sem=recv_sems.at[k],
            device_id=to, device_id_type=MESH)

    def direct(j):
        return copy(j, land(x, y, c), (*chips[j], c), src=pack_ref.at[pl.ds(off + c * h, h), :])

    def relayed(j):
        if j == 0:
            return copy(6, land(*chips[0], c, 1), (x, 1 - y, c))
        return copy(7, land(*chips[1], c, 0), (1 - x, y, c))

    def start():
        direct(0).start()
        direct(1).start()

    def arrive(j):
        if j == 0:
            for k in range(2):
                copy(k, land(*chips[k], c), me).wait_recv()
                relayed(k).start()
                copy(3 + k, land(*chips[k], c), sibling).start()
        if j == 2:
            copy(7, land(*chips[2], c, 0), me).wait_recv()
            copy(6, land(*chips[2], c, 1), me).wait_recv()
            copy(5, land(*chips[2], c), sibling).start()
        copy(3 + j, land(*chips[j], 1 - c), me).wait_recv()

    def drain():
        for j in range(2):
            direct(j).wait_send()
            relayed(j).wait_send()
        for j in range(3):
            copy(3 + j, land(*chips[j], c), sibling).wait_send()

    return start, arrive, drain


def _prologue_call(dw_blk, c_blk, w_ada, b_ada):
    wa = w_ada.shape[1]

    def body(dw_ref, c_ref, wa_ref, ba_ref, dwg_ref, ca_ref, modg_ref,
             cg_scr, part_scr, s1, r1, l1, s2, r2, l2, s3, r3, l3):
        start_c, finish_c = _allgather_parts(c_ref, cg_scr, s2, r2, l2)
        start_dw, finish_dw = _allgather_parts(dw_ref, dwg_ref, s1, r1, l1)
        start_mod, finish_mod = _allgather_parts(part_scr, modg_ref, s3, r3, l3)
        start_c()
        start_dw()
        finish_c()
        cv = cg_scr[...]
        ca = cv * _sig(cv)
        ca_ref[...] = ca
        part_scr[...] = _mm(ca.astype(BF16), wa_ref[...]) + ba_ref[...]
        start_mod()
        finish_dw()
        finish_mod()

    vm = BS(memory_space=pltpu.VMEM)
    return pl.pallas_call(
        body, name="prologue_adaln_conv_dw",
        out_shape=(SDS((N_DEV * 8, D), F32), SDS((N_DEV * 8, D), F32), SDS((N_DEV * N_DEV * 8, wa), F32)),
        in_specs=[vm, vm, vm, vm], out_specs=(vm, vm, vm),
        scratch_shapes=[pltpu.VMEM((N_DEV * 8, D), F32), pltpu.VMEM((N_DEV * 8, wa), F32)]
        + _allgather_sems() + _allgather_sems() + _allgather_sems(),
        compiler_params=pltpu.CompilerParams(vmem_limit_bytes=VMEM_LIMIT),
    )(dw_blk, c_blk, w_ada, b_ada)


def _halves_exchange(g_ref, out_ref, send_sems, recv_sems):
    x, y, c = _mesh_pos()

    def copies():
        return [pltpu.make_async_remote_copy(
            src_ref=g_ref.at[k, 1 - c], dst_ref=out_ref.at[k], send_sem=send_sems.at[k], recv_sem=recv_sems.at[k],
            device_id=(x, y, 1 - c), device_id_type=MESH) for k in range(N_CHIPS)]

    def start():
        for cp in copies():
            cp.start()

    def finish():
        for cp in copies():
            cp.wait()

    return start, finish


def _halves_sems():
    return [pltpu.SemaphoreType.DMA((N_CHIPS,)), pltpu.SemaphoreType.DMA((N_CHIPS,))]


def _sibling_halves_call(g, tag):
    _, _, h, n = g.shape

    def body(g_ref, out_ref, send_sems, recv_sems):
        start, finish = _halves_exchange(g_ref, out_ref, send_sems, recv_sems)
        start()
        finish()

    return pl.pallas_call(
        body, name="rs_sibling_halves_" + tag, out_shape=SDS((N_CHIPS, h, n), g.dtype),
        in_specs=[BS(memory_space=pl.ANY)], out_specs=BS(memory_space=pl.ANY),
        scratch_shapes=_halves_sems(),
    )(g)


def _chip_exchange(p_ref, out_ref, send_sems, recv_sems):
    x, y, c = _mesh_pos()
    chips = [(1 - x, y), (x, 1 - y), (1 - x, 1 - y)]

    def copies():
        return [pltpu.make_async_remote_copy(
            src_ref=p_ref.at[2 * cx + cy], dst_ref=out_ref.at[j], send_sem=send_sems.at[j], recv_sem=recv_sems.at[j],
            device_id=(cx, cy, c), device_id_type=MESH) for j, (cx, cy) in enumerate(chips)]

    def start():
        for cp in copies():
            cp.start()

    def finish():
        for cp in copies():
            cp.wait()

    return start, finish


def _exchange_sems():
    return [pltpu.SemaphoreType.DMA((3,)), pltpu.SemaphoreType.DMA((3,))]


def _join_exchange(in_ref, out_ref, send_sems, recv_sems):
    h = in_ref.shape[1]
    q = h // 4
    x, y, c = _mesh_pos()

    def copy(k, half):
        return pltpu.make_async_remote_copy(
            src_ref=in_ref.at[half, pl.ds(k * q, q)], dst_ref=out_ref.at[half, pl.ds(k * q, q)],
            send_sem=send_sems.at[k], recv_sem=recv_sems.at[k],
            device_id=(x, y, 1 - c), device_id_type=MESH)

    def start():
        for k in range(4):
            copy(k, c).start()

    def finish():
        for k in range(4):
            copy(k, c).wait_send()
            copy(k, 1 - c).wait_recv()

    return start, finish


def _join_sems():
    return [pltpu.SemaphoreType.DMA((4,)), pltpu.SemaphoreType.DMA((4,))]


def _sibling_join_call(full, tag):
    def body(in_ref, out_ref, send_sems, recv_sems):
        start, finish = _join_exchange(in_ref, out_ref, send_sems, recv_sems)
        start()
        finish()

    return pl.pallas_call(
        body, name="rs_sibling_join_" + tag, out_shape=SDS(full.shape, full.dtype),
        in_specs=[BS(memory_space=pl.ANY)], out_specs=BS(memory_space=pl.ANY),
        scratch_shapes=_join_sems(), input_output_aliases={0: 0},
    )(full)


def _add_halves_call(g, recv, c_idx, tag):
    _, _, h, n = g.shape
    tr = h // 2

    def body(c_ref, g_ref, r_ref, o_ref):
        o_ref[...] = (g_ref[...].astype(F32) + r_ref[...].astype(F32)).astype(BF16)

    return pl.pallas_call(
        body, name="rs_add_halves_" + tag, out_shape=SDS((N_CHIPS, h, n), BF16),
        grid_spec=pltpu.PrefetchScalarGridSpec(
            num_scalar_prefetch=1, grid=(N_CHIPS, 2),
            in_specs=[BS((None, None, tr, n), lambda k, r, c_ref: (k, c_ref[0], r, 0)),
                      BS((None, tr, n), lambda k, r, c_ref: (k, r, 0))],
            out_specs=BS((None, tr, n), lambda k, r, c_ref: (k, r, 0))),
        compiler_params=_params("arbitrary", "arbitrary"),
    )(c_idx, g, recv)


def _add_chips_call(p, recv, chip_c_idx, tag):
    _, h, n = p.shape
    tr = h // 2

    def body(k_ref, p_ref, r_ref, o_ref):
        acc = p_ref[...].astype(F32)
        for j in range(3):
            acc = acc + r_ref[j].astype(F32)
        o_ref[...] = acc

    return pl.pallas_call(
        body, name="rs_add_chips_" + tag, out_shape=SDS((2, h, n), F32),
        grid_spec=pltpu.PrefetchScalarGridSpec(
            num_scalar_prefetch=1, grid=(2,),
            in_specs=[BS((None, tr, n), lambda r, k_ref: (k_ref[0], r, 0)),
                      BS((3, tr, n), lambda r, k_ref: (0, r, 0))],
            out_specs=BS((None, tr, n), lambda r, k_ref: (k_ref[1], r, 0))),
        compiler_params=_params("arbitrary"),
    )(chip_c_idx, p, recv)


def _load_rows(wg_hbm, w_vmem, sem, off):
    cp = pltpu.make_async_copy(wg_hbm.at[:, pl.ds(off, w_vmem.shape[1]), :], w_vmem, sem)
    cp.start()
    return cp


def _fwd_in_call(x, mod, pre_tm, wg, b_in, pack, order):
    S = x.shape[0]
    tmf = 2 * TM
    nt = S // tmf
    wc = IN_COLS // N_CHIPS

    def body(ord_ref, x_ref, mod_ref, g_ref, w_hbm, b_ref, pack_ref, p_ref, h_hbm, wg_out, w_vmem, h_scr, sems,
             send_sems, recv_sems, send_sems2, recv_sems2):
        q, i = pl.program_id(0), pl.program_id(1)
        rows = pl.ds(pl.multiple_of(i * tmf, tmf), tmf)
        start, arrive, drain = _relay_gather(pack_ref, wg_out, send_sems, recv_sems, O_IN, R_IN)
        start2, finish2 = _pack_gather(pack_ref, wg_out, send_sems2, recv_sems2, [(O_OUT, R_BR)])

        def weights(phase):
            return pltpu.make_async_copy(wg_out.at[ord_ref[phase], pl.ds(O_IN, R_IN), :], w_vmem.at[phase % 2],
                                         sems.at[phase % 2])

        @pl.when((q == 0) & (i == 0))
        def _():
            start()
            weights(0).start()
            weights(0).wait()

        @pl.when((q == 1) & (i == 0))
        def _():
            arrive(0)
            start2()
            weights(1).start()
            weights(1).wait()
            arrive(1)
            weights(2).start()

        @pl.when((q == 2) & (i == 0))
        def _():
            weights(2).wait()
            arrive(2)
            weights(3).start()

        @pl.when((q == 3) & (i == 0))
        def _():
            weights(3).wait()

        @pl.when(q == 0)
        def _():
            xv = x_ref[...]
            r = lax.rsqrt(jnp.mean(xv * xv, axis=-1, keepdims=True) + EPS)
            h = xv * r * g_ref[...] * (1.0 + mod_ref[:, D:2 * D]) + mod_ref[:, 0:D]
            h_scr[rows, :] = h.astype(BF16)

        hb = h_scr[rows, :]
        slot = q % 2
        for k in range(wc // D):
            p_ref[:, k * D:(k + 1) * D] = _mm(hb, w_vmem[slot, k * D:(k + 1) * D, :]) + b_ref[:, k * D:(k + 1) * D]

        @pl.when((q == N_CHIPS - 1) & (i == nt - 1))
        def _():
            cp = pltpu.make_async_copy(h_scr, h_hbm, sems.at[0])
            cp.start()
            drain()
            finish2()
            cp.wait()

    hbm = BS(memory_space=pl.ANY)
    return pl.pallas_call(
        body, name="fwd_in", out_shape=(SDS((S, IN_COLS), F32), SDS((S, D), BF16), SDS(wg.shape, wg.dtype)),
        grid_spec=pltpu.PrefetchScalarGridSpec(
            num_scalar_prefetch=1, grid=(N_CHIPS, nt),
            in_specs=[BS((tmf, D), lambda q, i, o: (jnp.where(q == 0, i, nt - 1), 0)),
                      BS((1, 6 * D), lambda q, i, o: (0, 0)),
                      BS((1, D), lambda q, i, o: (0, 0)), hbm, BS((1, wc), lambda q, i, o: (0, o[q])), hbm],
            out_specs=(BS((tmf, wc), lambda q, i, o: (i, o[q])), hbm, hbm),
            scratch_shapes=[pltpu.VMEM((2, R_IN, D), BF16), pltpu.VMEM((S, D), BF16), pltpu.SemaphoreType.DMA((2,))]
            + _relay_sems() + _gather_sems(1)),
        input_output_aliases={4: 2},
        compiler_params=_params("arbitrary", "arbitrary"),
    )(order, x, mod, pre_tm, wg, b_in, pack)


def _lower_bound(lg_ref):
    l0, l1 = lg_ref[0:1, :], lg_ref[1:2, :]
    mx = jnp.maximum(l0, l1)
    e0, e1 = jnp.exp(l0 - mx), jnp.exp(l1 - mx)
    return e0 / (e0 + e1)


def _tri_masks():
    ri = lax.broadcasted_iota(jnp.int32, (CHUNK, CHUNK), 0)
    ci = lax.broadcasted_iota(jnp.int32, (CHUNK, CHUNK), 1)
    return (ri >= ci).astype(F32), (ci >= ri).astype(F32)


def _cumsum_mm(tri, g):
    tb = tri.astype(BF16)
    hi = g.astype(BF16)
    r1 = g - hi.astype(F32)
    mid = r1.astype(BF16)
    lo = (r1 - mid.astype(F32)).astype(BF16)
    return _mm(tb, hi) + _mm(tb, mid) + _mm(tb, lo)


def _hg_gates(q_r, f_r, lb, tril):
    sq = _sig(q_r)
    q = q_r * sq
    sf = _sig(f_r)
    f = lb + (1.0 - lb) * sf
    k = 1.0 - f
    g = jnp.log(f)
    b = _cumsum_mm(tril, g)
    b_last = _rowsum(g)
    row = lax.broadcasted_iota(jnp.int32, g.shape, 0)
    ref = _rowsum(jnp.where(row < CHUNK // 2, g, 0.0))
    e = jnp.exp(b)
    eq = jnp.exp(jnp.minimum(b - ref, 80.0))
    ek = jnp.exp(jnp.minimum(ref - b, 80.0))
    dd = jnp.exp(b_last - b)
    return dict(sq=sq, q=q, sf=sf, f=f, k=k, e=e, eq=eq, ek=ek, dd=dd, elast=jnp.exp(b_last),
                qe=q * e, qt=q * eq, kt=k * ek, kd=k * dd)


def _hgrn_fwd_call(p, logits, gn, wg, pack):
    S = p.shape[0]
    ncb = TB // CHUNK
    ranges = [(O_FF1, R_FF)]

    def body(q_ref, f_ref, v_ref, og_ref, lg_ref, gn_ref, wg_in, pack_ref, o_ref, oa_ref, st_ref, wg_out,
             st_scr, send_sems, recv_sems):
        start, finish = _pack_gather(pack_ref, wg_out, send_sems, recv_sems, ranges)

        @pl.when(pl.program_id(0) == 0)
        def _():
            start()
            st_scr[...] = jnp.zeros_like(st_scr)

        lb = _lower_bound(lg_ref)
        tril, _ = _tri_masks()

        def chunk(ci, carry):
            rows = pl.ds(pl.multiple_of(ci * CHUNK, CHUNK), CHUNK)
            st_ref[ci] = st_scr[...]
            t = _hg_gates(q_ref[rows, :], f_ref[rows, :], lb, tril)
            v = v_ref[rows, :]
            for h in range(HEADS):
                sl = slice(h * DK, (h + 1) * DK)
                stp = st_scr[:, sl]
                vb = v[:, sl].astype(BF16)
                inter = _mm(t["qe"][:, sl].astype(BF16), stp.astype(BF16), NT)
                a = jnp.where(tril > 0.5, _mm(t["qt"][:, sl].astype(BF16), t["kt"][:, sl].astype(BF16), NT), 0.0)
                o = inter + _mm(a.astype(BF16), vb)
                st_scr[:, sl] = stp * t["elast"][:, sl] + _mm(vb, t["kd"][:, sl].astype(BF16), TN)
                oh = o * lax.rsqrt(jnp.mean(o * o, axis=-1, keepdims=True) + EPS)
                og = og_ref[rows, sl]
                o_ref[rows, sl] = o
                oa_ref[rows, sl] = (oh * gn_ref[:, sl] * (og * _sig(og))).astype(BF16)
            return carry

        lax.fori_loop(0, ncb, chunk, 0)

        @pl.when(pl.program_id(0) == S // TB - 1)
        def _():
            finish()

    col = lambda j: BS((TB, D), lambda i, j=j: (i, j))
    hbm = BS(memory_space=pl.ANY)
    return pl.pallas_call(
        body, name="hgrn_fwd", grid=(S // TB,),
        out_shape=(SDS((S, D), F32), SDS((S, D), BF16), SDS((S // CHUNK, DK, D), F32), SDS(wg.shape, wg.dtype)),
        in_specs=[col(0), col(1), col(2), col(3), BS((2, D), lambda i: (0, 0)), BS((1, D), lambda i: (0, 0)),
                  hbm, hbm],
        out_specs=(BS((TB, D), lambda i: (i, 0)), BS((TB, D), lambda i: (i, 0)),
                   BS((ncb, DK, D), lambda i: (i, 0, 0)), hbm),
        scratch_shapes=[pltpu.VMEM((DK, D), F32)] + _gather_sems(len(ranges)),
        input_output_aliases={6: 3},
        compiler_params=_params("arbitrary"),
    )(p, p, p, p, logits, gn, wg, pack)


def _layernorm_stats(uc):
    mu = jnp.mean(uc, axis=-1, keepdims=True)
    xc = uc - mu
    rs = lax.rsqrt(jnp.mean(xc * xc, axis=-1, keepdims=True) + EPS)
    return xc * rs, rs


EXT = HALO + TM + 8


def _fill_shifted(ext, shifted):
    for m in range(1, 8):
        shifted[m - 1] = ext[m:m + HALO + TM, :]


def _window(ext, shifted, s0, n):
    m = s0 % 8
    q = s0 - m
    return ext[q:q + n, :] if m == 0 else shifted[m - 1, q:q + n, :]


def _conv_fwd_call(p, dw, db, ln_g, ln_b, wg, pack):
    S = p.shape[0]
    ranges = [(O_FF2, R_FF), (O_BRA, 2 * R_BR)]

    def body(cv_ref, cg_ref, dw_ref, db_ref, g_ref, b_ref, wg_in, pack_ref, u_ref, uc_ref, cb_ref, wg_out,
             uext, ush, send_sems, recv_sems):
        start, finish = _pack_gather(pack_ref, wg_out, send_sems, recv_sems, ranges)

        @pl.when(pl.program_id(0) == 0)
        def _():
            start()
            uext[0:HALO, :] = jnp.zeros((HALO, D), F32)
            uext[HALO + TM:EXT, :] = jnp.zeros((EXT - HALO - TM, D), F32)

        u = cv_ref[...] * _sig(cg_ref[...])
        uext[HALO:HALO + TM, :] = u
        u_ref[...] = u
        _fill_shifted(uext, ush)
        for rb in range(TM // SUB):
            acc = jnp.broadcast_to(db_ref[...], (SUB, D))
            for j in range(CONV_K):
                s0 = HALO - (CONV_K - 1) + j + rb * SUB
                acc = acc + dw_ref[j:j + 1, :] * _window(uext, ush, s0, SUB)
            uc_ref[rb * SUB:(rb + 1) * SUB, :] = acc
            xh, _ = _layernorm_stats(acc)
            ln = xh * g_ref[...] + b_ref[...]
            cb_ref[rb * SUB:(rb + 1) * SUB, :] = (ln * _sig(ln)).astype(BF16)
        uext[0:HALO, :] = uext[TM:TM + HALO, :]

        @pl.when(pl.program_id(0) == S // TM - 1)
        def _():
            finish()

    vec = BS((1, D), lambda i: (0, 0))
    hbm = BS(memory_space=pl.ANY)
    return pl.pallas_call(
        body, name="conv_fwd", grid=(S // TM,),
        out_shape=(SDS((S, D), F32), SDS((S, D), F32), SDS((S, D), BF16), SDS(wg.shape, wg.dtype)),
        in_specs=[BS((TM, D), lambda i: (i, 4)), BS((TM, D), lambda i: (i, 5)),
                  BS((CONV_K, D), lambda i: (0, 0)), vec, vec, vec, hbm, hbm],
        out_specs=(BS((TM, D), lambda i: (i, 0)),) * 3 + (hbm,),
        scratch_shapes=[pltpu.VMEM((EXT, D), F32), pltpu.VMEM((7, HALO + TM, D), F32)] + _gather_sems(len(ranges)),
        input_output_aliases={6: 3},
        compiler_params=_params("arbitrary"),
    )(p, p, dw, db, ln_g, ln_b, wg, pack)


def _mm_rows(a, w_ref):
    acc = _mm(a[:, 0:R_BR], w_ref[0])
    for k in range(1, N_CHIPS):
        acc = acc + _mm(a[:, k * R_BR:(k + 1) * R_BR], w_ref[k])
    return acc


def _mm_rows_t(a, w_ref):
    return jnp.concatenate([_mm(a, w_ref[k], NT) for k in range(N_CHIPS)], axis=1)


def _br_spec(off):
    return BS((N_CHIPS, R_BR, D), lambda i: (0, off // R_BR, 0))


def _merge_fwd_call(oa, cb, p, x, mod, post_tm, pre_cm, wg):
    S = x.shape[0]

    def body(oa_ref, cb_ref, ga_ref, gb_ref, x_ref, mod_ref, post_ref, pre_ref, wa_ref, wb_ref, wo_ref,
             ya_ref, yb_ref, mg_ref, y_ref, x2_ref, h2_ref):
        ya = _mm_rows(oa_ref[...], wa_ref)
        yb = _mm_rows(cb_ref[...], wb_ref)
        ya_ref[...] = ya.astype(BF16)
        yb_ref[...] = yb.astype(BF16)
        mg = (_sig(ga_ref[...]) * ya + _sig(gb_ref[...]) * yb).astype(BF16)
        mg_ref[...] = mg
        y = _mm_rows(mg, wo_ref)
        y_ref[...] = y
        n = y * lax.rsqrt(jnp.mean(y * y, axis=-1, keepdims=True) + EPS) * post_ref[...]
        x2 = x_ref[...] + mod_ref[:, 2 * D:3 * D] * n
        x2_ref[...] = x2
        r2 = lax.rsqrt(jnp.mean(x2 * x2, axis=-1, keepdims=True) + EPS)
        h2 = x2 * r2 * pre_ref[...] * (1.0 + mod_ref[:, 4 * D:5 * D]) + mod_ref[:, 3 * D:4 * D]
        h2_ref[...] = h2.astype(BF16)

    tile = BS((TM, D), lambda i: (i, 0))
    vec = BS((1, D), lambda i: (0, 0))
    return pl.pallas_call(
        body, name="merge_fwd", grid=(S // TM,),
        out_shape=(SDS((S, D), BF16), SDS((S, D), BF16), SDS((S, D), BF16), SDS((S, D), F32), SDS((S, D), F32),
                   SDS((S, D), BF16)),
        in_specs=[tile, tile, BS((TM, D), lambda i: (i, 6)), BS((TM, D), lambda i: (i, 7)), tile,
                  BS((1, 6 * D), lambda i: (0, 0)), vec, vec, _br_spec(O_BRA), _br_spec(O_BRB), _br_spec(O_OUT)],
        out_specs=(tile,) * 6,
        compiler_params=_params("arbitrary"),
    )(oa, cb, p, p, x, mod, post_tm, pre_cm, wg, wg, wg)


def _ffn_call(h2, x2, target, mod, post_cm, pre_cm, wg):
    S = x2.shape[0]

    def body(h2_ref, x2_ref, t_ref, mod_ref, post_ref, pre_ref, w_hbm,
             z_ref, da_ref, dy2_ref, dx2_ref, acc_ref, w1_v, w2_v, ra_scr, sems):
        @pl.when(pl.program_id(0) == 0)
        def _():
            c1 = _load_rows(w_hbm, w1_v, sems.at[0], O_FF1)
            c2 = _load_rows(w_hbm, w2_v, sems.at[1], O_FF2)
            c1.wait()
            c2.wait()
            acc_ref[...] = jnp.zeros_like(acc_ref)

        h2 = h2_ref[...]
        for k in range(N_CHIPS):
            ra = jnp.maximum(_mm(h2, w1_v[k]), 0.0)
            ra_scr[:, k * D:(k + 1) * D] = ra
            z_ref[:, k * D:(k + 1) * D] = (ra * ra).astype(BF16)
        y2 = _mm(z_ref[:, 0:D], w2_v[0])
        for k in range(1, N_CHIPS):
            y2 = y2 + _mm(z_ref[:, k * D:(k + 1) * D], w2_v[k])
        ry = lax.rsqrt(jnp.mean(y2 * y2, axis=-1, keepdims=True) + EPS)
        yn = y2 * ry
        n = yn * post_ref[...]
        g2 = mod_ref[:, 5 * D:6 * D]
        x2 = x2_ref[...]
        err = x2 + g2 * n - t_ref[...]
        acc_ref[5:6, :] += _rowsum(err * err) * (0.5 / D)
        dout = err * (1.0 / D)
        acc_ref[0:1, :] += _rowsum(dout * n)
        dn = dout * g2
        acc_ref[1:2, :] += _rowsum(dn * yn)
        dyn = dn * post_ref[...]
        dy2 = (ry * (dyn - yn * jnp.mean(dyn * yn, axis=-1, keepdims=True))).astype(BF16)
        dy2_ref[...] = dy2
        for k in range(N_CHIPS):
            dz = _mm(dy2, w2_v[k], NT)
            da_ref[:, k * D:(k + 1) * D] = (dz * (2.0 * ra_scr[:, k * D:(k + 1) * D])).astype(BF16)
        dh2 = jnp.zeros((TM, D), F32)
        for k in range(N_CHIPS):
            dh2 = dh2 + _mm(da_ref[:, k * D:(k + 1) * D], w1_v[k], NT)
        r2 = lax.rsqrt(jnp.mean(x2 * x2, axis=-1, keepdims=True) + EPS)
        xn = x2 * r2
        yv = xn * pre_ref[...]
        acc_ref[2:3, :] += _rowsum(dh2)
        acc_ref[3:4, :] += _rowsum(dh2 * yv)
        dyv = dh2 * (1.0 + mod_ref[:, 4 * D:5 * D])
        acc_ref[4:5, :] += _rowsum(dyv * xn)
        dxn = dyv * pre_ref[...]
        dx2_ref[...] = dout + r2 * (dxn - xn * jnp.mean(dxn * xn, axis=-1, keepdims=True))

    tile = BS((TM, D), lambda i: (i, 0))
    wide = BS((TM, D_FF), lambda i: (i, 0))
    vec = BS((1, D), lambda i: (0, 0))
    return pl.pallas_call(
        body, name="ffn_fwd_bwd", grid=(S // TM,),
        out_shape=(SDS((S, D_FF), BF16), SDS((S, D_FF), BF16), SDS((S, D), BF16), SDS((S, D), F32),
                   SDS((8, D), F32)),
        in_specs=[tile, tile, tile, BS((1, 6 * D), lambda i: (0, 0)), vec, vec, BS(memory_space=pl.ANY)],
        out_specs=(wide, wide, tile, tile, BS((8, D), lambda i: (0, 0))),
        scratch_shapes=[pltpu.VMEM((N_CHIPS, R_FF, D), BF16), pltpu.VMEM((N_CHIPS, R_FF, D), BF16),
                        pltpu.VMEM((TM, D_FF), F32),
                        pltpu.SemaphoreType.DMA((2,))],
        compiler_params=_params("arbitrary"),
    )(h2, x2, target, mod, post_cm, pre_cm, wg)


def _merge_bwd_call(dx2, y, ya, yb, p, mod, post_tm, wg, g):
    S = y.shape[0]

    def body(dx2_ref, y_ref, ya_ref, yb_ref, ga_ref, gb_ref, mod_ref, post_ref, wa_ref, wb_ref, wo_ref, g_ref,
             dy_ref, dya_ref, dyb_ref, doa_ref, dcb_ref, dpg_ref, acc_ref, bsum_ref, hr_ref, send_sems, recv_sems):
        start, finish = _halves_exchange(g_ref, hr_ref, send_sems, recv_sems)

        @pl.when(pl.program_id(0) == 0)
        def _():
            start()
            acc_ref[...] = jnp.zeros_like(acc_ref)
            bsum_ref[...] = jnp.zeros_like(bsum_ref)

        y = y_ref[...]
        ry = lax.rsqrt(jnp.mean(y * y, axis=-1, keepdims=True) + EPS)
        yn = y * ry
        dx2 = dx2_ref[...]
        acc_ref[0:1, :] += _rowsum(dx2 * (yn * post_ref[...]))
        dn = dx2 * mod_ref[:, 2 * D:3 * D]
        acc_ref[1:2, :] += _rowsum(dn * yn)
        dyn = dn * post_ref[...]
        dy = (ry * (dyn - yn * jnp.mean(dyn * yn, axis=-1, keepdims=True))).astype(BF16)
        dy_ref[...] = dy
        dmg = _mm_rows_t(dy, wo_ref)
        sa, sb = _sig(ga_ref[...]), _sig(gb_ref[...])
        dya = (dmg * sa).astype(BF16)
        dyb = (dmg * sb).astype(BF16)
        dya_ref[...] = dya
        dyb_ref[...] = dyb
        dga = dmg * ya_ref[...].astype(F32) * (sa * (1.0 - sa))
        dgb = dmg * yb_ref[...].astype(F32) * (sb * (1.0 - sb))
        dpg_ref[:, 0:D] = dga.astype(BF16)
        dpg_ref[:, D:2 * D] = dgb.astype(BF16)
        bsum_ref[:, 0:D] += _rowsum(dga)
        bsum_ref[:, D:2 * D] += _rowsum(dgb)
        doa_ref[...] = _mm_rows_t(dya, wa_ref)
        dcb_ref[...] = _mm_rows_t(dyb, wb_ref)

        @pl.when(pl.program_id(0) == S // TM - 1)
        def _():
            finish()

    tile = BS((TM, D), lambda i: (i, 0))
    vec = BS((1, D), lambda i: (0, 0))
    return pl.pallas_call(
        body, name="merge_bwd", grid=(S // TM,),
        out_shape=(SDS((S, D), BF16), SDS((S, D), BF16), SDS((S, D), BF16), SDS((S, D), F32), SDS((S, D), F32),
                   SDS((S, 2 * D), BF16), SDS((8, D), F32), SDS((1, 2 * D), F32),
                   SDS((N_CHIPS,) + g.shape[2:], g.dtype)),
        in_specs=[tile, tile, tile, tile, BS((TM, D), lambda i: (i, 6)), BS((TM, D), lambda i: (i, 7)),
                  BS((1, 6 * D), lambda i: (0, 0)), vec, _br_spec(O_BRA), _br_spec(O_BRB), _br_spec(O_OUT),
                  BS(memory_space=pl.ANY)],
        out_specs=(tile, tile, tile, tile, tile, BS((TM, 2 * D), lambda i: (i, 0)),
                   BS((8, D), lambda i: (0, 0)), BS((1, 2 * D), lambda i: (0, 0)), BS(memory_space=pl.ANY)),
        scratch_shapes=_halves_sems(),
        compiler_params=_params("arbitrary"),
    )(dx2, y, ya, yb, p, p, mod, post_tm, wg, wg, wg, g)


def _hgrn_bwd_call(p, o, doa, st, logits, gn, part, g):
    S = p.shape[0]
    nb = S // TB
    ncb = TB // CHUNK

    def body(q_ref, f_ref, v_ref, og_ref, o_ref, doa_ref, st_ref, lg_ref, gn_ref, part_ref, g_ref,
             dp_ref, bsum_ref, dlg_ref, dgn_ref, recv_ref, hr_ref,
             dst_scr, dlb_scr, dqe_s, dqt_s, dkt_s, dkd_s, dv_s, dog_s, dble_s, send_sems, recv_sems, hs, hr):
        i = pl.program_id(0)
        start, finish = _chip_exchange(part_ref, recv_ref, send_sems, recv_sems)
        start_h, finish_h = _halves_exchange(g_ref, hr_ref, hs, hr)

        @pl.when(i == 0)
        def _():
            start_h()
            start()
            dst_scr[...] = jnp.zeros_like(dst_scr)
            dlb_scr[...] = jnp.zeros_like(dlb_scr)
            bsum_ref[...] = jnp.zeros_like(bsum_ref)
            dgn_ref[...] = jnp.zeros_like(dgn_ref)

        lb = _lower_bound(lg_ref)
        tril, triu = _tri_masks()

        def chunk(tt, carry):
            ci = ncb - 1 - tt
            rows = pl.ds(pl.multiple_of(ci * CHUNK, CHUNK), CHUNK)
            q_r, f_r = q_ref[rows, :], f_ref[rows, :]
            t = _hg_gates(q_r, f_r, lb, tril)
            v = v_ref[rows, :]
            for h in range(HEADS):
                sl = slice(h * DK, (h + 1) * DK)
                stp = st_ref[ci, :, sl]
                stb = stp.astype(BF16)
                qeb = t["qe"][:, sl].astype(BF16)
                qtb = t["qt"][:, sl].astype(BF16)
                ktb = t["kt"][:, sl].astype(BF16)
                kdb = t["kd"][:, sl].astype(BF16)
                vb = v[:, sl].astype(BF16)
                a = jnp.where(tril > 0.5, _mm(qtb, ktb, NT), 0.0)
                o_h = o_ref[rows, sl]
                rinv = lax.rsqrt(jnp.mean(o_h * o_h, axis=-1, keepdims=True) + EPS)
                oh = o_h * rinv
                og = og_ref[rows, sl]
                so = _sig(og)
                d_oa = doa_ref[rows, sl]
                don = d_oa * (og * so)
                dog_s[:, sl] = d_oa * (oh * gn_ref[:, sl]) * _dsilu(og, so)
                dgn_ref[:, sl] += _rowsum(don * oh)
                doh = don * gn_ref[:, sl]
                do = (rinv * (doh - oh * jnp.mean(doh * oh, axis=-1, keepdims=True))).astype(BF16)
                dqe_s[:, sl] = _mm(do, stb, NN)
                dstp = _mm(do, qeb, TN)
                dab = jnp.where(tril > 0.5, _mm(do, vb, NT), 0.0).astype(BF16)
                dqt_s[:, sl] = _mm(dab, ktb, NN)
                dkt_s[:, sl] = _mm(dab, qtb, TN)
                dstn = dst_scr[:, sl]
                dsb = dstn.astype(BF16)
                dkd_s[:, sl] = _mm(vb, dsb, NN)
                dv_s[:, sl] = _mm(a.astype(BF16), do, TN) + _mm(kdb, dsb, NT)
                el = t["elast"][:, sl]
                dst_scr[:, sl] = dstn * el + dstp
                dble_s[:, sl] = el * _rowsum(stp * dstn)
            dqe, dqt, dkt, dkd = dqe_s[...], dqt_s[...], dkt_s[...], dkd_s[...]
            dq = dqe * t["e"] + dqt * t["eq"]
            dk = dkt * t["ek"] + dkd * t["dd"]
            dkk = dkd * t["kd"]
            qt_r = t["qt"].astype(BF16).astype(F32)
            kt_r = t["kt"].astype(BF16).astype(F32)
            dbv = dqe * t["qe"] + dqt * qt_r - dkt * kt_r - dkk
            dg = _cumsum_mm(triu, dbv) + (_rowsum(dkk) + dble_s[...])
            df = dg / t["f"] - dk
            sf = t["sf"]
            dlb_scr[...] += _rowsum(df * (1.0 - sf))
            dqr = dq * _dsilu(q_r, t["sq"])
            dfr = df * (1.0 - lb) * (sf * (1.0 - sf))
            dvv, dog = dv_s[...], dog_s[...]
            dp_ref[rows, 0:D] = dqr.astype(BF16)
            dp_ref[rows, D:2 * D] = dfr.astype(BF16)
            dp_ref[rows, 2 * D:3 * D] = dvv.astype(BF16)
            dp_ref[rows, 3 * D:4 * D] = dog.astype(BF16)
            bsum_ref[:, 0:D] += _rowsum(dqr)
            bsum_ref[:, D:2 * D] += _rowsum(dfr)
            bsum_ref[:, 2 * D:3 * D] += _rowsum(dvv)
            bsum_ref[:, 3 * D:4 * D] += _rowsum(dog)
            return carry

        lax.fori_loop(0, ncb, chunk, 0)

        dl = dlb_scr[...] * lb * (1.0 - lb)
        dlg_ref[0:1, :] = dl
        dlg_ref[1:2, :] = -dl

        @pl.when(i == nb - 1)
        def _():
            finish_h()
            finish()

    col = lambda j: BS((TB, D), lambda i, j=j: (nb - 1 - i, j))
    rev = BS((TB, D), lambda i: (nb - 1 - i, 0))
    cd = pltpu.VMEM((CHUNK, D), F32)
    return pl.pallas_call(
        body, name="hgrn_bwd", grid=(nb,),
        out_shape=(SDS((S, 4 * D), BF16), SDS((1, 4 * D), F32), SDS((2, D), F32), SDS((1, D), F32),
                   SDS((3,) + part.shape[1:], part.dtype), SDS((N_CHIPS,) + g.shape[2:], g.dtype)),
        in_specs=[col(0), col(1), col(2), col(3), rev, rev, BS((ncb, DK, D), lambda i: (nb - 1 - i, 0, 0)),
                  BS((2, D), lambda i: (0, 0)), BS((1, D), lambda i: (0, 0)), BS(memory_space=pl.ANY),
                  BS(memory_space=pl.ANY)],
        out_specs=(BS((TB, 4 * D), lambda i: (nb - 1 - i, 0)), BS((1, 4 * D), lambda i: (0, 0)),
                   BS((2, D), lambda i: (0, 0)), BS((1, D), lambda i: (0, 0)), BS(memory_space=pl.ANY),
                   BS(memory_space=pl.ANY)),
        scratch_shapes=[pltpu.VMEM((DK, D), F32), pltpu.VMEM((1, D), F32), cd, cd, cd, cd, cd, cd,
                        pltpu.VMEM((1, D), F32)] + _exchange_sems() + _halves_sems(),
        compiler_params=_params("arbitrary"),
    )(p, p, p, p, o, doa, st, logits, gn, part, g)


def _conv_bwd_call(dcb, uc, u, p, dw, ln_g, ln_b, part):
    S = uc.shape[0]
    nb = S // TM
    hb = TM // HALO

    def body(dcb_ref, uc_ref, u_ref, uh_ref, cv_ref, cg_ref, dw_ref, g_ref, b_ref, part_ref,
             dp_ref, bsum_ref, ddw_ref, acc_ref, recv_ref, uext, dext, ush, dsh, send_sems, recv_sems):
        i = pl.program_id(0)
        start, finish = _chip_exchange(part_ref, recv_ref, send_sems, recv_sems)

        @pl.when(i == 0)
        def _():
            start()
            dext[TM:EXT, :] = jnp.zeros((EXT - TM, D), F32)
            uext[HALO + TM:EXT, :] = jnp.zeros((EXT - HALO - TM, D), F32)
            bsum_ref[...] = jnp.zeros_like(bsum_ref)
            ddw_ref[...] = jnp.zeros_like(ddw_ref)
            acc_ref[...] = jnp.zeros_like(acc_ref)

        first_tile = (nb - 1 - i) == 0
        uext[0:HALO, :] = jnp.where(first_tile, 0.0, uh_ref[...])
        uext[HALO:HALO + TM, :] = u_ref[...]
        _fill_shifted(uext, ush)

        for rb in range(TM // SUB):
            rs_ = slice(rb * SUB, (rb + 1) * SUB)
            xh, rs = _layernorm_stats(uc_ref[rs_, :])
            ln = xh * g_ref[...] + b_ref[...]
            dln = dcb_ref[rs_, :] * _dsilu(ln, _sig(ln))
            acc_ref[1:2, :] += _rowsum(dln * xh)
            acc_ref[2:3, :] += _rowsum(dln)
            dxh = dln * g_ref[...]
            duc = rs * (dxh - jnp.mean(dxh, axis=-1, keepdims=True)
                        - xh * jnp.mean(dxh * xh, axis=-1, keepdims=True))
            dext[rs_, :] = duc
            acc_ref[0:1, :] += _rowsum(duc)
        _fill_shifted(dext, dsh)

        for j in range(CONV_K):
            part = jnp.zeros((SUB, D), F32)
            for rb in range(TM // SUB):
                s0 = HALO - (CONV_K - 1) + j + rb * SUB
                part = part + dext[rb * SUB:(rb + 1) * SUB, :] * _window(uext, ush, s0, SUB)
            ddw_ref[j:j + 1, :] += _rowsum(part)

        for rb in range(TM // SUB):
            rs_ = slice(rb * SUB, (rb + 1) * SUB)
            du = jnp.zeros((SUB, D), F32)
            for j in range(CONV_K):
                s0 = rb * SUB + (CONV_K - 1) - j
                du = du + dw_ref[j:j + 1, :] * _window(dext, dsh, s0, SUB)
            cg = cg_ref[rs_, :]
            sg = _sig(cg)
            dcv = du * sg
            dcg = du * cv_ref[rs_, :] * (sg * (1.0 - sg))
            dp_ref[rs_, 0:D] = dcv.astype(BF16)
            dp_ref[rs_, D:2 * D] = dcg.astype(BF16)
            bsum_ref[:, 0:D] += _rowsum(dcv)
            bsum_ref[:, D:2 * D] += _rowsum(dcg)

        dext[TM:TM + HALO, :] = dext[0:HALO, :]

        @pl.when(i == nb - 1)
        def _():
            finish()

    rev = BS((TM, D), lambda i: (nb - 1 - i, 0))
    vec = BS((1, D), lambda i: (0, 0))
    return pl.pallas_call(
        body, name="conv_bwd", grid=(nb,),
        out_shape=(SDS((S, 2 * D), BF16), SDS((1, 2 * D), F32), SDS((32, D), F32), SDS((8, D), F32),
                   SDS((3,) + part.shape[1:], part.dtype)),
        in_specs=[rev, rev, rev, BS((HALO, D), lambda i: (jnp.maximum((nb - 1 - i) * hb - 1, 0), 0)),
                  BS((TM, D), lambda i: (nb - 1 - i, 4)), BS((TM, D), lambda i: (nb - 1 - i, 5)),
                  BS((CONV_K, D), lambda i: (0, 0)), vec, vec, BS(memory_space=pl.ANY)],
        out_specs=(BS((TM, 2 * D), lambda i: (nb - 1 - i, 0)), BS((1, 2 * D), lambda i: (0, 0)),
                   BS((32, D), lambda i: (0, 0)), BS((8, D), lambda i: (0, 0)), BS(memory_space=pl.ANY)),
        scratch_shapes=[pltpu.VMEM((EXT, D), F32), pltpu.VMEM((EXT, D), F32),
                        pltpu.VMEM((7, HALO + TM, D), F32), pltpu.VMEM((7, HALO + TM, D), F32)] + _exchange_sems(),
        compiler_params=_params("arbitrary"),
    )(dcb, uc, u, u, p, p, dw, ln_g, ln_b, part)


def _in_bwd_call(dp_hg, dp_cv, dp_gt, x, dx2, mod, pre_tm, wg, part, full_a, full_b):
    S = x.shape[0]
    tm = 2 * TM

    def body(hg_ref, cv_ref, gt_ref, x_ref, dx2_ref, mod_ref, g_ref, w_hbm, part_ref, fa_in, fb_in,
             gx_ref, acc_ref, recv_ref, fa_out, fb_out, w_vmem, sem, send_sems, recv_sems, sa, ra, sb, rb):
        start, finish = _chip_exchange(part_ref, recv_ref, send_sems, recv_sems)
        start_a, finish_a = _join_exchange(fa_in, fa_out, sa, ra)
        start_b, finish_b = _join_exchange(fb_in, fb_out, sb, rb)

        @pl.when(pl.program_id(0) == 0)
        def _():
            start_a()
            start_b()
            start()
            _load_rows(w_hbm, w_vmem, sem, O_IN).wait()
            acc_ref[...] = jnp.zeros_like(acc_ref)

        dh = jnp.zeros((tm, D), F32)
        for k in range(IN_COLS // D):
            src, kk = ((hg_ref, k), (cv_ref, k - 4), (gt_ref, k - 6))[0 if k < 4 else (1 if k < 6 else 2)]
            dh = dh + _mm(src[:, kk * D:(kk + 1) * D], w_vmem[k // 2, (k % 2) * D:(k % 2 + 1) * D, :], NT)
        xv = x_ref[...]
        r = lax.rsqrt(jnp.mean(xv * xv, axis=-1, keepdims=True) + EPS)
        xn = xv * r
        yv = xn * g_ref[...]
        acc_ref[0:1, :] += _rowsum(dh)
        acc_ref[1:2, :] += _rowsum(dh * yv)
        dyv = dh * (1.0 + mod_ref[:, D:2 * D])
        acc_ref[2:3, :] += _rowsum(dyv * xn)
        dxn = dyv * g_ref[...]
        gx_ref[...] = dx2_ref[...] + r * (dxn - xn * jnp.mean(dxn * xn, axis=-1, keepdims=True))

        @pl.when(pl.program_id(0) == S // tm - 1)
        def _():
            finish_a()
            finish_b()
            finish()

    tile = BS((tm, D), lambda i: (i, 0))
    hbm = BS(memory_space=pl.ANY)
    return pl.pallas_call(
        body, name="in_bwd", grid=(S // tm,),
        out_shape=(SDS((S, D), F32), SDS((8, D), F32), SDS((3,) + part.shape[1:], part.dtype),
                   SDS(full_a.shape, full_a.dtype), SDS(full_b.shape, full_b.dtype)),
        in_specs=[BS((tm, 4 * D), lambda i: (i, 0)), BS((tm, 2 * D), lambda i: (i, 0)),
                  BS((tm, 2 * D), lambda i: (i, 0)), tile, tile, BS((1, 6 * D), lambda i: (0, 0)),
                  BS((1, D), lambda i: (0, 0)), hbm, hbm, hbm, hbm],
        out_specs=(tile, BS((8, D), lambda i: (0, 0)), hbm, hbm, hbm),
        scratch_shapes=[pltpu.VMEM((N_CHIPS, R_IN, D), BF16), pltpu.SemaphoreType.DMA] + _exchange_sems()
        + _join_sems() + _join_sems(),
        input_output_aliases={9: 3, 10: 4},
        compiler_params=_params("arbitrary"),
    )(dp_hg, dp_cv, dp_gt, x, dx2, mod, pre_tm, wg, part, full_a, full_b)


def _wgrad_call(gp, a, b, name, bm, place, rows):
    S, M = a.shape
    N = b.shape[1]
    bk = min(S, 1024)
    nk = S // bk

    def body(a_ref, b_ref, *rest):
        o_ref, acc = rest[-2], rest[-1]
        k = pl.program_id(2)

        @pl.when(k == 0)
        def _():
            acc[...] = jnp.zeros_like(acc)

        acc[...] += _mm(a_ref[...], b_ref[...], TN)

        @pl.when(k == nk - 1)
        def _():
            o_ref[...] = acc[...].astype(BF16)

    in_specs = [BS((bk, bm), lambda i, j, k: (k, i)), BS((bk, D), lambda i, j, k: (k, j))]
    args = [a, b]
    if gp is not None:
        in_specs.append(BS(memory_space=pl.ANY))
        args.append(gp)
    return pl.pallas_call(
        body, name=name, grid=(M // bm, N // D, nk),
        out_shape=SDS((N_CHIPS, rows, D), BF16),
        in_specs=in_specs,
        out_specs=BS((None, bm, D), lambda i, j, k: (*place(i, j), 0)),
        scratch_shapes=[pltpu.VMEM((bm, D), F32)],
        input_output_aliases={} if gp is None else {2: 0},
        compiler_params=_params("parallel", "parallel", "arbitrary"),
    )(*args)


def _wgrad_rows_call(gp, a, b, name, blk):
    S = a.shape[0]
    bk = min(S, 1024)
    nk = S // bk

    def body(a_ref, b_ref, *rest):
        o_ref, acc = rest[-2], rest[-1]
        k = pl.program_id(0)

        @pl.when(k == 0)
        def _():
            acc[...] = jnp.zeros_like(acc)

        acc[...] += _mm(a_ref[...], b_ref[...], TN)

        @pl.when(k == nk - 1)
        def _():
            for c in range(N_CHIPS):
                o_ref[c] = acc[c * R_BR:(c + 1) * R_BR, :].astype(BF16)

    in_specs = [BS((bk, D), lambda k: (k, 0)), BS((bk, D), lambda k: (k, 0))]
    args = [a, b]
    if gp is not None:
        in_specs.append(BS(memory_space=pl.ANY))
        args.append(gp)
    return pl.pallas_call(
        body, name=name, grid=(nk,),
        out_shape=SDS((N_CHIPS, 3 * R_BR, D), BF16),
        in_specs=in_specs,
        out_specs=BS((N_CHIPS, R_BR, D), lambda k: (0, blk, 0)),
        scratch_shapes=[pltpu.VMEM((D, D), F32)],
        input_output_aliases={} if gp is None else {2: 0},
        compiler_params=_params("arbitrary"),
    )(*args)


def _outer_call(cact, dmod):
    n = dmod.shape[1]

    def body(a_ref, b_ref, o_ref):
        o_ref[...] = _mm(a_ref[...], b_ref[...], TN, HI)

    return pl.pallas_call(
        body, name="wgrad_ada", out_shape=SDS((D, n), F32),
        compiler_params=pltpu.CompilerParams(vmem_limit_bytes=VMEM_LIMIT),
    )(cact, dmod)


def _adamw_call(w, g, m, v, name):
    R, C = w.shape
    tr = R
    while tr * C > 512 * 1024 and tr % 16 == 0:
        tr //= 2
    c1 = 1.0 - ADAM_B1 ** ADAM_STEP
    c2 = 1.0 - ADAM_B2 ** ADAM_STEP

    def body(w_ref, g_ref, m_ref, v_ref, d_ref, m2_ref, v2_ref):
        g = g_ref[...]
        m2 = ADAM_B1 * m_ref[...] + (1.0 - ADAM_B1) * g
        v2 = ADAM_B2 * v_ref[...] + (1.0 - ADAM_B2) * (g * g)
        m2_ref[...] = m2
        v2_ref[...] = v2
        d_ref[...] = -ADAM_LR * ((m2 / c1) / (jnp.sqrt(v2 / c2) + ADAM_EPS) + ADAM_WD * w_ref[...])

    tile = BS((tr, C), lambda i: (i, 0))
    return pl.pallas_call(
        body, name=name, grid=(R // tr,), out_shape=(SDS((R, C), F32),) * 3,
        in_specs=[tile] * 4, out_specs=(tile,) * 3, compiler_params=_params("parallel"),
    )(w, g, m, v)


def _adamw_rows_call(ws, g, ms, vs, name):
    k = len(ws)
    r = ws[0].shape[0]
    c1 = 1.0 - ADAM_B1 ** ADAM_STEP
    c2 = 1.0 - ADAM_B2 ** ADAM_STEP

    def body(g_ref, *refs):
        ins, outs = refs[:3 * k], refs[3 * k:]
        for j in range(k):
            w_ref, m_ref, v_ref = ins[j], ins[k + j], ins[2 * k + j]
            d_ref, m2_ref, v2_ref = outs[j], outs[k + j], outs[2 * k + j]
            g = g_ref[j * r:(j + 1) * r, :]
            m2 = ADAM_B1 * m_ref[...] + (1.0 - ADAM_B1) * g
            v2 = ADAM_B2 * v_ref[...] + (1.0 - ADAM_B2) * (g * g)
            m2_ref[...] = m2
            v2_ref[...] = v2
            d_ref[...] = -ADAM_LR * ((m2 / c1) / (jnp.sqrt(v2 / c2) + ADAM_EPS) + ADAM_WD * w_ref[...])

    out = pl.pallas_call(
        body, name=name, out_shape=(SDS(ws[0].shape, F32),) * (3 * k),
        compiler_params=pltpu.CompilerParams(vmem_limit_bytes=VMEM_LIMIT),
    )(g, *ws, *ms, *vs)
    return out[:k], out[k:2 * k], out[2 * k:]


def _adamw_gather_call(w, g, m, v, srows, name):
    R, C = w.shape
    tr = R
    while tr * C > 512 * 1024 and tr % 16 == 0:
        tr //= 2
    nsteps = R // tr
    mr = srows.shape[0]
    c1 = 1.0 - ADAM_B1 ** ADAM_STEP
    c2 = 1.0 - ADAM_B2 ** ADAM_STEP

    def body(w_ref, g_ref, m_ref, v_ref, s_ref, d_ref, m2_ref, v2_ref, all_ref, sum_ref,
             x_scr, out_scr, send_sems, recv_sems, local_sem):
        i = pl.program_id(0)
        start, finish = _allgather_parts(x_scr, out_scr, send_sems, recv_sems, local_sem)

        @pl.when(i == 0)
        def _():
            x_scr[...] = s_ref[...]
            start()

        g = g_ref[...]
        m2 = ADAM_B1 * m_ref[...] + (1.0 - ADAM_B1) * g
        v2 = ADAM_B2 * v_ref[...] + (1.0 - ADAM_B2) * (g * g)
        m2_ref[...] = m2
        v2_ref[...] = v2
        d_ref[...] = -ADAM_LR * ((m2 / c1) / (jnp.sqrt(v2 / c2) + ADAM_EPS) + ADAM_WD * w_ref[...])

        @pl.when(i == nsteps - 1)
        def _():
            finish()
            all_ref[...] = out_scr[...]
            acc = out_scr[0:mr, :]
            for d in range(1, N_DEV):
                acc = acc + out_scr[d * mr:(d + 1) * mr, :]
            sum_ref[...] = acc

    tile = BS((tr, C), lambda i: (i, 0))
    return pl.pallas_call(
        body, name=name, grid=(nsteps,),
        out_shape=(SDS((R, C), F32),) * 3 + (SDS((N_DEV * mr, D), F32), SDS((mr, D), F32)),
        in_specs=[tile] * 4 + [BS((mr, D), lambda i: (0, 0))],
        out_specs=(tile,) * 3 + (BS((N_DEV * mr, D), lambda i: (0, 0)), BS((mr, D), lambda i: (0, 0))),
        scratch_shapes=[pltpu.VMEM((mr, D), F32), pltpu.VMEM((N_DEV * mr, D), F32)] + _allgather_sems(),
        compiler_params=_params("arbitrary"),
    )(w, g, m, v, srows)


def _rs_begin(g, c_idx, tag):
    n = g.shape[1]
    g = g.reshape(N_CHIPS, 2, n // 2, D)
    return _add_halves_call(g, _sibling_halves_call(g, tag), c_idx, tag)


def _rs_end(part, recv, c_idx, chip_idx, tag):
    n = 2 * part.shape[1]
    full = _add_chips_call(part, recv, jnp.concatenate([chip_idx, c_idx]), tag)
    return _sibling_join_call(full, tag).reshape(n, D)


def _local_step(x, mod, cact, target, wg, pack, small, c_idx, chip_idx):
    p, h1, wg = _fwd_in_call(x, mod, small["pre_tm"], wg, small["b_in"], pack, small["order"])
    o, oa, st, wg = _hgrn_fwd_call(p, small["logits"], small["hg_norm"], wg, pack)
    u, uc, cb, wg = _conv_fwd_call(p, small["conv_dw"], small["conv_db"], small["ln_g"], small["ln_b"], wg, pack)
    ya, yb, mg, y, x2, h2 = _merge_fwd_call(oa, cb, p, x, mod, small["post_tm"], small["pre_cm"], wg)
    z, da, dy2, dx2, acc_f = _ffn_call(h2, x2, target, mod, small["post_cm"], small["pre_cm"], wg)

    g_ff = _wgrad_call(None, h2, da, "wgrad_ff1", D, lambda i, j: (j, 0), 2 * R_FF)
    g_ff = _wgrad_call(g_ff, z, dy2, "wgrad_ff2", D, lambda i, j: (i, 1), 2 * R_FF)
    g_ff = g_ff.reshape(N_CHIPS, 2, R_FF, D)
    dy, dya, dyb, doa, dcb, dp_gt, acc_m, bs_gt, hr_ff = _merge_bwd_call(dx2, y, ya, yb, p, mod, small["post_tm"],
                                                                        wg, g_ff)
    part_ff = _add_halves_call(g_ff, hr_ff, c_idx, "ff")

    g_br = _wgrad_rows_call(None, oa, dya, "wgrad_br_a", 0)
    g_br = _wgrad_rows_call(g_br, cb, dyb, "wgrad_br_b", 1)
    g_br = _wgrad_rows_call(g_br, mg, dy, "wgrad_out", 2)
    g_br = g_br.reshape(N_CHIPS, 2, 3 * R_BR // 2, D)
    dp_hg, bs_hg, dlg, dgn, recv_ff, hr_br = _hgrn_bwd_call(p, o, doa, st, small["logits"], small["hg_norm"],
                                                            part_ff, g_br)
    part_br = _add_halves_call(g_br, hr_br, c_idx, "br")
    dp_cv, bs_cv, ddw, acc_c, recv_br = _conv_bwd_call(dcb, uc, u, p, small["conv_dw"], small["ln_g"], small["ln_b"],
                                                        part_br)

    g_in = _wgrad_call(None, h1, dp_hg, "wgrad_in_hg", D, lambda i, j: (j // 2, j % 2), R_IN)
    g_in = _wgrad_call(g_in, h1, dp_cv, "wgrad_in_cv", D, lambda i, j: (2, j), R_IN)
    g_in = _wgrad_call(g_in, h1, dp_gt, "wgrad_in_gt", D, lambda i, j: (3, j), R_IN)
    part_in = _rs_begin(g_in, c_idx, "in")
    chip_c = jnp.concatenate([chip_idx, c_idx])
    full_ff = _add_chips_call(part_ff, recv_ff, chip_c, "ff")
    full_br = _add_chips_call(part_br, recv_br, chip_c, "br")
    gx, acc_i, recv_in, full_ff, full_br = _in_bwd_call(dp_hg, dp_cv, dp_gt, x, dx2, mod, small["pre_tm"], wg,
                                                        part_in, full_ff, full_br)
    red_ff = full_ff.reshape(2 * R_FF, D)
    red_br = full_br.reshape(3 * R_BR, D)
    red_in = _rs_end(part_in, recv_in, c_idx, chip_idx, "in")

    zrow = jnp.zeros((1, D), F32)
    rows = [acc_i[0:1], acc_i[1:2], acc_m[0:1], acc_f[2:3], acc_f[3:4], acc_f[0:1],
            acc_i[2:3], acc_m[1:2], acc_f[4:5], acc_f[1:2],
            jnp.concatenate([bs_hg, bs_cv, bs_gt], axis=1).reshape(8, D),
            dlg, dgn, acc_c[0:1], acc_c[1:2], acc_c[2:3],
            ddw,
            cact, acc_f[5:6]] + [zrow] * 6
    return gx, jnp.concatenate(rows, axis=0), red_in, red_ff, red_br


def kernel(x, c, w_ada, b_ada, pre_norm_tm, post_norm_tm, pre_norm_cm, post_norm_cm, w_in, b_in, hg_lb_logits, hg_norm, conv_dw, conv_db, conv_ln_g, conv_ln_b, w_br_a, w_br_b, w_out, w_ff1, w_ff2, loss_target, m_w_ada, m_b_ada, m_pre_norm_tm, m_post_norm_tm, m_pre_norm_cm, m_post_norm_cm, m_w_in, m_b_in, m_hg_lb_logits, m_hg_norm, m_conv_dw, m_conv_db, m_conv_ln_g, m_conv_ln_b, m_w_br_a, m_w_br_b, m_w_out, m_w_ff1, m_w_ff2, v_w_ada, v_b_ada, v_pre_norm_tm, v_post_norm_tm, v_pre_norm_cm, v_post_norm_cm, v_w_in, v_b_in, v_hg_lb_logits, v_hg_norm, v_conv_dw, v_conv_db, v_conv_ln_g, v_conv_ln_b, v_w_br_a, v_w_br_b, v_w_out, v_w_ff1, v_w_ff2):
    xi, yi, ci = lax.axis_index("x"), lax.axis_index("y"), lax.axis_index("c")
    chip = 2 * xi + yi
    c_idx = jnp.reshape(ci, (1,)).astype(jnp.int32)
    chip_idx = jnp.reshape(chip, (1,)).astype(jnp.int32)

    def pack_small(ada_b, pre_t, post_t, pre_c, post_c, in_b, lg, hgn, cdb, lng, lnb, cdw):
        flat = jnp.concatenate([cdw[0].reshape(-1), jnp.zeros((8 * D - CONV_K * 256,), F32)]).reshape(8, D)
        return jnp.concatenate([ada_b.reshape(6, D), pre_t, post_t, pre_c, post_c, in_b.reshape(8, D), lg, hgn,
                                cdb, lng, lnb, flat], axis=0)

    w_in_halves = w_in[0].reshape(D, 2, D).transpose(1, 0, 2).reshape(R_IN, D)
    pack = jnp.concatenate([w_in_halves, w_ff1[0], w_ff2[0], w_br_a[0], w_br_b[0], w_out[0]],
                           axis=0).astype(BF16)
    wg = lax.dynamic_update_slice(lax.empty((N_CHIPS, PACK_W, D), BF16), pack[None], (chip, 0, 0))
    wa = 6 * D // N_CHIPS
    me = 4 * xi + 2 * yi + ci
    dw_blk = jnp.concatenate([conv_dw[0].reshape(-1), jnp.zeros((8 * D - CONV_K * 256,), F32)]).reshape(8, D)
    dw_all, ca_all, mod_all = _prologue_call(
        dw_blk, jnp.broadcast_to(c, (8, D)), w_ada[0].astype(BF16),
        lax.dynamic_slice_in_dim(b_ada, chip * wa, wa, axis=1))
    order = jnp.stack([chip, 2 * (1 - xi) + yi, 2 * xi + (1 - yi), 2 * (1 - xi) + (1 - yi)]).astype(jnp.int32)
    dw_all = dw_all.reshape(N_CHIPS, 2, 8 * D)[:, 0, :CONV_K * 256].reshape(N_CHIPS, CONV_K, 256)
    dw_full = dw_all.transpose(1, 0, 2).reshape(CONV_K, D)
    cact = lax.dynamic_slice_in_dim(ca_all, me * 8, 1, axis=0)
    mod_mine = lax.dynamic_index_in_dim(mod_all.reshape(N_CHIPS, 2, N_DEV, 8, wa)[:, 0, :, 0, :], me, axis=1,
                                        keepdims=False)
    mod = mod_mine.reshape(1, 6 * D)

    small = dict(pre_tm=pre_norm_tm, post_tm=post_norm_tm, pre_cm=pre_norm_cm, post_cm=post_norm_cm,
                 b_in=b_in, logits=hg_lb_logits, hg_norm=hg_norm, conv_dw=dw_full, conv_db=conv_db,
                 ln_g=conv_ln_g, ln_b=conv_ln_b, order=order)

    gx, srows, red_in, red_ff, red_br = _local_step(x[0], mod, cact, loss_target[0], wg, pack, small, c_idx,
                                                    chip_idx)

    shapes = {"in": w_in.shape, "br_a": w_br_a.shape, "br_b": w_br_b.shape, "out": w_out.shape,
              "ff1": w_ff1.shape, "ff2": w_ff2.shape}
    offs = {"in": (red_in, 0, R_IN), "ff1": (red_ff, 0, R_FF), "ff2": (red_ff, R_FF, 2 * R_FF),
            "br_a": (red_br, 0, R_BR), "br_b": (red_br, R_BR, 2 * R_BR), "out": (red_br, 2 * R_BR, 3 * R_BR)}
    wmv = {"in": (w_in, m_w_in, v_w_in), "br_a": (w_br_a, m_w_br_a, v_w_br_a), "br_b": (w_br_b, m_w_br_b, v_w_br_b),
           "out": (w_out, m_w_out, v_w_out), "ff1": (w_ff1, m_w_ff1, v_w_ff1), "ff2": (w_ff2, m_w_ff2, v_w_ff2)}
    res = {}
    for n in ("in", "ff1", "ff2"):
        shp = shapes[n]
        g2d = offs[n][0][offs[n][1]:offs[n][2]]
        if n == "in":
            g2d = g2d.reshape(2, D, D).transpose(1, 0, 2)
        g2d = g2d.reshape(shp[1], shp[2])
        w_, m_, v_ = (a[0] for a in wmv[n])
        if n == "in":
            d_, m2_, v2_, sall, ssum = _adamw_gather_call(w_, g2d, m_, v_, srows, "adamw_in")
        else:
            d_, m2_, v2_ = _adamw_call(w_, g2d, m_, v_, "adamw_" + n)
        res[n] = tuple(a.reshape(shp) for a in (g2d, d_, m2_, v2_))
    trio = ("br_a", "br_b", "out")
    d3, m3, v3 = _adamw_rows_call([wmv[n][0][0] for n in trio], red_br, [wmv[n][1][0] for n in trio],
                                  [wmv[n][2][0] for n in trio], "adamw_br")
    for j, n in enumerate(trio):
        res[n] = tuple(a.reshape(shapes[n]) for a in (red_br[j * R_BR:(j + 1) * R_BR], d3[j], m3[j], v3[j]))

    sall = sall.reshape(N_DEV, SMALL_ROWS, D)
    loss = jnp.sum(ssum[57])
    dmod_all = sall[:, 0:6, :].reshape(N_DEV, 6 * D)
    g_ada = _outer_call(sall[:, 56, :], lax.dynamic_slice_in_dim(dmod_all, chip * wa, wa, axis=1))
    g_dw = lax.dynamic_slice_in_dim(ssum[24:24 + CONV_K], chip * 256, 256, axis=1)
    g_small = jnp.concatenate(
        [ssum[0:24], jnp.concatenate([g_dw.reshape(-1), jnp.zeros((8 * D - CONV_K * 256,), F32)]).reshape(8, D)],
        axis=0)
    d_, m2_, v2_ = _adamw_call(w_ada[0], g_ada, m_w_ada[0], v_w_ada[0], "adamw_ada")
    res["ada"] = tuple(a.reshape(w_ada.shape) for a in (g_ada, d_, m2_, v2_))

    ws = pack_small(b_ada, pre_norm_tm, post_norm_tm, pre_norm_cm, post_norm_cm, b_in, hg_lb_logits, hg_norm,
                    conv_db, conv_ln_g, conv_ln_b, conv_dw)
    ms = pack_small(m_b_ada, m_pre_norm_tm, m_post_norm_tm, m_pre_norm_cm, m_post_norm_cm, m_b_in, m_hg_lb_logits,
                    m_hg_norm, m_conv_db, m_conv_ln_g, m_conv_ln_b, m_conv_dw)
    vs = pack_small(v_b_ada, v_pre_norm_tm, v_post_norm_tm, v_pre_norm_cm, v_post_norm_cm, v_b_in, v_hg_lb_logits,
                    v_hg_norm, v_conv_db, v_conv_ln_g, v_conv_ln_b, v_conv_dw)
    sres = (g_small,) + tuple(_adamw_call(ws, g_small, ms, vs, "adamw_small"))

    def unpack_small(t):
        return {"b_ada": t[0:6].reshape(1, 6 * D), "pre_tm": t[6:7], "post_tm": t[7:8], "pre_cm": t[8:9],
                "post_cm": t[9:10], "b_in": t[10:18].reshape(1, IN_COLS), "logits": t[18:20], "hg_norm": t[20:21],
                "conv_db": t[21:22], "ln_g": t[22:23], "ln_b": t[23:24],
                "conv_dw": t[24:32].reshape(-1)[:CONV_K * 256].reshape(1, CONV_K, 256)}

    order = ["ada", "b_ada", "pre_tm", "post_tm", "pre_cm", "post_cm", "in", "b_in", "logits", "hg_norm", "conv_dw",
             "conv_db", "ln_g", "ln_b", "br_a", "br_b", "out", "ff1", "ff2"]
    outs = [loss, gx.reshape(x.shape)]
    for kind in range(4):
        sm = unpack_small(sres[kind])
        for n in order:
            outs.append(res[n][kind] if n in res else sm[n])
    return tuple(outs)
```

```python
import jax
import jax.numpy as jnp
from jax import lax
from jax.experimental import pallas as pl
from jax.experimental.pallas import tpu as pltpu

F32, BF16 = jnp.float32, jnp.bfloat16
SDS = jax.ShapeDtypeStruct
BS = pl.BlockSpec
MESH = pl.DeviceIdType.MESH
HI = lax.Precision.HIGHEST

D = 1024
D_FF = 4096
IN_COLS = 8192
HEADS, DK = 8, 128
CHUNK = 128
CONV_K = 31
HALO = 32
SUB = 32
EPS = 1e-6
N_CHIPS, N_DEV = 4, 8
TM = 256
TB = 256
VMEM_LIMIT = 56 * 1024 * 1024

R_IN, R_BR, R_FF = 2048, 256, 1024
PACK_W = R_IN + 3 * R_BR + 2 * R_FF
O_IN, O_FF1, O_FF2, O_BRA, O_BRB, O_OUT = 0, 2048, 3072, 4096, 4352, 4608
SMALL_ROWS = 64

ADAM_LR, ADAM_B1, ADAM_B2, ADAM_EPS, ADAM_WD, ADAM_STEP = 0.001, 0.9, 0.999, 1e-08, 0.01, 10

NN = (((1,), (0,)), ((), ()))
NT = (((1,), (1,)), ((), ()))
TN = (((0,), (0,)), ((), ()))


def _mm(a, b, dims=NN, precision=None):
    return lax.dot_general(a, b, dims, preferred_element_type=F32, precision=precision)


def _sig(v):
    return jax.nn.sigmoid(v)


def _dsilu(v, s):
    return s * (1.0 + v * (1.0 - s))


def _params(*sem):
    return pltpu.CompilerParams(dimension_semantics=sem if sem else None, vmem_limit_bytes=VMEM_LIMIT)


def _rowsum(v):
    return jnp.sum(v, axis=0, keepdims=True)


def _mesh_pos():
    return lax.axis_index("x"), lax.axis_index("y"), lax.axis_index("c")


def _allgather_parts(x_ref, out_ref, send_sems, recv_sems, local_sem):
    m_per = x_ref.shape[0]
    x, y, c = _mesh_pos()
    me, sibling = (x, y, c), (x, y, 1 - c)
    chips = [(1 - x, y), (x, 1 - y), (1 - x, 1 - y)]

    def rows(px, py, pc):
        return out_ref.at[pl.ds((4 * px + 2 * py + pc) * m_per, m_per), :]

    def copy(k, block, to, src=None):
        return pltpu.make_async_remote_copy(
            src_ref=rows(*block) if src is None else src, dst_ref=rows(*block),
            send_sem=send_sems.at[k], recv_sem=recv_sems.at[k], device_id=to, device_id_type=MESH)

    def first():
        return [copy(0, me, sibling, src=x_ref)] + [copy(1 + j, me, (*chip, c), src=x_ref)
                                                    for j, chip in enumerate(chips)]

    def start():
        pltpu.make_async_copy(x_ref, rows(*me), local_sem).start()
        for cp in first():
            cp.start()

    def finish():
        passed = [copy(4 + j, (*chip, c), sibling) for j, chip in enumerate(chips)]
        for j, chip in enumerate(chips):
            copy(1 + j, (*chip, c), me).wait_recv()
            passed[j].start()
        copy(0, sibling, me).wait_recv()
        for j, chip in enumerate(chips):
            copy(4 + j, (*chip, 1 - c), me).wait_recv()
        for cp in first() + passed:
            cp.wait_send()
        pltpu.make_async_copy(x_ref, rows(*me), local_sem).wait()

    return start, finish


def _allgather_sems():
    return [pltpu.SemaphoreType.DMA((7,)), pltpu.SemaphoreType.DMA((7,)), pltpu.SemaphoreType.DMA]
def _gather_sems(n_ranges):
    return [pltpu.SemaphoreType.DMA((6 * n_ranges,)), pltpu.SemaphoreType.DMA((6 * n_ranges,))]


def _pack_gather(pack_ref, wg_ref, send_sems, recv_sems, ranges):
    x, y, c = _mesh_pos()
    me, sibling = (x, y, c), (x, y, 1 - c)
    chips = [(1 - x, y), (x, 1 - y), (1 - x, 1 - y)]

    def land(r, px, py, pc):
        off, n = ranges[r]
        return wg_ref.at[2 * px + py, pl.ds(off + pc * (n // 2), n // 2), :]

    def mine(r):
        off, n = ranges[r]
        return pack_ref.at[pl.ds(off + c * (n // 2), n // 2), :]

    def copy(r, k, block, to, src=None):
        return pltpu.make_async_remote_copy(
            src_ref=land(r, *block) if src is None else src, dst_ref=land(r, *block),
            send_sem=send_sems.at[6 * r + k], recv_sem=recv_sems.at[6 * r + k], device_id=to, device_id_type=MESH)

    def start():
        for r in range(len(ranges)):
            for j, chip in enumerate(chips):
                copy(r, j, me, (*chip, c), src=mine(r)).start()

    def finish():
        for r in range(len(ranges)):
            for j, chip in enumerate(chips):
                copy(r, j, (*chip, c), me).wait_recv()
                copy(r, 3 + j, (*chip, c), sibling).start()
        for r in range(len(ranges)):
            for j, chip in enumerate(chips):
                copy(r, 3 + j, (*chip, 1 - c), me).wait_recv()
        for r in range(len(ranges)):
            for j, chip in enumerate(chips):
                copy(r, j, me, (*chip, c), src=mine(r)).wait_send()
                copy(r, 3 + j, (*chip, c), sibling).wait_send()

    return start, finish


def _relay_sems():
    return [pltpu.SemaphoreType.DMA((8,)), pltpu.SemaphoreType.DMA((8,))]


def _relay_gather(pack_ref, wg_ref, send_sems, recv_sems, off, n):
    x, y, c = _mesh_pos()
    me, sibling = (x, y, c), (x, y, 1 - c)
    chips = [(1 - x, y), (x, 1 - y), (1 - x, 1 - y)]
    h, q = n // 2, n // 4

    def land(px, py, pc, piece=None):
        if piece is None:
            return wg_ref.at[2 * px + py, pl.ds(off + pc * h, h), :]
        return wg_ref.at[2 * px + py, pl.ds(off + pc * h + piece * q, q), :]

    def copy(k, ref, to, src=None):
        return pltpu.make_async_remote_copy(
            src_ref=ref if src is None else src, dst_ref=ref, send_sem=send_sems.at[k], recv_sem=recv_sems.at[k],
            device_id=to, device_id_type=MESH)

    def direct(j):
        return copy(j, land(x, y, c), (*chips[j], c), src=pack_ref.at[pl.ds(off + c * h, h), :])

    def relayed(j):
        if j == 0:
            return copy(6, land(*chips[0], c, 1), (x, 1 - y, c))
        return copy(7, land(*chips[1], c, 0), (1 - x, y, c))

    def start():
        direct(0).start()
        direct(1).start()

    def arrive(j):
        if j == 0:
            for k in range(2):
                copy(k, land(*chips[k], c), me).wait_recv()
                relayed(k).start()
                copy(3 + k, land(*chips[k], c), sibling).start()
        if j == 2:
            copy(7, land(*chips[2], c, 0), me).wait_recv()
            copy(6, land(*chips[2], c, 1), me).wait_recv()
            copy(5, land(*chips[2], c), sibling).start()
        copy(3 + j, land(*chips[j], 1 - c), me).wait_recv()

    def drain():
        for j in range(2):
            direct(j).wait_send()
            relayed(j).wait_send()
        for j in range(3):
            copy(3 + j, land(*chips[j], c), sibling).wait_send()

    return start, arrive, drain


def _prologue_call(dw_blk, c_blk, w_ada, b_ada):
    wa = w_ada.shape[1]

    def body(dw_ref, c_ref, wa_ref, ba_ref, dwg_ref, ca_ref, modg_ref,
             cg_scr, part_scr, s1, r1, l1, s2, r2, l2, s3, r3, l3):
        start_c, finish_c = _allgather_parts(c_ref, cg_scr, s2, r2, l2)
        start_dw, finish_dw = _allgather_parts(dw_ref, dwg_ref, s1, r1, l1)
        start_mod, finish_mod = _allgather_parts(part_scr, modg_ref, s3, r3, l3)
        start_c()
        start_dw()
        finish_c()
        cv = cg_scr[...]
        ca = cv * _sig(cv)
        ca_ref[...] = ca
        pick = (lax.broadcasted_iota(jnp.int32, (N_DEV, N_DEV * 8), 1)
                == 8 * lax.broadcasted_iota(jnp.int32, (N_DEV, N_DEV * 8), 0)).astype(BF16)
        ca8 = _mm(pick, ca.astype(BF16)).astype(BF16)
        part_scr[...] = _mm(ca8, wa_ref[...]) + ba_ref[...]
        start_mod()
        finish_dw()
        finish_mod()

    vm = BS(memory_space=pltpu.VMEM)
    return pl.pallas_call(
        body, name="prologue_adaln_conv_dw",
        out_shape=(SDS((N_DEV * 8, D), F32), SDS((N_DEV * 8, D), F32), SDS((N_DEV * N_DEV, wa), F32)),
        in_specs=[vm, vm, vm, vm], out_specs=(vm, vm, vm),
        scratch_shapes=[pltpu.VMEM((N_DEV * 8, D), F32), pltpu.VMEM((N_DEV, wa), F32)]
        + _allgather_sems() + _allgather_sems() + _allgather_sems(),
        compiler_params=pltpu.CompilerParams(vmem_limit_bytes=VMEM_LIMIT),
    )(dw_blk, c_blk, w_ada, b_ada)


def _halves_exchange(g_ref, out_ref, send_sems, recv_sems):
    x, y, c = _mesh_pos()

    def copies():
        return [pltpu.make_async_remote_copy(
            src_ref=g_ref.at[k, 1 - c], dst_ref=out_ref.at[k], send_sem=send_sems.at[k], recv_sem=recv_sems.at[k],
            device_id=(x, y, 1 - c), device_id_type=MESH) for k in range(N_CHIPS)]

    def start():
        for cp in copies():
            cp.start()

    def finish():
        for cp in copies():
            cp.wait()

    return start, finish


def _halves_sems():
    return [pltpu.SemaphoreType.DMA((N_CHIPS,)), pltpu.SemaphoreType.DMA((N_CHIPS,))]


def _sibling_halves_call(g, tag):
    _, _, h, n = g.shape

    def body(g_ref, out_ref, send_sems, recv_sems):
        start, finish = _halves_exchange(g_ref, out_ref, send_sems, recv_sems)
        start()
        finish()

    return pl.pallas_call(
        body, name="rs_sibling_halves_" + tag, out_shape=SDS((N_CHIPS, h, n), g.dtype),
        in_specs=[BS(memory_space=pl.ANY)], out_specs=BS(memory_space=pl.ANY),
        scratch_shapes=_halves_sems(),
    )(g)


def _chip_exchange(p_ref, out_ref, send_sems, recv_sems):
    x, y, c = _mesh_pos()
    chips = [(1 - x, y), (x, 1 - y), (1 - x, 1 - y)]

    def copies():
        return [pltpu.make_async_remote_copy(
            src_ref=p_ref.at[2 * cx + cy], dst_ref=out_ref.at[j], send_sem=send_sems.at[j], recv_sem=recv_sems.at[j],
            device_id=(cx, cy, c), device_id_type=MESH) for j, (cx, cy) in enumerate(chips)]

    def start():
        for cp in copies():
            cp.start()

    def finish():
        for cp in copies():
            cp.wait()

    return start, finish


def _exchange_sems():
    return [pltpu.SemaphoreType.DMA((3,)), pltpu.SemaphoreType.DMA((3,))]


def _join_exchange(in_ref, out_ref, send_sems, recv_sems):
    h = in_ref.shape[1]
    q = h // 4
    x, y, c = _mesh_pos()

    def copy(k, half):
        return pltpu.make_async_remote_copy(
            src_ref=in_ref.at[half, pl.ds(k * q, q)], dst_ref=out_ref.at[half, pl.ds(k * q, q)],
            send_sem=send_sems.at[k], recv_sem=recv_sems.at[k],
            device_id=(x, y, 1 - c), device_id_type=MESH)

    def start():
        for k in range(4):
            copy(k, c).start()

    def finish():
        for k in range(4):
            copy(k, c).wait_send()
            copy(k, 1 - c).wait_recv()

    return start, finish


def _join_sems():
    return [pltpu.SemaphoreType.DMA((4,)), pltpu.SemaphoreType.DMA((4,))]


def _sibling_join_call(full, tag):
    def body(in_ref, out_ref, send_sems, recv_sems):
        start, finish = _join_exchange(in_ref, out_ref, send_sems, recv_sems)
        start()
        finish()

    return pl.pallas_call(
        body, name="rs_sibling_join_" + tag, out_shape=SDS(full.shape, full.dtype),
        in_specs=[BS(memory_space=pl.ANY)], out_specs=BS(memory_space=pl.ANY),
        scratch_shapes=_join_sems(), input_output_aliases={0: 0},
    )(full)


def _add_halves_call(g, recv, c_idx, tag):
    _, _, h, n = g.shape
    tr = h // 2

    def body(c_ref, g_ref, r_ref, o_ref):
        o_ref[...] = (g_ref[...].astype(F32) + r_ref[...].astype(F32)).astype(BF16)

    return pl.pallas_call(
        body, name="rs_add_halves_" + tag, out_shape=SDS((N_CHIPS, h, n), BF16),
        grid_spec=pltpu.PrefetchScalarGridSpec(
            num_scalar_prefetch=1, grid=(N_CHIPS, 2),
            in_specs=[BS((None, None, tr, n), lambda k, r, c_ref: (k, c_ref[0], r, 0)),
                      BS((None, tr, n), lambda k, r, c_ref: (k, r, 0))],
            out_specs=BS((None, tr, n), lambda k, r, c_ref: (k, r, 0))),
        compiler_params=_params("arbitrary", "arbitrary"),
    )(c_idx, g, recv)


def _add_chips_call(p, recv, chip_c_idx, tag):
    _, h, n = p.shape
    tr = h // 2

    def body(k_ref, p_ref, r_ref, o_ref):
        acc = p_ref[...].astype(F32)
        for j in range(3):
            acc = acc + r_ref[j].astype(F32)
        o_ref[...] = acc

    return pl.pallas_call(
        body, name="rs_add_chips_" + tag, out_shape=SDS((2, h, n), F32),
        grid_spec=pltpu.PrefetchScalarGridSpec(
            num_scalar_prefetch=1, grid=(2,),
            in_specs=[BS((None, tr, n), lambda r, k_ref: (k_ref[0], r, 0)),
                      BS((3, tr, n), lambda r, k_ref: (0, r, 0))],
            out_specs=BS((None, tr, n), lambda r, k_ref: (k_ref[1], r, 0))),
        compiler_params=_params("arbitrary"),
    )(chip_c_idx, p, recv)


def _load_rows(wg_hbm, w_vmem, sem, off):
    cp = pltpu.make_async_copy(wg_hbm.at[:, pl.ds(off, w_vmem.shape[1]), :], w_vmem, sem)
    cp.start()
    return cp


def _fwd_in_call(x, mod, pre_tm, wg, b_in, pack, order):
    S = x.shape[0]
    tmf = 2 * TM
    nt = S // tmf
    wc = IN_COLS // N_CHIPS

    def body(ord_ref, x_ref, mod_ref, g_ref, w_hbm, b_ref, pack_ref, p_ref, h_hbm, wg_out, w_vmem, h_scr, sems,
             send_sems, recv_sems, send_sems2, recv_sems2):
        q, i = pl.program_id(0), pl.program_id(1)
        rows = pl.ds(pl.multiple_of(i * tmf, tmf), tmf)
        start, arrive, drain = _relay_gather(pack_ref, wg_out, send_sems, recv_sems, O_IN, R_IN)
        start2, finish2 = _pack_gather(pack_ref, wg_out, send_sems2, recv_sems2, [(O_OUT, R_BR)])

        def weights(phase):
            return pltpu.make_async_copy(wg_out.at[ord_ref[phase], pl.ds(O_IN, R_IN), :], w_vmem.at[phase % 2],
                                         sems.at[phase % 2])

        @pl.when((q == 0) & (i == 0))
        def _():
            start()
            weights(0).start()
            weights(0).wait()

        @pl.when((q == 1) & (i == 0))
        def _():
            arrive(0)
            start2()
            weights(1).start()
            weights(1).wait()
            arrive(1)
            weights(2).start()

        @pl.when((q == 2) & (i == 0))
        def _():
            weights(2).wait()
            arrive(2)
            weights(3).start()

        @pl.when((q == 3) & (i == 0))
        def _():
            weights(3).wait()

        @pl.when(q == 0)
        def _():
            xv = x_ref[...]
            r = lax.rsqrt(jnp.mean(xv * xv, axis=-1, keepdims=True) + EPS)
            h = xv * r * g_ref[...] * (1.0 + mod_ref[:, D:2 * D]) + mod_ref[:, 0:D]
            h_scr[rows, :] = h.astype(BF16)

        hb = h_scr[rows, :]
        slot = q % 2
        for k in range(wc // D):
            p_ref[:, k * D:(k + 1) * D] = _mm(hb, w_vmem[slot, k * D:(k + 1) * D, :]) + b_ref[:, k * D:(k + 1) * D]

        @pl.when((q == N_CHIPS - 1) & (i == nt - 1))
        def _():
            cp = pltpu.make_async_copy(h_scr, h_hbm, sems.at[0])
            cp.start()
            drain()
            finish2()
            cp.wait()

    hbm = BS(memory_space=pl.ANY)
    return pl.pallas_call(
        body, name="fwd_in", out_shape=(SDS((S, IN_COLS), F32), SDS((S, D), BF16), SDS(wg.shape, wg.dtype)),
        grid_spec=pltpu.PrefetchScalarGridSpec(
            num_scalar_prefetch=1, grid=(N_CHIPS, nt),
            in_specs=[BS((tmf, D), lambda q, i, o: (jnp.where(q == 0, i, nt - 1), 0)),
                      BS((1, 6 * D), lambda q, i, o: (0, 0)),
                      BS((1, D), lambda q, i, o: (0, 0)), hbm, BS((1, wc), lambda q, i, o: (0, o[q])), hbm],
            out_specs=(BS((tmf, wc), lambda q, i, o: (i, o[q])), hbm, hbm),
            scratch_shapes=[pltpu.VMEM((2, R_IN, D), BF16), pltpu.VMEM((S, D), BF16), pltpu.SemaphoreType.DMA((2,))]
            + _relay_sems() + _gather_sems(1)),
        input_output_aliases={4: 2},
        compiler_params=_params("arbitrary", "arbitrary"),
    )(order, x, mod, pre_tm, wg, b_in, pack)


def _lower_bound(lg_ref):
    l0, l1 = lg_ref[0:1, :], lg_ref[1:2, :]
    mx = jnp.maximum(l0, l1)
    e0, e1 = jnp.exp(l0 - mx), jnp.exp(l1 - mx)
    return e0 / (e0 + e1)


def _tri_masks():
    ri = lax.broadcasted_iota(jnp.int32, (CHUNK, CHUNK), 0)
    ci = lax.broadcasted_iota(jnp.int32, (CHUNK, CHUNK), 1)
    return (ri >= ci).astype(F32), (ci >= ri).astype(F32)


def _cumsum_mm(tri, g):
    tb = tri.astype(BF16)
    hi = g.astype(BF16)
    r1 = g - hi.astype(F32)
    mid = r1.astype(BF16)
    lo = (r1 - mid.astype(F32)).astype(BF16)
    return _mm(tb, hi) + _mm(tb, mid) + _mm(tb, lo)


def _hg_gates(q_r, f_r, lb, tril):
    sq = _sig(q_r)
    q = q_r * sq
    sf = _sig(f_r)
    f = lb + (1.0 - lb) * sf
    k = 1.0 - f
    g = jnp.log(f)
    b = _cumsum_mm(tril, g)
    b_last = _rowsum(g)
    row = lax.broadcasted_iota(jnp.int32, g.shape, 0)
    ref = _rowsum(jnp.where(row < CHUNK // 2, g, 0.0))
    e = jnp.exp(b)
    eq = jnp.exp(jnp.minimum(b - ref, 80.0))
    ek = jnp.exp(jnp.minimum(ref - b, 80.0))
    dd = jnp.exp(b_last - b)
    return dict(sq=sq, q=q, sf=sf, f=f, k=k, e=e, eq=eq, ek=ek, dd=dd, elast=jnp.exp(b_last),
                qe=q * e, qt=q * eq, kt=k * ek, kd=k * dd)


def _hgrn_fwd_call(p, logits, gn, wg, pack):
    S = p.shape[0]
    ncb = TB // CHUNK
    ranges = [(O_FF1, R_FF)]

    def body(q_ref, f_ref, v_ref, og_ref, lg_ref, gn_ref, wg_in, pack_ref, o_ref, oa_ref, st_ref, wg_out,
             st_scr, send_sems, recv_sems):
        start, finish = _pack_gather(pack_ref, wg_out, send_sems, recv_sems, ranges)

        @pl.when(pl.program_id(0) == 0)
        def _():
            start()
            st_scr[...] = jnp.zeros_like(st_scr)

        lb = _lower_bound(lg_ref)
        tril, _ = _tri_masks()

        def chunk(ci, carry):
            rows = pl.ds(pl.multiple_of(ci * CHUNK, CHUNK), CHUNK)
            st_ref[ci] = st_scr[...]
            t = _hg_gates(q_ref[rows, :], f_ref[rows, :], lb, tril)
            v = v_ref[rows, :]
            for h in range(HEADS):
                sl = slice(h * DK, (h + 1) * DK)
                stp = st_scr[:, sl]
                vb = v[:, sl].astype(BF16)
                inter = _mm(t["qe"][:, sl].astype(BF16), stp.astype(BF16), NT)
                a = jnp.where(tril > 0.5, _mm(t["qt"][:, sl].astype(BF16), t["kt"][:, sl].astype(BF16), NT), 0.0)
                o = inter + _mm(a.astype(BF16), vb)
                st_scr[:, sl] = stp * t["elast"][:, sl] + _mm(vb, t["kd"][:, sl].astype(BF16), TN)
                oh = o * lax.rsqrt(jnp.mean(o * o, axis=-1, keepdims=True) + EPS)
                og = og_ref[rows, sl]
                o_ref[rows, sl] = o
                oa_ref[rows, sl] = (oh * gn_ref[:, sl] * (og * _sig(og))).astype(BF16)
            return carry

        lax.fori_loop(0, ncb, chunk, 0)

        @pl.when(pl.program_id(0) == S // TB - 1)
        def _():
            finish()

    col = lambda j: BS((TB, D), lambda i, j=j: (i, j))
    hbm = BS(memory_space=pl.ANY)
    return pl.pallas_call(
        body, name="hgrn_fwd", grid=(S // TB,),
        out_shape=(SDS((S, D), F32), SDS((S, D), BF16), SDS((S // CHUNK, DK, D), F32), SDS(wg.shape, wg.dtype)),
        in_specs=[col(0), col(1), col(2), col(3), BS((2, D), lambda i: (0, 0)), BS((1, D), lambda i: (0, 0)),
                  hbm, hbm],
        out_specs=(BS((TB, D), lambda i: (i, 0)), BS((TB, D), lambda i: (i, 0)),
                   BS((ncb, DK, D), lambda i: (i, 0, 0)), hbm),
        scratch_shapes=[pltpu.VMEM((DK, D), F32)] + _gather_sems(len(ranges)),
        input_output_aliases={6: 3},
        compiler_params=_params("arbitrary"),
    )(p, p, p, p, logits, gn, wg, pack)


def _layernorm_stats(uc):
    mu = jnp.mean(uc, axis=-1, keepdims=True)
    xc = uc - mu
    rs = lax.rsqrt(jnp.mean(xc * xc, axis=-1, keepdims=True) + EPS)
    return xc * rs, rs


EXT = HALO + TM + 8


def _fill_shifted(ext, shifted):
    for m in range(1, 8):
        shifted[m - 1] = ext[m:m + HALO + TM, :]


def _window(ext, shifted, s0, n):
    m = s0 % 8
    q = s0 - m
    return ext[q:q + n, :] if m == 0 else shifted[m - 1, q:q + n, :]


def _conv_fwd_call(p, dw, db, ln_g, ln_b, wg, pack):
    S = p.shape[0]
    ranges = [(O_FF2, R_FF), (O_BRA, 2 * R_BR)]

    def body(cv_ref, cg_ref, dw_ref, db_ref, g_ref, b_ref, wg_in, pack_ref, u_ref, uc_ref, cb_ref, wg_out,
             uext, ush, send_sems, recv_sems):
        start, finish = _pack_gather(pack_ref, wg_out, send_sems, recv_sems, ranges)

        @pl.when(pl.program_id(0) == 0)
        def _():
            start()
            uext[0:HALO, :] = jnp.zeros((HALO, D), F32)
            uext[HALO + TM:EXT, :] = jnp.zeros((EXT - HALO - TM, D), F32)

        u = cv_ref[...] * _sig(cg_ref[...])
        uext[HALO:HALO + TM, :] = u
        u_ref[...] = u
        _fill_shifted(uext, ush)
        for rb in range(TM // SUB):
            acc = jnp.broadcast_to(db_ref[...], (SUB, D))
            for j in range(CONV_K):
                s0 = HALO - (CONV_K - 1) + j + rb * SUB
                acc = acc + dw_ref[j:j + 1, :] * _window(uext, ush, s0, SUB)
            uc_ref[rb * SUB:(rb + 1) * SUB, :] = acc
            xh, _ = _layernorm_stats(acc)
            ln = xh * g_ref[...] + b_ref[...]
            cb_ref[rb * SUB:(rb + 1) * SUB, :] = (ln * _sig(ln)).astype(BF16)
        uext[0:HALO, :] = uext[TM:TM + HALO, :]

        @pl.when(pl.program_id(0) == S // TM - 1)
        def _():
            finish()

    vec = BS((1, D), lambda i: (0, 0))
    hbm = BS(memory_space=pl.ANY)
    return pl.pallas_call(
        body, name="conv_fwd", grid=(S // TM,),
        out_shape=(SDS((S, D), F32), SDS((S, D), F32), SDS((S, D), BF16), SDS(wg.shape, wg.dtype)),
        in_specs=[BS((TM, D), lambda i: (i, 4)), BS((TM, D), lambda i: (i, 5)),
                  BS((CONV_K, D), lambda i: (0, 0)), vec, vec, vec, hbm, hbm],
        out_specs=(BS((TM, D), lambda i: (i, 0)),) * 3 + (hbm,),
        scratch_shapes=[pltpu.VMEM((EXT, D), F32), pltpu.VMEM((7, HALO + TM, D), F32)] + _gather_sems(len(ranges)),
        input_output_aliases={6: 3},
        compiler_params=_params("arbitrary"),
    )(p, p, dw, db, ln_g, ln_b, wg, pack)


def _mm_rows(a, w_ref):
    acc = _mm(a[:, 0:R_BR], w_ref[0])
    for k in range(1, N_CHIPS):
        acc = acc + _mm(a[:, k * R_BR:(k + 1) * R_BR], w_ref[k])
    return acc


def _mm_rows_t(a, w_ref):
    return jnp.concatenate([_mm(a, w_ref[k], NT) for k in range(N_CHIPS)], axis=1)


def _br_spec(off):
    return BS((N_CHIPS, R_BR, D), lambda i: (0, off // R_BR, 0))


def _merge_fwd_call(oa, cb, p, x, mod, post_tm, pre_cm, wg):
    S = x.shape[0]

    def body(oa_ref, cb_ref, ga_ref, gb_ref, x_ref, mod_ref, post_ref, pre_ref, wa_ref, wb_ref, wo_ref,
             ya_ref, yb_ref, mg_ref, y_ref, x2_ref, h2_ref):
        ya = _mm_rows(oa_ref[...], wa_ref)
        yb = _mm_rows(cb_ref[...], wb_ref)
        ya_ref[...] = ya.astype(BF16)
        yb_ref[...] = yb.astype(BF16)
        mg = (_sig(ga_ref[...]) * ya + _sig(gb_ref[...]) * yb).astype(BF16)
        mg_ref[...] = mg
        y = _mm_rows(mg, wo_ref)
        y_ref[...] = y
        n = y * lax.rsqrt(jnp.mean(y * y, axis=-1, keepdims=True) + EPS) * post_ref[...]
        x2 = x_ref[...] + mod_ref[:, 2 * D:3 * D] * n
        x2_ref[...] = x2
        r2 = lax.rsqrt(jnp.mean(x2 * x2, axis=-1, keepdims=True) + EPS)
        h2 = x2 * r2 * pre_ref[...] * (1.0 + mod_ref[:, 4 * D:5 * D]) + mod_ref[:, 3 * D:4 * D]
        h2_ref[...] = h2.astype(BF16)

    tile = BS((TM, D), lambda i: (i, 0))
    vec = BS((1, D), lambda i: (0, 0))
    return pl.pallas_call(
        body, name="merge_fwd", grid=(S // TM,),
        out_shape=(SDS((S, D), BF16), SDS((S, D), BF16), SDS((S, D), BF16), SDS((S, D), F32), SDS((S, D), F32),
                   SDS((S, D), BF16)),
        in_specs=[tile, tile, BS((TM, D), lambda i: (i, 6)), BS((TM, D), lambda i: (i, 7)), tile,
                  BS((1, 6 * D), lambda i: (0, 0)), vec, vec, _br_spec(O_BRA), _br_spec(O_BRB), _br_spec(O_OUT)],
        out_specs=(tile,) * 6,
        compiler_params=_params("arbitrary"),
    )(oa, cb, p, p, x, mod, post_tm, pre_cm, wg, wg, wg)


def _ffn_call(h2, x2, target, mod, post_cm, pre_cm, wg):
    S = x2.shape[0]

    def body(h2_ref, x2_ref, t_ref, mod_ref, post_ref, pre_ref, w_hbm,
             z_ref, da_ref, dy2_ref, dx2_ref, acc_ref, w1_v, w2_v, ra_scr, sems):
        @pl.when(pl.program_id(0) == 0)
        def _():
            c1 = _load_rows(w_hbm, w1_v, sems.at[0], O_FF1)
            c2 = _load_rows(w_hbm, w2_v, sems.at[1], O_FF2)
            c1.wait()
            c2.wait()
            acc_ref[...] = jnp.zeros_like(acc_ref)

        h2 = h2_ref[...]
        for k in range(N_CHIPS):
            ra = jnp.maximum(_mm(h2, w1_v[k]), 0.0)
            ra_scr[:, k * D:(k + 1) * D] = ra
            z_ref[:, k * D:(k + 1) * D] = (ra * ra).astype(BF16)
        y2 = _mm(z_ref[:, 0:D], w2_v[0])
        for k in range(1, N_CHIPS):
            y2 = y2 + _mm(z_ref[:, k * D:(k + 1) * D], w2_v[k])
        ry = lax.rsqrt(jnp.mean(y2 * y2, axis=-1, keepdims=True) + EPS)
        yn = y2 * ry
        n = yn * post_ref[...]
        g2 = mod_ref[:, 5 * D:6 * D]
        x2 = x2_ref[...]
        err = x2 + g2 * n - t_ref[...]
        acc_ref[5:6, :] += _rowsum(err * err) * (0.5 / D)
        dout = err * (1.0 / D)
        acc_ref[0:1, :] += _rowsum(dout * n)
        dn = dout * g2
        acc_ref[1:2, :] += _rowsum(dn * yn)
        dyn = dn * post_ref[...]
        dy2 = (ry * (dyn - yn * jnp.mean(dyn * yn, axis=-1, keepdims=True))).astype(BF16)
        dy2_ref[...] = dy2
        for k in range(N_CHIPS):
            dz = _mm(dy2, w2_v[k], NT)
            da_ref[:, k * D:(k + 1) * D] = (dz * (2.0 * ra_scr[:, k * D:(k + 1) * D])).astype(BF16)
        dh2 = jnp.zeros((TM, D), F32)
        for k in range(N_CHIPS):
            dh2 = dh2 + _mm(da_ref[:, k * D:(k + 1) * D], w1_v[k], NT)
        r2 = lax.rsqrt(jnp.mean(x2 * x2, axis=-1, keepdims=True) + EPS)
        xn = x2 * r2
        yv = xn * pre_ref[...]
        acc_ref[2:3, :] += _rowsum(dh2)
        acc_ref[3:4, :] += _rowsum(dh2 * yv)
        dyv = dh2 * (1.0 + mod_ref[:, 4 * D:5 * D])
        acc_ref[4:5, :] += _rowsum(dyv * xn)
        dxn = dyv * pre_ref[...]
        dx2_ref[...] = dout + r2 * (dxn - xn * jnp.mean(dxn * xn, axis=-1, keepdims=True))

    tile = BS((TM, D), lambda i: (i, 0))
    wide = BS((TM, D_FF), lambda i: (i, 0))
    vec = BS((1, D), lambda i: (0, 0))
    return pl.pallas_call(
        body, name="ffn_fwd_bwd", grid=(S // TM,),
        out_shape=(SDS((S, D_FF), BF16), SDS((S, D_FF), BF16), SDS((S, D), BF16), SDS((S, D), F32),
                   SDS((8, D), F32)),
        in_specs=[tile, tile, tile, BS((1, 6 * D), lambda i: (0, 0)), vec, vec, BS(memory_space=pl.ANY)],
        out_specs=(wide, wide, tile, tile, BS((8, D), lambda i: (0, 0))),
        scratch_shapes=[pltpu.VMEM((N_CHIPS, R_FF, D), BF16), pltpu.VMEM((N_CHIPS, R_FF, D), BF16),
                        pltpu.VMEM((TM, D_FF), F32),
                        pltpu.SemaphoreType.DMA((2,))],
        compiler_params=_params("arbitrary"),
    )(h2, x2, target, mod, post_cm, pre_cm, wg)


def _merge_bwd_call(dx2, y, ya, yb, p, mod, post_tm, wg, g):
    S = y.shape[0]

    def body(dx2_ref, y_ref, ya_ref, yb_ref, ga_ref, gb_ref, mod_ref, post_ref, wa_ref, wb_ref, wo_ref, g_ref,
             dy_ref, dya_ref, dyb_ref, doa_ref, dcb_ref, dpg_ref, acc_ref, bsum_ref, hr_ref, send_sems, recv_sems):
        start, finish = _halves_exchange(g_ref, hr_ref, send_sems, recv_sems)

        @pl.when(pl.program_id(0) == 0)
        def _():
            start()
            acc_ref[...] = jnp.zeros_like(acc_ref)
            bsum_ref[...] = jnp.zeros_like(bsum_ref)

        y = y_ref[...]
        ry = lax.rsqrt(jnp.mean(y * y, axis=-1, keepdims=True) + EPS)
        yn = y * ry
        dx2 = dx2_ref[...]
        acc_ref[0:1, :] += _rowsum(dx2 * (yn * post_ref[...]))
        dn = dx2 * mod_ref[:, 2 * D:3 * D]
        acc_ref[1:2, :] += _rowsum(dn * yn)
        dyn = dn * post_ref[...]
        dy = (ry * (dyn - yn * jnp.mean(dyn * yn, axis=-1, keepdims=True))).astype(BF16)
        dy_ref[...] = dy
        dmg = _mm_rows_t(dy, wo_ref)
        sa, sb = _sig(ga_ref[...]), _sig(gb_ref[...])
        dya = (dmg * sa).astype(BF16)
        dyb = (dmg * sb).astype(BF16)
        dya_ref[...] = dya
        dyb_ref[...] = dyb
        dga = dmg * ya_ref[...].astype(F32) * (sa * (1.0 - sa))
        dgb = dmg * yb_ref[...].astype(F32) * (sb * (1.0 - sb))
        dpg_ref[:, 0:D] = dga.astype(BF16)
        dpg_ref[:, D:2 * D] = dgb.astype(BF16)
        bsum_ref[:, 0:D] += _rowsum(dga)
        bsum_ref[:, D:2 * D] += _rowsum(dgb)
        doa_ref[...] = _mm_rows_t(dya, wa_ref)
        dcb_ref[...] = _mm_rows_t(dyb, wb_ref)

        @pl.when(pl.program_id(0) == S // TM - 1)
        def _():
            finish()

    tile = BS((TM, D), lambda i: (i, 0))
    vec = BS((1, D), lambda i: (0, 0))
    return pl.pallas_call(
        body, name="merge_bwd", grid=(S // TM,),
        out_shape=(SDS((S, D), BF16), SDS((S, D), BF16), SDS((S, D), BF16), SDS((S, D), F32), SDS((S, D), F32),
                   SDS((S, 2 * D), BF16), SDS((8, D), F32), SDS((1, 2 * D), F32),
                   SDS((N_CHIPS,) + g.shape[2:], g.dtype)),
        in_specs=[tile, tile, tile, tile, BS((TM, D), lambda i: (i, 6)), BS((TM, D), lambda i: (i, 7)),
                  BS((1, 6 * D), lambda i: (0, 0)), vec, _br_spec(O_BRA), _br_spec(O_BRB), _br_spec(O_OUT),
                  BS(memory_space=pl.ANY)],
        out_specs=(tile, tile, tile, tile, tile, BS((TM, 2 * D), lambda i: (i, 0)),
                   BS((8, D), lambda i: (0, 0)), BS((1, 2 * D), lambda i: (0, 0)), BS(memory_space=pl.ANY)),
        scratch_shapes=_halves_sems(),
        compiler_params=_params("arbitrary"),
    )(dx2, y, ya, yb, p, p, mod, post_tm, wg, wg, wg, g)


def _hgrn_bwd_call(p, o, doa, st, logits, gn, part, g):
    S = p.shape[0]
    nb = S // TB
    ncb = TB // CHUNK

    def body(q_ref, f_ref, v_ref, og_ref, o_ref, doa_ref, st_ref, lg_ref, gn_ref, part_ref, g_ref,
             dp_ref, bsum_ref, dlg_ref, dgn_ref, recv_ref, hr_ref,
             dst_scr, dlb_scr, dqe_s, dqt_s, dkt_s, dkd_s, dv_s, dog_s, dble_s, send_sems, recv_sems, hs, hr):
        i = pl.program_id(0)
        start, finish = _chip_exchange(part_ref, recv_ref, send_sems, recv_sems)
        start_h, finish_h = _halves_exchange(g_ref, hr_ref, hs, hr)

        @pl.when(i == 0)
        def _():
            start_h()
            start()
            dst_scr[...] = jnp.zeros_like(dst_scr)
            dlb_scr[...] = jnp.zeros_like(dlb_scr)
            bsum_ref[...] = jnp.zeros_like(bsum_ref)
            dgn_ref[...] = jnp.zeros_like(dgn_ref)

        lb = _lower_bound(lg_ref)
        tril, triu = _tri_masks()

        def chunk(tt, carry):
            ci = ncb - 1 - tt
            rows = pl.ds(pl.multiple_of(ci * CHUNK, CHUNK), CHUNK)
            q_r, f_r = q_ref[rows, :], f_ref[rows, :]
            t = _hg_gates(q_r, f_r, lb, tril)
            v = v_ref[rows, :]
            for h in range(HEADS):
                sl = slice(h * DK, (h + 1) * DK)
                stp = st_ref[ci, :, sl]
                stb = stp.astype(BF16)
                qeb = t["qe"][:, sl].astype(BF16)
                qtb = t["qt"][:, sl].astype(BF16)
                ktb = t["kt"][:, sl].astype(BF16)
                kdb = t["kd"][:, sl].astype(BF16)
                vb = v[:, sl].astype(BF16)
                a = jnp.where(tril > 0.5, _mm(qtb, ktb, NT), 0.0)
                o_h = o_ref[rows, sl]
                rinv = lax.rsqrt(jnp.mean(o_h * o_h, axis=-1, keepdims=True) + EPS)
                oh = o_h * rinv
                og = og_ref[rows, sl]
                so = _sig(og)
                d_oa = doa_ref[rows, sl]
                don = d_oa * (og * so)
                dog_s[:, sl] = d_oa * (oh * gn_ref[:, sl]) * _dsilu(og, so)
                dgn_ref[:, sl] += _rowsum(don * oh)
                doh = don * gn_ref[:, sl]
                do = (rinv * (doh - oh * jnp.mean(doh * oh, axis=-1, keepdims=True))).astype(BF16)
                dqe_s[:, sl] = _mm(do, stb, NN)
                dstp = _mm(do, qeb, TN)
                dab = jnp.where(tril > 0.5, _mm(do, vb, NT), 0.0).astype(BF16)
                dqt_s[:, sl] = _mm(dab, ktb, NN)
                dkt_s[:, sl] = _mm(dab, qtb, TN)
                dstn = dst_scr[:, sl]
                dsb = dstn.astype(BF16)
                dkd_s[:, sl] = _mm(vb, dsb, NN)
                dv_s[:, sl] = _mm(a.astype(BF16), do, TN) + _mm(kdb, dsb, NT)
                el = t["elast"][:, sl]
                dst_scr[:, sl] = dstn * el + dstp
                dble_s[:, sl] = el * _rowsum(stp * dstn)
            dqe, dqt, dkt, dkd = dqe_s[...], dqt_s[...], dkt_s[...], dkd_s[...]
            dq = dqe * t["e"] + dqt * t["eq"]
            dk = dkt * t["ek"] + dkd * t["dd"]
            dkk = dkd * t["kd"]
            qt_r = t["qt"].astype(BF16).astype(F32)
            kt_r = t["kt"].astype(BF16).astype(F32)
            dbv = dqe * t["qe"] + dqt * qt_r - dkt * kt_r - dkk
            dg = _cumsum_mm(triu, dbv) + (_rowsum(dkk) + dble_s[...])
            df = dg / t["f"] - dk
            sf = t["sf"]
            dlb_scr[...] += _rowsum(df * (1.0 - sf))
            dqr = dq * _dsilu(q_r, t["sq"])
            dfr = df * (1.0 - lb) * (sf * (1.0 - sf))
            dvv, dog = dv_s[...], dog_s[...]
            dp_ref[rows, 0:D] = dqr.astype(BF16)
            dp_ref[rows, D:2 * D] = dfr.astype(BF16)
            dp_ref[rows, 2 * D:3 * D] = dvv.astype(BF16)
            dp_ref[rows, 3 * D:4 * D] = dog.astype(BF16)
            bsum_ref[:, 0:D] += _rowsum(dqr)
            bsum_ref[:, D:2 * D] += _rowsum(dfr)
            bsum_ref[:, 2 * D:3 * D] += _rowsum(dvv)
            bsum_ref[:, 3 * D:4 * D] += _rowsum(dog)
            return carry

        lax.fori_loop(0, ncb, chunk, 0)

        dl = dlb_scr[...] * lb * (1.0 - lb)
        dlg_ref[0:1, :] = dl
        dlg_ref[1:2, :] = -dl

        @pl.when(i == nb - 1)
        def _():
            finish_h()
            finish()

    col = lambda j: BS((TB, D), lambda i, j=j: (nb - 1 - i, j))
    rev = BS((TB, D), lambda i: (nb - 1 - i, 0))
    cd = pltpu.VMEM((CHUNK, D), F32)
    return pl.pallas_call(
        body, name="hgrn_bwd", grid=(nb,),
        out_shape=(SDS((S, 4 * D), BF16), SDS((1, 4 * D), F32), SDS((2, D), F32), SDS((1, D), F32),
                   SDS((3,) + part.shape[1:], part.dtype), SDS((N_CHIPS,) + g.shape[2:], g.dtype)),
        in_specs=[col(0), col(1), col(2), col(3), rev, rev, BS((ncb, DK, D), lambda i: (nb - 1 - i, 0, 0)),
                  BS((2, D), lambda i: (0, 0)), BS((1, D), lambda i: (0, 0)), BS(memory_space=pl.ANY),
                  BS(memory_space=pl.ANY)],
        out_specs=(BS((TB, 4 * D), lambda i: (nb - 1 - i, 0)), BS((1, 4 * D), lambda i: (0, 0)),
                   BS((2, D), lambda i: (0, 0)), BS((1, D), lambda i: (0, 0)), BS(memory_space=pl.ANY),
                   BS(memory_space=pl.ANY)),
        scratch_shapes=[pltpu.VMEM((DK, D), F32), pltpu.VMEM((1, D), F32), cd, cd, cd, cd, cd, cd,
                        pltpu.VMEM((1, D), F32)] + _exchange_sems() + _halves_sems(),
        compiler_params=_params("arbitrary"),
    )(p, p, p, p, o, doa, st, logits, gn, part, g)


def _conv_bwd_call(dcb, uc, u, p, dw, ln_g, ln_b, part):
    S = uc.shape[0]
    nb = S // TM
    hb = TM // HALO

    def body(dcb_ref, uc_ref, u_ref, uh_ref, cv_ref, cg_ref, dw_ref, g_ref, b_ref, part_ref,
             dp_ref, bsum_ref, ddw_ref, acc_ref, recv_ref, uext, dext, ush, dsh, send_sems, recv_sems):
        i = pl.program_id(0)
        start, finish = _chip_exchange(part_ref, recv_ref, send_sems, recv_sems)

        @pl.when(i == 0)
        def _():
            start()
            dext[TM:EXT, :] = jnp.zeros((EXT - TM, D), F32)
            uext[HALO + TM:EXT, :] = jnp.zeros((EXT - HALO - TM, D), F32)
            bsum_ref[...] = jnp.zeros_like(bsum_ref)
            ddw_ref[...] = jnp.zeros_like(ddw_ref)
            acc_ref[...] = jnp.zeros_like(acc_ref)

        first_tile = (nb - 1 - i) == 0
        uext[0:HALO, :] = jnp.where(first_tile, 0.0, uh_ref[...])
        uext[HALO:HALO + TM, :] = u_ref[...]
        _fill_shifted(uext, ush)

        for rb in range(TM // SUB):
            rs_ = slice(rb * SUB, (rb + 1) * SUB)
            xh, rs = _layernorm_stats(uc_ref[rs_, :])
            ln = xh * g_ref[...] + b_ref[...]
            dln = dcb_ref[rs_, :] * _dsilu(ln, _sig(ln))
            acc_ref[1:2, :] += _rowsum(dln * xh)
            acc_ref[2:3, :] += _rowsum(dln)
            dxh = dln * g_ref[...]
            duc = rs * (dxh - jnp.mean(dxh, axis=-1, keepdims=True)
                        - xh * jnp.mean(dxh * xh, axis=-1, keepdims=True))
            dext[rs_, :] = duc
            acc_ref[0:1, :] += _rowsum(duc)
        _fill_shifted(dext, dsh)

        for j in range(CONV_K):
            part = jnp.zeros((SUB, D), F32)
            for rb in range(TM // SUB):
                s0 = HALO - (CONV_K - 1) + j + rb * SUB
                part = part + dext[rb * SUB:(rb + 1) * SUB, :] * _window(uext, ush, s0, SUB)
            ddw_ref[j:j + 1, :] += _rowsum(part)

        for rb in range(TM // SUB):
            rs_ = slice(rb * SUB, (rb + 1) * SUB)
            du = jnp.zeros((SUB, D), F32)
            for j in range(CONV_K):
                s0 = rb * SUB + (CONV_K - 1) - j
                du = du + dw_ref[j:j + 1, :] * _window(dext, dsh, s0, SUB)
            cg = cg_ref[rs_, :]
            sg = _sig(cg)
            dcv = du * sg
            dcg = du * cv_ref[rs_, :] * (sg * (1.0 - sg))
            dp_ref[rs_, 0:D] = dcv.astype(BF16)
            dp_ref[rs_, D:2 * D] = dcg.astype(BF16)
            bsum_ref[:, 0:D] += _rowsum(dcv)
            bsum_ref[:, D:2 * D] += _rowsum(dcg)

        dext[TM:TM + HALO, :] = dext[0:HALO, :]

        @pl.when(i == nb - 1)
        def _():
            finish()

    rev = BS((TM, D), lambda i: (nb - 1 - i, 0))
    vec = BS((1, D), lambda i: (0, 0))
    return pl.pallas_call(
        body, name="conv_bwd", grid=(nb,),
        out_shape=(SDS((S, 2 * D), BF16), SDS((1, 2 * D), F32), SDS((32, D), F32), SDS((8, D), F32),
                   SDS((3,) + part.shape[1:], part.dtype)),
        in_specs=[rev, rev, rev, BS((HALO, D), lambda i: (jnp.maximum((nb - 1 - i) * hb - 1, 0), 0)),
                  BS((TM, D), lambda i: (nb - 1 - i, 4)), BS((TM, D), lambda i: (nb - 1 - i, 5)),
                  BS((CONV_K, D), lambda i: (0, 0)), vec, vec, BS(memory_space=pl.ANY)],
        out_specs=(BS((TM, 2 * D), lambda i: (nb - 1 - i, 0)), BS((1, 2 * D), lambda i: (0, 0)),
                   BS((32, D), lambda i: (0, 0)), BS((8, D), lambda i: (0, 0)), BS(memory_space=pl.ANY)),
        scratch_shapes=[pltpu.VMEM((EXT, D), F32), pltpu.VMEM((EXT, D), F32),
                        pltpu.VMEM((7, HALO + TM, D), F32), pltpu.VMEM((7, HALO + TM, D), F32)] + _exchange_sems(),
        compiler_params=_params("arbitrary"),
    )(dcb, uc, u, u, p, p, dw, ln_g, ln_b, part)


def _in_bwd_call(dp_hg, dp_cv, dp_gt, x, dx2, mod, pre_tm, wg, part, full_a, full_b):
    S = x.shape[0]
    tm = TM

    def body(hg_ref, cv_ref, gt_ref, x_ref, dx2_ref, mod_ref, g_ref, w_hbm, part_ref, fa_in, fb_in,
             gx_ref, acc_ref, recv_ref, fa_out, fb_out, w_vmem, sem, send_sems, recv_sems, sa, ra, sb, rb):
        start, finish = _chip_exchange(part_ref, recv_ref, send_sems, recv_sems)
        start_a, finish_a = _join_exchange(fa_in, fa_out, sa, ra)
        start_b, finish_b = _join_exchange(fb_in, fb_out, sb, rb)

        @pl.when(pl.program_id(0) == 0)
        def _():
            start_a()
            start_b()
            start()
            _load_rows(w_hbm, w_vmem, sem, O_IN).wait()
            acc_ref[...] = jnp.zeros_like(acc_ref)

        dh = jnp.zeros((tm, D), F32)
        for k in range(IN_COLS // D):
            src, kk = ((hg_ref, k), (cv_ref, k - 4), (gt_ref, k - 6))[0 if k < 4 else (1 if k < 6 else 2)]
            dh = dh + _mm(src[:, kk * D:(kk + 1) * D], w_vmem[k // 2, (k % 2) * D:(k % 2 + 1) * D, :], NT)
        xv = x_ref[...]
        r = lax.rsqrt(jnp.mean(xv * xv, axis=-1, keepdims=True) + EPS)
        xn = xv * r
        yv = xn * g_ref[...]
        acc_ref[0:1, :] += _rowsum(dh)
        acc_ref[1:2, :] += _rowsum(dh * yv)
        dyv = dh * (1.0 + mod_ref[:, D:2 * D])
        acc_ref[2:3, :] += _rowsum(dyv * xn)
        dxn = dyv * g_ref[...]
        gx_ref[...] = dx2_ref[...] + r * (dxn - xn * jnp.mean(dxn * xn, axis=-1, keepdims=True))

        @pl.when(pl.program_id(0) == S // tm - 1)
        def _():
            finish_a()
            finish_b()
            finish()

    tile = BS((tm, D), lambda i: (i, 0))
    hbm = BS(memory_space=pl.ANY)
    return pl.pallas_call(
        body, name="in_bwd", grid=(S // tm,),
        out_shape=(SDS((S, D), F32), SDS((8, D), F32), SDS((3,) + part.shape[1:], part.dtype),
                   SDS(full_a.shape, full_a.dtype), SDS(full_b.shape, full_b.dtype)),
        in_specs=[BS((tm, 4 * D), lambda i: (i, 0)), BS((tm, 2 * D), lambda i: (i, 0)),
                  BS((tm, 2 * D), lambda i: (i, 0)), tile, tile, BS((1, 6 * D), lambda i: (0, 0)),
                  BS((1, D), lambda i: (0, 0)), hbm, hbm, hbm, hbm],
        out_specs=(tile, BS((8, D), lambda i: (0, 0)), hbm, hbm, hbm),
        scratch_shapes=[pltpu.VMEM((N_CHIPS, R_IN, D), BF16), pltpu.SemaphoreType.DMA] + _exchange_sems()
        + _join_sems() + _join_sems(),
        input_output_aliases={9: 3, 10: 4},
        compiler_params=_params("arbitrary"),
    )(dp_hg, dp_cv, dp_gt, x, dx2, mod, pre_tm, wg, part, full_a, full_b)


def _wgrad_call(gp, a, b, name, bm, place, rows):
    S, M = a.shape
    N = b.shape[1]
    bk = min(S, 1024)
    nk = S // bk

    def body(a_ref, b_ref, *rest):
        o_ref, acc = rest[-2], rest[-1]
        k = pl.program_id(2)

        @pl.when(k == 0)
        def _():
            acc[...] = jnp.zeros_like(acc)

        acc[...] += _mm(a_ref[...], b_ref[...], TN)

        @pl.when(k == nk - 1)
        def _():
            o_ref[...] = acc[...].astype(BF16)

    in_specs = [BS((bk, bm), lambda i, j, k: (k, i)), BS((bk, D), lambda i, j, k: (k, j))]
    args = [a, b]
    if gp is not None:
        in_specs.append(BS(memory_space=pl.ANY))
        args.append(gp)
    return pl.pallas_call(
        body, name=name, grid=(M // bm, N // D, nk),
        out_shape=SDS((N_CHIPS, rows, D), BF16),
        in_specs=in_specs,
        out_specs=BS((None, bm, D), lambda i, j, k: (*place(i, j), 0)),
        scratch_shapes=[pltpu.VMEM((bm, D), F32)],
        input_output_aliases={} if gp is None else {2: 0},
        compiler_params=_params("parallel", "parallel", "arbitrary"),
    )(*args)


def _wgrad_rows_call(gp, a, b, name, blk):
    S = a.shape[0]
    bk = min(S, 1024)
    nk = S // bk

    def body(a_ref, b_ref, *rest):
        o_ref, acc = rest[-2], rest[-1]
        k = pl.program_id(0)

        @pl.when(k == 0)
        def _():
            acc[...] = jnp.zeros_like(acc)

        acc[...] += _mm(a_ref[...], b_ref[...], TN)

        @pl.when(k == nk - 1)
        def _():
            for c in range(N_CHIPS):
                o_ref[c] = acc[c * R_BR:(c + 1) * R_BR, :].astype(BF16)

    in_specs = [BS((bk, D), lambda k: (k, 0)), BS((bk, D), lambda k: (k, 0))]
    args = [a, b]
    if gp is not None:
        in_specs.append(BS(memory_space=pl.ANY))
        args.append(gp)
    return pl.pallas_call(
        body, name=name, grid=(nk,),
        out_shape=SDS((N_CHIPS, 3 * R_BR, D), BF16),
        in_specs=in_specs,
        out_specs=BS((N_CHIPS, R_BR, D), lambda k: (0, blk, 0)),
        scratch_shapes=[pltpu.VMEM((D, D), F32)],
        input_output_aliases={} if gp is None else {2: 0},
        compiler_params=_params("arbitrary"),
    )(*args)


def _outer_call(cact, dmod):
    n = dmod.shape[1]

    def body(a_ref, b_ref, o_ref):
        o_ref[...] = _mm(a_ref[...], b_ref[...], TN, HI)

    return pl.pallas_call(
        body, name="wgrad_ada", out_shape=SDS((D, n), F32),
        compiler_params=pltpu.CompilerParams(vmem_limit_bytes=VMEM_LIMIT),
    )(cact, dmod)


def _adamw_call(w, g, m, v, name):
    R, C = w.shape
    tr = R
    while tr * C > 512 * 1024 and tr % 16 == 0:
        tr //= 2
    c1 = 1.0 - ADAM_B1 ** ADAM_STEP
    c2 = 1.0 - ADAM_B2 ** ADAM_STEP

    def body(w_ref, g_ref, m_ref, v_ref, d_ref, m2_ref, v2_ref):
        g = g_ref[...]
        m2 = ADAM_B1 * m_ref[...] + (1.0 - ADAM_B1) * g
        v2 = ADAM_B2 * v_ref[...] + (1.0 - ADAM_B2) * (g * g)
        m2_ref[...] = m2
        v2_ref[...] = v2
        d_ref[...] = -ADAM_LR * ((m2 / c1) / (jnp.sqrt(v2 / c2) + ADAM_EPS) + ADAM_WD * w_ref[...])

    tile = BS((tr, C), lambda i: (i, 0))
    return pl.pallas_call(
        body, name=name, grid=(R // tr,), out_shape=(SDS((R, C), F32),) * 3,
        in_specs=[tile] * 4, out_specs=(tile,) * 3, compiler_params=_params("parallel"),
    )(w, g, m, v)


def _adamw_rows_call(ws, g, ms, vs, name):
    k = len(ws)
    r = ws[0].shape[0]
    c1 = 1.0 - ADAM_B1 ** ADAM_STEP
    c2 = 1.0 - ADAM_B2 ** ADAM_STEP

    def body(g_ref, *refs):
        ins, outs = refs[:3 * k], refs[3 * k:]
        for j in range(k):
            w_ref, m_ref, v_ref = ins[j], ins[k + j], ins[2 * k + j]
            d_ref, m2_ref, v2_ref = outs[j], outs[k + j], outs[2 * k + j]
            g = g_ref[j * r:(j + 1) * r, :]
            m2 = ADAM_B1 * m_ref[...] + (1.0 - ADAM_B1) * g
            v2 = ADAM_B2 * v_ref[...] + (1.0 - ADAM_B2) * (g * g)
            m2_ref[...] = m2
            v2_ref[...] = v2
            d_ref[...] = -ADAM_LR * ((m2 / c1) / (jnp.sqrt(v2 / c2) + ADAM_EPS) + ADAM_WD * w_ref[...])

    out = pl.pallas_call(
        body, name=name, out_shape=(SDS(ws[0].shape, F32),) * (3 * k),
        compiler_params=pltpu.CompilerParams(vmem_limit_bytes=VMEM_LIMIT),
    )(g, *ws, *ms, *vs)
    return out[:k], out[k:2 * k], out[2 * k:]


def _adamw_gather_call(w, g, m, v, srows, name):
    R, C = w.shape
    tr = R
    while tr * C > 512 * 1024 and tr % 16 == 0:
        tr //= 2
    nsteps = R // tr
    mr = srows.shape[0]
    c1 = 1.0 - ADAM_B1 ** ADAM_STEP
    c2 = 1.0 - ADAM_B2 ** ADAM_STEP

    def body(w_ref, g_ref, m_ref, v_ref, s_ref, d_ref, m2_ref, v2_ref, all_ref, sum_ref,
             x_scr, out_scr, send_sems, recv_sems, local_sem):
        i = pl.program_id(0)
        start, finish = _allgather_parts(x_scr, out_scr, send_sems, recv_sems, local_sem)

        @pl.when(i == 0)
        def _():
            x_scr[...] = s_ref[...]
            start()

        g = g_ref[...]
        m2 = ADAM_B1 * m_ref[...] + (1.0 - ADAM_B1) * g
        v2 = ADAM_B2 * v_ref[...] + (1.0 - ADAM_B2) * (g * g)
        m2_ref[...] = m2
        v2_ref[...] = v2
        d_ref[...] = -ADAM_LR * ((m2 / c1) / (jnp.sqrt(v2 / c2) + ADAM_EPS) + ADAM_WD * w_ref[...])

        @pl.when(i == nsteps - 1)
        def _():
            finish()
            all_ref[...] = out_scr[...]
            acc = out_scr[0:mr, :]
            for d in range(1, N_DEV):
                acc = acc + out_scr[d * mr:(d + 1) * mr, :]
            sum_ref[...] = acc

    tile = BS((tr, C), lambda i: (i, 0))
    return pl.pallas_call(
        body, name=name, grid=(nsteps,),
        out_shape=(SDS((R, C), F32),) * 3 + (SDS((N_DEV * mr, D), F32), SDS((mr, D), F32)),
        in_specs=[tile] * 4 + [BS((mr, D), lambda i: (0, 0))],
        out_specs=(tile,) * 3 + (BS((N_DEV * mr, D), lambda i: (0, 0)), BS((mr, D), lambda i: (0, 0))),
        scratch_shapes=[pltpu.VMEM((mr, D), F32), pltpu.VMEM((N_DEV * mr, D), F32)] + _allgather_sems(),
        compiler_params=_params("arbitrary"),
    )(w, g, m, v, srows)


def _rs_begin(g, c_idx, tag):
    n = g.shape[1]
    g = g.reshape(N_CHIPS, 2, n // 2, D)
    return _add_halves_call(g, _sibling_halves_call(g, tag), c_idx, tag)


def _rs_end(part, recv, c_idx, chip_idx, tag):
    n = 2 * part.shape[1]
    full = _add_chips_call(part, recv, jnp.concatenate([chip_idx, c_idx]), tag)
    return _sibling_join_call(full, tag).reshape(n, D)


def _local_step(x, mod, cact, target, wg, pack, small, c_idx, chip_idx):
    p, h1, wg = _fwd_in_call(x, mod, small["pre_tm"], wg, small["b_in"], pack, small["order"])
    o, oa, st, wg = _hgrn_fwd_call(p, small["logits"], small["hg_norm"], wg, pack)
    u, uc, cb, wg = _conv_fwd_call(p, small["conv_dw"], small["conv_db"], small["ln_g"], small["ln_b"], wg, pack)
    ya, yb, mg, y, x2, h2 = _merge_fwd_call(oa, cb, p, x, mod, small["post_tm"], small["pre_cm"], wg)
    z, da, dy2, dx2, acc_f = _ffn_call(h2, x2, target, mod, small["post_cm"], small["pre_cm"], wg)

    g_ff = _wgrad_call(None, h2, da, "wgrad_ff1", D, lambda i, j: (j, 0), 2 * R_FF)
    g_ff = _wgrad_call(g_ff, z, dy2, "wgrad_ff2", D, lambda i, j: (i, 1), 2 * R_FF)
    g_ff = g_ff.reshape(N_CHIPS, 2, R_FF, D)
    dy, dya, dyb, doa, dcb, dp_gt, acc_m, bs_gt, hr_ff = _merge_bwd_call(dx2, y, ya, yb, p, mod, small["post_tm"],
                                                                        wg, g_ff)
    part_ff = _add_halves_call(g_ff, hr_ff, c_idx, "ff")

    g_br = _wgrad_rows_call(None, oa, dya, "wgrad_br_a", 0)
    g_br = _wgrad_rows_call(g_br, cb, dyb, "wgrad_br_b", 1)
    g_br = _wgrad_rows_call(g_br, mg, dy, "wgrad_out", 2)
    g_br = g_br.reshape(N_CHIPS, 2, 3 * R_BR // 2, D)
    dp_hg, bs_hg, dlg, dgn, recv_ff, hr_br = _hgrn_bwd_call(p, o, doa, st, small["logits"], small["hg_norm"],
                                                            part_ff, g_br)
    part_br = _add_halves_call(g_br, hr_br, c_idx, "br")
    dp_cv, bs_cv, ddw, acc_c, recv_br = _conv_bwd_call(dcb, uc, u, p, small["conv_dw"], small["ln_g"], small["ln_b"],
                                                        part_br)

    g_in = _wgrad_call(None, h1, dp_hg, "wgrad_in_hg", D, lambda i, j: (j // 2, j % 2), R_IN)
    g_in = _wgrad_call(g_in, h1, dp_cv, "wgrad_in_cv", D, lambda i, j: (2, j), R_IN)
    g_in = _wgrad_call(g_in, h1, dp_gt, "wgrad_in_gt", D, lambda i, j: (3, j), R_IN)
    part_in = _rs_begin(g_in, c_idx, "in")
    chip_c = jnp.concatenate([chip_idx, c_idx])
    full_ff = _add_chips_call(part_ff, recv_ff, chip_c, "ff")
    full_br = _add_chips_call(part_br, recv_br, chip_c, "br")
    gx, acc_i, recv_in, full_ff, full_br = _in_bwd_call(dp_hg, dp_cv, dp_gt, x, dx2, mod, small["pre_tm"], wg,
                                                        part_in, full_ff, full_br)
    red_ff = full_ff.reshape(2 * R_FF, D)
    red_br = full_br.reshape(3 * R_BR, D)
    red_in = _rs_end(part_in, recv_in, c_idx, chip_idx, "in")

    zrow = jnp.zeros((1, D), F32)
    rows = [acc_i[0:1], acc_i[1:2], acc_m[0:1], acc_f[2:3], acc_f[3:4], acc_f[0:1],
            acc_i[2:3], acc_m[1:2], acc_f[4:5], acc_f[1:2],
            jnp.concatenate([bs_hg, bs_cv, bs_gt], axis=1).reshape(8, D),
            dlg, dgn, acc_c[0:1], acc_c[1:2], acc_c[2:3],
            ddw,
            cact, acc_f[5:6]] + [zrow] * 6
    return gx, jnp.concatenate(rows, axis=0), red_in, red_ff, red_br


def kernel(x, c, w_ada, b_ada, pre_norm_tm, post_norm_tm, pre_norm_cm, post_norm_cm, w_in, b_in, hg_lb_logits, hg_norm, conv_dw, conv_db, conv_ln_g, conv_ln_b, w_br_a, w_br_b, w_out, w_ff1, w_ff2, loss_target, m_w_ada, m_b_ada, m_pre_norm_tm, m_post_norm_tm, m_pre_norm_cm, m_post_norm_cm, m_w_in, m_b_in, m_hg_lb_logits, m_hg_norm, m_conv_dw, m_conv_db, m_conv_ln_g, m_conv_ln_b, m_w_br_a, m_w_br_b, m_w_out, m_w_ff1, m_w_ff2, v_w_ada, v_b_ada, v_pre_norm_tm, v_post_norm_tm, v_pre_norm_cm, v_post_norm_cm, v_w_in, v_b_in, v_hg_lb_logits, v_hg_norm, v_conv_dw, v_conv_db, v_conv_ln_g, v_conv_ln_b, v_w_br_a, v_w_br_b, v_w_out, v_w_ff1, v_w_ff2):
    xi, yi, ci = lax.axis_index("x"), lax.axis_index("y"), lax.axis_index("c")
    chip = 2 * xi + yi
    c_idx = jnp.reshape(ci, (1,)).astype(jnp.int32)
    chip_idx = jnp.reshape(chip, (1,)).astype(jnp.int32)

    def pack_small(ada_b, pre_t, post_t, pre_c, post_c, in_b, lg, hgn, cdb, lng, lnb, cdw):
        flat = jnp.concatenate([cdw[0].reshape(-1), jnp.zeros((8 * D - CONV_K * 256,), F32)]).reshape(8, D)
        return jnp.concatenate([ada_b.reshape(6, D), pre_t, post_t, pre_c, post_c, in_b.reshape(8, D), lg, hgn,
                                cdb, lng, lnb, flat], axis=0)

    w_in_halves = w_in[0].reshape(D, 2, D).transpose(1, 0, 2).reshape(R_IN, D)
    pack = jnp.concatenate([w_in_halves, w_ff1[0], w_ff2[0], w_br_a[0], w_br_b[0], w_out[0]],
                           axis=0).astype(BF16)
    wg = lax.dynamic_update_slice(lax.empty((N_CHIPS, PACK_W, D), BF16), pack[None], (chip, 0, 0))
    wa = 6 * D // N_CHIPS
    me = 4 * xi + 2 * yi + ci
    dw_blk = jnp.concatenate([conv_dw[0].reshape(-1), jnp.zeros((8 * D - CONV_K * 256,), F32)]).reshape(8, D)
    dw_all, ca_all, mod_all = _prologue_call(
        dw_blk, jnp.broadcast_to(c, (8, D)), w_ada[0].astype(BF16),
        lax.dynamic_slice_in_dim(b_ada, chip * wa, wa, axis=1))
    order = jnp.stack([chip, 2 * (1 - xi) + yi, 2 * xi + (1 - yi), 2 * (1 - xi) + (1 - yi)]).astype(jnp.int32)
    dw_all = dw_all.reshape(N_CHIPS, 2, 8 * D)[:, 0, :CONV_K * 256].reshape(N_CHIPS, CONV_K, 256)
    dw_full = dw_all.transpose(1, 0, 2).reshape(CONV_K, D)
    cact = lax.dynamic_slice_in_dim(ca_all, me * 8, 1, axis=0)
    mod_mine = lax.dynamic_index_in_dim(mod_all.reshape(N_CHIPS, 2, N_DEV, wa)[:, 0], me, axis=1,
                                        keepdims=False)
    mod = mod_mine.reshape(1, 6 * D)

    small = dict(pre_tm=pre_norm_tm, post_tm=post_norm_tm, pre_cm=pre_norm_cm, post_cm=post_norm_cm,
                 b_in=b_in, logits=hg_lb_logits, hg_norm=hg_norm, conv_dw=dw_full, conv_db=conv_db,
                 ln_g=conv_ln_g, ln_b=conv_ln_b, order=order)

    gx, srows, red_in, red_ff, red_br = _local_step(x[0], mod, cact, loss_target[0], wg, pack, small, c_idx,
                                                    chip_idx)

    shapes = {"in": w_in.shape, "br_a": w_br_a.shape, "br_b": w_br_b.shape, "out": w_out.shape,
              "ff1": w_ff1.shape, "ff2": w_ff2.shape}
    offs = {"in": (red_in, 0, R_IN), "ff1": (red_ff, 0, R_FF), "ff2": (red_ff, R_FF, 2 * R_FF),
            "br_a": (red_br, 0, R_BR), "br_b": (red_br, R_BR, 2 * R_BR), "out": (red_br, 2 * R_BR, 3 * R_BR)}
    wmv = {"in": (w_in, m_w_in, v_w_in), "br_a": (w_br_a, m_w_br_a, v_w_br_a), "br_b": (w_br_b, m_w_br_b, v_w_br_b),
           "out": (w_out, m_w_out, v_w_out), "ff1": (w_ff1, m_w_ff1, v_w_ff1), "ff2": (w_ff2, m_w_ff2, v_w_ff2)}
    res = {}
    for n in ("in", "ff1", "ff2"):
        shp = shapes[n]
        g2d = offs[n][0][offs[n][1]:offs[n][2]]
        if n == "in":
            g2d = g2d.reshape(2, D, D).transpose(1, 0, 2)
        g2d = g2d.reshape(shp[1], shp[2])
        w_, m_, v_ = (a[0] for a in wmv[n])
        if n == "in":
            d_, m2_, v2_, sall, ssum = _adamw_gather_call(w_, g2d, m_, v_, srows, "adamw_in")
        else:
            d_, m2_, v2_ = _adamw_call(w_, g2d, m_, v_, "adamw_" + n)
        res[n] = tuple(a.reshape(shp) for a in (g2d, d_, m2_, v2_))
    trio = ("br_a", "br_b", "out")
    d3, m3, v3 = _adamw_rows_call([wmv[n][0][0] for n in trio], red_br, [wmv[n][1][0] for n in trio],
                                  [wmv[n][2][0] for n in trio], "adamw_br")
    for j, n in enumerate(trio):
        res[n] = tuple(a.reshape(shapes[n]) for a in (red_br[j * R_BR:(j + 1) * R_BR], d3[j], m3[j], v3[j]))

    sall = sall.reshape(N_DEV, SMALL_ROWS, D)
    loss = jnp.sum(ssum[57])
    dmod_all = sall[:, 0:6, :].reshape(N_DEV, 6 * D)
    g_ada = _outer_call(sall[:, 56, :], lax.dynamic_slice_in_dim(dmod_all, chip * wa, wa, axis=1))
    g_dw = lax.dynamic_slice_in_dim(ssum[24:24 + CONV_K], chip * 256, 256, axis=1)
    g_small = jnp.concatenate(
        [ssum[0:24], jnp.concatenate([g_dw.reshape(-1), jnp.zeros((8 * D - CONV_K * 256,), F32)]).reshape(8, D)],
        axis=0)
    d_, m2_, v2_ = _adamw_call(w_ada[0], g_ada, m_w_ada[0], v_w_ada[0], "adamw_ada")
    res["ada"] = tuple(a.reshape(w_ada.shape) for a in (g_ada, d_, m2_, v2_))

    ws = pack_small(b_ada, pre_norm_tm, post_norm_tm, pre_norm_cm, post_norm_cm, b_in, hg_lb_logits, hg_norm,
                    conv_db, conv_ln_g, conv_ln_b, conv_dw)
    ms = pack_small(m_b_ada, m_pre_norm_tm, m_post_norm_tm, m_pre_norm_cm, m_post_norm_cm, m_b_in, m_hg_lb_logits,
                    m_hg_norm, m_conv_db, m_conv_ln_g, m_conv_ln_b, m_conv_dw)
    vs = pack_small(v_b_ada, v_pre_norm_tm, v_post_norm_tm, v_pre_norm_cm, v_post_norm_cm, v_b_in, v_hg_lb_logits,
                    v_hg_norm, v_conv_db, v_conv_ln_g, v_conv_ln_b, v_conv_dw)
    sres = (g_small,) + tuple(_adamw_call(ws, g_small, ms, vs, "adamw_small"))

    def unpack_small(t):
        return {"b_ada": t[0:6].reshape(1, 6 * D), "pre_tm": t[6:7], "post_tm": t[7:8], "pre_cm": t[8:9],
                "post_cm": t[9:10], "b_in": t[10:18].reshape(1, IN_COLS), "logits": t[18:20], "hg_norm": t[20:21],
                "conv_db": t[21:22], "ln_g": t[22:23], "ln_b": t[23:24],
                "conv_dw": t[24:32].reshape(-1)[:CONV_K * 256].reshape(1, CONV_K, 256)}

    order = ["ada", "b_ada", "pre_tm", "post_tm", "pre_cm", "post_cm", "in", "b_in", "logits", "hg_norm", "conv_dw",
             "conv_db", "ln_g", "ln_b", "br_a", "br_b", "out", "ff1", "ff2"]
    outs = [loss, gx.reshape(x.shape)]
    for kind in range(4):
        sm = unpack_small(sres[kind])
        for n in order:
            outs.append(res[n][kind] if n in res else sm[n])
    return tuple(outs)
```

```python
import jax
import jax.numpy as jnp
from jax import lax
from jax.experimental import pallas as pl
from jax.experimental.pallas import tpu as pltpu

F32, BF16 = jnp.float32, jnp.bfloat16
SDS = jax.ShapeDtypeStruct
BS = pl.BlockSpec
MESH = pl.DeviceIdType.MESH
HI = lax.Precision.HIGHEST

D = 1024
D_FF = 4096
IN_COLS = 8192
HEADS, DK = 8, 128
CHUNK = 128
CONV_K = 31
HALO = 32
SUB = 32
EPS = 1e-6
N_CHIPS, N_DEV = 4, 8
TM = 256
TB = 256
VMEM_LIMIT = 56 * 1024 * 1024

R_IN, R_BR, R_FF = 2048, 256, 1024
PACK_W = R_IN + 3 * R_BR + 2 * R_FF
O_IN, O_FF1, O_FF2, O_BRA, O_BRB, O_OUT = 0, 2048, 3072, 4096, 4352, 4608
SMALL_ROWS = 64

ADAM_LR, ADAM_B1, ADAM_B2, ADAM_EPS, ADAM_WD, ADAM_STEP = 0.001, 0.9, 0.999, 1e-08, 0.01, 10

NN = (((1,), (0,)), ((), ()))
NT = (((1,), (1,)), ((), ()))
TN = (((0,), (0,)), ((), ()))


def _mm(a, b, dims=NN, precision=None):
    return lax.dot_general(a, b, dims, preferred_element_type=F32, precision=precision)


def _sig(v):
    return jax.nn.sigmoid(v)


def _dsilu(v, s):
    return s * (1.0 + v * (1.0 - s))


def _params(*sem):
    return pltpu.CompilerParams(dimension_semantics=sem if sem else None, vmem_limit_bytes=VMEM_LIMIT)


def _rowsum(v):
    return jnp.sum(v, axis=0, keepdims=True)


def _mesh_pos():
    return lax.axis_index("x"), lax.axis_index("y"), lax.axis_index("c")


def _allgather_parts(x_ref, out_ref, send_sems, recv_sems, local_sem):
    m_per = x_ref.shape[0]
    x, y, c = _mesh_pos()
    me, sibling = (x, y, c), (x, y, 1 - c)
    chips = [(1 - x, y), (x, 1 - y), (1 - x, 1 - y)]

    def rows(px, py, pc):
        return out_ref.at[pl.ds((4 * px + 2 * py + pc) * m_per, m_per), :]

    def copy(k, block, to, src=None):
        return pltpu.make_async_remote_copy(
            src_ref=rows(*block) if src is None else src, dst_ref=rows(*block),
            send_sem=send_sems.at[k], recv_sem=recv_sems.at[k], device_id=to, device_id_type=MESH)

    def first():
        return [copy(0, me, sibling, src=x_ref)] + [copy(1 + j, me, (*chip, c), src=x_ref)
                                                    for j, chip in enumerate(chips)]

    def start():
        pltpu.make_async_copy(x_ref, rows(*me), local_sem).start()
        for cp in first():
            cp.start()

    def finish():
        passed = [copy(4 + j, (*chip, c), sibling) for j, chip in enumerate(chips)]
        for j, chip in enumerate(chips):
            copy(1 + j, (*chip, c), me).wait_recv()
            passed[j].start()
        copy(0, sibling, me).wait_recv()
        for j, chip in enumerate(chips):
            copy(4 + j, (*chip, 1 - c), me).wait_recv()
        for cp in first() + passed:
            cp.wait_send()
        pltpu.make_async_copy(x_ref, rows(*me), local_sem).wait()

    return start, finish


def _allgather_sems():
    return [pltpu.SemaphoreType.DMA((7,)), pltpu.SemaphoreType.DMA((7,)), pltpu.SemaphoreType.DMA]
def _gather_sems(n_ranges):
    return [pltpu.SemaphoreType.DMA((6 * n_ranges,)), pltpu.SemaphoreType.DMA((6 * n_ranges,))]


def _pack_gather(pack_ref, wg_ref, send_sems, recv_sems, ranges):
    x, y, c = _mesh_pos()
    me, sibling = (x, y, c), (x, y, 1 - c)
    chips = [(1 - x, y), (x, 1 - y), (1 - x, 1 - y)]

    def land(r, px, py, pc):
        off, n = ranges[r]
        return wg_ref.at[2 * px + py, pl.ds(off + pc * (n // 2), n // 2), :]

    def mine(r):
        off, n = ranges[r]
        return pack_ref.at[pl.ds(off + c * (n // 2), n // 2), :]

    def copy(r, k, block, to, src=None):
        return pltpu.make_async_remote_copy(
            src_ref=land(r, *block) if src is None else src, dst_ref=land(r, *block),
            send_sem=send_sems.at[6 * r + k], recv_sem=recv_sems.at[6 * r + k], device_id=to, device_id_type=MESH)

    def start():
        for r in range(len(ranges)):
            for j, chip in enumerate(chips):
                copy(r, j, me, (*chip, c), src=mine(r)).start()

    def finish():
        for r in range(len(ranges)):
            for j, chip in enumerate(chips):
                copy(r, j, (*chip, c), me).wait_recv()
                copy(r, 3 + j, (*chip, c), sibling).start()
        for r in range(len(ranges)):
            for j, chip in enumerate(chips):
                copy(r, 3 + j, (*chip, 1 - c), me).wait_recv()
        for r in range(len(ranges)):
            for j, chip in enumerate(chips):
                copy(r, j, me, (*chip, c), src=mine(r)).wait_send()
                copy(r, 3 + j, (*chip, c), sibling).wait_send()

    return start, finish


def _relay_sems():
    return [pltpu.SemaphoreType.DMA((8,)), pltpu.SemaphoreType.DMA((8,))]


def _relay_gather(pack_ref, wg_ref, send_sems, recv_sems, off, n):
    x, y, c = _mesh_pos()
    me, sibling = (x, y, c), (x, y, 1 - c)
    chips = [(1 - x, y), (x, 1 - y), (1 - x, 1 - y)]
    h, q = n // 2, n // 4

    def land(px, py, pc, piece=None):
        if piece is None:
            return wg_ref.at[2 * px + py, pl.ds(off + pc * h, h), :]
        return wg_ref.at[2 * px + py, pl.ds(off + pc * h + piece * q, q), :]

    def copy(k, ref, to, src=None):
        return pltpu.make_async_remote_copy(
            src_ref=ref if src is None else src, dst_ref=ref, send_sem=send_sems.at[k], recv_sem=recv_sems.at[k],
            device_id=to, device_id_type=MESH)

    def direct(j):
        return copy(j, land(x, y, c), (*chips[j], c), src=pack_ref.at[pl.ds(off + c * h, h), :])

    def relayed(j):
        if j == 0:
            return copy(6, land(*chips[0], c, 1), (x, 1 - y, c))
        return copy(7, land(*chips[1], c, 0), (1 - x, y, c))

    def start():
        direct(0).start()
        direct(1).start()

    def arrive(j):
        if j == 0:
            for k in range(2):
                copy(k, land(*chips[k], c), me).wait_recv()
                relayed(k).start()
                copy(3 + k, land(*chips[k], c), sibling).start()
        if j == 2:
            copy(7, land(*chips[2], c, 0), me).wait_recv()
            copy(6, land(*chips[2], c, 1), me).wait_recv()
            copy(5, land(*chips[2], c), sibling).start()
        copy(3 + j, land(*chips[j], 1 - c), me).wait_recv()

    def drain():
        for j in range(2):
            direct(j).wait_send()
            relayed(j).wait_send()
        for j in range(3):
            copy(3 + j, land(*chips[j], c), sibling).wait_send()

    return start, arrive, drain


def _prologue_call(dw_blk, c_blk, w_ada, b_ada):
    wa = w_ada.shape[1]

    def body(dw_ref, c_ref, wa_ref, ba_ref, dwg_ref, ca_ref, modg_ref,
             cg_scr, part_scr, s1, r1, l1, s2, r2, l2, s3, r3, l3):
        start_c, finish_c = _allgather_parts(c_ref, cg_scr, s2, r2, l2)
        start_dw, finish_dw = _allgather_parts(dw_ref, dwg_ref, s1, r1, l1)
        start_mod, finish_mod = _allgather_parts(part_scr, modg_ref, s3, r3, l3)
        start_c()
        start_dw()
        finish_c()
        cv = cg_scr[...]
        ca = cv * _sig(cv)
        ca_ref[...] = ca
        pick = (lax.broadcasted_iota(jnp.int32, (N_DEV, N_DEV * 8), 1)
                == 8 * lax.broadcasted_iota(jnp.int32, (N_DEV, N_DEV * 8), 0)).astype(BF16)
        ca8 = _mm(pick, ca.astype(BF16)).astype(BF16)
        part_scr[...] = _mm(ca8, wa_ref[...]) + ba_ref[...]
        start_mod()
        finish_dw()
        finish_mod()

    vm = BS(memory_space=pltpu.VMEM)
    return pl.pallas_call(
        body, name="prologue_adaln_conv_dw",
        out_shape=(SDS((N_DEV * 8, D), F32), SDS((N_DEV * 8, D), F32), SDS((N_DEV * N_DEV, wa), F32)),
        in_specs=[vm, vm, vm, vm], out_specs=(vm, vm, vm),
        scratch_shapes=[pltpu.VMEM((N_DEV * 8, D), F32), pltpu.VMEM((N_DEV, wa), F32)]
        + _allgather_sems() + _allgather_sems() + _allgather_sems(),
        compiler_params=pltpu.CompilerParams(vmem_limit_bytes=VMEM_LIMIT),
    )(dw_blk, c_blk, w_ada, b_ada)


def _halves_exchange(g_ref, out_ref, send_sems, recv_sems):
    x, y, c = _mesh_pos()

    def copies():
        return [pltpu.make_async_remote_copy(
            src_ref=g_ref.at[k, 1 - c], dst_ref=out_ref.at[k], send_sem=send_sems.at[k], recv_sem=recv_sems.at[k],
            device_id=(x, y, 1 - c), device_id_type=MESH) for k in range(N_CHIPS)]

    def start():
        for cp in copies():
            cp.start()

    def finish():
        for cp in copies():
            cp.wait()

    return start, finish


def _halves_sems():
    return [pltpu.SemaphoreType.DMA((N_CHIPS,)), pltpu.SemaphoreType.DMA((N_CHIPS,))]


def _sibling_halves_call(g, tag):
    _, _, h, n = g.shape

    def body(g_ref, out_ref, send_sems, recv_sems):
        start, finish = _halves_exchange(g_ref, out_ref, send_sems, recv_sems)
        start()
        finish()

    return pl.pallas_call(
        body, name="rs_sibling_halves_" + tag, out_shape=SDS((N_CHIPS, h, n), g.dtype),
        in_specs=[BS(memory_space=pl.ANY)], out_specs=BS(memory_space=pl.ANY),
        scratch_shapes=_halves_sems(),
    )(g)


def _chip_exchange(p_ref, out_ref, send_sems, recv_sems):
    x, y, c = _mesh_pos()
    chips = [(1 - x, y), (x, 1 - y), (1 - x, 1 - y)]

    def copies():
        return [pltpu.make_async_remote_copy(
            src_ref=p_ref.at[2 * cx + cy], dst_ref=out_ref.at[j], send_sem=send_sems.at[j], recv_sem=recv_sems.at[j],
            device_id=(cx, cy, c), device_id_type=MESH) for j, (cx, cy) in enumerate(chips)]

    def start():
        for cp in copies():
            cp.start()

    def finish():
        for cp in copies():
            cp.wait()

    return start, finish


def _exchange_sems():
    return [pltpu.SemaphoreType.DMA((3,)), pltpu.SemaphoreType.DMA((3,))]


def _join_exchange(in_ref, out_ref, send_sems, recv_sems):
    h = in_ref.shape[1]
    q = h // 4
    x, y, c = _mesh_pos()

    def copy(k, half):
        return pltpu.make_async_remote_copy(
            src_ref=in_ref.at[half, pl.ds(k * q, q)], dst_ref=out_ref.at[half, pl.ds(k * q, q)],
            send_sem=send_sems.at[k], recv_sem=recv_sems.at[k],
            device_id=(x, y, 1 - c), device_id_type=MESH)

    def start():
        for k in range(4):
            copy(k, c).start()

    def finish():
        for k in range(4):
            copy(k, c).wait_send()
            copy(k, 1 - c).wait_recv()

    return start, finish


def _join_sems():
    return [pltpu.SemaphoreType.DMA((4,)), pltpu.SemaphoreType.DMA((4,))]


def _add_halves_call(g, recv, c_idx, tag):
    _, _, h, n = g.shape
    tr = h // 2

    def body(c_ref, g_ref, r_ref, o_ref):
        o_ref[...] = (g_ref[...].astype(F32) + r_ref[...].astype(F32)).astype(BF16)

    return pl.pallas_call(
        body, name="rs_add_halves_" + tag, out_shape=SDS((N_CHIPS, h, n), BF16),
        grid_spec=pltpu.PrefetchScalarGridSpec(
            num_scalar_prefetch=1, grid=(N_CHIPS, 2),
            in_specs=[BS((None, None, tr, n), lambda k, r, c_ref: (k, c_ref[0], r, 0)),
                      BS((None, tr, n), lambda k, r, c_ref: (k, r, 0))],
            out_specs=BS((None, tr, n), lambda k, r, c_ref: (k, r, 0))),
        compiler_params=_params("arbitrary", "arbitrary"),
    )(c_idx, g, recv)


def _add_chips_call(p, recv, chip_c_idx, tag):
    _, h, n = p.shape
    tr = h // 2

    def body(k_ref, p_ref, r_ref, o_ref):
        acc = p_ref[...].astype(F32)
        for j in range(3):
            acc = acc + r_ref[j].astype(F32)
        o_ref[...] = acc

    return pl.pallas_call(
        body, name="rs_add_chips_" + tag, out_shape=SDS((2, h, n), F32),
        grid_spec=pltpu.PrefetchScalarGridSpec(
            num_scalar_prefetch=1, grid=(2,),
            in_specs=[BS((None, tr, n), lambda r, k_ref: (k_ref[0], r, 0)),
                      BS((3, tr, n), lambda r, k_ref: (0, r, 0))],
            out_specs=BS((None, tr, n), lambda r, k_ref: (k_ref[1], r, 0))),
        compiler_params=_params("arbitrary"),
    )(chip_c_idx, p, recv)


def _load_rows(wg_hbm, w_vmem, sem, off):
    cp = pltpu.make_async_copy(wg_hbm.at[:, pl.ds(off, w_vmem.shape[1]), :], w_vmem, sem)
    cp.start()
    return cp


def _fwd_in_call(x, mod, pre_tm, wg, b_in, pack, order):
    S = x.shape[0]
    tmf = 2 * TM
    nt = S // tmf
    wc = IN_COLS // N_CHIPS

    def body(ord_ref, x_ref, mod_ref, g_ref, w_hbm, b_ref, pack_ref, p_ref, h_hbm, wg_out, w_vmem, h_scr, sems,
             send_sems, recv_sems, send_sems2, recv_sems2):
        q, i = pl.program_id(0), pl.program_id(1)
        rows = pl.ds(pl.multiple_of(i * tmf, tmf), tmf)
        start, arrive, drain = _relay_gather(pack_ref, wg_out, send_sems, recv_sems, O_IN, R_IN)
        start2, finish2 = _pack_gather(pack_ref, wg_out, send_sems2, recv_sems2, [(O_OUT, R_BR)])

        def weights(phase):
            return pltpu.make_async_copy(wg_out.at[ord_ref[phase], pl.ds(O_IN, R_IN), :], w_vmem.at[phase % 2],
                                         sems.at[phase % 2])

        @pl.when((q == 0) & (i == 0))
        def _():
            start()
            weights(0).start()
            weights(0).wait()

        @pl.when((q == 1) & (i == 0))
        def _():
            arrive(0)
            start2()
            weights(1).start()
            weights(1).wait()
            arrive(1)
            weights(2).start()

        @pl.when((q == 2) & (i == 0))
        def _():
            weights(2).wait()
            arrive(2)
            weights(3).start()

        @pl.when((q == 3) & (i == 0))
        def _():
            weights(3).wait()

        @pl.when(q == 0)
        def _():
            xv = x_ref[...]
            r = lax.rsqrt(jnp.mean(xv * xv, axis=-1, keepdims=True) + EPS)
            h = xv * r * g_ref[...] * (1.0 + mod_ref[:, D:2 * D]) + mod_ref[:, 0:D]
            h_scr[rows, :] = h.astype(BF16)

        hb = h_scr[rows, :]
        slot = q % 2
        for k in range(wc // D):
            p_ref[:, k * D:(k + 1) * D] = _mm(hb, w_vmem[slot, k * D:(k + 1) * D, :]) + b_ref[:, k * D:(k + 1) * D]

        @pl.when((q == N_CHIPS - 1) & (i == nt - 1))
        def _():
            cp = pltpu.make_async_copy(h_scr, h_hbm, sems.at[0])
            cp.start()
            drain()
            finish2()
            cp.wait()

    hbm = BS(memory_space=pl.ANY)
    return pl.pallas_call(
        body, name="fwd_in", out_shape=(SDS((S, IN_COLS), F32), SDS((S, D), BF16), SDS(wg.shape, wg.dtype)),
        grid_spec=pltpu.PrefetchScalarGridSpec(
            num_scalar_prefetch=1, grid=(N_CHIPS, nt),
            in_specs=[BS((tmf, D), lambda q, i, o: (jnp.where(q == 0, i, nt - 1), 0)),
                      BS((1, 6 * D), lambda q, i, o: (0, 0)),
                      BS((1, D), lambda q, i, o: (0, 0)), hbm, BS((1, wc), lambda q, i, o: (0, o[q])), hbm],
            out_specs=(BS((tmf, wc), lambda q, i, o: (i, o[q])), hbm, hbm),
            scratch_shapes=[pltpu.VMEM((2, R_IN, D), BF16), pltpu.VMEM((S, D), BF16), pltpu.SemaphoreType.DMA((2,))]
            + _relay_sems() + _gather_sems(1)),
        input_output_aliases={4: 2},
        compiler_params=_params("arbitrary", "arbitrary"),
    )(order, x, mod, pre_tm, wg, b_in, pack)


def _lower_bound(lg_ref):
    l0, l1 = lg_ref[0:1, :], lg_ref[1:2, :]
    mx = jnp.maximum(l0, l1)
    e0, e1 = jnp.exp(l0 - mx), jnp.exp(l1 - mx)
    return e0 / (e0 + e1)


def _tri_masks():
    ri = lax.broadcasted_iota(jnp.int32, (CHUNK, CHUNK), 0)
    ci = lax.broadcasted_iota(jnp.int32, (CHUNK, CHUNK), 1)
    return (ri >= ci).astype(F32), (ci >= ri).astype(F32)


def _cumsum_mm(tri, g):
    tb = tri.astype(BF16)
    hi = g.astype(BF16)
    r1 = g - hi.astype(F32)
    mid = r1.astype(BF16)
    lo = (r1 - mid.astype(F32)).astype(BF16)
    return _mm(tb, hi) + _mm(tb, mid) + _mm(tb, lo)


def _hg_gates(q_r, f_r, lb, tril):
    sq = _sig(q_r)
    q = q_r * sq
    sf = _sig(f_r)
    f = lb + (1.0 - lb) * sf
    k = 1.0 - f
    g = jnp.log(f)
    b = _cumsum_mm(tril, g)
    b_last = _rowsum(g)
    row = lax.broadcasted_iota(jnp.int32, g.shape, 0)
    ref = _rowsum(jnp.where(row < CHUNK // 2, g, 0.0))
    e = jnp.exp(b)
    eq = jnp.exp(jnp.minimum(b - ref, 80.0))
    ek = jnp.exp(jnp.minimum(ref - b, 80.0))
    dd = jnp.exp(b_last - b)
    return dict(sq=sq, q=q, sf=sf, f=f, k=k, e=e, eq=eq, ek=ek, dd=dd, elast=jnp.exp(b_last),
                qe=q * e, qt=q * eq, kt=k * ek, kd=k * dd)


def _hgrn_fwd_call(p, logits, gn, wg, pack):
    S = p.shape[0]
    ncb = TB // CHUNK
    ranges = [(O_FF1, R_FF)]

    def body(q_ref, f_ref, v_ref, og_ref, lg_ref, gn_ref, wg_in, pack_ref, o_ref, oa_ref, st_ref, wg_out,
             st_scr, send_sems, recv_sems):
        start, finish = _pack_gather(pack_ref, wg_out, send_sems, recv_sems, ranges)

        @pl.when(pl.program_id(0) == 0)
        def _():
            start()
            st_scr[...] = jnp.zeros_like(st_scr)

        lb = _lower_bound(lg_ref)
        tril, _ = _tri_masks()

        def chunk(ci, carry):
            rows = pl.ds(pl.multiple_of(ci * CHUNK, CHUNK), CHUNK)
            st_ref[ci] = st_scr[...]
            t = _hg_gates(q_ref[rows, :], f_ref[rows, :], lb, tril)
            v = v_ref[rows, :]
            for h in range(HEADS):
                sl = slice(h * DK, (h + 1) * DK)
                stp = st_scr[:, sl]
                vb = v[:, sl].astype(BF16)
                inter = _mm(t["qe"][:, sl].astype(BF16), stp.astype(BF16), NT)
                a = jnp.where(tril > 0.5, _mm(t["qt"][:, sl].astype(BF16), t["kt"][:, sl].astype(BF16), NT), 0.0)
                o = inter + _mm(a.astype(BF16), vb)
                st_scr[:, sl] = stp * t["elast"][:, sl] + _mm(vb, t["kd"][:, sl].astype(BF16), TN)
                oh = o * lax.rsqrt(jnp.mean(o * o, axis=-1, keepdims=True) + EPS)
                og = og_ref[rows, sl]
                o_ref[rows, sl] = o
                oa_ref[rows, sl] = (oh * gn_ref[:, sl] * (og * _sig(og))).astype(BF16)
            return carry

        lax.fori_loop(0, ncb, chunk, 0)

        @pl.when(pl.program_id(0) == S // TB - 1)
        def _():
            finish()

    col = lambda j: BS((TB, D), lambda i, j=j: (i, j))
    hbm = BS(memory_space=pl.ANY)
    return pl.pallas_call(
        body, name="hgrn_fwd", grid=(S // TB,),
        out_shape=(SDS((S, D), F32), SDS((S, D), BF16), SDS((S // CHUNK, DK, D), F32), SDS(wg.shape, wg.dtype)),
        in_specs=[col(0), col(1), col(2), col(3), BS((2, D), lambda i: (0, 0)), BS((1, D), lambda i: (0, 0)),
                  hbm, hbm],
        out_specs=(BS((TB, D), lambda i: (i, 0)), BS((TB, D), lambda i: (i, 0)),
                   BS((ncb, DK, D), lambda i: (i, 0, 0)), hbm),
        scratch_shapes=[pltpu.VMEM((DK, D), F32)] + _gather_sems(len(ranges)),
        input_output_aliases={6: 3},
        compiler_params=_params("arbitrary"),
    )(p, p, p, p, logits, gn, wg, pack)


def _layernorm_stats(uc):
    mu = jnp.mean(uc, axis=-1, keepdims=True)
    xc = uc - mu
    rs = lax.rsqrt(jnp.mean(xc * xc, axis=-1, keepdims=True) + EPS)
    return xc * rs, rs


EXT = HALO + TM + 8


def _fill_shifted(ext, shifted):
    for m in range(1, 8):
        shifted[m - 1] = ext[m:m + HALO + TM, :]


def _window(ext, shifted, s0, n):
    m = s0 % 8
    q = s0 - m
    return ext[q:q + n, :] if m == 0 else shifted[m - 1, q:q + n, :]


def _conv_fwd_call(p, dw, db, ln_g, ln_b, wg, pack):
    S = p.shape[0]
    ranges = [(O_FF2, R_FF), (O_BRA, 2 * R_BR)]

    def body(cv_ref, cg_ref, dw_ref, db_ref, g_ref, b_ref, wg_in, pack_ref, u_ref, uc_ref, cb_ref, wg_out,
             uext, ush, send_sems, recv_sems):
        start, finish = _pack_gather(pack_ref, wg_out, send_sems, recv_sems, ranges)

        @pl.when(pl.program_id(0) == 0)
        def _():
            start()
            uext[0:HALO, :] = jnp.zeros((HALO, D), F32)
            uext[HALO + TM:EXT, :] = jnp.zeros((EXT - HALO - TM, D), F32)

        u = cv_ref[...] * _sig(cg_ref[...])
        uext[HALO:HALO + TM, :] = u
        u_ref[...] = u
        _fill_shifted(uext, ush)
        for rb in range(TM // SUB):
            acc = jnp.broadcast_to(db_ref[...], (SUB, D))
            for j in range(CONV_K):
                s0 = HALO - (CONV_K - 1) + j + rb * SUB
                acc = acc + dw_ref[j:j + 1, :] * _window(uext, ush, s0, SUB)
            uc_ref[rb * SUB:(rb + 1) * SUB, :] = acc
            xh, _ = _layernorm_stats(acc)
            ln = xh * g_ref[...] + b_ref[...]
            cb_ref[rb * SUB:(rb + 1) * SUB, :] = (ln * _sig(ln)).astype(BF16)
        uext[0:HALO, :] = uext[TM:TM + HALO, :]

        @pl.when(pl.program_id(0) == S // TM - 1)
        def _():
            finish()

    vec = BS((1, D), lambda i: (0, 0))
    hbm = BS(memory_space=pl.ANY)
    return pl.pallas_call(
        body, name="conv_fwd", grid=(S // TM,),
        out_shape=(SDS((S, D), F32), SDS((S, D), F32), SDS((S, D), BF16), SDS(wg.shape, wg.dtype)),
        in_specs=[BS((TM, D), lambda i: (i, 4)), BS((TM, D), lambda i: (i, 5)),
                  BS((CONV_K, D), lambda i: (0, 0)), vec, vec, vec, hbm, hbm],
        out_specs=(BS((TM, D), lambda i: (i, 0)),) * 3 + (hbm,),
        scratch_shapes=[pltpu.VMEM((EXT, D), F32), pltpu.VMEM((7, HALO + TM, D), F32)] + _gather_sems(len(ranges)),
        input_output_aliases={6: 3},
        compiler_params=_params("arbitrary"),
    )(p, p, dw, db, ln_g, ln_b, wg, pack)


def _mm_rows(a, w_ref):
    acc = _mm(a[:, 0:R_BR], w_ref[0])
    for k in range(1, N_CHIPS):
        acc = acc + _mm(a[:, k * R_BR:(k + 1) * R_BR], w_ref[k])
    return acc


def _mm_rows_t(a, w_ref):
    return jnp.concatenate([_mm(a, w_ref[k], NT) for k in range(N_CHIPS)], axis=1)


def _br_spec(off):
    return BS((N_CHIPS, R_BR, D), lambda i: (0, off // R_BR, 0))


def _merge_fwd_call(oa, cb, p, x, mod, post_tm, pre_cm, wg):
    S = x.shape[0]

    def body(oa_ref, cb_ref, ga_ref, gb_ref, x_ref, mod_ref, post_ref, pre_ref, wa_ref, wb_ref, wo_ref,
             ya_ref, yb_ref, mg_ref, y_ref, x2_ref, h2_ref):
        ya = _mm_rows(oa_ref[...], wa_ref)
        yb = _mm_rows(cb_ref[...], wb_ref)
        ya_ref[...] = ya.astype(BF16)
        yb_ref[...] = yb.astype(BF16)
        mg = (_sig(ga_ref[...]) * ya + _sig(gb_ref[...]) * yb).astype(BF16)
        mg_ref[...] = mg
        y = _mm_rows(mg, wo_ref)
        y_ref[...] = y
        n = y * lax.rsqrt(jnp.mean(y * y, axis=-1, keepdims=True) + EPS) * post_ref[...]
        x2 = x_ref[...] + mod_ref[:, 2 * D:3 * D] * n
        x2_ref[...] = x2
        r2 = lax.rsqrt(jnp.mean(x2 * x2, axis=-1, keepdims=True) + EPS)
        h2 = x2 * r2 * pre_ref[...] * (1.0 + mod_ref[:, 4 * D:5 * D]) + mod_ref[:, 3 * D:4 * D]
        h2_ref[...] = h2.astype(BF16)

    tile = BS((TM, D), lambda i: (i, 0))
    vec = BS((1, D), lambda i: (0, 0))
    return pl.pallas_call(
        body, name="merge_fwd", grid=(S // TM,),
        out_shape=(SDS((S, D), BF16), SDS((S, D), BF16), SDS((S, D), BF16), SDS((S, D), F32), SDS((S, D), F32),
                   SDS((S, D), BF16)),
        in_specs=[tile, tile, BS((TM, D), lambda i: (i, 6)), BS((TM, D), lambda i: (i, 7)), tile,
                  BS((1, 6 * D), lambda i: (0, 0)), vec, vec, _br_spec(O_BRA), _br_spec(O_BRB), _br_spec(O_OUT)],
        out_specs=(tile,) * 6,
        compiler_params=_params("arbitrary"),
    )(oa, cb, p, p, x, mod, post_tm, pre_cm, wg, wg, wg)


def _ffn_call(h2, x2, target, mod, post_cm, pre_cm, wg):
    S = x2.shape[0]

    def body(h2_ref, x2_ref, t_ref, mod_ref, post_ref, pre_ref, w_hbm,
             z_ref, da_ref, dy2_ref, dx2_ref, acc_ref, w1_v, w2_v, ra_scr, sems):
        @pl.when(pl.program_id(0) == 0)
        def _():
            c1 = _load_rows(w_hbm, w1_v, sems.at[0], O_FF1)
            c2 = _load_rows(w_hbm, w2_v, sems.at[1], O_FF2)
            c1.wait()
            c2.wait()
            acc_ref[...] = jnp.zeros_like(acc_ref)

        h2 = h2_ref[...]
        for k in range(N_CHIPS):
            ra = jnp.maximum(_mm(h2, w1_v[k]), 0.0)
            ra_scr[:, k * D:(k + 1) * D] = ra
            z_ref[:, k * D:(k + 1) * D] = (ra * ra).astype(BF16)
        y2 = _mm(z_ref[:, 0:D], w2_v[0])
        for k in range(1, N_CHIPS):
            y2 = y2 + _mm(z_ref[:, k * D:(k + 1) * D], w2_v[k])
        ry = lax.rsqrt(jnp.mean(y2 * y2, axis=-1, keepdims=True) + EPS)
        yn = y2 * ry
        n = yn * post_ref[...]
        g2 = mod_ref[:, 5 * D:6 * D]
        x2 = x2_ref[...]
        err = x2 + g2 * n - t_ref[...]
        acc_ref[5:6, :] += _rowsum(err * err) * (0.5 / D)
        dout = err * (1.0 / D)
        acc_ref[0:1, :] += _rowsum(dout * n)
        dn = dout * g2
        acc_ref[1:2, :] += _rowsum(dn * yn)
        dyn = dn * post_ref[...]
        dy2 = (ry * (dyn - yn * jnp.mean(dyn * yn, axis=-1, keepdims=True))).astype(BF16)
        dy2_ref[...] = dy2
        for k in range(N_CHIPS):
            dz = _mm(dy2, w2_v[k], NT)
            da_ref[:, k * D:(k + 1) * D] = (dz * (2.0 * ra_scr[:, k * D:(k + 1) * D])).astype(BF16)
        dh2 = jnp.zeros((TM, D), F32)
        for k in range(N_CHIPS):
            dh2 = dh2 + _mm(da_ref[:, k * D:(k + 1) * D], w1_v[k], NT)
        r2 = lax.rsqrt(jnp.mean(x2 * x2, axis=-1, keepdims=True) + EPS)
        xn = x2 * r2
        yv = xn * pre_ref[...]
        acc_ref[2:3, :] += _rowsum(dh2)
        acc_ref[3:4, :] += _rowsum(dh2 * yv)
        dyv = dh2 * (1.0 + mod_ref[:, 4 * D:5 * D])
        acc_ref[4:5, :] += _rowsum(dyv * xn)
        dxn = dyv * pre_ref[...]
        dx2_ref[...] = dout + r2 * (dxn - xn * jnp.mean(dxn * xn, axis=-1, keepdims=True))

    tile = BS((TM, D), lambda i: (i, 0))
    wide = BS((TM, D_FF), lambda i: (i, 0))
    vec = BS((1, D), lambda i: (0, 0))
    return pl.pallas_call(
        body, name="ffn_fwd_bwd", grid=(S // TM,),
        out_shape=(SDS((S, D_FF), BF16), SDS((S, D_FF), BF16), SDS((S, D), BF16), SDS((S, D), F32),
                   SDS((8, D), F32)),
        in_specs=[tile, tile, tile, BS((1, 6 * D), lambda i: (0, 0)), vec, vec, BS(memory_space=pl.ANY)],
        out_specs=(wide, wide, tile, tile, BS((8, D), lambda i: (0, 0))),
        scratch_shapes=[pltpu.VMEM((N_CHIPS, R_FF, D), BF16), pltpu.VMEM((N_CHIPS, R_FF, D), BF16),
                        pltpu.VMEM((TM, D_FF), F32),
                        pltpu.SemaphoreType.DMA((2,))],
        compiler_params=_params("arbitrary"),
    )(h2, x2, target, mod, post_cm, pre_cm, wg)


def _merge_bwd_call(dx2, y, ya, yb, p, mod, post_tm, wg, g):
    S = y.shape[0]

    def body(dx2_ref, y_ref, ya_ref, yb_ref, ga_ref, gb_ref, mod_ref, post_ref, wa_ref, wb_ref, wo_ref, g_ref,
             dy_ref, dya_ref, dyb_ref, doa_ref, dcb_ref, dpg_ref, acc_ref, bsum_ref, hr_ref, send_sems, recv_sems):
        start, finish = _halves_exchange(g_ref, hr_ref, send_sems, recv_sems)

        @pl.when(pl.program_id(0) == 0)
        def _():
            start()
            acc_ref[...] = jnp.zeros_like(acc_ref)
            bsum_ref[...] = jnp.zeros_like(bsum_ref)

        y = y_ref[...]
        ry = lax.rsqrt(jnp.mean(y * y, axis=-1, keepdims=True) + EPS)
        yn = y * ry
        dx2 = dx2_ref[...]
        acc_ref[0:1, :] += _rowsum(dx2 * (yn * post_ref[...]))
        dn = dx2 * mod_ref[:, 2 * D:3 * D]
        acc_ref[1:2, :] += _rowsum(dn * yn)
        dyn = dn * post_ref[...]
        dy = (ry * (dyn - yn * jnp.mean(dyn * yn, axis=-1, keepdims=True))).astype(BF16)
        dy_ref[...] = dy
        dmg = _mm_rows_t(dy, wo_ref)
        sa, sb = _sig(ga_ref[...]), _sig(gb_ref[...])
        dya = (dmg * sa).astype(BF16)
        dyb = (dmg * sb).astype(BF16)
        dya_ref[...] = dya
        dyb_ref[...] = dyb
        dga = dmg * ya_ref[...].astype(F32) * (sa * (1.0 - sa))
        dgb = dmg * yb_ref[...].astype(F32) * (sb * (1.0 - sb))
        dpg_ref[:, 0:D] = dga.astype(BF16)
        dpg_ref[:, D:2 * D] = dgb.astype(BF16)
        bsum_ref[:, 0:D] += _rowsum(dga)
        bsum_ref[:, D:2 * D] += _rowsum(dgb)
        doa_ref[...] = _mm_rows_t(dya, wa_ref)
        dcb_ref[...] = _mm_rows_t(dyb, wb_ref)

        @pl.when(pl.program_id(0) == S // TM - 1)
        def _():
            finish()

    tile = BS((TM, D), lambda i: (i, 0))
    vec = BS((1, D), lambda i: (0, 0))
    return pl.pallas_call(
        body, name="merge_bwd", grid=(S // TM,),
        out_shape=(SDS((S, D), BF16), SDS((S, D), BF16), SDS((S, D), BF16), SDS((S, D), F32), SDS((S, D), F32),
                   SDS((S, 2 * D), BF16), SDS((8, D), F32), SDS((1, 2 * D), F32),
                   SDS((N_CHIPS,) + g.shape[2:], g.dtype)),
        in_specs=[tile, tile, tile, tile, BS((TM, D), lambda i: (i, 6)), BS((TM, D), lambda i: (i, 7)),
                  BS((1, 6 * D), lambda i: (0, 0)), vec, _br_spec(O_BRA), _br_spec(O_BRB), _br_spec(O_OUT),
                  BS(memory_space=pl.ANY)],
        out_specs=(tile, tile, tile, tile, tile, BS((TM, 2 * D), lambda i: (i, 0)),
                   BS((8, D), lambda i: (0, 0)), BS((1, 2 * D), lambda i: (0, 0)), BS(memory_space=pl.ANY)),
        scratch_shapes=_halves_sems(),
        compiler_params=_params("arbitrary"),
    )(dx2, y, ya, yb, p, p, mod, post_tm, wg, wg, wg, g)


def _hgrn_bwd_call(p, o, doa, st, logits, gn, part, g):
    S = p.shape[0]
    nb = S // TB
    ncb = TB // CHUNK

    def body(q_ref, f_ref, v_ref, og_ref, o_ref, doa_ref, st_ref, lg_ref, gn_ref, part_ref, g_ref,
             dp_ref, bsum_ref, dlg_ref, dgn_ref, recv_ref, hr_ref,
             dst_scr, dlb_scr, dqe_s, dqt_s, dkt_s, dkd_s, dv_s, dog_s, dble_s, send_sems, recv_sems, hs, hr):
        i = pl.program_id(0)
        start, finish = _chip_exchange(part_ref, recv_ref, send_sems, recv_sems)
        start_h, finish_h = _halves_exchange(g_ref, hr_ref, hs, hr)

        @pl.when(i == 0)
        def _():
            start_h()
            start()
            dst_scr[...] = jnp.zeros_like(dst_scr)
            dlb_scr[...] = jnp.zeros_like(dlb_scr)
            bsum_ref[...] = jnp.zeros_like(bsum_ref)
            dgn_ref[...] = jnp.zeros_like(dgn_ref)

        lb = _lower_bound(lg_ref)
        tril, triu = _tri_masks()

        def chunk(tt, carry):
            ci = ncb - 1 - tt
            rows = pl.ds(pl.multiple_of(ci * CHUNK, CHUNK), CHUNK)
            q_r, f_r = q_ref[rows, :], f_ref[rows, :]
            t = _hg_gates(q_r, f_r, lb, tril)
            v = v_ref[rows, :]
            for h in range(HEADS):
                sl = slice(h * DK, (h + 1) * DK)
                stp = st_ref[ci, :, sl]
                stb = stp.astype(BF16)
                qeb = t["qe"][:, sl].astype(BF16)
                qtb = t["qt"][:, sl].astype(BF16)
                ktb = t["kt"][:, sl].astype(BF16)
                kdb = t["kd"][:, sl].astype(BF16)
                vb = v[:, sl].astype(BF16)
                a = jnp.where(tril > 0.5, _mm(qtb, ktb, NT), 0.0)
                o_h = o_ref[rows, sl]
                rinv = lax.rsqrt(jnp.mean(o_h * o_h, axis=-1, keepdims=True) + EPS)
                oh = o_h * rinv
                og = og_ref[rows, sl]
                so = _sig(og)
                d_oa = doa_ref[rows, sl]
                don = d_oa * (og * so)
                dog_s[:, sl] = d_oa * (oh * gn_ref[:, sl]) * _dsilu(og, so)
                dgn_ref[:, sl] += _rowsum(don * oh)
                doh = don * gn_ref[:, sl]
                do = (rinv * (doh - oh * jnp.mean(doh * oh, axis=-1, keepdims=True))).astype(BF16)
                dqe_s[:, sl] = _mm(do, stb, NN)
                dstp = _mm(do, qeb, TN)
                dab = jnp.where(tril > 0.5, _mm(do, vb, NT), 0.0).astype(BF16)
                dqt_s[:, sl] = _mm(dab, ktb, NN)
                dkt_s[:, sl] = _mm(dab, qtb, TN)
                dstn = dst_scr[:, sl]
                dsb = dstn.astype(BF16)
                dkd_s[:, sl] = _mm(vb, dsb, NN)
                dv_s[:, sl] = _mm(a.astype(BF16), do, TN) + _mm(kdb, dsb, NT)
                el = t["elast"][:, sl]
                dst_scr[:, sl] = dstn * el + dstp
                dble_s[:, sl] = el * _rowsum(stp * dstn)
            dqe, dqt, dkt, dkd = dqe_s[...], dqt_s[...], dkt_s[...], dkd_s[...]
            dq = dqe * t["e"] + dqt * t["eq"]
            dk = dkt * t["ek"] + dkd * t["dd"]
            dkk = dkd * t["kd"]
            qt_r = t["qt"].astype(BF16).astype(F32)
            kt_r = t["kt"].astype(BF16).astype(F32)
            dbv = dqe * t["qe"] + dqt * qt_r - dkt * kt_r - dkk
            dg = _cumsum_mm(triu, dbv) + (_rowsum(dkk) + dble_s[...])
            df = dg / t["f"] - dk
            sf = t["sf"]
            dlb_scr[...] += _rowsum(df * (1.0 - sf))
            dqr = dq * _dsilu(q_r, t["sq"])
            dfr = df * (1.0 - lb) * (sf * (1.0 - sf))
            dvv, dog = dv_s[...], dog_s[...]
            dp_ref[rows, 0:D] = dqr.astype(BF16)
            dp_ref[rows, D:2 * D] = dfr.astype(BF16)
            dp_ref[rows, 2 * D:3 * D] = dvv.astype(BF16)
            dp_ref[rows, 3 * D:4 * D] = dog.astype(BF16)
            bsum_ref[:, 0:D] += _rowsum(dqr)
            bsum_ref[:, D:2 * D] += _rowsum(dfr)
            bsum_ref[:, 2 * D:3 * D] += _rowsum(dvv)
            bsum_ref[:, 3 * D:4 * D] += _rowsum(dog)
            return carry

        lax.fori_loop(0, ncb, chunk, 0)

        dl = dlb_scr[...] * lb * (1.0 - lb)
        dlg_ref[0:1, :] = dl
        dlg_ref[1:2, :] = -dl

        @pl.when(i == nb - 1)
        def _():
            finish_h()
            finish()

    col = lambda j: BS((TB, D), lambda i, j=j: (nb - 1 - i, j))
    rev = BS((TB, D), lambda i: (nb - 1 - i, 0))
    cd = pltpu.VMEM((CHUNK, D), F32)
    return pl.pallas_call(
        body, name="hgrn_bwd", grid=(nb,),
        out_shape=(SDS((S, 4 * D), BF16), SDS((1, 4 * D), F32), SDS((2, D), F32), SDS((1, D), F32),
                   SDS((3,) + part.shape[1:], part.dtype), SDS((N_CHIPS,) + g.shape[2:], g.dtype)),
        in_specs=[col(0), col(1), col(2), col(3), rev, rev, BS((ncb, DK, D), lambda i: (nb - 1 - i, 0, 0)),
                  BS((2, D), lambda i: (0, 0)), BS((1, D), lambda i: (0, 0)), BS(memory_space=pl.ANY),
                  BS(memory_space=pl.ANY)],
        out_specs=(BS((TB, 4 * D), lambda i: (nb - 1 - i, 0)), BS((1, 4 * D), lambda i: (0, 0)),
                   BS((2, D), lambda i: (0, 0)), BS((1, D), lambda i: (0, 0)), BS(memory_space=pl.ANY),
                   BS(memory_space=pl.ANY)),
        scratch_shapes=[pltpu.VMEM((DK, D), F32), pltpu.VMEM((1, D), F32), cd, cd, cd, cd, cd, cd,
                        pltpu.VMEM((1, D), F32)] + _exchange_sems() + _halves_sems(),
        compiler_params=_params("arbitrary"),
    )(p, p, p, p, o, doa, st, logits, gn, part, g)


def _conv_bwd_call(dcb, uc, u, p, dw, ln_g, ln_b, part):
    S = uc.shape[0]
    nb = S // TM
    hb = TM // HALO

    def body(dcb_ref, uc_ref, u_ref, uh_ref, cv_ref, cg_ref, dw_ref, g_ref, b_ref, part_ref,
             dp_ref, bsum_ref, ddw_ref, acc_ref, recv_ref, uext, dext, ush, dsh, send_sems, recv_sems):
        i = pl.program_id(0)
        start, finish = _chip_exchange(part_ref, recv_ref, send_sems, recv_sems)

        @pl.when(i == 0)
        def _():
            start()
            dext[TM:EXT, :] = jnp.zeros((EXT - TM, D), F32)
            uext[HALO + TM:EXT, :] = jnp.zeros((EXT - HALO - TM, D), F32)
            bsum_ref[...] = jnp.zeros_like(bsum_ref)
            ddw_ref[...] = jnp.zeros_like(ddw_ref)
            acc_ref[...] = jnp.zeros_like(acc_ref)

        first_tile = (nb - 1 - i) == 0
        uext[0:HALO, :] = jnp.where(first_tile, 0.0, uh_ref[...])
        uext[HALO:HALO + TM, :] = u_ref[...]
        _fill_shifted(uext, ush)

        for rb in range(TM // SUB):
            rs_ = slice(rb * SUB, (rb + 1) * SUB)
            xh, rs = _layernorm_stats(uc_ref[rs_, :])
            ln = xh * g_ref[...] + b_ref[...]
            dln = dcb_ref[rs_, :] * _dsilu(ln, _sig(ln))
            acc_ref[1:2, :] += _rowsum(dln * xh)
            acc_ref[2:3, :] += _rowsum(dln)
            dxh = dln * g_ref[...]
            duc = rs * (dxh - jnp.mean(dxh, axis=-1, keepdims=True)
                        - xh * jnp.mean(dxh * xh, axis=-1, keepdims=True))
            dext[rs_, :] = duc
            acc_ref[0:1, :] += _rowsum(duc)
        _fill_shifted(dext, dsh)

        for j in range(CONV_K):
            part = jnp.zeros((SUB, D), F32)
            for rb in range(TM // SUB):
                s0 = HALO - (CONV_K - 1) + j + rb * SUB
                part = part + dext[rb * SUB:(rb + 1) * SUB, :] * _window(uext, ush, s0, SUB)
            ddw_ref[j:j + 1, :] += _rowsum(part)

        for rb in range(TM // SUB):
            rs_ = slice(rb * SUB, (rb + 1) * SUB)
            du = jnp.zeros((SUB, D), F32)
            for j in range(CONV_K):
                s0 = rb * SUB + (CONV_K - 1) - j
                du = du + dw_ref[j:j + 1, :] * _window(dext, dsh, s0, SUB)
            cg = cg_ref[rs_, :]
            sg = _sig(cg)
            dcv = du * sg
            dcg = du * cv_ref[rs_, :] * (sg * (1.0 - sg))
            dp_ref[rs_, 0:D] = dcv.astype(BF16)
            dp_ref[rs_, D:2 * D] = dcg.astype(BF16)
            bsum_ref[:, 0:D] += _rowsum(dcv)
            bsum_ref[:, D:2 * D] += _rowsum(dcg)

        dext[TM:TM + HALO, :] = dext[0:HALO, :]

        @pl.when(i == nb - 1)
        def _():
            finish()

    rev = BS((TM, D), lambda i: (nb - 1 - i, 0))
    vec = BS((1, D), lambda i: (0, 0))
    return pl.pallas_call(
        body, name="conv_bwd", grid=(nb,),
        out_shape=(SDS((S, 2 * D), BF16), SDS((1, 2 * D), F32), SDS((32, D), F32), SDS((8, D), F32),
                   SDS((3,) + part.shape[1:], part.dtype)),
        in_specs=[rev, rev, rev, BS((HALO, D), lambda i: (jnp.maximum((nb - 1 - i) * hb - 1, 0), 0)),
                  BS((TM, D), lambda i: (nb - 1 - i, 4)), BS((TM, D), lambda i: (nb - 1 - i, 5)),
                  BS((CONV_K, D), lambda i: (0, 0)), vec, vec, BS(memory_space=pl.ANY)],
        out_specs=(BS((TM, 2 * D), lambda i: (nb - 1 - i, 0)), BS((1, 2 * D), lambda i: (0, 0)),
                   BS((32, D), lambda i: (0, 0)), BS((8, D), lambda i: (0, 0)), BS(memory_space=pl.ANY)),
        scratch_shapes=[pltpu.VMEM((EXT, D), F32), pltpu.VMEM((EXT, D), F32),
                        pltpu.VMEM((7, HALO + TM, D), F32), pltpu.VMEM((7, HALO + TM, D), F32)] + _exchange_sems(),
        compiler_params=_params("arbitrary"),
    )(dcb, uc, u, u, p, p, dw, ln_g, ln_b, part)


def _in_bwd_call(dp_hg, dp_cv, dp_gt, x, dx2, mod, pre_tm, wg, part, full_a, full_b):
    S = x.shape[0]
    tm = TM

    def body(hg_ref, cv_ref, gt_ref, x_ref, dx2_ref, mod_ref, g_ref, w_hbm, part_ref, fa_in, fb_in,
             gx_ref, acc_ref, recv_ref, fa_out, fb_out, w_vmem, sem, send_sems, recv_sems, sa, ra, sb, rb):
        start, finish = _chip_exchange(part_ref, recv_ref, send_sems, recv_sems)
        start_a, finish_a = _join_exchange(fa_in, fa_out, sa, ra)
        start_b, finish_b = _join_exchange(fb_in, fb_out, sb, rb)

        @pl.when(pl.program_id(0) == 0)
        def _():
            start_a()
            start_b()
            start()
            _load_rows(w_hbm, w_vmem, sem, O_IN).wait()
            acc_ref[...] = jnp.zeros_like(acc_ref)

        dh = jnp.zeros((tm, D), F32)
        for k in range(IN_COLS // D):
            src, kk = ((hg_ref, k), (cv_ref, k - 4), (gt_ref, k - 6))[0 if k < 4 else (1 if k < 6 else 2)]
            dh = dh + _mm(src[:, kk * D:(kk + 1) * D], w_vmem[k // 2, (k % 2) * D:(k % 2 + 1) * D, :], NT)
        xv = x_ref[...]
        r = lax.rsqrt(jnp.mean(xv * xv, axis=-1, keepdims=True) + EPS)
        xn = xv * r
        yv = xn * g_ref[...]
        acc_ref[0:1, :] += _rowsum(dh)
        acc_ref[1:2, :] += _rowsum(dh * yv)
        dyv = dh * (1.0 + mod_ref[:, D:2 * D])
        acc_ref[2:3, :] += _rowsum(dyv * xn)
        dxn = dyv * g_ref[...]
        gx_ref[...] = dx2_ref[...] + r * (dxn - xn * jnp.mean(dxn * xn, axis=-1, keepdims=True))

        @pl.when(pl.program_id(0) == S // tm - 1)
        def _():
            finish_a()
            finish_b()
            finish()

    tile = BS((tm, D), lambda i: (i, 0))
    hbm = BS(memory_space=pl.ANY)
    return pl.pallas_call(
        body, name="in_bwd", grid=(S // tm,),
        out_shape=(SDS((S, D), F32), SDS((8, D), F32), SDS((3,) + part.shape[1:], part.dtype),
                   SDS(full_a.shape, full_a.dtype), SDS(full_b.shape, full_b.dtype)),
        in_specs=[BS((tm, 4 * D), lambda i: (i, 0)), BS((tm, 2 * D), lambda i: (i, 0)),
                  BS((tm, 2 * D), lambda i: (i, 0)), tile, tile, BS((1, 6 * D), lambda i: (0, 0)),
                  BS((1, D), lambda i: (0, 0)), hbm, hbm, hbm, hbm],
        out_specs=(tile, BS((8, D), lambda i: (0, 0)), hbm, hbm, hbm),
        scratch_shapes=[pltpu.VMEM((N_CHIPS, R_IN, D), BF16), pltpu.SemaphoreType.DMA] + _exchange_sems()
        + _join_sems() + _join_sems(),
        input_output_aliases={9: 3, 10: 4},
        compiler_params=_params("arbitrary"),
    )(dp_hg, dp_cv, dp_gt, x, dx2, mod, pre_tm, wg, part, full_a, full_b)


def _wgrad_call(gp, a, b, name, bm, place, rows):
    S, M = a.shape
    N = b.shape[1]
    bk = min(S, 1024)
    nk = S // bk

    def body(a_ref, b_ref, *rest):
        o_ref, acc = rest[-2], rest[-1]
        k = pl.program_id(2)

        @pl.when(k == 0)
        def _():
            acc[...] = jnp.zeros_like(acc)

        acc[...] += _mm(a_ref[...], b_ref[...], TN)

        @pl.when(k == nk - 1)
        def _():
            o_ref[...] = acc[...].astype(BF16)

    in_specs = [BS((bk, bm), lambda i, j, k: (k, i)), BS((bk, D), lambda i, j, k: (k, j))]
    args = [a, b]
    if gp is not None:
        in_specs.append(BS(memory_space=pl.ANY))
        args.append(gp)
    return pl.pallas_call(
        body, name=name, grid=(M // bm, N // D, nk),
        out_shape=SDS((N_CHIPS, rows, D), BF16),
        in_specs=in_specs,
        out_specs=BS((None, bm, D), lambda i, j, k: (*place(i, j), 0)),
        scratch_shapes=[pltpu.VMEM((bm, D), F32)],
        input_output_aliases={} if gp is None else {2: 0},
        compiler_params=_params("parallel", "parallel", "arbitrary"),
    )(*args)


def _wgrad_rows_call(gp, a, b, name, blk):
    S = a.shape[0]
    bk = min(S, 1024)
    nk = S // bk

    def body(a_ref, b_ref, *rest):
        o_ref, acc = rest[-2], rest[-1]
        k = pl.program_id(0)

        @pl.when(k == 0)
        def _():
            acc[...] = jnp.zeros_like(acc)

        acc[...] += _mm(a_ref[...], b_ref[...], TN)

        @pl.when(k == nk - 1)
        def _():
            for c in range(N_CHIPS):
                o_ref[c] = acc[c * R_BR:(c + 1) * R_BR, :].astype(BF16)

    in_specs = [BS((bk, D), lambda k: (k, 0)), BS((bk, D), lambda k: (k, 0))]
    args = [a, b]
    if gp is not None:
        in_specs.append(BS(memory_space=pl.ANY))
        args.append(gp)
    return pl.pallas_call(
        body, name=name, grid=(nk,),
        out_shape=SDS((N_CHIPS, 3 * R_BR, D), BF16),
        in_specs=in_specs,
        out_specs=BS((N_CHIPS, R_BR, D), lambda k: (0, blk, 0)),
        scratch_shapes=[pltpu.VMEM((D, D), F32)],
        input_output_aliases={} if gp is None else {2: 0},
        compiler_params=_params("arbitrary"),
    )(*args)


def _adamw_outer_call(w, cact, dmod, m, v, name):
    R, C = w.shape
    tr = 256
    c1 = 1.0 - ADAM_B1 ** ADAM_STEP
    c2 = 1.0 - ADAM_B2 ** ADAM_STEP

    def body(w_ref, a_ref, b_ref, m_ref, v_ref, g_ref, d_ref, m2_ref, v2_ref):
        g = _mm(a_ref[...], b_ref[...], TN, HI)
        g_ref[...] = g
        m2 = ADAM_B1 * m_ref[...] + (1.0 - ADAM_B1) * g
        v2 = ADAM_B2 * v_ref[...] + (1.0 - ADAM_B2) * (g * g)
        m2_ref[...] = m2
        v2_ref[...] = v2
        d_ref[...] = -ADAM_LR * ((m2 / c1) / (jnp.sqrt(v2 / c2) + ADAM_EPS) + ADAM_WD * w_ref[...])

    tile = BS((tr, C), lambda i: (i, 0))
    return pl.pallas_call(
        body, name=name, grid=(R // tr,), out_shape=(SDS((R, C), F32),) * 4,
        in_specs=[tile, BS((N_DEV, tr), lambda i: (0, i)), BS((N_DEV, C), lambda i: (0, 0)), tile, tile],
        out_specs=(tile,) * 4, compiler_params=_params("parallel"),
    )(w, cact, dmod, m, v)


def _adamw_call(w, g, m, v, name):
    R, C = w.shape
    tr = R
    while tr * C > 512 * 1024 and tr % 16 == 0:
        tr //= 2
    c1 = 1.0 - ADAM_B1 ** ADAM_STEP
    c2 = 1.0 - ADAM_B2 ** ADAM_STEP

    def body(w_ref, g_ref, m_ref, v_ref, d_ref, m2_ref, v2_ref):
        g = g_ref[...]
        m2 = ADAM_B1 * m_ref[...] + (1.0 - ADAM_B1) * g
        v2 = ADAM_B2 * v_ref[...] + (1.0 - ADAM_B2) * (g * g)
        m2_ref[...] = m2
        v2_ref[...] = v2
        d_ref[...] = -ADAM_LR * ((m2 / c1) / (jnp.sqrt(v2 / c2) + ADAM_EPS) + ADAM_WD * w_ref[...])

    tile = BS((tr, C), lambda i: (i, 0))
    return pl.pallas_call(
        body, name=name, grid=(R // tr,), out_shape=(SDS((R, C), F32),) * 3,
        in_specs=[tile] * 4, out_specs=(tile,) * 3, compiler_params=_params("parallel"),
    )(w, g, m, v)


def _adamw_rows_call(ws, g, ms, vs, name):
    k = len(ws)
    r = ws[0].shape[0]
    c1 = 1.0 - ADAM_B1 ** ADAM_STEP
    c2 = 1.0 - ADAM_B2 ** ADAM_STEP

    def body(g_ref, *refs):
        ins, outs = refs[:3 * k], refs[3 * k:]
        for j in range(k):
            w_ref, m_ref, v_ref = ins[j], ins[k + j], ins[2 * k + j]
            d_ref, m2_ref, v2_ref = outs[j], outs[k + j], outs[2 * k + j]
            g = g_ref[j * r:(j + 1) * r, :]
            m2 = ADAM_B1 * m_ref[...] + (1.0 - ADAM_B1) * g
            v2 = ADAM_B2 * v_ref[...] + (1.0 - ADAM_B2) * (g * g)
            m2_ref[...] = m2
            v2_ref[...] = v2
            d_ref[...] = -ADAM_LR * ((m2 / c1) / (jnp.sqrt(v2 / c2) + ADAM_EPS) + ADAM_WD * w_ref[...])

    out = pl.pallas_call(
        body, name=name, out_shape=(SDS(ws[0].shape, F32),) * (3 * k),
        compiler_params=pltpu.CompilerParams(vmem_limit_bytes=VMEM_LIMIT),
    )(g, *ws, *ms, *vs)
    return out[:k], out[k:2 * k], out[2 * k:]


def _adamw_join_call(w, g, m, v, full, name):
    R, C = w.shape
    tr = R
    while tr * C > 512 * 1024 and tr % 16 == 0:
        tr //= 2
    nsteps = R // tr
    c1 = 1.0 - ADAM_B1 ** ADAM_STEP
    c2 = 1.0 - ADAM_B2 ** ADAM_STEP

    def body(w_ref, g_ref, m_ref, v_ref, f_in, d_ref, m2_ref, v2_ref, f_out, send_sems, recv_sems):
        i = pl.program_id(0)
        start, finish = _join_exchange(f_in, f_out, send_sems, recv_sems)

        @pl.when(i == 0)
        def _():
            start()

        g = g_ref[...]
        m2 = ADAM_B1 * m_ref[...] + (1.0 - ADAM_B1) * g
        v2 = ADAM_B2 * v_ref[...] + (1.0 - ADAM_B2) * (g * g)
        m2_ref[...] = m2
        v2_ref[...] = v2
        d_ref[...] = -ADAM_LR * ((m2 / c1) / (jnp.sqrt(v2 / c2) + ADAM_EPS) + ADAM_WD * w_ref[...])

        @pl.when(i == nsteps - 1)
        def _():
            finish()

    tile = BS((tr, C), lambda i: (i, 0))
    hbm = BS(memory_space=pl.ANY)
    return pl.pallas_call(
        body, name=name, grid=(nsteps,), out_shape=(SDS((R, C), F32),) * 3 + (SDS(full.shape, full.dtype),),
        in_specs=[tile] * 4 + [hbm], out_specs=(tile,) * 3 + (hbm,), scratch_shapes=_join_sems(),
        input_output_aliases={4: 3}, compiler_params=_params("arbitrary"),
    )(w, g, m, v, full)


def _adamw_gather_call(w, g, m, v, srows, name):
    R, C = w.shape
    tr = R
    while tr * C > 512 * 1024 and tr % 16 == 0:
        tr //= 2
    nsteps = R // tr
    mr = srows.shape[0]
    c1 = 1.0 - ADAM_B1 ** ADAM_STEP
    c2 = 1.0 - ADAM_B2 ** ADAM_STEP

    def body(w_ref, g_ref, m_ref, v_ref, s_ref, d_ref, m2_ref, v2_ref, all_ref, sum_ref,
             x_scr, out_scr, send_sems, recv_sems, local_sem):
        i = pl.program_id(0)
        start, finish = _allgather_parts(x_scr, out_scr, send_sems, recv_sems, local_sem)

        @pl.when(i == 0)
        def _():
            x_scr[...] = s_ref[...]
            start()

        g = g_ref[...]
        m2 = ADAM_B1 * m_ref[...] + (1.0 - ADAM_B1) * g
        v2 = ADAM_B2 * v_ref[...] + (1.0 - ADAM_B2) * (g * g)
        m2_ref[...] = m2
        v2_ref[...] = v2
        d_ref[...] = -ADAM_LR * ((m2 / c1) / (jnp.sqrt(v2 / c2) + ADAM_EPS) + ADAM_WD * w_ref[...])

        @pl.when(i == nsteps - 1)
        def _():
            finish()
            all_ref[...] = out_scr[...]
            acc = out_scr[0:mr, :]
            for d in range(1, N_DEV):
                acc = acc + out_scr[d * mr:(d + 1) * mr, :]
            sum_ref[...] = acc

    tile = BS((tr, C), lambda i: (i, 0))
    return pl.pallas_call(
        body, name=name, grid=(nsteps,),
        out_shape=(SDS((R, C), F32),) * 3 + (SDS((N_DEV * mr, D), F32), SDS((mr, D), F32)),
        in_specs=[tile] * 4 + [BS((mr, D), lambda i: (0, 0))],
        out_specs=(tile,) * 3 + (BS((N_DEV * mr, D), lambda i: (0, 0)), BS((mr, D), lambda i: (0, 0))),
        scratch_shapes=[pltpu.VMEM((mr, D), F32), pltpu.VMEM((N_DEV * mr, D), F32)] + _allgather_sems(),
        compiler_params=_params("arbitrary"),
    )(w, g, m, v, srows)


def _rs_begin(g, c_idx, tag):
    n = g.shape[1]
    g = g.reshape(N_CHIPS, 2, n // 2, D)
    return _add_halves_call(g, _sibling_halves_call(g, tag), c_idx, tag)


def _local_step(x, mod, cact, target, wg, pack, small, c_idx, chip_idx):
    p, h1, wg = _fwd_in_call(x, mod, small["pre_tm"], wg, small["b_in"], pack, small["order"])
    o, oa, st, wg = _hgrn_fwd_call(p, small["logits"], small["hg_norm"], wg, pack)
    u, uc, cb, wg = _conv_fwd_call(p, small["conv_dw"], small["conv_db"], small["ln_g"], small["ln_b"], wg, pack)
    ya, yb, mg, y, x2, h2 = _merge_fwd_call(oa, cb, p, x, mod, small["post_tm"], small["pre_cm"], wg)
    z, da, dy2, dx2, acc_f = _ffn_call(h2, x2, target, mod, small["post_cm"], small["pre_cm"], wg)

    g_ff = _wgrad_call(None, h2, da, "wgrad_ff1", D, lambda i, j: (j, 0), 2 * R_FF)
    g_ff = _wgrad_call(g_ff, z, dy2, "wgrad_ff2", D, lambda i, j: (i, 1), 2 * R_FF)
    g_ff = g_ff.reshape(N_CHIPS, 2, R_FF, D)
    dy, dya, dyb, doa, dcb, dp_gt, acc_m, bs_gt, hr_ff = _merge_bwd_call(dx2, y, ya, yb, p, mod, small["post_tm"],
                                                                        wg, g_ff)
    part_ff = _add_halves_call(g_ff, hr_ff, c_idx, "ff")

    g_br = _wgrad_rows_call(None, oa, dya, "wgrad_br_a", 0)
    g_br = _wgrad_rows_call(g_br, cb, dyb, "wgrad_br_b", 1)
    g_br = _wgrad_rows_call(g_br, mg, dy, "wgrad_out", 2)
    g_br = g_br.reshape(N_CHIPS, 2, 3 * R_BR // 2, D)
    dp_hg, bs_hg, dlg, dgn, recv_ff, hr_br = _hgrn_bwd_call(p, o, doa, st, small["logits"], small["hg_norm"],
                                                            part_ff, g_br)
    part_br = _add_halves_call(g_br, hr_br, c_idx, "br")
    dp_cv, bs_cv, ddw, acc_c, recv_br = _conv_bwd_call(dcb, uc, u, p, small["conv_dw"], small["ln_g"], small["ln_b"],
                                                        part_br)

    g_in = _wgrad_call(None, h1, dp_hg, "wgrad_in_hg", D, lambda i, j: (j // 2, j % 2), R_IN)
    g_in = _wgrad_call(g_in, h1, dp_cv, "wgrad_in_cv", D, lambda i, j: (2, j), R_IN)
    g_in = _wgrad_call(g_in, h1, dp_gt, "wgrad_in_gt", D, lambda i, j: (3, j), R_IN)
    part_in = _rs_begin(g_in, c_idx, "in")
    chip_c = jnp.concatenate([chip_idx, c_idx])
    full_ff = _add_chips_call(part_ff, recv_ff, chip_c, "ff")
    full_br = _add_chips_call(part_br, recv_br, chip_c, "br")
    gx, acc_i, recv_in, full_ff, full_br = _in_bwd_call(dp_hg, dp_cv, dp_gt, x, dx2, mod, small["pre_tm"], wg,
                                                        part_in, full_ff, full_br)
    red_ff = full_ff.reshape(2 * R_FF, D)
    red_br = full_br.reshape(3 * R_BR, D)
    full_in = _add_chips_call(part_in, recv_in, chip_c, "in")

    zrow = jnp.zeros((1, D), F32)
    rows = [acc_i[0:1], acc_i[1:2], acc_m[0:1], acc_f[2:3], acc_f[3:4], acc_f[0:1],
            acc_i[2:3], acc_m[1:2], acc_f[4:5], acc_f[1:2],
            jnp.concatenate([bs_hg, bs_cv, bs_gt], axis=1).reshape(8, D),
            dlg, dgn, acc_c[0:1], acc_c[1:2], acc_c[2:3],
            ddw,
            cact, acc_f[5:6]] + [zrow] * 6
    return gx, jnp.concatenate(rows, axis=0), full_in, red_ff, red_br


def kernel(x, c, w_ada, b_ada, pre_norm_tm, post_norm_tm, pre_norm_cm, post_norm_cm, w_in, b_in, hg_lb_logits, hg_norm, conv_dw, conv_db, conv_ln_g, conv_ln_b, w_br_a, w_br_b, w_out, w_ff1, w_ff2, loss_target, m_w_ada, m_b_ada, m_pre_norm_tm, m_post_norm_tm, m_pre_norm_cm, m_post_norm_cm, m_w_in, m_b_in, m_hg_lb_logits, m_hg_norm, m_conv_dw, m_conv_db, m_conv_ln_g, m_conv_ln_b, m_w_br_a, m_w_br_b, m_w_out, m_w_ff1, m_w_ff2, v_w_ada, v_b_ada, v_pre_norm_tm, v_post_norm_tm, v_pre_norm_cm, v_post_norm_cm, v_w_in, v_b_in, v_hg_lb_logits, v_hg_norm, v_conv_dw, v_conv_db, v_conv_ln_g, v_conv_ln_b, v_w_br_a, v_w_br_b, v_w_out, v_w_ff1, v_w_ff2):
    xi, yi, ci = lax.axis_index("x"), lax.axis_index("y"), lax.axis_index("c")
    chip = 2 * xi + yi
    c_idx = jnp.reshape(ci, (1,)).astype(jnp.int32)
    chip_idx = jnp.reshape(chip, (1,)).astype(jnp.int32)

    def pack_small(ada_b, pre_t, post_t, pre_c, post_c, in_b, lg, hgn, cdb, lng, lnb, cdw):
        flat = jnp.concatenate([cdw[0].reshape(-1), jnp.zeros((8 * D - CONV_K * 256,), F32)]).reshape(8, D)
        return jnp.concatenate([ada_b.reshape(6, D), pre_t, post_t, pre_c, post_c, in_b.reshape(8, D), lg, hgn,
                                cdb, lng, lnb, flat], axis=0)

    w_in_halves = w_in[0].reshape(D, 2, D).transpose(1, 0, 2).reshape(R_IN, D)
    pack = jnp.concatenate([w_in_halves, w_ff1[0], w_ff2[0], w_br_a[0], w_br_b[0], w_out[0]],
                           axis=0).astype(BF16)
    wg = lax.dynamic_update_slice(lax.empty((N_CHIPS, PACK_W, D), BF16), pack[None], (chip, 0, 0))
    wa = 6 * D // N_CHIPS
    me = 4 * xi + 2 * yi + ci
    dw_blk = jnp.concatenate([conv_dw[0].reshape(-1), jnp.zeros((8 * D - CONV_K * 256,), F32)]).reshape(8, D)
    dw_all, ca_all, mod_all = _prologue_call(
        dw_blk, jnp.broadcast_to(c, (8, D)), w_ada[0].astype(BF16),
        lax.dynamic_slice_in_dim(b_ada, chip * wa, wa, axis=1))
    order = jnp.stack([chip, 2 * (1 - xi) + yi, 2 * xi + (1 - yi), 2 * (1 - xi) + (1 - yi)]).astype(jnp.int32)
    dw_all = dw_all.reshape(N_CHIPS, 2, 8 * D)[:, 0, :CONV_K * 256].reshape(N_CHIPS, CONV_K, 256)
    dw_full = dw_all.transpose(1, 0, 2).reshape(CONV_K, D)
    cact = lax.dynamic_slice_in_dim(ca_all, me * 8, 1, axis=0)
    mod_mine = lax.dynamic_index_in_dim(mod_all.reshape(N_CHIPS, 2, N_DEV, wa)[:, 0], me, axis=1,
                                        keepdims=False)
    mod = mod_mine.reshape(1, 6 * D)

    small = dict(pre_tm=pre_norm_tm, post_tm=post_norm_tm, pre_cm=pre_norm_cm, post_cm=post_norm_cm,
                 b_in=b_in, logits=hg_lb_logits, hg_norm=hg_norm, conv_dw=dw_full, conv_db=conv_db,
                 ln_g=conv_ln_g, ln_b=conv_ln_b, order=order)

    gx, srows, full_in, red_ff, red_br = _local_step(x[0], mod, cact, loss_target[0], wg, pack, small, c_idx,
                                                    chip_idx)

    shapes = {"in": w_in.shape, "br_a": w_br_a.shape, "br_b": w_br_b.shape, "out": w_out.shape,
              "ff1": w_ff1.shape, "ff2": w_ff2.shape}
    offs = {"ff1": (red_ff, 0, R_FF), "ff2": (red_ff, R_FF, 2 * R_FF),
            "br_a": (red_br, 0, R_BR), "br_b": (red_br, R_BR, 2 * R_BR), "out": (red_br, 2 * R_BR, 3 * R_BR)}
    wmv = {"in": (w_in, m_w_in, v_w_in), "br_a": (w_br_a, m_w_br_a, v_w_br_a), "br_b": (w_br_b, m_w_br_b, v_w_br_b),
           "out": (w_out, m_w_out, v_w_out), "ff1": (w_ff1, m_w_ff1, v_w_ff1), "ff2": (w_ff2, m_w_ff2, v_w_ff2)}
    res = {}
    for n in ("ff1", "in", "ff2"):
        shp = shapes[n]
        g2d = offs[n][0][offs[n][1]:offs[n][2]]
        if n == "in":
            g2d = g2d.reshape(2, D, D).transpose(1, 0, 2)
        g2d = g2d.reshape(shp[1], shp[2])
        w_, m_, v_ = (a[0] for a in wmv[n])
        if n == "ff1":
            d_, m2_, v2_, full_in = _adamw_join_call(w_, g2d, m_, v_, full_in, "adamw_ff1")
            offs["in"] = (full_in.reshape(R_IN, D), 0, R_IN)
        elif n == "in":
            d_, m2_, v2_, sall, ssum = _adamw_gather_call(w_, g2d, m_, v_, srows, "adamw_in")
        else:
            d_, m2_, v2_ = _adamw_call(w_, g2d, m_, v_, "adamw_" + n)
        res[n] = tuple(a.reshape(shp) for a in (g2d, d_, m2_, v2_))
    trio = ("br_a", "br_b", "out")
    d3, m3, v3 = _adamw_rows_call([wmv[n][0][0] for n in trio], red_br, [wmv[n][1][0] for n in trio],
                                  [wmv[n][2][0] for n in trio], "adamw_br")
    for j, n in enumerate(trio):
        res[n] = tuple(a.reshape(shapes[n]) for a in (red_br[j * R_BR:(j + 1) * R_BR], d3[j], m3[j], v3[j]))

    sall = sall.reshape(N_DEV, SMALL_ROWS, D)
    loss = jnp.sum(ssum[57])
    dmod_all = sall[:, 0:6, :].reshape(N_DEV, 6 * D)
    g_ada, d_, m2_, v2_ = _adamw_outer_call(w_ada[0], sall[:, 56, :],
                                            lax.dynamic_slice_in_dim(dmod_all, chip * wa, wa, axis=1),
                                            m_w_ada[0], v_w_ada[0], "adamw_ada")
    res["ada"] = tuple(a.reshape(w_ada.shape) for a in (g_ada, d_, m2_, v2_))
    g_dw = lax.dynamic_slice_in_dim(ssum[24:24 + CONV_K], chip * 256, 256, axis=1)
    g_small = jnp.concatenate(
        [ssum[0:24], jnp.concatenate([g_dw.reshape(-1), jnp.zeros((8 * D - CONV_K * 256,), F32)]).reshape(8, D)],
        axis=0)

    ws = pack_small(b_ada, pre_norm_tm, post_norm_tm, pre_norm_cm, post_norm_cm, b_in, hg_lb_logits, hg_norm,
                    conv_db, conv_ln_g, conv_ln_b, conv_dw)
    ms = pack_small(m_b_ada, m_pre_norm_tm, m_post_norm_tm, m_pre_norm_cm, m_post_norm_cm, m_b_in, m_hg_lb_logits,
                    m_hg_norm, m_conv_db, m_conv_ln_g, m_conv_ln_b, m_conv_dw)
    vs = pack_small(v_b_ada, v_pre_norm_tm, v_post_norm_tm, v_pre_norm_cm, v_post_norm_cm, v_b_in, v_hg_lb_logits,
                    v_hg_norm, v_conv_db, v_conv_ln_g, v_conv_ln_b, v_conv_dw)
    sres = (g_small,) + tuple(_adamw_call(ws, g_small, ms, vs, "adamw_small"))

    def unpack_small(t):
        return {"b_ada": t[0:6].reshape(1, 6 * D), "pre_tm": t[6:7], "post_tm": t[7:8], "pre_cm": t[8:9],
                "post_cm": t[9:10], "b_in": t[10:18].reshape(1, IN_COLS), "logits": t[18:20], "hg_norm": t[20:21],
                "conv_db": t[21:22], "ln_g": t[22:23], "ln_b": t[23:24],
                "conv_dw": t[24:32].reshape(-1)[:CONV_K * 256].reshape(1, CONV_K, 256)}

    order = ["ada", "b_ada", "pre_tm", "post_tm", "pre_cm", "post_cm", "in", "b_in", "logits", "hg_norm", "conv_dw",
             "conv_db", "ln_g", "ln_b", "br_a", "br_b", "out", "ff1", "ff2"]
    outs = [loss, gx.reshape(x.shape)]
    for kind in range(4):
        sm = unpack_small(sres[kind])
        for n in order:
            outs.append(res[n][kind] if n in res else sm[n])
    return tuple(outs)
```

```python
import jax
import jax.numpy as jnp
from jax import lax
from jax.experimental import pallas as pl
from jax.experimental.pallas import tpu as pltpu

F32, BF16 = jnp.float32, jnp.bfloat16
SDS = jax.ShapeDtypeStruct
BS = pl.BlockSpec
MESH = pl.DeviceIdType.MESH
HI = lax.Precision.HIGHEST

D = 1024
D_FF = 4096
IN_COLS = 8192
HEADS, DK = 8, 128
CHUNK = 128
CONV_K = 31
HALO = 32
SUB = 32
EPS = 1e-6
N_CHIPS, N_DEV = 4, 8
TM = 256
TB = 256
VMEM_LIMIT = 56 * 1024 * 1024

R_IN, R_BR, R_FF = 2048, 256, 1024
PACK_W = R_IN + 3 * R_BR + 2 * R_FF
O_IN, O_FF1, O_FF2, O_BRA, O_BRB, O_OUT = 0, 2048, 3072, 4096, 4352, 4608
SMALL_ROWS = 64

ADAM_LR, ADAM_B1, ADAM_B2, ADAM_EPS, ADAM_WD, ADAM_STEP = 0.001, 0.9, 0.999, 1e-08, 0.01, 10

NN = (((1,), (0,)), ((), ()))
NT = (((1,), (1,)), ((), ()))
TN = (((0,), (0,)), ((), ()))


def _mm(a, b, dims=NN, precision=None):
    return lax.dot_general(a, b, dims, preferred_element_type=F32, precision=precision)


def _sig(v):
    return jax.nn.sigmoid(v)


def _dsilu(v, s):
    return s * (1.0 + v * (1.0 - s))


def _params(*sem):
    return pltpu.CompilerParams(dimension_semantics=sem if sem else None, vmem_limit_bytes=VMEM_LIMIT)


def _rowsum(v):
    return jnp.sum(v, axis=0, keepdims=True)


def _mesh_pos():
    return lax.axis_index("x"), lax.axis_index("y"), lax.axis_index("c")


def _allgather_parts(x_ref, out_ref, send_sems, recv_sems, local_sem):
    m_per = x_ref.shape[0]
    x, y, c = _mesh_pos()
    me, sibling = (x, y, c), (x, y, 1 - c)
    chips = [(1 - x, y), (x, 1 - y), (1 - x, 1 - y)]

    def rows(px, py, pc):
        return out_ref.at[pl.ds((4 * px + 2 * py + pc) * m_per, m_per), :]

    def copy(k, block, to, src=None):
        return pltpu.make_async_remote_copy(
            src_ref=rows(*block) if src is None else src, dst_ref=rows(*block),
            send_sem=send_sems.at[k], recv_sem=recv_sems.at[k], device_id=to, device_id_type=MESH)

    def first():
        return [copy(0, me, sibling, src=x_ref)] + [copy(1 + j, me, (*chip, c), src=x_ref)
                                                    for j, chip in enumerate(chips)]

    def start():
        pltpu.make_async_copy(x_ref, rows(*me), local_sem).start()
        for cp in first():
            cp.start()

    def finish():
        passed = [copy(4 + j, (*chip, c), sibling) for j, chip in enumerate(chips)]
        for j, chip in enumerate(chips):
            copy(1 + j, (*chip, c), me).wait_recv()
            passed[j].start()
        copy(0, sibling, me).wait_recv()
        for j, chip in enumerate(chips):
            copy(4 + j, (*chip, 1 - c), me).wait_recv()
        for cp in first() + passed:
            cp.wait_send()
        pltpu.make_async_copy(x_ref, rows(*me), local_sem).wait()

    return start, finish


def _allgather_sems():
    return [pltpu.SemaphoreType.DMA((7,)), pltpu.SemaphoreType.DMA((7,)), pltpu.SemaphoreType.DMA]
def _gather_sems(n_ranges):
    return [pltpu.SemaphoreType.DMA((6 * n_ranges,)), pltpu.SemaphoreType.DMA((6 * n_ranges,))]


def _pack_gather(pack_ref, wg_ref, send_sems, recv_sems, ranges):
    x, y, c = _mesh_pos()
    me, sibling = (x, y, c), (x, y, 1 - c)
    chips = [(1 - x, y), (x, 1 - y), (1 - x, 1 - y)]

    def land(r, px, py, pc):
        off, n = ranges[r]
        return wg_ref.at[2 * px + py, pl.ds(off + pc * (n // 2), n // 2), :]

    def mine(r):
        off, n = ranges[r]
        return pack_ref.at[pl.ds(off + c * (n // 2), n // 2), :]

    def copy(r, k, block, to, src=None):
        return pltpu.make_async_remote_copy(
            src_ref=land(r, *block) if src is None else src, dst_ref=land(r, *block),
            send_sem=send_sems.at[6 * r + k], recv_sem=recv_sems.at[6 * r + k], device_id=to, device_id_type=MESH)

    def start():
        for r in range(len(ranges)):
            for j, chip in enumerate(chips):
                copy(r, j, me, (*chip, c), src=mine(r)).start()

    def finish():
        for r in range(len(ranges)):
            for j, chip in enumerate(chips):
                copy(r, j, (*chip, c), me).wait_recv()
                copy(r, 3 + j, (*chip, c), sibling).start()
        for r in range(len(ranges)):
            for j, chip in enumerate(chips):
                copy(r, 3 + j, (*chip, 1 - c), me).wait_recv()
        for r in range(len(ranges)):
            for j, chip in enumerate(chips):
                copy(r, j, me, (*chip, c), src=mine(r)).wait_send()
                copy(r, 3 + j, (*chip, c), sibling).wait_send()

    return start, finish


def _relay_sems():
    return [pltpu.SemaphoreType.DMA((8,)), pltpu.SemaphoreType.DMA((8,))]


def _relay_gather(pack_ref, wg_ref, send_sems, recv_sems, off, n):
    x, y, c = _mesh_pos()
    me, sibling = (x, y, c), (x, y, 1 - c)
    chips = [(1 - x, y), (x, 1 - y), (1 - x, 1 - y)]
    h, q = n // 2, n // 4

    def land(px, py, pc, piece=None):
        if piece is None:
            return wg_ref.at[2 * px + py, pl.ds(off + pc * h, h), :]
        return wg_ref.at[2 * px + py, pl.ds(off + pc * h + piece * q, q), :]

    def copy(k, ref, to, src=None):
        return pltpu.make_async_remote_copy(
            src_ref=ref if src is None else src, dst_ref=ref, send_sem=send_sems.at[k], recv_sem=recv_sems.at[k],
            device_id=to, device_id_type=MESH)

    def direct(j):
        return copy(j, land(x, y, c), (*chips[j], c), src=pack_ref.at[pl.ds(off + c * h, h), :])

    def relayed(j):
        if j == 0:
            return copy(6, land(*chips[0], c, 1), (x, 1 - y, c))
        return copy(7, land(*chips[1], c, 0), (1 - x, y, c))

    def start():
        direct(0).start()
        direct(1).start()

    def arrive(j):
        if j == 0:
            for k in range(2):
                copy(k, land(*chips[k], c), me).wait_recv()
                relayed(k).start()
                copy(3 + k, land(*chips[k], c), sibling).start()
        if j == 2:
            copy(7, land(*chips[2], c, 0), me).wait_recv()
            copy(6, land(*chips[2], c, 1), me).wait_recv()
            copy(5, land(*chips[2], c), sibling).start()
        copy(3 + j, land(*chips[j], 1 - c), me).wait_recv()

    def drain():
        for j in range(2):
            direct(j).wait_send()
            relayed(j).wait_send()
        for j in range(3):
            copy(3 + j, land(*chips[j], c), sibling).wait_send()

    return start, arrive, drain


def _prologue_call(dw_blk, c_blk, w_ada, b_ada):
    wa = w_ada.shape[1]

    def body(dw_ref, c_ref, wa_ref, ba_ref, dwg_ref, ca_ref, modg_ref,
             cg_scr, part_scr, s1, r1, l1, s2, r2, l2, s3, r3, l3):
        start_c, finish_c = _allgather_parts(c_ref, cg_scr, s2, r2, l2)
        start_dw, finish_dw = _allgather_parts(dw_ref, dwg_ref, s1, r1, l1)
        start_mod, finish_mod = _allgather_parts(part_scr, modg_ref, s3, r3, l3)
        start_c()
        start_dw()
        finish_c()
        cv = cg_scr[...]
        ca = cv * _sig(cv)
        ca_ref[...] = ca
        pick = (lax.broadcasted_iota(jnp.int32, (N_DEV, N_DEV * 8), 1)
                == 8 * lax.broadcasted_iota(jnp.int32, (N_DEV, N_DEV * 8), 0)).astype(BF16)
        ca8 = _mm(pick, ca.astype(BF16)).astype(BF16)
        part_scr[...] = _mm(ca8, wa_ref[...]) + ba_ref[...]
        start_mod()
        finish_dw()
        finish_mod()

    vm = BS(memory_space=pltpu.VMEM)
    return pl.pallas_call(
        body, name="prologue_adaln_conv_dw",
        out_shape=(SDS((N_DEV * 8, D), F32), SDS((N_DEV * 8, D), F32), SDS((N_DEV * N_DEV, wa), F32)),
        in_specs=[vm, vm, vm, vm], out_specs=(vm, vm, vm),
        scratch_shapes=[pltpu.VMEM((N_DEV * 8, D), F32), pltpu.VMEM((N_DEV, wa), F32)]
        + _allgather_sems() + _allgather_sems() + _allgather_sems(),
        compiler_params=pltpu.CompilerParams(vmem_limit_bytes=VMEM_LIMIT),
    )(dw_blk, c_blk, w_ada, b_ada)


def _halves_exchange(g_ref, out_ref, send_sems, recv_sems):
    x, y, c = _mesh_pos()

    def copies():
        return [pltpu.make_async_remote_copy(
            src_ref=g_ref.at[k, 1 - c], dst_ref=out_ref.at[k], send_sem=send_sems.at[k], recv_sem=recv_sems.at[k],
            device_id=(x, y, 1 - c), device_id_type=MESH) for k in range(N_CHIPS)]

    def start():
        for cp in copies():
            cp.start()

    def finish():
        for cp in copies():
            cp.wait()

    return start, finish


def _halves_sems():
    return [pltpu.SemaphoreType.DMA((N_CHIPS,)), pltpu.SemaphoreType.DMA((N_CHIPS,))]


def _sibling_halves_call(g, tag):
    _, _, h, n = g.shape

    def body(g_ref, out_ref, send_sems, recv_sems):
        start, finish = _halves_exchange(g_ref, out_ref, send_sems, recv_sems)
        start()
        finish()

    return pl.pallas_call(
        body, name="rs_sibling_halves_" + tag, out_shape=SDS((N_CHIPS, h, n), g.dtype),
        in_specs=[BS(memory_space=pl.ANY)], out_specs=BS(memory_space=pl.ANY),
        scratch_shapes=_halves_sems(),
    )(g)


def _chip_exchange(p_ref, out_ref, send_sems, recv_sems):
    x, y, c = _mesh_pos()
    chips = [(1 - x, y), (x, 1 - y), (1 - x, 1 - y)]

    def copies():
        return [pltpu.make_async_remote_copy(
            src_ref=p_ref.at[2 * cx + cy], dst_ref=out_ref.at[j], send_sem=send_sems.at[j], recv_sem=recv_sems.at[j],
            device_id=(cx, cy, c), device_id_type=MESH) for j, (cx, cy) in enumerate(chips)]

    def start():
        for cp in copies():
            cp.start()

    def finish():
        for cp in copies():
            cp.wait()

    return start, finish


def _exchange_sems():
    return [pltpu.SemaphoreType.DMA((3,)), pltpu.SemaphoreType.DMA((3,))]


def _join_exchange(in_ref, out_ref, send_sems, recv_sems):
    h = in_ref.shape[1]
    q = h // 4
    x, y, c = _mesh_pos()

    def copy(k, half):
        return pltpu.make_async_remote_copy(
            src_ref=in_ref.at[half, pl.ds(k * q, q)], dst_ref=out_ref.at[half, pl.ds(k * q, q)],
            send_sem=send_sems.at[k], recv_sem=recv_sems.at[k],
            device_id=(x, y, 1 - c), device_id_type=MESH)

    def start():
        for k in range(4):
            copy(k, c).start()

    def finish():
        for k in range(4):
            copy(k, c).wait_send()
            copy(k, 1 - c).wait_recv()

    return start, finish


def _join_sems():
    return [pltpu.SemaphoreType.DMA((4,)), pltpu.SemaphoreType.DMA((4,))]


def _sibling_join_call(full, tag):
    def body(in_ref, out_ref, send_sems, recv_sems):
        start, finish = _join_exchange(in_ref, out_ref, send_sems, recv_sems)
        start()
        finish()

    return pl.pallas_call(
        body, name="rs_sibling_join_" + tag, out_shape=SDS(full.shape, full.dtype),
        in_specs=[BS(memory_space=pl.ANY)], out_specs=BS(memory_space=pl.ANY),
        scratch_shapes=_join_sems(), input_output_aliases={0: 0},
    )(full)


def _add_halves_call(g, recv, c_idx, tag):
    _, _, h, n = g.shape
    tr = h // 2

    def body(c_ref, g_ref, r_ref, o_ref):
        o_ref[...] = (g_ref[...].astype(F32) + r_ref[...].astype(F32)).astype(BF16)

    return pl.pallas_call(
        body, name="rs_add_halves_" + tag, out_shape=SDS((N_CHIPS, h, n), BF16),
        grid_spec=pltpu.PrefetchScalarGridSpec(
            num_scalar_prefetch=1, grid=(N_CHIPS, 2),
            in_specs=[BS((None, None, tr, n), lambda k, r, c_ref: (k, c_ref[0], r, 0)),
                      BS((None, tr, n), lambda k, r, c_ref: (k, r, 0))],
            out_specs=BS((None, tr, n), lambda k, r, c_ref: (k, r, 0))),
        compiler_params=_params("arbitrary", "arbitrary"),
    )(c_idx, g, recv)


def _add_chips_call(p, recv, chip_c_idx, tag):
    _, h, n = p.shape
    tr = h // 2

    def body(k_ref, p_ref, r_ref, o_ref):
        acc = p_ref[...].astype(F32)
        for j in range(3):
            acc = acc + r_ref[j].astype(F32)
        o_ref[...] = acc

    return pl.pallas_call(
        body, name="rs_add_chips_" + tag, out_shape=SDS((2, h, n), F32),
        grid_spec=pltpu.PrefetchScalarGridSpec(
            num_scalar_prefetch=1, grid=(2,),
            in_specs=[BS((None, tr, n), lambda r, k_ref: (k_ref[0], r, 0)),
                      BS((3, tr, n), lambda r, k_ref: (0, r, 0))],
            out_specs=BS((None, tr, n), lambda r, k_ref: (k_ref[1], r, 0))),
        compiler_params=_params("arbitrary"),
    )(chip_c_idx, p, recv)


def _load_rows(wg_hbm, w_vmem, sem, off):
    cp = pltpu.make_async_copy(wg_hbm.at[:, pl.ds(off, w_vmem.shape[1]), :], w_vmem, sem)
    cp.start()
    return cp


def _fwd_in_call(x, mod, pre_tm, wg, b_in, pack, order):
    S = x.shape[0]
    tmf = 2 * TM
    nt = S // tmf
    wc = IN_COLS // N_CHIPS

    def body(ord_ref, x_ref, mod_ref, g_ref, w_hbm, b_ref, pack_ref, p_ref, h_hbm, wg_out, w_vmem, h_scr, sems,
             send_sems, recv_sems, send_sems2, recv_sems2):
        q, i = pl.program_id(0), pl.program_id(1)
        rows = pl.ds(pl.multiple_of(i * tmf, tmf), tmf)
        start, arrive, drain = _relay_gather(pack_ref, wg_out, send_sems, recv_sems, O_IN, R_IN)
        start2, finish2 = _pack_gather(pack_ref, wg_out, send_sems2, recv_sems2, [(O_OUT, R_BR)])

        def weights(phase):
            return pltpu.make_async_copy(wg_out.at[ord_ref[phase], pl.ds(O_IN, R_IN), :], w_vmem.at[phase % 2],
                                         sems.at[phase % 2])

        @pl.when((q == 0) & (i == 0))
        def _():
            start()
            weights(0).start()
            weights(0).wait()

        @pl.when((q == 1) & (i == 0))
        def _():
            arrive(0)
            start2()
            weights(1).start()
            weights(1).wait()
            arrive(1)
            weights(2).start()

        @pl.when((q == 2) & (i == 0))
        def _():
            weights(2).wait()
            arrive(2)
            weights(3).start()

        @pl.when((q == 3) & (i == 0))
        def _():
            weights(3).wait()

        @pl.when(q == 0)
        def _():
            xv = x_ref[...]
            r = lax.rsqrt(jnp.mean(xv * xv, axis=-1, keepdims=True) + EPS)
            h = xv * r * g_ref[...] * (1.0 + mod_ref[:, D:2 * D]) + mod_ref[:, 0:D]
            h_scr[rows, :] = h.astype(BF16)

        hb = h_scr[rows, :]
        slot = q % 2
        for k in range(wc // D):
            p_ref[:, k * D:(k + 1) * D] = _mm(hb, w_vmem[slot, k * D:(k + 1) * D, :]) + b_ref[:, k * D:(k + 1) * D]

        @pl.when((q == N_CHIPS - 1) & (i == nt - 1))
        def _():
            cp = pltpu.make_async_copy(h_scr, h_hbm, sems.at[0])
            cp.start()
            drain()
            finish2()
            cp.wait()

    hbm = BS(memory_space=pl.ANY)
    return pl.pallas_call(
        body, name="fwd_in", out_shape=(SDS((S, IN_COLS), F32), SDS((S, D), BF16), SDS(wg.shape, wg.dtype)),
        grid_spec=pltpu.PrefetchScalarGridSpec(
            num_scalar_prefetch=1, grid=(N_CHIPS, nt),
            in_specs=[BS((tmf, D), lambda q, i, o: (jnp.where(q == 0, i, nt - 1), 0)),
                      BS((1, 6 * D), lambda q, i, o: (0, 0)),
                      BS((1, D), lambda q, i, o: (0, 0)), hbm, BS((1, wc), lambda q, i, o: (0, o[q])), hbm],
            out_specs=(BS((tmf, wc), lambda q, i, o: (i, o[q])), hbm, hbm),
            scratch_shapes=[pltpu.VMEM((2, R_IN, D), BF16), pltpu.VMEM((S, D), BF16), pltpu.SemaphoreType.DMA((2,))]
            + _relay_sems() + _gather_sems(1)),
        input_output_aliases={4: 2},
        compiler_params=_params("arbitrary", "arbitrary"),
    )(order, x, mod, pre_tm, wg, b_in, pack)


def _lower_bound(lg_ref):
    l0, l1 = lg_ref[0:1, :], lg_ref[1:2, :]
    mx = jnp.maximum(l0, l1)
    e0, e1 = jnp.exp(l0 - mx), jnp.exp(l1 - mx)
    return e0 / (e0 + e1)


def _tri_masks():
    ri = lax.broadcasted_iota(jnp.int32, (CHUNK, CHUNK), 0)
    ci = lax.broadcasted_iota(jnp.int32, (CHUNK, CHUNK), 1)
    return (ri >= ci).astype(F32), (ci >= ri).astype(F32)


def _cumsum_mm(tri, g):
    tb = tri.astype(BF16)
    hi = g.astype(BF16)
    r1 = g - hi.astype(F32)
    mid = r1.astype(BF16)
    lo = (r1 - mid.astype(F32)).astype(BF16)
    return _mm(tb, hi) + _mm(tb, mid) + _mm(tb, lo)


def _hg_gates(q_r, f_r, lb, tril):
    sq = _sig(q_r)
    q = q_r * sq
    sf = _sig(f_r)
    f = lb + (1.0 - lb) * sf
    k = 1.0 - f
    g = jnp.log(f)
    b = _cumsum_mm(tril, g)
    b_last = _rowsum(g)
    row = lax.broadcasted_iota(jnp.int32, g.shape, 0)
    ref = _rowsum(jnp.where(row < CHUNK // 2, g, 0.0))
    e = jnp.exp(b)
    eq = jnp.exp(jnp.minimum(b - ref, 80.0))
    ek = jnp.exp(jnp.minimum(ref - b, 80.0))
    dd = jnp.exp(b_last - b)
    return dict(sq=sq, q=q, sf=sf, f=f, k=k, e=e, eq=eq, ek=ek, dd=dd, elast=jnp.exp(b_last),
                qe=q * e, qt=q * eq, kt=k * ek, kd=k * dd)


def _hgrn_fwd_call(p, logits, gn, wg, pack):
    S = p.shape[0]
    ncb = TB // CHUNK
    ranges = [(O_FF1, R_FF)]

    def body(q_ref, f_ref, v_ref, og_ref, lg_ref, gn_ref, wg_in, pack_ref, o_ref, oa_ref, st_ref, wg_out,
             st_scr, send_sems, recv_sems):
        start, finish = _pack_gather(pack_ref, wg_out, send_sems, recv_sems, ranges)

        @pl.when(pl.program_id(0) == 0)
        def _():
            start()
            st_scr[...] = jnp.zeros_like(st_scr)

        lb = _lower_bound(lg_ref)
        tril, _ = _tri_masks()

        def chunk(ci, carry):
            rows = pl.ds(pl.multiple_of(ci * CHUNK, CHUNK), CHUNK)
            st_ref[ci] = st_scr[...]
            t = _hg_gates(q_ref[rows, :], f_ref[rows, :], lb, tril)
            v = v_ref[rows, :]
            for h in range(HEADS):
                sl = slice(h * DK, (h + 1) * DK)
                stp = st_scr[:, sl]
                vb = v[:, sl].astype(BF16)
                inter = _mm(t["qe"][:, sl].astype(BF16), stp.astype(BF16), NT)
                a = jnp.where(tril > 0.5, _mm(t["qt"][:, sl].astype(BF16), t["kt"][:, sl].astype(BF16), NT), 0.0)
                o = inter + _mm(a.astype(BF16), vb)
                st_scr[:, sl] = stp * t["elast"][:, sl] + _mm(vb, t["kd"][:, sl].astype(BF16), TN)
                oh = o * lax.rsqrt(jnp.mean(o * o, axis=-1, keepdims=True) + EPS)
                og = og_ref[rows, sl]
                o_ref[rows, sl] = o
                oa_ref[rows, sl] = (oh * gn_ref[:, sl] * (og * _sig(og))).astype(BF16)
            return carry

        lax.fori_loop(0, ncb, chunk, 0)

        @pl.when(pl.program_id(0) == S // TB - 1)
        def _():
            finish()

    col = lambda j: BS((TB, D), lambda i, j=j: (i, j))
    hbm = BS(memory_space=pl.ANY)
    return pl.pallas_call(
        body, name="hgrn_fwd", grid=(S // TB,),
        out_shape=(SDS((S, D), F32), SDS((S, D), BF16), SDS((S // CHUNK, DK, D), F32), SDS(wg.shape, wg.dtype)),
        in_specs=[col(0), col(1), col(2), col(3), BS((2, D), lambda i: (0, 0)), BS((1, D), lambda i: (0, 0)),
                  hbm, hbm],
        out_specs=(BS((TB, D), lambda i: (i, 0)), BS((TB, D), lambda i: (i, 0)),
                   BS((ncb, DK, D), lambda i: (i, 0, 0)), hbm),
        scratch_shapes=[pltpu.VMEM((DK, D), F32)] + _gather_sems(len(ranges)),
        input_output_aliases={6: 3},
        compiler_params=_params("arbitrary"),
    )(p, p, p, p, logits, gn, wg, pack)


def _layernorm_stats(uc):
    mu = jnp.mean(uc, axis=-1, keepdims=True)
    xc = uc - mu
    rs = lax.rsqrt(jnp.mean(xc * xc, axis=-1, keepdims=True) + EPS)
    return xc * rs, rs


EXT = HALO + TM + 8


def _fill_shifted(ext, shifted):
    for m in range(1, 8):
        shifted[m - 1] = ext[m:m + HALO + TM, :]


def _window(ext, shifted, s0, n):
    m = s0 % 8
    q = s0 - m
    return ext[q:q + n, :] if m == 0 else shifted[m - 1, q:q + n, :]


def _conv_fwd_call(p, dw, db, ln_g, ln_b, wg, pack):
    S = p.shape[0]
    ranges = [(O_FF2, R_FF), (O_BRA, 2 * R_BR)]

    def body(cv_ref, cg_ref, dw_ref, db_ref, g_ref, b_ref, wg_in, pack_ref, u_ref, uc_ref, cb_ref, wg_out,
             uext, ush, send_sems, recv_sems):
        start, finish = _pack_gather(pack_ref, wg_out, send_sems, recv_sems, ranges)

        @pl.when(pl.program_id(0) == 0)
        def _():
            start()
            uext[0:HALO, :] = jnp.zeros((HALO, D), F32)
            uext[HALO + TM:EXT, :] = jnp.zeros((EXT - HALO - TM, D), F32)

        u = cv_ref[...] * _sig(cg_ref[...])
        uext[HALO:HALO + TM, :] = u
        u_ref[...] = u
        _fill_shifted(uext, ush)
        for rb in range(TM // SUB):
            acc = jnp.broadcast_to(db_ref[...], (SUB, D))
            for j in range(CONV_K):
                s0 = HALO - (CONV_K - 1) + j + rb * SUB
                acc = acc + dw_ref[j:j + 1, :] * _window(uext, ush, s0, SUB)
            uc_ref[rb * SUB:(rb + 1) * SUB, :] = acc
            xh, _ = _layernorm_stats(acc)
            ln = xh * g_ref[...] + b_ref[...]
            cb_ref[rb * SUB:(rb + 1) * SUB, :] = (ln * _sig(ln)).astype(BF16)
        uext[0:HALO, :] = uext[TM:TM + HALO, :]

        @pl.when(pl.program_id(0) == S // TM - 1)
        def _():
            finish()

    vec = BS((1, D), lambda i: (0, 0))
    hbm = BS(memory_space=pl.ANY)
    return pl.pallas_call(
        body, name="conv_fwd", grid=(S // TM,),
        out_shape=(SDS((S, D), F32), SDS((S, D), F32), SDS((S, D), BF16), SDS(wg.shape, wg.dtype)),
        in_specs=[BS((TM, D), lambda i: (i, 4)), BS((TM, D), lambda i: (i, 5)),
                  BS((CONV_K, D), lambda i: (0, 0)), vec, vec, vec, hbm, hbm],
        out_specs=(BS((TM, D), lambda i: (i, 0)),) * 3 + (hbm,),
        scratch_shapes=[pltpu.VMEM((EXT, D), F32), pltpu.VMEM((7, HALO + TM, D), F32)] + _gather_sems(len(ranges)),
        input_output_aliases={6: 3},
        compiler_params=_params("arbitrary"),
    )(p, p, dw, db, ln_g, ln_b, wg, pack)


def _mm_rows(a, w_ref):
    acc = _mm(a[:, 0:R_BR], w_ref[0])
    for k in range(1, N_CHIPS):
        acc = acc + _mm(a[:, k * R_BR:(k + 1) * R_BR], w_ref[k])
    return acc


def _mm_rows_t(a, w_ref):
    return jnp.concatenate([_mm(a, w_ref[k], NT) for k in range(N_CHIPS)], axis=1)


def _br_spec(off):
    return BS((N_CHIPS, R_BR, D), lambda i: (0, off // R_BR, 0))


def _merge_fwd_call(oa, cb, p, x, mod, post_tm, pre_cm, wg):
    S = x.shape[0]

    def body(oa_ref, cb_ref, ga_ref, gb_ref, x_ref, mod_ref, post_ref, pre_ref, wa_ref, wb_ref, wo_ref,
             ya_ref, yb_ref, mg_ref, y_ref, x2_ref, h2_ref):
        ya = _mm_rows(oa_ref[...], wa_ref)
        yb = _mm_rows(cb_ref[...], wb_ref)
        ya_ref[...] = ya.astype(BF16)
        yb_ref[...] = yb.astype(BF16)
        mg = (_sig(ga_ref[...]) * ya + _sig(gb_ref[...]) * yb).astype(BF16)
        mg_ref[...] = mg
        y = _mm_rows(mg, wo_ref)
        y_ref[...] = y
        n = y * lax.rsqrt(jnp.mean(y * y, axis=-1, keepdims=True) + EPS) * post_ref[...]
        x2 = x_ref[...] + mod_ref[:, 2 * D:3 * D] * n
        x2_ref[...] = x2
        r2 = lax.rsqrt(jnp.mean(x2 * x2, axis=-1, keepdims=True) + EPS)
        h2 = x2 * r2 * pre_ref[...] * (1.0 + mod_ref[:, 4 * D:5 * D]) + mod_ref[:, 3 * D:4 * D]
        h2_ref[...] = h2.astype(BF16)

    tile = BS((TM, D), lambda i: (i, 0))
    vec = BS((1, D), lambda i: (0, 0))
    return pl.pallas_call(
        body, name="merge_fwd", grid=(S // TM,),
        out_shape=(SDS((S, D), BF16), SDS((S, D), BF16), SDS((S, D), BF16), SDS((S, D), F32), SDS((S, D), F32),
                   SDS((S, D), BF16)),
        in_specs=[tile, tile, BS((TM, D), lambda i: (i, 6)), BS((TM, D), lambda i: (i, 7)), tile,
                  BS((1, 6 * D), lambda i: (0, 0)), vec, vec, _br_spec(O_BRA), _br_spec(O_BRB), _br_spec(O_OUT)],
        out_specs=(tile,) * 6,
        compiler_params=_params("arbitrary"),
    )(oa, cb, p, p, x, mod, post_tm, pre_cm, wg, wg, wg)


def _ffn_call(h2, x2, target, mod, post_cm, pre_cm, wg):
    S = x2.shape[0]

    def body(h2_ref, x2_ref, t_ref, mod_ref, post_ref, pre_ref, w_hbm,
             z_ref, da_ref, dy2_ref, dx2_ref, acc_ref, w1_v, w2_v, ra_scr, sems):
        @pl.when(pl.program_id(0) == 0)
        def _():
            c1 = _load_rows(w_hbm, w1_v, sems.at[0], O_FF1)
            c2 = _load_rows(w_hbm, w2_v, sems.at[1], O_FF2)
            c1.wait()
            c2.wait()
            acc_ref[...] = jnp.zeros_like(acc_ref)

        h2 = h2_ref[...]
        for k in range(N_CHIPS):
            ra = jnp.maximum(_mm(h2, w1_v[k]), 0.0)
            ra_scr[:, k * D:(k + 1) * D] = ra
            z_ref[:, k * D:(k + 1) * D] = (ra * ra).astype(BF16)
        y2 = _mm(z_ref[:, 0:D], w2_v[0])
        for k in range(1, N_CHIPS):
            y2 = y2 + _mm(z_ref[:, k * D:(k + 1) * D], w2_v[k])
        ry = lax.rsqrt(jnp.mean(y2 * y2, axis=-1, keepdims=True) + EPS)
        yn = y2 * ry
        n = yn * post_ref[...]
        g2 = mod_ref[:, 5 * D:6 * D]
        x2 = x2_ref[...]
        err = x2 + g2 * n - t_ref[...]
        acc_ref[5:6, :] += _rowsum(err * err) * (0.5 / D)
        dout = err * (1.0 / D)
        acc_ref[0:1, :] += _rowsum(dout * n)
        dn = dout * g2
        acc_ref[1:2, :] += _rowsum(dn * yn)
        dyn = dn * post_ref[...]
        dy2 = (ry * (dyn - yn * jnp.mean(dyn * yn, axis=-1, keepdims=True))).astype(BF16)
        dy2_ref[...] = dy2
        for k in range(N_CHIPS):
            dz = _mm(dy2, w2_v[k], NT)
            da_ref[:, k * D:(k + 1) * D] = (dz * (2.0 * ra_scr[:, k * D:(k + 1) * D])).astype(BF16)
        dh2 = jnp.zeros((TM, D), F32)
        for k in range(N_CHIPS):
            dh2 = dh2 + _mm(da_ref[:, k * D:(k + 1) * D], w1_v[k], NT)
        r2 = lax.rsqrt(jnp.mean(x2 * x2, axis=-1, keepdims=True) + EPS)
        xn = x2 * r2
        yv = xn * pre_ref[...]
        acc_ref[2:3, :] += _rowsum(dh2)
        acc_ref[3:4, :] += _rowsum(dh2 * yv)
        dyv = dh2 * (1.0 + mod_ref[:, 4 * D:5 * D])
        acc_ref[4:5, :] += _rowsum(dyv * xn)
        dxn = dyv * pre_ref[...]
        dx2_ref[...] = dout + r2 * (dxn - xn * jnp.mean(dxn * xn, axis=-1, keepdims=True))

    tile = BS((TM, D), lambda i: (i, 0))
    wide = BS((TM, D_FF), lambda i: (i, 0))
    vec = BS((1, D), lambda i: (0, 0))
    return pl.pallas_call(
        body, name="ffn_fwd_bwd", grid=(S // TM,),
        out_shape=(SDS((S, D_FF), BF16), SDS((S, D_FF), BF16), SDS((S, D), BF16), SDS((S, D), F32),
                   SDS((8, D), F32)),
        in_specs=[tile, tile, tile, BS((1, 6 * D), lambda i: (0, 0)), vec, vec, BS(memory_space=pl.ANY)],
        out_specs=(wide, wide, tile, tile, BS((8, D), lambda i: (0, 0))),
        scratch_shapes=[pltpu.VMEM((N_CHIPS, R_FF, D), BF16), pltpu.VMEM((N_CHIPS, R_FF, D), BF16),
                        pltpu.VMEM((TM, D_FF), F32),
                        pltpu.SemaphoreType.DMA((2,))],
        compiler_params=_params("arbitrary"),
    )(h2, x2, target, mod, post_cm, pre_cm, wg)


def _merge_bwd_call(dx2, y, ya, yb, p, mod, post_tm, wg, g):
    S = y.shape[0]

    def body(dx2_ref, y_ref, ya_ref, yb_ref, ga_ref, gb_ref, mod_ref, post_ref, wa_ref, wb_ref, wo_ref, g_ref,
             dy_ref, dya_ref, dyb_ref, doa_ref, dcb_ref, dpg_ref, acc_ref, bsum_ref, hr_ref, send_sems, recv_sems):
        start, finish = _halves_exchange(g_ref, hr_ref, send_sems, recv_sems)

        @pl.when(pl.program_id(0) == 0)
        def _():
            start()
            acc_ref[...] = jnp.zeros_like(acc_ref)
            bsum_ref[...] = jnp.zeros_like(bsum_ref)

        y = y_ref[...]
        ry = lax.rsqrt(jnp.mean(y * y, axis=-1, keepdims=True) + EPS)
        yn = y * ry
        dx2 = dx2_ref[...]
        acc_ref[0:1, :] += _rowsum(dx2 * (yn * post_ref[...]))
        dn = dx2 * mod_ref[:, 2 * D:3 * D]
        acc_ref[1:2, :] += _rowsum(dn * yn)
        dyn = dn * post_ref[...]
        dy = (ry * (dyn - yn * jnp.mean(dyn * yn, axis=-1, keepdims=True))).astype(BF16)
        dy_ref[...] = dy
        dmg = _mm_rows_t(dy, wo_ref)
        sa, sb = _sig(ga_ref[...]), _sig(gb_ref[...])
        dya = (dmg * sa).astype(BF16)
        dyb = (dmg * sb).astype(BF16)
        dya_ref[...] = dya
        dyb_ref[...] = dyb
        dga = dmg * ya_ref[...].astype(F32) * (sa * (1.0 - sa))
        dgb = dmg * yb_ref[...].astype(F32) * (sb * (1.0 - sb))
        dpg_ref[:, 0:D] = dga.astype(BF16)
        dpg_ref[:, D:2 * D] = dgb.astype(BF16)
        bsum_ref[:, 0:D] += _rowsum(dga)
        bsum_ref[:, D:2 * D] += _rowsum(dgb)
        doa_ref[...] = _mm_rows_t(dya, wa_ref)
        dcb_ref[...] = _mm_rows_t(dyb, wb_ref)

        @pl.when(pl.program_id(0) == S // TM - 1)
        def _():
            finish()

    tile = BS((TM, D), lambda i: (i, 0))
    vec = BS((1, D), lambda i: (0, 0))
    return pl.pallas_call(
        body, name="merge_bwd", grid=(S // TM,),
        out_shape=(SDS((S, D), BF16), SDS((S, D), BF16), SDS((S, D), BF16), SDS((S, D), F32), SDS((S, D), F32),
                   SDS((S, 2 * D), BF16), SDS((8, D), F32), SDS((1, 2 * D), F32),
                   SDS((N_CHIPS,) + g.shape[2:], g.dtype)),
        in_specs=[tile, tile, tile, tile, BS((TM, D), lambda i: (i, 6)), BS((TM, D), lambda i: (i, 7)),
                  BS((1, 6 * D), lambda i: (0, 0)), vec, _br_spec(O_BRA), _br_spec(O_BRB), _br_spec(O_OUT),
                  BS(memory_space=pl.ANY)],
        out_specs=(tile, tile, tile, tile, tile, BS((TM, 2 * D), lambda i: (i, 0)),
                   BS((8, D), lambda i: (0, 0)), BS((1, 2 * D), lambda i: (0, 0)), BS(memory_space=pl.ANY)),
        scratch_shapes=_halves_sems(),
        compiler_params=_params("arbitrary"),
    )(dx2, y, ya, yb, p, p, mod, post_tm, wg, wg, wg, g)


def _hgrn_bwd_call(p, o, doa, st, logits, gn, part, g):
    S = p.shape[0]
    nb = S // TB
    ncb = TB // CHUNK

    def body(q_ref, f_ref, v_ref, og_ref, o_ref, doa_ref, st_ref, lg_ref, gn_ref, part_ref, g_ref,
             dp_ref, bsum_ref, dlg_ref, dgn_ref, recv_ref, hr_ref,
             dst_scr, dlb_scr, dqe_s, dqt_s, dkt_s, dkd_s, dv_s, dog_s, dble_s, send_sems, recv_sems, hs, hr):
        i = pl.program_id(0)
        start, finish = _chip_exchange(part_ref, recv_ref, send_sems, recv_sems)
        start_h, finish_h = _halves_exchange(g_ref, hr_ref, hs, hr)

        @pl.when(i == 0)
        def _():
            start_h()
            start()
            dst_scr[...] = jnp.zeros_like(dst_scr)
            dlb_scr[...] = jnp.zeros_like(dlb_scr)
            bsum_ref[...] = jnp.zeros_like(bsum_ref)
            dgn_ref[...] = jnp.zeros_like(dgn_ref)

        lb = _lower_bound(lg_ref)
        tril, triu = _tri_masks()

        def chunk(tt, carry):
            ci = ncb - 1 - tt
            rows = pl.ds(pl.multiple_of(ci * CHUNK, CHUNK), CHUNK)
            q_r, f_r = q_ref[rows, :], f_ref[rows, :]
            t = _hg_gates(q_r, f_r, lb, tril)
            v = v_ref[rows, :]
            for h in range(HEADS):
                sl = slice(h * DK, (h + 1) * DK)
                stp = st_ref[ci, :, sl]
                stb = stp.astype(BF16)
                qeb = t["qe"][:, sl].astype(BF16)
                qtb = t["qt"][:, sl].astype(BF16)
                ktb = t["kt"][:, sl].astype(BF16)
                kdb = t["kd"][:, sl].astype(BF16)
                vb = v[:, sl].astype(BF16)
                a = jnp.where(tril > 0.5, _mm(qtb, ktb, NT), 0.0)
                o_h = o_ref[rows, sl]
                rinv = lax.rsqrt(jnp.mean(o_h * o_h, axis=-1, keepdims=True) + EPS)
                oh = o_h * rinv
                og = og_ref[rows, sl]
                so = _sig(og)
                d_oa = doa_ref[rows, sl]
                don = d_oa * (og * so)
                dog_s[:, sl] = d_oa * (oh * gn_ref[:, sl]) * _dsilu(og, so)
                dgn_ref[:, sl] += _rowsum(don * oh)
                doh = don * gn_ref[:, sl]
                do = (rinv * (doh - oh * jnp.mean(doh * oh, axis=-1, keepdims=True))).astype(BF16)
                dqe_s[:, sl] = _mm(do, stb, NN)
                dstp = _mm(do, qeb, TN)
                dab = jnp.where(tril > 0.5, _mm(do, vb, NT), 0.0).astype(BF16)
                dqt_s[:, sl] = _mm(dab, ktb, NN)
                dkt_s[:, sl] = _mm(dab, qtb, TN)
                dstn = dst_scr[:, sl]
                dsb = dstn.astype(BF16)
                dkd_s[:, sl] = _mm(vb, dsb, NN)
                dv_s[:, sl] = _mm(a.astype(BF16), do, TN) + _mm(kdb, dsb, NT)
                el = t["elast"][:, sl]
                dst_scr[:, sl] = dstn * el + dstp
                dble_s[:, sl] = el * _rowsum(stp * dstn)
            dqe, dqt, dkt, dkd = dqe_s[...], dqt_s[...], dkt_s[...], dkd_s[...]
            dq = dqe * t["e"] + dqt * t["eq"]
            dk = dkt * t["ek"] + dkd * t["dd"]
            dkk = dkd * t["kd"]
            qt_r = t["qt"].astype(BF16).astype(F32)
            kt_r = t["kt"].astype(BF16).astype(F32)
            dbv = dqe * t["qe"] + dqt * qt_r - dkt * kt_r - dkk
            dg = _cumsum_mm(triu, dbv) + (_rowsum(dkk) + dble_s[...])
            df = dg / t["f"] - dk
            sf = t["sf"]
            dlb_scr[...] += _rowsum(df * (1.0 - sf))
            dqr = dq * _dsilu(q_r, t["sq"])
            dfr = df * (1.0 - lb) * (sf * (1.0 - sf))
            dvv, dog = dv_s[...], dog_s[...]
            dp_ref[rows, 0:D] = dqr.astype(BF16)
            dp_ref[rows, D:2 * D] = dfr.astype(BF16)
            dp_ref[rows, 2 * D:3 * D] = dvv.astype(BF16)
            dp_ref[rows, 3 * D:4 * D] = dog.astype(BF16)
            bsum_ref[:, 0:D] += _rowsum(dqr)
            bsum_ref[:, D:2 * D] += _rowsum(dfr)
            bsum_ref[:, 2 * D:3 * D] += _rowsum(dvv)
            bsum_ref[:, 3 * D:4 * D] += _rowsum(dog)
            return carry

        lax.fori_loop(0, ncb, chunk, 0)

        dl = dlb_scr[...] * lb * (1.0 - lb)
        dlg_ref[0:1, :] = dl
        dlg_ref[1:2, :] = -dl

        @pl.when(i == nb - 1)
        def _():
            finish_h()
            finish()

    col = lambda j: BS((TB, D), lambda i, j=j: (nb - 1 - i, j))
    rev = BS((TB, D), lambda i: (nb - 1 - i, 0))
    cd = pltpu.VMEM((CHUNK, D), F32)
    return pl.pallas_call(
        body, name="hgrn_bwd", grid=(nb,),
        out_shape=(SDS((S, 4 * D), BF16), SDS((1, 4 * D), F32), SDS((2, D), F32), SDS((1, D), F32),
                   SDS((3,) + part.shape[1:], part.dtype), SDS((N_CHIPS,) + g.shape[2:], g.dtype)),
        in_specs=[col(0), col(1), col(2), col(3), rev, rev, BS((ncb, DK, D), lambda i: (nb - 1 - i, 0, 0)),
                  BS((2, D), lambda i: (0, 0)), BS((1, D), lambda i: (0, 0)), BS(memory_space=pl.ANY),
                  BS(memory_space=pl.ANY)],
        out_specs=(BS((TB, 4 * D), lambda i: (nb - 1 - i, 0)), BS((1, 4 * D), lambda i: (0, 0)),
                   BS((2, D), lambda i: (0, 0)), BS((1, D), lambda i: (0, 0)), BS(memory_space=pl.ANY),
                   BS(memory_space=pl.ANY)),
        scratch_shapes=[pltpu.VMEM((DK, D), F32), pltpu.VMEM((1, D), F32), cd, cd, cd, cd, cd, cd,
                        pltpu.VMEM((1, D), F32)] + _exchange_sems() + _halves_sems(),
        compiler_params=_params("arbitrary"),
    )(p, p, p, p, o, doa, st, logits, gn, part, g)


def _conv_bwd_call(dcb, uc, u, p, dw, ln_g, ln_b, part):
    S = uc.shape[0]
    nb = S // TM
    hb = TM // HALO

    def body(dcb_ref, uc_ref, u_ref, uh_ref, cv_ref, cg_ref, dw_ref, g_ref, b_ref, part_ref,
             dp_ref, bsum_ref, ddw_ref, acc_ref, recv_ref, uext, dext, ush, dsh, send_sems, recv_sems):
        i = pl.program_id(0)
        start, finish = _chip_exchange(part_ref, recv_ref, send_sems, recv_sems)

        @pl.when(i == 0)
        def _():
            start()
            dext[TM:EXT, :] = jnp.zeros((EXT - TM, D), F32)
            uext[HALO + TM:EXT, :] = jnp.zeros((EXT - HALO - TM, D), F32)
            bsum_ref[...] = jnp.zeros_like(bsum_ref)
            ddw_ref[...] = jnp.zeros_like(ddw_ref)
            acc_ref[...] = jnp.zeros_like(acc_ref)

        first_tile = (nb - 1 - i) == 0
        uext[0:HALO, :] = jnp.where(first_tile, 0.0, uh_ref[...])
        uext[HALO:HALO + TM, :] = u_ref[...]
        _fill_shifted(uext, ush)

        for rb in range(TM // SUB):
            rs_ = slice(rb * SUB, (rb + 1) * SUB)
            xh, rs = _layernorm_stats(uc_ref[rs_, :])
            ln = xh * g_ref[...] + b_ref[...]
            dln = dcb_ref[rs_, :] * _dsilu(ln, _sig(ln))
            acc_ref[1:2, :] += _rowsum(dln * xh)
            acc_ref[2:3, :] += _rowsum(dln)
            dxh = dln * g_ref[...]
            duc = rs * (dxh - jnp.mean(dxh, axis=-1, keepdims=True)
                        - xh * jnp.mean(dxh * xh, axis=-1, keepdims=True))
            dext[rs_, :] = duc
            acc_ref[0:1, :] += _rowsum(duc)
        _fill_shifted(dext, dsh)

        for j in range(CONV_K):
            part = jnp.zeros((SUB, D), F32)
            for rb in range(TM // SUB):
                s0 = HALO - (CONV_K - 1) + j + rb * SUB
                part = part + dext[rb * SUB:(rb + 1) * SUB, :] * _window(uext, ush, s0, SUB)
            ddw_ref[j:j + 1, :] += _rowsum(part)

        for rb in range(TM // SUB):
            rs_ = slice(rb * SUB, (rb + 1) * SUB)
            du = jnp.zeros((SUB, D), F32)
            for j in range(CONV_K):
                s0 = rb * SUB + (CONV_K - 1) - j
                du = du + dw_ref[j:j + 1, :] * _window(dext, dsh, s0, SUB)
            cg = cg_ref[rs_, :]
            sg = _sig(cg)
            dcv = du * sg
            dcg = du * cv_ref[rs_, :] * (sg * (1.0 - sg))
            dp_ref[rs_, 0:D] = dcv.astype(BF16)
            dp_ref[rs_, D:2 * D] = dcg.astype(BF16)
            bsum_ref[:, 0:D] += _rowsum(dcv)
            bsum_ref[:, D:2 * D] += _rowsum(dcg)

        dext[TM:TM + HALO, :] = dext[0:HALO, :]

        @pl.when(i == nb - 1)
        def _():
            finish()

    rev = BS((TM, D), lambda i: (nb - 1 - i, 0))
    vec = BS((1, D), lambda i: (0, 0))
    return pl.pallas_call(
        body, name="conv_bwd", grid=(nb,),
        out_shape=(SDS((S, 2 * D), BF16), SDS((1, 2 * D), F32), SDS((32, D), F32), SDS((8, D), F32),
                   SDS((3,) + part.shape[1:], part.dtype)),
        in_specs=[rev, rev, rev, BS((HALO, D), lambda i: (jnp.maximum((nb - 1 - i) * hb - 1, 0), 0)),
                  BS((TM, D), lambda i: (nb - 1 - i, 4)), BS((TM, D), lambda i: (nb - 1 - i, 5)),
                  BS((CONV_K, D), lambda i: (0, 0)), vec, vec, BS(memory_space=pl.ANY)],
        out_specs=(BS((TM, 2 * D), lambda i: (nb - 1 - i, 0)), BS((1, 2 * D), lambda i: (0, 0)),
                   BS((32, D), lambda i: (0, 0)), BS((8, D), lambda i: (0, 0)), BS(memory_space=pl.ANY)),
        scratch_shapes=[pltpu.VMEM((EXT, D), F32), pltpu.VMEM((EXT, D), F32),
                        pltpu.VMEM((7, HALO + TM, D), F32), pltpu.VMEM((7, HALO + TM, D), F32)] + _exchange_sems(),
        compiler_params=_params("arbitrary"),
    )(dcb, uc, u, u, p, p, dw, ln_g, ln_b, part)


def _in_bwd_call(dp_hg, dp_cv, dp_gt, x, dx2, mod, pre_tm, wg, part, full_a, full_b):
    S = x.shape[0]
    tm = TM

    def body(hg_ref, cv_ref, gt_ref, x_ref, dx2_ref, mod_ref, g_ref, w_hbm, part_ref, fa_in, fb_in,
             gx_ref, acc_ref, recv_ref, fa_out, fb_out, w_vmem, sem, send_sems, recv_sems, sa, ra, sb, rb):
        start, finish = _chip_exchange(part_ref, recv_ref, send_sems, recv_sems)
        start_a, finish_a = _join_exchange(fa_in, fa_out, sa, ra)
        start_b, finish_b = _join_exchange(fb_in, fb_out, sb, rb)

        @pl.when(pl.program_id(0) == 0)
        def _():
            start_a()
            start_b()
            start()
            _load_rows(w_hbm, w_vmem, sem, O_IN).wait()
            acc_ref[...] = jnp.zeros_like(acc_ref)

        dh = jnp.zeros((tm, D), F32)
        for k in range(IN_COLS // D):
            src, kk = ((hg_ref, k), (cv_ref, k - 4), (gt_ref, k - 6))[0 if k < 4 else (1 if k < 6 else 2)]
            dh = dh + _mm(src[:, kk * D:(kk + 1) * D], w_vmem[k // 2, (k % 2) * D:(k % 2 + 1) * D, :], NT)
        xv = x_ref[...]
        r = lax.rsqrt(jnp.mean(xv * xv, axis=-1, keepdims=True) + EPS)
        xn = xv * r
        yv = xn * g_ref[...]
        acc_ref[0:1, :] += _rowsum(dh)
        acc_ref[1:2, :] += _rowsum(dh * yv)
        dyv = dh * (1.0 + mod_ref[:, D:2 * D])
        acc_ref[2:3, :] += _rowsum(dyv * xn)
        dxn = dyv * g_ref[...]
        gx_ref[...] = dx2_ref[...] + r * (dxn - xn * jnp.mean(dxn * xn, axis=-1, keepdims=True))

        @pl.when(pl.program_id(0) == S // tm - 1)
        def _():
            finish_a()
            finish_b()
            finish()

    tile = BS((tm, D), lambda i: (i, 0))
    hbm = BS(memory_space=pl.ANY)
    return pl.pallas_call(
        body, name="in_bwd", grid=(S // tm,),
        out_shape=(SDS((S, D), F32), SDS((8, D), F32), SDS((3,) + part.shape[1:], part.dtype),
                   SDS(full_a.shape, full_a.dtype), SDS(full_b.shape, full_b.dtype)),
        in_specs=[BS((tm, 4 * D), lambda i: (i, 0)), BS((tm, 2 * D), lambda i: (i, 0)),
                  BS((tm, 2 * D), lambda i: (i, 0)), tile, tile, BS((1, 6 * D), lambda i: (0, 0)),
                  BS((1, D), lambda i: (0, 0)), hbm, hbm, hbm, hbm],
        out_specs=(tile, BS((8, D), lambda i: (0, 0)), hbm, hbm, hbm),
        scratch_shapes=[pltpu.VMEM((N_CHIPS, R_IN, D), BF16), pltpu.SemaphoreType.DMA] + _exchange_sems()
        + _join_sems() + _join_sems(),
        input_output_aliases={9: 3, 10: 4},
        compiler_params=_params("arbitrary"),
    )(dp_hg, dp_cv, dp_gt, x, dx2, mod, pre_tm, wg, part, full_a, full_b)


def _wgrad_call(gp, a, b, name, bm, place, rows):
    S, M = a.shape
    N = b.shape[1]
    bk = min(S, 1024)
    nk = S // bk

    def body(a_ref, b_ref, *rest):
        o_ref, acc = rest[-2], rest[-1]
        k = pl.program_id(2)

        @pl.when(k == 0)
        def _():
            acc[...] = jnp.zeros_like(acc)

        acc[...] += _mm(a_ref[...], b_ref[...], TN)

        @pl.when(k == nk - 1)
        def _():
            o_ref[...] = acc[...].astype(BF16)

    in_specs = [BS((bk, bm), lambda i, j, k: (k, i)), BS((bk, D), lambda i, j, k: (k, j))]
    args = [a, b]
    if gp is not None:
        in_specs.append(BS(memory_space=pl.ANY))
        args.append(gp)
    return pl.pallas_call(
        body, name=name, grid=(M // bm, N // D, nk),
        out_shape=SDS((N_CHIPS, rows, D), BF16),
        in_specs=in_specs,
        out_specs=BS((None, bm, D), lambda i, j, k: (*place(i, j), 0)),
        scratch_shapes=[pltpu.VMEM((bm, D), F32)],
        input_output_aliases={} if gp is None else {2: 0},
        compiler_params=_params("parallel", "parallel", "arbitrary"),
    )(*args)


def _wgrad_rows_call(gp, a, b, name, blk):
    S = a.shape[0]
    bk = min(S, 1024)
    nk = S // bk

    def body(a_ref, b_ref, *rest):
        o_ref, acc = rest[-2], rest[-1]
        k = pl.program_id(0)

        @pl.when(k == 0)
        def _():
            acc[...] = jnp.zeros_like(acc)

        acc[...] += _mm(a_ref[...], b_ref[...], TN)

        @pl.when(k == nk - 1)
        def _():
            for c in range(N_CHIPS):
                o_ref[c] = acc[c * R_BR:(c + 1) * R_BR, :].astype(BF16)

    in_specs = [BS((bk, D), lambda k: (k, 0)), BS((bk, D), lambda k: (k, 0))]
    args = [a, b]
    if gp is not None:
        in_specs.append(BS(memory_space=pl.ANY))
        args.append(gp)
    return pl.pallas_call(
        body, name=name, grid=(nk,),
        out_shape=SDS((N_CHIPS, 3 * R_BR, D), BF16),
        in_specs=in_specs,
        out_specs=BS((N_CHIPS, R_BR, D), lambda k: (0, blk, 0)),
        scratch_shapes=[pltpu.VMEM((D, D), F32)],
        input_output_aliases={} if gp is None else {2: 0},
        compiler_params=_params("arbitrary"),
    )(*args)


def _outer_call(cact, dmod):
    n = dmod.shape[1]

    def body(a_ref, b_ref, o_ref):
        o_ref[...] = _mm(a_ref[...], b_ref[...], TN, HI)

    return pl.pallas_call(
        body, name="wgrad_ada", out_shape=SDS((D, n), F32),
        compiler_params=pltpu.CompilerParams(vmem_limit_bytes=VMEM_LIMIT),
    )(cact, dmod)


def _adamw_call(w, g, m, v, name):
    R, C = w.shape
    tr = R
    while tr * C > 512 * 1024 and tr % 16 == 0:
        tr //= 2
    c1 = 1.0 - ADAM_B1 ** ADAM_STEP
    c2 = 1.0 - ADAM_B2 ** ADAM_STEP

    def body(w_ref, g_ref, m_ref, v_ref, d_ref, m2_ref, v2_ref):
        g = g_ref[...]
        m2 = ADAM_B1 * m_ref[...] + (1.0 - ADAM_B1) * g
        v2 = ADAM_B2 * v_ref[...] + (1.0 - ADAM_B2) * (g * g)
        m2_ref[...] = m2
        v2_ref[...] = v2
        d_ref[...] = -ADAM_LR * ((m2 / c1) / (jnp.sqrt(v2 / c2) + ADAM_EPS) + ADAM_WD * w_ref[...])

    tile = BS((tr, C), lambda i: (i, 0))
    return pl.pallas_call(
        body, name=name, grid=(R // tr,), out_shape=(SDS((R, C), F32),) * 3,
        in_specs=[tile] * 4, out_specs=(tile,) * 3, compiler_params=_params("parallel"),
    )(w, g, m, v)


def _adamw_rows_call(ws, g, ms, vs, name):
    k = len(ws)
    r = ws[0].shape[0]
    c1 = 1.0 - ADAM_B1 ** ADAM_STEP
    c2 = 1.0 - ADAM_B2 ** ADAM_STEP

    def body(g_ref, *refs):
        ins, outs = refs[:3 * k], refs[3 * k:]
        for j in range(k):
            w_ref, m_ref, v_ref = ins[j], ins[k + j], ins[2 * k + j]
            d_ref, m2_ref, v2_ref = outs[j], outs[k + j], outs[2 * k + j]
            g = g_ref[j * r:(j + 1) * r, :]
            m2 = ADAM_B1 * m_ref[...] + (1.0 - ADAM_B1) * g
            v2 = ADAM_B2 * v_ref[...] + (1.0 - ADAM_B2) * (g * g)
            m2_ref[...] = m2
            v2_ref[...] = v2
            d_ref[...] = -ADAM_LR * ((m2 / c1) / (jnp.sqrt(v2 / c2) + ADAM_EPS) + ADAM_WD * w_ref[...])

    out = pl.pallas_call(
        body, name=name, out_shape=(SDS(ws[0].shape, F32),) * (3 * k),
        compiler_params=pltpu.CompilerParams(vmem_limit_bytes=VMEM_LIMIT),
    )(g, *ws, *ms, *vs)
    return out[:k], out[k:2 * k], out[2 * k:]


def _adamw_small_call(ws, ms, vs, row0, ssum, g_dw):
    n = len(ws)
    c1 = 1.0 - ADAM_B1 ** ADAM_STEP
    c2 = 1.0 - ADAM_B2 ** ADAM_STEP

    def adam(w, g, m, v):
        m2 = ADAM_B1 * m + (1.0 - ADAM_B1) * g
        v2 = ADAM_B2 * v + (1.0 - ADAM_B2) * (g * g)
        return -ADAM_LR * ((m2 / c1) / (jnp.sqrt(v2 / c2) + ADAM_EPS) + ADAM_WD * w), m2, v2

    def body(s_ref, gdw_ref, *refs):
        ins, outs = refs[:3 * n], refs[3 * n:]
        for j in range(n):
            w_ref, m_ref, v_ref = ins[j], ins[n + j], ins[2 * n + j]
            g_ref, d_ref, m2_ref, v2_ref = outs[j], outs[n + j], outs[2 * n + j], outs[3 * n + j]
            if j == n - 1:
                pieces = [(slice(None), slice(None), gdw_ref[...])]
            elif w_ref.shape[0] == 1:
                pieces = [(slice(None), slice(i * D, (i + 1) * D), s_ref[row0[j] + i:row0[j] + i + 1, :])
                          for i in range(w_ref.shape[1] // D)]
            else:
                pieces = [(slice(None), slice(None), s_ref[row0[j]:row0[j] + w_ref.shape[0], :])]
            for rs, cs, g in pieces:
                d, m2, v2 = adam(w_ref[rs, cs], g, m_ref[rs, cs], v_ref[rs, cs])
                g_ref[rs, cs] = g
                d_ref[rs, cs] = d
                m2_ref[rs, cs] = m2
                v2_ref[rs, cs] = v2

    out = pl.pallas_call(
        body, name="adamw_small", out_shape=tuple(SDS(w.shape, F32) for w in ws) * 4,
        compiler_params=pltpu.CompilerParams(vmem_limit_bytes=VMEM_LIMIT),
    )(ssum, g_dw, *ws, *ms, *vs)
    return [(out[j], out[n + j], out[2 * n + j], out[3 * n + j]) for j in range(n)]


def _adamw_gather_call(w, g, m, v, srows, name):
    R, C = w.shape
    tr = R
    while tr * C > 512 * 1024 and tr % 16 == 0:
        tr //= 2
    nsteps = R // tr
    mr = srows.shape[0]
    c1 = 1.0 - ADAM_B1 ** ADAM_STEP
    c2 = 1.0 - ADAM_B2 ** ADAM_STEP

    def body(w_ref, g_ref, m_ref, v_ref, s_ref, d_ref, m2_ref, v2_ref, all_ref, sum_ref,
             x_scr, out_scr, send_sems, recv_sems, local_sem):
        i = pl.program_id(0)
        start, finish = _allgather_parts(x_scr, out_scr, send_sems, recv_sems, local_sem)

        @pl.when(i == 0)
        def _():
            x_scr[...] = s_ref[...]
            start()

        g = g_ref[...]
        m2 = ADAM_B1 * m_ref[...] + (1.0 - ADAM_B1) * g
        v2 = ADAM_B2 * v_ref[...] + (1.0 - ADAM_B2) * (g * g)
        m2_ref[...] = m2
        v2_ref[...] = v2
        d_ref[...] = -ADAM_LR * ((m2 / c1) / (jnp.sqrt(v2 / c2) + ADAM_EPS) + ADAM_WD * w_ref[...])

        @pl.when(i == nsteps - 1)
        def _():
            finish()
            all_ref[...] = out_scr[...]
            acc = out_scr[0:mr, :]
            for d in range(1, N_DEV):
                acc = acc + out_scr[d * mr:(d + 1) * mr, :]
            sum_ref[...] = acc

    tile = BS((tr, C), lambda i: (i, 0))
    return pl.pallas_call(
        body, name=name, grid=(nsteps,),
        out_shape=(SDS((R, C), F32),) * 3 + (SDS((N_DEV * mr, D), F32), SDS((mr, D), F32)),
        in_specs=[tile] * 4 + [BS((mr, D), lambda i: (0, 0))],
        out_specs=(tile,) * 3 + (BS((N_DEV * mr, D), lambda i: (0, 0)), BS((mr, D), lambda i: (0, 0))),
        scratch_shapes=[pltpu.VMEM((mr, D), F32), pltpu.VMEM((N_DEV * mr, D), F32)] + _allgather_sems(),
        compiler_params=_params("arbitrary"),
    )(w, g, m, v, srows)


def _rs_begin(g, c_idx, tag):
    n = g.shape[1]
    g = g.reshape(N_CHIPS, 2, n // 2, D)
    return _add_halves_call(g, _sibling_halves_call(g, tag), c_idx, tag)


def _rs_end(part, recv, c_idx, chip_idx, tag):
    n = 2 * part.shape[1]
    full = _add_chips_call(part, recv, jnp.concatenate([chip_idx, c_idx]), tag)
    return _sibling_join_call(full, tag).reshape(n, D)


def _local_step(x, mod, cact, target, wg, pack, small, c_idx, chip_idx):
    p, h1, wg = _fwd_in_call(x, mod, small["pre_tm"], wg, small["b_in"], pack, small["order"])
    o, oa, st, wg = _hgrn_fwd_call(p, small["logits"], small["hg_norm"], wg, pack)
    u, uc, cb, wg = _conv_fwd_call(p, small["conv_dw"], small["conv_db"], small["ln_g"], small["ln_b"], wg, pack)
    ya, yb, mg, y, x2, h2 = _merge_fwd_call(oa, cb, p, x, mod, small["post_tm"], small["pre_cm"], wg)
    z, da, dy2, dx2, acc_f = _ffn_call(h2, x2, target, mod, small["post_cm"], small["pre_cm"], wg)

    g_ff = _wgrad_call(None, h2, da, "wgrad_ff1", D, lambda i, j: (j, 0), 2 * R_FF)
    g_ff = _wgrad_call(g_ff, z, dy2, "wgrad_ff2", D, lambda i, j: (i, 1), 2 * R_FF)
    g_ff = g_ff.reshape(N_CHIPS, 2, R_FF, D)
    dy, dya, dyb, doa, dcb, dp_gt, acc_m, bs_gt, hr_ff = _merge_bwd_call(dx2, y, ya, yb, p, mod, small["post_tm"],
                                                                        wg, g_ff)
    part_ff = _add_halves_call(g_ff, hr_ff, c_idx, "ff")

    g_br = _wgrad_rows_call(None, oa, dya, "wgrad_br_a", 0)
    g_br = _wgrad_rows_call(g_br, cb, dyb, "wgrad_br_b", 1)
    g_br = _wgrad_rows_call(g_br, mg, dy, "wgrad_out", 2)
    g_br = g_br.reshape(N_CHIPS, 2, 3 * R_BR // 2, D)
    dp_hg, bs_hg, dlg, dgn, recv_ff, hr_br = _hgrn_bwd_call(p, o, doa, st, small["logits"], small["hg_norm"],
                                                            part_ff, g_br)
    part_br = _add_halves_call(g_br, hr_br, c_idx, "br")
    dp_cv, bs_cv, ddw, acc_c, recv_br = _conv_bwd_call(dcb, uc, u, p, small["conv_dw"], small["ln_g"], small["ln_b"],
                                                        part_br)

    g_in = _wgrad_call(None, h1, dp_hg, "wgrad_in_hg", D, lambda i, j: (j // 2, j % 2), R_IN)
    g_in = _wgrad_call(g_in, h1, dp_cv, "wgrad_in_cv", D, lambda i, j: (2, j), R_IN)
    g_in = _wgrad_call(g_in, h1, dp_gt, "wgrad_in_gt", D, lambda i, j: (3, j), R_IN)
    part_in = _rs_begin(g_in, c_idx, "in")
    chip_c = jnp.concatenate([chip_idx, c_idx])
    full_ff = _add_chips_call(part_ff, recv_ff, chip_c, "ff")
    full_br = _add_chips_call(part_br, recv_br, chip_c, "br")
    gx, acc_i, recv_in, full_ff, full_br = _in_bwd_call(dp_hg, dp_cv, dp_gt, x, dx2, mod, small["pre_tm"], wg,
                                                        part_in, full_ff, full_br)
    red_ff = full_ff.reshape(2 * R_FF, D)
    red_br = full_br.reshape(3 * R_BR, D)
    red_in = _rs_end(part_in, recv_in, c_idx, chip_idx, "in")

    zrow = jnp.zeros((1, D), F32)
    rows = [acc_i[0:1], acc_i[1:2], acc_m[0:1], acc_f[2:3], acc_f[3:4], acc_f[0:1],
            acc_i[2:3], acc_m[1:2], acc_f[4:5], acc_f[1:2],
            jnp.concatenate([bs_hg, bs_cv, bs_gt], axis=1).reshape(8, D),
            dlg, dgn, acc_c[0:1], acc_c[1:2], acc_c[2:3],
            ddw,
            cact, acc_f[5:6]] + [zrow] * 6
    return gx, jnp.concatenate(rows, axis=0), red_in, red_ff, red_br


def kernel(x, c, w_ada, b_ada, pre_norm_tm, post_norm_tm, pre_norm_cm, post_norm_cm, w_in, b_in, hg_lb_logits, hg_norm, conv_dw, conv_db, conv_ln_g, conv_ln_b, w_br_a, w_br_b, w_out, w_ff1, w_ff2, loss_target, m_w_ada, m_b_ada, m_pre_norm_tm, m_post_norm_tm, m_pre_norm_cm, m_post_norm_cm, m_w_in, m_b_in, m_hg_lb_logits, m_hg_norm, m_conv_dw, m_conv_db, m_conv_ln_g, m_conv_ln_b, m_w_br_a, m_w_br_b, m_w_out, m_w_ff1, m_w_ff2, v_w_ada, v_b_ada, v_pre_norm_tm, v_post_norm_tm, v_pre_norm_cm, v_post_norm_cm, v_w_in, v_b_in, v_hg_lb_logits, v_hg_norm, v_conv_dw, v_conv_db, v_conv_ln_g, v_conv_ln_b, v_w_br_a, v_w_br_b, v_w_out, v_w_ff1, v_w_ff2):
    xi, yi, ci = lax.axis_index("x"), lax.axis_index("y"), lax.axis_index("c")
    chip = 2 * xi + yi
    c_idx = jnp.reshape(ci, (1,)).astype(jnp.int32)
    chip_idx = jnp.reshape(chip, (1,)).astype(jnp.int32)

    w_in_halves = w_in[0].reshape(D, 2, D).transpose(1, 0, 2).reshape(R_IN, D)
    pack = jnp.concatenate([w_in_halves, w_ff1[0], w_ff2[0], w_br_a[0], w_br_b[0], w_out[0]],
                           axis=0).astype(BF16)
    wg = lax.dynamic_update_slice(lax.empty((N_CHIPS, PACK_W, D), BF16), pack[None], (chip, 0, 0))
    wa = 6 * D // N_CHIPS
    me = 4 * xi + 2 * yi + ci
    dw_blk = jnp.concatenate([conv_dw[0].reshape(-1), jnp.zeros((8 * D - CONV_K * 256,), F32)]).reshape(8, D)
    dw_all, ca_all, mod_all = _prologue_call(
        dw_blk, jnp.broadcast_to(c, (8, D)), w_ada[0].astype(BF16),
        lax.dynamic_slice_in_dim(b_ada, chip * wa, wa, axis=1))
    order = jnp.stack([chip, 2 * (1 - xi) + yi, 2 * xi + (1 - yi), 2 * (1 - xi) + (1 - yi)]).astype(jnp.int32)
    dw_all = dw_all.reshape(N_CHIPS, 2, 8 * D)[:, 0, :CONV_K * 256].reshape(N_CHIPS, CONV_K, 256)
    dw_full = dw_all.transpose(1, 0, 2).reshape(CONV_K, D)
    cact = lax.dynamic_slice_in_dim(ca_all, me * 8, 1, axis=0)
    mod_mine = lax.dynamic_index_in_dim(mod_all.reshape(N_CHIPS, 2, N_DEV, wa)[:, 0], me, axis=1,
                                        keepdims=False)
    mod = mod_mine.reshape(1, 6 * D)

    small = dict(pre_tm=pre_norm_tm, post_tm=post_norm_tm, pre_cm=pre_norm_cm, post_cm=post_norm_cm,
                 b_in=b_in, logits=hg_lb_logits, hg_norm=hg_norm, conv_dw=dw_full, conv_db=conv_db,
                 ln_g=conv_ln_g, ln_b=conv_ln_b, order=order)

    gx, srows, red_in, red_ff, red_br = _local_step(x[0], mod, cact, loss_target[0], wg, pack, small, c_idx,
                                                    chip_idx)

    shapes = {"in": w_in.shape, "br_a": w_br_a.shape, "br_b": w_br_b.shape, "out": w_out.shape,
              "ff1": w_ff1.shape, "ff2": w_ff2.shape}
    offs = {"in": (red_in, 0, R_IN), "ff1": (red_ff, 0, R_FF), "ff2": (red_ff, R_FF, 2 * R_FF),
            "br_a": (red_br, 0, R_BR), "br_b": (red_br, R_BR, 2 * R_BR), "out": (red_br, 2 * R_BR, 3 * R_BR)}
    wmv = {"in": (w_in, m_w_in, v_w_in), "br_a": (w_br_a, m_w_br_a, v_w_br_a), "br_b": (w_br_b, m_w_br_b, v_w_br_b),
           "out": (w_out, m_w_out, v_w_out), "ff1": (w_ff1, m_w_ff1, v_w_ff1), "ff2": (w_ff2, m_w_ff2, v_w_ff2)}
    res = {}
    for n in ("in", "ff1", "ff2"):
        shp = shapes[n]
        g2d = offs[n][0][offs[n][1]:offs[n][2]]
        if n == "in":
            g2d = g2d.reshape(2, D, D).transpose(1, 0, 2)
        g2d = g2d.reshape(shp[1], shp[2])
        w_, m_, v_ = (a[0] for a in wmv[n])
        if n == "in":
            d_, m2_, v2_, sall, ssum = _adamw_gather_call(w_, g2d, m_, v_, srows, "adamw_in")
        else:
            d_, m2_, v2_ = _adamw_call(w_, g2d, m_, v_, "adamw_" + n)
        res[n] = tuple(a.reshape(shp) for a in (g2d, d_, m2_, v2_))
    trio = ("br_a", "br_b", "out")
    d3, m3, v3 = _adamw_rows_call([wmv[n][0][0] for n in trio], red_br, [wmv[n][1][0] for n in trio],
                                  [wmv[n][2][0] for n in trio], "adamw_br")
    for j, n in enumerate(trio):
        res[n] = tuple(a.reshape(shapes[n]) for a in (red_br[j * R_BR:(j + 1) * R_BR], d3[j], m3[j], v3[j]))

    sall = sall.reshape(N_DEV, SMALL_ROWS, D)
    loss = jnp.sum(ssum[57])
    dmod_all = sall[:, 0:6, :].reshape(N_DEV, 6 * D)
    g_ada = _outer_call(sall[:, 56, :], lax.dynamic_slice_in_dim(dmod_all, chip * wa, wa, axis=1))
    g_dw = lax.dynamic_slice_in_dim(ssum[24:24 + CONV_K], chip * 256, 256, axis=1)
    d_, m2_, v2_ = _adamw_call(w_ada[0], g_ada, m_w_ada[0], v_w_ada[0], "adamw_ada")
    res["ada"] = tuple(a.reshape(w_ada.shape) for a in (g_ada, d_, m2_, v2_))

    names = ["b_ada", "pre_tm", "post_tm", "pre_cm", "post_cm", "b_in", "logits", "hg_norm", "conv_db", "ln_g", "ln_b",
             "conv_dw"]
    row0 = [0, 6, 7, 8, 9, 10, 18, 20, 21, 22, 23, None]
    sres = _adamw_small_call(
        [b_ada, pre_norm_tm, post_norm_tm, pre_norm_cm, post_norm_cm, b_in, hg_lb_logits, hg_norm, conv_db,
         conv_ln_g, conv_ln_b, conv_dw[0]],
        [m_b_ada, m_pre_norm_tm, m_post_norm_tm, m_pre_norm_cm, m_post_norm_cm, m_b_in, m_hg_lb_logits, m_hg_norm,
         m_conv_db, m_conv_ln_g, m_conv_ln_b, m_conv_dw[0]],
        [v_b_ada, v_pre_norm_tm, v_post_norm_tm, v_pre_norm_cm, v_post_norm_cm, v_b_in, v_hg_lb_logits, v_hg_norm,
         v_conv_db, v_conv_ln_g, v_conv_ln_b, v_conv_dw[0]], row0, ssum, g_dw)
    for nm, r4 in zip(names, sres):
        res[nm] = tuple(a.reshape(conv_dw.shape) for a in r4) if nm == "conv_dw" else r4

    order = ["ada", "b_ada", "pre_tm", "post_tm", "pre_cm", "post_cm", "in", "b_in", "logits", "hg_norm", "conv_dw",
             "conv_db", "ln_g", "ln_b", "br_a", "br_b", "out", "ff1", "ff2"]
    outs = [loss, gx.reshape(x.shape)]
    for kind in range(4):
        outs.extend(res[n][kind] for n in order)
    return tuple(outs)
```

```python
import jax
import jax.numpy as jnp
from jax import lax
from jax.experimental import pallas as pl
from jax.experimental.pallas import tpu as pltpu

F32, BF16 = jnp.float32, jnp.bfloat16
SDS = jax.ShapeDtypeStruct
BS = pl.BlockSpec
MESH = pl.DeviceIdType.MESH
HI = lax.Precision.HIGHEST

D = 1024
D_FF = 4096
IN_COLS = 8192
HEADS, DK = 8, 128
CHUNK = 128
CONV_K = 31
HALO = 32
SUB = 32
EPS = 1e-6
N_CHIPS, N_DEV = 4, 8
TM = 256
TB = 256
VMEM_LIMIT = 56 * 1024 * 1024

R_IN, R_BR, R_FF = 2048, 256, 1024
PACK_W = R_IN + 3 * R_BR + 2 * R_FF
O_IN, O_FF1, O_FF2, O_BRA, O_BRB, O_OUT = 0, 2048, 3072, 4096, 4352, 4608
SMALL_ROWS = 64

ADAM_LR, ADAM_B1, ADAM_B2, ADAM_EPS, ADAM_WD, ADAM_STEP = 0.001, 0.9, 0.999, 1e-08, 0.01, 10

NN = (((1,), (0,)), ((), ()))
NT = (((1,), (1,)), ((), ()))
TN = (((0,), (0,)), ((), ()))


def _mm(a, b, dims=NN, precision=None):
    return lax.dot_general(a, b, dims, preferred_element_type=F32, precision=precision)


def _sig(v):
    return jax.nn.sigmoid(v)


def _dsilu(v, s):
    return s * (1.0 + v * (1.0 - s))


def _params(*sem):
    return pltpu.CompilerParams(dimension_semantics=sem if sem else None, vmem_limit_bytes=VMEM_LIMIT)


def _rowsum(v):
    return jnp.sum(v, axis=0, keepdims=True)


def _mesh_pos():
    return lax.axis_index("x"), lax.axis_index("y"), lax.axis_index("c")


def _allgather_parts(x_ref, out_ref, send_sems, recv_sems, local_sem):
    m_per = x_ref.shape[0]
    x, y, c = _mesh_pos()
    me, sibling = (x, y, c), (x, y, 1 - c)
    chips = [(1 - x, y), (x, 1 - y), (1 - x, 1 - y)]

    def rows(px, py, pc):
        return out_ref.at[pl.ds((4 * px + 2 * py + pc) * m_per, m_per), :]

    def copy(k, block, to, src=None):
        return pltpu.make_async_remote_copy(
            src_ref=rows(*block) if src is None else src, dst_ref=rows(*block),
            send_sem=send_sems.at[k], recv_sem=recv_sems.at[k], device_id=to, device_id_type=MESH)

    def first():
        return [copy(0, me, sibling, src=x_ref)] + [copy(1 + j, me, (*chip, c), src=x_ref)
                                                    for j, chip in enumerate(chips)]

    def start():
        pltpu.make_async_copy(x_ref, rows(*me), local_sem).start()
        for cp in first():
            cp.start()

    def finish():
        passed = [copy(4 + j, (*chip, c), sibling) for j, chip in enumerate(chips)]
        for j, chip in enumerate(chips):
            copy(1 + j, (*chip, c), me).wait_recv()
            passed[j].start()
        copy(0, sibling, me).wait_recv()
        for j, chip in enumerate(chips):
            copy(4 + j, (*chip, 1 - c), me).wait_recv()
        for cp in first() + passed:
            cp.wait_send()
        pltpu.make_async_copy(x_ref, rows(*me), local_sem).wait()

    return start, finish


def _allgather_sems():
    return [pltpu.SemaphoreType.DMA((7,)), pltpu.SemaphoreType.DMA((7,)), pltpu.SemaphoreType.DMA]
def _gather_sems(n_ranges):
    return [pltpu.SemaphoreType.DMA((6 * n_ranges,)), pltpu.SemaphoreType.DMA((6 * n_ranges,))]


def _pack_gather(pack_ref, wg_ref, send_sems, recv_sems, ranges):
    x, y, c = _mesh_pos()
    me, sibling = (x, y, c), (x, y, 1 - c)
    chips = [(1 - x, y), (x, 1 - y), (1 - x, 1 - y)]

    def land(r, px, py, pc):
        off, n = ranges[r]
        return wg_ref.at[2 * px + py, pl.ds(off + pc * (n // 2), n // 2), :]

    def mine(r):
        off, n = ranges[r]
        return pack_ref.at[pl.ds(off + c * (n // 2), n // 2), :]

    def copy(r, k, block, to, src=None):
        return pltpu.make_async_remote_copy(
            src_ref=land(r, *block) if src is None else src, dst_ref=land(r, *block),
            send_sem=send_sems.at[6 * r + k], recv_sem=recv_sems.at[6 * r + k], device_id=to, device_id_type=MESH)

    def start():
        for r in range(len(ranges)):
            for j, chip in enumerate(chips):
                copy(r, j, me, (*chip, c), src=mine(r)).start()

    def finish():
        for r in range(len(ranges)):
            for j, chip in enumerate(chips):
                copy(r, j, (*chip, c), me).wait_recv()
                copy(r, 3 + j, (*chip, c), sibling).start()
        for r in range(len(ranges)):
            for j, chip in enumerate(chips):
                copy(r, 3 + j, (*chip, 1 - c), me).wait_recv()
        for r in range(len(ranges)):
            for j, chip in enumerate(chips):
                copy(r, j, me, (*chip, c), src=mine(r)).wait_send()
                copy(r, 3 + j, (*chip, c), sibling).wait_send()

    return start, finish


def _relay_sems():
    return [pltpu.SemaphoreType.DMA((8,)), pltpu.SemaphoreType.DMA((8,))]


def _relay_gather(pack_ref, wg_ref, send_sems, recv_sems, off, n):
    x, y, c = _mesh_pos()
    me, sibling = (x, y, c), (x, y, 1 - c)
    chips = [(1 - x, y), (x, 1 - y), (1 - x, 1 - y)]
    h, q = n // 2, n // 4

    def land(px, py, pc, piece=None):
        if piece is None:
            return wg_ref.at[2 * px + py, pl.ds(off + pc * h, h), :]
        return wg_ref.at[2 * px + py, pl.ds(off + pc * h + piece * q, q), :]

    def copy(k, ref, to, src=None):
        return pltpu.make_async_remote_copy(
            src_ref=ref if src is None else src, dst_ref=ref, send_sem=send_sems.at[k], recv_sem=recv_sems.at[k],
            device_id=to, device_id_type=MESH)

    def direct(j):
        return copy(j, land(x, y, c), (*chips[j], c), src=pack_ref.at[pl.ds(off + c * h, h), :])

    def relayed(j):
        if j == 0:
            return copy(6, land(*chips[0], c, 1), (x, 1 - y, c))
        return copy(7, land(*chips[1], c, 0), (1 - x, y, c))

    def start():
        direct(0).start()
        direct(1).start()

    def arrive(j):
        if j == 0:
            for k in range(2):
                copy(k, land(*chips[k], c), me).wait_recv()
                relayed(k).start()
                copy(3 + k, land(*chips[k], c), sibling).start()
        if j == 2:
            copy(7, land(*chips[2], c, 0), me).wait_recv()
            copy(6, land(*chips[2], c, 1), me).wait_recv()
            copy(5, land(*chips[2], c), sibling).start()
        copy(3 + j, land(*chips[j], 1 - c), me).wait_recv()

    def drain():
        for j in range(2):
            direct(j).wait_send()
            relayed(j).wait_send()
        for j in range(3):
            copy(3 + j, land(*chips[j], c), sibling).wait_send()

    return start, arrive, drain


def _prologue_call(dw_blk, c_blk, w_ada, b_ada):
    wa = w_ada.shape[1]

    def body(dw_ref, c_ref, wa_ref, ba_ref, dwg_ref, ca_ref, modg_ref,
             cg_scr, part_scr, s1, r1, l1, s2, r2, l2, s3, r3, l3):
        start_c, finish_c = _allgather_parts(c_ref, cg_scr, s2, r2, l2)
        start_dw, finish_dw = _allgather_parts(dw_ref, dwg_ref, s1, r1, l1)
        start_mod, finish_mod = _allgather_parts(part_scr, modg_ref, s3, r3, l3)
        start_c()
        start_dw()
        finish_c()
        cv = cg_scr[...]
        ca = cv * _sig(cv)
        ca_ref[...] = ca
        pick = (lax.broadcasted_iota(jnp.int32, (N_DEV, N_DEV * 8), 1)
                == 8 * lax.broadcasted_iota(jnp.int32, (N_DEV, N_DEV * 8), 0)).astype(BF16)
        ca8 = _mm(pick, ca.astype(BF16)).astype(BF16)
        part_scr[...] = _mm(ca8, wa_ref[...]) + ba_ref[...]
        start_mod()
        finish_dw()
        finish_mod()

    vm = BS(memory_space=pltpu.VMEM)
    return pl.pallas_call(
        body, name="prologue_adaln_conv_dw",
        out_shape=(SDS((N_DEV * 8, D), F32), SDS((N_DEV * 8, D), F32), SDS((N_DEV * N_DEV, wa), F32)),
        in_specs=[vm, vm, vm, vm], out_specs=(vm, vm, vm),
        scratch_shapes=[pltpu.VMEM((N_DEV * 8, D), F32), pltpu.VMEM((N_DEV, wa), F32)]
        + _allgather_sems() + _allgather_sems() + _allgather_sems(),
        compiler_params=pltpu.CompilerParams(vmem_limit_bytes=VMEM_LIMIT),
    )(dw_blk, c_blk, w_ada, b_ada)


def _halves_exchange(g_ref, out_ref, send_sems, recv_sems):
    x, y, c = _mesh_pos()

    def copies():
        return [pltpu.make_async_remote_copy(
            src_ref=g_ref.at[k, 1 - c], dst_ref=out_ref.at[k], send_sem=send_sems.at[k], recv_sem=recv_sems.at[k],
            device_id=(x, y, 1 - c), device_id_type=MESH) for k in range(N_CHIPS)]

    def start():
        for cp in copies():
            cp.start()

    def finish():
        for cp in copies():
            cp.wait()

    return start, finish


def _halves_sems():
    return [pltpu.SemaphoreType.DMA((N_CHIPS,)), pltpu.SemaphoreType.DMA((N_CHIPS,))]


def _sibling_halves_call(g, tag):
    _, _, h, n = g.shape

    def body(g_ref, out_ref, send_sems, recv_sems):
        start, finish = _halves_exchange(g_ref, out_ref, send_sems, recv_sems)
        start()
        finish()

    return pl.pallas_call(
        body, name="rs_sibling_halves_" + tag, out_shape=SDS((N_CHIPS, h, n), g.dtype),
        in_specs=[BS(memory_space=pl.ANY)], out_specs=BS(memory_space=pl.ANY),
        scratch_shapes=_halves_sems(),
    )(g)


def _chip_exchange(p_ref, out_ref, send_sems, recv_sems):
    x, y, c = _mesh_pos()
    chips = [(1 - x, y), (x, 1 - y), (1 - x, 1 - y)]

    def copies():
        return [pltpu.make_async_remote_copy(
            src_ref=p_ref.at[2 * cx + cy], dst_ref=out_ref.at[j], send_sem=send_sems.at[j], recv_sem=recv_sems.at[j],
            device_id=(cx, cy, c), device_id_type=MESH) for j, (cx, cy) in enumerate(chips)]

    def start():
        for cp in copies():
            cp.start()

    def finish():
        for cp in copies():
            cp.wait()

    return start, finish


def _exchange_sems():
    return [pltpu.SemaphoreType.DMA((3,)), pltpu.SemaphoreType.DMA((3,))]


def _join_exchange(in_ref, out_ref, send_sems, recv_sems):
    h = in_ref.shape[1]
    q = h // 4
    x, y, c = _mesh_pos()

    def copy(k, half):
        return pltpu.make_async_remote_copy(
            src_ref=in_ref.at[half, pl.ds(k * q, q)], dst_ref=out_ref.at[half, pl.ds(k * q, q)],
            send_sem=send_sems.at[k], recv_sem=recv_sems.at[k],
            device_id=(x, y, 1 - c), device_id_type=MESH)

    def start():
        for k in range(4):
            copy(k, c).start()

    def finish():
        for k in range(4):
            copy(k, c).wait_send()
            copy(k, 1 - c).wait_recv()

    return start, finish


def _join_sems():
    return [pltpu.SemaphoreType.DMA((4,)), pltpu.SemaphoreType.DMA((4,))]


def _sibling_join_call(full, tag):
    def body(in_ref, out_ref, send_sems, recv_sems):
        start, finish = _join_exchange(in_ref, out_ref, send_sems, recv_sems)
        start()
        finish()

    return pl.pallas_call(
        body, name="rs_sibling_join_" + tag, out_shape=SDS(full.shape, full.dtype),
        in_specs=[BS(memory_space=pl.ANY)], out_specs=BS(memory_space=pl.ANY),
        scratch_shapes=_join_sems(), input_output_aliases={0: 0},
    )(full)


def _add_halves_call(g, recv, c_idx, tag):
    _, _, h, n = g.shape
    tr = h // 2

    def body(c_ref, g_ref, r_ref, o_ref):
        o_ref[...] = (g_ref[...].astype(F32) + r_ref[...].astype(F32)).astype(BF16)

    return pl.pallas_call(
        body, name="rs_add_halves_" + tag, out_shape=SDS((N_CHIPS, h, n), BF16),
        grid_spec=pltpu.PrefetchScalarGridSpec(
            num_scalar_prefetch=1, grid=(N_CHIPS, 2),
            in_specs=[BS((None, None, tr, n), lambda k, r, c_ref: (k, c_ref[0], r, 0)),
                      BS((None, tr, n), lambda k, r, c_ref: (k, r, 0))],
            out_specs=BS((None, tr, n), lambda k, r, c_ref: (k, r, 0))),
        compiler_params=_params("arbitrary", "arbitrary"),
    )(c_idx, g, recv)


def _add_chips_call(p, recv, chip_c_idx, tag):
    _, h, n = p.shape
    tr = h // 2

    def body(k_ref, p_ref, r_ref, o_ref):
        acc = p_ref[...].astype(F32)
        for j in range(3):
            acc = acc + r_ref[j].astype(F32)
        o_ref[...] = acc

    return pl.pallas_call(
        body, name="rs_add_chips_" + tag, out_shape=SDS((2, h, n), F32),
        grid_spec=pltpu.PrefetchScalarGridSpec(
            num_scalar_prefetch=1, grid=(2,),
            in_specs=[BS((None, tr, n), lambda r, k_ref: (k_ref[0], r, 0)),
                      BS((3, tr, n), lambda r, k_ref: (0, r, 0))],
            out_specs=BS((None, tr, n), lambda r, k_ref: (k_ref[1], r, 0))),
        compiler_params=_params("arbitrary"),
    )(chip_c_idx, p, recv)


def _load_rows(wg_hbm, w_vmem, sem, off):
    cp = pltpu.make_async_copy(wg_hbm.at[:, pl.ds(off, w_vmem.shape[1]), :], w_vmem, sem)
    cp.start()
    return cp


def _fwd_in_call(x, mod, pre_tm, wg, b_in, pack, order):
    S = x.shape[0]
    tmf = 2 * TM
    nt = S // tmf
    wc = IN_COLS // N_CHIPS

    def body(ord_ref, x_ref, mod_ref, g_ref, w_hbm, b_ref, pack_ref, p_ref, h_hbm, wg_out, w_vmem, h_scr, sems,
             send_sems, recv_sems, send_sems2, recv_sems2):
        q, i = pl.program_id(0), pl.program_id(1)
        rows = pl.ds(pl.multiple_of(i * tmf, tmf), tmf)
        start, arrive, drain = _relay_gather(pack_ref, wg_out, send_sems, recv_sems, O_IN, R_IN)
        start2, finish2 = _pack_gather(pack_ref, wg_out, send_sems2, recv_sems2, [(O_OUT, R_BR)])

        def weights(phase):
            return pltpu.make_async_copy(wg_out.at[ord_ref[phase], pl.ds(O_IN, R_IN), :], w_vmem.at[phase % 2],
                                         sems.at[phase % 2])

        @pl.when((q == 0) & (i == 0))
        def _():
            start()
            weights(0).start()
            weights(0).wait()

        @pl.when((q == 1) & (i == 0))
        def _():
            arrive(0)
            start2()
            weights(1).start()
            weights(1).wait()
            arrive(1)
            weights(2).start()

        @pl.when((q == 2) & (i == 0))
        def _():
            weights(2).wait()
            arrive(2)
            weights(3).start()

        @pl.when((q == 3) & (i == 0))
        def _():
            weights(3).wait()

        @pl.when(q == 0)
        def _():
            xv = x_ref[...]
            r = lax.rsqrt(jnp.mean(xv * xv, axis=-1, keepdims=True) + EPS)
            h = xv * r * g_ref[...] * (1.0 + mod_ref[:, D:2 * D]) + mod_ref[:, 0:D]
            h_scr[rows, :] = h.astype(BF16)

        hb = h_scr[rows, :]
        slot = q % 2
        for k in range(wc // D):
            p_ref[:, k * D:(k + 1) * D] = _mm(hb, w_vmem[slot, k * D:(k + 1) * D, :]) + b_ref[:, k * D:(k + 1) * D]

        @pl.when((q == N_CHIPS - 1) & (i == nt - 1))
        def _():
            cp = pltpu.make_async_copy(h_scr, h_hbm, sems.at[0])
            cp.start()
            drain()
            finish2()
            cp.wait()

    hbm = BS(memory_space=pl.ANY)
    return pl.pallas_call(
        body, name="fwd_in", out_shape=(SDS((S, IN_COLS), F32), SDS((S, D), BF16), SDS(wg.shape, wg.dtype)),
        grid_spec=pltpu.PrefetchScalarGridSpec(
            num_scalar_prefetch=1, grid=(N_CHIPS, nt),
            in_specs=[BS((tmf, D), lambda q, i, o: (jnp.where(q == 0, i, nt - 1), 0)),
                      BS((1, 6 * D), lambda q, i, o: (0, 0)),
                      BS((1, D), lambda q, i, o: (0, 0)), hbm, BS((1, wc), lambda q, i, o: (0, o[q])), hbm],
            out_specs=(BS((tmf, wc), lambda q, i, o: (i, o[q])), hbm, hbm),
            scratch_shapes=[pltpu.VMEM((2, R_IN, D), BF16), pltpu.VMEM((S, D), BF16), pltpu.SemaphoreType.DMA((2,))]
            + _relay_sems() + _gather_sems(1)),
        input_output_aliases={4: 2},
        compiler_params=_params("arbitrary", "arbitrary"),
    )(order, x, mod, pre_tm, wg, b_in, pack)


def _lower_bound(lg_ref):
    l0, l1 = lg_ref[0:1, :], lg_ref[1:2, :]
    mx = jnp.maximum(l0, l1)
    e0, e1 = jnp.exp(l0 - mx), jnp.exp(l1 - mx)
    return e0 / (e0 + e1)


def _tri_masks():
    ri = lax.broadcasted_iota(jnp.int32, (CHUNK, CHUNK), 0)
    ci = lax.broadcasted_iota(jnp.int32, (CHUNK, CHUNK), 1)
    return (ri >= ci).astype(F32), (ci >= ri).astype(F32)


def _cumsum_mm(tri, g):
    tb = tri.astype(BF16)
    hi = g.astype(BF16)
    r1 = g - hi.astype(F32)
    mid = r1.astype(BF16)
    lo = (r1 - mid.astype(F32)).astype(BF16)
    return _mm(tb, hi) + _mm(tb, mid) + _mm(tb, lo)


def _hg_gates(q_r, f_r, lb, tril):
    sq = _sig(q_r)
    q = q_r * sq
    sf = _sig(f_r)
    f = lb + (1.0 - lb) * sf
    k = 1.0 - f
    g = jnp.log(f)
    b = _cumsum_mm(tril, g)
    b_last = _rowsum(g)
    row = lax.broadcasted_iota(jnp.int32, g.shape, 0)
    ref = _rowsum(jnp.where(row < CHUNK // 2, g, 0.0))
    e = jnp.exp(b)
    eq = jnp.exp(jnp.minimum(b - ref, 80.0))
    ek = jnp.exp(jnp.minimum(ref - b, 80.0))
    dd = jnp.exp(b_last - b)
    return dict(sq=sq, q=q, sf=sf, f=f, k=k, e=e, eq=eq, ek=ek, dd=dd, elast=jnp.exp(b_last),
                qe=q * e, qt=q * eq, kt=k * ek, kd=k * dd)


def _hgrn_fwd_call(p, logits, gn, wg, pack):
    S = p.shape[0]
    ncb = TB // CHUNK
    ranges = [(O_FF1, R_FF)]

    def body(q_ref, f_ref, v_ref, og_ref, lg_ref, gn_ref, wg_in, pack_ref, o_ref, oa_ref, st_ref, wg_out,
             st_scr, send_sems, recv_sems):
        start, finish = _pack_gather(pack_ref, wg_out, send_sems, recv_sems, ranges)

        @pl.when(pl.program_id(0) == 0)
        def _():
            start()
            st_scr[...] = jnp.zeros_like(st_scr)

        lb = _lower_bound(lg_ref)
        tril, _ = _tri_masks()

        def chunk(ci, carry):
            rows = pl.ds(pl.multiple_of(ci * CHUNK, CHUNK), CHUNK)
            st_ref[ci] = st_scr[...]
            t = _hg_gates(q_ref[rows, :], f_ref[rows, :], lb, tril)
            v = v_ref[rows, :]
            for h in range(HEADS):
                sl = slice(h * DK, (h + 1) * DK)
                stp = st_scr[:, sl]
                vb = v[:, sl].astype(BF16)
                inter = _mm(t["qe"][:, sl].astype(BF16), stp.astype(BF16), NT)
                a = jnp.where(tril > 0.5, _mm(t["qt"][:, sl].astype(BF16), t["kt"][:, sl].astype(BF16), NT), 0.0)
                o = inter + _mm(a.astype(BF16), vb)
                st_scr[:, sl] = stp * t["elast"][:, sl] + _mm(vb, t["kd"][:, sl].astype(BF16), TN)
                oh = o * lax.rsqrt(jnp.mean(o * o, axis=-1, keepdims=True) + EPS)
                og = og_ref[rows, sl]
                o_ref[rows, sl] = o
                oa_ref[rows, sl] = (oh * gn_ref[:, sl] * (og * _sig(og))).astype(BF16)
            return carry

        lax.fori_loop(0, ncb, chunk, 0)

        @pl.when(pl.program_id(0) == S // TB - 1)
        def _():
            finish()

    col = lambda j: BS((TB, D), lambda i, j=j: (i, j))
    hbm = BS(memory_space=pl.ANY)
    return pl.pallas_call(
        body, name="hgrn_fwd", grid=(S // TB,),
        out_shape=(SDS((S, D), F32), SDS((S, D), BF16), SDS((S // CHUNK, DK, D), F32), SDS(wg.shape, wg.dtype)),
        in_specs=[col(0), col(1), col(2), col(3), BS((2, D), lambda i: (0, 0)), BS((1, D), lambda i: (0, 0)),
                  hbm, hbm],
        out_specs=(BS((TB, D), lambda i: (i, 0)), BS((TB, D), lambda i: (i, 0)),
                   BS((ncb, DK, D), lambda i: (i, 0, 0)), hbm),
        scratch_shapes=[pltpu.VMEM((DK, D), F32)] + _gather_sems(len(ranges)),
        input_output_aliases={6: 3},
        compiler_params=_params("arbitrary"),
    )(p, p, p, p, logits, gn, wg, pack)


def _layernorm_stats(uc):
    mu = jnp.mean(uc, axis=-1, keepdims=True)
    xc = uc - mu
    rs = lax.rsqrt(jnp.mean(xc * xc, axis=-1, keepdims=True) + EPS)
    return xc * rs, rs


EXT = HALO + TM + 8


def _fill_shifted(ext, shifted):
    for m in range(1, 8):
        shifted[m - 1] = ext[m:m + HALO + TM, :]


def _window(ext, shifted, s0, n):
    m = s0 % 8
    q = s0 - m
    return ext[q:q + n, :] if m == 0 else shifted[m - 1, q:q + n, :]


def _conv_fwd_call(p, dw, db, ln_g, ln_b, wg, pack):
    S = p.shape[0]
    ranges = [(O_FF2, R_FF), (O_BRA, 2 * R_BR)]

    def body(cv_ref, cg_ref, dw_ref, db_ref, g_ref, b_ref, wg_in, pack_ref, u_ref, uc_ref, cb_ref, wg_out,
             uext, ush, send_sems, recv_sems):
        start, finish = _pack_gather(pack_ref, wg_out, send_sems, recv_sems, ranges)

        @pl.when(pl.program_id(0) == 0)
        def _():
            start()
            uext[0:HALO, :] = jnp.zeros((HALO, D), F32)
            uext[HALO + TM:EXT, :] = jnp.zeros((EXT - HALO - TM, D), F32)

        u = cv_ref[...] * _sig(cg_ref[...])
        uext[HALO:HALO + TM, :] = u
        u_ref[...] = u
        _fill_shifted(uext, ush)
        for rb in range(TM // SUB):
            acc = jnp.broadcast_to(db_ref[...], (SUB, D))
            for j in range(CONV_K):
                s0 = HALO - (CONV_K - 1) + j + rb * SUB
                acc = acc + dw_ref[j:j + 1, :] * _window(uext, ush, s0, SUB)
            uc_ref[rb * SUB:(rb + 1) * SUB, :] = acc
            xh, _ = _layernorm_stats(acc)
            ln = xh * g_ref[...] + b_ref[...]
            cb_ref[rb * SUB:(rb + 1) * SUB, :] = (ln * _sig(ln)).astype(BF16)
        uext[0:HALO, :] = uext[TM:TM + HALO, :]

        @pl.when(pl.program_id(0) == S // TM - 1)
        def _():
            finish()

    vec = BS((1, D), lambda i: (0, 0))
    hbm = BS(memory_space=pl.ANY)
    return pl.pallas_call(
        body, name="conv_fwd", grid=(S // TM,),
        out_shape=(SDS((S, D), F32), SDS((S, D), F32), SDS((S, D), BF16), SDS(wg.shape, wg.dtype)),
        in_specs=[BS((TM, D), lambda i: (i, 4)), BS((TM, D), lambda i: (i, 5)),
                  BS((CONV_K, D), lambda i: (0, 0)), vec, vec, vec, hbm, hbm],
        out_specs=(BS((TM, D), lambda i: (i, 0)),) * 3 + (hbm,),
        scratch_shapes=[pltpu.VMEM((EXT, D), F32), pltpu.VMEM((7, HALO + TM, D), F32)] + _gather_sems(len(ranges)),
        input_output_aliases={6: 3},
        compiler_params=_params("arbitrary"),
    )(p, p, dw, db, ln_g, ln_b, wg, pack)


def _mm_rows(a, w_ref):
    acc = _mm(a[:, 0:R_BR], w_ref[0])
    for k in range(1, N_CHIPS):
        acc = acc + _mm(a[:, k * R_BR:(k + 1) * R_BR], w_ref[k])
    return acc


def _mm_rows_t(a, w_ref):
    return jnp.concatenate([_mm(a, w_ref[k], NT) for k in range(N_CHIPS)], axis=1)


def _br_spec(off):
    return BS((N_CHIPS, R_BR, D), lambda i: (0, off // R_BR, 0))


def _merge_fwd_call(oa, cb, p, x, mod, post_tm, pre_cm, wg):
    S = x.shape[0]

    def body(oa_ref, cb_ref, ga_ref, gb_ref, x_ref, mod_ref, post_ref, pre_ref, wa_ref, wb_ref, wo_ref,
             ya_ref, yb_ref, mg_ref, y_ref, x2_ref, h2_ref):
        ya = _mm_rows(oa_ref[...], wa_ref)
        yb = _mm_rows(cb_ref[...], wb_ref)
        ya_ref[...] = ya.astype(BF16)
        yb_ref[...] = yb.astype(BF16)
        mg = (_sig(ga_ref[...]) * ya + _sig(gb_ref[...]) * yb).astype(BF16)
        mg_ref[...] = mg
        y = _mm_rows(mg, wo_ref)
        y_ref[...] = y
        n = y * lax.rsqrt(jnp.mean(y * y, axis=-1, keepdims=True) + EPS) * post_ref[...]
        x2 = x_ref[...] + mod_ref[:, 2 * D:3 * D] * n
        x2_ref[...] = x2
        r2 = lax.rsqrt(jnp.mean(x2 * x2, axis=-1, keepdims=True) + EPS)
        h2 = x2 * r2 * pre_ref[...] * (1.0 + mod_ref[:, 4 * D:5 * D]) + mod_ref[:, 3 * D:4 * D]
        h2_ref[...] = h2.astype(BF16)

    tile = BS((TM, D), lambda i: (i, 0))
    vec = BS((1, D), lambda i: (0, 0))
    return pl.pallas_call(
        body, name="merge_fwd", grid=(S // TM,),
        out_shape=(SDS((S, D), BF16), SDS((S, D), BF16), SDS((S, D), BF16), SDS((S, D), F32), SDS((S, D), F32),
                   SDS((S, D), BF16)),
        in_specs=[tile, tile, BS((TM, D), lambda i: (i, 6)), BS((TM, D), lambda i: (i, 7)), tile,
                  BS((1, 6 * D), lambda i: (0, 0)), vec, vec, _br_spec(O_BRA), _br_spec(O_BRB), _br_spec(O_OUT)],
        out_specs=(tile,) * 6,
        compiler_params=_params("arbitrary"),
    )(oa, cb, p, p, x, mod, post_tm, pre_cm, wg, wg, wg)


def _ffn_call(h2, x2, target, mod, post_cm, pre_cm, wg):
    S = x2.shape[0]

    def body(h2_ref, x2_ref, t_ref, mod_ref, post_ref, pre_ref, w_hbm,
             z_ref, da_ref, dy2_ref, dx2_ref, acc_ref, w1_v, w2_v, ra_scr, sems):
        @pl.when(pl.program_id(0) == 0)
        def _():
            c1 = _load_rows(w_hbm, w1_v, sems.at[0], O_FF1)
            c2 = _load_rows(w_hbm, w2_v, sems.at[1], O_FF2)
            c1.wait()
            c2.wait()
            acc_ref[...] = jnp.zeros_like(acc_ref)

        h2 = h2_ref[...]
        for k in range(N_CHIPS):
            ra = jnp.maximum(_mm(h2, w1_v[k]), 0.0)
            ra_scr[:, k * D:(k + 1) * D] = ra
            z_ref[:, k * D:(k + 1) * D] = (ra * ra).astype(BF16)
        y2 = _mm(z_ref[:, 0:D], w2_v[0])
        for k in range(1, N_CHIPS):
            y2 = y2 + _mm(z_ref[:, k * D:(k + 1) * D], w2_v[k])
        ry = lax.rsqrt(jnp.mean(y2 * y2, axis=-1, keepdims=True) + EPS)
        yn = y2 * ry
        n = yn * post_ref[...]
        g2 = mod_ref[:, 5 * D:6 * D]
        x2 = x2_ref[...]
        err = x2 + g2 * n - t_ref[...]
        acc_ref[5:6, :] += _rowsum(err * err) * (0.5 / D)
        dout = err * (1.0 / D)
        acc_ref[0:1, :] += _rowsum(dout * n)
        dn = dout * g2
        acc_ref[1:2, :] += _rowsum(dn * yn)
        dyn = dn * post_ref[...]
        dy2 = (ry * (dyn - yn * jnp.mean(dyn * yn, axis=-1, keepdims=True))).astype(BF16)
        dy2_ref[...] = dy2
        for k in range(N_CHIPS):
            dz = _mm(dy2, w2_v[k], NT)
            da_ref[:, k * D:(k + 1) * D] = (dz * (2.0 * ra_scr[:, k * D:(k + 1) * D])).astype(BF16)
        dh2 = jnp.zeros((TM, D), F32)
        for k in range(N_CHIPS):
            dh2 = dh2 + _mm(da_ref[:, k * D:(k + 1) * D], w1_v[k], NT)
        r2 = lax.rsqrt(jnp.mean(x2 * x2, axis=-1, keepdims=True) + EPS)
        xn = x2 * r2
        yv = xn * pre_ref[...]
        acc_ref[2:3, :] += _rowsum(dh2)
        acc_ref[3:4, :] += _rowsum(dh2 * yv)
        dyv = dh2 * (1.0 + mod_ref[:, 4 * D:5 * D])
        acc_ref[4:5, :] += _rowsum(dyv * xn)
        dxn = dyv * pre_ref[...]
        dx2_ref[...] = dout + r2 * (dxn - xn * jnp.mean(dxn * xn, axis=-1, keepdims=True))

    tile = BS((TM, D), lambda i: (i, 0))
    wide = BS((TM, D_FF), lambda i: (i, 0))
    vec = BS((1, D), lambda i: (0, 0))
    return pl.pallas_call(
        body, name="ffn_fwd_bwd", grid=(S // TM,),
        out_shape=(SDS((S, D_FF), BF16), SDS((S, D_FF), BF16), SDS((S, D), BF16), SDS((S, D), F32),
                   SDS((8, D), F32)),
        in_specs=[tile, tile, tile, BS((1, 6 * D), lambda i: (0, 0)), vec, vec, BS(memory_space=pl.ANY)],
        out_specs=(wide, wide, tile, tile, BS((8, D), lambda i: (0, 0))),
        scratch_shapes=[pltpu.VMEM((N_CHIPS, R_FF, D), BF16), pltpu.VMEM((N_CHIPS, R_FF, D), BF16),
                        pltpu.VMEM((TM, D_FF), F32),
                        pltpu.SemaphoreType.DMA((2,))],
        compiler_params=_params("arbitrary"),
    )(h2, x2, target, mod, post_cm, pre_cm, wg)


def _merge_bwd_call(dx2, y, ya, yb, p, mod, post_tm, wg, g):
    S = y.shape[0]

    def body(dx2_ref, y_ref, ya_ref, yb_ref, ga_ref, gb_ref, mod_ref, post_ref, wa_ref, wb_ref, wo_ref, g_ref,
             dy_ref, dya_ref, dyb_ref, doa_ref, dcb_ref, dpg_ref, acc_ref, bsum_ref, hr_ref, send_sems, recv_sems):
        start, finish = _halves_exchange(g_ref, hr_ref, send_sems, recv_sems)

        @pl.when(pl.program_id(0) == 0)
        def _():
            start()
            acc_ref[...] = jnp.zeros_like(acc_ref)
            bsum_ref[...] = jnp.zeros_like(bsum_ref)

        y = y_ref[...]
        ry = lax.rsqrt(jnp.mean(y * y, axis=-1, keepdims=True) + EPS)
        yn = y * ry
        dx2 = dx2_ref[...]
        acc_ref[0:1, :] += _rowsum(dx2 * (yn * post_ref[...]))
        dn = dx2 * mod_ref[:, 2 * D:3 * D]
        acc_ref[1:2, :] += _rowsum(dn * yn)
        dyn = dn * post_ref[...]
        dy = (ry * (dyn - yn * jnp.mean(dyn * yn, axis=-1, keepdims=True))).astype(BF16)
        dy_ref[...] = dy
        dmg = _mm_rows_t(dy, wo_ref)
        sa, sb = _sig(ga_ref[...]), _sig(gb_ref[...])
        dya = (dmg * sa).astype(BF16)
        dyb = (dmg * sb).astype(BF16)
        dya_ref[...] = dya
        dyb_ref[...] = dyb
        dga = dmg * ya_ref[...].astype(F32) * (sa * (1.0 - sa))
        dgb = dmg * yb_ref[...].astype(F32) * (sb * (1.0 - sb))
        dpg_ref[:, 0:D] = dga.astype(BF16)
        dpg_ref[:, D:2 * D] = dgb.astype(BF16)
        bsum_ref[:, 0:D] += _rowsum(dga)
        bsum_ref[:, D:2 * D] += _rowsum(dgb)
        doa_ref[...] = _mm_rows_t(dya, wa_ref)
        dcb_ref[...] = _mm_rows_t(dyb, wb_ref)

        @pl.when(pl.program_id(0) == S // TM - 1)
        def _():
            finish()

    tile = BS((TM, D), lambda i: (i, 0))
    vec = BS((1, D), lambda i: (0, 0))
    return pl.pallas_call(
        body, name="merge_bwd", grid=(S // TM,),
        out_shape=(SDS((S, D), BF16), SDS((S, D), BF16), SDS((S, D), BF16), SDS((S, D), F32), SDS((S, D), F32),
                   SDS((S, 2 * D), BF16), SDS((8, D), F32), SDS((1, 2 * D), F32),
                   SDS((N_CHIPS,) + g.shape[2:], g.dtype)),
        in_specs=[tile, tile, tile, tile, BS((TM, D), lambda i: (i, 6)), BS((TM, D), lambda i: (i, 7)),
                  BS((1, 6 * D), lambda i: (0, 0)), vec, _br_spec(O_BRA), _br_spec(O_BRB), _br_spec(O_OUT),
                  BS(memory_space=pl.ANY)],
        out_specs=(tile, tile, tile, tile, tile, BS((TM, 2 * D), lambda i: (i, 0)),
                   BS((8, D), lambda i: (0, 0)), BS((1, 2 * D), lambda i: (0, 0)), BS(memory_space=pl.ANY)),
        scratch_shapes=_halves_sems(),
        compiler_params=_params("arbitrary"),
    )(dx2, y, ya, yb, p, p, mod, post_tm, wg, wg, wg, g)


def _hgrn_bwd_call(p, o, doa, st, logits, gn, part, g):
    S = p.shape[0]
    nb = S // TB
    ncb = TB // CHUNK

    def body(q_ref, f_ref, v_ref, og_ref, o_ref, doa_ref, st_ref, lg_ref, gn_ref, part_ref, g_ref,
             dp_ref, bsum_ref, dlg_ref, dgn_ref, recv_ref, hr_ref,
             dst_scr, dlb_scr, dqe_s, dqt_s, dkt_s, dkd_s, dv_s, dog_s, dble_s, send_sems, recv_sems, hs, hr):
        i = pl.program_id(0)
        start, finish = _chip_exchange(part_ref, recv_ref, send_sems, recv_sems)
        start_h, finish_h = _halves_exchange(g_ref, hr_ref, hs, hr)

        @pl.when(i == 0)
        def _():
            start_h()
            start()
            dst_scr[...] = jnp.zeros_like(dst_scr)
            dlb_scr[...] = jnp.zeros_like(dlb_scr)
            bsum_ref[...] = jnp.zeros_like(bsum_ref)
            dgn_ref[...] = jnp.zeros_like(dgn_ref)

        lb = _lower_bound(lg_ref)
        tril, triu = _tri_masks()

        def chunk(tt, carry):
            ci = ncb - 1 - tt
            rows = pl.ds(pl.multiple_of(ci * CHUNK, CHUNK), CHUNK)
            q_r, f_r = q_ref[rows, :], f_ref[rows, :]
            t = _hg_gates(q_r, f_r, lb, tril)
            v = v_ref[rows, :]
            for h in range(HEADS):
                sl = slice(h * DK, (h + 1) * DK)
                stp = st_ref[ci, :, sl]
                stb = stp.astype(BF16)
                qeb = t["qe"][:, sl].astype(BF16)
                qtb = t["qt"][:, sl].astype(BF16)
                ktb = t["kt"][:, sl].astype(BF16)
                kdb = t["kd"][:, sl].astype(BF16)
                vb = v[:, sl].astype(BF16)
                a = jnp.where(tril > 0.5, _mm(qtb, ktb, NT), 0.0)
                o_h = o_ref[rows, sl]
                rinv = lax.rsqrt(jnp.mean(o_h * o_h, axis=-1, keepdims=True) + EPS)
                oh = o_h * rinv
                og = og_ref[rows, sl]
                so = _sig(og)
                d_oa = doa_ref[rows, sl]
                don = d_oa * (og * so)
                dog_s[:, sl] = d_oa * (oh * gn_ref[:, sl]) * _dsilu(og, so)
                dgn_ref[:, sl] += _rowsum(don * oh)
                doh = don * gn_ref[:, sl]
                do = (rinv * (doh - oh * jnp.mean(doh * oh, axis=-1, keepdims=True))).astype(BF16)
                dqe_s[:, sl] = _mm(do, stb, NN)
                dstp = _mm(do, qeb, TN)
                dab = jnp.where(tril > 0.5, _mm(do, vb, NT), 0.0).astype(BF16)
                dqt_s[:, sl] = _mm(dab, ktb, NN)
                dkt_s[:, sl] = _mm(dab, qtb, TN)
                dstn = dst_scr[:, sl]
                dsb = dstn.astype(BF16)
                dkd_s[:, sl] = _mm(vb, dsb, NN)
                dv_s[:, sl] = _mm(a.astype(BF16), do, TN) + _mm(kdb, dsb, NT)
                el = t["elast"][:, sl]
                dst_scr[:, sl] = dstn * el + dstp
                dble_s[:, sl] = el * _rowsum(stp * dstn)
            dqe, dqt, dkt, dkd = dqe_s[...], dqt_s[...], dkt_s[...], dkd_s[...]
            dq = dqe * t["e"] + dqt * t["eq"]
            dk = dkt * t["ek"] + dkd * t["dd"]
            dkk = dkd * t["kd"]
            qt_r = t["qt"].astype(BF16).astype(F32)
            kt_r = t["kt"].astype(BF16).astype(F32)
            dbv = dqe * t["qe"] + dqt * qt_r - dkt * kt_r - dkk
            dg = _cumsum_mm(triu, dbv) + (_rowsum(dkk) + dble_s[...])
            df = dg / t["f"] - dk
            sf = t["sf"]
            dlb_scr[...] += _rowsum(df * (1.0 - sf))
            dqr = dq * _dsilu(q_r, t["sq"])
            dfr = df * (1.0 - lb) * (sf * (1.0 - sf))
            dvv, dog = dv_s[...], dog_s[...]
            dp_ref[rows, 0:D] = dqr.astype(BF16)
            dp_ref[rows, D:2 * D] = dfr.astype(BF16)
            dp_ref[rows, 2 * D:3 * D] = dvv.astype(BF16)
            dp_ref[rows, 3 * D:4 * D] = dog.astype(BF16)
            bsum_ref[:, 0:D] += _rowsum(dqr)
            bsum_ref[:, D:2 * D] += _rowsum(dfr)
            bsum_ref[:, 2 * D:3 * D] += _rowsum(dvv)
            bsum_ref[:, 3 * D:4 * D] += _rowsum(dog)
            return carry

        lax.fori_loop(0, ncb, chunk, 0)

        dl = dlb_scr[...] * lb * (1.0 - lb)
        dlg_ref[0:1, :] = dl
        dlg_ref[1:2, :] = -dl

        @pl.when(i == nb - 1)
        def _():
            finish_h()
            finish()

    col = lambda j: BS((TB, D), lambda i, j=j: (nb - 1 - i, j))
    rev = BS((TB, D), lambda i: (nb - 1 - i, 0))
    cd = pltpu.VMEM((CHUNK, D), F32)
    return pl.pallas_call(
        body, name="hgrn_bwd", grid=(nb,),
        out_shape=(SDS((S, 4 * D), BF16), SDS((1, 4 * D), F32), SDS((2, D), F32), SDS((1, D), F32),
                   SDS((3,) + part.shape[1:], part.dtype), SDS((N_CHIPS,) + g.shape[2:], g.dtype)),
        in_specs=[col(0), col(1), col(2), col(3), rev, rev, BS((ncb, DK, D), lambda i: (nb - 1 - i, 0, 0)),
                  BS((2, D), lambda i: (0, 0)), BS((1, D), lambda i: (0, 0)), BS(memory_space=pl.ANY),
                  BS(memory_space=pl.ANY)],
        out_specs=(BS((TB, 4 * D), lambda i: (nb - 1 - i, 0)), BS((1, 4 * D), lambda i: (0, 0)),
                   BS((2, D), lambda i: (0, 0)), BS((1, D), lambda i: (0, 0)), BS(memory_space=pl.ANY),
                   BS(memory_space=pl.ANY)),
        scratch_shapes=[pltpu.VMEM((DK, D), F32), pltpu.VMEM((1, D), F32), cd, cd, cd, cd, cd, cd,
                        pltpu.VMEM((1, D), F32)] + _exchange_sems() + _halves_sems(),
        compiler_params=_params("arbitrary"),
    )(p, p, p, p, o, doa, st, logits, gn, part, g)


def _conv_bwd_call(dcb, uc, u, p, dw, ln_g, ln_b, part):
    S = uc.shape[0]
    nb = S // TM
    hb = TM // HALO

    def body(dcb_ref, uc_ref, u_ref, uh_ref, cv_ref, cg_ref, dw_ref, g_ref, b_ref, part_ref,
             dp_ref, bsum_ref, ddw_ref, acc_ref, recv_ref, uext, dext, ush, dsh, send_sems, recv_sems):
        i = pl.program_id(0)
        start, finish = _chip_exchange(part_ref, recv_ref, send_sems, recv_sems)

        @pl.when(i == 0)
        def _():
            start()
            dext[TM:EXT, :] = jnp.zeros((EXT - TM, D), F32)
            uext[HALO + TM:EXT, :] = jnp.zeros((EXT - HALO - TM, D), F32)
            bsum_ref[...] = jnp.zeros_like(bsum_ref)
            ddw_ref[...] = jnp.zeros_like(ddw_ref)
            acc_ref[...] = jnp.zeros_like(acc_ref)

        first_tile = (nb - 1 - i) == 0
        uext[0:HALO, :] = jnp.where(first_tile, 0.0, uh_ref[...])
        uext[HALO:HALO + TM, :] = u_ref[...]
        _fill_shifted(uext, ush)

        for rb in range(TM // SUB):
            rs_ = slice(rb * SUB, (rb + 1) * SUB)
            xh, rs = _layernorm_stats(uc_ref[rs_, :])
            ln = xh * g_ref[...] + b_ref[...]
            dln = dcb_ref[rs_, :] * _dsilu(ln, _sig(ln))
            acc_ref[1:2, :] += _rowsum(dln * xh)
            acc_ref[2:3, :] += _rowsum(dln)
            dxh = dln * g_ref[...]
            duc = rs * (dxh - jnp.mean(dxh, axis=-1, keepdims=True)
                        - xh * jnp.mean(dxh * xh, axis=-1, keepdims=True))
            dext[rs_, :] = duc
            acc_ref[0:1, :] += _rowsum(duc)
        _fill_shifted(dext, dsh)

        for j in range(CONV_K):
            part = jnp.zeros((SUB, D), F32)
            for rb in range(TM // SUB):
                s0 = HALO - (CONV_K - 1) + j + rb * SUB
                part = part + dext[rb * SUB:(rb + 1) * SUB, :] * _window(uext, ush, s0, SUB)
            ddw_ref[j:j + 1, :] += _rowsum(part)

        for rb in range(TM // SUB):
            rs_ = slice(rb * SUB, (rb + 1) * SUB)
            du = jnp.zeros((SUB, D), F32)
            for j in range(CONV_K):
                s0 = rb * SUB + (CONV_K - 1) - j
                du = du + dw_ref[j:j + 1, :] * _window(dext, dsh, s0, SUB)
            cg = cg_ref[rs_, :]
            sg = _sig(cg)
            dcv = du * sg
            dcg = du * cv_ref[rs_, :] * (sg * (1.0 - sg))
            dp_ref[rs_, 0:D] = dcv.astype(BF16)
            dp_ref[rs_, D:2 * D] = dcg.astype(BF16)
            bsum_ref[:, 0:D] += _rowsum(dcv)
            bsum_ref[:, D:2 * D] += _rowsum(dcg)

        dext[TM:TM + HALO, :] = dext[0:HALO, :]

        @pl.when(i == nb - 1)
        def _():
            finish()

    rev = BS((TM, D), lambda i: (nb - 1 - i, 0))
    vec = BS((1, D), lambda i: (0, 0))
    return pl.pallas_call(
        body, name="conv_bwd", grid=(nb,),
        out_shape=(SDS((S, 2 * D), BF16), SDS((1, 2 * D), F32), SDS((32, D), F32), SDS((8, D), F32),
                   SDS((3,) + part.shape[1:], part.dtype)),
        in_specs=[rev, rev, rev, BS((HALO, D), lambda i: (jnp.maximum((nb - 1 - i) * hb - 1, 0), 0)),
                  BS((TM, D), lambda i: (nb - 1 - i, 4)), BS((TM, D), lambda i: (nb - 1 - i, 5)),
                  BS((CONV_K, D), lambda i: (0, 0)), vec, vec, BS(memory_space=pl.ANY)],
        out_specs=(BS((TM, 2 * D), lambda i: (nb - 1 - i, 0)), BS((1, 2 * D), lambda i: (0, 0)),
                   BS((32, D), lambda i: (0, 0)), BS((8, D), lambda i: (0, 0)), BS(memory_space=pl.ANY)),
        scratch_shapes=[pltpu.VMEM((EXT, D), F32), pltpu.VMEM((EXT, D), F32),
                        pltpu.VMEM((7, HALO + TM, D), F32), pltpu.VMEM((7, HALO + TM, D), F32)] + _exchange_sems(),
        compiler_params=_params("arbitrary"),
    )(dcb, uc, u, u, p, p, dw, ln_g, ln_b, part)


def _in_bwd_call(dp_hg, dp_cv, dp_gt, x, dx2, mod, pre_tm, wg, part, full_a, full_b):
    S = x.shape[0]
    tm = TM

    def body(hg_ref, cv_ref, gt_ref, x_ref, dx2_ref, mod_ref, g_ref, w_hbm, part_ref, fa_in, fb_in,
             gx_ref, acc_ref, recv_ref, fa_out, fb_out, w_vmem, sem, send_sems, recv_sems, sa, ra, sb, rb):
        start, finish = _chip_exchange(part_ref, recv_ref, send_sems, recv_sems)
        start_a, finish_a = _join_exchange(fa_in, fa_out, sa, ra)
        start_b, finish_b = _join_exchange(fb_in, fb_out, sb, rb)

        @pl.when(pl.program_id(0) == 0)
        def _():
            start_a()
            start_b()
            start()
            _load_rows(w_hbm, w_vmem, sem, O_IN).wait()
            acc_ref[...] = jnp.zeros_like(acc_ref)

        dh = jnp.zeros((tm, D), F32)
        for k in range(IN_COLS // D):
            src, kk = ((hg_ref, k), (cv_ref, k - 4), (gt_ref, k - 6))[0 if k < 4 else (1 if k < 6 else 2)]
            dh = dh + _mm(src[:, kk * D:(kk + 1) * D], w_vmem[k // 2, (k % 2) * D:(k % 2 + 1) * D, :], NT)
        xv = x_ref[...]
        r = lax.rsqrt(jnp.mean(xv * xv, axis=-1, keepdims=True) + EPS)
        xn = xv * r
        yv = xn * g_ref[...]
        acc_ref[0:1, :] += _rowsum(dh)
        acc_ref[1:2, :] += _rowsum(dh * yv)
        dyv = dh * (1.0 + mod_ref[:, D:2 * D])
        acc_ref[2:3, :] += _rowsum(dyv * xn)
        dxn = dyv * g_ref[...]
        gx_ref[...] = dx2_ref[...] + r * (dxn - xn * jnp.mean(dxn * xn, axis=-1, keepdims=True))

        @pl.when(pl.program_id(0) == S // tm - 1)
        def _():
            finish_a()
            finish_b()
            finish()

    tile = BS((tm, D), lambda i: (i, 0))
    hbm = BS(memory_space=pl.ANY)
    return pl.pallas_call(
        body, name="in_bwd", grid=(S // tm,),
        out_shape=(SDS((S, D), F32), SDS((8, D), F32), SDS((3,) + part.shape[1:], part.dtype),
                   SDS(full_a.shape, full_a.dtype), SDS(full_b.shape, full_b.dtype)),
        in_specs=[BS((tm, 4 * D), lambda i: (i, 0)), BS((tm, 2 * D), lambda i: (i, 0)),
                  BS((tm, 2 * D), lambda i: (i, 0)), tile, tile, BS((1, 6 * D), lambda i: (0, 0)),
                  BS((1, D), lambda i: (0, 0)), hbm, hbm, hbm, hbm],
        out_specs=(tile, BS((8, D), lambda i: (0, 0)), hbm, hbm, hbm),
        scratch_shapes=[pltpu.VMEM((N_CHIPS, R_IN, D), BF16), pltpu.SemaphoreType.DMA] + _exchange_sems()
        + _join_sems() + _join_sems(),
        input_output_aliases={9: 3, 10: 4},
        compiler_params=_params("arbitrary"),
    )(dp_hg, dp_cv, dp_gt, x, dx2, mod, pre_tm, wg, part, full_a, full_b)


def _wgrad_call(gp, a, b, name, bm, place, rows):
    S, M = a.shape
    N = b.shape[1]
    bk = min(S, 1024)
    nk = S // bk

    def body(a_ref, b_ref, *rest):
        o_ref, acc = rest[-2], rest[-1]
        k = pl.program_id(2)

        @pl.when(k == 0)
        def _():
            acc[...] = jnp.zeros_like(acc)

        acc[...] += _mm(a_ref[...], b_ref[...], TN)

        @pl.when(k == nk - 1)
        def _():
            o_ref[...] = acc[...].astype(BF16)

    in_specs = [BS((bk, bm), lambda i, j, k: (k, i)), BS((bk, D), lambda i, j, k: (k, j))]
    args = [a, b]
    if gp is not None:
        in_specs.append(BS(memory_space=pl.ANY))
        args.append(gp)
    return pl.pallas_call(
        body, name=name, grid=(M // bm, N // D, nk),
        out_shape=SDS((N_CHIPS, rows, D), BF16),
        in_specs=in_specs,
        out_specs=BS((None, bm, D), lambda i, j, k: (*place(i, j), 0)),
        scratch_shapes=[pltpu.VMEM((bm, D), F32)],
        input_output_aliases={} if gp is None else {2: 0},
        compiler_params=_params("parallel", "parallel", "arbitrary"),
    )(*args)


def _wgrad_rows_call(gp, a, b, name, blk):
    S = a.shape[0]
    bk = min(S, 1024)
    nk = S // bk

    def body(a_ref, b_ref, *rest):
        o_ref, acc = rest[-2], rest[-1]
        k = pl.program_id(0)

        @pl.when(k == 0)
        def _():
            acc[...] = jnp.zeros_like(acc)

        acc[...] += _mm(a_ref[...], b_ref[...], TN)

        @pl.when(k == nk - 1)
        def _():
            for c in range(N_CHIPS):
                o_ref[c] = acc[c * R_BR:(c + 1) * R_BR, :].astype(BF16)

    in_specs = [BS((bk, D), lambda k: (k, 0)), BS((bk, D), lambda k: (k, 0))]
    args = [a, b]
    if gp is not None:
        in_specs.append(BS(memory_space=pl.ANY))
        args.append(gp)
    return pl.pallas_call(
        body, name=name, grid=(nk,),
        out_shape=SDS((N_CHIPS, 3 * R_BR, D), BF16),
        in_specs=in_specs,
        out_specs=BS((N_CHIPS, R_BR, D), lambda k: (0, blk, 0)),
        scratch_shapes=[pltpu.VMEM((D, D), F32)],
        input_output_aliases={} if gp is None else {2: 0},
        compiler_params=_params("arbitrary"),
    )(*args)


def _outer_call(cact, dmod):
    n = dmod.shape[1]

    def body(a_ref, b_ref, o_ref):
        o_ref[...] = _mm(a_ref[...], b_ref[...], TN, HI)

    return pl.pallas_call(
        body, name="wgrad_ada", out_shape=SDS((D, n), F32),
        compiler_params=pltpu.CompilerParams(vmem_limit_bytes=VMEM_LIMIT),
    )(cact, dmod)


def _adamw_call(w, g, m, v, name):
    R, C = w.shape
    tr = R
    while tr * C > 512 * 1024 and tr % 16 == 0:
        tr //= 2
    c1 = 1.0 - ADAM_B1 ** ADAM_STEP
    c2 = 1.0 - ADAM_B2 ** ADAM_STEP

    def body(w_ref, g_ref, m_ref, v_ref, d_ref, m2_ref, v2_ref):
        g = g_ref[...]
        m2 = ADAM_B1 * m_ref[...] + (1.0 - ADAM_B1) * g
        v2 = ADAM_B2 * v_ref[...] + (1.0 - ADAM_B2) * (g * g)
        m2_ref[...] = m2
        v2_ref[...] = v2
        d_ref[...] = -ADAM_LR * ((m2 / c1) / (jnp.sqrt(v2 / c2) + ADAM_EPS) + ADAM_WD * w_ref[...])

    tile = BS((tr, C), lambda i: (i, 0))
    return pl.pallas_call(
        body, name=name, grid=(R // tr,), out_shape=(SDS((R, C), F32),) * 3,
        in_specs=[tile] * 4, out_specs=(tile,) * 3, compiler_params=_params("parallel"),
    )(w, g, m, v)


def _adamw_rows_call(ws, g, ms, vs, name):
    k = len(ws)
    r = ws[0].shape[0]
    c1 = 1.0 - ADAM_B1 ** ADAM_STEP
    c2 = 1.0 - ADAM_B2 ** ADAM_STEP

    def body(g_ref, *refs):
        ins, outs = refs[:3 * k], refs[3 * k:]
        for j in range(k):
            w_ref, m_ref, v_ref = ins[j], ins[k + j], ins[2 * k + j]
            d_ref, m2_ref, v2_ref = outs[j], outs[k + j], outs[2 * k + j]
            g = g_ref[j * r:(j + 1) * r, :]
            m2 = ADAM_B1 * m_ref[...] + (1.0 - ADAM_B1) * g
            v2 = ADAM_B2 * v_ref[...] + (1.0 - ADAM_B2) * (g * g)
            m2_ref[...] = m2
            v2_ref[...] = v2
            d_ref[...] = -ADAM_LR * ((m2 / c1) / (jnp.sqrt(v2 / c2) + ADAM_EPS) + ADAM_WD * w_ref[...])

    out = pl.pallas_call(
        body, name=name, out_shape=(SDS(ws[0].shape, F32),) * (3 * k),
        compiler_params=pltpu.CompilerParams(vmem_limit_bytes=VMEM_LIMIT),
    )(g, *ws, *ms, *vs)
    return out[:k], out[k:2 * k], out[2 * k:]


def _adamw_small_call(ws, ms, vs, row0, ssum, g_dw):
    n = len(ws)
    c1 = 1.0 - ADAM_B1 ** ADAM_STEP
    c2 = 1.0 - ADAM_B2 ** ADAM_STEP

    def adam(w, g, m, v):
        m2 = ADAM_B1 * m + (1.0 - ADAM_B1) * g
        v2 = ADAM_B2 * v + (1.0 - ADAM_B2) * (g * g)
        return -ADAM_LR * ((m2 / c1) / (jnp.sqrt(v2 / c2) + ADAM_EPS) + ADAM_WD * w), m2, v2

    def body(s_ref, gdw_ref, *refs):
        ins, outs = refs[:3 * n], refs[3 * n:]
        for j in range(n):
            w_ref, m_ref, v_ref = ins[j], ins[n + j], ins[2 * n + j]
            g_ref, d_ref, m2_ref, v2_ref = outs[j], outs[n + j], outs[2 * n + j], outs[3 * n + j]
            if j == n - 1:
                pieces = [(slice(None), slice(None), gdw_ref[...])]
            elif w_ref.shape[0] == 1:
                pieces = [(slice(None), slice(i * D, (i + 1) * D), s_ref[row0[j] + i:row0[j] + i + 1, :])
                          for i in range(w_ref.shape[1] // D)]
            else:
                pieces = [(slice(None), slice(None), s_ref[row0[j]:row0[j] + w_ref.shape[0], :])]
            for rs, cs, g in pieces:
                d, m2, v2 = adam(w_ref[rs, cs], g, m_ref[rs, cs], v_ref[rs, cs])
                g_ref[rs, cs] = g
                d_ref[rs, cs] = d
                m2_ref[rs, cs] = m2
                v2_ref[rs, cs] = v2

    out = pl.pallas_call(
        body, name="adamw_small", out_shape=tuple(SDS(w.shape, F32) for w in ws) * 4,
        compiler_params=pltpu.CompilerParams(vmem_limit_bytes=VMEM_LIMIT),
    )(ssum, g_dw, *ws, *ms, *vs)
    return [(out[j], out[n + j], out[2 * n + j], out[3 * n + j]) for j in range(n)]


def _adamw_gather_call(w, g, m, v, srows, name):
    R, C = w.shape
    tr = R
    while tr * C > 512 * 1024 and tr % 16 == 0:
        tr //= 2
    nsteps = R // tr
    mr = srows.shape[0]
    c1 = 1.0 - ADAM_B1 ** ADAM_STEP
    c2 = 1.0 - ADAM_B2 ** ADAM_STEP

    def body(w_ref, g_ref, m_ref, v_ref, s_ref, go_ref, d_ref, m2_ref, v2_ref, all_ref, sum_ref,
             x_scr, out_scr, send_sems, recv_sems, local_sem):
        i = pl.program_id(0)
        start, finish = _allgather_parts(x_scr, out_scr, send_sems, recv_sems, local_sem)

        @pl.when(i == 0)
        def _():
            x_scr[...] = s_ref[...]
            start()

        for k in range(2):
            cs = slice(k * D, (k + 1) * D)
            g = g_ref[k]
            go_ref[:, cs] = g
            m2 = ADAM_B1 * m_ref[:, cs] + (1.0 - ADAM_B1) * g
            v2 = ADAM_B2 * v_ref[:, cs] + (1.0 - ADAM_B2) * (g * g)
            m2_ref[:, cs] = m2
            v2_ref[:, cs] = v2
            d_ref[:, cs] = -ADAM_LR * ((m2 / c1) / (jnp.sqrt(v2 / c2) + ADAM_EPS) + ADAM_WD * w_ref[:, cs])

        @pl.when(i == nsteps - 1)
        def _():
            finish()
            all_ref[...] = out_scr[...]
            acc = out_scr[0:mr, :]
            for d in range(1, N_DEV):
                acc = acc + out_scr[d * mr:(d + 1) * mr, :]
            sum_ref[...] = acc

    tile = BS((tr, C), lambda i: (i, 0))
    return pl.pallas_call(
        body, name=name, grid=(nsteps,),
        out_shape=(SDS((R, C), F32),) * 4 + (SDS((N_DEV * mr, D), F32), SDS((mr, D), F32)),
        in_specs=[tile, BS((2, tr, D), lambda i: (0, i, 0)), tile, tile, BS((mr, D), lambda i: (0, 0))],
        out_specs=(tile,) * 4 + (BS((N_DEV * mr, D), lambda i: (0, 0)), BS((mr, D), lambda i: (0, 0))),
        scratch_shapes=[pltpu.VMEM((mr, D), F32), pltpu.VMEM((N_DEV * mr, D), F32)] + _allgather_sems(),
        compiler_params=_params("arbitrary"),
    )(w, g, m, v, srows)


def _rs_begin(g, c_idx, tag):
    n = g.shape[1]
    g = g.reshape(N_CHIPS, 2, n // 2, D)
    return _add_halves_call(g, _sibling_halves_call(g, tag), c_idx, tag)


def _rs_end(part, recv, c_idx, chip_idx, tag):
    n = 2 * part.shape[1]
    full = _add_chips_call(part, recv, jnp.concatenate([chip_idx, c_idx]), tag)
    return _sibling_join_call(full, tag).reshape(n, D)


def _local_step(x, mod, cact, target, wg, pack, small, c_idx, chip_idx):
    p, h1, wg = _fwd_in_call(x, mod, small["pre_tm"], wg, small["b_in"], pack, small["order"])
    o, oa, st, wg = _hgrn_fwd_call(p, small["logits"], small["hg_norm"], wg, pack)
    u, uc, cb, wg = _conv_fwd_call(p, small["conv_dw"], small["conv_db"], small["ln_g"], small["ln_b"], wg, pack)
    ya, yb, mg, y, x2, h2 = _merge_fwd_call(oa, cb, p, x, mod, small["post_tm"], small["pre_cm"], wg)
    z, da, dy2, dx2, acc_f = _ffn_call(h2, x2, target, mod, small["post_cm"], small["pre_cm"], wg)

    g_ff = _wgrad_call(None, h2, da, "wgrad_ff1", D, lambda i, j: (j, 0), 2 * R_FF)
    g_ff = _wgrad_call(g_ff, z, dy2, "wgrad_ff2", D, lambda i, j: (i, 1), 2 * R_FF)
    g_ff = g_ff.reshape(N_CHIPS, 2, R_FF, D)
    dy, dya, dyb, doa, dcb, dp_gt, acc_m, bs_gt, hr_ff = _merge_bwd_call(dx2, y, ya, yb, p, mod, small["post_tm"],
                                                                        wg, g_ff)
    part_ff = _add_halves_call(g_ff, hr_ff, c_idx, "ff")

    g_br = _wgrad_rows_call(None, oa, dya, "wgrad_br_a", 0)
    g_br = _wgrad_rows_call(g_br, cb, dyb, "wgrad_br_b", 1)
    g_br = _wgrad_rows_call(g_br, mg, dy, "wgrad_out", 2)
    g_br = g_br.reshape(N_CHIPS, 2, 3 * R_BR // 2, D)
    dp_hg, bs_hg, dlg, dgn, recv_ff, hr_br = _hgrn_bwd_call(p, o, doa, st, small["logits"], small["hg_norm"],
                                                            part_ff, g_br)
    part_br = _add_halves_call(g_br, hr_br, c_idx, "br")
    dp_cv, bs_cv, ddw, acc_c, recv_br = _conv_bwd_call(dcb, uc, u, p, small["conv_dw"], small["ln_g"], small["ln_b"],
                                                        part_br)

    g_in = _wgrad_call(None, h1, dp_hg, "wgrad_in_hg", D, lambda i, j: (j // 2, j % 2), R_IN)
    g_in = _wgrad_call(g_in, h1, dp_cv, "wgrad_in_cv", D, lambda i, j: (2, j), R_IN)
    g_in = _wgrad_call(g_in, h1, dp_gt, "wgrad_in_gt", D, lambda i, j: (3, j), R_IN)
    part_in = _rs_begin(g_in, c_idx, "in")
    chip_c = jnp.concatenate([chip_idx, c_idx])
    full_ff = _add_chips_call(part_ff, recv_ff, chip_c, "ff")
    full_br = _add_chips_call(part_br, recv_br, chip_c, "br")
    gx, acc_i, recv_in, full_ff, full_br = _in_bwd_call(dp_hg, dp_cv, dp_gt, x, dx2, mod, small["pre_tm"], wg,
                                                        part_in, full_ff, full_br)
    red_ff = full_ff.reshape(2 * R_FF, D)
    red_br = full_br.reshape(3 * R_BR, D)
    red_in = _rs_end(part_in, recv_in, c_idx, chip_idx, "in")

    zrow = jnp.zeros((1, D), F32)
    rows = [acc_i[0:1], acc_i[1:2], acc_m[0:1], acc_f[2:3], acc_f[3:4], acc_f[0:1],
            acc_i[2:3], acc_m[1:2], acc_f[4:5], acc_f[1:2],
            jnp.concatenate([bs_hg, bs_cv, bs_gt], axis=1).reshape(8, D),
            dlg, dgn, acc_c[0:1], acc_c[1:2], acc_c[2:3],
            ddw,
            cact, acc_f[5:6]] + [zrow] * 6
    return gx, jnp.concatenate(rows, axis=0), red_in, red_ff, red_br


def kernel(x, c, w_ada, b_ada, pre_norm_tm, post_norm_tm, pre_norm_cm, post_norm_cm, w_in, b_in, hg_lb_logits, hg_norm, conv_dw, conv_db, conv_ln_g, conv_ln_b, w_br_a, w_br_b, w_out, w_ff1, w_ff2, loss_target, m_w_ada, m_b_ada, m_pre_norm_tm, m_post_norm_tm, m_pre_norm_cm, m_post_norm_cm, m_w_in, m_b_in, m_hg_lb_logits, m_hg_norm, m_conv_dw, m_conv_db, m_conv_ln_g, m_conv_ln_b, m_w_br_a, m_w_br_b, m_w_out, m_w_ff1, m_w_ff2, v_w_ada, v_b_ada, v_pre_norm_tm, v_post_norm_tm, v_pre_norm_cm, v_post_norm_cm, v_w_in, v_b_in, v_hg_lb_logits, v_hg_norm, v_conv_dw, v_conv_db, v_conv_ln_g, v_conv_ln_b, v_w_br_a, v_w_br_b, v_w_out, v_w_ff1, v_w_ff2):
    xi, yi, ci = lax.axis_index("x"), lax.axis_index("y"), lax.axis_index("c")
    chip = 2 * xi + yi
    c_idx = jnp.reshape(ci, (1,)).astype(jnp.int32)
    chip_idx = jnp.reshape(chip, (1,)).astype(jnp.int32)

    w_in_halves = w_in[0].reshape(D, 2, D).transpose(1, 0, 2).reshape(R_IN, D)
    pack = jnp.concatenate([w_in_halves, w_ff1[0], w_ff2[0], w_br_a[0], w_br_b[0], w_out[0]],
                           axis=0).astype(BF16)
    wg = lax.dynamic_update_slice(lax.empty((N_CHIPS, PACK_W, D), BF16), pack[None], (chip, 0, 0))
    wa = 6 * D // N_CHIPS
    me = 4 * xi + 2 * yi + ci
    dw_blk = jnp.concatenate([conv_dw[0].reshape(-1), jnp.zeros((8 * D - CONV_K * 256,), F32)]).reshape(8, D)
    dw_all, ca_all, mod_all = _prologue_call(
        dw_blk, jnp.broadcast_to(c, (8, D)), w_ada[0].astype(BF16),
        lax.dynamic_slice_in_dim(b_ada, chip * wa, wa, axis=1))
    order = jnp.stack([chip, 2 * (1 - xi) + yi, 2 * xi + (1 - yi), 2 * (1 - xi) + (1 - yi)]).astype(jnp.int32)
    dw_all = dw_all.reshape(N_CHIPS, 2, 8 * D)[:, 0, :CONV_K * 256].reshape(N_CHIPS, CONV_K, 256)
    dw_full = dw_all.transpose(1, 0, 2).reshape(CONV_K, D)
    cact = lax.dynamic_slice_in_dim(ca_all, me * 8, 1, axis=0)
    mod_mine = lax.dynamic_index_in_dim(mod_all.reshape(N_CHIPS, 2, N_DEV, wa)[:, 0], me, axis=1,
                                        keepdims=False)
    mod = mod_mine.reshape(1, 6 * D)

    small = dict(pre_tm=pre_norm_tm, post_tm=post_norm_tm, pre_cm=pre_norm_cm, post_cm=post_norm_cm,
                 b_in=b_in, logits=hg_lb_logits, hg_norm=hg_norm, conv_dw=dw_full, conv_db=conv_db,
                 ln_g=conv_ln_g, ln_b=conv_ln_b, order=order)

    gx, srows, red_in, red_ff, red_br = _local_step(x[0], mod, cact, loss_target[0], wg, pack, small, c_idx,
                                                    chip_idx)

    shapes = {"in": w_in.shape, "br_a": w_br_a.shape, "br_b": w_br_b.shape, "out": w_out.shape,
              "ff1": w_ff1.shape, "ff2": w_ff2.shape}
    offs = {"in": (red_in, 0, R_IN), "ff1": (red_ff, 0, R_FF), "ff2": (red_ff, R_FF, 2 * R_FF),
            "br_a": (red_br, 0, R_BR), "br_b": (red_br, R_BR, 2 * R_BR), "out": (red_br, 2 * R_BR, 3 * R_BR)}
    wmv = {"in": (w_in, m_w_in, v_w_in), "br_a": (w_br_a, m_w_br_a, v_w_br_a), "br_b": (w_br_b, m_w_br_b, v_w_br_b),
           "out": (w_out, m_w_out, v_w_out), "ff1": (w_ff1, m_w_ff1, v_w_ff1), "ff2": (w_ff2, m_w_ff2, v_w_ff2)}
    res = {}
    for n in ("in", "ff1", "ff2"):
        shp = shapes[n]
        g2d = offs[n][0][offs[n][1]:offs[n][2]]
        w_, m_, v_ = (a[0] for a in wmv[n])
        if n == "in":
            g2d, d_, m2_, v2_, sall, ssum = _adamw_gather_call(w_, g2d.reshape(2, D, D), m_, v_, srows, "adamw_in")
        else:
            g2d = g2d.reshape(shp[1], shp[2])
            d_, m2_, v2_ = _adamw_call(w_, g2d, m_, v_, "adamw_" + n)
        res[n] = tuple(a.reshape(shp) for a in (g2d, d_, m2_, v2_))
    trio = ("br_a", "br_b", "out")
    d3, m3, v3 = _adamw_rows_call([wmv[n][0][0] for n in trio], red_br, [wmv[n][1][0] for n in trio],
                                  [wmv[n][2][0] for n in trio], "adamw_br")
    for j, n in enumerate(trio):
        res[n] = tuple(a.reshape(shapes[n]) for a in (red_br[j * R_BR:(j + 1) * R_BR], d3[j], m3[j], v3[j]))

    sall = sall.reshape(N_DEV, SMALL_ROWS, D)
    loss = jnp.sum(ssum[57])
    dmod_all = sall[:, 0:6, :].reshape(N_DEV, 6 * D)
    g_ada = _outer_call(sall[:, 56, :], lax.dynamic_slice_in_dim(dmod_all, chip * wa, wa, axis=1))
    g_dw = lax.dynamic_slice_in_dim(ssum[24:24 + CONV_K], chip * 256, 256, axis=1)
    d_, m2_, v2_ = _adamw_call(w_ada[0], g_ada, m_w_ada[0], v_w_ada[0], "adamw_ada")
    res["ada"] = tuple(a.reshape(w_ada.shape) for a in (g_ada, d_, m2_, v2_))

    names = ["b_ada", "pre_tm", "post_tm", "pre_cm", "post_cm", "b_in", "logits", "hg_norm", "conv_db", "ln_g", "ln_b",
             "conv_dw"]
    row0 = [0, 6, 7, 8, 9, 10, 18, 20, 21, 22, 23, None]
    sres = _adamw_small_call(
        [b_ada, pre_norm_tm, post_norm_tm, pre_norm_cm, post_norm_cm, b_in, hg_lb_logits, hg_norm, conv_db,
         conv_ln_g, conv_ln_b, conv_dw[0]],
        [m_b_ada, m_pre_norm_tm, m_post_norm_tm, m_pre_norm_cm, m_post_norm_cm, m_b_in, m_hg_lb_logits, m_hg_norm,
         m_conv_db, m_conv_ln_g, m_conv_ln_b, m_conv_dw[0]],
        [v_b_ada, v_pre_norm_tm, v_post_norm_tm, v_pre_norm_cm, v_post_norm_cm, v_b_in, v_hg_lb_logits, v_hg_norm,
         v_conv_db, v_conv_ln_g, v_conv_ln_b, v_conv_dw[0]], row0, ssum, g_dw)
    for nm, r4 in zip(names, sres):
        res[nm] = tuple(a.reshape(conv_dw.shape) for a in r4) if nm == "conv_dw" else r4

    order = ["ada", "b_ada", "pre_tm", "post_tm", "pre_cm", "post_cm", "in", "b_in", "logits", "hg_norm", "conv_dw",
             "conv_db", "ln_g", "ln_b", "br_a", "br_b", "out", "ff1", "ff2"]
    outs = [loss, gx.reshape(x.shape)]
    for kind in range(4):
        outs.extend(res[n][kind] for n in order)
    return tuple(outs)
```

```python
import jax
import jax.numpy as jnp
from jax import lax
from jax.experimental import pallas as pl
from jax.experimental.pallas import tpu as pltpu

F32, BF16 = jnp.float32, jnp.bfloat16
SDS = jax.ShapeDtypeStruct
BS = pl.BlockSpec
MESH = pl.DeviceIdType.MESH
HI = lax.Precision.HIGHEST

D = 1024
D_FF = 4096
IN_COLS = 8192
HEADS, DK = 8, 128
CHUNK = 128
CONV_K = 31
HALO = 32
SUB = 32
EPS = 1e-6
N_CHIPS, N_DEV = 4, 8
TM = 256
TB = 256
VMEM_LIMIT = 56 * 1024 * 1024

R_IN, R_BR, R_FF = 2048, 256, 1024
PACK_W = R_IN + 3 * R_BR + 2 * R_FF
O_IN, O_FF1, O_FF2, O_BRA, O_BRB, O_OUT = 0, 2048, 3072, 4096, 4352, 4608
SMALL_ROWS = 64

ADAM_LR, ADAM_B1, ADAM_B2, ADAM_EPS, ADAM_WD, ADAM_STEP = 0.001, 0.9, 0.999, 1e-08, 0.01, 10

NN = (((1,), (0,)), ((), ()))
NT = (((1,), (1,)), ((), ()))
TN = (((0,), (0,)), ((), ()))


def _mm(a, b, dims=NN, precision=None):
    return lax.dot_general(a, b, dims, preferred_element_type=F32, precision=precision)


def _sig(v):
    return jax.nn.sigmoid(v)


def _dsilu(v, s):
    return s * (1.0 + v * (1.0 - s))


def _params(*sem):
    return pltpu.CompilerParams(dimension_semantics=sem if sem else None, vmem_limit_bytes=VMEM_LIMIT)


def _rowsum(v):
    return jnp.sum(v, axis=0, keepdims=True)


def _mesh_pos():
    return lax.axis_index("x"), lax.axis_index("y"), lax.axis_index("c")


def _allgather_parts(x_ref, out_ref, send_sems, recv_sems, local_sem):
    m_per = x_ref.shape[0]
    x, y, c = _mesh_pos()
    me, sibling = (x, y, c), (x, y, 1 - c)
    chips = [(1 - x, y), (x, 1 - y), (1 - x, 1 - y)]

    def rows(px, py, pc):
        return out_ref.at[pl.ds((4 * px + 2 * py + pc) * m_per, m_per), :]

    def copy(k, block, to, src=None):
        return pltpu.make_async_remote_copy(
            src_ref=rows(*block) if src is None else src, dst_ref=rows(*block),
            send_sem=send_sems.at[k], recv_sem=recv_sems.at[k], device_id=to, device_id_type=MESH)

    def first():
        return [copy(0, me, sibling, src=x_ref)] + [copy(1 + j, me, (*chip, c), src=x_ref)
                                                    for j, chip in enumerate(chips)]

    def start():
        pltpu.make_async_copy(x_ref, rows(*me), local_sem).start()
        for cp in first():
            cp.start()

    def finish():
        passed = [copy(4 + j, (*chip, c), sibling) for j, chip in enumerate(chips)]
        for j, chip in enumerate(chips):
            copy(1 + j, (*chip, c), me).wait_recv()
            passed[j].start()
        copy(0, sibling, me).wait_recv()
        for j, chip in enumerate(chips):
            copy(4 + j, (*chip, 1 - c), me).wait_recv()
        for cp in first() + passed:
            cp.wait_send()
        pltpu.make_async_copy(x_ref, rows(*me), local_sem).wait()

    return start, finish


def _allgather_sems():
    return [pltpu.SemaphoreType.DMA((7,)), pltpu.SemaphoreType.DMA((7,)), pltpu.SemaphoreType.DMA]
def _gather_sems(n_ranges):
    return [pltpu.SemaphoreType.DMA((6 * n_ranges,)), pltpu.SemaphoreType.DMA((6 * n_ranges,))]


def _pack_gather(pack_ref, wg_ref, send_sems, recv_sems, ranges):
    x, y, c = _mesh_pos()
    me, sibling = (x, y, c), (x, y, 1 - c)
    chips = [(1 - x, y), (x, 1 - y), (1 - x, 1 - y)]

    def land(r, px, py, pc):
        off, n = ranges[r]
        return wg_ref.at[2 * px + py, pl.ds(off + pc * (n // 2), n // 2), :]

    def mine(r):
        off, n = ranges[r]
        return pack_ref.at[pl.ds(off + c * (n // 2), n // 2), :]

    def copy(r, k, block, to, src=None):
        return pltpu.make_async_remote_copy(
            src_ref=land(r, *block) if src is None else src, dst_ref=land(r, *block),
            send_sem=send_sems.at[6 * r + k], recv_sem=recv_sems.at[6 * r + k], device_id=to, device_id_type=MESH)

    def start():
        for r in range(len(ranges)):
            for j, chip in enumerate(chips):
                copy(r, j, me, (*chip, c), src=mine(r)).start()

    def finish():
        for r in range(len(ranges)):
            for j, chip in enumerate(chips):
                copy(r, j, (*chip, c), me).wait_recv()
                copy(r, 3 + j, (*chip, c), sibling).start()
        for r in range(len(ranges)):
            for j, chip in enumerate(chips):
                copy(r, 3 + j, (*chip, 1 - c), me).wait_recv()
        for r in range(len(ranges)):
            for j, chip in enumerate(chips):
                copy(r, j, me, (*chip, c), src=mine(r)).wait_send()
                copy(r, 3 + j, (*chip, c), sibling).wait_send()

    return start, finish


def _relay_sems():
    return [pltpu.SemaphoreType.DMA((8,)), pltpu.SemaphoreType.DMA((8,))]


def _relay_gather(pack_ref, wg_ref, send_sems, recv_sems, off, n):
    x, y, c = _mesh_pos()
    me, sibling = (x, y, c), (x, y, 1 - c)
    chips = [(1 - x, y), (x, 1 - y), (1 - x, 1 - y)]
    h, q = n // 2, n // 4

    def land(px, py, pc, piece=None):
        if piece is None:
            return wg_ref.at[2 * px + py, pl.ds(off + pc * h, h), :]
        return wg_ref.at[2 * px + py, pl.ds(off + pc * h + piece * q, q), :]

    def copy(k, ref, to, src=None):
        return pltpu.make_async_remote_copy(
            src_ref=ref if src is None else src, dst_ref=ref, send_sem=send_sems.at[k], recv_sem=recv_sems.at[k],
            device_id=to, device_id_type=MESH)

    def direct(j):
        return copy(j, land(x, y, c), (*chips[j], c), src=pack_ref.at[pl.ds(off + c * h, h), :])

    def relayed(j):
        if j == 0:
            return copy(6, land(*chips[0], c, 1), (x, 1 - y, c))
        return copy(7, land(*chips[1], c, 0), (1 - x, y, c))

    def start():
        direct(0).start()
        direct(1).start()

    def arrive(j):
        if j == 0:
            for k in range(2):
                copy(k, land(*chips[k], c), me).wait_recv()
                relayed(k).start()
                copy(3 + k, land(*chips[k], c), sibling).start()
        if j == 2:
            copy(7, land(*chips[2], c, 0), me).wait_recv()
            copy(6, land(*chips[2], c, 1), me).wait_recv()
            copy(5, land(*chips[2], c), sibling).start()
        copy(3 + j, land(*chips[j], 1 - c), me).wait_recv()

    def drain():
        for j in range(2):
            direct(j).wait_send()
            relayed(j).wait_send()
        for j in range(3):
            copy(3 + j, land(*chips[j], c), sibling).wait_send()

    return start, arrive, drain


def _prologue_call(dw_blk, c_blk, w_ada, b_ada):
    wa = w_ada.shape[1]

    def body(dw_ref, c_ref, wa_ref, ba_ref, dwg_ref, ca_ref, modg_ref,
             cg_scr, part_scr, s1, r1, l1, s2, r2, l2, s3, r3, l3):
        start_c, finish_c = _allgather_parts(c_ref, cg_scr, s2, r2, l2)
        start_dw, finish_dw = _allgather_parts(dw_ref, dwg_ref, s1, r1, l1)
        start_mod, finish_mod = _allgather_parts(part_scr, modg_ref, s3, r3, l3)
        start_c()
        start_dw()
        finish_c()
        cv = cg_scr[...]
        ca = cv * _sig(cv)
        ca_ref[...] = ca
        pick = (lax.broadcasted_iota(jnp.int32, (N_DEV, N_DEV * 8), 1)
                == 8 * lax.broadcasted_iota(jnp.int32, (N_DEV, N_DEV * 8), 0)).astype(BF16)
        ca8 = _mm(pick, ca.astype(BF16)).astype(BF16)
        part_scr[...] = _mm(ca8, wa_ref[...]) + ba_ref[...]
        start_mod()
        finish_dw()
        finish_mod()

    vm = BS(memory_space=pltpu.VMEM)
    return pl.pallas_call(
        body, name="prologue_adaln_conv_dw",
        out_shape=(SDS((N_DEV * 8, D), F32), SDS((N_DEV * 8, D), F32), SDS((N_DEV * N_DEV, wa), F32)),
        in_specs=[vm, vm, vm, vm], out_specs=(vm, vm, vm),
        scratch_shapes=[pltpu.VMEM((N_DEV * 8, D), F32), pltpu.VMEM((N_DEV, wa), F32)]
        + _allgather_sems() + _allgather_sems() + _allgather_sems(),
        compiler_params=pltpu.CompilerParams(vmem_limit_bytes=VMEM_LIMIT),
    )(dw_blk, c_blk, w_ada, b_ada)


def _halves_exchange(g_ref, out_ref, send_sems, recv_sems):
    x, y, c = _mesh_pos()

    def copies():
        return [pltpu.make_async_remote_copy(
            src_ref=g_ref.at[k, 1 - c], dst_ref=out_ref.at[k], send_sem=send_sems.at[k], recv_sem=recv_sems.at[k],
            device_id=(x, y, 1 - c), device_id_type=MESH) for k in range(N_CHIPS)]

    def start():
        for cp in copies():
            cp.start()

    def finish():
        for cp in copies():
            cp.wait()

    return start, finish


def _halves_sems():
    return [pltpu.SemaphoreType.DMA((N_CHIPS,)), pltpu.SemaphoreType.DMA((N_CHIPS,))]


def _sibling_halves_call(g, tag):
    _, _, h, n = g.shape

    def body(g_ref, out_ref, send_sems, recv_sems):
        start, finish = _halves_exchange(g_ref, out_ref, send_sems, recv_sems)
        start()
        finish()

    return pl.pallas_call(
        body, name="rs_sibling_halves_" + tag, out_shape=SDS((N_CHIPS, h, n), g.dtype),
        in_specs=[BS(memory_space=pl.ANY)], out_specs=BS(memory_space=pl.ANY),
        scratch_shapes=_halves_sems(),
    )(g)


def _chip_exchange(p_ref, out_ref, send_sems, recv_sems):
    x, y, c = _mesh_pos()
    chips = [(1 - x, y), (x, 1 - y), (1 - x, 1 - y)]

    def copies():
        return [pltpu.make_async_remote_copy(
            src_ref=p_ref.at[2 * cx + cy], dst_ref=out_ref.at[j], send_sem=send_sems.at[j], recv_sem=recv_sems.at[j],
            device_id=(cx, cy, c), device_id_type=MESH) for j, (cx, cy) in enumerate(chips)]

    def start():
        for cp in copies():
            cp.start()

    def finish():
        for cp in copies():
            cp.wait()

    return start, finish


def _exchange_sems():
    return [pltpu.SemaphoreType.DMA((3,)), pltpu.SemaphoreType.DMA((3,))]


def _join_exchange(in_ref, out_ref, send_sems, recv_sems):
    h = in_ref.shape[1]
    q = h // 4
    x, y, c = _mesh_pos()

    def copy(k, half):
        return pltpu.make_async_remote_copy(
            src_ref=in_ref.at[half, pl.ds(k * q, q)], dst_ref=out_ref.at[half, pl.ds(k * q, q)],
            send_sem=send_sems.at[k], recv_sem=recv_sems.at[k],
            device_id=(x, y, 1 - c), device_id_type=MESH)

    def start():
        for k in range(4):
            copy(k, c).start()

    def finish():
        for k in range(4):
            copy(k, c).wait_send()
            copy(k, 1 - c).wait_recv()

    return start, finish


def _join_sems():
    return [pltpu.SemaphoreType.DMA((4,)), pltpu.SemaphoreType.DMA((4,))]


def _join_gather_call(full, srows, tag):
    mr = srows.shape[0]

    def body(in_ref, s_ref, out_ref, all_ref, sum_ref, send_sems, recv_sems, gs, gr, gl):
        start, finish = _join_exchange(in_ref, out_ref, send_sems, recv_sems)
        start_g, finish_g = _allgather_parts(s_ref, all_ref, gs, gr, gl)
        start()
        start_g()
        finish_g()
        acc = all_ref[0:mr, :]
        for d in range(1, N_DEV):
            acc = acc + all_ref[d * mr:(d + 1) * mr, :]
        sum_ref[...] = acc
        finish()

    hbm = BS(memory_space=pl.ANY)
    vm = BS(memory_space=pltpu.VMEM)
    return pl.pallas_call(
        body, name="rs_sibling_join_" + tag,
        out_shape=(SDS(full.shape, full.dtype), SDS((N_DEV * mr, D), F32), SDS((mr, D), F32)),
        in_specs=[hbm, vm], out_specs=(hbm, vm, vm),
        scratch_shapes=_join_sems() + _allgather_sems(), input_output_aliases={0: 0},
        compiler_params=pltpu.CompilerParams(vmem_limit_bytes=VMEM_LIMIT),
    )(full, srows)


def _add_halves_call(g, recv, c_idx, tag):
    _, _, h, n = g.shape
    tr = h // 2

    def body(c_ref, g_ref, r_ref, o_ref):
        o_ref[...] = (g_ref[...].astype(F32) + r_ref[...].astype(F32)).astype(BF16)

    return pl.pallas_call(
        body, name="rs_add_halves_" + tag, out_shape=SDS((N_CHIPS, h, n), BF16),
        grid_spec=pltpu.PrefetchScalarGridSpec(
            num_scalar_prefetch=1, grid=(N_CHIPS, 2),
            in_specs=[BS((None, None, tr, n), lambda k, r, c_ref: (k, c_ref[0], r, 0)),
                      BS((None, tr, n), lambda k, r, c_ref: (k, r, 0))],
            out_specs=BS((None, tr, n), lambda k, r, c_ref: (k, r, 0))),
        compiler_params=_params("arbitrary", "arbitrary"),
    )(c_idx, g, recv)


def _add_chips_call(p, recv, chip_c_idx, tag):
    _, h, n = p.shape
    tr = h // 2

    def body(k_ref, p_ref, r_ref, o_ref):
        acc = p_ref[...].astype(F32)
        for j in range(3):
            acc = acc + r_ref[j].astype(F32)
        o_ref[...] = acc

    return pl.pallas_call(
        body, name="rs_add_chips_" + tag, out_shape=SDS((2, h, n), F32),
        grid_spec=pltpu.PrefetchScalarGridSpec(
            num_scalar_prefetch=1, grid=(2,),
            in_specs=[BS((None, tr, n), lambda r, k_ref: (k_ref[0], r, 0)),
                      BS((3, tr, n), lambda r, k_ref: (0, r, 0))],
            out_specs=BS((None, tr, n), lambda r, k_ref: (k_ref[1], r, 0))),
        compiler_params=_params("arbitrary"),
    )(chip_c_idx, p, recv)


def _load_rows(wg_hbm, w_vmem, sem, off):
    cp = pltpu.make_async_copy(wg_hbm.at[:, pl.ds(off, w_vmem.shape[1]), :], w_vmem, sem)
    cp.start()
    return cp


def _fwd_in_call(x, mod, pre_tm, wg, b_in, pack, order):
    S = x.shape[0]
    tmf = 2 * TM
    nt = S // tmf
    wc = IN_COLS // N_CHIPS

    def body(ord_ref, x_ref, mod_ref, g_ref, w_hbm, b_ref, pack_ref, p_ref, h_hbm, wg_out, w_vmem, h_scr, sems,
             send_sems, recv_sems, send_sems2, recv_sems2):
        q, i = pl.program_id(0), pl.program_id(1)
        rows = pl.ds(pl.multiple_of(i * tmf, tmf), tmf)
        start, arrive, drain = _relay_gather(pack_ref, wg_out, send_sems, recv_sems, O_IN, R_IN)
        start2, finish2 = _pack_gather(pack_ref, wg_out, send_sems2, recv_sems2, [(O_OUT, R_BR)])

        def weights(phase):
            return pltpu.make_async_copy(wg_out.at[ord_ref[phase], pl.ds(O_IN, R_IN), :], w_vmem.at[phase % 2],
                                         sems.at[phase % 2])

        @pl.when((q == 0) & (i == 0))
        def _():
            start()
            weights(0).start()
            weights(0).wait()

        @pl.when((q == 1) & (i == 0))
        def _():
            arrive(0)
            start2()
            weights(1).start()
            weights(1).wait()
            arrive(1)
            weights(2).start()

        @pl.when((q == 2) & (i == 0))
        def _():
            weights(2).wait()
            arrive(2)
            weights(3).start()

        @pl.when((q == 3) & (i == 0))
        def _():
            weights(3).wait()

        @pl.when(q == 0)
        def _():
            xv = x_ref[...]
            r = lax.rsqrt(jnp.mean(xv * xv, axis=-1, keepdims=True) + EPS)
            h = xv * r * g_ref[...] * (1.0 + mod_ref[:, D:2 * D]) + mod_ref[:, 0:D]
            h_scr[rows, :] = h.astype(BF16)

        hb = h_scr[rows, :]
        slot = q % 2
        for k in range(wc // D):
            p_ref[:, k * D:(k + 1) * D] = _mm(hb, w_vmem[slot, k * D:(k + 1) * D, :]) + b_ref[:, k * D:(k + 1) * D]

        @pl.when((q == N_CHIPS - 1) & (i == nt - 1))
        def _():
            cp = pltpu.make_async_copy(h_scr, h_hbm, sems.at[0])
            cp.start()
            drain()
            finish2()
            cp.wait()

    hbm = BS(memory_space=pl.ANY)
    return pl.pallas_call(
        body, name="fwd_in", out_shape=(SDS((S, IN_COLS), F32), SDS((S, D), BF16), SDS(wg.shape, wg.dtype)),
        grid_spec=pltpu.PrefetchScalarGridSpec(
            num_scalar_prefetch=1, grid=(N_CHIPS, nt),
            in_specs=[BS((tmf, D), lambda q, i, o: (jnp.where(q == 0, i, nt - 1), 0)),
                      BS((1, 6 * D), lambda q, i, o: (0, 0)),
                      BS((1, D), lambda q, i, o: (0, 0)), hbm, BS((1, wc), lambda q, i, o: (0, o[q])), hbm],
            out_specs=(BS((tmf, wc), lambda q, i, o: (i, o[q])), hbm, hbm),
            scratch_shapes=[pltpu.VMEM((2, R_IN, D), BF16), pltpu.VMEM((S, D), BF16), pltpu.SemaphoreType.DMA((2,))]
            + _relay_sems() + _gather_sems(1)),
        input_output_aliases={4: 2},
        compiler_params=_params("arbitrary", "arbitrary"),
    )(order, x, mod, pre_tm, wg, b_in, pack)


def _lower_bound(lg_ref):
    l0, l1 = lg_ref[0:1, :], lg_ref[1:2, :]
    mx = jnp.maximum(l0, l1)
    e0, e1 = jnp.exp(l0 - mx), jnp.exp(l1 - mx)
    return e0 / (e0 + e1)


def _tri_masks():
    ri = lax.broadcasted_iota(jnp.int32, (CHUNK, CHUNK), 0)
    ci = lax.broadcasted_iota(jnp.int32, (CHUNK, CHUNK), 1)
    return (ri >= ci).astype(F32), (ci >= ri).astype(F32)


def _cumsum_mm(tri, g):
    tb = tri.astype(BF16)
    hi = g.astype(BF16)
    r1 = g - hi.astype(F32)
    mid = r1.astype(BF16)
    lo = (r1 - mid.astype(F32)).astype(BF16)
    return _mm(tb, hi) + _mm(tb, mid) + _mm(tb, lo)


def _hg_gates(q_r, f_r, lb, tril):
    sq = _sig(q_r)
    q = q_r * sq
    sf = _sig(f_r)
    f = lb + (1.0 - lb) * sf
    k = 1.0 - f
    g = jnp.log(f)
    b = _cumsum_mm(tril, g)
    b_last = _rowsum(g)
    row = lax.broadcasted_iota(jnp.int32, g.shape, 0)
    ref = _rowsum(jnp.where(row < CHUNK // 2, g, 0.0))
    e = jnp.exp(b)
    eq = jnp.exp(jnp.minimum(b - ref, 80.0))
    ek = jnp.exp(jnp.minimum(ref - b, 80.0))
    dd = jnp.exp(b_last - b)
    return dict(sq=sq, q=q, sf=sf, f=f, k=k, e=e, eq=eq, ek=ek, dd=dd, elast=jnp.exp(b_last),
                qe=q * e, qt=q * eq, kt=k * ek, kd=k * dd)


def _hgrn_fwd_call(p, logits, gn, wg, pack):
    S = p.shape[0]
    ncb = TB // CHUNK
    ranges = [(O_FF1, R_FF)]

    def body(q_ref, f_ref, v_ref, og_ref, lg_ref, gn_ref, wg_in, pack_ref, o_ref, oa_ref, st_ref, wg_out,
             st_scr, send_sems, recv_sems):
        start, finish = _pack_gather(pack_ref, wg_out, send_sems, recv_sems, ranges)

        @pl.when(pl.program_id(0) == 0)
        def _():
            start()
            st_scr[...] = jnp.zeros_like(st_scr)

        lb = _lower_bound(lg_ref)
        tril, _ = _tri_masks()

        def chunk(ci, carry):
            rows = pl.ds(pl.multiple_of(ci * CHUNK, CHUNK), CHUNK)
            st_ref[ci] = st_scr[...]
            t = _hg_gates(q_ref[rows, :], f_ref[rows, :], lb, tril)
            v = v_ref[rows, :]
            for h in range(HEADS):
                sl = slice(h * DK, (h + 1) * DK)
                stp = st_scr[:, sl]
                vb = v[:, sl].astype(BF16)
                inter = _mm(t["qe"][:, sl].astype(BF16), stp.astype(BF16), NT)
                a = jnp.where(tril > 0.5, _mm(t["qt"][:, sl].astype(BF16), t["kt"][:, sl].astype(BF16), NT), 0.0)
                o = inter + _mm(a.astype(BF16), vb)
                st_scr[:, sl] = stp * t["elast"][:, sl] + _mm(vb, t["kd"][:, sl].astype(BF16), TN)
                oh = o * lax.rsqrt(jnp.mean(o * o, axis=-1, keepdims=True) + EPS)
                og = og_ref[rows, sl]
                o_ref[rows, sl] = o
                oa_ref[rows, sl] = (oh * gn_ref[:, sl] * (og * _sig(og))).astype(BF16)
            return carry

        lax.fori_loop(0, ncb, chunk, 0)

        @pl.when(pl.program_id(0) == S // TB - 1)
        def _():
            finish()

    col = lambda j: BS((TB, D), lambda i, j=j: (i, j))
    hbm = BS(memory_space=pl.ANY)
    return pl.pallas_call(
        body, name="hgrn_fwd", grid=(S // TB,),
        out_shape=(SDS((S, D), F32), SDS((S, D), BF16), SDS((S // CHUNK, DK, D), F32), SDS(wg.shape, wg.dtype)),
        in_specs=[col(0), col(1), col(2), col(3), BS((2, D), lambda i: (0, 0)), BS((1, D), lambda i: (0, 0)),
                  hbm, hbm],
        out_specs=(BS((TB, D), lambda i: (i, 0)), BS((TB, D), lambda i: (i, 0)),
                   BS((ncb, DK, D), lambda i: (i, 0, 0)), hbm),
        scratch_shapes=[pltpu.VMEM((DK, D), F32)] + _gather_sems(len(ranges)),
        input_output_aliases={6: 3},
        compiler_params=_params("arbitrary"),
    )(p, p, p, p, logits, gn, wg, pack)


def _layernorm_stats(uc):
    mu = jnp.mean(uc, axis=-1, keepdims=True)
    xc = uc - mu
    rs = lax.rsqrt(jnp.mean(xc * xc, axis=-1, keepdims=True) + EPS)
    return xc * rs, rs


EXT = HALO + TM + 8


def _fill_shifted(ext, shifted):
    for m in range(1, 8):
        shifted[m - 1] = ext[m:m + HALO + TM, :]


def _window(ext, shifted, s0, n):
    m = s0 % 8
    q = s0 - m
    return ext[q:q + n, :] if m == 0 else shifted[m - 1, q:q + n, :]


def _conv_fwd_call(p, dw, db, ln_g, ln_b, wg, pack):
    S = p.shape[0]
    ranges = [(O_FF2, R_FF), (O_BRA, 2 * R_BR)]

    def body(cv_ref, cg_ref, dw_ref, db_ref, g_ref, b_ref, wg_in, pack_ref, u_ref, uc_ref, cb_ref, wg_out,
             uext, ush, send_sems, recv_sems):
        start, finish = _pack_gather(pack_ref, wg_out, send_sems, recv_sems, ranges)

        @pl.when(pl.program_id(0) == 0)
        def _():
            start()
            uext[0:HALO, :] = jnp.zeros((HALO, D), F32)
            uext[HALO + TM:EXT, :] = jnp.zeros((EXT - HALO - TM, D), F32)

        u = cv_ref[...] * _sig(cg_ref[...])
        uext[HALO:HALO + TM, :] = u
        u_ref[...] = u
        _fill_shifted(uext, ush)
        for rb in range(TM // SUB):
            acc = jnp.broadcast_to(db_ref[...], (SUB, D))
            for j in range(CONV_K):
                s0 = HALO - (CONV_K - 1) + j + rb * SUB
                acc = acc + dw_ref[j:j + 1, :] * _window(uext, ush, s0, SUB)
            uc_ref[rb * SUB:(rb + 1) * SUB, :] = acc
            xh, _ = _layernorm_stats(acc)
            ln = xh * g_ref[...] + b_ref[...]
            cb_ref[rb * SUB:(rb + 1) * SUB, :] = (ln * _sig(ln)).astype(BF16)
        uext[0:HALO, :] = uext[TM:TM + HALO, :]

        @pl.when(pl.program_id(0) == S // TM - 1)
        def _():
            finish()

    vec = BS((1, D), lambda i: (0, 0))
    hbm = BS(memory_space=pl.ANY)
    return pl.pallas_call(
        body, name="conv_fwd", grid=(S // TM,),
        out_shape=(SDS((S, D), F32), SDS((S, D), F32), SDS((S, D), BF16), SDS(wg.shape, wg.dtype)),
        in_specs=[BS((TM, D), lambda i: (i, 4)), BS((TM, D), lambda i: (i, 5)),
                  BS((CONV_K, D), lambda i: (0, 0)), vec, vec, vec, hbm, hbm],
        out_specs=(BS((TM, D), lambda i: (i, 0)),) * 3 + (hbm,),
        scratch_shapes=[pltpu.VMEM((EXT, D), F32), pltpu.VMEM((7, HALO + TM, D), F32)] + _gather_sems(len(ranges)),
        input_output_aliases={6: 3},
        compiler_params=_params("arbitrary"),
    )(p, p, dw, db, ln_g, ln_b, wg, pack)


def _mm_rows(a, w_ref):
    acc = _mm(a[:, 0:R_BR], w_ref[0])
    for k in range(1, N_CHIPS):
        acc = acc + _mm(a[:, k * R_BR:(k + 1) * R_BR], w_ref[k])
    return acc


def _mm_rows_t(a, w_ref):
    return jnp.concatenate([_mm(a, w_ref[k], NT) for k in range(N_CHIPS)], axis=1)


def _br_spec(off):
    return BS((N_CHIPS, R_BR, D), lambda i: (0, off // R_BR, 0))


def _merge_fwd_call(oa, cb, p, x, mod, post_tm, pre_cm, wg):
    S = x.shape[0]

    def body(oa_ref, cb_ref, ga_ref, gb_ref, x_ref, mod_ref, post_ref, pre_ref, wa_ref, wb_ref, wo_ref,
             ya_ref, yb_ref, mg_ref, y_ref, x2_ref, h2_ref):
        ya = _mm_rows(oa_ref[...], wa_ref)
        yb = _mm_rows(cb_ref[...], wb_ref)
        ya_ref[...] = ya.astype(BF16)
        yb_ref[...] = yb.astype(BF16)
        mg = (_sig(ga_ref[...]) * ya + _sig(gb_ref[...]) * yb).astype(BF16)
        mg_ref[...] = mg
        y = _mm_rows(mg, wo_ref)
        y_ref[...] = y
        n = y * lax.rsqrt(jnp.mean(y * y, axis=-1, keepdims=True) + EPS) * post_ref[...]
        x2 = x_ref[...] + mod_ref[:, 2 * D:3 * D] * n
        x2_ref[...] = x2
        r2 = lax.rsqrt(jnp.mean(x2 * x2, axis=-1, keepdims=True) + EPS)
        h2 = x2 * r2 * pre_ref[...] * (1.0 + mod_ref[:, 4 * D:5 * D]) + mod_ref[:, 3 * D:4 * D]
        h2_ref[...] = h2.astype(BF16)

    tile = BS((TM, D), lambda i: (i, 0))
    vec = BS((1, D), lambda i: (0, 0))
    return pl.pallas_call(
        body, name="merge_fwd", grid=(S // TM,),
        out_shape=(SDS((S, D), BF16), SDS((S, D), BF16), SDS((S, D), BF16), SDS((S, D), F32), SDS((S, D), F32),
                   SDS((S, D), BF16)),
        in_specs=[tile, tile, BS((TM, D), lambda i: (i, 6)), BS((TM, D), lambda i: (i, 7)), tile,
                  BS((1, 6 * D), lambda i: (0, 0)), vec, vec, _br_spec(O_BRA), _br_spec(O_BRB), _br_spec(O_OUT)],
        out_specs=(tile,) * 6,
        compiler_params=_params("arbitrary"),
    )(oa, cb, p, p, x, mod, post_tm, pre_cm, wg, wg, wg)


def _ffn_call(h2, x2, target, mod, post_cm, pre_cm, wg):
    S = x2.shape[0]

    def body(h2_ref, x2_ref, t_ref, mod_ref, post_ref, pre_ref, w_hbm,
             z_ref, da_ref, dy2_ref, dx2_ref, acc_ref, w1_v, w2_v, ra_scr, sems):
        @pl.when(pl.program_id(0) == 0)
        def _():
            c1 = _load_rows(w_hbm, w1_v, sems.at[0], O_FF1)
            c2 = _load_rows(w_hbm, w2_v, sems.at[1], O_FF2)
            c1.wait()
            c2.wait()
            acc_ref[...] = jnp.zeros_like(acc_ref)

        h2 = h2_ref[...]
        for k in range(N_CHIPS):
            ra = jnp.maximum(_mm(h2, w1_v[k]), 0.0)
            ra_scr[:, k * D:(k + 1) * D] = ra
            z_ref[:, k * D:(k + 1) * D] = (ra * ra).astype(BF16)
        y2 = _mm(z_ref[:, 0:D], w2_v[0])
        for k in range(1, N_CHIPS):
            y2 = y2 + _mm(z_ref[:, k * D:(k + 1) * D], w2_v[k])
        ry = lax.rsqrt(jnp.mean(y2 * y2, axis=-1, keepdims=True) + EPS)
        yn = y2 * ry
        n = yn * post_ref[...]
        g2 = mod_ref[:, 5 * D:6 * D]
        x2 = x2_ref[...]
        err = x2 + g2 * n - t_ref[...]
        acc_ref[5:6, :] += _rowsum(err * err) * (0.5 / D)
        dout = err * (1.0 / D)
        acc_ref[0:1, :] += _rowsum(dout * n)
        dn = dout * g2
        acc_ref[1:2, :] += _rowsum(dn * yn)
        dyn = dn * post_ref[...]
        dy2 = (ry * (dyn - yn * jnp.mean(dyn * yn, axis=-1, keepdims=True))).astype(BF16)
        dy2_ref[...] = dy2
        for k in range(N_CHIPS):
            dz = _mm(dy2, w2_v[k], NT)
            da_ref[:, k * D:(k + 1) * D] = (dz * (2.0 * ra_scr[:, k * D:(k + 1) * D])).astype(BF16)
        dh2 = jnp.zeros((TM, D), F32)
        for k in range(N_CHIPS):
            dh2 = dh2 + _mm(da_ref[:, k * D:(k + 1) * D], w1_v[k], NT)
        r2 = lax.rsqrt(jnp.mean(x2 * x2, axis=-1, keepdims=True) + EPS)
        xn = x2 * r2
        yv = xn * pre_ref[...]
        acc_ref[2:3, :] += _rowsum(dh2)
        acc_ref[3:4, :] += _rowsum(dh2 * yv)
        dyv = dh2 * (1.0 + mod_ref[:, 4 * D:5 * D])
        acc_ref[4:5, :] += _rowsum(dyv * xn)
        dxn = dyv * pre_ref[...]
        dx2_ref[...] = dout + r2 * (dxn - xn * jnp.mean(dxn * xn, axis=-1, keepdims=True))

    tile = BS((TM, D), lambda i: (i, 0))
    wide = BS((TM, D_FF), lambda i: (i, 0))
    vec = BS((1, D), lambda i: (0, 0))
    return pl.pallas_call(
        body, name="ffn_fwd_bwd", grid=(S // TM,),
        out_shape=(SDS((S, D_FF), BF16), SDS((S, D_FF), BF16), SDS((S, D), BF16), SDS((S, D), F32),
                   SDS((8, D), F32)),
        in_specs=[tile, tile, tile, BS((1, 6 * D), lambda i: (0, 0)), vec, vec, BS(memory_space=pl.ANY)],
        out_specs=(wide, wide, tile, tile, BS((8, D), lambda i: (0, 0))),
        scratch_shapes=[pltpu.VMEM((N_CHIPS, R_FF, D), BF16), pltpu.VMEM((N_CHIPS, R_FF, D), BF16),
                        pltpu.VMEM((TM, D_FF), F32),
                        pltpu.SemaphoreType.DMA((2,))],
        compiler_params=_params("arbitrary"),
    )(h2, x2, target, mod, post_cm, pre_cm, wg)


def _merge_bwd_call(dx2, y, ya, yb, p, mod, post_tm, wg, g):
    S = y.shape[0]

    def body(dx2_ref, y_ref, ya_ref, yb_ref, ga_ref, gb_ref, mod_ref, post_ref, wa_ref, wb_ref, wo_ref, g_ref,
             dy_ref, dya_ref, dyb_ref, doa_ref, dcb_ref, dpg_ref, acc_ref, bsum_ref, hr_ref, send_sems, recv_sems):
        start, finish = _halves_exchange(g_ref, hr_ref, send_sems, recv_sems)

        @pl.when(pl.program_id(0) == 0)
        def _():
            start()
            acc_ref[...] = jnp.zeros_like(acc_ref)
            bsum_ref[...] = jnp.zeros_like(bsum_ref)

        y = y_ref[...]
        ry = lax.rsqrt(jnp.mean(y * y, axis=-1, keepdims=True) + EPS)
        yn = y * ry
        dx2 = dx2_ref[...]
        acc_ref[0:1, :] += _rowsum(dx2 * (yn * post_ref[...]))
        dn = dx2 * mod_ref[:, 2 * D:3 * D]
        acc_ref[1:2, :] += _rowsum(dn * yn)
        dyn = dn * post_ref[...]
        dy = (ry * (dyn - yn * jnp.mean(dyn * yn, axis=-1, keepdims=True))).astype(BF16)
        dy_ref[...] = dy
        dmg = _mm_rows_t(dy, wo_ref)
        sa, sb = _sig(ga_ref[...]), _sig(gb_ref[...])
        dya = (dmg * sa).astype(BF16)
        dyb = (dmg * sb).astype(BF16)
        dya_ref[...] = dya
        dyb_ref[...] = dyb
        dga = dmg * ya_ref[...].astype(F32) * (sa * (1.0 - sa))
        dgb = dmg * yb_ref[...].astype(F32) * (sb * (1.0 - sb))
        dpg_ref[:, 0:D] = dga.astype(BF16)
        dpg_ref[:, D:2 * D] = dgb.astype(BF16)
        bsum_ref[:, 0:D] += _rowsum(dga)
        bsum_ref[:, D:2 * D] += _rowsum(dgb)
        doa_ref[...] = _mm_rows_t(dya, wa_ref)
        dcb_ref[...] = _mm_rows_t(dyb, wb_ref)

        @pl.when(pl.program_id(0) == S // TM - 1)
        def _():
            finish()

    tile = BS((TM, D), lambda i: (i, 0))
    vec = BS((1, D), lambda i: (0, 0))
    return pl.pallas_call(
        body, name="merge_bwd", grid=(S // TM,),
        out_shape=(SDS((S, D), BF16), SDS((S, D), BF16), SDS((S, D), BF16), SDS((S, D), F32), SDS((S, D), F32),
                   SDS((S, 2 * D), BF16), SDS((8, D), F32), SDS((1, 2 * D), F32),
                   SDS((N_CHIPS,) + g.shape[2:], g.dtype)),
        in_specs=[tile, tile, tile, tile, BS((TM, D), lambda i: (i, 6)), BS((TM, D), lambda i: (i, 7)),
                  BS((1, 6 * D), lambda i: (0, 0)), vec, _br_spec(O_BRA), _br_spec(O_BRB), _br_spec(O_OUT),
                  BS(memory_space=pl.ANY)],
        out_specs=(tile, tile, tile, tile, tile, BS((TM, 2 * D), lambda i: (i, 0)),
                   BS((8, D), lambda i: (0, 0)), BS((1, 2 * D), lambda i: (0, 0)), BS(memory_space=pl.ANY)),
        scratch_shapes=_halves_sems(),
        compiler_params=_params("arbitrary"),
    )(dx2, y, ya, yb, p, p, mod, post_tm, wg, wg, wg, g)


def _hgrn_bwd_call(p, o, doa, st, logits, gn, part, g):
    S = p.shape[0]
    nb = S // TB
    ncb = TB // CHUNK

    def body(q_ref, f_ref, v_ref, og_ref, o_ref, doa_ref, st_ref, lg_ref, gn_ref, part_ref, g_ref,
             dp_ref, bsum_ref, dlg_ref, dgn_ref, recv_ref, hr_ref,
             dst_scr, dlb_scr, dqe_s, dqt_s, dkt_s, dkd_s, dv_s, dog_s, dble_s, send_sems, recv_sems, hs, hr):
        i = pl.program_id(0)
        start, finish = _chip_exchange(part_ref, recv_ref, send_sems, recv_sems)
        start_h, finish_h = _halves_exchange(g_ref, hr_ref, hs, hr)

        @pl.when(i == 0)
        def _():
            start_h()
            start()
            dst_scr[...] = jnp.zeros_like(dst_scr)
            dlb_scr[...] = jnp.zeros_like(dlb_scr)
            bsum_ref[...] = jnp.zeros_like(bsum_ref)
            dgn_ref[...] = jnp.zeros_like(dgn_ref)

        lb = _lower_bound(lg_ref)
        tril, triu = _tri_masks()

        def chunk(tt, carry):
            ci = ncb - 1 - tt
            rows = pl.ds(pl.multiple_of(ci * CHUNK, CHUNK), CHUNK)
            q_r, f_r = q_ref[rows, :], f_ref[rows, :]
            t = _hg_gates(q_r, f_r, lb, tril)
            v = v_ref[rows, :]
            for h in range(HEADS):
                sl = slice(h * DK, (h + 1) * DK)
                stp = st_ref[ci, :, sl]
                stb = stp.astype(BF16)
                qeb = t["qe"][:, sl].astype(BF16)
                qtb = t["qt"][:, sl].astype(BF16)
                ktb = t["kt"][:, sl].astype(BF16)
                kdb = t["kd"][:, sl].astype(BF16)
                vb = v[:, sl].astype(BF16)
                a = jnp.where(tril > 0.5, _mm(qtb, ktb, NT), 0.0)
                o_h = o_ref[rows, sl]
                rinv = lax.rsqrt(jnp.mean(o_h * o_h, axis=-1, keepdims=True) + EPS)
                oh = o_h * rinv
                og = og_ref[rows, sl]
                so = _sig(og)
                d_oa = doa_ref[rows, sl]
                don = d_oa * (og * so)
                dog_s[:, sl] = d_oa * (oh * gn_ref[:, sl]) * _dsilu(og, so)
                dgn_ref[:, sl] += _rowsum(don * oh)
                doh = don * gn_ref[:, sl]
                do = (rinv * (doh - oh * jnp.mean(doh * oh, axis=-1, keepdims=True))).astype(BF16)
                dqe_s[:, sl] = _mm(do, stb, NN)
                dstp = _mm(do, qeb, TN)
                dab = jnp.where(tril > 0.5, _mm(do, vb, NT), 0.0).astype(BF16)
                dqt_s[:, sl] = _mm(dab, ktb, NN)
                dkt_s[:, sl] = _mm(dab, qtb, TN)
                dstn = dst_scr[:, sl]
                dsb = dstn.astype(BF16)
                dkd_s[:, sl] = _mm(vb, dsb, NN)
                dv_s[:, sl] = _mm(a.astype(BF16), do, TN) + _mm(kdb, dsb, NT)
                el = t["elast"][:, sl]
                dst_scr[:, sl] = dstn * el + dstp
                dble_s[:, sl] = el * _rowsum(stp * dstn)
            dqe, dqt, dkt, dkd = dqe_s[...], dqt_s[...], dkt_s[...], dkd_s[...]
            dq = dqe * t["e"] + dqt * t["eq"]
            dk = dkt * t["ek"] + dkd * t["dd"]
            dkk = dkd * t["kd"]
            qt_r = t["qt"].astype(BF16).astype(F32)
            kt_r = t["kt"].astype(BF16).astype(F32)
            dbv = dqe * t["qe"] + dqt * qt_r - dkt * kt_r - dkk
            dg = _cumsum_mm(triu, dbv) + (_rowsum(dkk) + dble_s[...])
            df = dg / t["f"] - dk
            sf = t["sf"]
            dlb_scr[...] += _rowsum(df * (1.0 - sf))
            dqr = dq * _dsilu(q_r, t["sq"])
            dfr = df * (1.0 - lb) * (sf * (1.0 - sf))
            dvv, dog = dv_s[...], dog_s[...]
            dp_ref[rows, 0:D] = dqr.astype(BF16)
            dp_ref[rows, D:2 * D] = dfr.astype(BF16)
            dp_ref[rows, 2 * D:3 * D] = dvv.astype(BF16)
            dp_ref[rows, 3 * D:4 * D] = dog.astype(BF16)
            bsum_ref[:, 0:D] += _rowsum(dqr)
            bsum_ref[:, D:2 * D] += _rowsum(dfr)
            bsum_ref[:, 2 * D:3 * D] += _rowsum(dvv)
            bsum_ref[:, 3 * D:4 * D] += _rowsum(dog)
            return carry

        lax.fori_loop(0, ncb, chunk, 0)

        dl = dlb_scr[...] * lb * (1.0 - lb)
        dlg_ref[0:1, :] = dl
        dlg_ref[1:2, :] = -dl

        @pl.when(i == nb - 1)
        def _():
            finish_h()
            finish()

    col = lambda j: BS((TB, D), lambda i, j=j: (nb - 1 - i, j))
    rev = BS((TB, D), lambda i: (nb - 1 - i, 0))
    cd = pltpu.VMEM((CHUNK, D), F32)
    return pl.pallas_call(
        body, name="hgrn_bwd", grid=(nb,),
        out_shape=(SDS((S, 4 * D), BF16), SDS((1, 4 * D), F32), SDS((2, D), F32), SDS((1, D), F32),
                   SDS((3,) + part.shape[1:], part.dtype), SDS((N_CHIPS,) + g.shape[2:], g.dtype)),
        in_specs=[col(0), col(1), col(2), col(3), rev, rev, BS((ncb, DK, D), lambda i: (nb - 1 - i, 0, 0)),
                  BS((2, D), lambda i: (0, 0)), BS((1, D), lambda i: (0, 0)), BS(memory_space=pl.ANY),
                  BS(memory_space=pl.ANY)],
        out_specs=(BS((TB, 4 * D), lambda i: (nb - 1 - i, 0)), BS((1, 4 * D), lambda i: (0, 0)),
                   BS((2, D), lambda i: (0, 0)), BS((1, D), lambda i: (0, 0)), BS(memory_space=pl.ANY),
                   BS(memory_space=pl.ANY)),
        scratch_shapes=[pltpu.VMEM((DK, D), F32), pltpu.VMEM((1, D), F32), cd, cd, cd, cd, cd, cd,
                        pltpu.VMEM((1, D), F32)] + _exchange_sems() + _halves_sems(),
        compiler_params=_params("arbitrary"),
    )(p, p, p, p, o, doa, st, logits, gn, part, g)


def _conv_bwd_call(dcb, uc, u, p, dw, ln_g, ln_b, part):
    S = uc.shape[0]
    nb = S // TM
    hb = TM // HALO

    def body(dcb_ref, uc_ref, u_ref, uh_ref, cv_ref, cg_ref, dw_ref, g_ref, b_ref, part_ref,
             dp_ref, bsum_ref, ddw_ref, acc_ref, recv_ref, uext, dext, ush, dsh, send_sems, recv_sems):
        i = pl.program_id(0)
        start, finish = _chip_exchange(part_ref, recv_ref, send_sems, recv_sems)

        @pl.when(i == 0)
        def _():
            start()
            dext[TM:EXT, :] = jnp.zeros((EXT - TM, D), F32)
            uext[HALO + TM:EXT, :] = jnp.zeros((EXT - HALO - TM, D), F32)
            bsum_ref[...] = jnp.zeros_like(bsum_ref)
            ddw_ref[...] = jnp.zeros_like(ddw_ref)
            acc_ref[...] = jnp.zeros_like(acc_ref)

        first_tile = (nb - 1 - i) == 0
        uext[0:HALO, :] = jnp.where(first_tile, 0.0, uh_ref[...])
        uext[HALO:HALO + TM, :] = u_ref[...]
        _fill_shifted(uext, ush)

        for rb in range(TM // SUB):
            rs_ = slice(rb * SUB, (rb + 1) * SUB)
            xh, rs = _layernorm_stats(uc_ref[rs_, :])
            ln = xh * g_ref[...] + b_ref[...]
            dln = dcb_ref[rs_, :] * _dsilu(ln, _sig(ln))
            acc_ref[1:2, :] += _rowsum(dln * xh)
            acc_ref[2:3, :] += _rowsum(dln)
            dxh = dln * g_ref[...]
            duc = rs * (dxh - jnp.mean(dxh, axis=-1, keepdims=True)
                        - xh * jnp.mean(dxh * xh, axis=-1, keepdims=True))
            dext[rs_, :] = duc
            acc_ref[0:1, :] += _rowsum(duc)
        _fill_shifted(dext, dsh)

        for j in range(CONV_K):
            part = jnp.zeros((SUB, D), F32)
            for rb in range(TM // SUB):
                s0 = HALO - (CONV_K - 1) + j + rb * SUB
                part = part + dext[rb * SUB:(rb + 1) * SUB, :] * _window(uext, ush, s0, SUB)
            ddw_ref[j:j + 1, :] += _rowsum(part)

        for rb in range(TM // SUB):
            rs_ = slice(rb * SUB, (rb + 1) * SUB)
            du = jnp.zeros((SUB, D), F32)
            for j in range(CONV_K):
                s0 = rb * SUB + (CONV_K - 1) - j
                du = du + dw_ref[j:j + 1, :] * _window(dext, dsh, s0, SUB)
            cg = cg_ref[rs_, :]
            sg = _sig(cg)
            dcv = du * sg
            dcg = du * cv_ref[rs_, :] * (sg * (1.0 - sg))
            dp_ref[rs_, 0:D] = dcv.astype(BF16)
            dp_ref[rs_, D:2 * D] = dcg.astype(BF16)
            bsum_ref[:, 0:D] += _rowsum(dcv)
            bsum_ref[:, D:2 * D] += _rowsum(dcg)

        dext[TM:TM + HALO, :] = dext[0:HALO, :]

        @pl.when(i == nb - 1)
        def _():
            finish()

    rev = BS((TM, D), lambda i: (nb - 1 - i, 0))
    vec = BS((1, D), lambda i: (0, 0))
    return pl.pallas_call(
        body, name="conv_bwd", grid=(nb,),
        out_shape=(SDS((S, 2 * D), BF16), SDS((1, 2 * D), F32), SDS((32, D), F32), SDS((8, D), F32),
                   SDS((3,) + part.shape[1:], part.dtype)),
        in_specs=[rev, rev, rev, BS((HALO, D), lambda i: (jnp.maximum((nb - 1 - i) * hb - 1, 0), 0)),
                  BS((TM, D), lambda i: (nb - 1 - i, 4)), BS((TM, D), lambda i: (nb - 1 - i, 5)),
                  BS((CONV_K, D), lambda i: (0, 0)), vec, vec, BS(memory_space=pl.ANY)],
        out_specs=(BS((TM, 2 * D), lambda i: (nb - 1 - i, 0)), BS((1, 2 * D), lambda i: (0, 0)),
                   BS((32, D), lambda i: (0, 0)), BS((8, D), lambda i: (0, 0)), BS(memory_space=pl.ANY)),
        scratch_shapes=[pltpu.VMEM((EXT, D), F32), pltpu.VMEM((EXT, D), F32),
                        pltpu.VMEM((7, HALO + TM, D), F32), pltpu.VMEM((7, HALO + TM, D), F32)] + _exchange_sems(),
        compiler_params=_params("arbitrary"),
    )(dcb, uc, u, u, p, p, dw, ln_g, ln_b, part)


def _in_bwd_call(dp_hg, dp_cv, dp_gt, x, dx2, mod, pre_tm, wg, part, full_a, full_b):
    S = x.shape[0]
    tm = TM

    def body(hg_ref, cv_ref, gt_ref, x_ref, dx2_ref, mod_ref, g_ref, w_hbm, part_ref, fa_in, fb_in,
             gx_ref, acc_ref, recv_ref, fa_out, fb_out, w_vmem, sem, send_sems, recv_sems, sa, ra, sb, rb):
        start, finish = _chip_exchange(part_ref, recv_ref, send_sems, recv_sems)
        start_a, finish_a = _join_exchange(fa_in, fa_out, sa, ra)
        start_b, finish_b = _join_exchange(fb_in, fb_out, sb, rb)

        @pl.when(pl.program_id(0) == 0)
        def _():
            start_a()
            start_b()
            start()
            _load_rows(w_hbm, w_vmem, sem, O_IN).wait()
            acc_ref[...] = jnp.zeros_like(acc_ref)

        dh = jnp.zeros((tm, D), F32)
        for k in range(IN_COLS // D):
            src, kk = ((hg_ref, k), (cv_ref, k - 4), (gt_ref, k - 6))[0 if k < 4 else (1 if k < 6 else 2)]
            dh = dh + _mm(src[:, kk * D:(kk + 1) * D], w_vmem[k // 2, (k % 2) * D:(k % 2 + 1) * D, :], NT)
        xv = x_ref[...]
        r = lax.rsqrt(jnp.mean(xv * xv, axis=-1, keepdims=True) + EPS)
        xn = xv * r
        yv = xn * g_ref[...]
        acc_ref[0:1, :] += _rowsum(dh)
        acc_ref[1:2, :] += _rowsum(dh * yv)
        dyv = dh * (1.0 + mod_ref[:, D:2 * D])
        acc_ref[2:3, :] += _rowsum(dyv * xn)
        dxn = dyv * g_ref[...]
        gx_ref[...] = dx2_ref[...] + r * (dxn - xn * jnp.mean(dxn * xn, axis=-1, keepdims=True))

        @pl.when(pl.program_id(0) == S // tm - 1)
        def _():
            finish_a()
            finish_b()
            finish()

    tile = BS((tm, D), lambda i: (i, 0))
    hbm = BS(memory_space=pl.ANY)
    return pl.pallas_call(
        body, name="in_bwd", grid=(S // tm,),
        out_shape=(SDS((S, D), F32), SDS((8, D), F32), SDS((3,) + part.shape[1:], part.dtype),
                   SDS(full_a.shape, full_a.dtype), SDS(full_b.shape, full_b.dtype)),
        in_specs=[BS((tm, 4 * D), lambda i: (i, 0)), BS((tm, 2 * D), lambda i: (i, 0)),
                  BS((tm, 2 * D), lambda i: (i, 0)), tile, tile, BS((1, 6 * D), lambda i: (0, 0)),
                  BS((1, D), lambda i: (0, 0)), hbm, hbm, hbm, hbm],
        out_specs=(tile, BS((8, D), lambda i: (0, 0)), hbm, hbm, hbm),
        scratch_shapes=[pltpu.VMEM((N_CHIPS, R_IN, D), BF16), pltpu.SemaphoreType.DMA] + _exchange_sems()
        + _join_sems() + _join_sems(),
        input_output_aliases={9: 3, 10: 4},
        compiler_params=_params("arbitrary"),
    )(dp_hg, dp_cv, dp_gt, x, dx2, mod, pre_tm, wg, part, full_a, full_b)


def _wgrad_call(gp, a, b, name, bm, place, rows):
    S, M = a.shape
    N = b.shape[1]
    bk = min(S, 1024)
    nk = S // bk

    def body(a_ref, b_ref, *rest):
        o_ref, acc = rest[-2], rest[-1]
        k = pl.program_id(2)

        @pl.when(k == 0)
        def _():
            acc[...] = jnp.zeros_like(acc)

        acc[...] += _mm(a_ref[...], b_ref[...], TN)

        @pl.when(k == nk - 1)
        def _():
            o_ref[...] = acc[...].astype(BF16)

    in_specs = [BS((bk, bm), lambda i, j, k: (k, i)), BS((bk, D), lambda i, j, k: (k, j))]
    args = [a, b]
    if gp is not None:
        in_specs.append(BS(memory_space=pl.ANY))
        args.append(gp)
    return pl.pallas_call(
        body, name=name, grid=(M // bm, N // D, nk),
        out_shape=SDS((N_CHIPS, rows, D), BF16),
        in_specs=in_specs,
        out_specs=BS((None, bm, D), lambda i, j, k: (*place(i, j), 0)),
        scratch_shapes=[pltpu.VMEM((bm, D), F32)],
        input_output_aliases={} if gp is None else {2: 0},
        compiler_params=_params("parallel", "parallel", "arbitrary"),
    )(*args)


def _wgrad_rows_call(gp, a, b, name, blk):
    S = a.shape[0]
    bk = min(S, 1024)
    nk = S // bk

    def body(a_ref, b_ref, *rest):
        o_ref, acc = rest[-2], rest[-1]
        k = pl.program_id(0)

        @pl.when(k == 0)
        def _():
            acc[...] = jnp.zeros_like(acc)

        acc[...] += _mm(a_ref[...], b_ref[...], TN)

        @pl.when(k == nk - 1)
        def _():
            for c in range(N_CHIPS):
                o_ref[c] = acc[c * R_BR:(c + 1) * R_BR, :].astype(BF16)

    in_specs = [BS((bk, D), lambda k: (k, 0)), BS((bk, D), lambda k: (k, 0))]
    args = [a, b]
    if gp is not None:
        in_specs.append(BS(memory_space=pl.ANY))
        args.append(gp)
    return pl.pallas_call(
        body, name=name, grid=(nk,),
        out_shape=SDS((N_CHIPS, 3 * R_BR, D), BF16),
        in_specs=in_specs,
        out_specs=BS((N_CHIPS, R_BR, D), lambda k: (0, blk, 0)),
        scratch_shapes=[pltpu.VMEM((D, D), F32)],
        input_output_aliases={} if gp is None else {2: 0},
        compiler_params=_params("arbitrary"),
    )(*args)


def _outer_call(cact, dmod):
    n = dmod.shape[1]

    def body(a_ref, b_ref, o_ref):
        o_ref[...] = _mm(a_ref[...], b_ref[...], TN, HI)

    return pl.pallas_call(
        body, name="wgrad_ada", out_shape=SDS((D, n), F32),
        compiler_params=pltpu.CompilerParams(vmem_limit_bytes=VMEM_LIMIT),
    )(cact, dmod)


def _adamw_call(w, g, m, v, name):
    R, C = w.shape
    tr = R
    while tr * C > 512 * 1024 and tr % 16 == 0:
        tr //= 2
    c1 = 1.0 - ADAM_B1 ** ADAM_STEP
    c2 = 1.0 - ADAM_B2 ** ADAM_STEP

    def body(w_ref, g_ref, m_ref, v_ref, d_ref, m2_ref, v2_ref):
        g = g_ref[...]
        m2 = ADAM_B1 * m_ref[...] + (1.0 - ADAM_B1) * g
        v2 = ADAM_B2 * v_ref[...] + (1.0 - ADAM_B2) * (g * g)
        m2_ref[...] = m2
        v2_ref[...] = v2
        d_ref[...] = -ADAM_LR * ((m2 / c1) / (jnp.sqrt(v2 / c2) + ADAM_EPS) + ADAM_WD * w_ref[...])

    tile = BS((tr, C), lambda i: (i, 0))
    return pl.pallas_call(
        body, name=name, grid=(R // tr,), out_shape=(SDS((R, C), F32),) * 3,
        in_specs=[tile] * 4, out_specs=(tile,) * 3, compiler_params=_params("parallel"),
    )(w, g, m, v)


def _adamw_rows_call(ws, g, ms, vs, name):
    k = len(ws)
    r = ws[0].shape[0]
    c1 = 1.0 - ADAM_B1 ** ADAM_STEP
    c2 = 1.0 - ADAM_B2 ** ADAM_STEP

    def body(g_ref, *refs):
        ins, outs = refs[:3 * k], refs[3 * k:]
        for j in range(k):
            w_ref, m_ref, v_ref = ins[j], ins[k + j], ins[2 * k + j]
            d_ref, m2_ref, v2_ref = outs[j], outs[k + j], outs[2 * k + j]
            g = g_ref[j * r:(j + 1) * r, :]
            m2 = ADAM_B1 * m_ref[...] + (1.0 - ADAM_B1) * g
            v2 = ADAM_B2 * v_ref[...] + (1.0 - ADAM_B2) * (g * g)
            m2_ref[...] = m2
            v2_ref[...] = v2
            d_ref[...] = -ADAM_LR * ((m2 / c1) / (jnp.sqrt(v2 / c2) + ADAM_EPS) + ADAM_WD * w_ref[...])

    out = pl.pallas_call(
        body, name=name, out_shape=(SDS(ws[0].shape, F32),) * (3 * k),
        compiler_params=pltpu.CompilerParams(vmem_limit_bytes=VMEM_LIMIT),
    )(g, *ws, *ms, *vs)
    return out[:k], out[k:2 * k], out[2 * k:]


def _adamw_small_call(ws, ms, vs, row0, ssum, g_dw):
    n = len(ws)
    c1 = 1.0 - ADAM_B1 ** ADAM_STEP
    c2 = 1.0 - ADAM_B2 ** ADAM_STEP

    def adam(w, g, m, v):
        m2 = ADAM_B1 * m + (1.0 - ADAM_B1) * g
        v2 = ADAM_B2 * v + (1.0 - ADAM_B2) * (g * g)
        return -ADAM_LR * ((m2 / c1) / (jnp.sqrt(v2 / c2) + ADAM_EPS) + ADAM_WD * w), m2, v2

    def body(s_ref, gdw_ref, *refs):
        ins, outs = refs[:3 * n], refs[3 * n:]
        for j in range(n):
            w_ref, m_ref, v_ref = ins[j], ins[n + j], ins[2 * n + j]
            g_ref, d_ref, m2_ref, v2_ref = outs[j], outs[n + j], outs[2 * n + j], outs[3 * n + j]
            if j == n - 1:
                pieces = [(slice(None), slice(None), gdw_ref[...])]
            elif w_ref.shape[0] == 1:
                pieces = [(slice(None), slice(i * D, (i + 1) * D), s_ref[row0[j] + i:row0[j] + i + 1, :])
                          for i in range(w_ref.shape[1] // D)]
            else:
                pieces = [(slice(None), slice(None), s_ref[row0[j]:row0[j] + w_ref.shape[0], :])]
            for rs, cs, g in pieces:
                d, m2, v2 = adam(w_ref[rs, cs], g, m_ref[rs, cs], v_ref[rs, cs])
                g_ref[rs, cs] = g
                d_ref[rs, cs] = d
                m2_ref[rs, cs] = m2
                v2_ref[rs, cs] = v2

    out = pl.pallas_call(
        body, name="adamw_small", out_shape=tuple(SDS(w.shape, F32) for w in ws) * 4,
        compiler_params=pltpu.CompilerParams(vmem_limit_bytes=VMEM_LIMIT),
    )(ssum, g_dw, *ws, *ms, *vs)
    return [(out[j], out[n + j], out[2 * n + j], out[3 * n + j]) for j in range(n)]


def _adamw_halves_call(w, g, m, v, name):
    R, C = w.shape
    tr = R
    while tr * C > 512 * 1024 and tr % 16 == 0:
        tr //= 2
    c1 = 1.0 - ADAM_B1 ** ADAM_STEP
    c2 = 1.0 - ADAM_B2 ** ADAM_STEP

    def body(w_ref, g_ref, m_ref, v_ref, go_ref, d_ref, m2_ref, v2_ref):
        for k in range(2):
            cs = slice(k * D, (k + 1) * D)
            g = g_ref[k]
            go_ref[:, cs] = g
            m2 = ADAM_B1 * m_ref[:, cs] + (1.0 - ADAM_B1) * g
            v2 = ADAM_B2 * v_ref[:, cs] + (1.0 - ADAM_B2) * (g * g)
            m2_ref[:, cs] = m2
            v2_ref[:, cs] = v2
            d_ref[:, cs] = -ADAM_LR * ((m2 / c1) / (jnp.sqrt(v2 / c2) + ADAM_EPS) + ADAM_WD * w_ref[:, cs])

    tile = BS((tr, C), lambda i: (i, 0))
    return pl.pallas_call(
        body, name=name, grid=(R // tr,), out_shape=(SDS((R, C), F32),) * 4,
        in_specs=[tile, BS((2, tr, D), lambda i: (0, i, 0)), tile, tile], out_specs=(tile,) * 4,
        compiler_params=_params("parallel"),
    )(w, g, m, v)


def _rs_begin(g, c_idx, tag):
    n = g.shape[1]
    g = g.reshape(N_CHIPS, 2, n // 2, D)
    return _add_halves_call(g, _sibling_halves_call(g, tag), c_idx, tag)


def _local_step(x, mod, cact, target, wg, pack, small, c_idx, chip_idx):
    p, h1, wg = _fwd_in_call(x, mod, small["pre_tm"], wg, small["b_in"], pack, small["order"])
    o, oa, st, wg = _hgrn_fwd_call(p, small["logits"], small["hg_norm"], wg, pack)
    u, uc, cb, wg = _conv_fwd_call(p, small["conv_dw"], small["conv_db"], small["ln_g"], small["ln_b"], wg, pack)
    ya, yb, mg, y, x2, h2 = _merge_fwd_call(oa, cb, p, x, mod, small["post_tm"], small["pre_cm"], wg)
    z, da, dy2, dx2, acc_f = _ffn_call(h2, x2, target, mod, small["post_cm"], small["pre_cm"], wg)

    g_ff = _wgrad_call(None, h2, da, "wgrad_ff1", D, lambda i, j: (j, 0), 2 * R_FF)
    g_ff = _wgrad_call(g_ff, z, dy2, "wgrad_ff2", D, lambda i, j: (i, 1), 2 * R_FF)
    g_ff = g_ff.reshape(N_CHIPS, 2, R_FF, D)
    dy, dya, dyb, doa, dcb, dp_gt, acc_m, bs_gt, hr_ff = _merge_bwd_call(dx2, y, ya, yb, p, mod, small["post_tm"],
                                                                        wg, g_ff)
    part_ff = _add_halves_call(g_ff, hr_ff, c_idx, "ff")

    g_br = _wgrad_rows_call(None, oa, dya, "wgrad_br_a", 0)
    g_br = _wgrad_rows_call(g_br, cb, dyb, "wgrad_br_b", 1)
    g_br = _wgrad_rows_call(g_br, mg, dy, "wgrad_out", 2)
    g_br = g_br.reshape(N_CHIPS, 2, 3 * R_BR // 2, D)
    dp_hg, bs_hg, dlg, dgn, recv_ff, hr_br = _hgrn_bwd_call(p, o, doa, st, small["logits"], small["hg_norm"],
                                                            part_ff, g_br)
    part_br = _add_halves_call(g_br, hr_br, c_idx, "br")
    dp_cv, bs_cv, ddw, acc_c, recv_br = _conv_bwd_call(dcb, uc, u, p, small["conv_dw"], small["ln_g"], small["ln_b"],
                                                        part_br)

    g_in = _wgrad_call(None, h1, dp_hg, "wgrad_in_hg", D, lambda i, j: (j // 2, j % 2), R_IN)
    g_in = _wgrad_call(g_in, h1, dp_cv, "wgrad_in_cv", D, lambda i, j: (2, j), R_IN)
    g_in = _wgrad_call(g_in, h1, dp_gt, "wgrad_in_gt", D, lambda i, j: (3, j), R_IN)
    part_in = _rs_begin(g_in, c_idx, "in")
    chip_c = jnp.concatenate([chip_idx, c_idx])
    full_ff = _add_chips_call(part_ff, recv_ff, chip_c, "ff")
    full_br = _add_chips_call(part_br, recv_br, chip_c, "br")
    gx, acc_i, recv_in, full_ff, full_br = _in_bwd_call(dp_hg, dp_cv, dp_gt, x, dx2, mod, small["pre_tm"], wg,
                                                        part_in, full_ff, full_br)
    red_ff = full_ff.reshape(2 * R_FF, D)
    red_br = full_br.reshape(3 * R_BR, D)
    full_in = _add_chips_call(part_in, recv_in, chip_c, "in")

    zrow = jnp.zeros((1, D), F32)
    rows = [acc_i[0:1], acc_i[1:2], acc_m[0:1], acc_f[2:3], acc_f[3:4], acc_f[0:1],
            acc_i[2:3], acc_m[1:2], acc_f[4:5], acc_f[1:2],
            jnp.concatenate([bs_hg, bs_cv, bs_gt], axis=1).reshape(8, D),
            dlg, dgn, acc_c[0:1], acc_c[1:2], acc_c[2:3],
            ddw,
            cact, acc_f[5:6]] + [zrow] * 6
    full_in, sall, ssum = _join_gather_call(full_in, jnp.concatenate(rows, axis=0), "in")
    return gx, sall, ssum, full_in.reshape(R_IN, D), red_ff, red_br


def kernel(x, c, w_ada, b_ada, pre_norm_tm, post_norm_tm, pre_norm_cm, post_norm_cm, w_in, b_in, hg_lb_logits, hg_norm, conv_dw, conv_db, conv_ln_g, conv_ln_b, w_br_a, w_br_b, w_out, w_ff1, w_ff2, loss_target, m_w_ada, m_b_ada, m_pre_norm_tm, m_post_norm_tm, m_pre_norm_cm, m_post_norm_cm, m_w_in, m_b_in, m_hg_lb_logits, m_hg_norm, m_conv_dw, m_conv_db, m_conv_ln_g, m_conv_ln_b, m_w_br_a, m_w_br_b, m_w_out, m_w_ff1, m_w_ff2, v_w_ada, v_b_ada, v_pre_norm_tm, v_post_norm_tm, v_pre_norm_cm, v_post_norm_cm, v_w_in, v_b_in, v_hg_lb_logits, v_hg_norm, v_conv_dw, v_conv_db, v_conv_ln_g, v_conv_ln_b, v_w_br_a, v_w_br_b, v_w_out, v_w_ff1, v_w_ff2):
    xi, yi, ci = lax.axis_index("x"), lax.axis_index("y"), lax.axis_index("c")
    chip = 2 * xi + yi
    c_idx = jnp.reshape(ci, (1,)).astype(jnp.int32)
    chip_idx = jnp.reshape(chip, (1,)).astype(jnp.int32)

    w_in_halves = w_in[0].reshape(D, 2, D).transpose(1, 0, 2).reshape(R_IN, D)
    pack = jnp.concatenate([w_in_halves, w_ff1[0], w_ff2[0], w_br_a[0], w_br_b[0], w_out[0]],
                           axis=0).astype(BF16)
    wg = lax.dynamic_update_slice(lax.empty((N_CHIPS, PACK_W, D), BF16), pack[None], (chip, 0, 0))
    wa = 6 * D // N_CHIPS
    me = 4 * xi + 2 * yi + ci
    dw_blk = jnp.concatenate([conv_dw[0].reshape(-1), jnp.zeros((8 * D - CONV_K * 256,), F32)]).reshape(8, D)
    dw_all, ca_all, mod_all = _prologue_call(
        dw_blk, jnp.broadcast_to(c, (8, D)), w_ada[0].astype(BF16),
        lax.dynamic_slice_in_dim(b_ada, chip * wa, wa, axis=1))
    order = jnp.stack([chip, 2 * (1 - xi) + yi, 2 * xi + (1 - yi), 2 * (1 - xi) + (1 - yi)]).astype(jnp.int32)
    dw_all = dw_all.reshape(N_CHIPS, 2, 8 * D)[:, 0, :CONV_K * 256].reshape(N_CHIPS, CONV_K, 256)
    dw_full = dw_all.transpose(1, 0, 2).reshape(CONV_K, D)
    cact = lax.dynamic_slice_in_dim(ca_all, me * 8, 1, axis=0)
    mod_mine = lax.dynamic_index_in_dim(mod_all.reshape(N_CHIPS, 2, N_DEV, wa)[:, 0], me, axis=1,
                                        keepdims=False)
    mod = mod_mine.reshape(1, 6 * D)

    small = dict(pre_tm=pre_norm_tm, post_tm=post_norm_tm, pre_cm=pre_norm_cm, post_cm=post_norm_cm,
                 b_in=b_in, logits=hg_lb_logits, hg_norm=hg_norm, conv_dw=dw_full, conv_db=conv_db,
                 ln_g=conv_ln_g, ln_b=conv_ln_b, order=order)

    gx, sall, ssum, red_in, red_ff, red_br = _local_step(x[0], mod, cact, loss_target[0], wg, pack, small, c_idx,
                                                    chip_idx)

    shapes = {"in": w_in.shape, "br_a": w_br_a.shape, "br_b": w_br_b.shape, "out": w_out.shape,
              "ff1": w_ff1.shape, "ff2": w_ff2.shape}
    offs = {"in": (red_in, 0, R_IN), "ff1": (red_ff, 0, R_FF), "ff2": (red_ff, R_FF, 2 * R_FF),
            "br_a": (red_br, 0, R_BR), "br_b": (red_br, R_BR, 2 * R_BR), "out": (red_br, 2 * R_BR, 3 * R_BR)}
    wmv = {"in": (w_in, m_w_in, v_w_in), "br_a": (w_br_a, m_w_br_a, v_w_br_a), "br_b": (w_br_b, m_w_br_b, v_w_br_b),
           "out": (w_out, m_w_out, v_w_out), "ff1": (w_ff1, m_w_ff1, v_w_ff1), "ff2": (w_ff2, m_w_ff2, v_w_ff2)}
    res = {}
    for n in ("in", "ff1", "ff2"):
        shp = shapes[n]
        g2d = offs[n][0][offs[n][1]:offs[n][2]]
        w_, m_, v_ = (a[0] for a in wmv[n])
        if n == "in":
            g2d, d_, m2_, v2_ = _adamw_halves_call(w_, g2d.reshape(2, D, D), m_, v_, "adamw_in")
        else:
            g2d = g2d.reshape(shp[1], shp[2])
            d_, m2_, v2_ = _adamw_call(w_, g2d, m_, v_, "adamw_" + n)
        res[n] = tuple(a.reshape(shp) for a in (g2d, d_, m2_, v2_))
    trio = ("br_a", "br_b", "out")
    d3, m3, v3 = _adamw_rows_call([wmv[n][0][0] for n in trio], red_br, [wmv[n][1][0] for n in trio],
                                  [wmv[n][2][0] for n in trio], "adamw_br")
    for j, n in enumerate(trio):
        res[n] = tuple(a.reshape(shapes[n]) for a in (red_br[j * R_BR:(j + 1) * R_BR], d3[j], m3[j], v3[j]))

    sall = sall.reshape(N_DEV, SMALL_ROWS, D)
    loss = jnp.sum(ssum[57])
    dmod_all = sall[:, 0:6, :].reshape(N_DEV, 6 * D)
    g_ada = _outer_call(sall[:, 56, :], lax.dynamic_slice_in_dim(dmod_all, chip * wa, wa, axis=1))
    g_dw = lax.dynamic_slice_in_dim(ssum[24:24 + CONV_K], chip * 256, 256, axis=1)
    d_, m2_, v2_ = _adamw_call(w_ada[0], g_ada, m_w_ada[0], v_w_ada[0], "adamw_ada")
    res["ada"] = tuple(a.reshape(w_ada.shape) for a in (g_ada, d_, m2_, v2_))

    names = ["b_ada", "pre_tm", "post_tm", "pre_cm", "post_cm", "b_in", "logits", "hg_norm", "conv_db", "ln_g", "ln_b",
             "conv_dw"]
    row0 = [0, 6, 7, 8, 9, 10, 18, 20, 21, 22, 23, None]
    sres = _adamw_small_call(
        [b_ada, pre_norm_tm, post_norm_tm, pre_norm_cm, post_norm_cm, b_in, hg_lb_logits, hg_norm, conv_db,
         conv_ln_g, conv_ln_b, conv_dw[0]],
        [m_b_ada, m_pre_norm_tm, m_post_norm_tm, m_pre_norm_cm, m_post_norm_cm, m_b_in, m_hg_lb_logits, m_hg_norm,
         m_conv_db, m_conv_ln_g, m_conv_ln_b, m_conv_dw[0]],
        [v_b_ada, v_pre_norm_tm, v_post_norm_tm, v_pre_norm_cm, v_post_norm_cm, v_b_in, v_hg_lb_logits, v_hg_norm,
         v_conv_db, v_conv_ln_g, v_conv_ln_b, v_conv_dw[0]], row0, ssum, g_dw)
    for nm, r4 in zip(names, sres):
        res[nm] = tuple(a.reshape(conv_dw.shape) for a in r4) if nm == "conv_dw" else r4

    order = ["ada", "b_ada", "pre_tm", "post_tm", "pre_cm", "post_cm", "in", "b_in", "logits", "hg_norm", "conv_dw",
             "conv_db", "ln_g", "ln_b", "br_a", "br_b", "out", "ff1", "ff2"]
    outs = [loss, gx.reshape(x.shape)]
    for kind in range(4):
        outs.extend(res[n][kind] for n in order)
    return tuple(outs)
```

```python
import jax
import jax.numpy as jnp
from jax import lax
from jax.experimental import pallas as pl
from jax.experimental.pallas import tpu as pltpu

F32, BF16 = jnp.float32, jnp.bfloat16
SDS = jax.ShapeDtypeStruct
BS = pl.BlockSpec
MESH = pl.DeviceIdType.MESH
HI = lax.Precision.HIGHEST

D = 1024
D_FF = 4096
IN_COLS = 8192
HEADS, DK = 8, 128
CHUNK = 128
CONV_K = 31
HALO = 32
SUB = 32
EPS = 1e-6
N_CHIPS, N_DEV = 4, 8
TM = 256
TB = 256
VMEM_LIMIT = 56 * 1024 * 1024

R_IN, R_BR, R_FF = 2048, 256, 1024
PACK_W = R_IN + 3 * R_BR + 2 * R_FF
O_IN, O_FF1, O_FF2, O_BRA, O_BRB, O_OUT = 0, 2048, 3072, 4096, 4352, 4608
SMALL_ROWS = 64

ADAM_LR, ADAM_B1, ADAM_B2, ADAM_EPS, ADAM_WD, ADAM_STEP = 0.001, 0.9, 0.999, 1e-08, 0.01, 10

NN = (((1,), (0,)), ((), ()))
NT = (((1,), (1,)), ((), ()))
TN = (((0,), (0,)), ((), ()))


def _mm(a, b, dims=NN, precision=None):
    return lax.dot_general(a, b, dims, preferred_element_type=F32, precision=precision)


def _sig(v):
    return jax.nn.sigmoid(v)


def _dsilu(v, s):
    return s * (1.0 + v * (1.0 - s))


def _params(*sem):
    return pltpu.CompilerParams(dimension_semantics=sem if sem else None, vmem_limit_bytes=VMEM_LIMIT)


def _rowsum(v):
    return jnp.sum(v, axis=0, keepdims=True)


def _mesh_pos():
    return lax.axis_index("x"), lax.axis_index("y"), lax.axis_index("c")


def _allgather_parts(x_ref, out_ref, send_sems, recv_sems, local_sem):
    m_per = x_ref.shape[0]
    x, y, c = _mesh_pos()
    me, sibling = (x, y, c), (x, y, 1 - c)
    chips = [(1 - x, y), (x, 1 - y), (1 - x, 1 - y)]

    def rows(px, py, pc):
        return out_ref.at[pl.ds((4 * px + 2 * py + pc) * m_per, m_per), :]

    def copy(k, block, to, src=None):
        return pltpu.make_async_remote_copy(
            src_ref=rows(*block) if src is None else src, dst_ref=rows(*block),
            send_sem=send_sems.at[k], recv_sem=recv_sems.at[k], device_id=to, device_id_type=MESH)

    def first():
        return [copy(0, me, sibling, src=x_ref)] + [copy(1 + j, me, (*chip, c), src=x_ref)
                                                    for j, chip in enumerate(chips)]

    def start():
        pltpu.make_async_copy(x_ref, rows(*me), local_sem).start()
        for cp in first():
            cp.start()

    def finish():
        passed = [copy(4 + j, (*chip, c), sibling) for j, chip in enumerate(chips)]
        for j, chip in enumerate(chips):
            copy(1 + j, (*chip, c), me).wait_recv()
            passed[j].start()
        copy(0, sibling, me).wait_recv()
        for j, chip in enumerate(chips):
            copy(4 + j, (*chip, 1 - c), me).wait_recv()
        for cp in first() + passed:
            cp.wait_send()
        pltpu.make_async_copy(x_ref, rows(*me), local_sem).wait()

    return start, finish


def _allgather_sems():
    return [pltpu.SemaphoreType.DMA((7,)), pltpu.SemaphoreType.DMA((7,)), pltpu.SemaphoreType.DMA]
def _gather_sems(n_ranges):
    return [pltpu.SemaphoreType.DMA((6 * n_ranges,)), pltpu.SemaphoreType.DMA((6 * n_ranges,))]


def _pack_gather(pack_ref, wg_ref, send_sems, recv_sems, ranges):
    x, y, c = _mesh_pos()
    me, sibling = (x, y, c), (x, y, 1 - c)
    chips = [(1 - x, y), (x, 1 - y), (1 - x, 1 - y)]

    def land(r, px, py, pc):
        off, n = ranges[r]
        return wg_ref.at[2 * px + py, pl.ds(off + pc * (n // 2), n // 2), :]

    def mine(r):
        off, n = ranges[r]
        return pack_ref.at[pl.ds(off + c * (n // 2), n // 2), :]

    def copy(r, k, block, to, src=None):
        return pltpu.make_async_remote_copy(
            src_ref=land(r, *block) if src is None else src, dst_ref=land(r, *block),
            send_sem=send_sems.at[6 * r + k], recv_sem=recv_sems.at[6 * r + k], device_id=to, device_id_type=MESH)

    def start():
        for r in range(len(ranges)):
            for j, chip in enumerate(chips):
                copy(r, j, me, (*chip, c), src=mine(r)).start()

    def finish():
        for r in range(len(ranges)):
            for j, chip in enumerate(chips):
                copy(r, j, (*chip, c), me).wait_recv()
                copy(r, 3 + j, (*chip, c), sibling).start()
        for r in range(len(ranges)):
            for j, chip in enumerate(chips):
                copy(r, 3 + j, (*chip, 1 - c), me).wait_recv()
        for r in range(len(ranges)):
            for j, chip in enumerate(chips):
                copy(r, j, me, (*chip, c), src=mine(r)).wait_send()
                copy(r, 3 + j, (*chip, c), sibling).wait_send()

    return start, finish


def _relay_sems():
    return [pltpu.SemaphoreType.DMA((8,)), pltpu.SemaphoreType.DMA((8,))]


def _relay_gather(pack_ref, wg_ref, send_sems, recv_sems, off, n):
    x, y, c = _mesh_pos()
    me, sibling = (x, y, c), (x, y, 1 - c)
    chips = [(1 - x, y), (x, 1 - y), (1 - x, 1 - y)]
    h, q = n // 2, n // 4

    def land(px, py, pc, piece=None):
        if piece is None:
            return wg_ref.at[2 * px + py, pl.ds(off + pc * h, h), :]
        return wg_ref.at[2 * px + py, pl.ds(off + pc * h + piece * q, q), :]

    def copy(k, ref, to, src=None):
        return pltpu.make_async_remote_copy(
            src_ref=ref if src is None else src, dst_ref=ref, send_sem=send_sems.at[k], recv_sem=recv_sems.at[k],
            device_id=to, device_id_type=MESH)

    def direct(j):
        return copy(j, land(x, y, c), (*chips[j], c), src=pack_ref.at[pl.ds(off + c * h, h), :])

    def relayed(j):
        if j == 0:
            return copy(6, land(*chips[0], c, 1), (x, 1 - y, c))
        return copy(7, land(*chips[1], c, 0), (1 - x, y, c))

    def start():
        direct(0).start()
        direct(1).start()

    def arrive(j):
        if j == 0:
            for k in range(2):
                copy(k, land(*chips[k], c), me).wait_recv()
                relayed(k).start()
                copy(3 + k, land(*chips[k], c), sibling).start()
        if j == 2:
            copy(7, land(*chips[2], c, 0), me).wait_recv()
            copy(6, land(*chips[2], c, 1), me).wait_recv()
            copy(5, land(*chips[2], c), sibling).start()
        copy(3 + j, land(*chips[j], 1 - c), me).wait_recv()

    def drain():
        for j in range(2):
            direct(j).wait_send()
            relayed(j).wait_send()
        for j in range(3):
            copy(3 + j, land(*chips[j], c), sibling).wait_send()

    return start, arrive, drain


def _prologue_call(dw_blk, c_blk, w_ada, b_ada):
    wa = w_ada.shape[1]

    def body(dw_ref, c_ref, wa_ref, ba_ref, dwg_ref, ca_ref, modg_ref,
             cg_scr, part_scr, s1, r1, l1, s2, r2, l2, s3, r3, l3):
        start_c, finish_c = _allgather_parts(c_ref, cg_scr, s2, r2, l2)
        start_dw, finish_dw = _allgather_parts(dw_ref, dwg_ref, s1, r1, l1)
        start_mod, finish_mod = _allgather_parts(part_scr, modg_ref, s3, r3, l3)
        start_c()
        start_dw()
        finish_c()
        cv = cg_scr[...]
        ca = cv * _sig(cv)
        ca_ref[...] = ca
        pick = (lax.broadcasted_iota(jnp.int32, (N_DEV, N_DEV * 8), 1)
                == 8 * lax.broadcasted_iota(jnp.int32, (N_DEV, N_DEV * 8), 0)).astype(BF16)
        ca8 = _mm(pick, ca.astype(BF16)).astype(BF16)
        part_scr[...] = _mm(ca8, wa_ref[...]) + ba_ref[...]
        start_mod()
        finish_dw()
        finish_mod()

    vm = BS(memory_space=pltpu.VMEM)
    return pl.pallas_call(
        body, name="prologue_adaln_conv_dw",
        out_shape=(SDS((N_DEV * 8, D), F32), SDS((N_DEV * 8, D), F32), SDS((N_DEV * N_DEV, wa), F32)),
        in_specs=[vm, vm, vm, vm], out_specs=(vm, vm, vm),
        scratch_shapes=[pltpu.VMEM((N_DEV * 8, D), F32), pltpu.VMEM((N_DEV, wa), F32)]
        + _allgather_sems() + _allgather_sems() + _allgather_sems(),
        compiler_params=pltpu.CompilerParams(vmem_limit_bytes=VMEM_LIMIT),
    )(dw_blk, c_blk, w_ada, b_ada)


def _halves_exchange(g_ref, out_ref, send_sems, recv_sems):
    x, y, c = _mesh_pos()

    def copies():
        return [pltpu.make_async_remote_copy(
            src_ref=g_ref.at[k, 1 - c], dst_ref=out_ref.at[k], send_sem=send_sems.at[k], recv_sem=recv_sems.at[k],
            device_id=(x, y, 1 - c), device_id_type=MESH) for k in range(N_CHIPS)]

    def start():
        for cp in copies():
            cp.start()

    def finish():
        for cp in copies():
            cp.wait()

    return start, finish


def _halves_sems():
    return [pltpu.SemaphoreType.DMA((N_CHIPS,)), pltpu.SemaphoreType.DMA((N_CHIPS,))]


def _sibling_halves_call(g, tag):
    _, _, h, n = g.shape

    def body(g_ref, out_ref, send_sems, recv_sems):
        start, finish = _halves_exchange(g_ref, out_ref, send_sems, recv_sems)
        start()
        finish()

    return pl.pallas_call(
        body, name="rs_sibling_halves_" + tag, out_shape=SDS((N_CHIPS, h, n), g.dtype),
        in_specs=[BS(memory_space=pl.ANY)], out_specs=BS(memory_space=pl.ANY),
        scratch_shapes=_halves_sems(),
    )(g)


def _chip_exchange(p_ref, out_ref, send_sems, recv_sems):
    x, y, c = _mesh_pos()
    chips = [(1 - x, y), (x, 1 - y), (1 - x, 1 - y)]

    def copies():
        return [pltpu.make_async_remote_copy(
            src_ref=p_ref.at[2 * cx + cy], dst_ref=out_ref.at[j], send_sem=send_sems.at[j], recv_sem=recv_sems.at[j],
            device_id=(cx, cy, c), device_id_type=MESH) for j, (cx, cy) in enumerate(chips)]

    def start():
        for cp in copies():
            cp.start()

    def finish():
        for cp in copies():
            cp.wait()

    return start, finish


def _exchange_sems():
    return [pltpu.SemaphoreType.DMA((3,)), pltpu.SemaphoreType.DMA((3,))]


def _join_exchange(in_ref, out_ref, send_sems, recv_sems):
    h = in_ref.shape[1]
    q = h // 4
    x, y, c = _mesh_pos()

    def copy(k, half):
        return pltpu.make_async_remote_copy(
            src_ref=in_ref.at[half, pl.ds(k * q, q)], dst_ref=out_ref.at[half, pl.ds(k * q, q)],
            send_sem=send_sems.at[k], recv_sem=recv_sems.at[k],
            device_id=(x, y, 1 - c), device_id_type=MESH)

    def start():
        for k in range(4):
            copy(k, c).start()

    def finish():
        for k in range(4):
            copy(k, c).wait_send()
            copy(k, 1 - c).wait_recv()

    return start, finish


def _join_sems():
    return [pltpu.SemaphoreType.DMA((4,)), pltpu.SemaphoreType.DMA((4,))]


def _join_gather_call(full, srows, tag):
    mr = srows.shape[0]

    def body(in_ref, s_ref, out_ref, all_ref, sum_ref, send_sems, recv_sems, gs, gr, gl):
        start, finish = _join_exchange(in_ref, out_ref, send_sems, recv_sems)
        start_g, finish_g = _allgather_parts(s_ref, all_ref, gs, gr, gl)
        start()
        start_g()
        finish_g()
        acc = all_ref[0:mr, :]
        for d in range(1, N_DEV):
            acc = acc + all_ref[d * mr:(d + 1) * mr, :]
        sum_ref[...] = acc
        finish()

    hbm = BS(memory_space=pl.ANY)
    vm = BS(memory_space=pltpu.VMEM)
    return pl.pallas_call(
        body, name="rs_sibling_join_" + tag,
        out_shape=(SDS(full.shape, full.dtype), SDS((N_DEV * mr, D), F32), SDS((mr, D), F32)),
        in_specs=[hbm, vm], out_specs=(hbm, vm, vm),
        scratch_shapes=_join_sems() + _allgather_sems(), input_output_aliases={0: 0},
        compiler_params=pltpu.CompilerParams(vmem_limit_bytes=VMEM_LIMIT),
    )(full, srows)


def _add_halves_call(g, recv, c_idx, tag):
    _, _, h, n = g.shape
    tr = h // 2

    def body(c_ref, g_ref, r_ref, o_ref):
        o_ref[...] = (g_ref[...].astype(F32) + r_ref[...].astype(F32)).astype(BF16)

    return pl.pallas_call(
        body, name="rs_add_halves_" + tag, out_shape=SDS((N_CHIPS, h, n), BF16),
        grid_spec=pltpu.PrefetchScalarGridSpec(
            num_scalar_prefetch=1, grid=(N_CHIPS, 2),
            in_specs=[BS((None, None, tr, n), lambda k, r, c_ref: (k, c_ref[0], r, 0)),
                      BS((None, tr, n), lambda k, r, c_ref: (k, r, 0))],
            out_specs=BS((None, tr, n), lambda k, r, c_ref: (k, r, 0))),
        compiler_params=_params("arbitrary", "arbitrary"),
    )(c_idx, g, recv)


def _add_chips_call(p, recv, chip_c_idx, tag):
    _, h, n = p.shape
    tr = h // 2

    def body(k_ref, p_ref, r_ref, o_ref):
        acc = p_ref[...].astype(F32)
        for j in range(3):
            acc = acc + r_ref[j].astype(F32)
        o_ref[...] = acc

    return pl.pallas_call(
        body, name="rs_add_chips_" + tag, out_shape=SDS((2, h, n), F32),
        grid_spec=pltpu.PrefetchScalarGridSpec(
            num_scalar_prefetch=1, grid=(2,),
            in_specs=[BS((None, tr, n), lambda r, k_ref: (k_ref[0], r, 0)),
                      BS((3, tr, n), lambda r, k_ref: (0, r, 0))],
            out_specs=BS((None, tr, n), lambda r, k_ref: (k_ref[1], r, 0))),
        compiler_params=_params("arbitrary"),
    )(chip_c_idx, p, recv)


def _load_rows(wg_hbm, w_vmem, sem, off):
    cp = pltpu.make_async_copy(wg_hbm.at[:, pl.ds(off, w_vmem.shape[1]), :], w_vmem, sem)
    cp.start()
    return cp


def _fwd_in_call(x, mod, pre_tm, wg, b_in, pack, order):
    S = x.shape[0]
    tmf = 2 * TM
    nt = S // tmf
    wc = IN_COLS // N_CHIPS

    def body(ord_ref, x_ref, mod_ref, g_ref, w_hbm, b_ref, pack_ref, p_ref, h_hbm, wg_out, w_vmem, h_scr, sems,
             send_sems, recv_sems, send_sems2, recv_sems2):
        q, i = pl.program_id(0), pl.program_id(1)
        rows = pl.ds(pl.multiple_of(i * tmf, tmf), tmf)
        start, arrive, drain = _relay_gather(pack_ref, wg_out, send_sems, recv_sems, O_IN, R_IN)
        start2, finish2 = _pack_gather(pack_ref, wg_out, send_sems2, recv_sems2, [(O_OUT, R_BR)])

        def weights(phase):
            src = pack_ref.at[pl.ds(O_IN, R_IN), :] if phase == 0 else wg_out.at[ord_ref[phase], pl.ds(O_IN, R_IN), :]
            return pltpu.make_async_copy(src, w_vmem.at[phase % 2], sems.at[phase % 2])

        def own_block(k):
            mx, my, mc = _mesh_pos()
            rows = pl.ds(k * (PACK_W // 2), PACK_W // 2)
            return pltpu.make_async_remote_copy(
                src_ref=pack_ref.at[rows, :], dst_ref=wg_out.at[2 * mx + my, rows, :], send_sem=send_sems2.at[6 + k],
                recv_sem=recv_sems2.at[6 + k], device_id=(mx, my, 1 - mc), device_id_type=MESH)

        @pl.when((q == 0) & (i == 0))
        def _():
            start()
            own_block(0).start()
            own_block(1).start()
            weights(0).start()
            weights(0).wait()

        @pl.when((q == 1) & (i == 0))
        def _():
            arrive(0)
            start2()
            weights(1).start()
            weights(1).wait()
            arrive(1)
            weights(2).start()

        @pl.when((q == 2) & (i == 0))
        def _():
            weights(2).wait()
            arrive(2)
            weights(3).start()

        @pl.when((q == 3) & (i == 0))
        def _():
            weights(3).wait()

        @pl.when(q == 0)
        def _():
            xv = x_ref[...]
            r = lax.rsqrt(jnp.mean(xv * xv, axis=-1, keepdims=True) + EPS)
            h = xv * r * g_ref[...] * (1.0 + mod_ref[:, D:2 * D]) + mod_ref[:, 0:D]
            h_scr[rows, :] = h.astype(BF16)

        hb = h_scr[rows, :]
        slot = q % 2
        for k in range(wc // D):
            p_ref[:, k * D:(k + 1) * D] = _mm(hb, w_vmem[slot, k * D:(k + 1) * D, :]) + b_ref[:, k * D:(k + 1) * D]

        @pl.when((q == N_CHIPS - 1) & (i == nt - 1))
        def _():
            cp = pltpu.make_async_copy(h_scr, h_hbm, sems.at[0])
            cp.start()
            drain()
            finish2()
            own_block(0).wait()
            own_block(1).wait()
            cp.wait()

    hbm = BS(memory_space=pl.ANY)
    return pl.pallas_call(
        body, name="fwd_in", out_shape=(SDS((S, IN_COLS), F32), SDS((S, D), BF16), SDS(wg.shape, wg.dtype)),
        grid_spec=pltpu.PrefetchScalarGridSpec(
            num_scalar_prefetch=1, grid=(N_CHIPS, nt),
            in_specs=[BS((tmf, D), lambda q, i, o: (jnp.where(q == 0, i, nt - 1), 0)),
                      BS((1, 6 * D), lambda q, i, o: (0, 0)),
                      BS((1, D), lambda q, i, o: (0, 0)), hbm, BS((1, wc), lambda q, i, o: (0, o[q])), hbm],
            out_specs=(BS((tmf, wc), lambda q, i, o: (i, o[q])), hbm, hbm),
            scratch_shapes=[pltpu.VMEM((2, R_IN, D), BF16), pltpu.VMEM((S, D), BF16), pltpu.SemaphoreType.DMA((2,))]
            + _relay_sems() + [pltpu.SemaphoreType.DMA((8,)), pltpu.SemaphoreType.DMA((8,))]),
        input_output_aliases={4: 2},
        compiler_params=_params("arbitrary", "arbitrary"),
    )(order, x, mod, pre_tm, wg, b_in, pack)


def _lower_bound(lg_ref):
    l0, l1 = lg_ref[0:1, :], lg_ref[1:2, :]
    mx = jnp.maximum(l0, l1)
    e0, e1 = jnp.exp(l0 - mx), jnp.exp(l1 - mx)
    return e0 / (e0 + e1)


def _tri_masks():
    ri = lax.broadcasted_iota(jnp.int32, (CHUNK, CHUNK), 0)
    ci = lax.broadcasted_iota(jnp.int32, (CHUNK, CHUNK), 1)
    return (ri >= ci).astype(F32), (ci >= ri).astype(F32)


def _cumsum_mm(tri, g):
    tb = tri.astype(BF16)
    hi = g.astype(BF16)
    r1 = g - hi.astype(F32)
    mid = r1.astype(BF16)
    lo = (r1 - mid.astype(F32)).astype(BF16)
    return _mm(tb, hi) + _mm(tb, mid) + _mm(tb, lo)


def _hg_gates(q_r, f_r, lb, tril):
    sq = _sig(q_r)
    q = q_r * sq
    sf = _sig(f_r)
    f = lb + (1.0 - lb) * sf
    k = 1.0 - f
    g = jnp.log(f)
    b = _cumsum_mm(tril, g)
    b_last = _rowsum(g)
    row = lax.broadcasted_iota(jnp.int32, g.shape, 0)
    ref = _rowsum(jnp.where(row < CHUNK // 2, g, 0.0))
    e = jnp.exp(b)
    eq = jnp.exp(jnp.minimum(b - ref, 80.0))
    ek = jnp.exp(jnp.minimum(ref - b, 80.0))
    dd = jnp.exp(b_last - b)
    return dict(sq=sq, q=q, sf=sf, f=f, k=k, e=e, eq=eq, ek=ek, dd=dd, elast=jnp.exp(b_last),
                qe=q * e, qt=q * eq, kt=k * ek, kd=k * dd)


def _hgrn_fwd_call(p, logits, gn, wg, pack):
    S = p.shape[0]
    ncb = TB // CHUNK
    ranges = [(O_FF1, R_FF)]

    def body(q_ref, f_ref, v_ref, og_ref, lg_ref, gn_ref, wg_in, pack_ref, o_ref, oa_ref, st_ref, wg_out,
             st_scr, send_sems, recv_sems):
        start, finish = _pack_gather(pack_ref, wg_out, send_sems, recv_sems, ranges)

        @pl.when(pl.program_id(0) == 0)
        def _():
            start()
            st_scr[...] = jnp.zeros_like(st_scr)

        lb = _lower_bound(lg_ref)
        tril, _ = _tri_masks()

        def chunk(ci, carry):
            rows = pl.ds(pl.multiple_of(ci * CHUNK, CHUNK), CHUNK)
            st_ref[ci] = st_scr[...]
            t = _hg_gates(q_ref[rows, :], f_ref[rows, :], lb, tril)
            v = v_ref[rows, :]
            for h in range(HEADS):
                sl = slice(h * DK, (h + 1) * DK)
                stp = st_scr[:, sl]
                vb = v[:, sl].astype(BF16)
                inter = _mm(t["qe"][:, sl].astype(BF16), stp.astype(BF16), NT)
                a = jnp.where(tril > 0.5, _mm(t["qt"][:, sl].astype(BF16), t["kt"][:, sl].astype(BF16), NT), 0.0)
                o = inter + _mm(a.astype(BF16), vb)
                st_scr[:, sl] = stp * t["elast"][:, sl] + _mm(vb, t["kd"][:, sl].astype(BF16), TN)
                oh = o * lax.rsqrt(jnp.mean(o * o, axis=-1, keepdims=True) + EPS)
                og = og_ref[rows, sl]
                o_ref[rows, sl] = o
                oa_ref[rows, sl] = (oh * gn_ref[:, sl] * (og * _sig(og))).astype(BF16)
            return carry

        lax.fori_loop(0, ncb, chunk, 0)

        @pl.when(pl.program_id(0) == S // TB - 1)
        def _():
            finish()

    col = lambda j: BS((TB, D), lambda i, j=j: (i, j))
    hbm = BS(memory_space=pl.ANY)
    return pl.pallas_call(
        body, name="hgrn_fwd", grid=(S // TB,),
        out_shape=(SDS((S, D), F32), SDS((S, D), BF16), SDS((S // CHUNK, DK, D), F32), SDS(wg.shape, wg.dtype)),
        in_specs=[col(0), col(1), col(2), col(3), BS((2, D), lambda i: (0, 0)), BS((1, D), lambda i: (0, 0)),
                  hbm, hbm],
        out_specs=(BS((TB, D), lambda i: (i, 0)), BS((TB, D), lambda i: (i, 0)),
                   BS((ncb, DK, D), lambda i: (i, 0, 0)), hbm),
        scratch_shapes=[pltpu.VMEM((DK, D), F32)] + _gather_sems(len(ranges)),
        input_output_aliases={6: 3},
        compiler_params=_params("arbitrary"),
    )(p, p, p, p, logits, gn, wg, pack)


def _layernorm_stats(uc):
    mu = jnp.mean(uc, axis=-1, keepdims=True)
    xc = uc - mu
    rs = lax.rsqrt(jnp.mean(xc * xc, axis=-1, keepdims=True) + EPS)
    return xc * rs, rs


EXT = HALO + TM + 8


def _fill_shifted(ext, shifted):
    for m in range(1, 8):
        shifted[m - 1] = ext[m:m + HALO + TM, :]


def _window(ext, shifted, s0, n):
    m = s0 % 8
    q = s0 - m
    return ext[q:q + n, :] if m == 0 else shifted[m - 1, q:q + n, :]


def _conv_fwd_call(p, dw, db, ln_g, ln_b, wg, pack):
    S = p.shape[0]
    ranges = [(O_FF2, R_FF), (O_BRA, 2 * R_BR)]

    def body(cv_ref, cg_ref, dw_ref, db_ref, g_ref, b_ref, wg_in, pack_ref, u_ref, uc_ref, cb_ref, wg_out,
             uext, ush, send_sems, recv_sems):
        start, finish = _pack_gather(pack_ref, wg_out, send_sems, recv_sems, ranges)

        @pl.when(pl.program_id(0) == 0)
        def _():
            start()
            uext[0:HALO, :] = jnp.zeros((HALO, D), F32)
            uext[HALO + TM:EXT, :] = jnp.zeros((EXT - HALO - TM, D), F32)

        u = cv_ref[...] * _sig(cg_ref[...])
        uext[HALO:HALO + TM, :] = u
        u_ref[...] = u
        _fill_shifted(uext, ush)
        for rb in range(TM // SUB):
            acc = jnp.broadcast_to(db_ref[...], (SUB, D))
            for j in range(CONV_K):
                s0 = HALO - (CONV_K - 1) + j + rb * SUB
                acc = acc + dw_ref[j:j + 1, :] * _window(uext, ush, s0, SUB)
            uc_ref[rb * SUB:(rb + 1) * SUB, :] = acc
            xh, _ = _layernorm_stats(acc)
            ln = xh * g_ref[...] + b_ref[...]
            cb_ref[rb * SUB:(rb + 1) * SUB, :] = (ln * _sig(ln)).astype(BF16)
        uext[0:HALO, :] = uext[TM:TM + HALO, :]

        @pl.when(pl.program_id(0) == S // TM - 1)
        def _():
            finish()

    vec = BS((1, D), lambda i: (0, 0))
    hbm = BS(memory_space=pl.ANY)
    return pl.pallas_call(
        body, name="conv_fwd", grid=(S // TM,),
        out_shape=(SDS((S, D), F32), SDS((S, D), F32), SDS((S, D), BF16), SDS(wg.shape, wg.dtype)),
        in_specs=[BS((TM, D), lambda i: (i, 4)), BS((TM, D), lambda i: (i, 5)),
                  BS((CONV_K, D), lambda i: (0, 0)), vec, vec, vec, hbm, hbm],
        out_specs=(BS((TM, D), lambda i: (i, 0)),) * 3 + (hbm,),
        scratch_shapes=[pltpu.VMEM((EXT, D), F32), pltpu.VMEM((7, HALO + TM, D), F32)] + _gather_sems(len(ranges)),
        input_output_aliases={6: 3},
        compiler_params=_params("arbitrary"),
    )(p, p, dw, db, ln_g, ln_b, wg, pack)


def _mm_rows(a, w_ref):
    acc = _mm(a[:, 0:R_BR], w_ref[0])
    for k in range(1, N_CHIPS):
        acc = acc + _mm(a[:, k * R_BR:(k + 1) * R_BR], w_ref[k])
    return acc


def _mm_rows_t(a, w_ref):
    return jnp.concatenate([_mm(a, w_ref[k], NT) for k in range(N_CHIPS)], axis=1)


def _br_spec(off):
    return BS((N_CHIPS, R_BR, D), lambda i: (0, off // R_BR, 0))


def _merge_fwd_call(oa, cb, p, x, mod, post_tm, pre_cm, wg):
    S = x.shape[0]

    def body(oa_ref, cb_ref, ga_ref, gb_ref, x_ref, mod_ref, post_ref, pre_ref, wa_ref, wb_ref, wo_ref,
             ya_ref, yb_ref, mg_ref, y_ref, x2_ref, h2_ref):
        ya = _mm_rows(oa_ref[...], wa_ref)
        yb = _mm_rows(cb_ref[...], wb_ref)
        ya_ref[...] = ya.astype(BF16)
        yb_ref[...] = yb.astype(BF16)
        mg = (_sig(ga_ref[...]) * ya + _sig(gb_ref[...]) * yb).astype(BF16)
        mg_ref[...] = mg
        y = _mm_rows(mg, wo_ref)
        y_ref[...] = y
        n = y * lax.rsqrt(jnp.mean(y * y, axis=-1, keepdims=True) + EPS) * post_ref[...]
        x2 = x_ref[...] + mod_ref[:, 2 * D:3 * D] * n
        x2_ref[...] = x2
        r2 = lax.rsqrt(jnp.mean(x2 * x2, axis=-1, keepdims=True) + EPS)
        h2 = x2 * r2 * pre_ref[...] * (1.0 + mod_ref[:, 4 * D:5 * D]) + mod_ref[:, 3 * D:4 * D]
        h2_ref[...] = h2.astype(BF16)

    tile = BS((TM, D), lambda i: (i, 0))
    vec = BS((1, D), lambda i: (0, 0))
    return pl.pallas_call(
        body, name="merge_fwd", grid=(S // TM,),
        out_shape=(SDS((S, D), BF16), SDS((S, D), BF16), SDS((S, D), BF16), SDS((S, D), F32), SDS((S, D), F32),
                   SDS((S, D), BF16)),
        in_specs=[tile, tile, BS((TM, D), lambda i: (i, 6)), BS((TM, D), lambda i: (i, 7)), tile,
                  BS((1, 6 * D), lambda i: (0, 0)), vec, vec, _br_spec(O_BRA), _br_spec(O_BRB), _br_spec(O_OUT)],
        out_specs=(tile,) * 6,
        compiler_params=_params("arbitrary"),
    )(oa, cb, p, p, x, mod, post_tm, pre_cm, wg, wg, wg)


def _ffn_call(h2, x2, target, mod, post_cm, pre_cm, wg):
    S = x2.shape[0]

    def body(h2_ref, x2_ref, t_ref, mod_ref, post_ref, pre_ref, w_hbm,
             z_ref, da_ref, dy2_ref, dx2_ref, acc_ref, w1_v, w2_v, ra_scr, sems):
        @pl.when(pl.program_id(0) == 0)
        def _():
            c1 = _load_rows(w_hbm, w1_v, sems.at[0], O_FF1)
            c2 = _load_rows(w_hbm, w2_v, sems.at[1], O_FF2)
            c1.wait()
            c2.wait()
            acc_ref[...] = jnp.zeros_like(acc_ref)

        h2 = h2_ref[...]
        for k in range(N_CHIPS):
            ra = jnp.maximum(_mm(h2, w1_v[k]), 0.0)
            ra_scr[:, k * D:(k + 1) * D] = ra
            z_ref[:, k * D:(k + 1) * D] = (ra * ra).astype(BF16)
        y2 = _mm(z_ref[:, 0:D], w2_v[0])
        for k in range(1, N_CHIPS):
            y2 = y2 + _mm(z_ref[:, k * D:(k + 1) * D], w2_v[k])
        ry = lax.rsqrt(jnp.mean(y2 * y2, axis=-1, keepdims=True) + EPS)
        yn = y2 * ry
        n = yn * post_ref[...]
        g2 = mod_ref[:, 5 * D:6 * D]
        x2 = x2_ref[...]
        err = x2 + g2 * n - t_ref[...]
        acc_ref[5:6, :] += _rowsum(err * err) * (0.5 / D)
        dout = err * (1.0 / D)
        acc_ref[0:1, :] += _rowsum(dout * n)
        dn = dout * g2
        acc_ref[1:2, :] += _rowsum(dn * yn)
        dyn = dn * post_ref[...]
        dy2 = (ry * (dyn - yn * jnp.mean(dyn * yn, axis=-1, keepdims=True))).astype(BF16)
        dy2_ref[...] = dy2
        for k in range(N_CHIPS):
            dz = _mm(dy2, w2_v[k], NT)
            da_ref[:, k * D:(k + 1) * D] = (dz * (2.0 * ra_scr[:, k * D:(k + 1) * D])).astype(BF16)
        dh2 = jnp.zeros((TM, D), F32)
        for k in range(N_CHIPS):
            dh2 = dh2 + _mm(da_ref[:, k * D:(k + 1) * D], w1_v[k], NT)
        r2 = lax.rsqrt(jnp.mean(x2 * x2, axis=-1, keepdims=True) + EPS)
        xn = x2 * r2
        yv = xn * pre_ref[...]
        acc_ref[2:3, :] += _rowsum(dh2)
        acc_ref[3:4, :] += _rowsum(dh2 * yv)
        dyv = dh2 * (1.0 + mod_ref[:, 4 * D:5 * D])
        acc_ref[4:5, :] += _rowsum(dyv * xn)
        dxn = dyv * pre_ref[...]
        dx2_ref[...] = dout + r2 * (dxn - xn * jnp.mean(dxn * xn, axis=-1, keepdims=True))

    tile = BS((TM, D), lambda i: (i, 0))
    wide = BS((TM, D_FF), lambda i: (i, 0))
    vec = BS((1, D), lambda i: (0, 0))
    return pl.pallas_call(
        body, name="ffn_fwd_bwd", grid=(S // TM,),
        out_shape=(SDS((S, D_FF), BF16), SDS((S, D_FF), BF16), SDS((S, D), BF16), SDS((S, D), F32),
                   SDS((8, D), F32)),
        in_specs=[tile, tile, tile, BS((1, 6 * D), lambda i: (0, 0)), vec, vec, BS(memory_space=pl.ANY)],
        out_specs=(wide, wide, tile, tile, BS((8, D), lambda i: (0, 0))),
        scratch_shapes=[pltpu.VMEM((N_CHIPS, R_FF, D), BF16), pltpu.VMEM((N_CHIPS, R_FF, D), BF16),
                        pltpu.VMEM((TM, D_FF), F32),
                        pltpu.SemaphoreType.DMA((2,))],
        compiler_params=_params("arbitrary"),
    )(h2, x2, target, mod, post_cm, pre_cm, wg)


def _merge_bwd_call(dx2, y, ya, yb, p, mod, post_tm, wg, g):
    S = y.shape[0]

    def body(dx2_ref, y_ref, ya_ref, yb_ref, ga_ref, gb_ref, mod_ref, post_ref, wa_ref, wb_ref, wo_ref, g_ref,
             dy_ref, dya_ref, dyb_ref, doa_ref, dcb_ref, dpg_ref, acc_ref, bsum_ref, hr_ref, send_sems, recv_sems):
        start, finish = _halves_exchange(g_ref, hr_ref, send_sems, recv_sems)

        @pl.when(pl.program_id(0) == 0)
        def _():
            start()
            acc_ref[...] = jnp.zeros_like(acc_ref)
            bsum_ref[...] = jnp.zeros_like(bsum_ref)

        y = y_ref[...]
        ry = lax.rsqrt(jnp.mean(y * y, axis=-1, keepdims=True) + EPS)
        yn = y * ry
        dx2 = dx2_ref[...]
        acc_ref[0:1, :] += _rowsum(dx2 * (yn * post_ref[...]))
        dn = dx2 * mod_ref[:, 2 * D:3 * D]
        acc_ref[1:2, :] += _rowsum(dn * yn)
        dyn = dn * post_ref[...]
        dy = (ry * (dyn - yn * jnp.mean(dyn * yn, axis=-1, keepdims=True))).astype(BF16)
        dy_ref[...] = dy
        dmg = _mm_rows_t(dy, wo_ref)
        sa, sb = _sig(ga_ref[...]), _sig(gb_ref[...])
        dya = (dmg * sa).astype(BF16)
        dyb = (dmg * sb).astype(BF16)
        dya_ref[...] = dya
        dyb_ref[...] = dyb
        dga = dmg * ya_ref[...].astype(F32) * (sa * (1.0 - sa))
        dgb = dmg * yb_ref[...].astype(F32) * (sb * (1.0 - sb))
        dpg_ref[:, 0:D] = dga.astype(BF16)
        dpg_ref[:, D:2 * D] = dgb.astype(BF16)
        bsum_ref[:, 0:D] += _rowsum(dga)
        bsum_ref[:, D:2 * D] += _rowsum(dgb)
        doa_ref[...] = _mm_rows_t(dya, wa_ref)
        dcb_ref[...] = _mm_rows_t(dyb, wb_ref)

        @pl.when(pl.program_id(0) == S // TM - 1)
        def _():
            finish()

    tile = BS((TM, D), lambda i: (i, 0))
    vec = BS((1, D), lambda i: (0, 0))
    return pl.pallas_call(
        body, name="merge_bwd", grid=(S // TM,),
        out_shape=(SDS((S, D), BF16), SDS((S, D), BF16), SDS((S, D), BF16), SDS((S, D), F32), SDS((S, D), F32),
                   SDS((S, 2 * D), BF16), SDS((8, D), F32), SDS((1, 2 * D), F32),
                   SDS((N_CHIPS,) + g.shape[2:], g.dtype)),
        in_specs=[tile, tile, tile, tile, BS((TM, D), lambda i: (i, 6)), BS((TM, D), lambda i: (i, 7)),
                  BS((1, 6 * D), lambda i: (0, 0)), vec, _br_spec(O_BRA), _br_spec(O_BRB), _br_spec(O_OUT),
                  BS(memory_space=pl.ANY)],
        out_specs=(tile, tile, tile, tile, tile, BS((TM, 2 * D), lambda i: (i, 0)),
                   BS((8, D), lambda i: (0, 0)), BS((1, 2 * D), lambda i: (0, 0)), BS(memory_space=pl.ANY)),
        scratch_shapes=_halves_sems(),
        compiler_params=_params("arbitrary"),
    )(dx2, y, ya, yb, p, p, mod, post_tm, wg, wg, wg, g)


def _hgrn_bwd_call(p, o, doa, st, logits, gn, part, g):
    S = p.shape[0]
    nb = S // TB
    ncb = TB // CHUNK

    def body(q_ref, f_ref, v_ref, og_ref, o_ref, doa_ref, st_ref, lg_ref, gn_ref, part_ref, g_ref,
             dp_ref, bsum_ref, dlg_ref, dgn_ref, recv_ref, hr_ref,
             dst_scr, dlb_scr, dqe_s, dqt_s, dkt_s, dkd_s, dv_s, dog_s, dble_s, send_sems, recv_sems, hs, hr):
        i = pl.program_id(0)
        start, finish = _chip_exchange(part_ref, recv_ref, send_sems, recv_sems)
        start_h, finish_h = _halves_exchange(g_ref, hr_ref, hs, hr)

        @pl.when(i == 0)
        def _():
            start_h()
            start()
            dst_scr[...] = jnp.zeros_like(dst_scr)
            dlb_scr[...] = jnp.zeros_like(dlb_scr)
            bsum_ref[...] = jnp.zeros_like(bsum_ref)
            dgn_ref[...] = jnp.zeros_like(dgn_ref)

        lb = _lower_bound(lg_ref)
        tril, triu = _tri_masks()

        def chunk(tt, carry):
            ci = ncb - 1 - tt
            rows = pl.ds(pl.multiple_of(ci * CHUNK, CHUNK), CHUNK)
            q_r, f_r = q_ref[rows, :], f_ref[rows, :]
            t = _hg_gates(q_r, f_r, lb, tril)
            v = v_ref[rows, :]
            for h in range(HEADS):
                sl = slice(h * DK, (h + 1) * DK)
                stp = st_ref[ci, :, sl]
                stb = stp.astype(BF16)
                qeb = t["qe"][:, sl].astype(BF16)
                qtb = t["qt"][:, sl].astype(BF16)
                ktb = t["kt"][:, sl].astype(BF16)
                kdb = t["kd"][:, sl].astype(BF16)
                vb = v[:, sl].astype(BF16)
                a = jnp.where(tril > 0.5, _mm(qtb, ktb, NT), 0.0)
                o_h = o_ref[rows, sl]
                rinv = lax.rsqrt(jnp.mean(o_h * o_h, axis=-1, keepdims=True) + EPS)
                oh = o_h * rinv
                og = og_ref[rows, sl]
                so = _sig(og)
                d_oa = doa_ref[rows, sl]
                don = d_oa * (og * so)
                dog_s[:, sl] = d_oa * (oh * gn_ref[:, sl]) * _dsilu(og, so)
                dgn_ref[:, sl] += _rowsum(don * oh)
                doh = don * gn_ref[:, sl]
                do = (rinv * (doh - oh * jnp.mean(doh * oh, axis=-1, keepdims=True))).astype(BF16)
                dqe_s[:, sl] = _mm(do, stb, NN)
                dstp = _mm(do, qeb, TN)
                dab = jnp.where(tril > 0.5, _mm(do, vb, NT), 0.0).astype(BF16)
                dqt_s[:, sl] = _mm(dab, ktb, NN)
                dkt_s[:, sl] = _mm(dab, qtb, TN)
                dstn = dst_scr[:, sl]
                dsb = dstn.astype(BF16)
                dkd_s[:, sl] = _mm(vb, dsb, NN)
                dv_s[:, sl] = _mm(a.astype(BF16), do, TN) + _mm(kdb, dsb, NT)
                el = t["elast"][:, sl]
                dst_scr[:, sl] = dstn * el + dstp
                dble_s[:, sl] = el * _rowsum(stp * dstn)
            dqe, dqt, dkt, dkd = dqe_s[...], dqt_s[...], dkt_s[...], dkd_s[...]
            dq = dqe * t["e"] + dqt * t["eq"]
            dk = dkt * t["ek"] + dkd * t["dd"]
            dkk = dkd * t["kd"]
            qt_r = t["qt"].astype(BF16).astype(F32)
            kt_r = t["kt"].astype(BF16).astype(F32)
            dbv = dqe * t["qe"] + dqt * qt_r - dkt * kt_r - dkk
            dg = _cumsum_mm(triu, dbv) + (_rowsum(dkk) + dble_s[...])
            df = dg / t["f"] - dk
            sf = t["sf"]
            dlb_scr[...] += _rowsum(df * (1.0 - sf))
            dqr = dq * _dsilu(q_r, t["sq"])
            dfr = df * (1.0 - lb) * (sf * (1.0 - sf))
            dvv, dog = dv_s[...], dog_s[...]
            dp_ref[rows, 0:D] = dqr.astype(BF16)
            dp_ref[rows, D:2 * D] = dfr.astype(BF16)
            dp_ref[rows, 2 * D:3 * D] = dvv.astype(BF16)
            dp_ref[rows, 3 * D:4 * D] = dog.astype(BF16)
            bsum_ref[:, 0:D] += _rowsum(dqr)
            bsum_ref[:, D:2 * D] += _rowsum(dfr)
            bsum_ref[:, 2 * D:3 * D] += _rowsum(dvv)
            bsum_ref[:, 3 * D:4 * D] += _rowsum(dog)
            return carry

        lax.fori_loop(0, ncb, chunk, 0)

        dl = dlb_scr[...] * lb * (1.0 - lb)
        dlg_ref[0:1, :] = dl
        dlg_ref[1:2, :] = -dl

        @pl.when(i == nb - 1)
        def _():
            finish_h()
            finish()

    col = lambda j: BS((TB, D), lambda i, j=j: (nb - 1 - i, j))
    rev = BS((TB, D), lambda i: (nb - 1 - i, 0))
    cd = pltpu.VMEM((CHUNK, D), F32)
    return pl.pallas_call(
        body, name="hgrn_bwd", grid=(nb,),
        out_shape=(SDS((S, 4 * D), BF16), SDS((1, 4 * D), F32), SDS((2, D), F32), SDS((1, D), F32),
                   SDS((3,) + part.shape[1:], part.dtype), SDS((N_CHIPS,) + g.shape[2:], g.dtype)),
        in_specs=[col(0), col(1), col(2), col(3), rev, rev, BS((ncb, DK, D), lambda i: (nb - 1 - i, 0, 0)),
                  BS((2, D), lambda i: (0, 0)), BS((1, D), lambda i: (0, 0)), BS(memory_space=pl.ANY),
                  BS(memory_space=pl.ANY)],
        out_specs=(BS((TB, 4 * D), lambda i: (nb - 1 - i, 0)), BS((1, 4 * D), lambda i: (0, 0)),
                   BS((2, D), lambda i: (0, 0)), BS((1, D), lambda i: (0, 0)), BS(memory_space=pl.ANY),
                   BS(memory_space=pl.ANY)),
        scratch_shapes=[pltpu.VMEM((DK, D), F32), pltpu.VMEM((1, D), F32), cd, cd, cd, cd, cd, cd,
                        pltpu.VMEM((1, D), F32)] + _exchange_sems() + _halves_sems(),
        compiler_params=_params("arbitrary"),
    )(p, p, p, p, o, doa, st, logits, gn, part, g)


def _conv_bwd_call(dcb, uc, u, p, dw, ln_g, ln_b, part):
    S = uc.shape[0]
    nb = S // TM
    hb = TM // HALO

    def body(dcb_ref, uc_ref, u_ref, uh_ref, cv_ref, cg_ref, dw_ref, g_ref, b_ref, part_ref,
             dp_ref, bsum_ref, ddw_ref, acc_ref, recv_ref, uext, dext, ush, dsh, send_sems, recv_sems):
        i = pl.program_id(0)
        start, finish = _chip_exchange(part_ref, recv_ref, send_sems, recv_sems)

        @pl.when(i == 0)
        def _():
            start()
            dext[TM:EXT, :] = jnp.zeros((EXT - TM, D), F32)
            uext[HALO + TM:EXT, :] = jnp.zeros((EXT - HALO - TM, D), F32)
            bsum_ref[...] = jnp.zeros_like(bsum_ref)
            ddw_ref[...] = jnp.zeros_like(ddw_ref)
            acc_ref[...] = jnp.zeros_like(acc_ref)

        first_tile = (nb - 1 - i) == 0
        uext[0:HALO, :] = jnp.where(first_tile, 0.0, uh_ref[...])
        uext[HALO:HALO + TM, :] = u_ref[...]
        _fill_shifted(uext, ush)

        for rb in range(TM // SUB):
            rs_ = slice(rb * SUB, (rb + 1) * SUB)
            xh, rs = _layernorm_stats(uc_ref[rs_, :])
            ln = xh * g_ref[...] + b_ref[...]
            dln = dcb_ref[rs_, :] * _dsilu(ln, _sig(ln))
            acc_ref[1:2, :] += _rowsum(dln * xh)
            acc_ref[2:3, :] += _rowsum(dln)
            dxh = dln * g_ref[...]
            duc = rs * (dxh - jnp.mean(dxh, axis=-1, keepdims=True)
                        - xh * jnp.mean(dxh * xh, axis=-1, keepdims=True))
            dext[rs_, :] = duc
            acc_ref[0:1, :] += _rowsum(duc)
        _fill_shifted(dext, dsh)

        for j in range(CONV_K):
            part = jnp.zeros((SUB, D), F32)
            for rb in range(TM // SUB):
                s0 = HALO - (CONV_K - 1) + j + rb * SUB
                part = part + dext[rb * SUB:(rb + 1) * SUB, :] * _window(uext, ush, s0, SUB)
            ddw_ref[j:j + 1, :] += _rowsum(part)

        for rb in range(TM // SUB):
            rs_ = slice(rb * SUB, (rb + 1) * SUB)
            du = jnp.zeros((SUB, D), F32)
            for j in range(CONV_K):
                s0 = rb * SUB + (CONV_K - 1) - j
                du = du + dw_ref[j:j + 1, :] * _window(dext, dsh, s0, SUB)
            cg = cg_ref[rs_, :]
            sg = _sig(cg)
            dcv = du * sg
            dcg = du * cv_ref[rs_, :] * (sg * (1.0 - sg))
            dp_ref[rs_, 0:D] = dcv.astype(BF16)
            dp_ref[rs_, D:2 * D] = dcg.astype(BF16)
            bsum_ref[:, 0:D] += _rowsum(dcv)
            bsum_ref[:, D:2 * D] += _rowsum(dcg)

        dext[TM:TM + HALO, :] = dext[0:HALO, :]

        @pl.when(i == nb - 1)
        def _():
            finish()

    rev = BS((TM, D), lambda i: (nb - 1 - i, 0))
    vec = BS((1, D), lambda i: (0, 0))
    return pl.pallas_call(
        body, name="conv_bwd", grid=(nb,),
        out_shape=(SDS((S, 2 * D), BF16), SDS((1, 2 * D), F32), SDS((32, D), F32), SDS((8, D), F32),
                   SDS((3,) + part.shape[1:], part.dtype)),
        in_specs=[rev, rev, rev, BS((HALO, D), lambda i: (jnp.maximum((nb - 1 - i) * hb - 1, 0), 0)),
                  BS((TM, D), lambda i: (nb - 1 - i, 4)), BS((TM, D), lambda i: (nb - 1 - i, 5)),
                  BS((CONV_K, D), lambda i: (0, 0)), vec, vec, BS(memory_space=pl.ANY)],
        out_specs=(BS((TM, 2 * D), lambda i: (nb - 1 - i, 0)), BS((1, 2 * D), lambda i: (0, 0)),
                   BS((32, D), lambda i: (0, 0)), BS((8, D), lambda i: (0, 0)), BS(memory_space=pl.ANY)),
        scratch_shapes=[pltpu.VMEM((EXT, D), F32), pltpu.VMEM((EXT, D), F32),
                        pltpu.VMEM((7, HALO + TM, D), F32), pltpu.VMEM((7, HALO + TM, D), F32)] + _exchange_sems(),
        compiler_params=_params("arbitrary"),
    )(dcb, uc, u, u, p, p, dw, ln_g, ln_b, part)


def _in_bwd_call(dp_hg, dp_cv, dp_gt, x, dx2, mod, pre_tm, wg, part, full_a, full_b):
    S = x.shape[0]
    tm = TM

    def body(hg_ref, cv_ref, gt_ref, x_ref, dx2_ref, mod_ref, g_ref, w_hbm, part_ref, fa_in, fb_in,
             gx_ref, acc_ref, recv_ref, fa_out, fb_out, w_vmem, sem, send_sems, recv_sems, sa, ra, sb, rb):
        start, finish = _chip_exchange(part_ref, recv_ref, send_sems, recv_sems)
        start_a, finish_a = _join_exchange(fa_in, fa_out, sa, ra)
        start_b, finish_b = _join_exchange(fb_in, fb_out, sb, rb)

        @pl.when(pl.program_id(0) == 0)
        def _():
            start_a()
            start_b()
            start()
            _load_rows(w_hbm, w_vmem, sem, O_IN).wait()
            acc_ref[...] = jnp.zeros_like(acc_ref)

        dh = jnp.zeros((tm, D), F32)
        for k in range(IN_COLS // D):
            src, kk = ((hg_ref, k), (cv_ref, k - 4), (gt_ref, k - 6))[0 if k < 4 else (1 if k < 6 else 2)]
            dh = dh + _mm(src[:, kk * D:(kk + 1) * D], w_vmem[k // 2, (k % 2) * D:(k % 2 + 1) * D, :], NT)
        xv = x_ref[...]
        r = lax.rsqrt(jnp.mean(xv * xv, axis=-1, keepdims=True) + EPS)
        xn = xv * r
        yv = xn * g_ref[...]
        acc_ref[0:1, :] += _rowsum(dh)
        acc_ref[1:2, :] += _rowsum(dh * yv)
        dyv = dh * (1.0 + mod_ref[:, D:2 * D])
        acc_ref[2:3, :] += _rowsum(dyv * xn)
        dxn = dyv * g_ref[...]
        gx_ref[...] = dx2_ref[...] + r * (dxn - xn * jnp.mean(dxn * xn, axis=-1, keepdims=True))

        @pl.when(pl.program_id(0) == S // tm - 1)
        def _():
            finish_a()
            finish_b()
            finish()

    tile = BS((tm, D), lambda i: (i, 0))
    hbm = BS(memory_space=pl.ANY)
    return pl.pallas_call(
        body, name="in_bwd", grid=(S // tm,),
        out_shape=(SDS((S, D), F32), SDS((8, D), F32), SDS((3,) + part.shape[1:], part.dtype),
                   SDS(full_a.shape, full_a.dtype), SDS(full_b.shape, full_b.dtype)),
        in_specs=[BS((tm, 4 * D), lambda i: (i, 0)), BS((tm, 2 * D), lambda i: (i, 0)),
                  BS((tm, 2 * D), lambda i: (i, 0)), tile, tile, BS((1, 6 * D), lambda i: (0, 0)),
                  BS((1, D), lambda i: (0, 0)), hbm, hbm, hbm, hbm],
        out_specs=(tile, BS((8, D), lambda i: (0, 0)), hbm, hbm, hbm),
        scratch_shapes=[pltpu.VMEM((N_CHIPS, R_IN, D), BF16), pltpu.SemaphoreType.DMA] + _exchange_sems()
        + _join_sems() + _join_sems(),
        input_output_aliases={9: 3, 10: 4},
        compiler_params=_params("arbitrary"),
    )(dp_hg, dp_cv, dp_gt, x, dx2, mod, pre_tm, wg, part, full_a, full_b)


def _wgrad_call(gp, a, b, name, bm, place, rows):
    S, M = a.shape
    N = b.shape[1]
    bk = min(S, 1024)
    nk = S // bk

    def body(a_ref, b_ref, *rest):
        o_ref, acc = rest[-2], rest[-1]
        k = pl.program_id(2)

        @pl.when(k == 0)
        def _():
            acc[...] = jnp.zeros_like(acc)

        acc[...] += _mm(a_ref[...], b_ref[...], TN)

        @pl.when(k == nk - 1)
        def _():
            o_ref[...] = acc[...].astype(BF16)

    in_specs = [BS((bk, bm), lambda i, j, k: (k, i)), BS((bk, D), lambda i, j, k: (k, j))]
    args = [a, b]
    if gp is not None:
        in_specs.append(BS(memory_space=pl.ANY))
        args.append(gp)
    return pl.pallas_call(
        body, name=name, grid=(M // bm, N // D, nk),
        out_shape=SDS((N_CHIPS, rows, D), BF16),
        in_specs=in_specs,
        out_specs=BS((None, bm, D), lambda i, j, k: (*place(i, j), 0)),
        scratch_shapes=[pltpu.VMEM((bm, D), F32)],
        input_output_aliases={} if gp is None else {2: 0},
        compiler_params=_params("parallel", "parallel", "arbitrary"),
    )(*args)


def _wgrad_rows_call(gp, a, b, name, blk):
    S = a.shape[0]
    bk = min(S, 1024)
    nk = S // bk

    def body(a_ref, b_ref, *rest):
        o_ref, acc = rest[-2], rest[-1]
        k = pl.program_id(0)

        @pl.when(k == 0)
        def _():
            acc[...] = jnp.zeros_like(acc)

        acc[...] += _mm(a_ref[...], b_ref[...], TN)

        @pl.when(k == nk - 1)
        def _():
            for c in range(N_CHIPS):
                o_ref[c] = acc[c * R_BR:(c + 1) * R_BR, :].astype(BF16)

    in_specs = [BS((bk, D), lambda k: (k, 0)), BS((bk, D), lambda k: (k, 0))]
    args = [a, b]
    if gp is not None:
        in_specs.append(BS(memory_space=pl.ANY))
        args.append(gp)
    return pl.pallas_call(
        body, name=name, grid=(nk,),
        out_shape=SDS((N_CHIPS, 3 * R_BR, D), BF16),
        in_specs=in_specs,
        out_specs=BS((N_CHIPS, R_BR, D), lambda k: (0, blk, 0)),
        scratch_shapes=[pltpu.VMEM((D, D), F32)],
        input_output_aliases={} if gp is None else {2: 0},
        compiler_params=_params("arbitrary"),
    )(*args)


def _outer_call(cact, dmod):
    n = dmod.shape[1]

    def body(a_ref, b_ref, o_ref):
        o_ref[...] = _mm(a_ref[...], b_ref[...], TN, HI)

    return pl.pallas_call(
        body, name="wgrad_ada", out_shape=SDS((D, n), F32),
        compiler_params=pltpu.CompilerParams(vmem_limit_bytes=VMEM_LIMIT),
    )(cact, dmod)


def _adamw_call(w, g, m, v, name):
    R, C = w.shape
    tr = R
    while tr * C > 512 * 1024 and tr % 16 == 0:
        tr //= 2
    c1 = 1.0 - ADAM_B1 ** ADAM_STEP
    c2 = 1.0 - ADAM_B2 ** ADAM_STEP

    def body(w_ref, g_ref, m_ref, v_ref, d_ref, m2_ref, v2_ref):
        g = g_ref[...]
        m2 = ADAM_B1 * m_ref[...] + (1.0 - ADAM_B1) * g
        v2 = ADAM_B2 * v_ref[...] + (1.0 - ADAM_B2) * (g * g)
        m2_ref[...] = m2
        v2_ref[...] = v2
        d_ref[...] = -ADAM_LR * ((m2 / c1) / (jnp.sqrt(v2 / c2) + ADAM_EPS) + ADAM_WD * w_ref[...])

    tile = BS((tr, C), lambda i: (i, 0))
    return pl.pallas_call(
        body, name=name, grid=(R // tr,), out_shape=(SDS((R, C), F32),) * 3,
        in_specs=[tile] * 4, out_specs=(tile,) * 3, compiler_params=_params("parallel"),
    )(w, g, m, v)


def _adamw_rows_call(ws, g, ms, vs, name):
    k = len(ws)
    r = ws[0].shape[0]
    c1 = 1.0 - ADAM_B1 ** ADAM_STEP
    c2 = 1.0 - ADAM_B2 ** ADAM_STEP

    def body(g_ref, *refs):
        ins, outs = refs[:3 * k], refs[3 * k:]
        for j in range(k):
            w_ref, m_ref, v_ref = ins[j], ins[k + j], ins[2 * k + j]
            d_ref, m2_ref, v2_ref = outs[j], outs[k + j], outs[2 * k + j]
            g = g_ref[j * r:(j + 1) * r, :]
            m2 = ADAM_B1 * m_ref[...] + (1.0 - ADAM_B1) * g
            v2 = ADAM_B2 * v_ref[...] + (1.0 - ADAM_B2) * (g * g)
            m2_ref[...] = m2
            v2_ref[...] = v2
            d_ref[...] = -ADAM_LR * ((m2 / c1) / (jnp.sqrt(v2 / c2) + ADAM_EPS) + ADAM_WD * w_ref[...])

    out = pl.pallas_call(
        body, name=name, out_shape=(SDS(ws[0].shape, F32),) * (3 * k),
        compiler_params=pltpu.CompilerParams(vmem_limit_bytes=VMEM_LIMIT),
    )(g, *ws, *ms, *vs)
    return out[:k], out[k:2 * k], out[2 * k:]


def _adamw_small_call(ws, ms, vs, row0, ssum, g_dw):
    n = len(ws)
    c1 = 1.0 - ADAM_B1 ** ADAM_STEP
    c2 = 1.0 - ADAM_B2 ** ADAM_STEP

    def adam(w, g, m, v):
        m2 = ADAM_B1 * m + (1.0 - ADAM_B1) * g
        v2 = ADAM_B2 * v + (1.0 - ADAM_B2) * (g * g)
        return -ADAM_LR * ((m2 / c1) / (jnp.sqrt(v2 / c2) + ADAM_EPS) + ADAM_WD * w), m2, v2

    def body(s_ref, gdw_ref, *refs):
        ins, outs = refs[:3 * n], refs[3 * n:]
        for j in range(n):
            w_ref, m_ref, v_ref = ins[j], ins[n + j], ins[2 * n + j]
            g_ref, d_ref, m2_ref, v2_ref = outs[j], outs[n + j], outs[2 * n + j], outs[3 * n + j]
            if j == n - 1:
                pieces = [(slice(None), slice(None), gdw_ref[...])]
            elif w_ref.shape[0] == 1:
                pieces = [(slice(None), slice(i * D, (i + 1) * D), s_ref[row0[j] + i:row0[j] + i + 1, :])
                          for i in range(w_ref.shape[1] // D)]
            else:
                pieces = [(slice(None), slice(None), s_ref[row0[j]:row0[j] + w_ref.shape[0], :])]
            for rs, cs, g in pieces:
                d, m2, v2 = adam(w_ref[rs, cs], g, m_ref[rs, cs], v_ref[rs, cs])
                g_ref[rs, cs] = g
                d_ref[rs, cs] = d
                m2_ref[rs, cs] = m2
                v2_ref[rs, cs] = v2

    out = pl.pallas_call(
        body, name="adamw_small", out_shape=tuple(SDS(w.shape, F32) for w in ws) * 4,
        compiler_params=pltpu.CompilerParams(vmem_limit_bytes=VMEM_LIMIT),
    )(ssum, g_dw, *ws, *ms, *vs)
    return [(out[j], out[n + j], out[2 * n + j], out[3 * n + j]) for j in range(n)]


def _adamw_halves_call(w, g, m, v, name):
    R, C = w.shape
    tr = R
    while tr * C > 512 * 1024 and tr % 16 == 0:
        tr //= 2
    c1 = 1.0 - ADAM_B1 ** ADAM_STEP
    c2 = 1.0 - ADAM_B2 ** ADAM_STEP

    def body(w_ref, g_ref, m_ref, v_ref, go_ref, d_ref, m2_ref, v2_ref):
        for k in range(2):
            cs = slice(k * D, (k + 1) * D)
            g = g_ref[k]
            go_ref[:, cs] = g
            m2 = ADAM_B1 * m_ref[:, cs] + (1.0 - ADAM_B1) * g
            v2 = ADAM_B2 * v_ref[:, cs] + (1.0 - ADAM_B2) * (g * g)
            m2_ref[:, cs] = m2
            v2_ref[:, cs] = v2
            d_ref[:, cs] = -ADAM_LR * ((m2 / c1) / (jnp.sqrt(v2 / c2) + ADAM_EPS) + ADAM_WD * w_ref[:, cs])

    tile = BS((tr, C), lambda i: (i, 0))
    return pl.pallas_call(
        body, name=name, grid=(R // tr,), out_shape=(SDS((R, C), F32),) * 4,
        in_specs=[tile, BS((2, tr, D), lambda i: (0, i, 0)), tile, tile], out_specs=(tile,) * 4,
        compiler_params=_params("parallel"),
    )(w, g, m, v)


def _rs_begin(g, c_idx, tag):
    n = g.shape[1]
    g = g.reshape(N_CHIPS, 2, n // 2, D)
    return _add_halves_call(g, _sibling_halves_call(g, tag), c_idx, tag)


def _local_step(x, mod, cact, target, wg, pack, small, c_idx, chip_idx):
    p, h1, wg = _fwd_in_call(x, mod, small["pre_tm"], wg, small["b_in"], pack, small["order"])
    o, oa, st, wg = _hgrn_fwd_call(p, small["logits"], small["hg_norm"], wg, pack)
    u, uc, cb, wg = _conv_fwd_call(p, small["conv_dw"], small["conv_db"], small["ln_g"], small["ln_b"], wg, pack)
    ya, yb, mg, y, x2, h2 = _merge_fwd_call(oa, cb, p, x, mod, small["post_tm"], small["pre_cm"], wg)
    z, da, dy2, dx2, acc_f = _ffn_call(h2, x2, target, mod, small["post_cm"], small["pre_cm"], wg)

    g_ff = _wgrad_call(None, h2, da, "wgrad_ff1", D, lambda i, j: (j, 0), 2 * R_FF)
    g_ff = _wgrad_call(g_ff, z, dy2, "wgrad_ff2", D, lambda i, j: (i, 1), 2 * R_FF)
    g_ff = g_ff.reshape(N_CHIPS, 2, R_FF, D)
    dy, dya, dyb, doa, dcb, dp_gt, acc_m, bs_gt, hr_ff = _merge_bwd_call(dx2, y, ya, yb, p, mod, small["post_tm"],
                                                                        wg, g_ff)
    part_ff = _add_halves_call(g_ff, hr_ff, c_idx, "ff")

    g_br = _wgrad_rows_call(None, oa, dya, "wgrad_br_a", 0)
    g_br = _wgrad_rows_call(g_br, cb, dyb, "wgrad_br_b", 1)
    g_br = _wgrad_rows_call(g_br, mg, dy, "wgrad_out", 2)
    g_br = g_br.reshape(N_CHIPS, 2, 3 * R_BR // 2, D)
    dp_hg, bs_hg, dlg, dgn, recv_ff, hr_br = _hgrn_bwd_call(p, o, doa, st, small["logits"], small["hg_norm"],
                                                            part_ff, g_br)
    part_br = _add_halves_call(g_br, hr_br, c_idx, "br")
    dp_cv, bs_cv, ddw, acc_c, recv_br = _conv_bwd_call(dcb, uc, u, p, small["conv_dw"], small["ln_g"], small["ln_b"],
                                                        part_br)

    g_in = _wgrad_call(None, h1, dp_hg, "wgrad_in_hg", D, lambda i, j: (j // 2, j % 2), R_IN)
    g_in = _wgrad_call(g_in, h1, dp_cv, "wgrad_in_cv", D, lambda i, j: (2, j), R_IN)
    g_in = _wgrad_call(g_in, h1, dp_gt, "wgrad_in_gt", D, lambda i, j: (3, j), R_IN)
    part_in = _rs_begin(g_in, c_idx, "in")
    chip_c = jnp.concatenate([chip_idx, c_idx])
    full_ff = _add_chips_call(part_ff, recv_ff, chip_c, "ff")
    full_br = _add_chips_call(part_br, recv_br, chip_c, "br")
    gx, acc_i, recv_in, full_ff, full_br = _in_bwd_call(dp_hg, dp_cv, dp_gt, x, dx2, mod, small["pre_tm"], wg,
                                                        part_in, full_ff, full_br)
    red_ff = full_ff.reshape(2 * R_FF, D)
    red_br = full_br.reshape(3 * R_BR, D)
    full_in = _add_chips_call(part_in, recv_in, chip_c, "in")

    zrow = jnp.zeros((1, D), F32)
    rows = [acc_i[0:1], acc_i[1:2], acc_m[0:1], acc_f[2:3], acc_f[3:4], acc_f[0:1],
            acc_i[2:3], acc_m[1:2], acc_f[4:5], acc_f[1:2],
            jnp.concatenate([bs_hg, bs_cv, bs_gt], axis=1).reshape(8, D),
            dlg, dgn, acc_c[0:1], acc_c[1:2], acc_c[2:3],
            ddw,
            cact, acc_f[5:6]] + [zrow] * 6
    full_in, sall, ssum = _join_gather_call(full_in, jnp.concatenate(rows, axis=0), "in")
    return gx, sall, ssum, full_in.reshape(R_IN, D), red_ff, red_br


def kernel(x, c, w_ada, b_ada, pre_norm_tm, post_norm_tm, pre_norm_cm, post_norm_cm, w_in, b_in, hg_lb_logits, hg_norm, conv_dw, conv_db, conv_ln_g, conv_ln_b, w_br_a, w_br_b, w_out, w_ff1, w_ff2, loss_target, m_w_ada, m_b_ada, m_pre_norm_tm, m_post_norm_tm, m_pre_norm_cm, m_post_norm_cm, m_w_in, m_b_in, m_hg_lb_logits, m_hg_norm, m_conv_dw, m_conv_db, m_conv_ln_g, m_conv_ln_b, m_w_br_a, m_w_br_b, m_w_out, m_w_ff1, m_w_ff2, v_w_ada, v_b_ada, v_pre_norm_tm, v_post_norm_tm, v_pre_norm_cm, v_post_norm_cm, v_w_in, v_b_in, v_hg_lb_logits, v_hg_norm, v_conv_dw, v_conv_db, v_conv_ln_g, v_conv_ln_b, v_w_br_a, v_w_br_b, v_w_out, v_w_ff1, v_w_ff2):
    xi, yi, ci = lax.axis_index("x"), lax.axis_index("y"), lax.axis_index("c")
    chip = 2 * xi + yi
    c_idx = jnp.reshape(ci, (1,)).astype(jnp.int32)
    chip_idx = jnp.reshape(chip, (1,)).astype(jnp.int32)

    w_in_halves = w_in[0].reshape(D, 2, D).transpose(1, 0, 2).reshape(R_IN, D)
    pack = jnp.concatenate([w_in_halves, w_ff1[0], w_ff2[0], w_br_a[0], w_br_b[0], w_out[0]],
                           axis=0).astype(BF16)
    wg = lax.empty((N_CHIPS, PACK_W, D), BF16)
    wa = 6 * D // N_CHIPS
    me = 4 * xi + 2 * yi + ci
    dw_blk = jnp.concatenate([conv_dw[0].reshape(-1), jnp.zeros((8 * D - CONV_K * 256,), F32)]).reshape(8, D)
    dw_all, ca_all, mod_all = _prologue_call(
        dw_blk, jnp.broadcast_to(c, (8, D)), w_ada[0].astype(BF16),
        lax.dynamic_slice_in_dim(b_ada, chip * wa, wa, axis=1))
    order = jnp.stack([chip, 2 * (1 - xi) + yi, 2 * xi + (1 - yi), 2 * (1 - xi) + (1 - yi)]).astype(jnp.int32)
    dw_all = dw_all.reshape(N_CHIPS, 2, 8 * D)[:, 0, :CONV_K * 256].reshape(N_CHIPS, CONV_K, 256)
    dw_full = dw_all.transpose(1, 0, 2).reshape(CONV_K, D)
    cact = lax.dynamic_slice_in_dim(ca_all, me * 8, 1, axis=0)
    mod_mine = lax.dynamic_index_in_dim(mod_all.reshape(N_CHIPS, 2, N_DEV, wa)[:, 0], me, axis=1,
                                        keepdims=False)
    mod = mod_mine.reshape(1, 6 * D)

    small = dict(pre_tm=pre_norm_tm, post_tm=post_norm_tm, pre_cm=pre_norm_cm, post_cm=post_norm_cm,
                 b_in=b_in, logits=hg_lb_logits, hg_norm=hg_norm, conv_dw=dw_full, conv_db=conv_db,
                 ln_g=conv_ln_g, ln_b=conv_ln_b, order=order)

    gx, sall, ssum, red_in, red_ff, red_br = _local_step(x[0], mod, cact, loss_target[0], wg, pack, small, c_idx,
                                                    chip_idx)

    shapes = {"in": w_in.shape, "br_a": w_br_a.shape, "br_b": w_br_b.shape, "out": w_out.shape,
              "ff1": w_ff1.shape, "ff2": w_ff2.shape}
    offs = {"in": (red_in, 0, R_IN), "ff1": (red_ff, 0, R_FF), "ff2": (red_ff, R_FF, 2 * R_FF),
            "br_a": (red_br, 0, R_BR), "br_b": (red_br, R_BR, 2 * R_BR), "out": (red_br, 2 * R_BR, 3 * R_BR)}
    wmv = {"in": (w_in, m_w_in, v_w_in), "br_a": (w_br_a, m_w_br_a, v_w_br_a), "br_b": (w_br_b, m_w_br_b, v_w_br_b),
           "out": (w_out, m_w_out, v_w_out), "ff1": (w_ff1, m_w_ff1, v_w_ff1), "ff2": (w_ff2, m_w_ff2, v_w_ff2)}
    res = {}
    for n in ("in", "ff1", "ff2"):
        shp = shapes[n]
        g2d = offs[n][0][offs[n][1]:offs[n][2]]
        w_, m_, v_ = (a[0] for a in wmv[n])
        if n == "in":
            g2d, d_, m2_, v2_ = _adamw_halves_call(w_, g2d.reshape(2, D, D), m_, v_, "adamw_in")
        else:
            g2d = g2d.reshape(shp[1], shp[2])
            d_, m2_, v2_ = _adamw_call(w_, g2d, m_, v_, "adamw_" + n)
        res[n] = tuple(a.reshape(shp) for a in (g2d, d_, m2_, v2_))
    trio = ("br_a", "br_b", "out")
    d3, m3, v3 = _adamw_rows_call([wmv[n][0][0] for n in trio], red_br, [wmv[n][1][0] for n in trio],
                                  [wmv[n][2][0] for n in trio], "adamw_br")
    for j, n in enumerate(trio):
        res[n] = tuple(a.reshape(shapes[n]) for a in (red_br[j * R_BR:(j + 1) * R_BR], d3[j], m3[j], v3[j]))

    sall = sall.reshape(N_DEV, SMALL_ROWS, D)
    loss = jnp.sum(ssum[57])
    dmod_all = sall[:, 0:6, :].reshape(N_DEV, 6 * D)
    g_ada = _outer_call(sall[:, 56, :], lax.dynamic_slice_in_dim(dmod_all, chip * wa, wa, axis=1))
    g_dw = lax.dynamic_slice_in_dim(ssum[24:24 + CONV_K], chip * 256, 256, axis=1)
    d_, m2_, v2_ = _adamw_call(w_ada[0], g_ada, m_w_ada[0], v_w_ada[0], "adamw_ada")
    res["ada"] = tuple(a.reshape(w_ada.shape) for a in (g_ada, d_, m2_, v2_))

    names = ["b_ada", "pre_tm", "post_tm", "pre_cm", "post_cm", "b_in", "logits", "hg_norm", "conv_db", "ln_g", "ln_b",
             "conv_dw"]
    row0 = [0, 6, 7, 8, 9, 10, 18, 20, 21, 22, 23, None]
    sres = _adamw_small_call(
        [b_ada, pre_norm_tm, post_norm_tm, pre_norm_cm, post_norm_cm, b_in, hg_lb_logits, hg_norm, conv_db,
         conv_ln_g, conv_ln_b, conv_dw[0]],
        [m_b_ada, m_pre_norm_tm, m_post_norm_tm, m_pre_norm_cm, m_post_norm_cm, m_b_in, m_hg_lb_logits, m_hg_norm,
         m_conv_db, m_conv_ln_g, m_conv_ln_b, m_conv_dw[0]],
        [v_b_ada, v_pre_norm_tm, v_post_norm_tm, v_pre_norm_cm, v_post_norm_cm, v_b_in, v_hg_lb_logits, v_hg_norm,
         v_conv_db, v_conv_ln_g, v_conv_ln_b, v_conv_dw[0]], row0, ssum, g_dw)
    for nm, r4 in zip(names, sres):
        res[nm] = tuple(a.reshape(conv_dw.shape) for a in r4) if nm == "conv_dw" else r4

    order = ["ada", "b_ada", "pre_tm", "post_tm", "pre_cm", "post_cm", "in", "b_in", "logits", "hg_norm", "conv_dw",
             "conv_db", "ln_g", "ln_b", "br_a", "br_b", "out", "ff1", "ff2"]
    outs = [loss, gx.reshape(x.shape)]
    for kind in range(4):
        outs.extend(res[n][kind] for n in order)
    return tuple(outs)
```

```python
import jax
import jax.numpy as jnp
from jax import lax
from jax.experimental import pallas as pl
from jax.experimental.pallas import tpu as pltpu

F32, BF16 = jnp.float32, jnp.bfloat16
SDS = jax.ShapeDtypeStruct
BS = pl.BlockSpec
MESH = pl.DeviceIdType.MESH
HI = lax.Precision.HIGHEST

D = 1024
D_FF = 4096
IN_COLS = 8192
HEADS, DK = 8, 128
CHUNK = 128
CONV_K = 31
HALO = 32
SUB = 32
EPS = 1e-6
N_CHIPS, N_DEV = 4, 8
TM = 256
TB = 256
VMEM_LIMIT = 56 * 1024 * 1024

R_IN, R_BR, R_FF = 2048, 256, 1024
PACK_W = R_IN + 3 * R_BR + 2 * R_FF
O_IN, O_FF1, O_FF2, O_BRA, O_BRB, O_OUT = 0, 2048, 3072, 4096, 4352, 4608
SMALL_ROWS = 64

ADAM_LR, ADAM_B1, ADAM_B2, ADAM_EPS, ADAM_WD, ADAM_STEP = 0.001, 0.9, 0.999, 1e-08, 0.01, 10

NN = (((1,), (0,)), ((), ()))
NT = (((1,), (1,)), ((), ()))
TN = (((0,), (0,)), ((), ()))


def _mm(a, b, dims=NN, precision=None):
    return lax.dot_general(a, b, dims, preferred_element_type=F32, precision=precision)


def _sig(v):
    return jax.nn.sigmoid(v)


def _dsilu(v, s):
    return s * (1.0 + v * (1.0 - s))


def _params(*sem):
    return pltpu.CompilerParams(dimension_semantics=sem if sem else None, vmem_limit_bytes=VMEM_LIMIT)


def _rowsum(v):
    return jnp.sum(v, axis=0, keepdims=True)


def _mesh_pos():
    return lax.axis_index("x"), lax.axis_index("y"), lax.axis_index("c")


def _allgather_parts(x_ref, out_ref, send_sems, recv_sems, local_sem):
    m_per = x_ref.shape[0]
    x, y, c = _mesh_pos()
    me, sibling = (x, y, c), (x, y, 1 - c)
    chips = [(1 - x, y), (x, 1 - y), (1 - x, 1 - y)]

    def rows(px, py, pc):
        return out_ref.at[pl.ds((4 * px + 2 * py + pc) * m_per, m_per), :]

    def copy(k, block, to, src=None):
        return pltpu.make_async_remote_copy(
            src_ref=rows(*block) if src is None else src, dst_ref=rows(*block),
            send_sem=send_sems.at[k], recv_sem=recv_sems.at[k], device_id=to, device_id_type=MESH)

    def first():
        return [copy(0, me, sibling, src=x_ref)] + [copy(1 + j, me, (*chip, c), src=x_ref)
                                                    for j, chip in enumerate(chips)]

    def start():
        pltpu.make_async_copy(x_ref, rows(*me), local_sem).start()
        for cp in first():
            cp.start()

    def finish():
        passed = [copy(4 + j, (*chip, c), sibling) for j, chip in enumerate(chips)]
        for j, chip in enumerate(chips):
            copy(1 + j, (*chip, c), me).wait_recv()
            passed[j].start()
        copy(0, sibling, me).wait_recv()
        for j, chip in enumerate(chips):
            copy(4 + j, (*chip, 1 - c), me).wait_recv()
        for cp in first() + passed:
            cp.wait_send()
        pltpu.make_async_copy(x_ref, rows(*me), local_sem).wait()

    return start, finish


def _allgather_sems():
    return [pltpu.SemaphoreType.DMA((7,)), pltpu.SemaphoreType.DMA((7,)), pltpu.SemaphoreType.DMA]
def _gather_sems(n_ranges):
    return [pltpu.SemaphoreType.DMA((6 * n_ranges,)), pltpu.SemaphoreType.DMA((6 * n_ranges,))]


def _pack_gather(pack_ref, wg_ref, send_sems, recv_sems, ranges):
    x, y, c = _mesh_pos()
    me, sibling = (x, y, c), (x, y, 1 - c)
    chips = [(1 - x, y), (x, 1 - y), (1 - x, 1 - y)]

    def land(r, px, py, pc):
        off, n = ranges[r]
        return wg_ref.at[2 * px + py, pl.ds(off + pc * (n // 2), n // 2), :]

    def mine(r):
        off, n = ranges[r]
        return pack_ref.at[pl.ds(off + c * (n // 2), n // 2), :]

    def copy(r, k, block, to, src=None):
        return pltpu.make_async_remote_copy(
            src_ref=land(r, *block) if src is None else src, dst_ref=land(r, *block),
            send_sem=send_sems.at[6 * r + k], recv_sem=recv_sems.at[6 * r + k], device_id=to, device_id_type=MESH)

    def start():
        for r in range(len(ranges)):
            for j, chip in enumerate(chips):
                copy(r, j, me, (*chip, c), src=mine(r)).start()

    def finish():
        for r in range(len(ranges)):
            for j, chip in enumerate(chips):
                copy(r, j, (*chip, c), me).wait_recv()
                copy(r, 3 + j, (*chip, c), sibling).start()
        for r in range(len(ranges)):
            for j, chip in enumerate(chips):
                copy(r, 3 + j, (*chip, 1 - c), me).wait_recv()
        for r in range(len(ranges)):
            for j, chip in enumerate(chips):
                copy(r, j, me, (*chip, c), src=mine(r)).wait_send()
                copy(r, 3 + j, (*chip, c), sibling).wait_send()

    return start, finish


def _pack_call(w_in, w_ff1, w_ff2, w_br_a, w_br_b, w_out):
    blk = (R_BR, D)

    def body(in_ref, f1_ref, f2_ref, a_ref, b_ref, o_ref, out_ref):
        i = pl.program_id(0)
        for lo, hi, ref in ((0, 8, in_ref), (8, 12, f1_ref), (12, 16, f2_ref), (16, 17, a_ref), (17, 18, b_ref),
                            (18, 19, o_ref)):
            @pl.when((i >= lo) & (i < hi))
            def _(ref=ref):
                out_ref[...] = ref[...].astype(BF16)

    return pl.pallas_call(
        body, name="pack_weights", grid=(PACK_W // R_BR,), out_shape=SDS((PACK_W, D), BF16),
        in_specs=[BS(blk, lambda i: (jnp.minimum(i, 7) % 4, jnp.minimum(i, 7) // 4)),
                  BS(blk, lambda i: (jnp.clip(i - 8, 0, 3), 0)), BS(blk, lambda i: (jnp.clip(i - 12, 0, 3), 0)),
                  BS(blk, lambda i: (0, 0)), BS(blk, lambda i: (0, 0)), BS(blk, lambda i: (0, 0))],
        out_specs=BS(blk, lambda i: (i, 0)), compiler_params=_params("arbitrary"),
    )(w_in, w_ff1, w_ff2, w_br_a, w_br_b, w_out)


def _relay_sems():
    return [pltpu.SemaphoreType.DMA((8,)), pltpu.SemaphoreType.DMA((8,))]


def _relay_gather(pack_ref, wg_ref, send_sems, recv_sems, off, n):
    x, y, c = _mesh_pos()
    me, sibling = (x, y, c), (x, y, 1 - c)
    chips = [(1 - x, y), (x, 1 - y), (1 - x, 1 - y)]
    h, q = n // 2, n // 4

    def land(px, py, pc, piece=None):
        if piece is None:
            return wg_ref.at[2 * px + py, pl.ds(off + pc * h, h), :]
        return wg_ref.at[2 * px + py, pl.ds(off + pc * h + piece * q, q), :]

    def copy(k, ref, to, src=None):
        return pltpu.make_async_remote_copy(
            src_ref=ref if src is None else src, dst_ref=ref, send_sem=send_sems.at[k], recv_sem=recv_sems.at[k],
            device_id=to, device_id_type=MESH)

    def direct(j):
        return copy(j, land(x, y, c), (*chips[j], c), src=pack_ref.at[pl.ds(off + c * h, h), :])

    def relayed(j):
        if j == 0:
            return copy(6, land(*chips[0], c, 1), (x, 1 - y, c))
        return copy(7, land(*chips[1], c, 0), (1 - x, y, c))

    def start():
        direct(0).start()
        direct(1).start()

    def arrive(j):
        if j == 0:
            for k in range(2):
                copy(k, land(*chips[k], c), me).wait_recv()
                relayed(k).start()
                copy(3 + k, land(*chips[k], c), sibling).start()
        if j == 2:
            copy(7, land(*chips[2], c, 0), me).wait_recv()
            copy(6, land(*chips[2], c, 1), me).wait_recv()
            copy(5, land(*chips[2], c), sibling).start()
        copy(3 + j, land(*chips[j], 1 - c), me).wait_recv()

    def drain():
        for j in range(2):
            direct(j).wait_send()
            relayed(j).wait_send()
        for j in range(3):
            copy(3 + j, land(*chips[j], c), sibling).wait_send()

    return start, arrive, drain


def _prologue_call(dw_blk, c_blk, w_ada, b_ada):
    wa = w_ada.shape[1]

    def body(dw_ref, c_ref, wa_ref, ba_ref, dwg_ref, ca_ref, modg_ref,
             cg_scr, part_scr, s1, r1, l1, s2, r2, l2, s3, r3, l3):
        start_c, finish_c = _allgather_parts(c_ref, cg_scr, s2, r2, l2)
        start_dw, finish_dw = _allgather_parts(dw_ref, dwg_ref, s1, r1, l1)
        start_mod, finish_mod = _allgather_parts(part_scr, modg_ref, s3, r3, l3)
        start_c()
        start_dw()
        finish_c()
        cv = cg_scr[...]
        ca = cv * _sig(cv)
        ca_ref[...] = ca
        pick = (lax.broadcasted_iota(jnp.int32, (N_DEV, N_DEV * 8), 1)
                == 8 * lax.broadcasted_iota(jnp.int32, (N_DEV, N_DEV * 8), 0)).astype(BF16)
        ca8 = _mm(pick, ca.astype(BF16)).astype(BF16)
        part_scr[...] = _mm(ca8, wa_ref[...].astype(BF16)) + ba_ref[...]
        start_mod()
        finish_dw()
        finish_mod()

    vm = BS(memory_space=pltpu.VMEM)
    return pl.pallas_call(
        body, name="prologue_adaln_conv_dw",
        out_shape=(SDS((N_DEV * 8, D), F32), SDS((N_DEV * 8, D), F32), SDS((N_DEV * N_DEV, wa), F32)),
        in_specs=[vm, vm, vm, vm], out_specs=(vm, vm, vm),
        scratch_shapes=[pltpu.VMEM((N_DEV * 8, D), F32), pltpu.VMEM((N_DEV, wa), F32)]
        + _allgather_sems() + _allgather_sems() + _allgather_sems(),
        compiler_params=pltpu.CompilerParams(vmem_limit_bytes=VMEM_LIMIT),
    )(dw_blk, c_blk, w_ada, b_ada)


def _halves_exchange(g_ref, out_ref, send_sems, recv_sems):
    x, y, c = _mesh_pos()

    def copies():
        return [pltpu.make_async_remote_copy(
            src_ref=g_ref.at[k, 1 - c], dst_ref=out_ref.at[k], send_sem=send_sems.at[k], recv_sem=recv_sems.at[k],
            device_id=(x, y, 1 - c), device_id_type=MESH) for k in range(N_CHIPS)]

    def start():
        for cp in copies():
            cp.start()

    def finish():
        for cp in copies():
            cp.wait()

    return start, finish


def _halves_sems():
    return [pltpu.SemaphoreType.DMA((N_CHIPS,)), pltpu.SemaphoreType.DMA((N_CHIPS,))]


def _sibling_halves_call(g, tag):
    _, _, h, n = g.shape

    def body(g_ref, out_ref, send_sems, recv_sems):
        start, finish = _halves_exchange(g_ref, out_ref, send_sems, recv_sems)
        start()
        finish()

    return pl.pallas_call(
        body, name="rs_sibling_halves_" + tag, out_shape=SDS((N_CHIPS, h, n), g.dtype),
        in_specs=[BS(memory_space=pl.ANY)], out_specs=BS(memory_space=pl.ANY),
        scratch_shapes=_halves_sems(),
    )(g)


def _chip_exchange(p_ref, out_ref, send_sems, recv_sems):
    x, y, c = _mesh_pos()
    chips = [(1 - x, y), (x, 1 - y), (1 - x, 1 - y)]

    def copies():
        return [pltpu.make_async_remote_copy(
            src_ref=p_ref.at[2 * cx + cy], dst_ref=out_ref.at[j], send_sem=send_sems.at[j], recv_sem=recv_sems.at[j],
            device_id=(cx, cy, c), device_id_type=MESH) for j, (cx, cy) in enumerate(chips)]

    def start():
        for cp in copies():
            cp.start()

    def finish():
        for cp in copies():
            cp.wait()

    return start, finish


def _exchange_sems():
    return [pltpu.SemaphoreType.DMA((3,)), pltpu.SemaphoreType.DMA((3,))]


def _join_exchange(in_ref, out_ref, send_sems, recv_sems):
    h = in_ref.shape[1]
    q = h // 4
    x, y, c = _mesh_pos()

    def copy(k, half):
        return pltpu.make_async_remote_copy(
            src_ref=in_ref.at[half, pl.ds(k * q, q)], dst_ref=out_ref.at[half, pl.ds(k * q, q)],
            send_sem=send_sems.at[k], recv_sem=recv_sems.at[k],
            device_id=(x, y, 1 - c), device_id_type=MESH)

    def start():
        for k in range(4):
            copy(k, c).start()

    def finish():
        for k in range(4):
            copy(k, c).wait_send()
            copy(k, 1 - c).wait_recv()

    return start, finish


def _join_sems():
    return [pltpu.SemaphoreType.DMA((4,)), pltpu.SemaphoreType.DMA((4,))]


def _join_gather_call(full, srows, tag):
    mr = srows.shape[0]

    def body(in_ref, s_ref, out_ref, all_ref, sum_ref, send_sems, recv_sems, gs, gr, gl):
        start, finish = _join_exchange(in_ref, out_ref, send_sems, recv_sems)
        start_g, finish_g = _allgather_parts(s_ref, all_ref, gs, gr, gl)
        start()
        start_g()
        finish_g()
        acc = all_ref[0:mr, :]
        for d in range(1, N_DEV):
            acc = acc + all_ref[d * mr:(d + 1) * mr, :]
        sum_ref[...] = acc
        finish()

    hbm = BS(memory_space=pl.ANY)
    vm = BS(memory_space=pltpu.VMEM)
    return pl.pallas_call(
        body, name="rs_sibling_join_" + tag,
        out_shape=(SDS(full.shape, full.dtype), SDS((N_DEV * mr, D), F32), SDS((mr, D), F32)),
        in_specs=[hbm, vm], out_specs=(hbm, vm, vm),
        scratch_shapes=_join_sems() + _allgather_sems(), input_output_aliases={0: 0},
        compiler_params=pltpu.CompilerParams(vmem_limit_bytes=VMEM_LIMIT),
    )(full, srows)


def _add_halves_call(g, recv, c_idx, tag):
    _, _, h, n = g.shape
    tr = h // 2

    def body(c_ref, g_ref, r_ref, o_ref):
        o_ref[...] = (g_ref[...].astype(F32) + r_ref[...].astype(F32)).astype(BF16)

    return pl.pallas_call(
        body, name="rs_add_halves_" + tag, out_shape=SDS((N_CHIPS, h, n), BF16),
        grid_spec=pltpu.PrefetchScalarGridSpec(
            num_scalar_prefetch=1, grid=(N_CHIPS, 2),
            in_specs=[BS((None, None, tr, n), lambda k, r, c_ref: (k, c_ref[0], r, 0)),
                      BS((None, tr, n), lambda k, r, c_ref: (k, r, 0))],
            out_specs=BS((None, tr, n), lambda k, r, c_ref: (k, r, 0))),
        compiler_params=_params("arbitrary", "arbitrary"),
    )(c_idx, g, recv)


def _add_chips_call(p, recv, chip_c_idx, tag):
    _, h, n = p.shape
    tr = h // 2

    def body(k_ref, p_ref, r_ref, o_ref):
        acc = p_ref[...].astype(F32)
        for j in range(3):
            acc = acc + r_ref[j].astype(F32)
        o_ref[...] = acc

    return pl.pallas_call(
        body, name="rs_add_chips_" + tag, out_shape=SDS((2, h, n), F32),
        grid_spec=pltpu.PrefetchScalarGridSpec(
            num_scalar_prefetch=1, grid=(2,),
            in_specs=[BS((None, tr, n), lambda r, k_ref: (k_ref[0], r, 0)),
                      BS((3, tr, n), lambda r, k_ref: (0, r, 0))],
            out_specs=BS((None, tr, n), lambda r, k_ref: (k_ref[1], r, 0))),
        compiler_params=_params("arbitrary"),
    )(chip_c_idx, p, recv)


def _load_rows(wg_hbm, w_vmem, sem, off):
    cp = pltpu.make_async_copy(wg_hbm.at[:, pl.ds(off, w_vmem.shape[1]), :], w_vmem, sem)
    cp.start()
    return cp


def _fwd_in_call(x, mod, pre_tm, wg, b_in, pack, order):
    S = x.shape[0]
    tmf = 2 * TM
    nt = S // tmf
    wc = IN_COLS // N_CHIPS

    def body(ord_ref, x_ref, mod_ref, g_ref, w_hbm, b_ref, pack_ref, p_ref, h_hbm, wg_out, w_vmem, h_scr, sems,
             send_sems, recv_sems, send_sems2, recv_sems2):
        q, i = pl.program_id(0), pl.program_id(1)
        rows = pl.ds(pl.multiple_of(i * tmf, tmf), tmf)
        start, arrive, drain = _relay_gather(pack_ref, wg_out, send_sems, recv_sems, O_IN, R_IN)
        start2, finish2 = _pack_gather(pack_ref, wg_out, send_sems2, recv_sems2, [(O_OUT, R_BR)])

        def weights(phase):
            src = pack_ref.at[pl.ds(O_IN, R_IN), :] if phase == 0 else wg_out.at[ord_ref[phase], pl.ds(O_IN, R_IN), :]
            return pltpu.make_async_copy(src, w_vmem.at[phase % 2], sems.at[phase % 2])

        def own_block(k):
            mx, my, mc = _mesh_pos()
            rows = pl.ds(k * (PACK_W // 2), PACK_W // 2)
            return pltpu.make_async_remote_copy(
                src_ref=pack_ref.at[rows, :], dst_ref=wg_out.at[2 * mx + my, rows, :], send_sem=send_sems2.at[6 + k],
                recv_sem=recv_sems2.at[6 + k], device_id=(mx, my, 1 - mc), device_id_type=MESH)

        @pl.when((q == 0) & (i == 0))
        def _():
            start()
            own_block(0).start()
            own_block(1).start()
            weights(0).start()
            weights(0).wait()

        @pl.when((q == 1) & (i == 0))
        def _():
            arrive(0)
            start2()
            weights(1).start()
            weights(1).wait()
            arrive(1)
            weights(2).start()

        @pl.when((q == 2) & (i == 0))
        def _():
            weights(2).wait()
            arrive(2)
            weights(3).start()

        @pl.when((q == 3) & (i == 0))
        def _():
            weights(3).wait()

        @pl.when(q == 0)
        def _():
            xv = x_ref[...]
            r = lax.rsqrt(jnp.mean(xv * xv, axis=-1, keepdims=True) + EPS)
            h = xv * r * g_ref[...] * (1.0 + mod_ref[:, D:2 * D]) + mod_ref[:, 0:D]
            h_scr[rows, :] = h.astype(BF16)

        hb = h_scr[rows, :]
        slot = q % 2
        for k in range(wc // D):
            p_ref[:, k * D:(k + 1) * D] = _mm(hb, w_vmem[slot, k * D:(k + 1) * D, :]) + b_ref[:, k * D:(k + 1) * D]

        @pl.when((q == N_CHIPS - 1) & (i == nt - 1))
        def _():
            cp = pltpu.make_async_copy(h_scr, h_hbm, sems.at[0])
            cp.start()
            drain()
            finish2()
            own_block(0).wait()
            own_block(1).wait()
            cp.wait()

    hbm = BS(memory_space=pl.ANY)
    return pl.pallas_call(
        body, name="fwd_in", out_shape=(SDS((S, IN_COLS), F32), SDS((S, D), BF16), SDS(wg.shape, wg.dtype)),
        grid_spec=pltpu.PrefetchScalarGridSpec(
            num_scalar_prefetch=1, grid=(N_CHIPS, nt),
            in_specs=[BS((tmf, D), lambda q, i, o: (jnp.where(q == 0, i, nt - 1), 0)),
                      BS((1, 6 * D), lambda q, i, o: (0, 0)),
                      BS((1, D), lambda q, i, o: (0, 0)), hbm, BS((1, wc), lambda q, i, o: (0, o[q])), hbm],
            out_specs=(BS((tmf, wc), lambda q, i, o: (i, o[q])), hbm, hbm),
            scratch_shapes=[pltpu.VMEM((2, R_IN, D), BF16), pltpu.VMEM((S, D), BF16), pltpu.SemaphoreType.DMA((2,))]
            + _relay_sems() + [pltpu.SemaphoreType.DMA((8,)), pltpu.SemaphoreType.DMA((8,))]),
        input_output_aliases={4: 2},
        compiler_params=_params("arbitrary", "arbitrary"),
    )(order, x, mod, pre_tm, wg, b_in, pack)


def _lower_bound(lg_ref):
    l0, l1 = lg_ref[0:1, :], lg_ref[1:2, :]
    mx = jnp.maximum(l0, l1)
    e0, e1 = jnp.exp(l0 - mx), jnp.exp(l1 - mx)
    return e0 / (e0 + e1)


def _tri_masks():
    ri = lax.broadcasted_iota(jnp.int32, (CHUNK, CHUNK), 0)
    ci = lax.broadcasted_iota(jnp.int32, (CHUNK, CHUNK), 1)
    return (ri >= ci).astype(F32), (ci >= ri).astype(F32)


def _cumsum_mm(tri, g):
    tb = tri.astype(BF16)
    hi = g.astype(BF16)
    r1 = g - hi.astype(F32)
    mid = r1.astype(BF16)
    lo = (r1 - mid.astype(F32)).astype(BF16)
    return _mm(tb, hi) + _mm(tb, mid) + _mm(tb, lo)


def _hg_gates(q_r, f_r, lb, tril):
    sq = _sig(q_r)
    q = q_r * sq
    sf = _sig(f_r)
    f = lb + (1.0 - lb) * sf
    k = 1.0 - f
    g = jnp.log(f)
    b = _cumsum_mm(tril, g)
    b_last = _rowsum(g)
    row = lax.broadcasted_iota(jnp.int32, g.shape, 0)
    ref = _rowsum(jnp.where(row < CHUNK // 2, g, 0.0))
    e = jnp.exp(b)
    eq = jnp.exp(jnp.minimum(b - ref, 80.0))
    ek = jnp.exp(jnp.minimum(ref - b, 80.0))
    dd = jnp.exp(b_last - b)
    return dict(sq=sq, q=q, sf=sf, f=f, k=k, e=e, eq=eq, ek=ek, dd=dd, elast=jnp.exp(b_last),
                qe=q * e, qt=q * eq, kt=k * ek, kd=k * dd)


def _hgrn_fwd_call(p, logits, gn, wg, pack):
    S = p.shape[0]
    ncb = TB // CHUNK
    ranges = [(O_FF1, R_FF)]

    def body(q_ref, f_ref, v_ref, og_ref, lg_ref, gn_ref, wg_in, pack_ref, o_ref, oa_ref, st_ref, wg_out,
             st_scr, send_sems, recv_sems):
        start, finish = _pack_gather(pack_ref, wg_out, send_sems, recv_sems, ranges)

        @pl.when(pl.program_id(0) == 0)
        def _():
            start()
            st_scr[...] = jnp.zeros_like(st_scr)

        lb = _lower_bound(lg_ref)
        tril, _ = _tri_masks()

        def chunk(ci, carry):
            rows = pl.ds(pl.multiple_of(ci * CHUNK, CHUNK), CHUNK)
            st_ref[ci] = st_scr[...]
            t = _hg_gates(q_ref[rows, :], f_ref[rows, :], lb, tril)
            v = v_ref[rows, :]
            for h in range(HEADS):
                sl = slice(h * DK, (h + 1) * DK)
                stp = st_scr[:, sl]
                vb = v[:, sl].astype(BF16)
                inter = _mm(t["qe"][:, sl].astype(BF16), stp.astype(BF16), NT)
                a = jnp.where(tril > 0.5, _mm(t["qt"][:, sl].astype(BF16), t["kt"][:, sl].astype(BF16), NT), 0.0)
                o = inter + _mm(a.astype(BF16), vb)
                st_scr[:, sl] = stp * t["elast"][:, sl] + _mm(vb, t["kd"][:, sl].astype(BF16), TN)
                oh = o * lax.rsqrt(jnp.mean(o * o, axis=-1, keepdims=True) + EPS)
                og = og_ref[rows, sl]
                o_ref[rows, sl] = o
                oa_ref[rows, sl] = (oh * gn_ref[:, sl] * (og * _sig(og))).astype(BF16)
            return carry

        lax.fori_loop(0, ncb, chunk, 0)

        @pl.when(pl.program_id(0) == S // TB - 1)
        def _():
            finish()

    col = lambda j: BS((TB, D), lambda i, j=j: (i, j))
    hbm = BS(memory_space=pl.ANY)
    return pl.pallas_call(
        body, name="hgrn_fwd", grid=(S // TB,),
        out_shape=(SDS((S, D), F32), SDS((S, D), BF16), SDS((S // CHUNK, DK, D), F32), SDS(wg.shape, wg.dtype)),
        in_specs=[col(0), col(1), col(2), col(3), BS((2, D), lambda i: (0, 0)), BS((1, D), lambda i: (0, 0)),
                  hbm, hbm],
        out_specs=(BS((TB, D), lambda i: (i, 0)), BS((TB, D), lambda i: (i, 0)),
                   BS((ncb, DK, D), lambda i: (i, 0, 0)), hbm),
        scratch_shapes=[pltpu.VMEM((DK, D), F32)] + _gather_sems(len(ranges)),
        input_output_aliases={6: 3},
        compiler_params=_params("arbitrary"),
    )(p, p, p, p, logits, gn, wg, pack)


def _layernorm_stats(uc):
    mu = jnp.mean(uc, axis=-1, keepdims=True)
    xc = uc - mu
    rs = lax.rsqrt(jnp.mean(xc * xc, axis=-1, keepdims=True) + EPS)
    return xc * rs, rs


EXT = HALO + TM + 8


def _fill_shifted(ext, shifted):
    for m in range(1, 8):
        shifted[m - 1] = ext[m:m + HALO + TM, :]


def _window(ext, shifted, s0, n):
    m = s0 % 8
    q = s0 - m
    return ext[q:q + n, :] if m == 0 else shifted[m - 1, q:q + n, :]


def _conv_fwd_call(p, dw, db, ln_g, ln_b, wg, pack):
    S = p.shape[0]
    ranges = [(O_FF2, R_FF), (O_BRA, 2 * R_BR)]

    def body(cv_ref, cg_ref, dw_ref, db_ref, g_ref, b_ref, wg_in, pack_ref, u_ref, uc_ref, cb_ref, wg_out,
             uext, ush, send_sems, recv_sems):
        start, finish = _pack_gather(pack_ref, wg_out, send_sems, recv_sems, ranges)

        @pl.when(pl.program_id(0) == 0)
        def _():
            start()
            uext[0:HALO, :] = jnp.zeros((HALO, D), F32)
            uext[HALO + TM:EXT, :] = jnp.zeros((EXT - HALO - TM, D), F32)

        u = cv_ref[...] * _sig(cg_ref[...])
        uext[HALO:HALO + TM, :] = u
        u_ref[...] = u
        _fill_shifted(uext, ush)
        for rb in range(TM // SUB):
            acc = jnp.broadcast_to(db_ref[...], (SUB, D))
            for j in range(CONV_K):
                s0 = HALO - (CONV_K - 1) + j + rb * SUB
                acc = acc + dw_ref[j:j + 1, :] * _window(uext, ush, s0, SUB)
            uc_ref[rb * SUB:(rb + 1) * SUB, :] = acc
            xh, _ = _layernorm_stats(acc)
            ln = xh * g_ref[...] + b_ref[...]
            cb_ref[rb * SUB:(rb + 1) * SUB, :] = (ln * _sig(ln)).astype(BF16)
        uext[0:HALO, :] = uext[TM:TM + HALO, :]

        @pl.when(pl.program_id(0) == S // TM - 1)
        def _():
            finish()

    vec = BS((1, D), lambda i: (0, 0))
    hbm = BS(memory_space=pl.ANY)
    return pl.pallas_call(
        body, name="conv_fwd", grid=(S // TM,),
        out_shape=(SDS((S, D), F32), SDS((S, D), F32), SDS((S, D), BF16), SDS(wg.shape, wg.dtype)),
        in_specs=[BS((TM, D), lambda i: (i, 4)), BS((TM, D), lambda i: (i, 5)),
                  BS((CONV_K, D), lambda i: (0, 0)), vec, vec, vec, hbm, hbm],
        out_specs=(BS((TM, D), lambda i: (i, 0)),) * 3 + (hbm,),
        scratch_shapes=[pltpu.VMEM((EXT, D), F32), pltpu.VMEM((7, HALO + TM, D), F32)] + _gather_sems(len(ranges)),
        input_output_aliases={6: 3},
        compiler_params=_params("arbitrary"),
    )(p, p, dw, db, ln_g, ln_b, wg, pack)


def _mm_rows(a, w_ref):
    acc = _mm(a[:, 0:R_BR], w_ref[0])
    for k in range(1, N_CHIPS):
        acc = acc + _mm(a[:, k * R_BR:(k + 1) * R_BR], w_ref[k])
    return acc


def _mm_rows_t(a, w_ref):
    return jnp.concatenate([_mm(a, w_ref[k], NT) for k in range(N_CHIPS)], axis=1)


def _br_spec(off):
    return BS((N_CHIPS, R_BR, D), lambda i: (0, off // R_BR, 0))


def _merge_fwd_call(oa, cb, p, x, mod, post_tm, pre_cm, wg):
    S = x.shape[0]

    def body(oa_ref, cb_ref, ga_ref, gb_ref, x_ref, mod_ref, post_ref, pre_ref, wa_ref, wb_ref, wo_ref,
             ya_ref, yb_ref, mg_ref, y_ref, x2_ref, h2_ref):
        ya = _mm_rows(oa_ref[...], wa_ref)
        yb = _mm_rows(cb_ref[...], wb_ref)
        ya_ref[...] = ya.astype(BF16)
        yb_ref[...] = yb.astype(BF16)
        mg = (_sig(ga_ref[...]) * ya + _sig(gb_ref[...]) * yb).astype(BF16)
        mg_ref[...] = mg
        y = _mm_rows(mg, wo_ref)
        y_ref[...] = y
        n = y * lax.rsqrt(jnp.mean(y * y, axis=-1, keepdims=True) + EPS) * post_ref[...]
        x2 = x_ref[...] + mod_ref[:, 2 * D:3 * D] * n
        x2_ref[...] = x2
        r2 = lax.rsqrt(jnp.mean(x2 * x2, axis=-1, keepdims=True) + EPS)
        h2 = x2 * r2 * pre_ref[...] * (1.0 + mod_ref[:, 4 * D:5 * D]) + mod_ref[:, 3 * D:4 * D]
        h2_ref[...] = h2.astype(BF16)

    tile = BS((TM, D), lambda i: (i, 0))
    vec = BS((1, D), lambda i: (0, 0))
    return pl.pallas_call(
        body, name="merge_fwd", grid=(S // TM,),
        out_shape=(SDS((S, D), BF16), SDS((S, D), BF16), SDS((S, D), BF16), SDS((S, D), F32), SDS((S, D), F32),
                   SDS((S, D), BF16)),
        in_specs=[tile, tile, BS((TM, D), lambda i: (i, 6)), BS((TM, D), lambda i: (i, 7)), tile,
                  BS((1, 6 * D), lambda i: (0, 0)), vec, vec, _br_spec(O_BRA), _br_spec(O_BRB), _br_spec(O_OUT)],
        out_specs=(tile,) * 6,
        compiler_params=_params("arbitrary"),
    )(oa, cb, p, p, x, mod, post_tm, pre_cm, wg, wg, wg)


def _ffn_call(h2, x2, target, mod, post_cm, pre_cm, wg):
    S = x2.shape[0]

    def body(h2_ref, x2_ref, t_ref, mod_ref, post_ref, pre_ref, w_hbm,
             z_ref, da_ref, dy2_ref, dx2_ref, acc_ref, w1_v, w2_v, ra_scr, sems):
        @pl.when(pl.program_id(0) == 0)
        def _():
            c1 = _load_rows(w_hbm, w1_v, sems.at[0], O_FF1)
            c2 = _load_rows(w_hbm, w2_v, sems.at[1], O_FF2)
            c1.wait()
            c2.wait()
            acc_ref[...] = jnp.zeros_like(acc_ref)

        h2 = h2_ref[...]
        for k in range(N_CHIPS):
            ra = jnp.maximum(_mm(h2, w1_v[k]), 0.0)
            ra_scr[:, k * D:(k + 1) * D] = ra
            z_ref[:, k * D:(k + 1) * D] = (ra * ra).astype(BF16)
        y2 = _mm(z_ref[:, 0:D], w2_v[0])
        for k in range(1, N_CHIPS):
            y2 = y2 + _mm(z_ref[:, k * D:(k + 1) * D], w2_v[k])
        ry = lax.rsqrt(jnp.mean(y2 * y2, axis=-1, keepdims=True) + EPS)
        yn = y2 * ry
        n = yn * post_ref[...]
        g2 = mod_ref[:, 5 * D:6 * D]
        x2 = x2_ref[...]
        err = x2 + g2 * n - t_ref[...]
        acc_ref[5:6, :] += _rowsum(err * err) * (0.5 / D)
        dout = err * (1.0 / D)
        acc_ref[0:1, :] += _rowsum(dout * n)
        dn = dout * g2
        acc_ref[1:2, :] += _rowsum(dn * yn)
        dyn = dn * post_ref[...]
        dy2 = (ry * (dyn - yn * jnp.mean(dyn * yn, axis=-1, keepdims=True))).astype(BF16)
        dy2_ref[...] = dy2
        for k in range(N_CHIPS):
            dz = _mm(dy2, w2_v[k], NT)
            da_ref[:, k * D:(k + 1) * D] = (dz * (2.0 * ra_scr[:, k * D:(k + 1) * D])).astype(BF16)
        dh2 = jnp.zeros((TM, D), F32)
        for k in range(N_CHIPS):
            dh2 = dh2 + _mm(da_ref[:, k * D:(k + 1) * D], w1_v[k], NT)
        r2 = lax.rsqrt(jnp.mean(x2 * x2, axis=-1, keepdims=True) + EPS)
        xn = x2 * r2
        yv = xn * pre_ref[...]
        acc_ref[2:3, :] += _rowsum(dh2)
        acc_ref[3:4, :] += _rowsum(dh2 * yv)
        dyv = dh2 * (1.0 + mod_ref[:, 4 * D:5 * D])
        acc_ref[4:5, :] += _rowsum(dyv * xn)
        dxn = dyv * pre_ref[...]
        dx2_ref[...] = dout + r2 * (dxn - xn * jnp.mean(dxn * xn, axis=-1, keepdims=True))

    tile = BS((TM, D), lambda i: (i, 0))
    wide = BS((TM, D_FF), lambda i: (i, 0))
    vec = BS((1, D), lambda i: (0, 0))
    return pl.pallas_call(
        body, name="ffn_fwd_bwd", grid=(S // TM,),
        out_shape=(SDS((S, D_FF), BF16), SDS((S, D_FF), BF16), SDS((S, D), BF16), SDS((S, D), F32),
                   SDS((8, D), F32)),
        in_specs=[tile, tile, tile, BS((1, 6 * D), lambda i: (0, 0)), vec, vec, BS(memory_space=pl.ANY)],
        out_specs=(wide, wide, tile, tile, BS((8, D), lambda i: (0, 0))),
        scratch_shapes=[pltpu.VMEM((N_CHIPS, R_FF, D), BF16), pltpu.VMEM((N_CHIPS, R_FF, D), BF16),
                        pltpu.VMEM((TM, D_FF), F32),
                        pltpu.SemaphoreType.DMA((2,))],
        compiler_params=_params("arbitrary"),
    )(h2, x2, target, mod, post_cm, pre_cm, wg)


def _merge_bwd_call(dx2, y, ya, yb, p, mod, post_tm, wg, g):
    S = y.shape[0]

    def body(dx2_ref, y_ref, ya_ref, yb_ref, ga_ref, gb_ref, mod_ref, post_ref, wa_ref, wb_ref, wo_ref, g_ref,
             dy_ref, dya_ref, dyb_ref, doa_ref, dcb_ref, dpg_ref, acc_ref, bsum_ref, hr_ref, send_sems, recv_sems):
        start, finish = _halves_exchange(g_ref, hr_ref, send_sems, recv_sems)

        @pl.when(pl.program_id(0) == 0)
        def _():
            start()
            acc_ref[...] = jnp.zeros_like(acc_ref)
            bsum_ref[...] = jnp.zeros_like(bsum_ref)

        y = y_ref[...]
        ry = lax.rsqrt(jnp.mean(y * y, axis=-1, keepdims=True) + EPS)
        yn = y * ry
        dx2 = dx2_ref[...]
        acc_ref[0:1, :] += _rowsum(dx2 * (yn * post_ref[...]))
        dn = dx2 * mod_ref[:, 2 * D:3 * D]
        acc_ref[1:2, :] += _rowsum(dn * yn)
        dyn = dn * post_ref[...]
        dy = (ry * (dyn - yn * jnp.mean(dyn * yn, axis=-1, keepdims=True))).astype(BF16)
        dy_ref[...] = dy
        dmg = _mm_rows_t(dy, wo_ref)
        sa, sb = _sig(ga_ref[...]), _sig(gb_ref[...])
        dya = (dmg * sa).astype(BF16)
        dyb = (dmg * sb).astype(BF16)
        dya_ref[...] = dya
        dyb_ref[...] = dyb
        dga = dmg * ya_ref[...].astype(F32) * (sa * (1.0 - sa))
        dgb = dmg * yb_ref[...].astype(F32) * (sb * (1.0 - sb))
        dpg_ref[:, 0:D] = dga.astype(BF16)
        dpg_ref[:, D:2 * D] = dgb.astype(BF16)
        bsum_ref[:, 0:D] += _rowsum(dga)
        bsum_ref[:, D:2 * D] += _rowsum(dgb)
        doa_ref[...] = _mm_rows_t(dya, wa_ref)
        dcb_ref[...] = _mm_rows_t(dyb, wb_ref)

        @pl.when(pl.program_id(0) == S // TM - 1)
        def _():
            finish()

    tile = BS((TM, D), lambda i: (i, 0))
    vec = BS((1, D), lambda i: (0, 0))
    return pl.pallas_call(
        body, name="merge_bwd", grid=(S // TM,),
        out_shape=(SDS((S, D), BF16), SDS((S, D), BF16), SDS((S, D), BF16), SDS((S, D), F32), SDS((S, D), F32),
                   SDS((S, 2 * D), BF16), SDS((8, D), F32), SDS((1, 2 * D), F32),
                   SDS((N_CHIPS,) + g.shape[2:], g.dtype)),
        in_specs=[tile, tile, tile, tile, BS((TM, D), lambda i: (i, 6)), BS((TM, D), lambda i: (i, 7)),
                  BS((1, 6 * D), lambda i: (0, 0)), vec, _br_spec(O_BRA), _br_spec(O_BRB), _br_spec(O_OUT),
                  BS(memory_space=pl.ANY)],
        out_specs=(tile, tile, tile, tile, tile, BS((TM, 2 * D), lambda i: (i, 0)),
                   BS((8, D), lambda i: (0, 0)), BS((1, 2 * D), lambda i: (0, 0)), BS(memory_space=pl.ANY)),
        scratch_shapes=_halves_sems(),
        compiler_params=_params("arbitrary"),
    )(dx2, y, ya, yb, p, p, mod, post_tm, wg, wg, wg, g)


def _hgrn_bwd_call(p, o, doa, st, logits, gn, part, g):
    S = p.shape[0]
    nb = S // TB
    ncb = TB // CHUNK

    def body(q_ref, f_ref, v_ref, og_ref, o_ref, doa_ref, st_ref, lg_ref, gn_ref, part_ref, g_ref,
             dp_ref, bsum_ref, dlg_ref, dgn_ref, recv_ref, hr_ref,
             dst_scr, dlb_scr, dqe_s, dqt_s, dkt_s, dkd_s, dv_s, dog_s, dble_s, send_sems, recv_sems, hs, hr):
        i = pl.program_id(0)
        start, finish = _chip_exchange(part_ref, recv_ref, send_sems, recv_sems)
        start_h, finish_h = _halves_exchange(g_ref, hr_ref, hs, hr)

        @pl.when(i == 0)
        def _():
            start_h()
            start()
            dst_scr[...] = jnp.zeros_like(dst_scr)
            dlb_scr[...] = jnp.zeros_like(dlb_scr)
            bsum_ref[...] = jnp.zeros_like(bsum_ref)
            dgn_ref[...] = jnp.zeros_like(dgn_ref)

        lb = _lower_bound(lg_ref)
        tril, triu = _tri_masks()

        def chunk(tt, carry):
            ci = ncb - 1 - tt
            rows = pl.ds(pl.multiple_of(ci * CHUNK, CHUNK), CHUNK)
            q_r, f_r = q_ref[rows, :], f_ref[rows, :]
            t = _hg_gates(q_r, f_r, lb, tril)
            v = v_ref[rows, :]
            for h in range(HEADS):
                sl = slice(h * DK, (h + 1) * DK)
                stp = st_ref[ci, :, sl]
                stb = stp.astype(BF16)
                qeb = t["qe"][:, sl].astype(BF16)
                qtb = t["qt"][:, sl].astype(BF16)
                ktb = t["kt"][:, sl].astype(BF16)
                kdb = t["kd"][:, sl].astype(BF16)
                vb = v[:, sl].astype(BF16)
                a = jnp.where(tril > 0.5, _mm(qtb, ktb, NT), 0.0)
                o_h = o_ref[rows, sl]
                rinv = lax.rsqrt(jnp.mean(o_h * o_h, axis=-1, keepdims=True) + EPS)
                oh = o_h * rinv
                og = og_ref[rows, sl]
                so = _sig(og)
                d_oa = doa_ref[rows, sl]
                don = d_oa * (og * so)
                dog_s[:, sl] = d_oa * (oh * gn_ref[:, sl]) * _dsilu(og, so)
                dgn_ref[:, sl] += _rowsum(don * oh)
                doh = don * gn_ref[:, sl]
                do = (rinv * (doh - oh * jnp.mean(doh * oh, axis=-1, keepdims=True))).astype(BF16)
                dqe_s[:, sl] = _mm(do, stb, NN)
                dstp = _mm(do, qeb, TN)
                dab = jnp.where(tril > 0.5, _mm(do, vb, NT), 0.0).astype(BF16)
                dqt_s[:, sl] = _mm(dab, ktb, NN)
                dkt_s[:, sl] = _mm(dab, qtb, TN)
                dstn = dst_scr[:, sl]
                dsb = dstn.astype(BF16)
                dkd_s[:, sl] = _mm(vb, dsb, NN)
                dv_s[:, sl] = _mm(a.astype(BF16), do, TN) + _mm(kdb, dsb, NT)
                el = t["elast"][:, sl]
                dst_scr[:, sl] = dstn * el + dstp
                dble_s[:, sl] = el * _rowsum(stp * dstn)
            dqe, dqt, dkt, dkd = dqe_s[...], dqt_s[...], dkt_s[...], dkd_s[...]
            dq = dqe * t["e"] + dqt * t["eq"]
            dk = dkt * t["ek"] + dkd * t["dd"]
            dkk = dkd * t["kd"]
            qt_r = t["qt"].astype(BF16).astype(F32)
            kt_r = t["kt"].astype(BF16).astype(F32)
            dbv = dqe * t["qe"] + dqt * qt_r - dkt * kt_r - dkk
            dg = _cumsum_mm(triu, dbv) + (_rowsum(dkk) + dble_s[...])
            df = dg / t["f"] - dk
            sf = t["sf"]
            dlb_scr[...] += _rowsum(df * (1.0 - sf))
            dqr = dq * _dsilu(q_r, t["sq"])
            dfr = df * (1.0 - lb) * (sf * (1.0 - sf))
            dvv, dog = dv_s[...], dog_s[...]
            dp_ref[rows, 0:D] = dqr.astype(BF16)
            dp_ref[rows, D:2 * D] = dfr.astype(BF16)
            dp_ref[rows, 2 * D:3 * D] = dvv.astype(BF16)
            dp_ref[rows, 3 * D:4 * D] = dog.astype(BF16)
            bsum_ref[:, 0:D] += _rowsum(dqr)
            bsum_ref[:, D:2 * D] += _rowsum(dfr)
            bsum_ref[:, 2 * D:3 * D] += _rowsum(dvv)
            bsum_ref[:, 3 * D:4 * D] += _rowsum(dog)
            return carry

        lax.fori_loop(0, ncb, chunk, 0)

        dl = dlb_scr[...] * lb * (1.0 - lb)
        dlg_ref[0:1, :] = dl
        dlg_ref[1:2, :] = -dl

        @pl.when(i == nb - 1)
        def _():
            finish_h()
            finish()

    col = lambda j: BS((TB, D), lambda i, j=j: (nb - 1 - i, j))
    rev = BS((TB, D), lambda i: (nb - 1 - i, 0))
    cd = pltpu.VMEM((CHUNK, D), F32)
    return pl.pallas_call(
        body, name="hgrn_bwd", grid=(nb,),
        out_shape=(SDS((S, 4 * D), BF16), SDS((1, 4 * D), F32), SDS((2, D), F32), SDS((1, D), F32),
                   SDS((3,) + part.shape[1:], part.dtype), SDS((N_CHIPS,) + g.shape[2:], g.dtype)),
        in_specs=[col(0), col(1), col(2), col(3), rev, rev, BS((ncb, DK, D), lambda i: (nb - 1 - i, 0, 0)),
                  BS((2, D), lambda i: (0, 0)), BS((1, D), lambda i: (0, 0)), BS(memory_space=pl.ANY),
                  BS(memory_space=pl.ANY)],
        out_specs=(BS((TB, 4 * D), lambda i: (nb - 1 - i, 0)), BS((1, 4 * D), lambda i: (0, 0)),
                   BS((2, D), lambda i: (0, 0)), BS((1, D), lambda i: (0, 0)), BS(memory_space=pl.ANY),
                   BS(memory_space=pl.ANY)),
        scratch_shapes=[pltpu.VMEM((DK, D), F32), pltpu.VMEM((1, D), F32), cd, cd, cd, cd, cd, cd,
                        pltpu.VMEM((1, D), F32)] + _exchange_sems() + _halves_sems(),
        compiler_params=_params("arbitrary"),
    )(p, p, p, p, o, doa, st, logits, gn, part, g)


def _conv_bwd_call(dcb, uc, u, p, dw, ln_g, ln_b, part):
    S = uc.shape[0]
    nb = S // TM
    hb = TM // HALO

    def body(dcb_ref, uc_ref, u_ref, uh_ref, cv_ref, cg_ref, dw_ref, g_ref, b_ref, part_ref,
             dp_ref, bsum_ref, ddw_ref, acc_ref, recv_ref, uext, dext, ush, dsh, send_sems, recv_sems):
        i = pl.program_id(0)
        start, finish = _chip_exchange(part_ref, recv_ref, send_sems, recv_sems)

        @pl.when(i == 0)
        def _():
            start()
            dext[TM:EXT, :] = jnp.zeros((EXT - TM, D), F32)
            uext[HALO + TM:EXT, :] = jnp.zeros((EXT - HALO - TM, D), F32)
            bsum_ref[...] = jnp.zeros_like(bsum_ref)
            ddw_ref[...] = jnp.zeros_like(ddw_ref)
            acc_ref[...] = jnp.zeros_like(acc_ref)

        first_tile = (nb - 1 - i) == 0
        uext[0:HALO, :] = jnp.where(first_tile, 0.0, uh_ref[...])
        uext[HALO:HALO + TM, :] = u_ref[...]
        _fill_shifted(uext, ush)

        for rb in range(TM // SUB):
            rs_ = slice(rb * SUB, (rb + 1) * SUB)
            xh, rs = _layernorm_stats(uc_ref[rs_, :])
            ln = xh * g_ref[...] + b_ref[...]
            dln = dcb_ref[rs_, :] * _dsilu(ln, _sig(ln))
            acc_ref[1:2, :] += _rowsum(dln * xh)
            acc_ref[2:3, :] += _rowsum(dln)
            dxh = dln * g_ref[...]
            duc = rs * (dxh - jnp.mean(dxh, axis=-1, keepdims=True)
                        - xh * jnp.mean(dxh * xh, axis=-1, keepdims=True))
            dext[rs_, :] = duc
            acc_ref[0:1, :] += _rowsum(duc)
        _fill_shifted(dext, dsh)

        for j in range(CONV_K):
            part = jnp.zeros((SUB, D), F32)
            for rb in range(TM // SUB):
                s0 = HALO - (CONV_K - 1) + j + rb * SUB
                part = part + dext[rb * SUB:(rb + 1) * SUB, :] * _window(uext, ush, s0, SUB)
            ddw_ref[j:j + 1, :] += _rowsum(part)

        for rb in range(TM // SUB):
            rs_ = slice(rb * SUB, (rb + 1) * SUB)
            du = jnp.zeros((SUB, D), F32)
            for j in range(CONV_K):
                s0 = rb * SUB + (CONV_K - 1) - j
                du = du + dw_ref[j:j + 1, :] * _window(dext, dsh, s0, SUB)
            cg = cg_ref[rs_, :]
            sg = _sig(cg)
            dcv = du * sg
            dcg = du * cv_ref[rs_, :] * (sg * (1.0 - sg))
            dp_ref[rs_, 0:D] = dcv.astype(BF16)
            dp_ref[rs_, D:2 * D] = dcg.astype(BF16)
            bsum_ref[:, 0:D] += _rowsum(dcv)
            bsum_ref[:, D:2 * D] += _rowsum(dcg)

        dext[TM:TM + HALO, :] = dext[0:HALO, :]

        @pl.when(i == nb - 1)
        def _():
            finish()

    rev = BS((TM, D), lambda i: (nb - 1 - i, 0))
    vec = BS((1, D), lambda i: (0, 0))
    return pl.pallas_call(
        body, name="conv_bwd", grid=(nb,),
        out_shape=(SDS((S, 2 * D), BF16), SDS((1, 2 * D), F32), SDS((32, D), F32), SDS((8, D), F32),
                   SDS((3,) + part.shape[1:], part.dtype)),
        in_specs=[rev, rev, rev, BS((HALO, D), lambda i: (jnp.maximum((nb - 1 - i) * hb - 1, 0), 0)),
                  BS((TM, D), lambda i: (nb - 1 - i, 4)), BS((TM, D), lambda i: (nb - 1 - i, 5)),
                  BS((CONV_K, D), lambda i: (0, 0)), vec, vec, BS(memory_space=pl.ANY)],
        out_specs=(BS((TM, 2 * D), lambda i: (nb - 1 - i, 0)), BS((1, 2 * D), lambda i: (0, 0)),
                   BS((32, D), lambda i: (0, 0)), BS((8, D), lambda i: (0, 0)), BS(memory_space=pl.ANY)),
        scratch_shapes=[pltpu.VMEM((EXT, D), F32), pltpu.VMEM((EXT, D), F32),
                        pltpu.VMEM((7, HALO + TM, D), F32), pltpu.VMEM((7, HALO + TM, D), F32)] + _exchange_sems(),
        compiler_params=_params("arbitrary"),
    )(dcb, uc, u, u, p, p, dw, ln_g, ln_b, part)


def _in_bwd_call(dp_hg, dp_cv, dp_gt, x, dx2, mod, pre_tm, wg, part, full_a, full_b):
    S = x.shape[0]
    tm = TM

    def body(hg_ref, cv_ref, gt_ref, x_ref, dx2_ref, mod_ref, g_ref, w_hbm, part_ref, fa_in, fb_in,
             gx_ref, acc_ref, recv_ref, fa_out, fb_out, w_vmem, sem, send_sems, recv_sems, sa, ra, sb, rb):
        start, finish = _chip_exchange(part_ref, recv_ref, send_sems, recv_sems)
        start_a, finish_a = _join_exchange(fa_in, fa_out, sa, ra)
        start_b, finish_b = _join_exchange(fb_in, fb_out, sb, rb)

        @pl.when(pl.program_id(0) == 0)
        def _():
            start_a()
            start_b()
            start()
            _load_rows(w_hbm, w_vmem, sem, O_IN).wait()
            acc_ref[...] = jnp.zeros_like(acc_ref)

        dh = jnp.zeros((tm, D), F32)
        for k in range(IN_COLS // D):
            src, kk = ((hg_ref, k), (cv_ref, k - 4), (gt_ref, k - 6))[0 if k < 4 else (1 if k < 6 else 2)]
            dh = dh + _mm(src[:, kk * D:(kk + 1) * D], w_vmem[k // 2, (k % 2) * D:(k % 2 + 1) * D, :], NT)
        xv = x_ref[...]
        r = lax.rsqrt(jnp.mean(xv * xv, axis=-1, keepdims=True) + EPS)
        xn = xv * r
        yv = xn * g_ref[...]
        acc_ref[0:1, :] += _rowsum(dh)
        acc_ref[1:2, :] += _rowsum(dh * yv)
        dyv = dh * (1.0 + mod_ref[:, D:2 * D])
        acc_ref[2:3, :] += _rowsum(dyv * xn)
        dxn = dyv * g_ref[...]
        gx_ref[...] = dx2_ref[...] + r * (dxn - xn * jnp.mean(dxn * xn, axis=-1, keepdims=True))

        @pl.when(pl.program_id(0) == S // tm - 1)
        def _():
            finish_a()
            finish_b()
            finish()

    tile = BS((tm, D), lambda i: (i, 0))
    hbm = BS(memory_space=pl.ANY)
    return pl.pallas_call(
        body, name="in_bwd", grid=(S // tm,),
        out_shape=(SDS((S, D), F32), SDS((8, D), F32), SDS((3,) + part.shape[1:], part.dtype),
                   SDS(full_a.shape, full_a.dtype), SDS(full_b.shape, full_b.dtype)),
        in_specs=[BS((tm, 4 * D), lambda i: (i, 0)), BS((tm, 2 * D), lambda i: (i, 0)),
                  BS((tm, 2 * D), lambda i: (i, 0)), tile, tile, BS((1, 6 * D), lambda i: (0, 0)),
                  BS((1, D), lambda i: (0, 0)), hbm, hbm, hbm, hbm],
        out_specs=(tile, BS((8, D), lambda i: (0, 0)), hbm, hbm, hbm),
        scratch_shapes=[pltpu.VMEM((N_CHIPS, R_IN, D), BF16), pltpu.SemaphoreType.DMA] + _exchange_sems()
        + _join_sems() + _join_sems(),
        input_output_aliases={9: 3, 10: 4},
        compiler_params=_params("arbitrary"),
    )(dp_hg, dp_cv, dp_gt, x, dx2, mod, pre_tm, wg, part, full_a, full_b)


def _wgrad_call(gp, a, b, name, bm, place, rows):
    S, M = a.shape
    N = b.shape[1]
    bk = min(S, 1024)
    nk = S // bk

    def body(a_ref, b_ref, *rest):
        o_ref, acc = rest[-2], rest[-1]
        k = pl.program_id(2)

        @pl.when(k == 0)
        def _():
            acc[...] = jnp.zeros_like(acc)

        acc[...] += _mm(a_ref[...], b_ref[...], TN)

        @pl.when(k == nk - 1)
        def _():
            o_ref[...] = acc[...].astype(BF16)

    in_specs = [BS((bk, bm), lambda i, j, k: (k, i)), BS((bk, D), lambda i, j, k: (k, j))]
    args = [a, b]
    if gp is not None:
        in_specs.append(BS(memory_space=pl.ANY))
        args.append(gp)
    return pl.pallas_call(
        body, name=name, grid=(M // bm, N // D, nk),
        out_shape=SDS((N_CHIPS, rows, D), BF16),
        in_specs=in_specs,
        out_specs=BS((None, bm, D), lambda i, j, k: (*place(i, j), 0)),
        scratch_shapes=[pltpu.VMEM((bm, D), F32)],
        input_output_aliases={} if gp is None else {2: 0},
        compiler_params=_params("parallel", "parallel", "arbitrary"),
    )(*args)


def _wgrad_rows_call(gp, a, b, name, blk):
    S = a.shape[0]
    bk = min(S, 1024)
    nk = S // bk

    def body(a_ref, b_ref, *rest):
        o_ref, acc = rest[-2], rest[-1]
        k = pl.program_id(0)

        @pl.when(k == 0)
        def _():
            acc[...] = jnp.zeros_like(acc)

        acc[...] += _mm(a_ref[...], b_ref[...], TN)

        @pl.when(k == nk - 1)
        def _():
            for c in range(N_CHIPS):
                o_ref[c] = acc[c * R_BR:(c + 1) * R_BR, :].astype(BF16)

    in_specs = [BS((bk, D), lambda k: (k, 0)), BS((bk, D), lambda k: (k, 0))]
    args = [a, b]
    if gp is not None:
        in_specs.append(BS(memory_space=pl.ANY))
        args.append(gp)
    return pl.pallas_call(
        body, name=name, grid=(nk,),
        out_shape=SDS((N_CHIPS, 3 * R_BR, D), BF16),
        in_specs=in_specs,
        out_specs=BS((N_CHIPS, R_BR, D), lambda k: (0, blk, 0)),
        scratch_shapes=[pltpu.VMEM((D, D), F32)],
        input_output_aliases={} if gp is None else {2: 0},
        compiler_params=_params("arbitrary"),
    )(*args)


def _outer_call(cact, dmod):
    n = dmod.shape[1]

    def body(a_ref, b_ref, o_ref):
        o_ref[...] = _mm(a_ref[...], b_ref[...], TN, HI)

    return pl.pallas_call(
        body, name="wgrad_ada", out_shape=SDS((D, n), F32),
        compiler_params=pltpu.CompilerParams(vmem_limit_bytes=VMEM_LIMIT),
    )(cact, dmod)


def _adamw_call(w, g, m, v, name):
    R, C = w.shape
    tr = R
    while tr * C > 512 * 1024 and tr % 16 == 0:
        tr //= 2
    c1 = 1.0 - ADAM_B1 ** ADAM_STEP
    c2 = 1.0 - ADAM_B2 ** ADAM_STEP

    def body(w_ref, g_ref, m_ref, v_ref, d_ref, m2_ref, v2_ref):
        g = g_ref[...]
        m2 = ADAM_B1 * m_ref[...] + (1.0 - ADAM_B1) * g
        v2 = ADAM_B2 * v_ref[...] + (1.0 - ADAM_B2) * (g * g)
        m2_ref[...] = m2
        v2_ref[...] = v2
        d_ref[...] = -ADAM_LR * ((m2 / c1) / (jnp.sqrt(v2 / c2) + ADAM_EPS) + ADAM_WD * w_ref[...])

    tile = BS((tr, C), lambda i: (i, 0))
    return pl.pallas_call(
        body, name=name, grid=(R // tr,), out_shape=(SDS((R, C), F32),) * 3,
        in_specs=[tile] * 4, out_specs=(tile,) * 3, compiler_params=_params("parallel"),
    )(w, g, m, v)


def _adamw_rows_call(ws, g, ms, vs, name):
    k = len(ws)
    r = ws[0].shape[0]
    c1 = 1.0 - ADAM_B1 ** ADAM_STEP
    c2 = 1.0 - ADAM_B2 ** ADAM_STEP

    def body(g_ref, *refs):
        ins, outs = refs[:3 * k], refs[3 * k:]
        for j in range(k):
            w_ref, m_ref, v_ref = ins[j], ins[k + j], ins[2 * k + j]
            d_ref, m2_ref, v2_ref = outs[j], outs[k + j], outs[2 * k + j]
            g = g_ref[j * r:(j + 1) * r, :]
            m2 = ADAM_B1 * m_ref[...] + (1.0 - ADAM_B1) * g
            v2 = ADAM_B2 * v_ref[...] + (1.0 - ADAM_B2) * (g * g)
            m2_ref[...] = m2
            v2_ref[...] = v2
            d_ref[...] = -ADAM_LR * ((m2 / c1) / (jnp.sqrt(v2 / c2) + ADAM_EPS) + ADAM_WD * w_ref[...])

    out = pl.pallas_call(
        body, name=name, out_shape=(SDS(ws[0].shape, F32),) * (3 * k),
        compiler_params=pltpu.CompilerParams(vmem_limit_bytes=VMEM_LIMIT),
    )(g, *ws, *ms, *vs)
    return out[:k], out[k:2 * k], out[2 * k:]


def _adamw_small_call(ws, ms, vs, row0, ssum, g_dw):
    n = len(ws)
    c1 = 1.0 - ADAM_B1 ** ADAM_STEP
    c2 = 1.0 - ADAM_B2 ** ADAM_STEP

    def adam(w, g, m, v):
        m2 = ADAM_B1 * m + (1.0 - ADAM_B1) * g
        v2 = ADAM_B2 * v + (1.0 - ADAM_B2) * (g * g)
        return -ADAM_LR * ((m2 / c1) / (jnp.sqrt(v2 / c2) + ADAM_EPS) + ADAM_WD * w), m2, v2

    def body(s_ref, gdw_ref, *refs):
        ins, outs = refs[:3 * n], refs[3 * n:]
        for j in range(n):
            w_ref, m_ref, v_ref = ins[j], ins[n + j], ins[2 * n + j]
            g_ref, d_ref, m2_ref, v2_ref = outs[j], outs[n + j], outs[2 * n + j], outs[3 * n + j]
            if j == n - 1:
                pieces = [(slice(None), slice(None), gdw_ref[...])]
            elif w_ref.shape[0] == 1:
                pieces = [(slice(None), slice(i * D, (i + 1) * D), s_ref[row0[j] + i:row0[j] + i + 1, :])
                          for i in range(w_ref.shape[1] // D)]
            else:
                pieces = [(slice(None), slice(None), s_ref[row0[j]:row0[j] + w_ref.shape[0], :])]
            for rs, cs, g in pieces:
                d, m2, v2 = adam(w_ref[rs, cs], g, m_ref[rs, cs], v_ref[rs, cs])
                g_ref[rs, cs] = g
                d_ref[rs, cs] = d
                m2_ref[rs, cs] = m2
                v2_ref[rs, cs] = v2

    out = pl.pallas_call(
        body, name="adamw_small", out_shape=tuple(SDS(w.shape, F32) for w in ws) * 4,
        compiler_params=pltpu.CompilerParams(vmem_limit_bytes=VMEM_LIMIT),
    )(ssum, g_dw, *ws, *ms, *vs)
    return [(out[j], out[n + j], out[2 * n + j], out[3 * n + j]) for j in range(n)]


def _adamw_halves_call(w, g, m, v, name):
    R, C = w.shape
    tr = R
    while tr * C > 512 * 1024 and tr % 16 == 0:
        tr //= 2
    c1 = 1.0 - ADAM_B1 ** ADAM_STEP
    c2 = 1.0 - ADAM_B2 ** ADAM_STEP

    def body(w_ref, g_ref, m_ref, v_ref, go_ref, d_ref, m2_ref, v2_ref):
        for k in range(2):
            cs = slice(k * D, (k + 1) * D)
            g = g_ref[k]
            go_ref[:, cs] = g
            m2 = ADAM_B1 * m_ref[:, cs] + (1.0 - ADAM_B1) * g
            v2 = ADAM_B2 * v_ref[:, cs] + (1.0 - ADAM_B2) * (g * g)
            m2_ref[:, cs] = m2
            v2_ref[:, cs] = v2
            d_ref[:, cs] = -ADAM_LR * ((m2 / c1) / (jnp.sqrt(v2 / c2) + ADAM_EPS) + ADAM_WD * w_ref[:, cs])

    tile = BS((tr, C), lambda i: (i, 0))
    return pl.pallas_call(
        body, name=name, grid=(R // tr,), out_shape=(SDS((R, C), F32),) * 4,
        in_specs=[tile, BS((2, tr, D), lambda i: (0, i, 0)), tile, tile], out_specs=(tile,) * 4,
        compiler_params=_params("parallel"),
    )(w, g, m, v)


def _rs_begin(g, c_idx, tag):
    n = g.shape[1]
    g = g.reshape(N_CHIPS, 2, n // 2, D)
    return _add_halves_call(g, _sibling_halves_call(g, tag), c_idx, tag)


def _local_step(x, mod, cact, target, wg, pack, small, c_idx, chip_idx):
    p, h1, wg = _fwd_in_call(x, mod, small["pre_tm"], wg, small["b_in"], pack, small["order"])
    o, oa, st, wg = _hgrn_fwd_call(p, small["logits"], small["hg_norm"], wg, pack)
    u, uc, cb, wg = _conv_fwd_call(p, small["conv_dw"], small["conv_db"], small["ln_g"], small["ln_b"], wg, pack)
    ya, yb, mg, y, x2, h2 = _merge_fwd_call(oa, cb, p, x, mod, small["post_tm"], small["pre_cm"], wg)
    z, da, dy2, dx2, acc_f = _ffn_call(h2, x2, target, mod, small["post_cm"], small["pre_cm"], wg)

    g_ff = _wgrad_call(None, h2, da, "wgrad_ff1", D, lambda i, j: (j, 0), 2 * R_FF)
    g_ff = _wgrad_call(g_ff, z, dy2, "wgrad_ff2", D, lambda i, j: (i, 1), 2 * R_FF)
    g_ff = g_ff.reshape(N_CHIPS, 2, R_FF, D)
    dy, dya, dyb, doa, dcb, dp_gt, acc_m, bs_gt, hr_ff = _merge_bwd_call(dx2, y, ya, yb, p, mod, small["post_tm"],
                                                                        wg, g_ff)
    part_ff = _add_halves_call(g_ff, hr_ff, c_idx, "ff")

    g_br = _wgrad_rows_call(None, oa, dya, "wgrad_br_a", 0)
    g_br = _wgrad_rows_call(g_br, cb, dyb, "wgrad_br_b", 1)
    g_br = _wgrad_rows_call(g_br, mg, dy, "wgrad_out", 2)
    g_br = g_br.reshape(N_CHIPS, 2, 3 * R_BR // 2, D)
    dp_hg, bs_hg, dlg, dgn, recv_ff, hr_br = _hgrn_bwd_call(p, o, doa, st, small["logits"], small["hg_norm"],
                                                            part_ff, g_br)
    part_br = _add_halves_call(g_br, hr_br, c_idx, "br")
    dp_cv, bs_cv, ddw, acc_c, recv_br = _conv_bwd_call(dcb, uc, u, p, small["conv_dw"], small["ln_g"], small["ln_b"],
                                                        part_br)

    g_in = _wgrad_call(None, h1, dp_hg, "wgrad_in_hg", D, lambda i, j: (j // 2, j % 2), R_IN)
    g_in = _wgrad_call(g_in, h1, dp_cv, "wgrad_in_cv", D, lambda i, j: (2, j), R_IN)
    g_in = _wgrad_call(g_in, h1, dp_gt, "wgrad_in_gt", D, lambda i, j: (3, j), R_IN)
    part_in = _rs_begin(g_in, c_idx, "in")
    chip_c = jnp.concatenate([chip_idx, c_idx])
    full_ff = _add_chips_call(part_ff, recv_ff, chip_c, "ff")
    full_br = _add_chips_call(part_br, recv_br, chip_c, "br")
    gx, acc_i, recv_in, full_ff, full_br = _in_bwd_call(dp_hg, dp_cv, dp_gt, x, dx2, mod, small["pre_tm"], wg,
                                                        part_in, full_ff, full_br)
    red_ff = full_ff.reshape(2 * R_FF, D)
    red_br = full_br.reshape(3 * R_BR, D)
    full_in = _add_chips_call(part_in, recv_in, chip_c, "in")

    zrow = jnp.zeros((1, D), F32)
    rows = [acc_i[0:1], acc_i[1:2], acc_m[0:1], acc_f[2:3], acc_f[3:4], acc_f[0:1],
            acc_i[2:3], acc_m[1:2], acc_f[4:5], acc_f[1:2],
            jnp.concatenate([bs_hg, bs_cv, bs_gt], axis=1).reshape(8, D),
            dlg, dgn, acc_c[0:1], acc_c[1:2], acc_c[2:3],
            ddw,
            cact, acc_f[5:6]] + [zrow] * 6
    full_in, sall, ssum = _join_gather_call(full_in, jnp.concatenate(rows, axis=0), "in")
    return gx, sall, ssum, full_in.reshape(R_IN, D), red_ff, red_br


def kernel(x, c, w_ada, b_ada, pre_norm_tm, post_norm_tm, pre_norm_cm, post_norm_cm, w_in, b_in, hg_lb_logits, hg_norm, conv_dw, conv_db, conv_ln_g, conv_ln_b, w_br_a, w_br_b, w_out, w_ff1, w_ff2, loss_target, m_w_ada, m_b_ada, m_pre_norm_tm, m_post_norm_tm, m_pre_norm_cm, m_post_norm_cm, m_w_in, m_b_in, m_hg_lb_logits, m_hg_norm, m_conv_dw, m_conv_db, m_conv_ln_g, m_conv_ln_b, m_w_br_a, m_w_br_b, m_w_out, m_w_ff1, m_w_ff2, v_w_ada, v_b_ada, v_pre_norm_tm, v_post_norm_tm, v_pre_norm_cm, v_post_norm_cm, v_w_in, v_b_in, v_hg_lb_logits, v_hg_norm, v_conv_dw, v_conv_db, v_conv_ln_g, v_conv_ln_b, v_w_br_a, v_w_br_b, v_w_out, v_w_ff1, v_w_ff2):
    xi, yi, ci = lax.axis_index("x"), lax.axis_index("y"), lax.axis_index("c")
    chip = 2 * xi + yi
    c_idx = jnp.reshape(ci, (1,)).astype(jnp.int32)
    chip_idx = jnp.reshape(chip, (1,)).astype(jnp.int32)

    pack = _pack_call(w_in[0], w_ff1[0], w_ff2[0], w_br_a[0], w_br_b[0], w_out[0])
    wg = lax.empty((N_CHIPS, PACK_W, D), BF16)
    wa = 6 * D // N_CHIPS
    me = 4 * xi + 2 * yi + ci
    dw_blk = jnp.concatenate([conv_dw[0].reshape(-1), jnp.zeros((8 * D - CONV_K * 256,), F32)]).reshape(8, D)
    dw_all, ca_all, mod_all = _prologue_call(
        dw_blk, jnp.broadcast_to(c, (8, D)), w_ada[0],
        lax.dynamic_slice_in_dim(b_ada, chip * wa, wa, axis=1))
    order = jnp.stack([chip, 2 * (1 - xi) + yi, 2 * xi + (1 - yi), 2 * (1 - xi) + (1 - yi)]).astype(jnp.int32)
    dw_all = dw_all.reshape(N_CHIPS, 2, 8 * D)[:, 0, :CONV_K * 256].reshape(N_CHIPS, CONV_K, 256)
    dw_full = dw_all.transpose(1, 0, 2).reshape(CONV_K, D)
    cact = lax.dynamic_slice_in_dim(ca_all, me * 8, 1, axis=0)
    mod_mine = lax.dynamic_index_in_dim(mod_all.reshape(N_CHIPS, 2, N_DEV, wa)[:, 0], me, axis=1,
                                        keepdims=False)
    mod = mod_mine.reshape(1, 6 * D)

    small = dict(pre_tm=pre_norm_tm, post_tm=post_norm_tm, pre_cm=pre_norm_cm, post_cm=post_norm_cm,
                 b_in=b_in, logits=hg_lb_logits, hg_norm=hg_norm, conv_dw=dw_full, conv_db=conv_db,
                 ln_g=conv_ln_g, ln_b=conv_ln_b, order=order)

    gx, sall, ssum, red_in, red_ff, red_br = _local_step(x[0], mod, cact, loss_target[0], wg, pack, small, c_idx,
                                                    chip_idx)

    shapes = {"in": w_in.shape, "br_a": w_br_a.shape, "br_b": w_br_b.shape, "out": w_out.shape,
              "ff1": w_ff1.shape, "ff2": w_ff2.shape}
    offs = {"in": (red_in, 0, R_IN), "ff1": (red_ff, 0, R_FF), "ff2": (red_ff, R_FF, 2 * R_FF),
            "br_a": (red_br, 0, R_BR), "br_b": (red_br, R_BR, 2 * R_BR), "out": (red_br, 2 * R_BR, 3 * R_BR)}
    wmv = {"in": (w_in, m_w_in, v_w_in), "br_a": (w_br_a, m_w_br_a, v_w_br_a), "br_b": (w_br_b, m_w_br_b, v_w_br_b),
           "out": (w_out, m_w_out, v_w_out), "ff1": (w_ff1, m_w_ff1, v_w_ff1), "ff2": (w_ff2, m_w_ff2, v_w_ff2)}
    res = {}
    for n in ("in", "ff1", "ff2"):
        shp = shapes[n]
        g2d = offs[n][0][offs[n][1]:offs[n][2]]
        w_, m_, v_ = (a[0] for a in wmv[n])
        if n == "in":
            g2d, d_, m2_, v2_ = _adamw_halves_call(w_, g2d.reshape(2, D, D), m_, v_, "adamw_in")
        else:
            g2d = g2d.reshape(shp[1], shp[2])
            d_, m2_, v2_ = _adamw_call(w_, g2d, m_, v_, "adamw_" + n)
        res[n] = tuple(a.reshape(shp) for a in (g2d, d_, m2_, v2_))
    trio = ("br_a", "br_b", "out")
    d3, m3, v3 = _adamw_rows_call([wmv[n][0][0] for n in trio], red_br, [wmv[n][1][0] for n in trio],
                                  [wmv[n][2][0] for n in trio], "adamw_br")
    for j, n in enumerate(trio):
        res[n] = tuple(a.reshape(shapes[n]) for a in (red_br[j * R_BR:(j + 1) * R_BR], d3[j], m3[j], v3[j]))

    sall = sall.reshape(N_DEV, SMALL_ROWS, D)
    loss = jnp.sum(ssum[57])
    dmod_all = sall[:, 0:6, :].reshape(N_DEV, 6 * D)
    g_ada = _outer_call(sall[:, 56, :], lax.dynamic_slice_in_dim(dmod_all, chip * wa, wa, axis=1))
    g_dw = lax.dynamic_slice_in_dim(ssum[24:24 + CONV_K], chip * 256, 256, axis=1)
    d_, m2_, v2_ = _adamw_call(w_ada[0], g_ada, m_w_ada[0], v_w_ada[0], "adamw_ada")
    res["ada"] = tuple(a.reshape(w_ada.shape) for a in (g_ada, d_, m2_, v2_))

    names = ["b_ada", "pre_tm", "post_tm", "pre_cm", "post_cm", "b_in", "logits", "hg_norm", "conv_db", "ln_g", "ln_b",
             "conv_dw"]
    row0 = [0, 6, 7, 8, 9, 10, 18, 20, 21, 22, 23, None]
    sres = _adamw_small_call(
        [b_ada, pre_norm_tm, post_norm_tm, pre_norm_cm, post_norm_cm, b_in, hg_lb_logits, hg_norm, conv_db,
         conv_ln_g, conv_ln_b, conv_dw[0]],
        [m_b_ada, m_pre_norm_tm, m_post_norm_tm, m_pre_norm_cm, m_post_norm_cm, m_b_in, m_hg_lb_logits, m_hg_norm,
         m_conv_db, m_conv_ln_g, m_conv_ln_b, m_conv_dw[0]],
        [v_b_ada, v_pre_norm_tm, v_post_norm_tm, v_pre_norm_cm, v_post_norm_cm, v_b_in, v_hg_lb_logits, v_hg_norm,
         v_conv_db, v_conv_ln_g, v_conv_ln_b, v_conv_dw[0]], row0, ssum, g_dw)
    for nm, r4 in zip(names, sres):
        res[nm] = tuple(a.reshape(conv_dw.shape) for a in r4) if nm == "conv_dw" else r4

    order = ["ada", "b_ada", "pre_tm", "post_tm", "pre_cm", "post_cm", "in", "b_in", "logits", "hg_norm", "conv_dw",
             "conv_db", "ln_g", "ln_b", "br_a", "br_b", "out", "ff1", "ff2"]
    outs = [loss, gx.reshape(x.shape)]
    for kind in range(4):
        outs.extend(res[n][kind] for n in order)
    return tuple(outs)
```
